```python
import math
import jax, jax.numpy as jnp
from jax import lax
import numpy as np

D_MODEL = 1024
BATCH = 8
SEQ = 8192
DEPTH = 2

N_A_LAYERS = DEPTH // 2
N_B_LAYERS = DEPTH - N_A_LAYERS

GDN_HEADS = 8
GDN_HEAD_DIM = 128
GDN_WIDTH = GDN_HEADS * GDN_HEAD_DIM
CONV_WIDTH = 4
CHUNK = 64
GDN_IN_COLS = 4 * GDN_WIDTH + 2 * GDN_HEADS

SB_HEADS = 8
SB_HEAD_DIM = 128
SB_WIDTH = SB_HEADS * SB_HEAD_DIM
Q_BLOCK = 128

D_FF = 4 * D_MODEL

EPS = 1e-6

kernel_name = "yoco_gdn_stickbreaking_hybrid"


def rms_norm(x, gain):
    xf = x.astype(jnp.float32)
    y = xf * lax.rsqrt(jnp.mean(xf * xf, axis=-1, keepdims=True) + EPS)
    return (y * gain.astype(jnp.float32)).astype(x.dtype)


def l2_norm(x):
    xf = x.astype(jnp.float32)
    return xf * lax.rsqrt(jnp.sum(xf * xf, axis=-1, keepdims=True) + EPS)


def causal_dwconv(x, w):
    k_w = w.shape[0]
    t_len = x.shape[1]
    xp = jnp.pad(x, ((0, 0), (k_w - 1, 0), (0, 0)))
    return sum(xp[:, i:i + t_len, :] * w[i] for i in range(k_w))


def gated_delta_rule_chunked(q, k, v, beta, g):
    b, t_len, h, dk = q.shape
    dv = v.shape[-1]
    n = t_len // CHUNK

    def chunks(t):
        t = jnp.moveaxis(t, 2, 1)
        return t.reshape((b, h, n, CHUNK) + t.shape[3:])

    q, k, v, beta, g = (chunks(t) for t in (q, k, v, beta, g))
    gc = jnp.cumsum(g, axis=-1)
    idx = jnp.arange(CHUNK)
    incl = idx[:, None] >= idx[None, :]
    strict = idx[:, None] > idx[None, :]
    diff = gc[..., :, None] - gc[..., None, :]
    decay = jnp.where(incl, jnp.exp(jnp.where(incl, diff, 0.0)), 0.0)

    kb = k * beta[..., None]
    lower = jnp.where(strict, jnp.einsum('bhnid,bhnjd->bhnij', kb, k) * decay, 0.0)
    eye = jnp.eye(CHUNK, dtype=jnp.float32)
    t_mat = lax.linalg.triangular_solve(eye + lower, jnp.broadcast_to(eye, lower.shape),
                                        left_side=True, lower=True)
    w = t_mat @ (kb * jnp.exp(gc)[..., None])
    u = t_mat @ (v * beta[..., None])
    attn = jnp.einsum('bhnid,bhnjd->bhnij', q, k) * decay
    qg = q * jnp.exp(gc)[..., None]
    kg = k * jnp.exp(gc[..., -1:] - gc)[..., None]
    g_last = jnp.exp(gc[..., -1])

    xs = tuple(jnp.moveaxis(t, 2, 0) for t in (qg, kg, w, u, attn, g_last))

    def step(state, inp):
        qg_c, kg_c, w_c, u_c, attn_c, gl_c = inp
        v_new = u_c - w_c @ state
        o_c = qg_c @ state + attn_c @ v_new
        state = state * gl_c[..., None, None] + jnp.einsum('bhck,bhcv->bhkv', kg_c, v_new)
        return state, o_c

    s0 = jnp.zeros((b, h, dk, dv), jnp.float32)
    _, o = lax.scan(step, s0, xs)
    o = jnp.moveaxis(o, 0, 2).reshape(b, h, t_len, dv)
    return jnp.moveaxis(o, 1, 2)


def gated_deltanet(h, w_in, conv_w, a_log, dt_bias, out_gain, w_out):
    b, t_len, _ = h.shape
    proj = h @ w_in
    qkv, gate, b_raw, a_raw = jnp.split(
        proj, [3 * GDN_WIDTH, 4 * GDN_WIDTH, 4 * GDN_WIDTH + GDN_HEADS], axis=-1)
    qkv = jax.nn.silu(causal_dwconv(qkv, conv_w))
    q, k, v = jnp.split(qkv, 3, axis=-1)

    def heads(t):
        return t.reshape(b, t_len, GDN_HEADS, GDN_HEAD_DIM).astype(jnp.float32)

    q = l2_norm(heads(q)) * (GDN_HEAD_DIM ** -0.5)
    k = l2_norm(heads(k))
    v = heads(v)
    beta = jax.nn.sigmoid(b_raw.astype(jnp.float32))
    g = -jnp.exp(a_log.astype(jnp.float32)) * jax.nn.softplus(
        a_raw.astype(jnp.float32) + dt_bias.astype(jnp.float32))
    o = gated_delta_rule_chunked(q, k, v, beta, g)
    o = rms_norm(o, out_gain) * jax.nn.silu(heads(gate))
    return o.reshape(b, t_len, GDN_WIDTH).astype(h.dtype) @ w_out


def stick_breaking_attention(q, k, v):
    b, h, t_len, d = q.shape
    nb = t_len // Q_BLOCK
    qb = jnp.moveaxis(q.reshape(b, h, nb, Q_BLOCK, d), 2, 0)
    key_pos = jnp.arange(t_len)
    scale = d ** -0.5

    def block(args):
        q_blk, i = args
        z = jnp.einsum('bhqd,bhkd->bhqk', q_blk, k).astype(jnp.float32) * scale
        q_pos = i * Q_BLOCK + jnp.arange(Q_BLOCK)
        before = key_pos[None, :] < q_pos[:, None]
        log_beta = jax.nn.log_sigmoid(z)
        log_1m = jnp.where(before, jax.nn.log_sigmoid(-z), 0.0)
        tail = lax.cumsum(log_1m, axis=3, reverse=True) - log_1m
        a = jnp.where(before, jnp.exp(log_beta + tail), 0.0)
        return jnp.einsum('bhqk,bhkd->bhqd', a.astype(v.dtype), v)

    o = lax.map(block, (qb, jnp.arange(nb)))
    return jnp.moveaxis(o, 0, 2).reshape(b, h, t_len, d)


def shared_kv(x, kv_gain, w_kv):
    b, t_len, _ = x.shape
    kv = rms_norm(x, kv_gain) @ w_kv
    k, v = jnp.split(kv, 2, axis=-1)
    k = k.reshape(b, t_len, SB_HEADS, SB_HEAD_DIM).transpose(0, 2, 1, 3)
    v = v.reshape(b, t_len, SB_HEADS, SB_HEAD_DIM).transpose(0, 2, 1, 3)
    return k, v


def stick_breaking_mixer(h, w_q, w_o, k_sh, v_sh):
    b, t_len, _ = h.shape
    q = (h @ w_q).reshape(b, t_len, SB_HEADS, SB_HEAD_DIM).transpose(0, 2, 1, 3)
    o = stick_breaking_attention(q, k_sh, v_sh)
    return o.transpose(0, 2, 1, 3).reshape(b, t_len, SB_WIDTH) @ w_o


def squared_relu_mlp(h, w_up, w_down):
    return jnp.square(jax.nn.relu(h @ w_up)) @ w_down


def _fwd_setup_inputs(seed: int = 0) -> dict:
    key = jax.random.key(seed)
    ks = jax.random.split(key, 20)
    f32 = jnp.float32

    def nrm(k, shape, fan_in):
        return jax.random.normal(k, shape, f32) * (fan_in ** -0.5)

    def gain(k, shape):
        return 1.0 + 0.05 * jax.random.normal(k, shape, f32)

    x = jax.random.normal(ks[0], (BATCH, SEQ, D_MODEL), f32)
    dt = jnp.exp(jax.random.uniform(ks[10], (N_A_LAYERS, GDN_HEADS), f32,
                                    minval=math.log(1e-3), maxval=math.log(1e-1)))
    dt_bias = dt + jnp.log(-jnp.expm1(-dt))
    a_log = jnp.log(jax.random.uniform(ks[11], (N_A_LAYERS, GDN_HEADS), f32,
                                       minval=1.0, maxval=16.0))
    return {
        "x": x,
        "mix_pre_gain": gain(ks[1], (DEPTH, D_MODEL)),
        "mix_post_gain": gain(ks[2], (DEPTH, D_MODEL)),
        "mlp_pre_gain": gain(ks[3], (DEPTH, D_MODEL)),
        "mlp_post_gain": gain(ks[4], (DEPTH, D_MODEL)),
        "mlp_w_up": nrm(ks[5], (DEPTH, D_MODEL, D_FF), D_MODEL),
        "mlp_w_down": nrm(ks[6], (DEPTH, D_FF, D_MODEL), D_FF),
        "gdn_w_in": nrm(ks[7], (N_A_LAYERS, D_MODEL, GDN_IN_COLS), D_MODEL),
        "gdn_conv_w": nrm(ks[8], (N_A_LAYERS, CONV_WIDTH, 3 * GDN_WIDTH), CONV_WIDTH),
        "gdn_a_log": a_log,
        "gdn_dt_bias": dt_bias,
        "gdn_out_gain": gain(ks[12], (N_A_LAYERS, GDN_HEAD_DIM)),
        "gdn_w_out": nrm(ks[13], (N_A_LAYERS, GDN_WIDTH, D_MODEL), GDN_WIDTH),
        "kv_gain": gain(ks[14], (D_MODEL,)),
        "w_kv": nrm(ks[15], (D_MODEL, 2 * SB_WIDTH), D_MODEL),
        "sb_w_q": nrm(ks[16], (N_B_LAYERS, D_MODEL, SB_WIDTH), D_MODEL),
        "sb_w_o": nrm(ks[17], (N_B_LAYERS, SB_WIDTH, D_MODEL), SB_WIDTH),
    }


def _fwd_reference(x, mix_pre_gain, mix_post_gain, mlp_pre_gain, mlp_post_gain, mlp_w_up, mlp_w_down,
              gdn_w_in, gdn_conv_w, gdn_a_log, gdn_dt_bias, gdn_out_gain, gdn_w_out,
              kv_gain, w_kv, sb_w_q, sb_w_o):
    k_sh = None
    v_sh = None
    for layer in range(DEPTH):
        h = rms_norm(x, mix_pre_gain[layer])
        if layer < N_A_LAYERS:
            a = layer
            mix = gated_deltanet(h, gdn_w_in[a], gdn_conv_w[a], gdn_a_log[a], gdn_dt_bias[a],
                                 gdn_out_gain[a], gdn_w_out[a])
        else:
            if layer == N_A_LAYERS:
                k_sh, v_sh = shared_kv(x, kv_gain, w_kv)
            bl = layer - N_A_LAYERS
            mix = stick_breaking_mixer(h, sb_w_q[bl], sb_w_o[bl], k_sh, v_sh)
        x = x + rms_norm(mix, mix_post_gain[layer])
        h = rms_norm(x, mlp_pre_gain[layer])
        x = x + rms_norm(squared_relu_mlp(h, mlp_w_up[layer], mlp_w_down[layer]), mlp_post_gain[layer])
    return x


import jax as _jax
import jax.numpy as _jnp

TWIN_FORMAT = 'train_step'
FWD_PARAMS = ['x', 'mix_pre_gain', 'mix_post_gain', 'mlp_pre_gain', 'mlp_post_gain', 'mlp_w_up', 'mlp_w_down', 'gdn_w_in', 'gdn_conv_w', 'gdn_a_log', 'gdn_dt_bias', 'gdn_out_gain', 'gdn_w_out', 'kv_gain', 'w_kv', 'sb_w_q', 'sb_w_o']
TWIN_WEIGHTS = ['mix_pre_gain', 'mix_post_gain', 'mlp_pre_gain', 'mlp_post_gain', 'mlp_w_up', 'mlp_w_down', 'gdn_w_in', 'gdn_conv_w', 'gdn_a_log', 'gdn_dt_bias', 'gdn_out_gain', 'gdn_w_out', 'kv_gain', 'w_kv', 'sb_w_q', 'sb_w_o']
TWIN_DIFF_INPUT = 'x'
TWIN_INPUTS = ['x', 'mix_pre_gain', 'mix_post_gain', 'mlp_pre_gain', 'mlp_post_gain', 'mlp_w_up', 'mlp_w_down', 'gdn_w_in', 'gdn_conv_w', 'gdn_a_log', 'gdn_dt_bias', 'gdn_out_gain', 'gdn_w_out', 'kv_gain', 'w_kv', 'sb_w_q', 'sb_w_o', 'loss_target', 'm_mix_pre_gain', 'm_mix_post_gain', 'm_mlp_pre_gain', 'm_mlp_post_gain', 'm_mlp_w_up', 'm_mlp_w_down', 'm_gdn_w_in', 'm_gdn_conv_w', 'm_gdn_a_log', 'm_gdn_dt_bias', 'm_gdn_out_gain', 'm_gdn_w_out', 'm_kv_gain', 'm_w_kv', 'm_sb_w_q', 'm_sb_w_o', 'v_mix_pre_gain', 'v_mix_post_gain', 'v_mlp_pre_gain', 'v_mlp_post_gain', 'v_mlp_w_up', 'v_mlp_w_down', 'v_gdn_w_in', 'v_gdn_conv_w', 'v_gdn_a_log', 'v_gdn_dt_bias', 'v_gdn_out_gain', 'v_gdn_w_out', 'v_kv_gain', 'v_w_kv', 'v_sb_w_q', 'v_sb_w_o']
TWIN_OUTPUTS = ['loss', 'grad_x', 'grad_mix_pre_gain', 'grad_mix_post_gain', 'grad_mlp_pre_gain', 'grad_mlp_post_gain', 'grad_mlp_w_up', 'grad_mlp_w_down', 'grad_gdn_w_in', 'grad_gdn_conv_w', 'grad_gdn_a_log', 'grad_gdn_dt_bias', 'grad_gdn_out_gain', 'grad_gdn_w_out', 'grad_kv_gain', 'grad_w_kv', 'grad_sb_w_q', 'grad_sb_w_o', 'delta_mix_pre_gain', 'delta_mix_post_gain', 'delta_mlp_pre_gain', 'delta_mlp_post_gain', 'delta_mlp_w_up', 'delta_mlp_w_down', 'delta_gdn_w_in', 'delta_gdn_conv_w', 'delta_gdn_a_log', 'delta_gdn_dt_bias', 'delta_gdn_out_gain', 'delta_gdn_w_out', 'delta_kv_gain', 'delta_w_kv', 'delta_sb_w_q', 'delta_sb_w_o', 'new_m_mix_pre_gain', 'new_m_mix_post_gain', 'new_m_mlp_pre_gain', 'new_m_mlp_post_gain', 'new_m_mlp_w_up', 'new_m_mlp_w_down', 'new_m_gdn_w_in', 'new_m_gdn_conv_w', 'new_m_gdn_a_log', 'new_m_gdn_dt_bias', 'new_m_gdn_out_gain', 'new_m_gdn_w_out', 'new_m_kv_gain', 'new_m_w_kv', 'new_m_sb_w_q', 'new_m_sb_w_o', 'new_v_mix_pre_gain', 'new_v_mix_post_gain', 'new_v_mlp_pre_gain', 'new_v_mlp_post_gain', 'new_v_mlp_w_up', 'new_v_mlp_w_down', 'new_v_gdn_w_in', 'new_v_gdn_conv_w', 'new_v_gdn_a_log', 'new_v_gdn_dt_bias', 'new_v_gdn_out_gain', 'new_v_gdn_w_out', 'new_v_kv_gain', 'new_v_w_kv', 'new_v_sb_w_q', 'new_v_sb_w_o']
TWIN_LEAF_KINDS = {'loss': 'loss', 'grad_x': 'grad_x', 'grad_mix_pre_gain': 'grad_w', 'grad_mix_post_gain': 'grad_w', 'grad_mlp_pre_gain': 'grad_w', 'grad_mlp_post_gain': 'grad_w', 'grad_mlp_w_up': 'grad_w', 'grad_mlp_w_down': 'grad_w', 'grad_gdn_w_in': 'grad_w', 'grad_gdn_conv_w': 'grad_w', 'grad_gdn_a_log': 'grad_w', 'grad_gdn_dt_bias': 'grad_w', 'grad_gdn_out_gain': 'grad_w', 'grad_gdn_w_out': 'grad_w', 'grad_kv_gain': 'grad_w', 'grad_w_kv': 'grad_w', 'grad_sb_w_q': 'grad_w', 'grad_sb_w_o': 'grad_w', 'delta_mix_pre_gain': 'delta_w', 'delta_mix_post_gain': 'delta_w', 'delta_mlp_pre_gain': 'delta_w', 'delta_mlp_post_gain': 'delta_w', 'delta_mlp_w_up': 'delta_w', 'delta_mlp_w_down': 'delta_w', 'delta_gdn_w_in': 'delta_w', 'delta_gdn_conv_w': 'delta_w', 'delta_gdn_a_log': 'delta_w', 'delta_gdn_dt_bias': 'delta_w', 'delta_gdn_out_gain': 'delta_w', 'delta_gdn_w_out': 'delta_w', 'delta_kv_gain': 'delta_w', 'delta_w_kv': 'delta_w', 'delta_sb_w_q': 'delta_w', 'delta_sb_w_o': 'delta_w', 'new_m_mix_pre_gain': 'new_m', 'new_m_mix_post_gain': 'new_m', 'new_m_mlp_pre_gain': 'new_m', 'new_m_mlp_post_gain': 'new_m', 'new_m_mlp_w_up': 'new_m', 'new_m_mlp_w_down': 'new_m', 'new_m_gdn_w_in': 'new_m', 'new_m_gdn_conv_w': 'new_m', 'new_m_gdn_a_log': 'new_m', 'new_m_gdn_dt_bias': 'new_m', 'new_m_gdn_out_gain': 'new_m', 'new_m_gdn_w_out': 'new_m', 'new_m_kv_gain': 'new_m', 'new_m_w_kv': 'new_m', 'new_m_sb_w_q': 'new_m', 'new_m_sb_w_o': 'new_m', 'new_v_mix_pre_gain': 'new_v', 'new_v_mix_post_gain': 'new_v', 'new_v_mlp_pre_gain': 'new_v', 'new_v_mlp_post_gain': 'new_v', 'new_v_mlp_w_up': 'new_v', 'new_v_mlp_w_down': 'new_v', 'new_v_gdn_w_in': 'new_v', 'new_v_gdn_conv_w': 'new_v', 'new_v_gdn_a_log': 'new_v', 'new_v_gdn_dt_bias': 'new_v', 'new_v_gdn_out_gain': 'new_v', 'new_v_gdn_w_out': 'new_v', 'new_v_kv_gain': 'new_v', 'new_v_w_kv': 'new_v', 'new_v_sb_w_q': 'new_v', 'new_v_sb_w_o': 'new_v'}


def _forward(args):
    return _fwd_reference(*[args[k] for k in FWD_PARAMS])


def _output_shape():
    def fwd():
        inp = _fwd_setup_inputs(0)
        return _fwd_reference(*[inp[k] for k in FWD_PARAMS])
    out = _jax.eval_shape(fwd)
    return out.shape, out.dtype

N_MICROBATCH = 1
ADAM_LR = 0.001
ADAM_B1 = 0.9
ADAM_B2 = 0.999
ADAM_EPS = 1e-08
ADAM_WD = 0.01
ADAM_STEP = 10
PER_EXAMPLE_BATCH_AXIS = {'x': 0, 'loss_target': 0}
SHARED_INPUTS = []
_WEIGHT_DTYPES = {'mix_pre_gain': _jnp.float32, 'mix_post_gain': _jnp.float32, 'mlp_pre_gain': _jnp.float32, 'mlp_post_gain': _jnp.float32, 'mlp_w_up': _jnp.float32, 'mlp_w_down': _jnp.float32, 'gdn_w_in': _jnp.float32, 'gdn_conv_w': _jnp.float32, 'gdn_a_log': _jnp.float32, 'gdn_dt_bias': _jnp.float32, 'gdn_out_gain': _jnp.float32, 'gdn_w_out': _jnp.float32, 'kv_gain': _jnp.float32, 'w_kv': _jnp.float32, 'sb_w_q': _jnp.float32, 'sb_w_o': _jnp.float32}
MOMENT_SCALE = {'mix_pre_gain': 1.482019e+00, 'mix_post_gain': 6.687450e+01, 'mlp_pre_gain': 7.623502e+00, 'mlp_post_gain': 7.036331e+01, 'mlp_w_up': 3.753865e+00, 'mlp_w_down': 2.162160e+01, 'gdn_w_in': 1.035416e+00, 'gdn_conv_w': 5.512853e+00, 'gdn_a_log': 1.534856e+01, 'gdn_dt_bias': 1.546932e+01, 'gdn_out_gain': 4.415183e+01, 'gdn_w_out': 1.482906e+01, 'kv_gain': 1.843851e+01, 'w_kv': 1.315250e+01, 'sb_w_q': 3.383173e-01, 'sb_w_o': 1.779021e+01}


def _to_microbatches(a, axis):
    t = _jnp.moveaxis(a, axis, 0)
    t = t.reshape((N_MICROBATCH, t.shape[0] // N_MICROBATCH) + t.shape[1:])
    return _jnp.moveaxis(t, 1, axis + 1)


def setup_inputs(seed: int = 0) -> dict:
    inp = _fwd_setup_inputs(seed)
    key = _jax.random.fold_in(_jax.random.key(seed), 7919)
    shape, _ = _output_shape()
    out = dict(inp)
    out["loss_target"] = _jax.random.normal(_jax.random.fold_in(key, 0), shape, _jnp.float32)
    for i, name in enumerate(TWIN_WEIGHTS):
        w = inp[name].astype(_jnp.float32)
        if MOMENT_SCALE is None:
            s = _jnp.sqrt(_jnp.mean(_jnp.square(w)) + 1e-30)
        else:
            s = MOMENT_SCALE[name]
        km, kv = _jax.random.split(_jax.random.fold_in(key, i + 1))
        out[name] = w
        out["m_" + name] = s * _jax.random.normal(km, w.shape, _jnp.float32)
        out["v_" + name] = (s * s) * _jax.random.uniform(kv, w.shape, _jnp.float32, 0.5, 1.5)
    if N_MICROBATCH > 1:
        for name, axis in PER_EXAMPLE_BATCH_AXIS.items():
            out[name] = _to_microbatches(out[name], axis)
    return {'x': out['x'], 'mix_pre_gain': out['mix_pre_gain'], 'mix_post_gain': out['mix_post_gain'], 'mlp_pre_gain': out['mlp_pre_gain'], 'mlp_post_gain': out['mlp_post_gain'], 'mlp_w_up': out['mlp_w_up'], 'mlp_w_down': out['mlp_w_down'], 'gdn_w_in': out['gdn_w_in'], 'gdn_conv_w': out['gdn_conv_w'], 'gdn_a_log': out['gdn_a_log'], 'gdn_dt_bias': out['gdn_dt_bias'], 'gdn_out_gain': out['gdn_out_gain'], 'gdn_w_out': out['gdn_w_out'], 'kv_gain': out['kv_gain'], 'w_kv': out['w_kv'], 'sb_w_q': out['sb_w_q'], 'sb_w_o': out['sb_w_o'], 'loss_target': out['loss_target'], 'm_mix_pre_gain': out['m_mix_pre_gain'], 'm_mix_post_gain': out['m_mix_post_gain'], 'm_mlp_pre_gain': out['m_mlp_pre_gain'], 'm_mlp_post_gain': out['m_mlp_post_gain'], 'm_mlp_w_up': out['m_mlp_w_up'], 'm_mlp_w_down': out['m_mlp_w_down'], 'm_gdn_w_in': out['m_gdn_w_in'], 'm_gdn_conv_w': out['m_gdn_conv_w'], 'm_gdn_a_log': out['m_gdn_a_log'], 'm_gdn_dt_bias': out['m_gdn_dt_bias'], 'm_gdn_out_gain': out['m_gdn_out_gain'], 'm_gdn_w_out': out['m_gdn_w_out'], 'm_kv_gain': out['m_kv_gain'], 'm_w_kv': out['m_w_kv'], 'm_sb_w_q': out['m_sb_w_q'], 'm_sb_w_o': out['m_sb_w_o'], 'v_mix_pre_gain': out['v_mix_pre_gain'], 'v_mix_post_gain': out['v_mix_post_gain'], 'v_mlp_pre_gain': out['v_mlp_pre_gain'], 'v_mlp_post_gain': out['v_mlp_post_gain'], 'v_mlp_w_up': out['v_mlp_w_up'], 'v_mlp_w_down': out['v_mlp_w_down'], 'v_gdn_w_in': out['v_gdn_w_in'], 'v_gdn_conv_w': out['v_gdn_conv_w'], 'v_gdn_a_log': out['v_gdn_a_log'], 'v_gdn_dt_bias': out['v_gdn_dt_bias'], 'v_gdn_out_gain': out['v_gdn_out_gain'], 'v_gdn_w_out': out['v_gdn_w_out'], 'v_kv_gain': out['v_kv_gain'], 'v_w_kv': out['v_w_kv'], 'v_sb_w_q': out['v_sb_w_q'], 'v_sb_w_o': out['v_sb_w_o']}


def _loss(weights, diff, rest, loss_target):
    with _jax.named_scope("forward"):
        args = {**rest, TWIN_DIFF_INPUT: diff, **{k: w.astype(_WEIGHT_DTYPES[k]) for k, w in weights.items()}}
        y = _forward(args)
    with _jax.named_scope("loss_head"):
        err = _jnp.square(y.astype(_jnp.float32) - loss_target)
        return 0.5 * _jnp.sum(_jnp.mean(err, axis=-1)) if err.ndim else 0.5 * err


def _adamw(w, g, m, v):
    m = ADAM_B1 * m + (1.0 - ADAM_B1) * g
    v = ADAM_B2 * v + (1.0 - ADAM_B2) * _jnp.square(g)
    m_hat = m / (1.0 - ADAM_B1 ** ADAM_STEP)
    v_hat = v / (1.0 - ADAM_B2 ** ADAM_STEP)
    delta = -ADAM_LR * (m_hat / (_jnp.sqrt(v_hat) + ADAM_EPS) + ADAM_WD * w)
    return delta, m, v


def reference(x, mix_pre_gain, mix_post_gain, mlp_pre_gain, mlp_post_gain, mlp_w_up, mlp_w_down, gdn_w_in, gdn_conv_w, gdn_a_log, gdn_dt_bias, gdn_out_gain, gdn_w_out, kv_gain, w_kv, sb_w_q, sb_w_o, loss_target, m_mix_pre_gain, m_mix_post_gain, m_mlp_pre_gain, m_mlp_post_gain, m_mlp_w_up, m_mlp_w_down, m_gdn_w_in, m_gdn_conv_w, m_gdn_a_log, m_gdn_dt_bias, m_gdn_out_gain, m_gdn_w_out, m_kv_gain, m_w_kv, m_sb_w_q, m_sb_w_o, v_mix_pre_gain, v_mix_post_gain, v_mlp_pre_gain, v_mlp_post_gain, v_mlp_w_up, v_mlp_w_down, v_gdn_w_in, v_gdn_conv_w, v_gdn_a_log, v_gdn_dt_bias, v_gdn_out_gain, v_gdn_w_out, v_kv_gain, v_w_kv, v_sb_w_q, v_sb_w_o):
    given = dict(x=x, mix_pre_gain=mix_pre_gain, mix_post_gain=mix_post_gain, mlp_pre_gain=mlp_pre_gain, mlp_post_gain=mlp_post_gain, mlp_w_up=mlp_w_up, mlp_w_down=mlp_w_down, gdn_w_in=gdn_w_in, gdn_conv_w=gdn_conv_w, gdn_a_log=gdn_a_log, gdn_dt_bias=gdn_dt_bias, gdn_out_gain=gdn_out_gain, gdn_w_out=gdn_w_out, kv_gain=kv_gain, w_kv=w_kv, sb_w_q=sb_w_q, sb_w_o=sb_w_o, loss_target=loss_target, m_mix_pre_gain=m_mix_pre_gain, m_mix_post_gain=m_mix_post_gain, m_mlp_pre_gain=m_mlp_pre_gain, m_mlp_post_gain=m_mlp_post_gain, m_mlp_w_up=m_mlp_w_up, m_mlp_w_down=m_mlp_w_down, m_gdn_w_in=m_gdn_w_in, m_gdn_conv_w=m_gdn_conv_w, m_gdn_a_log=m_gdn_a_log, m_gdn_dt_bias=m_gdn_dt_bias, m_gdn_out_gain=m_gdn_out_gain, m_gdn_w_out=m_gdn_w_out, m_kv_gain=m_kv_gain, m_w_kv=m_w_kv, m_sb_w_q=m_sb_w_q, m_sb_w_o=m_sb_w_o, v_mix_pre_gain=v_mix_pre_gain, v_mix_post_gain=v_mix_post_gain, v_mlp_pre_gain=v_mlp_pre_gain, v_mlp_post_gain=v_mlp_post_gain, v_mlp_w_up=v_mlp_w_up, v_mlp_w_down=v_mlp_w_down, v_gdn_w_in=v_gdn_w_in, v_gdn_conv_w=v_gdn_conv_w, v_gdn_a_log=v_gdn_a_log, v_gdn_dt_bias=v_gdn_dt_bias, v_gdn_out_gain=v_gdn_out_gain, v_gdn_w_out=v_gdn_w_out, v_kv_gain=v_kv_gain, v_w_kv=v_w_kv, v_sb_w_q=v_sb_w_q, v_sb_w_o=v_sb_w_o)
    weights = {n: given[n] for n in TWIN_WEIGHTS}
    shared = {n: given[n] for n in SHARED_INPUTS}
    per_example = {n: given[n] for n in ['x']}
    grad_fn = _jax.value_and_grad(_loss, argnums=(0, 1))

    def one_microbatch(ex, loss_target):
        ex = dict(ex)
        diff = ex.pop(TWIN_DIFF_INPUT)
        return grad_fn(weights, diff, {**shared, **ex}, loss_target)

    if N_MICROBATCH == 1:
        loss, (grad_w, grad_x) = one_microbatch(per_example, given["loss_target"])
    else:
        def body(carry, xs):
            loss_sum, grad_sum = carry
            l_k, (gw_k, gx_k) = one_microbatch(xs[0], xs[1])
            with _jax.named_scope("update"):
                return (loss_sum + l_k, _jax.tree.map(_jnp.add, grad_sum, gw_k)), gx_k

        init = (_jnp.zeros((), _jnp.float32), _jax.tree.map(_jnp.zeros_like, weights))
        (loss, grad_w), grad_x = _jax.lax.scan(body, init, (per_example, given["loss_target"]))
    with _jax.named_scope("update"):
        delta_w, new_m, new_v = {}, {}, {}
        for n in TWIN_WEIGHTS:
            delta_w[n], new_m[n], new_v[n] = _adamw(weights[n], grad_w[n], given["m_" + n], given["v_" + n])
    return (loss, grad_x, *[grad_w[n] for n in TWIN_WEIGHTS], *[delta_w[n] for n in TWIN_WEIGHTS],
            *[new_m[n] for n in TWIN_WEIGHTS], *[new_v[n] for n in TWIN_WEIGHTS])
```

```python
import functools

import jax
import jax.numpy as jnp
from jax import lax
from jax.experimental import pallas as pl
from jax.experimental.pallas import tpu as pltpu

F32, BF16 = jnp.float32, jnp.bfloat16
HI = lax.Precision.HIGHEST
MESH = pl.DeviceIdType.MESH

EPS = 1e-6
D_MODEL = 1024
HEADS = 8
HEAD_DIM = 128
CHUNK = 64
CONV_K = 4
D_FF = 4096
QKV = 3 * HEADS * HEAD_DIM

ADAM_LR, ADAM_B1, ADAM_B2, ADAM_EPS, ADAM_WD, ADAM_STEP = 0.001, 0.9, 0.999, 1e-08, 0.01, 10

VMEM_LIMIT_BYTES = 48 * 1024 * 1024
LANES = 128

NN = ((1,), (0,))
NT = ((1,), (1,))
TN = ((0,), (0,))


def _dot(a, b, dims=NN, precision=None):
    return lax.dot_general(a, b, (dims, ((), ())), precision=precision, preferred_element_type=F32)


def _params(*sem):
    return pltpu.CompilerParams(dimension_semantics=sem, vmem_limit_bytes=VMEM_LIMIT_BYTES)


def _iota(shape, axis):
    return lax.broadcasted_iota(jnp.int32, shape, axis)


def _matmul(a, b, mode, out_dtype, name, tm=1024, tn=1024, tk=512):
    if mode == "nn":
        (m, k), (k2, n) = a.shape, b.shape
    elif mode == "nt":
        (m, k), (n, k2) = a.shape, b.shape
    else:
        (k, m), (k2, n) = a.shape, b.shape
    assert k == k2, (a.shape, b.shape, mode)
    tm, tn, tk = min(tm, m), min(tn, n), min(tk, k)
    assert m % tm == 0 and n % tn == 0 and k % tk == 0, (a.shape, b.shape, mode)
    nk = k // tk
    dims = {"nn": NN, "nt": NT, "tn": TN}[mode]

    def body(a_ref, b_ref, o_ref, acc_ref):
        kk = pl.program_id(2)

        @pl.when(kk == 0)
        def _():
            acc_ref[...] = jnp.zeros_like(acc_ref)

        acc_ref[...] += _dot(a_ref[...].astype(BF16), b_ref[...].astype(BF16), dims)

        @pl.when(kk == nk - 1)
        def _():
            o_ref[...] = acc_ref[...].astype(o_ref.dtype)

    a_spec = pl.BlockSpec((tk, tm), lambda i, j, kk: (kk, i)) if mode == "tn" else pl.BlockSpec((tm, tk), lambda i, j, kk: (i, kk))
    b_spec = pl.BlockSpec((tn, tk), lambda i, j, kk: (j, kk)) if mode == "nt" else pl.BlockSpec((tk, tn), lambda i, j, kk: (kk, j))
    return pl.pallas_call(
        body,
        name=name,
        grid=(m // tm, n // tn, nk),
        in_specs=[a_spec, b_spec],
        out_specs=pl.BlockSpec((tm, tn), lambda i, j, kk: (i, j)),
        out_shape=jax.ShapeDtypeStruct((m, n), out_dtype),
        scratch_shapes=[pltpu.VMEM((tm, tn), F32)],
        compiler_params=_params("parallel", "parallel", "arbitrary"),
    )(a, b)


def _row_specs(rows, tm):
    return [pl.BlockSpec((tm, w), lambda i, cb=cb: (i, cb)) for _, w, cb in rows]


def _full_spec(p):
    return pl.BlockSpec(p.shape, lambda i: (0,) * p.ndim)


def _rowwise(name, fn, rows, params, outs, tm=256):
    t = rows[0][0].shape[0]
    tm = min(tm, t)
    nr, npar = len(rows), len(params)

    def body(*refs):
        ins = [r[...].astype(F32) for r in refs[:nr]]
        ps = [p[...] for p in refs[nr:nr + npar]]
        res = fn(*ins, *ps)
        for o_ref, r in zip(refs[nr + npar:], res):
            o_ref[...] = r.astype(o_ref.dtype)

    return pl.pallas_call(
        body,
        name=name,
        grid=(t // tm,),
        in_specs=_row_specs(rows, tm) + [_full_spec(p) for p in params],
        out_specs=[pl.BlockSpec((tm, w), lambda i: (i, 0)) for w, _ in outs],
        out_shape=[jax.ShapeDtypeStruct((t, w), dt) for w, dt in outs],
        compiler_params=_params("parallel"),
    )(*[r[0] for r in rows], *params)


def _rowwise_bwd(name, fn, rows, params, cots, grad_dtypes, tm=256):
    t = rows[0][0].shape[0]
    tm = min(tm, t)
    nr, npar, nc = len(rows), len(params), len(cots)
    want = [j for j, dt in enumerate(grad_dtypes) if dt is not None]

    def body(*refs):
        i = pl.program_id(0)
        ins = [r[...].astype(F32) for r in refs[:nr]]
        ps = [p[...] for p in refs[nr:nr + npar]]
        cs = tuple(c[...].astype(F32) for c in refs[nr + npar:nr + npar + nc])
        _, vjp = jax.vjp(fn, *ins, *ps)
        gs = vjp(cs)
        outs = refs[nr + npar + nc:]
        for o_ref, j in zip(outs[:len(want)], want):
            o_ref[...] = gs[j].astype(o_ref.dtype)
        pg_refs = outs[len(want):]

        @pl.when(i == 0)
        def _():
            for pg in pg_refs:
                pg[...] = jnp.zeros_like(pg)

        for pg, g in zip(pg_refs, gs[nr:]):
            pg[...] += g

    res = pl.pallas_call(
        body,
        name=name,
        grid=(t // tm,),
        in_specs=_row_specs(rows, tm) + [_full_spec(p) for p in params] + [pl.BlockSpec((tm, c.shape[1]), lambda i: (i, 0)) for c in cots],
        out_specs=[pl.BlockSpec((tm, rows[j][1]), lambda i: (i, 0)) for j in want] + [_full_spec(p) for p in params],
        out_shape=[jax.ShapeDtypeStruct((t, rows[j][1]), grad_dtypes[j]) for j in want] + [jax.ShapeDtypeStruct(p.shape, F32) for p in params],
        compiler_params=_params("arbitrary"),
    )(*[r[0] for r in rows], *params, *cots)
    return res[:len(want)], res[len(want):]


def _rms(x, g):
    return x * lax.rsqrt(jnp.mean(x * x, axis=-1, keepdims=True) + EPS) * g


def _sigmoid(x):
    return 1.0 / (1.0 + jnp.exp(-x))


def _softplus(x):
    return jnp.maximum(x, 0.0) + jnp.log1p(jnp.exp(-jnp.abs(x)))


def _head_sum(x):
    w = HEADS * HEAD_DIM
    fold = jnp.where((_iota((w, LANES), 0) >> 7) == _iota((w, LANES), 1), 1.0, 0.0)
    spread = jnp.where(_iota((LANES, w), 0) == (_iota((LANES, w), 1) >> 7), 1.0, 0.0)
    return _dot(_dot(x, fold, precision=HI), spread, precision=HI)


def _fn_norm(x, g):
    return (_rms(x, g),)


def _fn_gates(ba, al, dt):
    col = _iota((1, LANES), 1)
    bg = jnp.where(col < HEADS, _sigmoid(ba), -jnp.exp(al) * _softplus(ba + dt))
    w = HEADS * HEAD_DIM
    src, head = _iota((LANES, w), 0), _iota((LANES, w), 1) >> 7
    e_beta = jnp.where(src == head, 1.0, 0.0)
    e_g = jnp.where(src == head + HEADS, 1.0, 0.0)
    return _dot(bg, e_beta, precision=HI), _dot(bg, e_g, precision=HI)


def _fn_post_q(c):
    s = c * _sigmoid(c)
    return (s * lax.rsqrt(_head_sum(s * s) + EPS) * (HEAD_DIM ** -0.5),)


def _fn_post_k(c):
    s = c * _sigmoid(c)
    return (s * lax.rsqrt(_head_sum(s * s) + EPS),)


def _fn_post_v(c):
    return (c * _sigmoid(c),)


def _fn_outnorm(o, gate, og):
    y = o * lax.rsqrt(_head_sum(o * o) * (1.0 / HEAD_DIM) + EPS) * og
    return (y * (gate * _sigmoid(gate)),)


def _fn_res_norm(x, m, gp, gn):
    x1 = x + _rms(m, gp)
    return x1, _rms(x1, gn)


def _fn_res_norm2(x, m, gp, ga, gb):
    x1 = x + _rms(m, gp)
    return x1, _rms(x1, ga), _rms(x1, gb)


def _fn_relu2(u):
    r = jnp.maximum(u, 0.0)
    return (r * r,)


def _loss_call(x3, d1, tgt, g, tm=256):
    t, d = x3.shape
    tm = min(tm, t)

    def body(x_ref, d_ref, t_ref, g_ref, loss_ref, dx_ref, dd_ref, dg_ref):
        i = pl.program_id(0)
        y, vjp = jax.vjp(lambda x, dd, gg: x + _rms(dd, gg), x_ref[...], d_ref[...], g_ref[...])
        err = y - t_ref[...]
        lrow = 0.5 * jnp.mean(err * err, axis=-1, keepdims=True)
        dx, dd, dg = vjp(err * (1.0 / d))
        dx_ref[...] = dx
        dd_ref[...] = dd

        @pl.when(i == 0)
        def _():
            loss_ref[...] = jnp.zeros_like(loss_ref)
            dg_ref[...] = jnp.zeros_like(dg_ref)

        loss_ref[...] += jnp.broadcast_to(jnp.sum(lrow, axis=0, keepdims=True), loss_ref.shape)
        dg_ref[...] += dg

    row = pl.BlockSpec((tm, d), lambda i: (i, 0))
    return pl.pallas_call(
        body,
        name="loss_head",
        grid=(t // tm,),
        in_specs=[row, row, row, _full_spec(g)],
        out_specs=[pl.BlockSpec((8, LANES), lambda i: (0, 0)), row, row, _full_spec(g)],
        out_shape=[jax.ShapeDtypeStruct((8, LANES), F32), jax.ShapeDtypeStruct((t, d), F32), jax.ShapeDtypeStruct((t, d), F32), jax.ShapeDtypeStruct(g.shape, F32)],
        compiler_params=_params("arbitrary"),
    )(x3, d1, tgt, g)


HALO = 8


def _conv_fwd(qkvg, conv_w, tm=256):
    t = qkvg.shape[0]
    tm = min(tm, t)

    def body(cur_ref, prev_ref, w_ref, o_ref, buf):
        i = pl.program_id(0)
        buf[0:HALO, :] = jnp.where(i > 0, prev_ref[...], 0.0)
        buf[HALO:, :] = cur_ref[...]
        acc = buf[pl.ds(HALO - CONV_K + 1, tm), :] * w_ref[pl.ds(0, 1), :]
        for j in range(1, CONV_K):
            acc = acc + buf[pl.ds(HALO - CONV_K + 1 + j, tm), :] * w_ref[pl.ds(j, 1), :]
        o_ref[...] = acc

    return pl.pallas_call(
        body,
        name="conv_fwd",
        grid=(t // tm,),
        in_specs=[
            pl.BlockSpec((tm, QKV), lambda i: (i, 0)),
            pl.BlockSpec((HALO, QKV), lambda i: (jnp.maximum(i * (tm // HALO) - 1, 0), 0)),
            pl.BlockSpec((CONV_K, QKV), lambda i: (0, 0)),
        ],
        out_specs=pl.BlockSpec((tm, QKV), lambda i: (i, 0)),
        out_shape=jax.ShapeDtypeStruct((t, QKV), F32),
        scratch_shapes=[pltpu.VMEM((tm + HALO, QKV), F32)],
        compiler_params=_params("parallel"),
    )(qkvg, qkvg, conv_w)


def _conv_bwd(dc, qkvg, conv_w, tm=256):
    t = dc.shape[0]
    tm = min(tm, t)
    n = t // tm

    def body(dc_ref, dcn_ref, x_ref, xp_ref, w_ref, dx_ref, dw_ref, bufd, bufx):
        i = pl.program_id(0)
        bufd[0:tm, :] = dc_ref[...]
        bufd[tm:, :] = jnp.where(i < n - 1, dcn_ref[...], 0.0)
        bufx[0:HALO, :] = jnp.where(i > 0, xp_ref[...], 0.0)
        bufx[HALO:, :] = x_ref[...]

        @pl.when(i == 0)
        def _():
            dw_ref[...] = jnp.zeros_like(dw_ref)

        dcv = dc_ref[...]
        acc = bufd[pl.ds(CONV_K - 1, tm), :] * w_ref[pl.ds(0, 1), :]
        for j in range(1, CONV_K):
            acc = acc + bufd[pl.ds(CONV_K - 1 - j, tm), :] * w_ref[pl.ds(j, 1), :]
        dx_ref[...] = acc
        for j in range(CONV_K):
            dw_ref[pl.ds(j, 1), :] += jnp.sum(dcv * bufx[pl.ds(HALO - CONV_K + 1 + j, tm), :], axis=0, keepdims=True)

    return pl.pallas_call(
        body,
        name="conv_bwd",
        grid=(n,),
        in_specs=[
            pl.BlockSpec((tm, QKV), lambda i: (i, 0)),
            pl.BlockSpec((HALO, QKV), lambda i: (jnp.minimum((i + 1) * (tm // HALO), t // HALO - 1), 0)),
            pl.BlockSpec((tm, QKV), lambda i: (i, 0)),
            pl.BlockSpec((HALO, QKV), lambda i: (jnp.maximum(i * (tm // HALO) - 1, 0), 0)),
            pl.BlockSpec((CONV_K, QKV), lambda i: (0, 0)),
        ],
        out_specs=[pl.BlockSpec((tm, QKV), lambda i: (i, 0)), pl.BlockSpec((HALO, QKV), lambda i: (0, 0))],
        out_shape=[jax.ShapeDtypeStruct((t, QKV), F32), jax.ShapeDtypeStruct((HALO, QKV), F32)],
        scratch_shapes=[pltpu.VMEM((tm + HALO, QKV), F32), pltpu.VMEM((tm + HALO, QKV), F32)],
        compiler_params=_params("arbitrary"),
    )(dc, dc, qkvg, qkvg, conv_w)


PREP_CHUNKS = 4


def _prep_chunk(q, k, v, b, g):
    c = CHUNK
    r, col = _iota((c, c), 0), _iota((c, c), 1)
    incl, strict = r >= col, r > col
    gc = _dot(jnp.where(incl, 1.0, 0.0), g, precision=HI)
    gci = _dot(gc, jnp.full((HEAD_DIM, c), 1.0 / HEAD_DIM, F32), precision=HI)
    gcj = _dot(jnp.full((c, HEAD_DIM), 1.0 / HEAD_DIM, F32), gc, NT, precision=HI)
    decay = jnp.where(incl, jnp.exp(jnp.where(incl, gci - gcj, 0.0)), 0.0)
    kb = k * b
    kbf = k.astype(BF16)
    lower = jnp.where(strict, _dot(kb.astype(BF16), kbf, NT) * decay, 0.0)
    x = -lower
    tmat = jnp.where(r == col, 1.0, 0.0) + x
    p = x
    for _ in range(5):
        p = _dot(p, p, precision=HI)
        tmat = tmat + _dot(tmat, p, precision=HI)
    tb = tmat.astype(BF16)
    egc = jnp.exp(gc)
    w = _dot(tb, (kb * egc).astype(BF16))
    u = _dot(tb, (v * b).astype(BF16))
    attn = _dot(q.astype(BF16), kbf, NT) * decay
    qg = q * egc
    glast = _dot(jnp.ones((c, c), F32), g, precision=HI)
    kg = k * jnp.exp(glast - gc)
    gl = jnp.exp(_dot(jnp.ones((8, c), F32), g, precision=HI))
    return w, u, qg, kg, attn, gl


def _head_block(rows):
    return pl.BlockSpec((rows, HEAD_DIM), lambda h, n: (n, h))


def _gdn_prep(q, k, v, beta, g):
    t = q.shape[0]
    gch = min(PREP_CHUNKS, t // CHUNK)
    rows = gch * CHUNK

    def body(q_ref, k_ref, v_ref, b_ref, g_ref, w_ref, u_ref, qg_ref, kg_ref, at_ref, gl_ref):
        for c in range(gch):
            sl = pl.ds(c * CHUNK, CHUNK)
            w, u, qg, kg, attn, gl = _prep_chunk(q_ref[sl, :], k_ref[sl, :], v_ref[sl, :], b_ref[sl, :], g_ref[sl, :])
            w_ref[sl, :] = w.astype(BF16)
            u_ref[sl, :] = u
            qg_ref[sl, :] = qg.astype(BF16)
            kg_ref[sl, :] = kg.astype(BF16)
            at_ref[0, sl, :] = attn.astype(BF16)
            gl_ref[0, pl.ds(c * 8, 8), :] = gl

    hb = _head_block(rows)
    wide = HEADS * HEAD_DIM
    return pl.pallas_call(
        body,
        name="gdn_prep",
        grid=(HEADS, t // rows),
        in_specs=[hb] * 5,
        out_specs=[hb, hb, hb, hb, pl.BlockSpec((1, rows, CHUNK), lambda h, n: (h, n, 0)), pl.BlockSpec((1, gch * 8, HEAD_DIM), lambda h, n: (h, n, 0))],
        out_shape=[
            jax.ShapeDtypeStruct((t, wide), BF16),
            jax.ShapeDtypeStruct((t, wide), F32),
            jax.ShapeDtypeStruct((t, wide), BF16),
            jax.ShapeDtypeStruct((t, wide), BF16),
            jax.ShapeDtypeStruct((HEADS, t, CHUNK), BF16),
            jax.ShapeDtypeStruct((HEADS, t // CHUNK * 8, HEAD_DIM), F32),
        ],
        compiler_params=_params("parallel", "parallel"),
    )(q, k, v, beta, g)


def _gdn_prep_bwd(q, k, v, beta, g, dw, du, dqg, dkg, dattn, dgl):
    t = q.shape[0]
    gch = min(PREP_CHUNKS, t // CHUNK)
    rows = gch * CHUNK

    def body(q_ref, k_ref, v_ref, b_ref, g_ref, dw_ref, du_ref, dqg_ref, dkg_ref, dat_ref, dgl_ref, dq_ref, dk_ref, dv_ref, db_ref, dg_ref):
        for c in range(gch):
            sl = pl.ds(c * CHUNK, CHUNK)
            _, vjp = jax.vjp(_prep_chunk, q_ref[sl, :], k_ref[sl, :], v_ref[sl, :], b_ref[sl, :], g_ref[sl, :])
            dq, dk, dv, db, dg = vjp((dw_ref[sl, :], du_ref[sl, :], dqg_ref[sl, :], dkg_ref[sl, :], dat_ref[0, sl, :], dgl_ref[0, pl.ds(c * 8, 8), :]))
            dq_ref[sl, :] = dq
            dk_ref[sl, :] = dk
            dv_ref[sl, :] = dv
            db_ref[sl, :] = db
            dg_ref[sl, :] = dg

    hb = _head_block(rows)
    wide = HEADS * HEAD_DIM
    return pl.pallas_call(
        body,
        name="gdn_prep_bwd",
        grid=(HEADS, t // rows),
        in_specs=[hb] * 9 + [pl.BlockSpec((1, rows, CHUNK), lambda h, n: (h, n, 0)), pl.BlockSpec((1, gch * 8, HEAD_DIM), lambda h, n: (h, n, 0))],
        out_specs=[hb] * 5,
        out_shape=[jax.ShapeDtypeStruct((t, wide), F32)] * 5,
        compiler_params=_params("parallel", "parallel"),
    )(q, k, v, beta, g, dw, du, dqg, dkg, dattn, dgl)


def _gdn_scan(w, u, qg, kg, attn, gl):
    t = w.shape[0]
    n = t // CHUNK
    wide = HEADS * HEAD_DIM

    def body(w_ref, u_ref, qg_ref, kg_ref, at_ref, gl_ref, o_ref, st_ref, s_ref):
        @pl.when(pl.program_id(0) == 0)
        def _():
            s_ref[...] = jnp.zeros_like(s_ref)

        for h in range(HEADS):
            hs = pl.ds(h * HEAD_DIM, HEAD_DIM)
            s = s_ref[h]
            sb = s.astype(BF16)
            st_ref[0, h] = s
            vn = u_ref[:, hs] - _dot(w_ref[:, hs], sb)
            vb = vn.astype(BF16)
            o_ref[:, hs] = _dot(qg_ref[:, hs], sb) + _dot(at_ref[h], vb)
            s_ref[h] = s * jnp.tile(gl_ref[h], (HEAD_DIM // 8, 1)) + _dot(kg_ref[:, hs], vb, TN)

    row = pl.BlockSpec((CHUNK, wide), lambda i: (i, 0))
    return pl.pallas_call(
        body,
        name="gdn_scan",
        grid=(n,),
        in_specs=[row, row, row, row, pl.BlockSpec((HEADS, CHUNK, CHUNK), lambda i: (0, i, 0)), pl.BlockSpec((HEADS, 8, HEAD_DIM), lambda i: (0, i, 0))],
        out_specs=[row, pl.BlockSpec((1, HEADS, HEAD_DIM, HEAD_DIM), lambda i: (i, 0, 0, 0))],
        out_shape=[jax.ShapeDtypeStruct((t, wide), F32), jax.ShapeDtypeStruct((n, HEADS, HEAD_DIM, HEAD_DIM), F32)],
        scratch_shapes=[pltpu.VMEM((HEADS, HEAD_DIM, HEAD_DIM), F32)],
        compiler_params=_params("arbitrary"),
    )(w, u, qg, kg, attn, gl)


def _gdn_scan_bwd(w, u, qg, kg, attn, gl, states, do):
    t = w.shape[0]
    n = t // CHUNK
    wide = HEADS * HEAD_DIM

    def body(w_ref, u_ref, qg_ref, kg_ref, at_ref, gl_ref, st_ref, do_ref, dw_ref, du_ref, dqg_ref, dkg_ref, dat_ref, dgl_ref, ds_ref):
        @pl.when(pl.program_id(0) == 0)
        def _():
            ds_ref[...] = jnp.zeros_like(ds_ref)

        for h in range(HEADS):
            hs = pl.ds(h * HEAD_DIM, HEAD_DIM)
            s = st_ref[0, h]
            sb = s.astype(BF16)
            wv, qgv, kgv, atv = w_ref[:, hs], qg_ref[:, hs], kg_ref[:, hs], at_ref[h]
            vb = (u_ref[:, hs] - _dot(wv, sb)).astype(BF16)
            dsn = ds_ref[h]
            dsb = dsn.astype(BF16)
            dob = do_ref[:, hs].astype(BF16)
            dvn = _dot(atv, dob, TN) + _dot(kgv, dsb)
            dvb = dvn.astype(BF16)
            dat_ref[h] = _dot(dob, vb, NT)
            dqg_ref[:, hs] = _dot(dob, sb, NT)
            dkg_ref[:, hs] = _dot(vb, dsb, NT)
            du_ref[:, hs] = dvn
            dw_ref[:, hs] = -_dot(dvb, sb, NT)
            dgl_ref[h] = jnp.sum((dsn * s).reshape(HEAD_DIM // 8, 8, HEAD_DIM), axis=0)
            ds_ref[h] = dsn * jnp.tile(gl_ref[h], (HEAD_DIM // 8, 1)) + _dot(qgv, dob, TN) - _dot(wv, dvb, TN)

    row = pl.BlockSpec((CHUNK, wide), lambda i: (n - 1 - i, 0))
    at = pl.BlockSpec((HEADS, CHUNK, CHUNK), lambda i: (0, n - 1 - i, 0))
    glb = pl.BlockSpec((HEADS, 8, HEAD_DIM), lambda i: (0, n - 1 - i, 0))
    return pl.pallas_call(
        body,
        name="gdn_scan_bwd",
        grid=(n,),
        in_specs=[row, row, row, row, at, glb, pl.BlockSpec((1, HEADS, HEAD_DIM, HEAD_DIM), lambda i: (n - 1 - i, 0, 0, 0)), row],
        out_specs=[row, row, row, row, at, glb],
        out_shape=[jax.ShapeDtypeStruct((t, wide), F32)] * 4 + [jax.ShapeDtypeStruct((HEADS, t, CHUNK), F32), jax.ShapeDtypeStruct((HEADS, n * 8, HEAD_DIM), F32)],
        scratch_shapes=[pltpu.VMEM((HEADS, HEAD_DIM, HEAD_DIM), F32)],
        compiler_params=_params("arbitrary"),
    )(w, u, qg, kg, attn, gl, states, do)


SB_BLOCK = 256


def _split_dot(x, m):
    hi = x.astype(BF16)
    lo = (x - hi.astype(F32)).astype(BF16)
    return _dot(hi, m) + _dot(lo, m)


def _sb_scores(q, k):
    z = _dot(q, k, NT) * (HEAD_DIM ** -0.5)
    e = jnp.exp(-jnp.abs(z))
    lp = jnp.log1p(e)
    return z, e, jnp.minimum(z, 0.0) - lp, jnp.minimum(-z, 0.0) - lp


def _sum_matrix(bk, rel):
    r, c = _iota((bk, bk + LANES), 0), _iota((bk, bk + LANES), 1)
    return jnp.where(rel(r, c) | (c >= bk), 1.0, 0.0).astype(BF16)


def _sb_fwd(q, k, v):
    t = q.shape[0]
    bq = min(SB_BLOCK, t)
    rep = bq // LANES

    def body(q_ref, k_ref, v_ref, o_ref, rt_ref):
        i = pl.program_id(1)
        qv = q_ref[...]
        after = _sum_matrix(bq, lambda r, c: r > c)
        mask = _iota((bq, bq), 1) < _iota((bq, bq), 0)

        def step(j, acc, run, diag):
            st = pl.multiple_of(j * bq, bq)
            kv, vv = k_ref[pl.ds(st, bq), :], v_ref[pl.ds(st, bq), :]
            _, _, lb, l1m = _sb_scores(qv, kv)
            if diag:
                l1m = jnp.where(mask, l1m, 0.0)
            sums = _split_dot(l1m, after)
            a = jnp.exp(lb + jnp.tile(run, (1, rep)) + sums[:, :bq])
            if diag:
                a = jnp.where(mask, a, 0.0)
            return acc + _dot(a.astype(BF16), vv), run + sums[:, bq:]

        zero = jnp.zeros((bq, LANES), F32)
        acc, run = step(i, zero, zero, True)
        acc, run = lax.fori_loop(1, i + 1, lambda jj, c: step(i - jj, c[0], c[1], False), (acc, run))
        o_ref[...] = acc
        rt_ref[...] = run

    qb = pl.BlockSpec((bq, HEAD_DIM), lambda h, i: (i, h))
    full = pl.BlockSpec((t, HEAD_DIM), lambda h, i: (0, h))
    return pl.pallas_call(
        body,
        name="sb_fwd",
        grid=(HEADS, t // bq),
        in_specs=[qb, full, full],
        out_specs=[qb, qb],
        out_shape=[jax.ShapeDtypeStruct(q.shape, F32), jax.ShapeDtypeStruct(q.shape, F32)],
        compiler_params=_params("parallel", "arbitrary"),
    )(q, k, v)


def _sb_bwd(q, k, v, rt, do):
    t = q.shape[0]
    bq = min(SB_BLOCK, t)
    rep = bq // LANES
    scale = HEAD_DIM ** -0.5

    def body(q_ref, k_ref, v_ref, rt_ref, do_ref, dq_ref, dk_ref, dv_ref):
        i = pl.program_id(1)

        @pl.when(i == 0)
        def _():
            dk_ref[...] = jnp.zeros_like(dk_ref)
            dv_ref[...] = jnp.zeros_like(dv_ref)

        qv = q_ref[...]
        dob = do_ref[...].astype(BF16)
        rtot = rt_ref[...]
        upto = _sum_matrix(bq, lambda r, c: r <= c)
        before = _sum_matrix(bq, lambda r, c: r < c)
        mask = _iota((bq, bq), 1) < _iota((bq, bq), 0)

        def step(j, dq, left, pg, diag):
            st = pl.multiple_of(j * bq, bq)
            kv, vv = k_ref[pl.ds(st, bq), :], v_ref[pl.ds(st, bq), :]
            z, e, lb, l1m = _sb_scores(qv, kv)
            if diag:
                l1m = jnp.where(mask, l1m, 0.0)
            beta = jnp.where(z >= 0.0, 1.0, e) / (1.0 + e)
            sums = _split_dot(l1m, upto)
            a = jnp.exp(lb + jnp.tile(rtot - left, (1, rep)) - sums[:, :bq])
            if diag:
                a = jnp.where(mask, a, 0.0)
            g = _dot(dob, vv, NT) * a
            dv_ref[pl.ds(st, bq), :] += _dot(a.astype(BF16), dob, TN)
            gsum = _split_dot(g, before)
            dz = g * (1.0 - beta) - (jnp.tile(pg, (1, rep)) + gsum[:, :bq]) * beta
            if diag:
                dz = jnp.where(mask, dz, 0.0)
            dzb = (dz * scale).astype(BF16)
            dk_ref[pl.ds(st, bq), :] += _dot(dzb, qv, TN)
            return dq + _dot(dzb, kv), left + sums[:, bq:], pg + gsum[:, bq:]

        zero = jnp.zeros((bq, LANES), F32)
        dq, left, pg = lax.fori_loop(0, i, lambda j, c: step(j, c[0], c[1], c[2], False), (zero, zero, zero))
        dq, _, _ = step(i, dq, left, pg, True)
        dq_ref[...] = dq

    qb = pl.BlockSpec((bq, HEAD_DIM), lambda h, i: (i, h))
    full = pl.BlockSpec((t, HEAD_DIM), lambda h, i: (0, h))
    return pl.pallas_call(
        body,
        name="sb_bwd",
        grid=(HEADS, t // bq),
        in_specs=[qb, full, full, qb, qb],
        out_specs=[qb, full, full],
        out_shape=[jax.ShapeDtypeStruct(q.shape, F32)] * 3,
        compiler_params=_params("parallel", "arbitrary"),
    )(q, k, v, rt, do)


def _adamw(w, g, m, v, name, tm=256):
    r, c = w.shape
    tm = tm if r % tm == 0 else r

    def body(w_ref, g_ref, m_ref, v_ref, d_ref, nm_ref, nv_ref):
        gv = g_ref[...]
        nm = ADAM_B1 * m_ref[...] + (1.0 - ADAM_B1) * gv
        nv = ADAM_B2 * v_ref[...] + (1.0 - ADAM_B2) * (gv * gv)
        m_hat = nm / (1.0 - ADAM_B1 ** ADAM_STEP)
        v_hat = nv / (1.0 - ADAM_B2 ** ADAM_STEP)
        d_ref[...] = -ADAM_LR * (m_hat / (jnp.sqrt(v_hat) + ADAM_EPS) + ADAM_WD * w_ref[...])
        nm_ref[...] = nm
        nv_ref[...] = nv

    blk = pl.BlockSpec((tm, c), lambda i: (i, 0))
    return pl.pallas_call(
        body,
        name=name,
        grid=(r // tm,),
        in_specs=[blk] * 4,
        out_specs=[blk] * 3,
        out_shape=[jax.ShapeDtypeStruct((r, c), F32)] * 3,
        compiler_params=_params("parallel"),
    )(w, g, m, v)


def _local_step(x, tgt, gains, wts, small):
    mix_pre, mix_post, mlp_pre, mlp_post, kv_gain = gains
    w_qkvg, w_ba, w_out, w_kv, w_q, w_o, w_up, w_down = wts
    conv_w, a_log, dt_bias, out_gain = small
    t, d = x.shape
    row = lambda a, i=None: a[i:i + 1] if i is not None else a
    al = jnp.zeros((1, LANES), F32).at[:, HEADS:2 * HEADS].set(a_log)
    dtb = jnp.zeros((1, LANES), F32).at[:, HEADS:2 * HEADS].set(dt_bias)
    og = jnp.tile(out_gain, (1, HEADS))
    full = lambda a: (a, a.shape[1], 0)

    (h0,) = _rowwise("norm_in", _fn_norm, [full(x)], [row(mix_pre, 0)], [(d, BF16)])
    qkvg = _matmul(h0, w_qkvg, "nn", F32, "mm_gdn_in", tk=1024)
    ba = _matmul(h0, w_ba, "nn", F32, "mm_gdn_ba", tk=1024)
    conv = _conv_fwd(qkvg, conv_w)
    (gq,) = _rowwise("post_q", _fn_post_q, [(conv, d, 0)], [], [(d, F32)])
    (gk,) = _rowwise("post_k", _fn_post_k, [(conv, d, 1)], [], [(d, F32)])
    (gv,) = _rowwise("post_v", _fn_post_v, [(conv, d, 2)], [], [(d, F32)])
    beta, gdec = _rowwise("gates", _fn_gates, [full(ba)], [al, dtb], [(d, F32), (d, F32)])
    pw, pu, pqg, pkg, pattn, pgl = _gdn_prep(gq, gk, gv, beta, gdec)
    o_gdn, states = _gdn_scan(pw, pu, pqg, pkg, pattn, pgl)
    (on,) = _rowwise("out_norm", _fn_outnorm, [full(o_gdn), (qkvg, d, 3)], [og], [(d, BF16)])
    mix0 = _matmul(on, w_out, "nn", F32, "mm_gdn_out", tk=1024)
    x1, h1 = _rowwise("res_a0", _fn_res_norm, [full(x), full(mix0)], [row(mix_post, 0), row(mlp_pre, 0)], [(d, F32), (d, BF16)])
    u0 = _matmul(h1, w_up[0], "nn", F32, "mm_up0", tk=1024)
    (a0,) = _rowwise("relu2_0", _fn_relu2, [full(u0)], [], [(D_FF, BF16)])
    d0 = _matmul(a0, w_down[0], "nn", F32, "mm_down0")
    x2, hkv, hq = _rowwise("res_b0", _fn_res_norm2, [full(x1), full(d0)], [row(mlp_post, 0), kv_gain, row(mix_pre, 1)], [(d, F32), (d, BF16), (d, BF16)])
    kvp = _matmul(hkv, w_kv, "nn", BF16, "mm_kv", tk=1024)
    qp = _matmul(hq, w_q, "nn", BF16, "mm_q", tk=1024)
    kp, vp = kvp[:, :d], kvp[:, d:]
    o_sb, rt = _sb_fwd(qp, kp, vp)
    mix1 = _matmul(o_sb, w_o, "nn", F32, "mm_sb_out", tk=1024)
    x3, h3 = _rowwise("res_a1", _fn_res_norm, [full(x2), full(mix1)], [row(mix_post, 1), row(mlp_pre, 1)], [(d, F32), (d, BF16)])
    u1 = _matmul(h3, w_up[1], "nn", F32, "mm_up1", tk=1024)
    (a1,) = _rowwise("relu2_1", _fn_relu2, [full(u1)], [], [(D_FF, BF16)])
    d1 = _matmul(a1, w_down[1], "nn", F32, "mm_down1")

    loss, dx3, dd1, g_mlp_post1 = _loss_call(x3, d1, tgt, row(mlp_post, 1))
    da1 = _matmul(dd1, w_down[1], "nt", F32, "mm_down1_dx")
    g_down1 = _matmul(a1, dd1, "tn", F32, "mm_down1_dw")
    (du1,), _ = _rowwise_bwd("relu2_1_bwd", _fn_relu2, [full(u1)], [], [da1], [BF16])
    dh3 = _matmul(du1, w_up[1], "nt", F32, "mm_up1_dx")
    g_up1 = _matmul(h3, du1, "tn", F32, "mm_up1_dw")
    (dx2, dmix1), (g_mix_post1, g_mlp_pre1) = _rowwise_bwd(
        "res_a1_bwd", _fn_res_norm, [full(x2), full(mix1)], [row(mix_post, 1), row(mlp_pre, 1)], [dx3, dh3], [F32, F32])
    do_sb = _matmul(dmix1, w_o, "nt", F32, "mm_sb_out_dx")
    g_o = _matmul(o_sb, dmix1, "tn", F32, "mm_sb_out_dw")
    dqp, dkp, dvp = _sb_bwd(qp, kp, vp, rt, do_sb)
    dhq = _matmul(dqp, w_q, "nt", F32, "mm_q_dx")
    g_q = _matmul(hq, dqp, "tn", F32, "mm_q_dw")
    dkv = jnp.concatenate([dkp, dvp], axis=1)
    dhkv = _matmul(dkv, w_kv, "nt", F32, "mm_kv_dx")
    g_kv = _matmul(hkv, dkv, "tn", F32, "mm_kv_dw")
    (dx1, dd0), (g_mlp_post0, g_kv_gain, g_mix_pre1) = _rowwise_bwd(
        "res_b0_bwd", _fn_res_norm2, [full(x1), full(d0)], [row(mlp_post, 0), kv_gain, row(mix_pre, 1)], [dx2, dhkv, dhq], [F32, F32])
    da0 = _matmul(dd0, w_down[0], "nt", F32, "mm_down0_dx")
    g_down0 = _matmul(a0, dd0, "tn", F32, "mm_down0_dw")
    (du0,), _ = _rowwise_bwd("relu2_0_bwd", _fn_relu2, [full(u0)], [], [da0], [BF16])
    dh1 = _matmul(du0, w_up[0], "nt", F32, "mm_up0_dx")
    g_up0 = _matmul(h1, du0, "tn", F32, "mm_up0_dw")
    (dx0, dmix0), (g_mix_post0, g_mlp_pre0) = _rowwise_bwd(
        "res_a0_bwd", _fn_res_norm, [full(x), full(mix0)], [row(mix_post, 0), row(mlp_pre, 0)], [dx1, dh1], [F32, F32])
    don = _matmul(dmix0, w_out, "nt", F32, "mm_gdn_out_dx")
    g_out = _matmul(on, dmix0, "tn", F32, "mm_gdn_out_dw")
    (do_gdn, dgate), (g_og,) = _rowwise_bwd("out_norm_bwd", _fn_outnorm, [full(o_gdn), (qkvg, d, 3)], [og], [don], [F32, F32])
    dpw, dpu, dpqg, dpkg, dpattn, dpgl = _gdn_scan_bwd(pw, pu, pqg, pkg, pattn, pgl, states, do_gdn)
    dgq, dgk, dgv, dbeta, dgdec = _gdn_prep_bwd(gq, gk, gv, beta, gdec, dpw, dpu, dpqg, dpkg, dpattn, dpgl)
    (dba,), (g_al, g_dtb) = _rowwise_bwd("gates_bwd", _fn_gates, [full(ba)], [al, dtb], [dbeta, dgdec], [F32])
    (dcq,), _ = _rowwise_bwd("post_q_bwd", _fn_post_q, [(conv, d, 0)], [], [dgq], [F32])
    (dck,), _ = _rowwise_bwd("post_k_bwd", _fn_post_k, [(conv, d, 1)], [], [dgk], [F32])
    (dcv,), _ = _rowwise_bwd("post_v_bwd", _fn_post_v, [(conv, d, 2)], [], [dgv], [F32])
    dconv = jnp.concatenate([dcq, dck, dcv], axis=1)
    dqkv, g_conv = _conv_bwd(dconv, qkvg, conv_w)
    dqkvg = jnp.concatenate([dqkv, dgate], axis=1)
    dh0 = _matmul(dqkvg, w_qkvg, "nt", F32, "mm_gdn_in_dx")
    dh0b = _matmul(dba, w_ba, "nt", F32, "mm_gdn_ba_dx", tk=LANES)
    g_qkvg = _matmul(h0, dqkvg, "tn", F32, "mm_gdn_in_dw")
    g_ba = _matmul(h0, dba, "tn", F32, "mm_gdn_ba_dw")
    (dx0b,), (g_mix_pre0,) = _rowwise_bwd("norm_in_bwd", lambda xx, gg: _fn_norm(xx, gg), [full(x)], [row(mix_pre, 0)], [dh0 + dh0b], [F32])
    grad_x = _rowwise("grad_x_sum", lambda a, b: (a + b,), [full(dx0), full(dx0b)], [], [(d, F32)])[0]

    grads = dict(
        mix_pre_gain=jnp.concatenate([g_mix_pre0, g_mix_pre1], axis=0),
        mix_post_gain=jnp.concatenate([g_mix_post0, g_mix_post1], axis=0),
        mlp_pre_gain=jnp.concatenate([g_mlp_pre0, g_mlp_pre1], axis=0),
        mlp_post_gain=jnp.concatenate([g_mlp_post0, g_mlp_post1], axis=0),
        mlp_w_up=jnp.stack([g_up0, g_up1]),
        mlp_w_down=jnp.stack([g_down0, g_down1]),
        gdn_w_in=jnp.concatenate([g_qkvg, g_ba[:, :2 * HEADS]], axis=1)[None],
        gdn_conv_w=g_conv[None, :CONV_K],
        gdn_a_log=g_al[:, HEADS:2 * HEADS],
        gdn_dt_bias=g_dtb[:, HEADS:2 * HEADS],
        gdn_out_gain=jnp.sum(g_og.reshape(HEADS, HEAD_DIM), axis=0, keepdims=True),
        gdn_w_out=g_out[None],
        kv_gain=g_kv_gain[0],
        w_kv=g_kv,
        sb_w_q=g_q[None],
        sb_w_o=g_o[None],
    )
    return loss, grad_x, grads


N_DEV = 8
N_CHIPS = 4
PACK_COLS = 1024
PACK_ROW_TILE = 256

_HBM = pl.BlockSpec(memory_space=pltpu.HBM)


def _place():
    return lax.axis_index("x"), lax.axis_index("y"), lax.axis_index("c")


def _other_chips(x, y):
    return [(1 - x, y), (x, 1 - y), (1 - x, 1 - y)]


def _remote(src, dst, send_sem, recv_sem, to):
    return pltpu.make_async_remote_copy(src_ref=src, dst_ref=dst, send_sem=send_sem, recv_sem=recv_sem, device_id=to, device_id_type=MESH)


def _gather8(v, name):
    rows, cols = v.shape

    def body(v_ref, out_ref, sum_ref, send_sems, recv_sems, local_sem):
        x, y, c = _place()
        me, sibling = (x, y, c), (x, y, 1 - c)
        chips = _other_chips(x, y)

        def blk(px, py, pc):
            return out_ref.at[pl.ds((4 * px + 2 * py + pc) * rows, rows), :]

        def copy(k, block, to, src=None):
            return _remote(blk(*block) if src is None else src, blk(*block), send_sems.at[k], recv_sems.at[k], to)

        mine = pltpu.make_async_copy(v_ref, blk(*me), local_sem)
        mine.start()
        first = [copy(0, me, sibling, src=v_ref)] + [copy(1 + j, me, (*chip, c), src=v_ref) for j, chip in enumerate(chips)]
        for cp in first:
            cp.start()
        passed = [copy(4 + j, (*chip, c), sibling) for j, chip in enumerate(chips)]
        for j, chip in enumerate(chips):
            copy(1 + j, (*chip, c), me).wait_recv()
            passed[j].start()
        copy(0, sibling, me).wait_recv()
        for j, chip in enumerate(chips):
            copy(4 + j, (*chip, 1 - c), me).wait_recv()
        for cp in first + passed:
            cp.wait_send()
        mine.wait()
        acc = out_ref[pl.ds(0, rows), :]
        for dev in range(1, N_DEV):
            acc = acc + out_ref[pl.ds(dev * rows, rows), :]
        sum_ref[...] = acc

    vm = pl.BlockSpec(memory_space=pltpu.VMEM)
    return pl.pallas_call(
        body,
        name=name,
        out_shape=[jax.ShapeDtypeStruct((N_DEV * rows, cols), v.dtype), jax.ShapeDtypeStruct((rows, cols), v.dtype)],
        in_specs=[vm],
        out_specs=[vm, vm],
        scratch_shapes=[pltpu.SemaphoreType.DMA((7,)), pltpu.SemaphoreType.DMA((7,)), pltpu.SemaphoreType.DMA],
    )(v)


def _gather_weights(wp):
    _, r, cols = wp.shape

    def body(w_ref, out_ref, send_sems, recv_sems, fsend_sems, frecv_sems, local_sem):
        x, y, c = _place()
        chips = _other_chips(x, y)
        s_me = 2 * x + y
        mine = pltpu.make_async_copy(w_ref, out_ref.at[s_me], local_sem)
        mine.start()
        first = [_remote(w_ref.at[c], out_ref.at[s_me, c], send_sems.at[j], recv_sems.at[j], (px, py, c)) for j, (px, py) in enumerate(chips)]
        for cp in first:
            cp.start()
        passed = []
        for j, (px, py) in enumerate(chips):
            half = out_ref.at[2 * px + py, c]
            _remote(half, half, send_sems.at[j], recv_sems.at[j], (px, py, c)).wait_recv()
            fwd = _remote(half, half, fsend_sems.at[j], frecv_sems.at[j], (x, y, 1 - c))
            fwd.start()
            passed.append(fwd)
        for j, (px, py) in enumerate(chips):
            half = out_ref.at[2 * px + py, 1 - c]
            _remote(half, half, fsend_sems.at[j], frecv_sems.at[j], (x, y, 1 - c)).wait_recv()
        for cp in first + passed:
            cp.wait_send()
        mine.wait()

    return pl.pallas_call(
        body,
        name="gather_weights",
        out_shape=jax.ShapeDtypeStruct((N_CHIPS, 2, r, cols), wp.dtype),
        in_specs=[_HBM],
        out_specs=_HBM,
        scratch_shapes=[pltpu.SemaphoreType.DMA((3,))] * 4 + [pltpu.SemaphoreType.DMA],
    )(wp)


def _swap_halves(g):
    _, n, cols = g.shape

    def body(g_ref, a_ref, send_sem, recv_sem):
        x, y, c = _place()
        cp = _remote(g_ref.at[1 - c], a_ref, send_sem, recv_sem, (x, y, 1 - c))
        cp.start()
        cp.wait()

    return pl.pallas_call(
        body,
        name="grads_to_sibling",
        out_shape=jax.ShapeDtypeStruct((n, cols), g.dtype),
        in_specs=[_HBM],
        out_specs=_HBM,
        scratch_shapes=[pltpu.SemaphoreType.DMA, pltpu.SemaphoreType.DMA],
    )(g)


def _scatter_to_chips(p):
    _, r, cols = p.shape

    def body(p_ref, b_ref, send_sems, recv_sems):
        x, y, c = _place()
        cps = [_remote(p_ref.at[2 * px + py], b_ref.at[j], send_sems.at[j], recv_sems.at[j], (px, py, c)) for j, (px, py) in enumerate(_other_chips(x, y))]
        for cp in cps:
            cp.start()
        for cp in cps:
            cp.wait()

    return pl.pallas_call(
        body,
        name="grads_to_chips",
        out_shape=jax.ShapeDtypeStruct((3, r, cols), p.dtype),
        in_specs=[_HBM],
        out_specs=_HBM,
        scratch_shapes=[pltpu.SemaphoreType.DMA((3,)), pltpu.SemaphoreType.DMA((3,))],
    )(p)


def _share_halves(q):
    r, cols = q.shape

    def body(q_ref, out_ref, send_sem, recv_sem, local_sem):
        x, y, c = _place()
        mine = pltpu.make_async_copy(q_ref, out_ref.at[c], local_sem)
        mine.start()
        cp = _remote(q_ref, out_ref.at[c], send_sem, recv_sem, (x, y, 1 - c))
        cp.start()
        _remote(q_ref, out_ref.at[1 - c], send_sem, recv_sem, (x, y, 1 - c)).wait_recv()
        cp.wait_send()
        mine.wait()

    return pl.pallas_call(
        body,
        name="grads_share",
        out_shape=jax.ShapeDtypeStruct((2, r, cols), q.dtype),
        in_specs=[_HBM],
        out_specs=_HBM,
        scratch_shapes=[pltpu.SemaphoreType.DMA, pltpu.SemaphoreType.DMA, pltpu.SemaphoreType.DMA],
    )(q)


_BIG = (
    ("mlp_w_up", (2, 1024, 1024), "cols"),
    ("mlp_w_down", (2, 1024, 1024), "rows"),
    ("gdn_w_in", (1, 1024, 1028), "cols"),
    ("gdn_w_out", (1, 256, 1024), "rows"),
    ("w_kv", (1024, 512), "cols"),
    ("sb_w_q", (1, 256, 1024), "rows"),
    ("sb_w_o", (1, 256, 1024), "rows"),
)


def _numel(shape):
    n = 1
    for s in shape:
        n *= s
    return n


_PACK_LEN = sum(_numel(s) for _, s, _ in _BIG)
_PACK_ROWS = -(-_PACK_LEN // (2 * PACK_COLS * PACK_ROW_TILE)) * PACK_ROW_TILE
_PACK_PAD = 2 * _PACK_ROWS * PACK_COLS - _PACK_LEN


def _pack_shards(shards, dtype):
    flat = jnp.concatenate([shards[n].astype(dtype).reshape(-1) for n, _, _ in _BIG] + [jnp.zeros((_PACK_PAD,), dtype)])
    return flat.reshape(2, _PACK_ROWS, PACK_COLS)


def _unpack_shards(packed):
    flat = packed.reshape(-1)
    out, off = {}, 0
    for n, shape, _ in _BIG:
        out[n] = flat[off:off + _numel(shape)].reshape(shape)
        off += _numel(shape)
    return out


def _join(stacked, how):
    nd = stacked.ndim - 1
    ax = nd - 1 if how == "cols" else nd - 2
    moved = jnp.moveaxis(stacked, 0, ax)
    shape = list(stacked.shape[1:])
    shape[ax] *= N_CHIPS
    return moved.reshape(shape)


def _split(full, shard_shape, how):
    nd = len(shard_shape)
    ax = nd - 1 if how == "cols" else nd - 2
    shape = list(shard_shape)
    shape.insert(ax, N_CHIPS)
    return jnp.moveaxis(full.reshape(shape), ax, 0)


def _unpack_full(gathered):
    flat = gathered.reshape(N_CHIPS, -1)
    out, off = {}, 0
    for n, shape, how in _BIG:
        out[n] = _join(flat[:, off:off + _numel(shape)].reshape((N_CHIPS,) + shape), how)
        off += _numel(shape)
    return out


def _pack_full(full):
    flat = jnp.concatenate([_split(full[n], shape, how).reshape(N_CHIPS, -1) for n, shape, how in _BIG] + [jnp.zeros((N_CHIPS, _PACK_PAD), F32)], axis=1)
    return jnp.swapaxes(flat.reshape(N_CHIPS, 2, _PACK_ROWS, PACK_COLS), 0, 1).reshape(2, N_CHIPS * _PACK_ROWS, PACK_COLS)


_SMALL = (
    ("mix_pre_gain", (2, 1024)),
    ("mix_post_gain", (2, 1024)),
    ("mlp_pre_gain", (2, 1024)),
    ("mlp_post_gain", (2, 1024)),
    ("kv_gain", (1024,)),
    ("gdn_out_gain", (1, 128)),
    ("gdn_a_log", (1, 8)),
    ("gdn_dt_bias", (1, 8)),
    ("gdn_conv_w", (1, 4, 3072)),
    ("loss", ()),
)


def _rows_of(shape):
    return -(-_numel(shape) // LANES)


_SMALL_ROWS = -(-sum(_rows_of(s) for _, s in _SMALL) // 8) * 8


def _pack_small(vals):
    parts = []
    for n, shape in _SMALL:
        flat = vals[n].reshape(-1)
        parts.append(jnp.pad(flat, (0, _rows_of(shape) * LANES - flat.shape[0])))
    flat = jnp.concatenate(parts)
    return jnp.pad(flat, (0, _SMALL_ROWS * LANES - flat.shape[0])).reshape(_SMALL_ROWS, LANES)


def _unpack_small(packed):
    flat = packed.reshape(-1)
    out, off = {}, 0
    for n, shape in _SMALL:
        out[n] = flat[off:off + _numel(shape)].reshape(shape)
        off += _rows_of(shape) * LANES
    return out


_WEIGHTS = ("mix_pre_gain", "mix_post_gain", "mlp_pre_gain", "mlp_post_gain", "mlp_w_up", "mlp_w_down", "gdn_w_in", "gdn_conv_w",
            "gdn_a_log", "gdn_dt_bias", "gdn_out_gain", "gdn_w_out", "kv_gain", "w_kv", "sb_w_q", "sb_w_o")


def _as2d(a):
    return a.reshape(1, -1) if a.ndim <= 1 else a.reshape(-1, a.shape[-1])


def kernel(x, mix_pre_gain, mix_post_gain, mlp_pre_gain, mlp_post_gain, mlp_w_up, mlp_w_down, gdn_w_in, gdn_conv_w, gdn_a_log, gdn_dt_bias, gdn_out_gain, gdn_w_out, kv_gain, w_kv, sb_w_q, sb_w_o, loss_target, m_mix_pre_gain, m_mix_post_gain, m_mlp_pre_gain, m_mlp_post_gain, m_mlp_w_up, m_mlp_w_down, m_gdn_w_in, m_gdn_conv_w, m_gdn_a_log, m_gdn_dt_bias, m_gdn_out_gain, m_gdn_w_out, m_kv_gain, m_w_kv, m_sb_w_q, m_sb_w_o, v_mix_pre_gain, v_mix_post_gain, v_mlp_pre_gain, v_mlp_post_gain, v_mlp_w_up, v_mlp_w_down, v_gdn_w_in, v_gdn_conv_w, v_gdn_a_log, v_gdn_dt_bias, v_gdn_out_gain, v_gdn_w_out, v_kv_gain, v_w_kv, v_sb_w_q, v_sb_w_o):
    w = dict(mix_pre_gain=mix_pre_gain, mix_post_gain=mix_post_gain, mlp_pre_gain=mlp_pre_gain, mlp_post_gain=mlp_post_gain, mlp_w_up=mlp_w_up, mlp_w_down=mlp_w_down, gdn_w_in=gdn_w_in, gdn_conv_w=gdn_conv_w, gdn_a_log=gdn_a_log, gdn_dt_bias=gdn_dt_bias, gdn_out_gain=gdn_out_gain, gdn_w_out=gdn_w_out, kv_gain=kv_gain, w_kv=w_kv, sb_w_q=sb_w_q, sb_w_o=sb_w_o)
    m = dict(mix_pre_gain=m_mix_pre_gain, mix_post_gain=m_mix_post_gain, mlp_pre_gain=m_mlp_pre_gain, mlp_post_gain=m_mlp_post_gain, mlp_w_up=m_mlp_w_up, mlp_w_down=m_mlp_w_down, gdn_w_in=m_gdn_w_in, gdn_conv_w=m_gdn_conv_w, gdn_a_log=m_gdn_a_log, gdn_dt_bias=m_gdn_dt_bias, gdn_out_gain=m_gdn_out_gain, gdn_w_out=m_gdn_w_out, kv_gain=m_kv_gain, w_kv=m_w_kv, sb_w_q=m_sb_w_q, sb_w_o=m_sb_w_o)
    v = dict(mix_pre_gain=v_mix_pre_gain, mix_post_gain=v_mix_post_gain, mlp_pre_gain=v_mlp_pre_gain, mlp_post_gain=v_mlp_post_gain, mlp_w_up=v_mlp_w_up, mlp_w_down=v_mlp_w_down, gdn_w_in=v_gdn_w_in, gdn_conv_w=v_gdn_conv_w, gdn_a_log=v_gdn_a_log, gdn_dt_bias=v_gdn_dt_bias, gdn_out_gain=v_gdn_out_gain, gdn_w_out=v_gdn_w_out, kv_gain=v_kv_gain, w_kv=v_w_kv, sb_w_q=v_sb_w_q, sb_w_o=v_sb_w_o)
    cx, cy, cc = _place()
    chip = 2 * cx + cy
    conv_cols = gdn_conv_w.shape[-1]

    full = _unpack_full(_gather_weights(_pack_shards(w, BF16)))
    conv_rows = jnp.pad(gdn_conv_w[0], ((0, 8 - CONV_K), (0, 0))).reshape(-1, LANES)
    conv_all, _ = _gather8(conv_rows, "gather_conv_w")
    conv_all = conv_all.reshape(N_CHIPS, 2, 8, conv_cols)[:, 0, :CONV_K]
    conv_full = jnp.swapaxes(conv_all, 0, 1).reshape(CONV_K, N_CHIPS * conv_cols)

    w_in = full["gdn_w_in"][0]
    wts = (w_in[:, :4 * HEADS * HEAD_DIM], jnp.pad(w_in[:, 4 * HEADS * HEAD_DIM:], ((0, 0), (0, LANES - 2 * HEADS))), full["gdn_w_out"][0], full["w_kv"],
           full["sb_w_q"][0], full["sb_w_o"][0], full["mlp_w_up"], full["mlp_w_down"])
    gains = (mix_pre_gain, mix_post_gain, mlp_pre_gain, mlp_post_gain, kv_gain[None])
    small = (conv_full, gdn_a_log, gdn_dt_bias, gdn_out_gain)
    loss_rows, grad_x, g_full = _local_step(x[0], loss_target[0], gains, wts, small)

    packed = _pack_full(g_full)
    from_sibling = _swap_halves(packed)
    own_half = lax.dynamic_index_in_dim(packed, cc, 0, keepdims=False)
    (partial,) = _rowwise("grads_add_sibling", lambda a, b: (a + b,), [(own_half, PACK_COLS, 0), (from_sibling, PACK_COLS, 0)], [], [(PACK_COLS, F32)], tm=PACK_ROW_TILE)
    partial = partial.reshape(N_CHIPS, _PACK_ROWS, PACK_COLS)
    from_chips = _scatter_to_chips(partial)
    mine = lax.dynamic_index_in_dim(partial, chip, 0, keepdims=False)
    (reduced,) = _rowwise("grads_add_chips", lambda a, b, c, d: (((a + b) + c) + d,),
                          [(mine, PACK_COLS, 0), (from_chips[0], PACK_COLS, 0), (from_chips[1], PACK_COLS, 0), (from_chips[2], PACK_COLS, 0)], [], [(PACK_COLS, F32)], tm=PACK_ROW_TILE)
    g_shard = _unpack_shards(_share_halves(reduced))

    g_small_local = {n: g_full[n] for n, _ in _SMALL if n != "loss"}
    g_small_local["loss"] = loss_rows[0, 0]
    _, small_sum = _gather8(_pack_small(g_small_local), "allreduce_small")
    g_small = _unpack_small(small_sum)
    loss = g_small.pop("loss")
    g_small["gdn_conv_w"] = lax.dynamic_slice_in_dim(g_small["gdn_conv_w"], chip * conv_cols, conv_cols, axis=2)

    grads = {**g_shard, **g_small}
    deltas, new_m, new_v = {}, {}, {}
    for n in _WEIGHTS:
        d2, m2, v2 = _adamw(_as2d(w[n]), _as2d(grads[n]), _as2d(m[n]), _as2d(v[n]), "adamw_" + n)
        deltas[n], new_m[n], new_v[n] = d2.reshape(w[n].shape), m2.reshape(w[n].shape), v2.reshape(w[n].shape)
    return (loss, grad_x[None], *[grads[n].reshape(w[n].shape) for n in _WEIGHTS], *[deltas[n] for n in _WEIGHTS],
            *[new_m[n] for n in _WEIGHTS], *[new_v[n] for n in _WEIGHTS])
```

```python
import functools

import jax
import jax.numpy as jnp
from jax import lax
from jax.experimental import pallas as pl
from jax.experimental.pallas import tpu as pltpu

F32, BF16 = jnp.float32, jnp.bfloat16
HI = lax.Precision.HIGHEST
MESH = pl.DeviceIdType.MESH

EPS = 1e-6
D_MODEL = 1024
HEADS = 8
HEAD_DIM = 128
CHUNK = 64
CHUNK_SHIFT = CHUNK.bit_length() - 1
CONV_K = 4
D_FF = 4096
QKV = 3 * HEADS * HEAD_DIM

ADAM_LR, ADAM_B1, ADAM_B2, ADAM_EPS, ADAM_WD, ADAM_STEP = 0.001, 0.9, 0.999, 1e-08, 0.01, 10

VMEM_LIMIT_BYTES = 48 * 1024 * 1024
LANES = 128

NN = ((1,), (0,))
NT = ((1,), (1,))
TN = ((0,), (0,))


def _dot(a, b, dims=NN, precision=None):
    return lax.dot_general(a, b, (dims, ((), ())), precision=precision, preferred_element_type=F32)


def _params(*sem):
    return pltpu.CompilerParams(dimension_semantics=sem, vmem_limit_bytes=VMEM_LIMIT_BYTES)


def _iota(shape, axis):
    return lax.broadcasted_iota(jnp.int32, shape, axis)


def _matmul(a, b, mode, out_dtype, name, tm=1024, tn=1024, tk=512, add=None):
    if mode == "nn":
        (m, k), (k2, n) = a.shape, b.shape
    elif mode == "nt":
        (m, k), (n, k2) = a.shape, b.shape
    else:
        (k, m), (k2, n) = a.shape, b.shape
    assert k == k2, (a.shape, b.shape, mode)
    tm, tn, tk = min(tm, m), min(tn, n), min(tk, k)
    assert m % tm == 0 and n % tn == 0 and k % tk == 0, (a.shape, b.shape, mode)
    nk = k // tk
    dims = {"nn": NN, "nt": NT, "tn": TN}[mode]
    has_add = add is not None

    def body(*refs):
        a_ref, b_ref = refs[:2]
        o_ref, acc_ref = refs[-2:]
        kk = pl.program_id(2)

        @pl.when(kk == 0)
        def _():
            acc_ref[...] = refs[2][...].astype(F32) if has_add else jnp.zeros_like(acc_ref)

        acc_ref[...] += _dot(a_ref[...].astype(BF16), b_ref[...].astype(BF16), dims)

        @pl.when(kk == nk - 1)
        def _():
            o_ref[...] = acc_ref[...].astype(o_ref.dtype)

    a_spec = pl.BlockSpec((tk, tm), lambda i, j, kk: (kk, i)) if mode == "tn" else pl.BlockSpec((tm, tk), lambda i, j, kk: (i, kk))
    b_spec = pl.BlockSpec((tn, tk), lambda i, j, kk: (j, kk)) if mode == "nt" else pl.BlockSpec((tk, tn), lambda i, j, kk: (kk, j))
    o_spec = pl.BlockSpec((tm, tn), lambda i, j, kk: (i, j))
    return pl.pallas_call(
        body,
        name=name,
        grid=(m // tm, n // tn, nk),
        in_specs=[a_spec, b_spec] + ([o_spec] if has_add else []),
        out_specs=o_spec,
        out_shape=jax.ShapeDtypeStruct((m, n), out_dtype),
        scratch_shapes=[pltpu.VMEM((tm, tn), F32)],
        compiler_params=_params("parallel", "parallel", "arbitrary"),
    )(a, b, *([add] if has_add else []))


def _row_specs(rows, tm):
    return [pl.BlockSpec((tm, w), lambda i, cb=cb: (i, cb)) for _, w, cb in rows]


def _full_spec(p):
    return pl.BlockSpec(p.shape, lambda i: (0,) * p.ndim)


def _rowwise(name, fn, rows, params, outs, tm=256):
    t = rows[0][0].shape[0]
    tm = min(tm, t)
    nr, npar = len(rows), len(params)

    def body(*refs):
        ins = [r[...].astype(F32) for r in refs[:nr]]
        ps = [p[...] for p in refs[nr:nr + npar]]
        res = fn(*ins, *ps)
        for o_ref, r in zip(refs[nr + npar:], res):
            o_ref[...] = r.astype(o_ref.dtype)

    return pl.pallas_call(
        body,
        name=name,
        grid=(t // tm,),
        in_specs=_row_specs(rows, tm) + [_full_spec(p) for p in params],
        out_specs=[pl.BlockSpec((tm, w), lambda i: (i, 0)) for w, _ in outs],
        out_shape=[jax.ShapeDtypeStruct((t, w), dt) for w, dt in outs],
        compiler_params=_params("parallel"),
    )(*[r[0] for r in rows], *params)


def _rowwise_bwd(name, fn, rows, params, cots, grad_dtypes, tm=256, joined=False):
    t = rows[0][0].shape[0]
    tm = min(tm, t)
    nr, npar, nc = len(rows), len(params), len(cots)
    want = [j for j, dt in enumerate(grad_dtypes) if dt is not None]
    widths = [rows[j][1] for j in want]
    n_row_outs = 1 if joined else len(want)

    def body(*refs):
        i = pl.program_id(0)
        ins = [r[...].astype(F32) for r in refs[:nr]]
        ps = [p[...] for p in refs[nr:nr + npar]]
        cs = tuple(c[...].astype(F32) for c in refs[nr + npar:nr + npar + nc])
        _, vjp = jax.vjp(fn, *ins, *ps)
        gs = vjp(cs)
        outs = refs[nr + npar + nc:]
        if joined:
            off = 0
            for j, w in zip(want, widths):
                outs[0][:, off:off + w] = gs[j].astype(outs[0].dtype)
                off += w
        else:
            for o_ref, j in zip(outs, want):
                o_ref[...] = gs[j].astype(o_ref.dtype)
        pg_refs = outs[n_row_outs:]

        @pl.when(i == 0)
        def _():
            for pg in pg_refs:
                pg[...] = jnp.zeros_like(pg)

        for pg, g in zip(pg_refs, gs[nr:]):
            pg[...] += g

    if joined:
        row_specs = [pl.BlockSpec((tm, sum(widths)), lambda i: (i, 0))]
        row_shapes = [jax.ShapeDtypeStruct((t, sum(widths)), grad_dtypes[want[0]])]
    else:
        row_specs = [pl.BlockSpec((tm, w), lambda i: (i, 0)) for w in widths]
        row_shapes = [jax.ShapeDtypeStruct((t, w), grad_dtypes[j]) for j, w in zip(want, widths)]
    res = pl.pallas_call(
        body,
        name=name,
        grid=(t // tm,),
        in_specs=_row_specs(rows, tm) + [_full_spec(p) for p in params] + [pl.BlockSpec((tm, c.shape[1]), lambda i: (i, 0)) for c in cots],
        out_specs=row_specs + [_full_spec(p) for p in params],
        out_shape=row_shapes + [jax.ShapeDtypeStruct(p.shape, F32) for p in params],
        compiler_params=_params("arbitrary"),
    )(*[r[0] for r in rows], *params, *cots)
    return res[:n_row_outs], res[n_row_outs:]


def _rms(x, g):
    return x * lax.rsqrt(jnp.mean(x * x, axis=-1, keepdims=True) + EPS) * g


def _sigmoid(x):
    return 1.0 / (1.0 + jnp.exp(-x))


def _softplus(x):
    return jnp.maximum(x, 0.0) + jnp.log1p(jnp.exp(-jnp.abs(x)))


def _two_pass(x, m):
    hi = x.astype(BF16)
    lo = (x - hi.astype(F32)).astype(BF16)
    return _dot(hi, m) + _dot(lo, m)


def _head_sum_impl(x):
    w = HEADS * HEAD_DIM
    fold = jnp.where((_iota((w, LANES), 0) >> 7) == _iota((w, LANES), 1), 1.0, 0.0).astype(BF16)
    spread = jnp.where(_iota((LANES, w), 0) == (_iota((LANES, w), 1) >> 7), 1.0, 0.0).astype(BF16)
    return _two_pass(_two_pass(x, fold), spread)


@jax.custom_vjp
def _head_sum(x):
    return _head_sum_impl(x)


_head_sum.defvjp(lambda x: (_head_sum_impl(x), None), lambda _, g: (_head_sum_impl(g),))


def _fn_norm(x, g):
    return (_rms(x, g),)


def _fn_gates(ba, al, dt):
    col = _iota((1, LANES), 1)
    g = jnp.where((col >= HEADS) & (col < 2 * HEADS), -jnp.exp(al) * _softplus(ba + dt), 0.0)
    rows = ba.shape[0]
    r, c = _iota((rows, rows), 0), _iota((rows, rows), 1)
    same = (r >> CHUNK_SHIFT) == (c >> CHUNK_SHIFT)
    gc = _dot(jnp.where(same & (r >= c), 1.0, 0.0), g, precision=HI)
    gtot = _dot(jnp.where(same, 1.0, 0.0), g, precision=HI)
    return _sigmoid(ba), gc, gtot


def _fn_post_q(c):
    s = c * _sigmoid(c)
    return (s * lax.rsqrt(_head_sum(s * s) + EPS) * (HEAD_DIM ** -0.5),)


def _fn_post_k(c):
    s = c * _sigmoid(c)
    return (s * lax.rsqrt(_head_sum(s * s) + EPS),)


def _fn_post_v(c):
    return (c * _sigmoid(c),)


def _fn_post(cq, ck, cv):
    return _fn_post_q(cq) + _fn_post_k(ck) + _fn_post_v(cv)


def _fn_outnorm(o, gate, og):
    y = o * lax.rsqrt(_head_sum(o * o) * (1.0 / HEAD_DIM) + EPS) * og
    return (y * (gate * _sigmoid(gate)),)


def _fn_res_norm(x, m, gp, gn):
    x1 = x + _rms(m, gp)
    return x1, _rms(x1, gn)


def _fn_res_norm2(x, m, gp, ga, gb):
    x1 = x + _rms(m, gp)
    return x1, _rms(x1, ga), _rms(x1, gb)


def _fn_relu2(u):
    r = jnp.maximum(u, 0.0)
    return (r * r,)


def _loss_call(x3, d1, tgt, g, tm=256):
    t, d = x3.shape
    tm = min(tm, t)

    def body(x_ref, d_ref, t_ref, g_ref, loss_ref, dx_ref, dd_ref, dg_ref):
        i = pl.program_id(0)
        y, vjp = jax.vjp(lambda x, dd, gg: x + _rms(dd, gg), x_ref[...], d_ref[...], g_ref[...])
        err = y - t_ref[...]
        lrow = 0.5 * jnp.mean(err * err, axis=-1, keepdims=True)
        dx, dd, dg = vjp(err * (1.0 / d))
        dx_ref[...] = dx
        dd_ref[...] = dd

        @pl.when(i == 0)
        def _():
            loss_ref[...] = jnp.zeros_like(loss_ref)
            dg_ref[...] = jnp.zeros_like(dg_ref)

        loss_ref[...] += jnp.broadcast_to(jnp.sum(lrow, axis=0, keepdims=True), loss_ref.shape)
        dg_ref[...] += dg

    row = pl.BlockSpec((tm, d), lambda i: (i, 0))
    return pl.pallas_call(
        body,
        name="loss_head",
        grid=(t // tm,),
        in_specs=[row, row, row, _full_spec(g)],
        out_specs=[pl.BlockSpec((8, LANES), lambda i: (0, 0)), row, row, _full_spec(g)],
        out_shape=[jax.ShapeDtypeStruct((8, LANES), F32), jax.ShapeDtypeStruct((t, d), F32), jax.ShapeDtypeStruct((t, d), F32), jax.ShapeDtypeStruct(g.shape, F32)],
        compiler_params=_params("arbitrary"),
    )(x3, d1, tgt, g)


HALO = 8


def _conv_fwd(qkvg, conv_w, tm=256):
    t = qkvg.shape[0]
    tm = min(tm, t)

    def body(cur_ref, prev_ref, w_ref, o_ref, buf):
        i = pl.program_id(0)
        buf[0:HALO, :] = jnp.where(i > 0, prev_ref[...], 0.0)
        buf[HALO:, :] = cur_ref[...]
        acc = buf[pl.ds(HALO - CONV_K + 1, tm), :] * w_ref[pl.ds(0, 1), :]
        for j in range(1, CONV_K):
            acc = acc + buf[pl.ds(HALO - CONV_K + 1 + j, tm), :] * w_ref[pl.ds(j, 1), :]
        o_ref[...] = acc

    return pl.pallas_call(
        body,
        name="conv_fwd",
        grid=(t // tm,),
        in_specs=[
            pl.BlockSpec((tm, QKV), lambda i: (i, 0)),
            pl.BlockSpec((HALO, QKV), lambda i: (jnp.maximum(i * (tm // HALO) - 1, 0), 0)),
            pl.BlockSpec((CONV_K, QKV), lambda i: (0, 0)),
        ],
        out_specs=pl.BlockSpec((tm, QKV), lambda i: (i, 0)),
        out_shape=jax.ShapeDtypeStruct((t, QKV), F32),
        scratch_shapes=[pltpu.VMEM((tm + HALO, QKV), F32)],
        compiler_params=_params("parallel"),
    )(qkvg, qkvg, conv_w)


def _conv_bwd(dc, dgate, qkvg, conv_w, tm=256):
    t = dc.shape[0]
    tm = min(tm, t)
    n = t // tm
    wg = dgate.shape[1]

    def body(dc_ref, dcn_ref, dgate_ref, x_ref, xp_ref, w_ref, dx_ref, dw_ref, bufd, bufx):
        i = pl.program_id(0)
        bufd[0:tm, :] = dc_ref[...]
        bufd[tm:, :] = jnp.where(i < n - 1, dcn_ref[...], 0.0)
        bufx[0:HALO, :] = jnp.where(i > 0, xp_ref[...], 0.0)
        bufx[HALO:, :] = x_ref[...]

        @pl.when(i == 0)
        def _():
            dw_ref[...] = jnp.zeros_like(dw_ref)

        dcv = dc_ref[...]
        acc = bufd[pl.ds(CONV_K - 1, tm), :] * w_ref[pl.ds(0, 1), :]
        for j in range(1, CONV_K):
            acc = acc + bufd[pl.ds(CONV_K - 1 - j, tm), :] * w_ref[pl.ds(j, 1), :]
        dx_ref[:, 0:QKV] = acc
        dx_ref[:, QKV:] = dgate_ref[...]
        for j in range(CONV_K):
            dw_ref[pl.ds(j, 1), :] += jnp.sum(dcv * bufx[pl.ds(HALO - CONV_K + 1 + j, tm), :], axis=0, keepdims=True)

    return pl.pallas_call(
        body,
        name="conv_bwd",
        grid=(n,),
        in_specs=[
            pl.BlockSpec((tm, QKV), lambda i: (i, 0)),
            pl.BlockSpec((HALO, QKV), lambda i: (jnp.minimum((i + 1) * (tm // HALO), t // HALO - 1), 0)),
            pl.BlockSpec((tm, wg), lambda i: (i, 0)),
            pl.BlockSpec((tm, QKV), lambda i: (i, 0)),
            pl.BlockSpec((HALO, QKV), lambda i: (jnp.maximum(i * (tm // HALO) - 1, 0), 0)),
            pl.BlockSpec((CONV_K, QKV), lambda i: (0, 0)),
        ],
        out_specs=[pl.BlockSpec((tm, QKV + wg), lambda i: (i, 0)), pl.BlockSpec((HALO, QKV), lambda i: (0, 0))],
        out_shape=[jax.ShapeDtypeStruct((t, QKV + wg), F32), jax.ShapeDtypeStruct((HALO, QKV), F32)],
        scratch_shapes=[pltpu.VMEM((tm + HALO, QKV), F32), pltpu.VMEM((tm + HALO, QKV), F32)],
        compiler_params=_params("arbitrary"),
    )(dc, dc, dgate, qkvg, qkvg, conv_w)


PREP_CHUNKS = 4


def _hi_lo(x):
    hi = x.astype(BF16)
    return hi, (x - hi.astype(F32)).astype(BF16)


def _mm3(a, b, dims=NN):
    (ah, al), (bh, bl) = _hi_lo(a), _hi_lo(b)
    return _dot(ah, bh, dims) + (_dot(ah, bl, dims) + _dot(al, bh, dims))


def _neumann(lower):
    c = lower.shape[0]
    p = -lower
    tmat = jnp.where(_iota((c, c), 0) == _iota((c, c), 1), 1.0, 0.0) + p
    for _ in range(CHUNK_SHIFT - 1):
        p = _mm3(p, p)
        tmat = tmat + _mm3(tmat, p)
    return tmat


def _inv_cotangent(tmat, dt):
    return -_mm3(_mm3(tmat, dt, TN), tmat, NT)


@jax.custom_vjp
def _tri_inv(lower):
    return _neumann(lower)


def _tri_inv_fwd(lower):
    tmat = _neumann(lower)
    return tmat, tmat


_tri_inv.defvjp(_tri_inv_fwd, lambda tmat, dt: (_inv_cotangent(tmat, dt),))


@jax.custom_vjp
def _tri_inv_known(lower, tmat):
    return tmat


_tri_inv_known.defvjp(lambda lower, tmat: (tmat, tmat), lambda tmat, dt: (_inv_cotangent(tmat, dt), jnp.zeros_like(tmat)))


def _prep_chunk(q, k, v, b, gc, gt, gcr, tmat=None):
    c = CHUNK
    r, col = _iota((c, c), 0), _iota((c, c), 1)
    incl, strict = r >= col, r > col
    decay = jnp.where(incl, jnp.exp(jnp.where(incl, gc - gcr, 0.0)), 0.0)
    kb = k * b
    kbf = k.astype(BF16)
    lower = jnp.where(strict, _dot(kb.astype(BF16), kbf, NT) * decay, 0.0)
    tmat = _tri_inv(lower) if tmat is None else _tri_inv_known(lower, tmat)
    tb = tmat.astype(BF16)
    egc = jnp.exp(gc)
    w = _dot(tb, (kb * egc).astype(BF16))
    u = _dot(tb, (v * b).astype(BF16))
    attn = _dot(q.astype(BF16), kbf, NT) * decay
    qg = q * egc
    kg = k * jnp.exp(gt - gc)
    gl = jnp.broadcast_to(jnp.exp(jnp.mean(gt.reshape(c // 8, 8, 1), axis=0)), (8, HEAD_DIM))
    return w, u, qg, kg, attn, gl, tmat


def _head_block(rows):
    return pl.BlockSpec((rows, HEAD_DIM), lambda h, n: (n, h))


def _prep_specs(rows, gch):
    col = pl.BlockSpec((1, rows, 1), lambda h, n: (h, n, 0))
    gcrow = pl.BlockSpec((1, gch, 1, CHUNK), lambda h, n: (h, n, 0, 0))
    square = pl.BlockSpec((1, rows, CHUNK), lambda h, n: (h, n, 0))
    gl = pl.BlockSpec((1, gch * 8, HEAD_DIM), lambda h, n: (h, n, 0))
    return _head_block(rows), col, gcrow, square, gl


def _gdn_prep(q, k, v, beta, gc, gt, gcr):
    t = q.shape[0]
    gch = min(PREP_CHUNKS, t // CHUNK)
    rows = gch * CHUNK

    def body(q_ref, k_ref, v_ref, b_ref, gc_ref, gt_ref, gcr_ref, w_ref, u_ref, qg_ref, kg_ref, at_ref, gl_ref, tm_ref):
        for c in range(gch):
            sl = pl.ds(c * CHUNK, CHUNK)
            w, u, qg, kg, attn, gl, tmat = _prep_chunk(q_ref[sl, :], k_ref[sl, :], v_ref[sl, :], b_ref[0, sl, :], gc_ref[0, sl, :], gt_ref[0, sl, :], gcr_ref[0, c])
            w_ref[sl, :] = w.astype(BF16)
            u_ref[sl, :] = u
            qg_ref[sl, :] = qg.astype(BF16)
            kg_ref[sl, :] = kg.astype(BF16)
            at_ref[0, sl, :] = attn.astype(BF16)
            gl_ref[0, pl.ds(c * 8, 8), :] = gl
            tm_ref[0, sl, :] = tmat

    hb, col, gcrow, square, glb = _prep_specs(rows, gch)
    wide = HEADS * HEAD_DIM
    return pl.pallas_call(
        body,
        name="gdn_prep",
        grid=(HEADS, t // rows),
        in_specs=[hb, hb, hb, col, col, col, gcrow],
        out_specs=[hb, hb, hb, hb, square, glb, square],
        out_shape=[
            jax.ShapeDtypeStruct((t, wide), BF16),
            jax.ShapeDtypeStruct((t, wide), F32),
            jax.ShapeDtypeStruct((t, wide), BF16),
            jax.ShapeDtypeStruct((t, wide), BF16),
            jax.ShapeDtypeStruct((HEADS, t, CHUNK), BF16),
            jax.ShapeDtypeStruct((HEADS, t // CHUNK * 8, HEAD_DIM), F32),
            jax.ShapeDtypeStruct((HEADS, t, CHUNK), F32),
        ],
        compiler_params=_params("parallel", "parallel"),
    )(q, k, v, beta, gc, gt, gcr)


def _gdn_prep_bwd(q, k, v, beta, gc, gt, gcr, tmat, dw, du, dqg, dkg, dattn, dgl):
    t = q.shape[0]
    gch = min(PREP_CHUNKS, t // CHUNK)
    rows = gch * CHUNK

    def body(q_ref, k_ref, v_ref, b_ref, gc_ref, gt_ref, gcr_ref, tm_ref, dw_ref, du_ref, dqg_ref, dkg_ref, dat_ref, dgl_ref,
             dq_ref, dk_ref, dv_ref, db_ref, dgc_ref, dgt_ref, dgcr_ref):
        for c in range(gch):
            sl = pl.ds(c * CHUNK, CHUNK)
            known = tm_ref[0, sl, :]
            _, vjp = jax.vjp(lambda *a: _prep_chunk(*a, tmat=known)[:6],
                             q_ref[sl, :], k_ref[sl, :], v_ref[sl, :], b_ref[0, sl, :], gc_ref[0, sl, :], gt_ref[0, sl, :], gcr_ref[0, c])
            dq, dk, dv, db, dgc, dgt, dgcr = vjp((dw_ref[sl, :], du_ref[sl, :], dqg_ref[sl, :], dkg_ref[sl, :], dat_ref[0, sl, :], dgl_ref[0, pl.ds(c * 8, 8), :]))
            dq_ref[sl, :] = dq
            dk_ref[sl, :] = dk
            dv_ref[sl, :] = dv
            db_ref[0, sl, :] = db
            dgc_ref[0, sl, :] = dgc
            dgt_ref[0, sl, :] = dgt
            dgcr_ref[0, c] = dgcr

    hb, col, gcrow, square, glb = _prep_specs(rows, gch)
    wide = HEADS * HEAD_DIM
    return pl.pallas_call(
        body,
        name="gdn_prep_bwd",
        grid=(HEADS, t // rows),
        in_specs=[hb, hb, hb, col, col, col, gcrow, square, hb, hb, hb, hb, square, glb],
        out_specs=[hb, hb, hb, col, col, col, gcrow],
        out_shape=[jax.ShapeDtypeStruct((t, wide), F32)] * 3 + [jax.ShapeDtypeStruct((HEADS, t, 1), F32)] * 3 + [jax.ShapeDtypeStruct((HEADS, t // CHUNK, 1, CHUNK), F32)],
        compiler_params=_params("parallel", "parallel"),
    )(q, k, v, beta, gc, gt, gcr, tmat, dw, du, dqg, dkg, dattn, dgl)


def _gdn_scan(w, u, qg, kg, attn, gl):
    t = w.shape[0]
    n = t // CHUNK
    wide = HEADS * HEAD_DIM

    def body(w_ref, u_ref, qg_ref, kg_ref, at_ref, gl_ref, o_ref, st_ref, s_ref):
        @pl.when(pl.program_id(0) == 0)
        def _():
            s_ref[...] = jnp.zeros_like(s_ref)

        for h in range(HEADS):
            hs = pl.ds(h * HEAD_DIM, HEAD_DIM)
            s = s_ref[h]
            sb = s.astype(BF16)
            st_ref[0, h] = s
            vn = u_ref[:, hs] - _dot(w_ref[:, hs], sb)
            vb = vn.astype(BF16)
            o_ref[:, hs] = _dot(qg_ref[:, hs], sb) + _dot(at_ref[h], vb)
            s_ref[h] = s * jnp.tile(gl_ref[h], (HEAD_DIM // 8, 1)) + _dot(kg_ref[:, hs], vb, TN)

    row = pl.BlockSpec((CHUNK, wide), lambda i: (i, 0))
    return pl.pallas_call(
        body,
        name="gdn_scan",
        grid=(n,),
        in_specs=[row, row, row, row, pl.BlockSpec((HEADS, CHUNK, CHUNK), lambda i: (0, i, 0)), pl.BlockSpec((HEADS, 8, HEAD_DIM), lambda i: (0, i, 0))],
        out_specs=[row, pl.BlockSpec((1, HEADS, HEAD_DIM, HEAD_DIM), lambda i: (i, 0, 0, 0))],
        out_shape=[jax.ShapeDtypeStruct((t, wide), F32), jax.ShapeDtypeStruct((n, HEADS, HEAD_DIM, HEAD_DIM), F32)],
        scratch_shapes=[pltpu.VMEM((HEADS, HEAD_DIM, HEAD_DIM), F32)],
        compiler_params=_params("arbitrary"),
    )(w, u, qg, kg, attn, gl)


def _gdn_scan_bwd(w, u, qg, kg, attn, gl, states, do):
    t = w.shape[0]
    n = t // CHUNK
    wide = HEADS * HEAD_DIM

    def body(w_ref, u_ref, qg_ref, kg_ref, at_ref, gl_ref, st_ref, do_ref, dw_ref, du_ref, dqg_ref, dkg_ref, dat_ref, dgl_ref, ds_ref):
        @pl.when(pl.program_id(0) == 0)
        def _():
            ds_ref[...] = jnp.zeros_like(ds_ref)

        for h in range(HEADS):
            hs = pl.ds(h * HEAD_DIM, HEAD_DIM)
            s = st_ref[0, h]
            sb = s.astype(BF16)
            wv, qgv, kgv, atv = w_ref[:, hs], qg_ref[:, hs], kg_ref[:, hs], at_ref[h]
            vb = (u_ref[:, hs] - _dot(wv, sb)).astype(BF16)
            dsn = ds_ref[h]
            dsb = dsn.astype(BF16)
            dob = do_ref[:, hs].astype(BF16)
            dvn = _dot(atv, dob, TN) + _dot(kgv, dsb)
            dvb = dvn.astype(BF16)
            dat_ref[h] = _dot(dob, vb, NT)
            dqg_ref[:, hs] = _dot(dob, sb, NT)
            dkg_ref[:, hs] = _dot(vb, dsb, NT)
            du_ref[:, hs] = dvn
            dw_ref[:, hs] = -_dot(dvb, sb, NT)
            dgl_ref[h] = jnp.sum((dsn * s).reshape(HEAD_DIM // 8, 8, HEAD_DIM), axis=0)
            ds_ref[h] = dsn * jnp.tile(gl_ref[h], (HEAD_DIM // 8, 1)) + _dot(qgv, dob, TN) - _dot(wv, dvb, TN)

    row = pl.BlockSpec((CHUNK, wide), lambda i: (n - 1 - i, 0))
    at = pl.BlockSpec((HEADS, CHUNK, CHUNK), lambda i: (0, n - 1 - i, 0))
    glb = pl.BlockSpec((HEADS, 8, HEAD_DIM), lambda i: (0, n - 1 - i, 0))
    return pl.pallas_call(
        body,
        name="gdn_scan_bwd",
        grid=(n,),
        in_specs=[row, row, row, row, at, glb, pl.BlockSpec((1, HEADS, HEAD_DIM, HEAD_DIM), lambda i: (n - 1 - i, 0, 0, 0)), row],
        out_specs=[row, row, row, row, at, glb],
        out_shape=[jax.ShapeDtypeStruct((t, wide), F32)] * 4 + [jax.ShapeDtypeStruct((HEADS, t, CHUNK), F32), jax.ShapeDtypeStruct((HEADS, n * 8, HEAD_DIM), F32)],
        scratch_shapes=[pltpu.VMEM((HEADS, HEAD_DIM, HEAD_DIM), F32)],
        compiler_params=_params("arbitrary"),
    )(w, u, qg, kg, attn, gl, states, do)


SB_Q = 512
SB_K = 256


def _sb_scores(q, k):
    z = _dot(q, k, NT) * (HEAD_DIM ** -0.5)
    e = jnp.exp(-jnp.abs(z))
    lb = jnp.minimum(z, 0.0) - jnp.log(1.0 + e)
    return z, e, lb, lb - z


def _tri(n, rel):
    return jnp.where(rel(_iota((n, n), 0), _iota((n, n), 1)), 1.0, 0.0).astype(BF16)


def _lanes(col):
    return jnp.broadcast_to(col, (col.shape[0], LANES))


def _sb_fwd(q, k, v):
    t = q.shape[0]
    bq, bk = min(SB_Q, t), min(SB_K, t)
    nsub, rep = bq // bk, bk // LANES

    def body(q_ref, k_ref, v_ref, o_ref, rt_ref):
        i = pl.program_id(1)
        qv = q_ref[...]
        after = _tri(bk, lambda r, c: r > c)

        def block(j, run, diag):
            st = pl.multiple_of(j * bk, bk)
            kv, vv = k_ref[pl.ds(st, bk), :], v_ref[pl.ds(st, bk), :]
            _, _, lb, l1m = _sb_scores(qv, kv)
            if diag:
                mask = _iota((bq, bk), 1) + j * bk < _iota((bq, bk), 0) + i * bq
                l1m = jnp.where(mask, l1m, 0.0)
            sums = _two_pass(l1m, after)
            a = jnp.exp(lb + jnp.tile(run, (1, rep)) + sums)
            if diag:
                a = jnp.where(mask, a, 0.0)
            return _dot(a.astype(BF16), vv), _lanes(sums[:, 0:1] + l1m[:, 0:1])

        def group(p, acc, run, diag):
            for s in reversed(range(nsub)):
                out, tot = block(p * nsub + s, run, diag)
                acc, run = acc + out, run + tot
            return acc, run

        zero = jnp.zeros((bq, LANES), F32)
        acc, run = group(i, zero, zero, True)
        acc, run = lax.fori_loop(1, i + 1, lambda jj, c: group(i - jj, c[0], c[1], False), (acc, run))
        o_ref[...] = acc
        rt_ref[...] = run

    qb = pl.BlockSpec((bq, HEAD_DIM), lambda h, i: (i, h))
    full = pl.BlockSpec((t, HEAD_DIM), lambda h, i: (0, h))
    return pl.pallas_call(
        body,
        name="sb_fwd",
        grid=(HEADS, t // bq),
        in_specs=[qb, full, full],
        out_specs=[qb, qb],
        out_shape=[jax.ShapeDtypeStruct(q.shape, F32), jax.ShapeDtypeStruct(q.shape, F32)],
        compiler_params=_params("parallel", "arbitrary"),
    )(q, k, v)


def _sb_bwd(q, k, v, rt, do):
    t = q.shape[0]
    bq, bk = min(SB_Q, t), min(SB_K, t)
    nsub, rep = bq // bk, bk // LANES
    scale = HEAD_DIM ** -0.5

    def body(q_ref, k_ref, v_ref, rt_ref, do_ref, dq_ref, dk_ref, dv_ref):
        i = pl.program_id(1)

        @pl.when(i == 0)
        def _():
            dk_ref[...] = jnp.zeros_like(dk_ref)
            dv_ref[...] = jnp.zeros_like(dv_ref)

        qv = q_ref[...]
        dob = do_ref[...].astype(BF16)
        rtot = rt_ref[...]
        upto = _tri(bk, lambda r, c: r <= c)
        before = _tri(bk, lambda r, c: r < c)

        def block(j, left, pg, diag):
            st = pl.multiple_of(j * bk, bk)
            kv, vv = k_ref[pl.ds(st, bk), :], v_ref[pl.ds(st, bk), :]
            z, e, lb, l1m = _sb_scores(qv, kv)
            if diag:
                mask = _iota((bq, bk), 1) + j * bk < _iota((bq, bk), 0) + i * bq
                l1m = jnp.where(mask, l1m, 0.0)
            beta = jnp.where(z >= 0.0, 1.0, e) / (1.0 + e)
            sums = _two_pass(l1m, upto)
            a = jnp.exp(lb + jnp.tile(rtot - left, (1, rep)) - sums)
            if diag:
                a = jnp.where(mask, a, 0.0)
            g = _dot(dob, vv, NT) * a
            dv_ref[pl.ds(st, bk), :] += _dot(a.astype(BF16), dob, TN)
            gsum = _two_pass(g, before)
            dz = g * (1.0 - beta) - (jnp.tile(pg, (1, rep)) + gsum) * beta
            if diag:
                dz = jnp.where(mask, dz, 0.0)
            dzb = (dz * scale).astype(BF16)
            dk_ref[pl.ds(st, bk), :] += _dot(dzb, qv, TN)
            return _dot(dzb, kv), _lanes(sums[:, bk - 1:bk]), _lanes(gsum[:, bk - 1:bk] + g[:, bk - 1:bk])

        def group(p, dq, left, pg, diag):
            for s in range(nsub):
                out, tot, gtot = block(p * nsub + s, left, pg, diag)
                dq, left, pg = dq + out, left + tot, pg + gtot
            return dq, left, pg

        zero = jnp.zeros((bq, LANES), F32)
        dq, left, pg = lax.fori_loop(0, i, lambda p, c: group(p, c[0], c[1], c[2], False), (zero, zero, zero))
        dq, _, _ = group(i, dq, left, pg, True)
        dq_ref[...] = dq

    qb = pl.BlockSpec((bq, HEAD_DIM), lambda h, i: (i, h))
    full = pl.BlockSpec((t, HEAD_DIM), lambda h, i: (0, h))
    return pl.pallas_call(
        body,
        name="sb_bwd",
        grid=(HEADS, t // bq),
        in_specs=[qb, full, full, qb, qb],
        out_specs=[qb, full, full],
        out_shape=[jax.ShapeDtypeStruct(q.shape, F32)] * 3,
        compiler_params=_params("parallel", "arbitrary"),
    )(q, k, v, rt, do)


def _adamw(w, g, m, v, name, tm=256):
    r, c = w.shape
    tm = tm if r % tm == 0 else r

    def body(w_ref, g_ref, m_ref, v_ref, d_ref, nm_ref, nv_ref):
        gv = g_ref[...]
        nm = ADAM_B1 * m_ref[...] + (1.0 - ADAM_B1) * gv
        nv = ADAM_B2 * v_ref[...] + (1.0 - ADAM_B2) * (gv * gv)
        m_hat = nm / (1.0 - ADAM_B1 ** ADAM_STEP)
        v_hat = nv / (1.0 - ADAM_B2 ** ADAM_STEP)
        d_ref[...] = -ADAM_LR * (m_hat / (jnp.sqrt(v_hat) + ADAM_EPS) + ADAM_WD * w_ref[...])
        nm_ref[...] = nm
        nv_ref[...] = nv

    blk = pl.BlockSpec((tm, c), lambda i: (i, 0))
    return pl.pallas_call(
        body,
        name=name,
        grid=(r // tm,),
        in_specs=[blk] * 4,
        out_specs=[blk] * 3,
        out_shape=[jax.ShapeDtypeStruct((r, c), F32)] * 3,
        compiler_params=_params("parallel"),
    )(w, g, m, v)


def _local_step(x, tgt, gains, wts, small):
    mix_pre, mix_post, mlp_pre, mlp_post, kv_gain = gains
    w_qkvg, w_ba, w_out, w_kv, w_q, w_o, w_up, w_down = wts
    conv_w, a_log, dt_bias, out_gain = small
    t, d = x.shape
    row = lambda a, i=None: a[i:i + 1] if i is not None else a
    al = jnp.zeros((1, LANES), F32).at[:, HEADS:2 * HEADS].set(a_log)
    dtb = jnp.zeros((1, LANES), F32).at[:, HEADS:2 * HEADS].set(dt_bias)
    og = jnp.tile(out_gain, (1, HEADS))
    full = lambda a: (a, a.shape[1], 0)

    (h0,) = _rowwise("norm_in", _fn_norm, [full(x)], [row(mix_pre, 0)], [(d, BF16)])
    qkvg = _matmul(h0, w_qkvg, "nn", F32, "mm_gdn_in", tk=1024)
    ba = _matmul(h0, w_ba, "nn", F32, "mm_gdn_ba", tk=1024)
    conv = _conv_fwd(qkvg, conv_w)
    conv_qkv = [(conv, d, 0), (conv, d, 1), (conv, d, 2)]
    gq, gk, gv = _rowwise("post_conv", _fn_post, conv_qkv, [], [(d, F32)] * 3)
    beta_c, gc_c, gt_c = _rowwise("gates", _fn_gates, [full(ba)], [al, dtb], [(LANES, F32)] * 3)
    per_head = lambda a, lo: jnp.swapaxes(a[:, lo:lo + HEADS], 0, 1)
    beta, gc, gt = per_head(beta_c, 0)[..., None], per_head(gc_c, HEADS)[..., None], per_head(gt_c, HEADS)[..., None]
    gcr = gc.reshape(HEADS, t // CHUNK, 1, CHUNK)
    pw, pu, pqg, pkg, pattn, pgl, ptm = _gdn_prep(gq, gk, gv, beta, gc, gt, gcr)
    o_gdn, states = _gdn_scan(pw, pu, pqg, pkg, pattn, pgl)
    (on,) = _rowwise("out_norm", _fn_outnorm, [full(o_gdn), (qkvg, d, 3)], [og], [(d, BF16)])
    mix0 = _matmul(on, w_out, "nn", F32, "mm_gdn_out", tk=1024)
    x1, h1 = _rowwise("res_a0", _fn_res_norm, [full(x), full(mix0)], [row(mix_post, 0), row(mlp_pre, 0)], [(d, F32), (d, BF16)])
    u0 = _matmul(h1, w_up[0], "nn", F32, "mm_up0", tk=1024)
    (a0,) = _rowwise("relu2_0", _fn_relu2, [full(u0)], [], [(D_FF, BF16)])
    d0 = _matmul(a0, w_down[0], "nn", F32, "mm_down0")
    x2, hkv, hq = _rowwise("res_b0", _fn_res_norm2, [full(x1), full(d0)], [row(mlp_post, 0), kv_gain, row(mix_pre, 1)], [(d, F32), (d, BF16), (d, BF16)])
    w_k, w_v = w_kv[:, :d], w_kv[:, d:]
    kp = _matmul(hkv, w_k, "nn", BF16, "mm_k", tk=1024)
    vp = _matmul(hkv, w_v, "nn", BF16, "mm_v", tk=1024)
    qp = _matmul(hq, w_q, "nn", BF16, "mm_q", tk=1024)
    o_sb, rt = _sb_fwd(qp, kp, vp)
    mix1 = _matmul(o_sb, w_o, "nn", F32, "mm_sb_out", tk=1024)
    x3, h3 = _rowwise("res_a1", _fn_res_norm, [full(x2), full(mix1)], [row(mix_post, 1), row(mlp_pre, 1)], [(d, F32), (d, BF16)])
    u1 = _matmul(h3, w_up[1], "nn", F32, "mm_up1", tk=1024)
    (a1,) = _rowwise("relu2_1", _fn_relu2, [full(u1)], [], [(D_FF, BF16)])
    d1 = _matmul(a1, w_down[1], "nn", F32, "mm_down1")

    loss, dx3, dd1, g_mlp_post1 = _loss_call(x3, d1, tgt, row(mlp_post, 1))
    da1 = _matmul(dd1, w_down[1], "nt", F32, "mm_down1_dx")
    g_down1 = _matmul(a1, dd1, "tn", F32, "mm_down1_dw")
    (du1,), _ = _rowwise_bwd("relu2_1_bwd", _fn_relu2, [full(u1)], [], [da1], [BF16])
    dh3 = _matmul(du1, w_up[1], "nt", F32, "mm_up1_dx")
    g_up1 = _matmul(h3, du1, "tn", F32, "mm_up1_dw")
    (dx2, dmix1), (g_mix_post1, g_mlp_pre1) = _rowwise_bwd(
        "res_a1_bwd", _fn_res_norm, [full(x2), full(mix1)], [row(mix_post, 1), row(mlp_pre, 1)], [dx3, dh3], [F32, F32])
    do_sb = _matmul(dmix1, w_o, "nt", F32, "mm_sb_out_dx")
    g_o = _matmul(o_sb, dmix1, "tn", F32, "mm_sb_out_dw")
    dqp, dkp, dvp = _sb_bwd(qp, kp, vp, rt, do_sb)
    dhq = _matmul(dqp, w_q, "nt", F32, "mm_q_dx")
    g_q = _matmul(hq, dqp, "tn", F32, "mm_q_dw")
    dhkv = _matmul(dvp, w_v, "nt", F32, "mm_v_dx", add=_matmul(dkp, w_k, "nt", F32, "mm_k_dx"))
    g_kv = jnp.concatenate([_matmul(hkv, dkp, "tn", F32, "mm_k_dw"), _matmul(hkv, dvp, "tn", F32, "mm_v_dw")], axis=1)
    (dx1, dd0), (g_mlp_post0, g_kv_gain, g_mix_pre1) = _rowwise_bwd(
        "res_b0_bwd", _fn_res_norm2, [full(x1), full(d0)], [row(mlp_post, 0), kv_gain, row(mix_pre, 1)], [dx2, dhkv, dhq], [F32, F32])
    da0 = _matmul(dd0, w_down[0], "nt", F32, "mm_down0_dx")
    g_down0 = _matmul(a0, dd0, "tn", F32, "mm_down0_dw")
    (du0,), _ = _rowwise_bwd("relu2_0_bwd", _fn_relu2, [full(u0)], [], [da0], [BF16])
    dh1 = _matmul(du0, w_up[0], "nt", F32, "mm_up0_dx")
    g_up0 = _matmul(h1, du0, "tn", F32, "mm_up0_dw")
    (dx0, dmix0), (g_mix_post0, g_mlp_pre0) = _rowwise_bwd(
        "res_a0_bwd", _fn_res_norm, [full(x), full(mix0)], [row(mix_post, 0), row(mlp_pre, 0)], [dx1, dh1], [F32, F32])
    don = _matmul(dmix0, w_out, "nt", F32, "mm_gdn_out_dx")
    g_out = _matmul(on, dmix0, "tn", F32, "mm_gdn_out_dw")
    (do_gdn, dgate), (g_og,) = _rowwise_bwd("out_norm_bwd", _fn_outnorm, [full(o_gdn), (qkvg, d, 3)], [og], [don], [F32, F32])
    dpw, dpu, dpqg, dpkg, dpattn, dpgl = _gdn_scan_bwd(pw, pu, pqg, pkg, pattn, pgl, states, do_gdn)
    dgq, dgk, dgv, dbeta, dgc, dgt, dgcr = _gdn_prep_bwd(gq, gk, gv, beta, gc, gt, gcr, ptm, dpw, dpu, dpqg, dpkg, dpattn, dpgl)
    to_lanes = lambda a, lo: jnp.pad(jnp.swapaxes(a, 0, 1), ((0, 0), (lo, LANES - lo - HEADS)))
    gate_cots = [to_lanes(dbeta[..., 0], 0), to_lanes(dgc[..., 0] + dgcr.reshape(HEADS, t), HEADS), to_lanes(dgt[..., 0], HEADS)]
    (dba,), (g_al, g_dtb) = _rowwise_bwd("gates_bwd", _fn_gates, [full(ba)], [al, dtb], gate_cots, [F32])
    (dconv,), _ = _rowwise_bwd("post_conv_bwd", _fn_post, conv_qkv, [], [dgq, dgk, dgv], [F32] * 3, joined=True)
    dqkvg, g_conv = _conv_bwd(dconv, dgate, qkvg, conv_w)
    dh0b = _matmul(dba, w_ba, "nt", F32, "mm_gdn_ba_dx", tk=LANES)
    dh0 = _matmul(dqkvg, w_qkvg, "nt", F32, "mm_gdn_in_dx", add=dh0b)
    g_qkvg = _matmul(h0, dqkvg, "tn", F32, "mm_gdn_in_dw")
    g_ba = _matmul(h0, dba, "tn", F32, "mm_gdn_ba_dw")
    (grad_x,), (g_mix_pre0,) = _rowwise_bwd("norm_in_bwd", lambda xx, gg: (_rms(xx, gg), xx), [full(x)], [row(mix_pre, 0)], [dh0, dx0], [F32])

    grads = dict(
        mix_pre_gain=jnp.concatenate([g_mix_pre0, g_mix_pre1], axis=0),
        mix_post_gain=jnp.concatenate([g_mix_post0, g_mix_post1], axis=0),
        mlp_pre_gain=jnp.concatenate([g_mlp_pre0, g_mlp_pre1], axis=0),
        mlp_post_gain=jnp.concatenate([g_mlp_post0, g_mlp_post1], axis=0),
        mlp_w_up=jnp.stack([g_up0, g_up1]),
        mlp_w_down=jnp.stack([g_down0, g_down1]),
        gdn_w_in=jnp.concatenate([g_qkvg, g_ba[:, :2 * HEADS]], axis=1)[None],
        gdn_conv_w=g_conv[None, :CONV_K],
        gdn_a_log=g_al[:, HEADS:2 * HEADS],
        gdn_dt_bias=g_dtb[:, HEADS:2 * HEADS],
        gdn_out_gain=jnp.sum(g_og.reshape(HEADS, HEAD_DIM), axis=0, keepdims=True),
        gdn_w_out=g_out[None],
        kv_gain=g_kv_gain[0],
        w_kv=g_kv,
        sb_w_q=g_q[None],
        sb_w_o=g_o[None],
    )
    return loss, grad_x, grads


N_DEV = 8
N_CHIPS = 4
PACK_COLS = 1024
PACK_ROW_TILE = 256

_HBM = pl.BlockSpec(memory_space=pltpu.HBM)


def _place():
    return lax.axis_index("x"), lax.axis_index("y"), lax.axis_index("c")


def _other_chips(x, y):
    return [(1 - x, y), (x, 1 - y), (1 - x, 1 - y)]


def _remote(src, dst, send_sem, recv_sem, to):
    return pltpu.make_async_remote_copy(src_ref=src, dst_ref=dst, send_sem=send_sem, recv_sem=recv_sem, device_id=to, device_id_type=MESH)


def _gather8(v, name):
    rows, cols = v.shape

    def body(v_ref, out_ref, sum_ref, send_sems, recv_sems, local_sem):
        x, y, c = _place()
        me, sibling = (x, y, c), (x, y, 1 - c)
        chips = _other_chips(x, y)

        def blk(px, py, pc):
            return out_ref.at[pl.ds((4 * px + 2 * py + pc) * rows, rows), :]

        def copy(k, block, to, src=None):
            return _remote(blk(*block) if src is None else src, blk(*block), send_sems.at[k], recv_sems.at[k], to)

        mine = pltpu.make_async_copy(v_ref, blk(*me), local_sem)
        mine.start()
        first = [copy(0, me, sibling, src=v_ref)] + [copy(1 + j, me, (*chip, c), src=v_ref) for j, chip in enumerate(chips)]
        for cp in first:
            cp.start()
        passed = [copy(4 + j, (*chip, c), sibling) for j, chip in enumerate(chips)]
        for j, chip in enumerate(chips):
            copy(1 + j, (*chip, c), me).wait_recv()
            passed[j].start()
        copy(0, sibling, me).wait_recv()
        for j, chip in enumerate(chips):
            copy(4 + j, (*chip, 1 - c), me).wait_recv()
        for cp in first + passed:
            cp.wait_send()
        mine.wait()
        acc = out_ref[pl.ds(0, rows), :]
        for dev in range(1, N_DEV):
            acc = acc + out_ref[pl.ds(dev * rows, rows), :]
        sum_ref[...] = acc

    vm = pl.BlockSpec(memory_space=pltpu.VMEM)
    return pl.pallas_call(
        body,
        name=name,
        out_shape=[jax.ShapeDtypeStruct((N_DEV * rows, cols), v.dtype), jax.ShapeDtypeStruct((rows, cols), v.dtype)],
        in_specs=[vm],
        out_specs=[vm, vm],
        scratch_shapes=[pltpu.SemaphoreType.DMA((7,)), pltpu.SemaphoreType.DMA((7,)), pltpu.SemaphoreType.DMA],
    )(v)


def _hbm_call(body, name, arrs, out_shapes, sem_counts):
    n = len(arrs)

    def wrapped(*refs):
        body(refs[:n], refs[n:2 * n], *refs[2 * n:])

    return pl.pallas_call(
        wrapped,
        name=name,
        out_shape=[jax.ShapeDtypeStruct(s, a.dtype) for s, a in zip(out_shapes, arrs)],
        in_specs=[_HBM] * n,
        out_specs=[_HBM] * n,
        scratch_shapes=[pltpu.SemaphoreType.DMA((k,)) for k in sem_counts],
    )(*arrs)


def _gather_weights(arrs):
    n = len(arrs)

    def body(w_refs, out_refs, send_sems, recv_sems, fsend_sems, frecv_sems, local_sems):
        x, y, c = _place()
        chips = _other_chips(x, y)
        s_me = 2 * x + y
        pairs = list(zip(w_refs, out_refs))
        mine = [pltpu.make_async_copy(w, o.at[s_me], local_sems.at[a]) for a, (w, o) in enumerate(pairs)]
        for cp in mine:
            cp.start()
        first = [_remote(w.at[c], o.at[s_me, c], send_sems.at[3 * a + j], recv_sems.at[3 * a + j], (px, py, c))
                 for a, (w, o) in enumerate(pairs) for j, (px, py) in enumerate(chips)]
        for cp in first:
            cp.start()
        passed = []
        for a, (w, o) in enumerate(pairs):
            for j, (px, py) in enumerate(chips):
                half = o.at[2 * px + py, c]
                _remote(half, half, send_sems.at[3 * a + j], recv_sems.at[3 * a + j], (px, py, c)).wait_recv()
                fwd = _remote(half, half, fsend_sems.at[3 * a + j], frecv_sems.at[3 * a + j], (x, y, 1 - c))
                fwd.start()
                passed.append(fwd)
        for a, (w, o) in enumerate(pairs):
            for j, (px, py) in enumerate(chips):
                half = o.at[2 * px + py, 1 - c]
                _remote(half, half, fsend_sems.at[3 * a + j], frecv_sems.at[3 * a + j], (x, y, 1 - c)).wait_recv()
        for cp in first + passed:
            cp.wait_send()
        for cp in mine:
            cp.wait()

    return _hbm_call(body, "gather_weights", arrs, [(N_CHIPS,) + a.shape for a in arrs], [3 * n] * 4 + [n])


def _swap_halves(arrs):
    n = len(arrs)

    def body(g_refs, a_refs, send_sems, recv_sems):
        x, y, c = _place()
        cps = [_remote(g.at[1 - c], a, send_sems.at[i], recv_sems.at[i], (x, y, 1 - c)) for i, (g, a) in enumerate(zip(g_refs, a_refs))]
        for cp in cps:
            cp.start()
        for cp in cps:
            cp.wait()

    return _hbm_call(body, "grads_to_sibling", arrs, [a.shape[1:] for a in arrs], [n, n])


def _scatter_to_chips(arrs):
    n = len(arrs)

    def body(p_refs, b_refs, send_sems, recv_sems):
        x, y, c = _place()
        cps = [_remote(p.at[2 * px + py], b.at[j], send_sems.at[3 * i + j], recv_sems.at[3 * i + j], (px, py, c))
               for i, (p, b) in enumerate(zip(p_refs, b_refs)) for j, (px, py) in enumerate(_other_chips(x, y))]
        for cp in cps:
            cp.start()
        for cp in cps:
            cp.wait()

    return _hbm_call(body, "grads_to_chips", arrs, [(3,) + a.shape[1:] for a in arrs], [3 * n, 3 * n])


def _share_halves(arrs):
    n = len(arrs)

    def body(q_refs, out_refs, send_sems, recv_sems, local_sems):
        x, y, c = _place()
        pairs = list(zip(q_refs, out_refs))
        mine = [pltpu.make_async_copy(q, o.at[c], local_sems.at[i]) for i, (q, o) in enumerate(pairs)]
        cps = [_remote(q, o.at[c], send_sems.at[i], recv_sems.at[i], (x, y, 1 - c)) for i, (q, o) in enumerate(pairs)]
        for cp in mine + cps:
            cp.start()
        for i, (q, o) in enumerate(pairs):
            _remote(q, o.at[1 - c], send_sems.at[i], recv_sems.at[i], (x, y, 1 - c)).wait_recv()
        for cp in cps:
            cp.wait_send()
        for cp in mine:
            cp.wait()

    return _hbm_call(body, "grads_share", arrs, [(2,) + a.shape for a in arrs], [n, n, n])


_BIG = (
    ("mlp_w_up", (2, 1024, 1024), "cols"),
    ("mlp_w_down", (2, 1024, 1024), "rows"),
    ("gdn_w_out", (1, 256, 1024), "rows"),
    ("w_kv", (1024, 512), "cols"),
    ("sb_w_q", (1, 256, 1024), "rows"),
    ("sb_w_o", (1, 256, 1024), "rows"),
)
_W_IN_SHARD = (1, 1024, 1028)


def _numel(shape):
    n = 1
    for s in shape:
        n *= s
    return n


_PACK_LEN = sum(_numel(s) for _, s, _ in _BIG)
_PACK_ROWS = -(-_PACK_LEN // (2 * PACK_COLS * PACK_ROW_TILE)) * PACK_ROW_TILE
_PACK_PAD = 2 * _PACK_ROWS * PACK_COLS - _PACK_LEN


def _pack_shards(shards, dtype):
    flat = jnp.concatenate([shards[n].astype(dtype).reshape(-1) for n, _, _ in _BIG] + [jnp.zeros((_PACK_PAD,), dtype)])
    return flat.reshape(2, _PACK_ROWS, PACK_COLS)


def _unpack_shards(packed):
    flat = packed.reshape(-1)
    out, off = {}, 0
    for n, shape, _ in _BIG:
        out[n] = flat[off:off + _numel(shape)].reshape(shape)
        off += _numel(shape)
    return out


def _join(stacked, how):
    nd = stacked.ndim - 1
    ax = nd - 1 if how == "cols" else nd - 2
    moved = jnp.moveaxis(stacked, 0, ax)
    shape = list(stacked.shape[1:])
    shape[ax] *= N_CHIPS
    return moved.reshape(shape)


def _split(full, shard_shape, how):
    nd = len(shard_shape)
    ax = nd - 1 if how == "cols" else nd - 2
    shape = list(shard_shape)
    shape.insert(ax, N_CHIPS)
    return jnp.moveaxis(full.reshape(shape), ax, 0)


def _unpack_full(gathered):
    flat = gathered.reshape(N_CHIPS, -1)
    out, off = {}, 0
    for n, shape, how in _BIG:
        out[n] = _join(flat[:, off:off + _numel(shape)].reshape((N_CHIPS,) + shape), how)
        off += _numel(shape)
    return out


def _pack_full(full):
    flat = jnp.concatenate([_split(full[n], shape, how).reshape(N_CHIPS, -1) for n, shape, how in _BIG] + [jnp.zeros((N_CHIPS, _PACK_PAD), F32)], axis=1)
    return jnp.swapaxes(flat.reshape(N_CHIPS, 2, _PACK_ROWS, PACK_COLS), 0, 1).reshape(2, N_CHIPS * _PACK_ROWS, PACK_COLS)


_SMALL = (
    ("mix_pre_gain", (2, 1024)),
    ("mix_post_gain", (2, 1024)),
    ("mlp_pre_gain", (2, 1024)),
    ("mlp_post_gain", (2, 1024)),
    ("kv_gain", (1024,)),
    ("gdn_out_gain", (1, 128)),
    ("gdn_a_log", (1, 8)),
    ("gdn_dt_bias", (1, 8)),
    ("gdn_conv_w", (1, 4, 3072)),
    ("loss", ()),
)


def _rows_of(shape):
    return -(-_numel(shape) // LANES)


_SMALL_ROWS = -(-sum(_rows_of(s) for _, s in _SMALL) // 8) * 8


def _pack_small(vals):
    parts = []
    for n, shape in _SMALL:
        flat = vals[n].reshape(-1)
        parts.append(jnp.pad(flat, (0, _rows_of(shape) * LANES - flat.shape[0])))
    flat = jnp.concatenate(parts)
    return jnp.pad(flat, (0, _SMALL_ROWS * LANES - flat.shape[0])).reshape(_SMALL_ROWS, LANES)


def _unpack_small(packed):
    flat = packed.reshape(-1)
    out, off = {}, 0
    for n, shape in _SMALL:
        out[n] = flat[off:off + _numel(shape)].reshape(shape)
        off += _rows_of(shape) * LANES
    return out


_WEIGHTS = ("mix_pre_gain", "mix_post_gain", "mlp_pre_gain", "mlp_post_gain", "mlp_w_up", "mlp_w_down", "gdn_w_in", "gdn_conv_w",
            "gdn_a_log", "gdn_dt_bias", "gdn_out_gain", "gdn_w_out", "kv_gain", "w_kv", "sb_w_q", "sb_w_o")


def _as2d(a):
    return a.reshape(1, -1) if a.ndim <= 1 else a.reshape(-1, a.shape[-1])


def kernel(x, mix_pre_gain, mix_post_gain, mlp_pre_gain, mlp_post_gain, mlp_w_up, mlp_w_down, gdn_w_in, gdn_conv_w, gdn_a_log, gdn_dt_bias, gdn_out_gain, gdn_w_out, kv_gain, w_kv, sb_w_q, sb_w_o, loss_target, m_mix_pre_gain, m_mix_post_gain, m_mlp_pre_gain, m_mlp_post_gain, m_mlp_w_up, m_mlp_w_down, m_gdn_w_in, m_gdn_conv_w, m_gdn_a_log, m_gdn_dt_bias, m_gdn_out_gain, m_gdn_w_out, m_kv_gain, m_w_kv, m_sb_w_q, m_sb_w_o, v_mix_pre_gain, v_mix_post_gain, v_mlp_pre_gain, v_mlp_post_gain, v_mlp_w_up, v_mlp_w_down, v_gdn_w_in, v_gdn_conv_w, v_gdn_a_log, v_gdn_dt_bias, v_gdn_out_gain, v_gdn_w_out, v_kv_gain, v_w_kv, v_sb_w_q, v_sb_w_o):
    w = dict(mix_pre_gain=mix_pre_gain, mix_post_gain=mix_post_gain, mlp_pre_gain=mlp_pre_gain, mlp_post_gain=mlp_post_gain, mlp_w_up=mlp_w_up, mlp_w_down=mlp_w_down, gdn_w_in=gdn_w_in, gdn_conv_w=gdn_conv_w, gdn_a_log=gdn_a_log, gdn_dt_bias=gdn_dt_bias, gdn_out_gain=gdn_out_gain, gdn_w_out=gdn_w_out, kv_gain=kv_gain, w_kv=w_kv, sb_w_q=sb_w_q, sb_w_o=sb_w_o)
    m = dict(mix_pre_gain=m_mix_pre_gain, mix_post_gain=m_mix_post_gain, mlp_pre_gain=m_mlp_pre_gain, mlp_post_gain=m_mlp_post_gain, mlp_w_up=m_mlp_w_up, mlp_w_down=m_mlp_w_down, gdn_w_in=m_gdn_w_in, gdn_conv_w=m_gdn_conv_w, gdn_a_log=m_gdn_a_log, gdn_dt_bias=m_gdn_dt_bias, gdn_out_gain=m_gdn_out_gain, gdn_w_out=m_gdn_w_out, kv_gain=m_kv_gain, w_kv=m_w_kv, sb_w_q=m_sb_w_q, sb_w_o=m_sb_w_o)
    v = dict(mix_pre_gain=v_mix_pre_gain, mix_post_gain=v_mix_post_gain, mlp_pre_gain=v_mlp_pre_gain, mlp_post_gain=v_mlp_post_gain, mlp_w_up=v_mlp_w_up, mlp_w_down=v_mlp_w_down, gdn_w_in=v_gdn_w_in, gdn_conv_w=v_gdn_conv_w, gdn_a_log=v_gdn_a_log, gdn_dt_bias=v_gdn_dt_bias, gdn_out_gain=v_gdn_out_gain, gdn_w_out=v_gdn_w_out, kv_gain=v_kv_gain, w_kv=v_w_kv, sb_w_q=v_sb_w_q, sb_w_o=v_sb_w_o)
    cx, cy, cc = _place()
    chip = 2 * cx + cy
    conv_cols = gdn_conv_w.shape[-1]

    in_rows = _W_IN_SHARD[1] // 2
    packed_all, w_in_all = _gather_weights((_pack_shards(w, BF16), gdn_w_in.astype(BF16).reshape(2, in_rows, _W_IN_SHARD[2])))
    full = _unpack_full(packed_all)
    conv_rows = jnp.pad(gdn_conv_w[0], ((0, 8 - CONV_K), (0, 0))).reshape(-1, LANES)
    conv_all, _ = _gather8(conv_rows, "gather_conv_w")
    conv_all = conv_all.reshape(N_CHIPS, 2, 8, conv_cols)[:, 0, :CONV_K]
    conv_full = jnp.swapaxes(conv_all, 0, 1).reshape(CONV_K, N_CHIPS * conv_cols)

    w_in = _join(w_in_all.reshape((N_CHIPS,) + _W_IN_SHARD), "cols")[0]
    wts = (w_in[:, :4 * HEADS * HEAD_DIM], jnp.pad(w_in[:, 4 * HEADS * HEAD_DIM:], ((0, 0), (0, LANES - 2 * HEADS))), full["gdn_w_out"][0], full["w_kv"],
           full["sb_w_q"][0], full["sb_w_o"][0], full["mlp_w_up"], full["mlp_w_down"])
    gains = (mix_pre_gain, mix_post_gain, mlp_pre_gain, mlp_post_gain, kv_gain[None])
    small = (conv_full, gdn_a_log, gdn_dt_bias, gdn_out_gain)
    loss_rows, grad_x, g_full = _local_step(x[0], loss_target[0], gains, wts, small)

    g_in = _split(g_full["gdn_w_in"], _W_IN_SHARD, "cols").reshape(N_CHIPS, 2, in_rows, _W_IN_SHARD[2])
    bufs = (_pack_full(g_full), jnp.swapaxes(g_in, 0, 1).reshape(2, N_CHIPS * in_rows, _W_IN_SHARD[2]))
    from_sibling = _swap_halves(bufs)
    partial = []
    for i, (buf, other) in enumerate(zip(bufs, from_sibling)):
        cols = buf.shape[-1]
        own_half = lax.dynamic_index_in_dim(buf, cc, 0, keepdims=False)
        (p,) = _rowwise(f"grads_add_sibling_{i}", lambda a, b: (a + b,), [(own_half, cols, 0), (other, cols, 0)], [], [(cols, F32)], tm=PACK_ROW_TILE)
        partial.append(p.reshape(N_CHIPS, -1, cols))
    from_chips = _scatter_to_chips(tuple(partial))
    reduced = []
    for i, (p, others) in enumerate(zip(partial, from_chips)):
        cols = p.shape[-1]
        mine = lax.dynamic_index_in_dim(p, chip, 0, keepdims=False)
        (r,) = _rowwise(f"grads_add_chips_{i}", lambda a, b, c, d: (((a + b) + c) + d,),
                        [(mine, cols, 0), (others[0], cols, 0), (others[1], cols, 0), (others[2], cols, 0)], [], [(cols, F32)], tm=PACK_ROW_TILE)
        reduced.append(r)
    shared = _share_halves(tuple(reduced))
    g_shard = _unpack_shards(shared[0])
    g_shard["gdn_w_in"] = shared[1].reshape(_W_IN_SHARD)

    g_small_local = {n: g_full[n] for n, _ in _SMALL if n != "loss"}
    g_small_local["loss"] = loss_rows[0, 0]
    _, small_sum = _gather8(_pack_small(g_small_local), "allreduce_small")
    g_small = _unpack_small(small_sum)
    loss = g_small.pop("loss")
    g_small["gdn_conv_w"] = lax.dynamic_slice_in_dim(g_small["gdn_conv_w"], chip * conv_cols, conv_cols, axis=2)

    grads = {**g_shard, **g_small}
    deltas, new_m, new_v = {}, {}, {}
    for n in _WEIGHTS:
        d2, m2, v2 = _adamw(_as2d(w[n]), _as2d(grads[n]), _as2d(m[n]), _as2d(v[n]), "adamw_" + n)
        deltas[n], new_m[n], new_v[n] = d2.reshape(w[n].shape), m2.reshape(w[n].shape), v2.reshape(w[n].shape)
    return (loss, grad_x[None], *[grads[n].reshape(w[n].shape) for n in _WEIGHTS], *[deltas[n] for n in _WEIGHTS],
            *[new_m[n] for n in _WEIGHTS], *[new_v[n] for n in _WEIGHTS])
```

```python
import functools

import jax
import jax.numpy as jnp
from jax import lax
from jax.experimental import pallas as pl
from jax.experimental.pallas import tpu as pltpu

F32, BF16 = jnp.float32, jnp.bfloat16
HI = lax.Precision.HIGHEST
MESH = pl.DeviceIdType.MESH

EPS = 1e-6
D_MODEL = 1024
HEADS = 8
HEAD_DIM = 128
CHUNK = 64
CHUNK_SHIFT = CHUNK.bit_length() - 1
CONV_K = 4
D_FF = 4096
QKV = 3 * HEADS * HEAD_DIM

ADAM_LR, ADAM_B1, ADAM_B2, ADAM_EPS, ADAM_WD, ADAM_STEP = 0.001, 0.9, 0.999, 1e-08, 0.01, 10

VMEM_LIMIT_BYTES = 48 * 1024 * 1024
LANES = 128

NN = ((1,), (0,))
NT = ((1,), (1,))
TN = ((0,), (0,))


def _dot(a, b, dims=NN, precision=None):
    return lax.dot_general(a, b, (dims, ((), ())), precision=precision, preferred_element_type=F32)


def _params(*sem):
    return pltpu.CompilerParams(dimension_semantics=sem, vmem_limit_bytes=VMEM_LIMIT_BYTES)


def _iota(shape, axis):
    return lax.broadcasted_iota(jnp.int32, shape, axis)


def _matmul(a, b, mode, out_dtype, name, tm=1024, tn=1024, tk=512, add=None):
    if mode == "nn":
        (m, k), (k2, n) = a.shape, b.shape
    elif mode == "nt":
        (m, k), (n, k2) = a.shape, b.shape
    else:
        (k, m), (k2, n) = a.shape, b.shape
    assert k == k2, (a.shape, b.shape, mode)
    tm, tn, tk = min(tm, m), min(tn, n), min(tk, k)
    assert m % tm == 0 and n % tn == 0 and k % tk == 0, (a.shape, b.shape, mode)
    nk = k // tk
    dims = {"nn": NN, "nt": NT, "tn": TN}[mode]
    has_add = add is not None

    def body(*refs):
        a_ref, b_ref = refs[:2]
        o_ref, acc_ref = refs[-2:]
        kk = pl.program_id(2)

        @pl.when(kk == 0)
        def _():
            acc_ref[...] = refs[2][...].astype(F32) if has_add else jnp.zeros_like(acc_ref)

        acc_ref[...] += _dot(a_ref[...].astype(BF16), b_ref[...].astype(BF16), dims)

        @pl.when(kk == nk - 1)
        def _():
            o_ref[...] = acc_ref[...].astype(o_ref.dtype)

    a_spec = pl.BlockSpec((tk, tm), lambda i, j, kk: (kk, i)) if mode == "tn" else pl.BlockSpec((tm, tk), lambda i, j, kk: (i, kk))
    b_spec = pl.BlockSpec((tn, tk), lambda i, j, kk: (j, kk)) if mode == "nt" else pl.BlockSpec((tk, tn), lambda i, j, kk: (kk, j))
    o_spec = pl.BlockSpec((tm, tn), lambda i, j, kk: (i, j))
    return pl.pallas_call(
        body,
        name=name,
        grid=(m // tm, n // tn, nk),
        in_specs=[a_spec, b_spec] + ([o_spec] if has_add else []),
        out_specs=o_spec,
        out_shape=jax.ShapeDtypeStruct((m, n), out_dtype),
        scratch_shapes=[pltpu.VMEM((tm, tn), F32)],
        compiler_params=_params("parallel", "parallel", "arbitrary"),
    )(a, b, *([add] if has_add else []))


def _row_specs(rows, tm):
    return [pl.BlockSpec((tm, w), lambda i, cb=cb: (i, cb)) for _, w, cb in rows]


def _full_spec(p):
    return pl.BlockSpec(p.shape, lambda i: (0,) * p.ndim)


def _rowwise(name, fn, rows, params, outs, tm=256):
    t = rows[0][0].shape[0]
    tm = min(tm, t)
    nr, npar = len(rows), len(params)

    def body(*refs):
        ins = [r[...].astype(F32) for r in refs[:nr]]
        ps = [p[...] for p in refs[nr:nr + npar]]
        res = fn(*ins, *ps)
        for o_ref, r in zip(refs[nr + npar:], res):
            o_ref[...] = r.astype(o_ref.dtype)

    return pl.pallas_call(
        body,
        name=name,
        grid=(t // tm,),
        in_specs=_row_specs(rows, tm) + [_full_spec(p) for p in params],
        out_specs=[pl.BlockSpec((tm, w), lambda i: (i, 0)) for w, _ in outs],
        out_shape=[jax.ShapeDtypeStruct((t, w), dt) for w, dt in outs],
        compiler_params=_params("parallel"),
    )(*[r[0] for r in rows], *params)


def _rowwise_bwd(name, fn, rows, params, cots, grad_dtypes, tm=256, joined=False):
    t = rows[0][0].shape[0]
    tm = min(tm, t)
    nr, npar, nc = len(rows), len(params), len(cots)
    want = [j for j, dt in enumerate(grad_dtypes) if dt is not None]
    widths = [rows[j][1] for j in want]
    n_row_outs = 1 if joined else len(want)

    def body(*refs):
        i = pl.program_id(0)
        ins = [r[...].astype(F32) for r in refs[:nr]]
        ps = [p[...] for p in refs[nr:nr + npar]]
        cs = tuple(c[...].astype(F32) for c in refs[nr + npar:nr + npar + nc])
        _, vjp = jax.vjp(fn, *ins, *ps)
        gs = vjp(cs)
        outs = refs[nr + npar + nc:]
        if joined:
            off = 0
            for j, w in zip(want, widths):
                outs[0][:, off:off + w] = gs[j].astype(outs[0].dtype)
                off += w
        else:
            for o_ref, j in zip(outs, want):
                o_ref[...] = gs[j].astype(o_ref.dtype)
        pg_refs = outs[n_row_outs:]

        @pl.when(i == 0)
        def _():
            for pg in pg_refs:
                pg[...] = jnp.zeros_like(pg)

        for pg, g in zip(pg_refs, gs[nr:]):
            pg[...] += g

    if joined:
        row_specs = [pl.BlockSpec((tm, sum(widths)), lambda i: (i, 0))]
        row_shapes = [jax.ShapeDtypeStruct((t, sum(widths)), grad_dtypes[want[0]])]
    else:
        row_specs = [pl.BlockSpec((tm, w), lambda i: (i, 0)) for w in widths]
        row_shapes = [jax.ShapeDtypeStruct((t, w), grad_dtypes[j]) for j, w in zip(want, widths)]
    res = pl.pallas_call(
        body,
        name=name,
        grid=(t // tm,),
        in_specs=_row_specs(rows, tm) + [_full_spec(p) for p in params] + [pl.BlockSpec((tm, c.shape[1]), lambda i: (i, 0)) for c in cots],
        out_specs=row_specs + [_full_spec(p) for p in params],
        out_shape=row_shapes + [jax.ShapeDtypeStruct(p.shape, F32) for p in params],
        compiler_params=_params("arbitrary"),
    )(*[r[0] for r in rows], *params, *cots)
    return res[:n_row_outs], res[n_row_outs:]


def _rms(x, g):
    return x * lax.rsqrt(jnp.mean(x * x, axis=-1, keepdims=True) + EPS) * g


def _sigmoid(x):
    return 1.0 / (1.0 + jnp.exp(-x))


def _softplus(x):
    return jnp.maximum(x, 0.0) + jnp.log1p(jnp.exp(-jnp.abs(x)))


def _two_pass(x, m):
    hi = x.astype(BF16)
    lo = (x - hi.astype(F32)).astype(BF16)
    return _dot(hi, m) + _dot(lo, m)


def _head_sum_impl(x):
    w = HEADS * HEAD_DIM
    fold = jnp.where((_iota((w, LANES), 0) >> 7) == _iota((w, LANES), 1), 1.0, 0.0).astype(BF16)
    spread = jnp.where(_iota((LANES, w), 0) == (_iota((LANES, w), 1) >> 7), 1.0, 0.0).astype(BF16)
    return _two_pass(_two_pass(x, fold), spread)


@jax.custom_vjp
def _head_sum(x):
    return _head_sum_impl(x)


_head_sum.defvjp(lambda x: (_head_sum_impl(x), None), lambda _, g: (_head_sum_impl(g),))


def _fn_norm(x, g):
    return (_rms(x, g),)


def _fn_gates(ba, al, dt):
    col = _iota((1, LANES), 1)
    g = jnp.where((col >= HEADS) & (col < 2 * HEADS), -jnp.exp(al) * _softplus(ba + dt), 0.0)
    rows = ba.shape[0]
    r, c = _iota((rows, rows), 0), _iota((rows, rows), 1)
    same = (r >> CHUNK_SHIFT) == (c >> CHUNK_SHIFT)
    gc = _dot(jnp.where(same & (r >= c), 1.0, 0.0), g, precision=HI)
    gtot = _dot(jnp.where(same, 1.0, 0.0), g, precision=HI)
    return _sigmoid(ba), gc, gtot


def _fn_post_q(c):
    s = c * _sigmoid(c)
    return (s * lax.rsqrt(_head_sum(s * s) + EPS) * (HEAD_DIM ** -0.5),)


def _fn_post_k(c):
    s = c * _sigmoid(c)
    return (s * lax.rsqrt(_head_sum(s * s) + EPS),)


def _fn_post_v(c):
    return (c * _sigmoid(c),)


def _fn_post(cq, ck, cv):
    return _fn_post_q(cq) + _fn_post_k(ck) + _fn_post_v(cv)


def _fn_outnorm(o, gate, og):
    y = o * lax.rsqrt(_head_sum(o * o) * (1.0 / HEAD_DIM) + EPS) * og
    return (y * (gate * _sigmoid(gate)),)


def _fn_res_norm(x, m, gp, gn):
    x1 = x + _rms(m, gp)
    return x1, _rms(x1, gn)


def _fn_res_norm2(x, m, gp, ga, gb):
    x1 = x + _rms(m, gp)
    return x1, _rms(x1, ga), _rms(x1, gb)


def _fn_relu2(u):
    r = jnp.maximum(u, 0.0)
    return (r * r,)


def _loss_call(x3, d1, tgt, g, tm=256):
    t, d = x3.shape
    tm = min(tm, t)

    def body(x_ref, d_ref, t_ref, g_ref, loss_ref, dx_ref, dd_ref, dg_ref):
        i = pl.program_id(0)
        y, vjp = jax.vjp(lambda x, dd, gg: x + _rms(dd, gg), x_ref[...], d_ref[...], g_ref[...])
        err = y - t_ref[...]
        lrow = 0.5 * jnp.mean(err * err, axis=-1, keepdims=True)
        dx, dd, dg = vjp(err * (1.0 / d))
        dx_ref[...] = dx
        dd_ref[...] = dd.astype(dd_ref.dtype)

        @pl.when(i == 0)
        def _():
            loss_ref[...] = jnp.zeros_like(loss_ref)
            dg_ref[...] = jnp.zeros_like(dg_ref)

        loss_ref[...] += jnp.broadcast_to(jnp.sum(lrow, axis=0, keepdims=True), loss_ref.shape)
        dg_ref[...] += dg

    row = pl.BlockSpec((tm, d), lambda i: (i, 0))
    return pl.pallas_call(
        body,
        name="loss_head",
        grid=(t // tm,),
        in_specs=[row, row, row, _full_spec(g)],
        out_specs=[pl.BlockSpec((8, LANES), lambda i: (0, 0)), row, row, _full_spec(g)],
        out_shape=[jax.ShapeDtypeStruct((8, LANES), F32), jax.ShapeDtypeStruct((t, d), F32), jax.ShapeDtypeStruct((t, d), BF16), jax.ShapeDtypeStruct(g.shape, F32)],
        compiler_params=_params("arbitrary"),
    )(x3, d1, tgt, g)


HALO = 8


def _conv_fwd(qkvg, conv_w, tm=256):
    t = qkvg.shape[0]
    tm = min(tm, t)

    def body(cur_ref, prev_ref, w_ref, o_ref, buf):
        i = pl.program_id(0)
        buf[0:HALO, :] = jnp.where(i > 0, prev_ref[...], 0.0)
        buf[HALO:, :] = cur_ref[...]
        acc = buf[pl.ds(HALO - CONV_K + 1, tm), :] * w_ref[pl.ds(0, 1), :]
        for j in range(1, CONV_K):
            acc = acc + buf[pl.ds(HALO - CONV_K + 1 + j, tm), :] * w_ref[pl.ds(j, 1), :]
        o_ref[...] = acc

    return pl.pallas_call(
        body,
        name="conv_fwd",
        grid=(t // tm,),
        in_specs=[
            pl.BlockSpec((tm, QKV), lambda i: (i, 0)),
            pl.BlockSpec((HALO, QKV), lambda i: (jnp.maximum(i * (tm // HALO) - 1, 0), 0)),
            pl.BlockSpec((CONV_K, QKV), lambda i: (0, 0)),
        ],
        out_specs=pl.BlockSpec((tm, QKV), lambda i: (i, 0)),
        out_shape=jax.ShapeDtypeStruct((t, QKV), F32),
        scratch_shapes=[pltpu.VMEM((tm + HALO, QKV), F32)],
        compiler_params=_params("parallel"),
    )(qkvg, qkvg, conv_w)


def _conv_bwd(dc, dgate, qkvg, conv_w, tm=256):
    t = dc.shape[0]
    tm = min(tm, t)
    n = t // tm
    wg = dgate.shape[1]

    def body(dc_ref, dcn_ref, dgate_ref, x_ref, xp_ref, w_ref, dx_ref, dw_ref, bufd, bufx):
        i = pl.program_id(0)
        bufd[0:tm, :] = dc_ref[...]
        bufd[tm:, :] = jnp.where(i < n - 1, dcn_ref[...], 0.0)
        bufx[0:HALO, :] = jnp.where(i > 0, xp_ref[...], 0.0)
        bufx[HALO:, :] = x_ref[...]

        @pl.when(i == 0)
        def _():
            dw_ref[...] = jnp.zeros_like(dw_ref)

        dcv = dc_ref[...]
        acc = bufd[pl.ds(CONV_K - 1, tm), :] * w_ref[pl.ds(0, 1), :]
        for j in range(1, CONV_K):
            acc = acc + bufd[pl.ds(CONV_K - 1 - j, tm), :] * w_ref[pl.ds(j, 1), :]
        dx_ref[:, 0:QKV] = acc.astype(dx_ref.dtype)
        dx_ref[:, QKV:] = dgate_ref[...].astype(dx_ref.dtype)
        for j in range(CONV_K):
            dw_ref[pl.ds(j, 1), :] += jnp.sum(dcv * bufx[pl.ds(HALO - CONV_K + 1 + j, tm), :], axis=0, keepdims=True)

    return pl.pallas_call(
        body,
        name="conv_bwd",
        grid=(n,),
        in_specs=[
            pl.BlockSpec((tm, QKV), lambda i: (i, 0)),
            pl.BlockSpec((HALO, QKV), lambda i: (jnp.minimum((i + 1) * (tm // HALO), t // HALO - 1), 0)),
            pl.BlockSpec((tm, wg), lambda i: (i, 0)),
            pl.BlockSpec((tm, QKV), lambda i: (i, 0)),
            pl.BlockSpec((HALO, QKV), lambda i: (jnp.maximum(i * (tm // HALO) - 1, 0), 0)),
            pl.BlockSpec((CONV_K, QKV), lambda i: (0, 0)),
        ],
        out_specs=[pl.BlockSpec((tm, QKV + wg), lambda i: (i, 0)), pl.BlockSpec((HALO, QKV), lambda i: (0, 0))],
        out_shape=[jax.ShapeDtypeStruct((t, QKV + wg), BF16), jax.ShapeDtypeStruct((HALO, QKV), F32)],
        scratch_shapes=[pltpu.VMEM((tm + HALO, QKV), F32), pltpu.VMEM((tm + HALO, QKV), F32)],
        compiler_params=_params("arbitrary"),
    )(dc, dc, dgate, qkvg, qkvg, conv_w)


PREP_CHUNKS = 4


def _hi_lo(x):
    hi = x.astype(BF16)
    return hi, (x - hi.astype(F32)).astype(BF16)


def _mm3(a, b, dims=NN):
    (ah, al), (bh, bl) = _hi_lo(a), _hi_lo(b)
    return _dot(ah, bh, dims) + (_dot(ah, bl, dims) + _dot(al, bh, dims))


def _neumann(lower):
    c = lower.shape[0]
    p = -lower
    tmat = jnp.where(_iota((c, c), 0) == _iota((c, c), 1), 1.0, 0.0) + p
    for _ in range(CHUNK_SHIFT - 1):
        p = _mm3(p, p)
        tmat = tmat + _mm3(tmat, p)
    return tmat


def _inv_cotangent(tmat, dt):
    return -_mm3(_mm3(tmat, dt, TN), tmat, NT)


@jax.custom_vjp
def _tri_inv(lower):
    return _neumann(lower)


def _tri_inv_fwd(lower):
    tmat = _neumann(lower)
    return tmat, tmat


_tri_inv.defvjp(_tri_inv_fwd, lambda tmat, dt: (_inv_cotangent(tmat, dt),))


@jax.custom_vjp
def _tri_inv_known(lower, tmat):
    return tmat


_tri_inv_known.defvjp(lambda lower, tmat: (tmat, tmat), lambda tmat, dt: (_inv_cotangent(tmat, dt), jnp.zeros_like(tmat)))


def _prep_chunk(q, k, v, b, gc, gt, gcr, tmat=None):
    c = CHUNK
    r, col = _iota((c, c), 0), _iota((c, c), 1)
    incl, strict = r >= col, r > col
    decay = jnp.where(incl, jnp.exp(jnp.where(incl, gc - gcr, 0.0)), 0.0)
    kb = k * b
    kbf = k.astype(BF16)
    lower = jnp.where(strict, _dot(kb.astype(BF16), kbf, NT) * decay, 0.0)
    tmat = _tri_inv(lower) if tmat is None else _tri_inv_known(lower, tmat)
    tb = tmat.astype(BF16)
    egc = jnp.exp(gc)
    w = _dot(tb, (kb * egc).astype(BF16))
    u = _dot(tb, (v * b).astype(BF16))
    attn = _dot(q.astype(BF16), kbf, NT) * decay
    qg = q * egc
    kg = k * jnp.exp(gt - gc)
    gl = jnp.broadcast_to(jnp.exp(jnp.mean(gt.reshape(c // 8, 8, 1), axis=0)), (8, HEAD_DIM))
    return w, u, qg, kg, attn, gl, tmat


def _prep_specs(rows, gch):
    head = pl.BlockSpec((rows, HEAD_DIM), lambda n, h: (n, h))
    gates = pl.BlockSpec((rows, LANES), lambda n, h: (n, 0))
    gcrow = pl.BlockSpec((1, gch, 1, CHUNK), lambda n, h: (h, n, 0, 0))
    square = pl.BlockSpec((1, rows, CHUNK), lambda n, h: (h, n, 0))
    gl = pl.BlockSpec((1, gch * 8, HEAD_DIM), lambda n, h: (h, n, 0))
    return head, gates, gcrow, square, gl


def _pick_lane(ref, sl, lane):
    return jnp.sum(jnp.where(_iota((1, LANES), 1) == lane, ref[sl, :], 0.0), axis=1, keepdims=True)


def _gdn_prep(q, k, v, beta, gc, gt, gcr):
    t = q.shape[0]
    gch = min(PREP_CHUNKS, t // CHUNK)
    rows = gch * CHUNK

    def body(q_ref, k_ref, v_ref, b_ref, gc_ref, gt_ref, gcr_ref, w_ref, u_ref, qg_ref, kg_ref, at_ref, gl_ref, tm_ref):
        h = pl.program_id(1)
        for c in range(gch):
            sl = pl.ds(c * CHUNK, CHUNK)
            w, u, qg, kg, attn, gl, tmat = _prep_chunk(q_ref[sl, :], k_ref[sl, :], v_ref[sl, :], _pick_lane(b_ref, sl, h),
                                                       _pick_lane(gc_ref, sl, h + HEADS), _pick_lane(gt_ref, sl, h + HEADS), gcr_ref[0, c])
            w_ref[sl, :] = w.astype(BF16)
            u_ref[sl, :] = u
            qg_ref[sl, :] = qg.astype(BF16)
            kg_ref[sl, :] = kg.astype(BF16)
            at_ref[0, sl, :] = attn.astype(BF16)
            gl_ref[0, pl.ds(c * 8, 8), :] = gl
            tm_ref[0, sl, :] = tmat

    hb, col, gcrow, square, glb = _prep_specs(rows, gch)
    wide = HEADS * HEAD_DIM
    return pl.pallas_call(
        body,
        name="gdn_prep",
        grid=(t // rows, HEADS),
        in_specs=[hb, hb, hb, col, col, col, gcrow],
        out_specs=[hb, hb, hb, hb, square, glb, square],
        out_shape=[
            jax.ShapeDtypeStruct((t, wide), BF16),
            jax.ShapeDtypeStruct((t, wide), F32),
            jax.ShapeDtypeStruct((t, wide), BF16),
            jax.ShapeDtypeStruct((t, wide), BF16),
            jax.ShapeDtypeStruct((HEADS, t, CHUNK), BF16),
            jax.ShapeDtypeStruct((HEADS, t // CHUNK * 8, HEAD_DIM), F32),
            jax.ShapeDtypeStruct((HEADS, t, CHUNK), F32),
        ],
        compiler_params=_params("parallel", "parallel"),
    )(q, k, v, beta, gc, gt, gcr)


def _gdn_prep_bwd(q, k, v, beta, gc, gt, gcr, tmat, dw, du, dqg, dkg, dattn, dgl):
    t = q.shape[0]
    gch = min(PREP_CHUNKS, t // CHUNK)
    rows = gch * CHUNK

    def body(q_ref, k_ref, v_ref, b_ref, gc_ref, gt_ref, gcr_ref, tm_ref, dw_ref, du_ref, dqg_ref, dkg_ref, dat_ref, dgl_ref,
             dq_ref, dk_ref, dv_ref, db_ref, dgc_ref, dgt_ref, dgcr_ref):
        h = pl.program_id(1)
        lane = _iota((1, LANES), 1)

        @pl.when(h == 0)
        def _():
            db_ref[...] = jnp.zeros_like(db_ref)
            dgc_ref[...] = jnp.zeros_like(dgc_ref)
            dgt_ref[...] = jnp.zeros_like(dgt_ref)

        for c in range(gch):
            sl = pl.ds(c * CHUNK, CHUNK)
            known = tm_ref[0, sl, :]
            _, vjp = jax.vjp(lambda *a: _prep_chunk(*a, tmat=known)[:6], q_ref[sl, :], k_ref[sl, :], v_ref[sl, :], _pick_lane(b_ref, sl, h),
                             _pick_lane(gc_ref, sl, h + HEADS), _pick_lane(gt_ref, sl, h + HEADS), gcr_ref[0, c])
            dq, dk, dv, db, dgc, dgt, dgcr = vjp((dw_ref[sl, :], du_ref[sl, :], dqg_ref[sl, :], dkg_ref[sl, :], dat_ref[0, sl, :], dgl_ref[0, pl.ds(c * 8, 8), :]))
            dq_ref[sl, :] = dq
            dk_ref[sl, :] = dk
            dv_ref[sl, :] = dv
            db_ref[sl, :] += jnp.where(lane == h, db, 0.0)
            dgc_ref[sl, :] += jnp.where(lane == h + HEADS, dgc, 0.0)
            dgt_ref[sl, :] += jnp.where(lane == h + HEADS, dgt, 0.0)
            dgcr_ref[0, c] = dgcr

    hb, col, gcrow, square, glb = _prep_specs(rows, gch)
    wide = HEADS * HEAD_DIM
    return pl.pallas_call(
        body,
        name="gdn_prep_bwd",
        grid=(t // rows, HEADS),
        in_specs=[hb, hb, hb, col, col, col, gcrow, square, hb, hb, hb, hb, square, glb],
        out_specs=[hb, hb, hb, col, col, col, gcrow],
        out_shape=[jax.ShapeDtypeStruct((t, wide), F32)] * 3 + [jax.ShapeDtypeStruct((t, LANES), F32)] * 3 + [jax.ShapeDtypeStruct((HEADS, t // CHUNK, 1, CHUNK), F32)],
        compiler_params=_params("parallel", "arbitrary"),
    )(q, k, v, beta, gc, gt, gcr, tmat, dw, du, dqg, dkg, dattn, dgl)


def _gdn_scan(w, u, qg, kg, attn, gl):
    t = w.shape[0]
    n = t // CHUNK
    wide = HEADS * HEAD_DIM

    def body(w_ref, u_ref, qg_ref, kg_ref, at_ref, gl_ref, o_ref, st_ref, s_ref):
        @pl.when(pl.program_id(0) == 0)
        def _():
            s_ref[...] = jnp.zeros_like(s_ref)

        for h in range(HEADS):
            hs = pl.ds(h * HEAD_DIM, HEAD_DIM)
            s = s_ref[h]
            sb = s.astype(BF16)
            st_ref[0, h] = s
            vn = u_ref[:, hs] - _dot(w_ref[:, hs], sb)
            vb = vn.astype(BF16)
            o_ref[:, hs] = _dot(qg_ref[:, hs], sb) + _dot(at_ref[h], vb)
            s_ref[h] = s * jnp.tile(gl_ref[h], (HEAD_DIM // 8, 1)) + _dot(kg_ref[:, hs], vb, TN)

    row = pl.BlockSpec((CHUNK, wide), lambda i: (i, 0))
    return pl.pallas_call(
        body,
        name="gdn_scan",
        grid=(n,),
        in_specs=[row, row, row, row, pl.BlockSpec((HEADS, CHUNK, CHUNK), lambda i: (0, i, 0)), pl.BlockSpec((HEADS, 8, HEAD_DIM), lambda i: (0, i, 0))],
        out_specs=[row, pl.BlockSpec((1, HEADS, HEAD_DIM, HEAD_DIM), lambda i: (i, 0, 0, 0))],
        out_shape=[jax.ShapeDtypeStruct((t, wide), F32), jax.ShapeDtypeStruct((n, HEADS, HEAD_DIM, HEAD_DIM), F32)],
        scratch_shapes=[pltpu.VMEM((HEADS, HEAD_DIM, HEAD_DIM), F32)],
        compiler_params=_params("arbitrary"),
    )(w, u, qg, kg, attn, gl)


def _gdn_scan_bwd(w, u, qg, kg, attn, gl, states, do):
    t = w.shape[0]
    n = t // CHUNK
    wide = HEADS * HEAD_DIM

    def body(w_ref, u_ref, qg_ref, kg_ref, at_ref, gl_ref, st_ref, do_ref, dw_ref, du_ref, dqg_ref, dkg_ref, dat_ref, dgl_ref, ds_ref):
        @pl.when(pl.program_id(0) == 0)
        def _():
            ds_ref[...] = jnp.zeros_like(ds_ref)

        for h in range(HEADS):
            hs = pl.ds(h * HEAD_DIM, HEAD_DIM)
            s = st_ref[0, h]
            sb = s.astype(BF16)
            wv, qgv, kgv, atv = w_ref[:, hs], qg_ref[:, hs], kg_ref[:, hs], at_ref[h]
            vb = (u_ref[:, hs] - _dot(wv, sb)).astype(BF16)
            dsn = ds_ref[h]
            dsb = dsn.astype(BF16)
            dob = do_ref[:, hs].astype(BF16)
            dvn = _dot(atv, dob, TN) + _dot(kgv, dsb)
            dvb = dvn.astype(BF16)
            dat_ref[h] = _dot(dob, vb, NT)
            dqg_ref[:, hs] = _dot(dob, sb, NT)
            dkg_ref[:, hs] = _dot(vb, dsb, NT)
            du_ref[:, hs] = dvn
            dw_ref[:, hs] = -_dot(dvb, sb, NT)
            dgl_ref[h] = jnp.sum((dsn * s).reshape(HEAD_DIM // 8, 8, HEAD_DIM), axis=0)
            ds_ref[h] = dsn * jnp.tile(gl_ref[h], (HEAD_DIM // 8, 1)) + _dot(qgv, dob, TN) - _dot(wv, dvb, TN)

    row = pl.BlockSpec((CHUNK, wide), lambda i: (n - 1 - i, 0))
    at = pl.BlockSpec((HEADS, CHUNK, CHUNK), lambda i: (0, n - 1 - i, 0))
    glb = pl.BlockSpec((HEADS, 8, HEAD_DIM), lambda i: (0, n - 1 - i, 0))
    return pl.pallas_call(
        body,
        name="gdn_scan_bwd",
        grid=(n,),
        in_specs=[row, row, row, row, at, glb, pl.BlockSpec((1, HEADS, HEAD_DIM, HEAD_DIM), lambda i: (n - 1 - i, 0, 0, 0)), row],
        out_specs=[row, row, row, row, at, glb],
        out_shape=[jax.ShapeDtypeStruct((t, wide), F32)] * 4 + [jax.ShapeDtypeStruct((HEADS, t, CHUNK), F32), jax.ShapeDtypeStruct((HEADS, n * 8, HEAD_DIM), F32)],
        scratch_shapes=[pltpu.VMEM((HEADS, HEAD_DIM, HEAD_DIM), F32)],
        compiler_params=_params("arbitrary"),
    )(w, u, qg, kg, attn, gl, states, do)


SB_Q = 1024
SB_K = 256


def _sb_scores(q, k):
    z = _dot(q, k, NT) * (HEAD_DIM ** -0.5)
    e = jnp.exp(-jnp.abs(z))
    lb = jnp.minimum(z, 0.0) - jnp.log(1.0 + e)
    return z, e, lb, lb - z


def _tri(n, rel):
    return jnp.where(rel(_iota((n, n), 0), _iota((n, n), 1)), 1.0, 0.0).astype(BF16)


def _lanes(col):
    return jnp.broadcast_to(col, (col.shape[0], LANES))


def _sb_fwd(q, k, v):
    t = q.shape[0]
    bq, bk = min(SB_Q, t), min(SB_K, t)
    nsub, rep = bq // bk, bk // LANES

    def body(q_ref, k_ref, v_ref, o_ref, rt_ref):
        i = pl.program_id(1)
        o_ref[...] = jnp.zeros_like(o_ref)
        rt_ref[...] = jnp.zeros_like(rt_ref)
        after = _tri(bk, lambda r, c: r > c)

        def block(j, r0, diag):
            st = pl.multiple_of(j * bk, bk)
            kv, vv = k_ref[pl.ds(st, bk), :], v_ref[pl.ds(st, bk), :]
            _, _, lb, l1m = _sb_scores(q_ref[r0:, :], kv)
            if diag:
                mask = _iota((bq - r0, bk), 1) + j * bk < _iota((bq - r0, bk), 0) + (r0 + i * bq)
                l1m = jnp.where(mask, l1m, 0.0)
            sums = _two_pass(l1m, after)
            run = rt_ref[r0:, :]
            a = jnp.exp(lb + jnp.tile(run, (1, rep)) + sums)
            if diag:
                a = jnp.where(mask, a, 0.0)
            o_ref[r0:, :] += _dot(a.astype(BF16), vv)
            rt_ref[r0:, :] = run + _lanes(sums[:, 0:1] + l1m[:, 0:1])

        def group(p, diag):
            for s in reversed(range(nsub)):
                block(p * nsub + s, s * bk if diag else 0, diag)

        def left_of_diagonal(jj, carry):
            group(i - jj, False)
            return carry

        group(i, True)
        lax.fori_loop(1, i + 1, left_of_diagonal, 0)

    qb = pl.BlockSpec((bq, HEAD_DIM), lambda h, i: (i, h))
    full = pl.BlockSpec((t, HEAD_DIM), lambda h, i: (0, h))
    return pl.pallas_call(
        body,
        name="sb_fwd",
        grid=(HEADS, t // bq),
        in_specs=[qb, full, full],
        out_specs=[qb, qb],
        out_shape=[jax.ShapeDtypeStruct(q.shape, F32), jax.ShapeDtypeStruct(q.shape, F32)],
        compiler_params=_params("parallel", "arbitrary"),
    )(q, k, v)


def _sb_bwd(q, k, v, rt, do):
    t = q.shape[0]
    bq, bk = min(SB_Q, t), min(SB_K, t)
    nsub, rep = bq // bk, bk // LANES
    scale = HEAD_DIM ** -0.5

    def body(q_ref, k_ref, v_ref, rt_ref, do_ref, dq_ref, dk_ref, dv_ref, left_ref, pg_ref):
        i = pl.program_id(1)

        @pl.when(i == 0)
        def _():
            dk_ref[...] = jnp.zeros_like(dk_ref)
            dv_ref[...] = jnp.zeros_like(dv_ref)

        dq_ref[...] = jnp.zeros_like(dq_ref)
        left_ref[...] = jnp.zeros_like(left_ref)
        pg_ref[...] = jnp.zeros_like(pg_ref)
        upto = _tri(bk, lambda r, c: r <= c)
        before = _tri(bk, lambda r, c: r < c)

        def block(j, r0, diag):
            st = pl.multiple_of(j * bk, bk)
            kv, vv = k_ref[pl.ds(st, bk), :], v_ref[pl.ds(st, bk), :]
            qv = q_ref[r0:, :]
            dob = do_ref[r0:, :].astype(BF16)
            z, e, lb, l1m = _sb_scores(qv, kv)
            if diag:
                mask = _iota((bq - r0, bk), 1) + j * bk < _iota((bq - r0, bk), 0) + (r0 + i * bq)
                l1m = jnp.where(mask, l1m, 0.0)
            beta = jnp.where(z >= 0.0, 1.0, e) / (1.0 + e)
            sums = _two_pass(l1m, upto)
            left = left_ref[r0:, :]
            a = jnp.exp(lb + jnp.tile(rt_ref[r0:, :] - left, (1, rep)) - sums)
            if diag:
                a = jnp.where(mask, a, 0.0)
            g = _dot(dob, vv, NT) * a
            dv_ref[pl.ds(st, bk), :] += _dot(a.astype(BF16), dob, TN)
            gsum = _two_pass(g, before)
            pg = pg_ref[r0:, :]
            dz = g * (1.0 - beta) - (jnp.tile(pg, (1, rep)) + gsum) * beta
            if diag:
                dz = jnp.where(mask, dz, 0.0)
            dzb = (dz * scale).astype(BF16)
            dk_ref[pl.ds(st, bk), :] += _dot(dzb, qv, TN)
            dq_ref[r0:, :] += _dot(dzb, kv)
            left_ref[r0:, :] = left + _lanes(sums[:, bk - 1:bk])
            pg_ref[r0:, :] = pg + _lanes(gsum[:, bk - 1:bk] + g[:, bk - 1:bk])

        def group(p, diag):
            for s in range(nsub):
                block(p * nsub + s, s * bk if diag else 0, diag)

        def left_of_diagonal(p, carry):
            group(p, False)
            return carry

        lax.fori_loop(0, i, left_of_diagonal, 0)
        group(i, True)

    qb = pl.BlockSpec((bq, HEAD_DIM), lambda h, i: (i, h))
    full = pl.BlockSpec((t, HEAD_DIM), lambda h, i: (0, h))
    return pl.pallas_call(
        body,
        name="sb_bwd",
        grid=(HEADS, t // bq),
        in_specs=[qb, full, full, qb, qb],
        out_specs=[qb, full, full],
        out_shape=[jax.ShapeDtypeStruct(q.shape, F32)] * 3,
        scratch_shapes=[pltpu.VMEM((bq, LANES), F32), pltpu.VMEM((bq, LANES), F32)],
        compiler_params=_params("parallel", "arbitrary"),
    )(q, k, v, rt, do)


def _adamw(w, g, m, v, name, tm=256):
    r, c = w.shape
    tm = tm if r % tm == 0 else r

    def body(w_ref, g_ref, m_ref, v_ref, d_ref, nm_ref, nv_ref):
        gv = g_ref[...]
        nm = ADAM_B1 * m_ref[...] + (1.0 - ADAM_B1) * gv
        nv = ADAM_B2 * v_ref[...] + (1.0 - ADAM_B2) * (gv * gv)
        m_hat = nm / (1.0 - ADAM_B1 ** ADAM_STEP)
        v_hat = nv / (1.0 - ADAM_B2 ** ADAM_STEP)
        d_ref[...] = -ADAM_LR * (m_hat / (jnp.sqrt(v_hat) + ADAM_EPS) + ADAM_WD * w_ref[...])
        nm_ref[...] = nm
        nv_ref[...] = nv

    blk = pl.BlockSpec((tm, c), lambda i: (i, 0))
    return pl.pallas_call(
        body,
        name=name,
        grid=(r // tm,),
        in_specs=[blk] * 4,
        out_specs=[blk] * 3,
        out_shape=[jax.ShapeDtypeStruct((r, c), F32)] * 3,
        compiler_params=_params("parallel"),
    )(w, g, m, v)


def _local_step(x, tgt, gains, wts, small):
    mix_pre, mix_post, mlp_pre, mlp_post, kv_gain = gains
    w_qkvg, w_ba, w_out, w_kv, w_q, w_o, w_up, w_down = wts
    conv_w, a_log, dt_bias, out_gain = small
    t, d = x.shape
    row = lambda a, i=None: a[i:i + 1] if i is not None else a
    al = jnp.zeros((1, LANES), F32).at[:, HEADS:2 * HEADS].set(a_log)
    dtb = jnp.zeros((1, LANES), F32).at[:, HEADS:2 * HEADS].set(dt_bias)
    og = jnp.tile(out_gain, (1, HEADS))
    full = lambda a: (a, a.shape[1], 0)

    (h0,) = _rowwise("norm_in", _fn_norm, [full(x)], [row(mix_pre, 0)], [(d, BF16)])
    qkvg = _matmul(h0, w_qkvg, "nn", F32, "mm_gdn_in", tk=1024)
    ba = _matmul(h0, w_ba, "nn", F32, "mm_gdn_ba", tk=1024)
    conv = _conv_fwd(qkvg, conv_w)
    conv_qkv = [(conv, d, 0), (conv, d, 1), (conv, d, 2)]
    gq, gk, gv = _rowwise("post_conv", _fn_post, conv_qkv, [], [(d, F32)] * 3)
    beta, gc, gt = _rowwise("gates", _fn_gates, [full(ba)], [al, dtb], [(LANES, F32)] * 3)
    gcr = jnp.swapaxes(gc[:, HEADS:2 * HEADS], 0, 1).reshape(HEADS, t // CHUNK, 1, CHUNK)
    pw, pu, pqg, pkg, pattn, pgl, ptm = _gdn_prep(gq, gk, gv, beta, gc, gt, gcr)
    o_gdn, states = _gdn_scan(pw, pu, pqg, pkg, pattn, pgl)
    (on,) = _rowwise("out_norm", _fn_outnorm, [full(o_gdn), (qkvg, d, 3)], [og], [(d, BF16)])
    mix0 = _matmul(on, w_out, "nn", F32, "mm_gdn_out", tk=1024)
    x1, h1 = _rowwise("res_a0", _fn_res_norm, [full(x), full(mix0)], [row(mix_post, 0), row(mlp_pre, 0)], [(d, F32), (d, BF16)])
    u0 = _matmul(h1, w_up[0], "nn", F32, "mm_up0", tk=1024)
    (a0,) = _rowwise("relu2_0", _fn_relu2, [full(u0)], [], [(D_FF, BF16)])
    d0 = _matmul(a0, w_down[0], "nn", F32, "mm_down0")
    x2, hkv, hq = _rowwise("res_b0", _fn_res_norm2, [full(x1), full(d0)], [row(mlp_post, 0), kv_gain, row(mix_pre, 1)], [(d, F32), (d, BF16), (d, BF16)])
    w_k, w_v = w_kv[:, :d], w_kv[:, d:]
    kp = _matmul(hkv, w_k, "nn", BF16, "mm_k", tk=1024)
    vp = _matmul(hkv, w_v, "nn", BF16, "mm_v", tk=1024)
    qp = _matmul(hq, w_q, "nn", BF16, "mm_q", tk=1024)
    o_sb, rt = _sb_fwd(qp, kp, vp)
    mix1 = _matmul(o_sb, w_o, "nn", F32, "mm_sb_out", tk=1024)
    x3, h3 = _rowwise("res_a1", _fn_res_norm, [full(x2), full(mix1)], [row(mix_post, 1), row(mlp_pre, 1)], [(d, F32), (d, BF16)])
    u1 = _matmul(h3, w_up[1], "nn", F32, "mm_up1", tk=1024)
    (a1,) = _rowwise("relu2_1", _fn_relu2, [full(u1)], [], [(D_FF, BF16)])
    d1 = _matmul(a1, w_down[1], "nn", F32, "mm_down1")

    loss, dx3, dd1, g_mlp_post1 = _loss_call(x3, d1, tgt, row(mlp_post, 1))
    da1 = _matmul(dd1, w_down[1], "nt", F32, "mm_down1_dx")
    g_down1 = _matmul(a1, dd1, "tn", F32, "mm_down1_dw")
    (du1,), _ = _rowwise_bwd("relu2_1_bwd", _fn_relu2, [full(u1)], [], [da1], [BF16])
    dh3 = _matmul(du1, w_up[1], "nt", F32, "mm_up1_dx")
    g_up1 = _matmul(h3, du1, "tn", F32, "mm_up1_dw")
    (dx2, dmix1), (g_mix_post1, g_mlp_pre1) = _rowwise_bwd(
        "res_a1_bwd", _fn_res_norm, [full(x2), full(mix1)], [row(mix_post, 1), row(mlp_pre, 1)], [dx3, dh3], [F32, BF16])
    do_sb = _matmul(dmix1, w_o, "nt", BF16, "mm_sb_out_dx")
    g_o = _matmul(o_sb, dmix1, "tn", F32, "mm_sb_out_dw")
    dqp, dkp, dvp = _sb_bwd(qp, kp, vp, rt, do_sb)
    dhq = _matmul(dqp, w_q, "nt", F32, "mm_q_dx")
    g_q = _matmul(hq, dqp, "tn", F32, "mm_q_dw")
    dhkv = _matmul(dvp, w_v, "nt", F32, "mm_v_dx", add=_matmul(dkp, w_k, "nt", F32, "mm_k_dx"))
    g_kv = jnp.concatenate([_matmul(hkv, dkp, "tn", F32, "mm_k_dw"), _matmul(hkv, dvp, "tn", F32, "mm_v_dw")], axis=1)
    (dx1, dd0), (g_mlp_post0, g_kv_gain, g_mix_pre1) = _rowwise_bwd(
        "res_b0_bwd", _fn_res_norm2, [full(x1), full(d0)], [row(mlp_post, 0), kv_gain, row(mix_pre, 1)], [dx2, dhkv, dhq], [F32, BF16])
    da0 = _matmul(dd0, w_down[0], "nt", F32, "mm_down0_dx")
    g_down0 = _matmul(a0, dd0, "tn", F32, "mm_down0_dw")
    (du0,), _ = _rowwise_bwd("relu2_0_bwd", _fn_relu2, [full(u0)], [], [da0], [BF16])
    dh1 = _matmul(du0, w_up[0], "nt", F32, "mm_up0_dx")
    g_up0 = _matmul(h1, du0, "tn", F32, "mm_up0_dw")
    (dx0, dmix0), (g_mix_post0, g_mlp_pre0) = _rowwise_bwd(
        "res_a0_bwd", _fn_res_norm, [full(x), full(mix0)], [row(mix_post, 0), row(mlp_pre, 0)], [dx1, dh1], [F32, BF16])
    don = _matmul(dmix0, w_out, "nt", F32, "mm_gdn_out_dx")
    g_out = _matmul(on, dmix0, "tn", F32, "mm_gdn_out_dw")
    (do_gdn, dgate), (g_og,) = _rowwise_bwd("out_norm_bwd", _fn_outnorm, [full(o_gdn), (qkvg, d, 3)], [og], [don], [F32, F32])
    dpw, dpu, dpqg, dpkg, dpattn, dpgl = _gdn_scan_bwd(pw, pu, pqg, pkg, pattn, pgl, states, do_gdn)
    dgq, dgk, dgv, dbeta, dgc, dgt, dgcr = _gdn_prep_bwd(gq, gk, gv, beta, gc, gt, gcr, ptm, dpw, dpu, dpqg, dpkg, dpattn, dpgl)
    dgcr_lanes = jnp.pad(jnp.swapaxes(dgcr.reshape(HEADS, t), 0, 1), ((0, 0), (HEADS, LANES - 2 * HEADS)))
    gate_cots = [dbeta, dgc + dgcr_lanes, dgt]
    (dba,), (g_al, g_dtb) = _rowwise_bwd("gates_bwd", _fn_gates, [full(ba)], [al, dtb], gate_cots, [BF16])
    (dconv,), _ = _rowwise_bwd("post_conv_bwd", _fn_post, conv_qkv, [], [dgq, dgk, dgv], [F32] * 3, joined=True)
    dqkvg, g_conv = _conv_bwd(dconv, dgate, qkvg, conv_w)
    dh0b = _matmul(dba, w_ba, "nt", F32, "mm_gdn_ba_dx", tk=LANES)
    dh0 = _matmul(dqkvg, w_qkvg, "nt", F32, "mm_gdn_in_dx", add=dh0b)
    g_qkvg = _matmul(h0, dqkvg, "tn", F32, "mm_gdn_in_dw")
    g_ba = _matmul(h0, dba, "tn", F32, "mm_gdn_ba_dw")
    (grad_x,), (g_mix_pre0,) = _rowwise_bwd("norm_in_bwd", lambda xx, gg: (_rms(xx, gg), xx), [full(x)], [row(mix_pre, 0)], [dh0, dx0], [F32])

    grads = dict(
        mix_pre_gain=jnp.concatenate([g_mix_pre0, g_mix_pre1], axis=0),
        mix_post_gain=jnp.concatenate([g_mix_post0, g_mix_post1], axis=0),
        mlp_pre_gain=jnp.concatenate([g_mlp_pre0, g_mlp_pre1], axis=0),
        mlp_post_gain=jnp.concatenate([g_mlp_post0, g_mlp_post1], axis=0),
        mlp_w_up=jnp.stack([g_up0, g_up1]),
        mlp_w_down=jnp.stack([g_down0, g_down1]),
        gdn_w_in=jnp.concatenate([g_qkvg, g_ba[:, :2 * HEADS]], axis=1)[None],
        gdn_conv_w=g_conv[None, :CONV_K],
        gdn_a_log=g_al[:, HEADS:2 * HEADS],
        gdn_dt_bias=g_dtb[:, HEADS:2 * HEADS],
        gdn_out_gain=jnp.sum(g_og.reshape(HEADS, HEAD_DIM), axis=0, keepdims=True),
        gdn_w_out=g_out[None],
        kv_gain=g_kv_gain[0],
        w_kv=g_kv,
        sb_w_q=g_q[None],
        sb_w_o=g_o[None],
    )
    return loss, grad_x, grads


N_DEV = 8
N_CHIPS = 4
PACK_COLS = 1024
PACK_ROW_TILE = 256

_HBM = pl.BlockSpec(memory_space=pltpu.HBM)


def _place():
    return lax.axis_index("x"), lax.axis_index("y"), lax.axis_index("c")


def _other_chips(x, y):
    return [(1 - x, y), (x, 1 - y), (1 - x, 1 - y)]


def _remote(src, dst, send_sem, recv_sem, to):
    return pltpu.make_async_remote_copy(src_ref=src, dst_ref=dst, send_sem=send_sem, recv_sem=recv_sem, device_id=to, device_id_type=MESH)


def _gather8(v, name):
    rows, cols = v.shape

    def body(v_ref, out_ref, sum_ref, send_sems, recv_sems, local_sem):
        x, y, c = _place()
        me, sibling = (x, y, c), (x, y, 1 - c)
        chips = _other_chips(x, y)

        def blk(px, py, pc):
            return out_ref.at[pl.ds((4 * px + 2 * py + pc) * rows, rows), :]

        def copy(k, block, to, src=None):
            return _remote(blk(*block) if src is None else src, blk(*block), send_sems.at[k], recv_sems.at[k], to)

        mine = pltpu.make_async_copy(v_ref, blk(*me), local_sem)
        mine.start()
        first = [copy(0, me, sibling, src=v_ref)] + [copy(1 + j, me, (*chip, c), src=v_ref) for j, chip in enumerate(chips)]
        for cp in first:
            cp.start()
        passed = [copy(4 + j, (*chip, c), sibling) for j, chip in enumerate(chips)]
        for j, chip in enumerate(chips):
            copy(1 + j, (*chip, c), me).wait_recv()
            passed[j].start()
        copy(0, sibling, me).wait_recv()
        for j, chip in enumerate(chips):
            copy(4 + j, (*chip, 1 - c), me).wait_recv()
        for cp in first + passed:
            cp.wait_send()
        mine.wait()
        acc = out_ref[pl.ds(0, rows), :]
        for dev in range(1, N_DEV):
            acc = acc + out_ref[pl.ds(dev * rows, rows), :]
        sum_ref[...] = acc

    vm = pl.BlockSpec(memory_space=pltpu.VMEM)
    return pl.pallas_call(
        body,
        name=name,
        out_shape=[jax.ShapeDtypeStruct((N_DEV * rows, cols), v.dtype), jax.ShapeDtypeStruct((rows, cols), v.dtype)],
        in_specs=[vm],
        out_specs=[vm, vm],
        scratch_shapes=[pltpu.SemaphoreType.DMA((7,)), pltpu.SemaphoreType.DMA((7,)), pltpu.SemaphoreType.DMA],
    )(v)


def _hbm_call(body, name, arrs, out_shapes, sem_counts):
    n = len(arrs)

    def wrapped(*refs):
        body(refs[:n], refs[n:2 * n], *refs[2 * n:])

    return pl.pallas_call(
        wrapped,
        name=name,
        out_shape=[jax.ShapeDtypeStruct(s, a.dtype) for s, a in zip(out_shapes, arrs)],
        in_specs=[_HBM] * n,
        out_specs=[_HBM] * n,
        scratch_shapes=[pltpu.SemaphoreType.DMA((k,)) for k in sem_counts],
    )(*arrs)


def _gather_weights(arrs):
    n = len(arrs)

    def body(w_refs, out_refs, send_sems, recv_sems, fsend_sems, frecv_sems):
        x, y, c = _place()
        chips = _other_chips(x, y)
        s_me = 2 * x + y
        pairs = list(zip(w_refs, out_refs))
        first = [_remote(w.at[c], o.at[s_me, c], send_sems.at[3 * a + j], recv_sems.at[3 * a + j], (px, py, c))
                 for a, (w, o) in enumerate(pairs) for j, (px, py) in enumerate(chips)]
        for cp in first:
            cp.start()
        passed = []
        for a, (w, o) in enumerate(pairs):
            for j, (px, py) in enumerate(chips):
                half = o.at[2 * px + py, c]
                _remote(half, half, send_sems.at[3 * a + j], recv_sems.at[3 * a + j], (px, py, c)).wait_recv()
                fwd = _remote(half, half, fsend_sems.at[3 * a + j], frecv_sems.at[3 * a + j], (x, y, 1 - c))
                fwd.start()
                passed.append(fwd)
        for a, (w, o) in enumerate(pairs):
            for j, (px, py) in enumerate(chips):
                half = o.at[2 * px + py, 1 - c]
                _remote(half, half, fsend_sems.at[3 * a + j], frecv_sems.at[3 * a + j], (x, y, 1 - c)).wait_recv()
        for cp in first + passed:
            cp.wait_send()

    return _hbm_call(body, "gather_weights", arrs, [(N_CHIPS,) + a.shape for a in arrs], [3 * n] * 4)


def _swap_halves(arrs):
    n = len(arrs)

    def body(g_refs, a_refs, send_sems, recv_sems):
        x, y, c = _place()
        cps = [_remote(g.at[1 - c], a, send_sems.at[i], recv_sems.at[i], (x, y, 1 - c)) for i, (g, a) in enumerate(zip(g_refs, a_refs))]
        for cp in cps:
            cp.start()
        for cp in cps:
            cp.wait()

    return _hbm_call(body, "grads_to_sibling", arrs, [a.shape[1:] for a in arrs], [n, n])


def _scatter_to_chips(arrs):
    n = len(arrs)

    def body(p_refs, b_refs, send_sems, recv_sems):
        x, y, c = _place()
        cps = [_remote(p.at[2 * px + py], b.at[j], send_sems.at[3 * i + j], recv_sems.at[3 * i + j], (px, py, c))
               for i, (p, b) in enumerate(zip(p_refs, b_refs)) for j, (px, py) in enumerate(_other_chips(x, y))]
        for cp in cps:
            cp.start()
        for cp in cps:
            cp.wait()

    return _hbm_call(body, "grads_to_chips", arrs, [(3,) + a.shape[1:] for a in arrs], [3 * n, 3 * n])


def _share_halves(arrs):
    n = len(arrs)

    def body(q_refs, out_refs, send_sems, recv_sems):
        x, y, c = _place()
        cps = [_remote(q, o, send_sems.at[i], recv_sems.at[i], (x, y, 1 - c)) for i, (q, o) in enumerate(zip(q_refs, out_refs))]
        for cp in cps:
            cp.start()
        for cp in cps:
            cp.wait()

    return _hbm_call(body, "grads_share", arrs, [a.shape for a in arrs], [n, n])


_BIG = (
    ("mlp_w_up", (2, 1024, 1024), "cols"),
    ("mlp_w_down", (2, 1024, 1024), "rows"),
    ("gdn_w_out", (1, 256, 1024), "rows"),
    ("w_kv", (1024, 512), "cols"),
    ("sb_w_q", (1, 256, 1024), "rows"),
    ("sb_w_o", (1, 256, 1024), "rows"),
)
_W_IN_SHARD = (1, 1024, 1028)


def _numel(shape):
    n = 1
    for s in shape:
        n *= s
    return n


_PACK_LEN = sum(_numel(s) for _, s, _ in _BIG)
_PACK_ROWS = -(-_PACK_LEN // (2 * PACK_COLS * PACK_ROW_TILE)) * PACK_ROW_TILE
_PACK_PAD = 2 * _PACK_ROWS * PACK_COLS - _PACK_LEN


def _pack_shards(shards, dtype):
    flat = jnp.concatenate([shards[n].astype(dtype).reshape(-1) for n, _, _ in _BIG] + [jnp.zeros((_PACK_PAD,), dtype)])
    return flat.reshape(2, _PACK_ROWS, PACK_COLS)


def _unpack_shards(packed):
    flat = packed.reshape(-1)
    out, off = {}, 0
    for n, shape, _ in _BIG:
        out[n] = flat[off:off + _numel(shape)].reshape(shape)
        off += _numel(shape)
    return out


def _join(stacked, how):
    nd = stacked.ndim - 1
    ax = nd - 1 if how == "cols" else nd - 2
    moved = jnp.moveaxis(stacked, 0, ax)
    shape = list(stacked.shape[1:])
    shape[ax] *= N_CHIPS
    return moved.reshape(shape)


def _split(full, shard_shape, how):
    nd = len(shard_shape)
    ax = nd - 1 if how == "cols" else nd - 2
    shape = list(shard_shape)
    shape.insert(ax, N_CHIPS)
    return jnp.moveaxis(full.reshape(shape), ax, 0)


def _unpack_full(gathered):
    flat = gathered.reshape(N_CHIPS, -1)
    out, off = {}, 0
    for n, shape, how in _BIG:
        out[n] = _join(flat[:, off:off + _numel(shape)].reshape((N_CHIPS,) + shape), how)
        off += _numel(shape)
    return out


def _pack_full(full):
    flat = jnp.concatenate([_split(full[n], shape, how).reshape(N_CHIPS, -1) for n, shape, how in _BIG] + [jnp.zeros((N_CHIPS, _PACK_PAD), F32)], axis=1)
    return jnp.swapaxes(flat.reshape(N_CHIPS, 2, _PACK_ROWS, PACK_COLS), 0, 1).reshape(2, N_CHIPS * _PACK_ROWS, PACK_COLS)


_SMALL = (
    ("mix_pre_gain", (2, 1024)),
    ("mix_post_gain", (2, 1024)),
    ("mlp_pre_gain", (2, 1024)),
    ("mlp_post_gain", (2, 1024)),
    ("kv_gain", (1024,)),
    ("gdn_out_gain", (1, 128)),
    ("gdn_a_log", (1, 8)),
    ("gdn_dt_bias", (1, 8)),
    ("gdn_conv_w", (1, 4, 3072)),
    ("loss", ()),
)


def _rows_of(shape):
    return -(-_numel(shape) // LANES)


_SMALL_ROWS = -(-sum(_rows_of(s) for _, s in _SMALL) // 8) * 8


def _pack_small(vals):
    parts = []
    for n, shape in _SMALL:
        flat = vals[n].reshape(-1)
        parts.append(jnp.pad(flat, (0, _rows_of(shape) * LANES - flat.shape[0])))
    flat = jnp.concatenate(parts)
    return jnp.pad(flat, (0, _SMALL_ROWS * LANES - flat.shape[0])).reshape(_SMALL_ROWS, LANES)


def _unpack_small(packed):
    flat = packed.reshape(-1)
    out, off = {}, 0
    for n, shape in _SMALL:
        out[n] = flat[off:off + _numel(shape)].reshape(shape)
        off += _rows_of(shape) * LANES
    return out


_WEIGHTS = ("mix_pre_gain", "mix_post_gain", "mlp_pre_gain", "mlp_post_gain", "mlp_w_up", "mlp_w_down", "gdn_w_in", "gdn_conv_w",
            "gdn_a_log", "gdn_dt_bias", "gdn_out_gain", "gdn_w_out", "kv_gain", "w_kv", "sb_w_q", "sb_w_o")


def _as2d(a):
    return a.reshape(1, -1) if a.ndim <= 1 else a.reshape(-1, a.shape[-1])


def kernel(x, mix_pre_gain, mix_post_gain, mlp_pre_gain, mlp_post_gain, mlp_w_up, mlp_w_down, gdn_w_in, gdn_conv_w, gdn_a_log, gdn_dt_bias, gdn_out_gain, gdn_w_out, kv_gain, w_kv, sb_w_q, sb_w_o, loss_target, m_mix_pre_gain, m_mix_post_gain, m_mlp_pre_gain, m_mlp_post_gain, m_mlp_w_up, m_mlp_w_down, m_gdn_w_in, m_gdn_conv_w, m_gdn_a_log, m_gdn_dt_bias, m_gdn_out_gain, m_gdn_w_out, m_kv_gain, m_w_kv, m_sb_w_q, m_sb_w_o, v_mix_pre_gain, v_mix_post_gain, v_mlp_pre_gain, v_mlp_post_gain, v_mlp_w_up, v_mlp_w_down, v_gdn_w_in, v_gdn_conv_w, v_gdn_a_log, v_gdn_dt_bias, v_gdn_out_gain, v_gdn_w_out, v_kv_gain, v_w_kv, v_sb_w_q, v_sb_w_o):
    w = dict(mix_pre_gain=mix_pre_gain, mix_post_gain=mix_post_gain, mlp_pre_gain=mlp_pre_gain, mlp_post_gain=mlp_post_gain, mlp_w_up=mlp_w_up, mlp_w_down=mlp_w_down, gdn_w_in=gdn_w_in, gdn_conv_w=gdn_conv_w, gdn_a_log=gdn_a_log, gdn_dt_bias=gdn_dt_bias, gdn_out_gain=gdn_out_gain, gdn_w_out=gdn_w_out, kv_gain=kv_gain, w_kv=w_kv, sb_w_q=sb_w_q, sb_w_o=sb_w_o)
    m = dict(mix_pre_gain=m_mix_pre_gain, mix_post_gain=m_mix_post_gain, mlp_pre_gain=m_mlp_pre_gain, mlp_post_gain=m_mlp_post_gain, mlp_w_up=m_mlp_w_up, mlp_w_down=m_mlp_w_down, gdn_w_in=m_gdn_w_in, gdn_conv_w=m_gdn_conv_w, gdn_a_log=m_gdn_a_log, gdn_dt_bias=m_gdn_dt_bias, gdn_out_gain=m_gdn_out_gain, gdn_w_out=m_gdn_w_out, kv_gain=m_kv_gain, w_kv=m_w_kv, sb_w_q=m_sb_w_q, sb_w_o=m_sb_w_o)
    v = dict(mix_pre_gain=v_mix_pre_gain, mix_post_gain=v_mix_post_gain, mlp_pre_gain=v_mlp_pre_gain, mlp_post_gain=v_mlp_post_gain, mlp_w_up=v_mlp_w_up, mlp_w_down=v_mlp_w_down, gdn_w_in=v_gdn_w_in, gdn_conv_w=v_gdn_conv_w, gdn_a_log=v_gdn_a_log, gdn_dt_bias=v_gdn_dt_bias, gdn_out_gain=v_gdn_out_gain, gdn_w_out=v_gdn_w_out, kv_gain=v_kv_gain, w_kv=v_w_kv, sb_w_q=v_sb_w_q, sb_w_o=v_sb_w_o)
    cx, cy, cc = _place()
    chip = 2 * cx + cy
    conv_cols = gdn_conv_w.shape[-1]

    in_rows = _W_IN_SHARD[1] // 2
    own = (_pack_shards(w, BF16), gdn_w_in.astype(BF16).reshape(2, in_rows, _W_IN_SHARD[2]))
    packed_all, w_in_all = [lax.dynamic_update_index_in_dim(others, mine, chip, 0) for others, mine in zip(_gather_weights(own), own)]
    full = _unpack_full(packed_all)
    conv_rows = jnp.pad(gdn_conv_w[0], ((0, 8 - CONV_K), (0, 0))).reshape(-1, LANES)
    conv_all, _ = _gather8(conv_rows, "gather_conv_w")
    conv_all = conv_all.reshape(N_CHIPS, 2, 8, conv_cols)[:, 0, :CONV_K]
    conv_full = jnp.swapaxes(conv_all, 0, 1).reshape(CONV_K, N_CHIPS * conv_cols)

    w_in = _join(w_in_all.reshape((N_CHIPS,) + _W_IN_SHARD), "cols")[0]
    wts = (w_in[:, :4 * HEADS * HEAD_DIM], jnp.pad(w_in[:, 4 * HEADS * HEAD_DIM:], ((0, 0), (0, LANES - 2 * HEADS))), full["gdn_w_out"][0], full["w_kv"],
           full["sb_w_q"][0], full["sb_w_o"][0], full["mlp_w_up"], full["mlp_w_down"])
    gains = (mix_pre_gain, mix_post_gain, mlp_pre_gain, mlp_post_gain, kv_gain[None])
    small = (conv_full, gdn_a_log, gdn_dt_bias, gdn_out_gain)
    loss_rows, grad_x, g_full = _local_step(x[0], loss_target[0], gains, wts, small)

    g_in = _split(g_full["gdn_w_in"], _W_IN_SHARD, "cols").reshape(N_CHIPS, 2, in_rows, _W_IN_SHARD[2])
    bufs = (_pack_full(g_full), jnp.swapaxes(g_in, 0, 1).reshape(2, N_CHIPS * in_rows, _W_IN_SHARD[2]))
    from_sibling = _swap_halves(bufs)
    partial, partial_bf16 = [], []
    for i, (buf, other) in enumerate(zip(bufs, from_sibling)):
        cols = buf.shape[-1]
        own_half = lax.dynamic_index_in_dim(buf, cc, 0, keepdims=False)
        p, pb = _rowwise(f"grads_add_sibling_{i}", lambda a, b: (a + b, a + b), [(own_half, cols, 0), (other, cols, 0)], [], [(cols, F32), (cols, BF16)], tm=PACK_ROW_TILE)
        partial.append(p.reshape(N_CHIPS, -1, cols))
        partial_bf16.append(pb.reshape(N_CHIPS, -1, cols))
    from_chips = _scatter_to_chips(tuple(partial_bf16))
    reduced = []
    for i, (p, others) in enumerate(zip(partial, from_chips)):
        cols = p.shape[-1]
        mine = lax.dynamic_index_in_dim(p, chip, 0, keepdims=False)
        (r,) = _rowwise(f"grads_add_chips_{i}", lambda a, b, c, d: (((a + b) + c) + d,),
                        [(mine, cols, 0), (others[0], cols, 0), (others[1], cols, 0), (others[2], cols, 0)], [], [(cols, F32)], tm=PACK_ROW_TILE)
        reduced.append(r)
    both = [jnp.where(cc == 0, jnp.stack([r, o]), jnp.stack([o, r])) for r, o in zip(reduced, _share_halves(tuple(reduced)))]
    g_shard = _unpack_shards(both[0])
    g_shard["gdn_w_in"] = both[1].reshape(_W_IN_SHARD)

    g_small_local = {n: g_full[n] for n, _ in _SMALL if n != "loss"}
    g_small_local["loss"] = loss_rows[0, 0]
    _, small_sum = _gather8(_pack_small(g_small_local), "allreduce_small")
    g_small = _unpack_small(small_sum)
    loss = g_small.pop("loss")
    g_small["gdn_conv_w"] = lax.dynamic_slice_in_dim(g_small["gdn_conv_w"], chip * conv_cols, conv_cols, axis=2)

    grads = {**g_shard, **g_small}
    deltas, new_m, new_v = {}, {}, {}
    for n in _WEIGHTS:
        d2, m2, v2 = _adamw(_as2d(w[n]), _as2d(grads[n]), _as2d(m[n]), _as2d(v[n]), "adamw_" + n)
        deltas[n], new_m[n], new_v[n] = d2.reshape(w[n].shape), m2.reshape(w[n].shape), v2.reshape(w[n].shape)
    return (loss, grad_x[None], *[grads[n].reshape(w[n].shape) for n in _WEIGHTS], *[deltas[n] for n in _WEIGHTS],
            *[new_m[n] for n in _WEIGHTS], *[new_v[n] for n in _WEIGHTS])
```

```python
import functools

import jax
import jax.numpy as jnp
from jax import lax
from jax.experimental import pallas as pl
from jax.experimental.pallas import tpu as pltpu

F32, BF16 = jnp.float32, jnp.bfloat16
HI = lax.Precision.HIGHEST
MESH = pl.DeviceIdType.MESH

EPS = 1e-6
D_MODEL = 1024
HEADS = 8
HEAD_DIM = 128
CHUNK = 64
CHUNK_SHIFT = CHUNK.bit_length() - 1
CONV_K = 4
D_FF = 4096
QKV = 3 * HEADS * HEAD_DIM

ADAM_LR, ADAM_B1, ADAM_B2, ADAM_EPS, ADAM_WD, ADAM_STEP = 0.001, 0.9, 0.999, 1e-08, 0.01, 10

VMEM_LIMIT_BYTES = 48 * 1024 * 1024
LANES = 128

NN = ((1,), (0,))
NT = ((1,), (1,))
TN = ((0,), (0,))


def _dot(a, b, dims=NN, precision=None):
    return lax.dot_general(a, b, (dims, ((), ())), precision=precision, preferred_element_type=F32)


def _params(*sem):
    return pltpu.CompilerParams(dimension_semantics=sem, vmem_limit_bytes=VMEM_LIMIT_BYTES)


def _iota(shape, axis):
    return lax.broadcasted_iota(jnp.int32, shape, axis)


def _matmul(a, b, mode, out_dtype, name, tm=1024, tn=1024, tk=512, add=None, epilogue=None, extras=()):
    if mode == "nn":
        (m, k), (k2, n) = a.shape, b.shape
    elif mode == "nt":
        (m, k), (n, k2) = a.shape, b.shape
    else:
        (k, m), (k2, n) = a.shape, b.shape
    assert k == k2, (a.shape, b.shape, mode)
    tm, tn, tk = min(tm, m), min(tn, n), min(tk, k)
    assert m % tm == 0 and n % tn == 0 and k % tk == 0, (a.shape, b.shape, mode)
    nk = k // tk
    dims = {"nn": NN, "nt": NT, "tn": TN}[mode]
    tiles = ([add] if add is not None else []) + list(extras)
    out_dtypes = out_dtype if epilogue is not None else (out_dtype,)
    n_in = 2 + len(tiles)

    def body(*refs):
        a_ref, b_ref = refs[:2]
        extra_refs = refs[n_in - len(extras):n_in]
        o_refs, acc_ref = refs[n_in:-1], refs[-1]
        kk = pl.program_id(2)

        @pl.when(kk == 0)
        def _():
            acc_ref[...] = refs[2][...].astype(F32) if add is not None else jnp.zeros_like(acc_ref)

        acc_ref[...] += _dot(a_ref[...].astype(BF16), b_ref[...].astype(BF16), dims)

        @pl.when(kk == nk - 1)
        def _():
            res = (acc_ref[...],) if epilogue is None else epilogue(acc_ref[...], *[r[...] for r in extra_refs])
            for o_ref, r in zip(o_refs, res):
                o_ref[...] = r.astype(o_ref.dtype)

    a_spec = pl.BlockSpec((tk, tm), lambda i, j, kk: (kk, i)) if mode == "tn" else pl.BlockSpec((tm, tk), lambda i, j, kk: (i, kk))
    b_spec = pl.BlockSpec((tn, tk), lambda i, j, kk: (j, kk)) if mode == "nt" else pl.BlockSpec((tk, tn), lambda i, j, kk: (kk, j))
    o_spec = pl.BlockSpec((tm, tn), lambda i, j, kk: (i, j))
    res = pl.pallas_call(
        body,
        name=name,
        grid=(m // tm, n // tn, nk),
        in_specs=[a_spec, b_spec] + [o_spec] * len(tiles),
        out_specs=[o_spec] * len(out_dtypes),
        out_shape=[jax.ShapeDtypeStruct((m, n), dt) for dt in out_dtypes],
        scratch_shapes=[pltpu.VMEM((tm, tn), F32)],
        compiler_params=_params("parallel", "parallel", "arbitrary"),
    )(a, b, *tiles)
    return res if epilogue is not None else res[0]


def _row_specs(rows, tm):
    return [pl.BlockSpec((tm, w), lambda i, cb=cb: (i, cb)) for _, w, cb in rows]


def _full_spec(p):
    return pl.BlockSpec(p.shape, lambda i: (0,) * p.ndim)


def _rowwise(name, fn, rows, params, outs, tm=256):
    t = rows[0][0].shape[0]
    tm = min(tm, t)
    nr, npar = len(rows), len(params)

    def body(*refs):
        ins = [r[...].astype(F32) for r in refs[:nr]]
        ps = [p[...] for p in refs[nr:nr + npar]]
        res = fn(*ins, *ps)
        for o_ref, r in zip(refs[nr + npar:], res):
            o_ref[...] = r.astype(o_ref.dtype)

    return pl.pallas_call(
        body,
        name=name,
        grid=(t // tm,),
        in_specs=_row_specs(rows, tm) + [_full_spec(p) for p in params],
        out_specs=[pl.BlockSpec((tm, w), lambda i: (i, 0)) for w, _ in outs],
        out_shape=[jax.ShapeDtypeStruct((t, w), dt) for w, dt in outs],
        compiler_params=_params("parallel"),
    )(*[r[0] for r in rows], *params)


def _rowwise_bwd(name, fn, rows, params, cots, grad_dtypes, tm=256, joined=False):
    t = rows[0][0].shape[0]
    tm = min(tm, t)
    nr, npar, nc = len(rows), len(params), len(cots)
    want = [j for j, dt in enumerate(grad_dtypes) if dt is not None]
    widths = [rows[j][1] for j in want]
    n_row_outs = 1 if joined else len(want)

    def body(*refs):
        i = pl.program_id(0)
        ins = [r[...].astype(F32) for r in refs[:nr]]
        ps = [p[...] for p in refs[nr:nr + npar]]
        cs = tuple(c[...].astype(F32) for c in refs[nr + npar:nr + npar + nc])
        _, vjp = jax.vjp(fn, *ins, *ps)
        gs = vjp(cs)
        outs = refs[nr + npar + nc:]
        if joined:
            off = 0
            for j, w in zip(want, widths):
                outs[0][:, off:off + w] = gs[j].astype(outs[0].dtype)
                off += w
        else:
            for o_ref, j in zip(outs, want):
                o_ref[...] = gs[j].astype(o_ref.dtype)
        pg_refs = outs[n_row_outs:]

        @pl.when(i == 0)
        def _():
            for pg in pg_refs:
                pg[...] = jnp.zeros_like(pg)

        for pg, g in zip(pg_refs, gs[nr:]):
            pg[...] += g

    if joined:
        row_specs = [pl.BlockSpec((tm, sum(widths)), lambda i: (i, 0))]
        row_shapes = [jax.ShapeDtypeStruct((t, sum(widths)), grad_dtypes[want[0]])]
    else:
        row_specs = [pl.BlockSpec((tm, w), lambda i: (i, 0)) for w in widths]
        row_shapes = [jax.ShapeDtypeStruct((t, w), grad_dtypes[j]) for j, w in zip(want, widths)]
    res = pl.pallas_call(
        body,
        name=name,
        grid=(t // tm,),
        in_specs=_row_specs(rows, tm) + [_full_spec(p) for p in params] + [pl.BlockSpec((tm, c.shape[1]), lambda i: (i, 0)) for c in cots],
        out_specs=row_specs + [_full_spec(p) for p in params],
        out_shape=row_shapes + [jax.ShapeDtypeStruct(p.shape, F32) for p in params],
        compiler_params=_params("arbitrary"),
    )(*[r[0] for r in rows], *params, *cots)
    return res[:n_row_outs], res[n_row_outs:]


def _rms(x, g):
    return x * lax.rsqrt(jnp.mean(x * x, axis=-1, keepdims=True) + EPS) * g


def _sigmoid(x):
    return 1.0 / (1.0 + jnp.exp(-x))


def _softplus(x):
    return jnp.maximum(x, 0.0) + jnp.log1p(jnp.exp(-jnp.abs(x)))


def _two_pass(x, m):
    hi = x.astype(BF16)
    lo = (x - hi.astype(F32)).astype(BF16)
    return _dot(hi, m) + _dot(lo, m)


def _head_sum_impl(x):
    w = HEADS * HEAD_DIM
    fold = jnp.where((_iota((w, LANES), 0) >> 7) == _iota((w, LANES), 1), 1.0, 0.0).astype(BF16)
    spread = jnp.where(_iota((LANES, w), 0) == (_iota((LANES, w), 1) >> 7), 1.0, 0.0).astype(BF16)
    return _two_pass(_two_pass(x, fold), spread)


@jax.custom_vjp
def _head_sum(x):
    return _head_sum_impl(x)


_head_sum.defvjp(lambda x: (_head_sum_impl(x), None), lambda _, g: (_head_sum_impl(g),))


def _fn_norm(x, g):
    return (_rms(x, g),)


def _fn_gates(ba, al, dt):
    col = _iota((1, LANES), 1)
    g = jnp.where((col >= HEADS) & (col < 2 * HEADS), -jnp.exp(al) * _softplus(ba + dt), 0.0)
    rows = ba.shape[0]
    r, c = _iota((rows, rows), 0), _iota((rows, rows), 1)
    same = (r >> CHUNK_SHIFT) == (c >> CHUNK_SHIFT)
    gc = _dot(jnp.where(same & (r >= c), 1.0, 0.0), g, precision=HI)
    gtot = _dot(jnp.where(same, 1.0, 0.0), g, precision=HI)
    return _sigmoid(ba), gc, gtot


def _fn_post_q(c):
    s = c * _sigmoid(c)
    return (s * lax.rsqrt(_head_sum(s * s) + EPS) * (HEAD_DIM ** -0.5),)


def _fn_post_k(c):
    s = c * _sigmoid(c)
    return (s * lax.rsqrt(_head_sum(s * s) + EPS),)


def _fn_post_v(c):
    return (c * _sigmoid(c),)


def _fn_post(cq, ck, cv):
    return _fn_post_q(cq) + _fn_post_k(ck) + _fn_post_v(cv)


def _fn_outnorm(o, gate, og):
    y = o * lax.rsqrt(_head_sum(o * o) * (1.0 / HEAD_DIM) + EPS) * og
    return (y * (gate * _sigmoid(gate)),)


def _fn_res_norm(x, m, gp, gn):
    x1 = x + _rms(m, gp)
    return x1, _rms(x1, gn)


def _fn_res_norm2(x, m, gp, ga, gb):
    x1 = x + _rms(m, gp)
    return x1, _rms(x1, ga), _rms(x1, gb)


def _relu2_of(u):
    r = jnp.maximum(u, 0.0)
    return (r * r,)


def _relu2_cotangent(da, a):
    return (da * (2.0 * jnp.sqrt(a.astype(F32))),)


def _loss_call(x3, d1, tgt, g, tm=256):
    t, d = x3.shape
    tm = min(tm, t)

    def body(x_ref, d_ref, t_ref, g_ref, loss_ref, dx_ref, dd_ref, dg_ref):
        i = pl.program_id(0)
        y, vjp = jax.vjp(lambda x, dd, gg: x + _rms(dd, gg), x_ref[...], d_ref[...], g_ref[...])
        err = y - t_ref[...]
        lrow = 0.5 * jnp.mean(err * err, axis=-1, keepdims=True)
        dx, dd, dg = vjp(err * (1.0 / d))
        dx_ref[...] = dx
        dd_ref[...] = dd.astype(dd_ref.dtype)

        @pl.when(i == 0)
        def _():
            loss_ref[...] = jnp.zeros_like(loss_ref)
            dg_ref[...] = jnp.zeros_like(dg_ref)

        loss_ref[...] += jnp.broadcast_to(jnp.sum(lrow, axis=0, keepdims=True), loss_ref.shape)
        dg_ref[...] += dg

    row = pl.BlockSpec((tm, d), lambda i: (i, 0))
    return pl.pallas_call(
        body,
        name="loss_head",
        grid=(t // tm,),
        in_specs=[row, row, row, _full_spec(g)],
        out_specs=[pl.BlockSpec((8, LANES), lambda i: (0, 0)), row, row, _full_spec(g)],
        out_shape=[jax.ShapeDtypeStruct((8, LANES), F32), jax.ShapeDtypeStruct((t, d), F32), jax.ShapeDtypeStruct((t, d), BF16), jax.ShapeDtypeStruct(g.shape, F32)],
        compiler_params=_params("arbitrary"),
    )(x3, d1, tgt, g)


HALO = 8


def _conv_fwd(qkvg, conv_w, tm=256):
    t = qkvg.shape[0]
    tm = min(tm, t)

    wide = QKV // 3

    def body(cur_ref, prev_ref, w_ref, o_ref, q_ref, k_ref, v_ref, buf):
        i = pl.program_id(0)
        buf[0:HALO, :] = jnp.where(i > 0, prev_ref[...], 0.0)
        buf[HALO:, :] = cur_ref[...]
        acc = buf[pl.ds(HALO - CONV_K + 1, tm), :] * w_ref[pl.ds(0, 1), :]
        for j in range(1, CONV_K):
            acc = acc + buf[pl.ds(HALO - CONV_K + 1 + j, tm), :] * w_ref[pl.ds(j, 1), :]
        o_ref[...] = acc
        (q_ref[...], k_ref[...], v_ref[...]) = _fn_post(acc[:, 0:wide], acc[:, wide:2 * wide], acc[:, 2 * wide:])

    part = pl.BlockSpec((tm, wide), lambda i: (i, 0))
    return pl.pallas_call(
        body,
        name="conv_fwd",
        grid=(t // tm,),
        in_specs=[
            pl.BlockSpec((tm, QKV), lambda i: (i, 0)),
            pl.BlockSpec((HALO, QKV), lambda i: (jnp.maximum(i * (tm // HALO) - 1, 0), 0)),
            pl.BlockSpec((CONV_K, QKV), lambda i: (0, 0)),
        ],
        out_specs=[pl.BlockSpec((tm, QKV), lambda i: (i, 0)), part, part, part],
        out_shape=[jax.ShapeDtypeStruct((t, QKV), F32)] + [jax.ShapeDtypeStruct((t, wide), F32)] * 3,
        scratch_shapes=[pltpu.VMEM((tm + HALO, QKV), F32)],
        compiler_params=_params("parallel"),
    )(qkvg, qkvg, conv_w)


def _conv_bwd(dc, dgate, qkvg, conv_w, tm=256):
    t = dc.shape[0]
    tm = min(tm, t)
    n = t // tm
    wg = dgate.shape[1]

    def body(dc_ref, dcn_ref, dgate_ref, x_ref, xp_ref, w_ref, dx_ref, dw_ref, bufd, bufx):
        i = pl.program_id(0)
        bufd[0:tm, :] = dc_ref[...]
        bufd[tm:, :] = jnp.where(i < n - 1, dcn_ref[...], 0.0)
        bufx[0:HALO, :] = jnp.where(i > 0, xp_ref[...], 0.0)
        bufx[HALO:, :] = x_ref[...]

        @pl.when(i == 0)
        def _():
            dw_ref[...] = jnp.zeros_like(dw_ref)

        dcv = dc_ref[...]
        acc = bufd[pl.ds(CONV_K - 1, tm), :] * w_ref[pl.ds(0, 1), :]
        for j in range(1, CONV_K):
            acc = acc + bufd[pl.ds(CONV_K - 1 - j, tm), :] * w_ref[pl.ds(j, 1), :]
        dx_ref[:, 0:QKV] = acc.astype(dx_ref.dtype)
        dx_ref[:, QKV:] = dgate_ref[...].astype(dx_ref.dtype)
        for j in range(CONV_K):
            dw_ref[pl.ds(j, 1), :] += jnp.sum(dcv * bufx[pl.ds(HALO - CONV_K + 1 + j, tm), :], axis=0, keepdims=True)

    return pl.pallas_call(
        body,
        name="conv_bwd",
        grid=(n,),
        in_specs=[
            pl.BlockSpec((tm, QKV), lambda i: (i, 0)),
            pl.BlockSpec((HALO, QKV), lambda i: (jnp.minimum((i + 1) * (tm // HALO), t // HALO - 1), 0)),
            pl.BlockSpec((tm, wg), lambda i: (i, 0)),
            pl.BlockSpec((tm, QKV), lambda i: (i, 0)),
            pl.BlockSpec((HALO, QKV), lambda i: (jnp.maximum(i * (tm // HALO) - 1, 0), 0)),
            pl.BlockSpec((CONV_K, QKV), lambda i: (0, 0)),
        ],
        out_specs=[pl.BlockSpec((tm, QKV + wg), lambda i: (i, 0)), pl.BlockSpec((HALO, QKV), lambda i: (0, 0))],
        out_shape=[jax.ShapeDtypeStruct((t, QKV + wg), BF16), jax.ShapeDtypeStruct((HALO, QKV), F32)],
        scratch_shapes=[pltpu.VMEM((tm + HALO, QKV), F32), pltpu.VMEM((tm + HALO, QKV), F32)],
        compiler_params=_params("arbitrary"),
    )(dc, dc, dgate, qkvg, qkvg, conv_w)


PREP_CHUNKS = 16
PREP_BWD_CHUNKS = 4


def _hi_lo(x):
    hi = x.astype(BF16)
    return hi, (x - hi.astype(F32)).astype(BF16)


def _mm3(a, b, dims=NN):
    (ah, al), (bh, bl) = _hi_lo(a), _hi_lo(b)
    return _dot(ah, bh, dims) + (_dot(ah, bl, dims) + _dot(al, bh, dims))


def _neumann(lowers):
    c = lowers[0].shape[0]
    eye = jnp.where(_iota((c, c), 0) == _iota((c, c), 1), 1.0, 0.0)
    ps = [-low for low in lowers]
    tmats = [eye + p for p in ps]
    for _ in range(CHUNK_SHIFT - 1):
        ps = [_mm3(p, p) for p in ps]
        tmats = [t + _mm3(t, p) for t, p in zip(tmats, ps)]
    return tuple(tmats)


def _inv_cotangents(tmats, dts):
    half = [_mm3(t, dt, TN) for t, dt in zip(tmats, dts)]
    return tuple(-_mm3(hf, t, NT) for hf, t in zip(half, tmats))


@jax.custom_vjp
def _tri_inv(lowers):
    return _neumann(lowers)


def _tri_inv_fwd(lowers):
    tmats = _neumann(lowers)
    return tmats, tmats


_tri_inv.defvjp(_tri_inv_fwd, lambda tmats, dts: (_inv_cotangents(tmats, dts),))


@jax.custom_vjp
def _tri_inv_known(lowers, tmats):
    return tmats


_tri_inv_known.defvjp(lambda lowers, tmats: (tmats, tmats),
                      lambda tmats, dts: (_inv_cotangents(tmats, dts), tuple(jnp.zeros_like(t) for t in tmats)))


def _prep_chunks(qs, ks, vs, bs, gcs, gts, gcrs, tmats=None):
    c = CHUNK
    r, col = _iota((c, c), 0), _iota((c, c), 1)
    incl, strict = r >= col, r > col
    decays = [jnp.where(incl, jnp.exp(jnp.where(incl, gc - gcr, 0.0)), 0.0) for gc, gcr in zip(gcs, gcrs)]
    kbs = [k * b for k, b in zip(ks, bs)]
    kbfs = [k.astype(BF16) for k in ks]
    lowers = tuple(jnp.where(strict, _dot(kb.astype(BF16), kbf, NT) * decay, 0.0) for kb, kbf, decay in zip(kbs, kbfs, decays))
    tmats = _tri_inv(lowers) if tmats is None else _tri_inv_known(lowers, tuple(tmats))
    outs = []
    for q, k, v, b, gc, gt, kb, kbf, decay, tmat in zip(qs, ks, vs, bs, gcs, gts, kbs, kbfs, decays, tmats):
        tb = tmat.astype(BF16)
        egc = jnp.exp(gc)
        w = _dot(tb, (kb * egc).astype(BF16))
        u = _dot(tb, (v * b).astype(BF16))
        attn = _dot(q.astype(BF16), kbf, NT) * decay
        gl = jnp.broadcast_to(jnp.exp(jnp.mean(gt.reshape(c // 8, 8, 1), axis=0)), (8, HEAD_DIM))
        outs.append((w, u, q * egc, k * jnp.exp(gt - gc), attn, gl))
    return tuple(outs), tmats


def _prep_specs(rows, gch):
    head = pl.BlockSpec((rows, HEAD_DIM), lambda n, h: (n, h))
    gates = pl.BlockSpec((rows, LANES), lambda n, h: (n, 0))
    gcrow = pl.BlockSpec((1, gch, 1, CHUNK), lambda n, h: (h, n, 0, 0))
    square = pl.BlockSpec((1, rows, CHUNK), lambda n, h: (h, n, 0))
    gl = pl.BlockSpec((1, gch * 8, HEAD_DIM), lambda n, h: (h, n, 0))
    return head, gates, gcrow, square, gl


def _pick_lane(ref, sl, lane):
    return jnp.sum(jnp.where(_iota((1, LANES), 1) == lane, ref[sl, :], 0.0), axis=1, keepdims=True)


def _prep_inputs(q_ref, k_ref, v_ref, b_ref, gc_ref, gt_ref, gcr_ref, sls, h):
    return ([q_ref[sl, :] for sl in sls], [k_ref[sl, :] for sl in sls], [v_ref[sl, :] for sl in sls],
            [_pick_lane(b_ref, sl, h) for sl in sls], [_pick_lane(gc_ref, sl, h + HEADS) for sl in sls],
            [_pick_lane(gt_ref, sl, h + HEADS) for sl in sls], [gcr_ref[0, c] for c in range(len(sls))])


def _gdn_prep(q, k, v, beta, gc, gt, gcr):
    t = q.shape[0]
    gch = min(PREP_CHUNKS, t // CHUNK)
    rows = gch * CHUNK

    def body(q_ref, k_ref, v_ref, b_ref, gc_ref, gt_ref, gcr_ref, w_ref, u_ref, qg_ref, kg_ref, at_ref, gl_ref, tm_ref):
        h = pl.program_id(1)
        sls = [pl.ds(c * CHUNK, CHUNK) for c in range(gch)]
        outs, tmats = _prep_chunks(*_prep_inputs(q_ref, k_ref, v_ref, b_ref, gc_ref, gt_ref, gcr_ref, sls, h))
        for c, (sl, (w, u, qg, kg, attn, gl), tmat) in enumerate(zip(sls, outs, tmats)):
            w_ref[sl, :] = w.astype(BF16)
            u_ref[sl, :] = u
            qg_ref[sl, :] = qg.astype(BF16)
            kg_ref[sl, :] = kg.astype(BF16)
            at_ref[0, sl, :] = attn.astype(BF16)
            gl_ref[0, pl.ds(c * 8, 8), :] = gl
            tm_ref[0, sl, :] = tmat

    hb, col, gcrow, square, glb = _prep_specs(rows, gch)
    wide = HEADS * HEAD_DIM
    return pl.pallas_call(
        body,
        name="gdn_prep",
        grid=(t // rows, HEADS),
        in_specs=[hb, hb, hb, col, col, col, gcrow],
        out_specs=[hb, hb, hb, hb, square, glb, square],
        out_shape=[
            jax.ShapeDtypeStruct((t, wide), BF16),
            jax.ShapeDtypeStruct((t, wide), F32),
            jax.ShapeDtypeStruct((t, wide), BF16),
            jax.ShapeDtypeStruct((t, wide), BF16),
            jax.ShapeDtypeStruct((HEADS, t, CHUNK), BF16),
            jax.ShapeDtypeStruct((HEADS, t // CHUNK * 8, HEAD_DIM), F32),
            jax.ShapeDtypeStruct((HEADS, t, CHUNK), F32),
        ],
        compiler_params=_params("parallel", "parallel"),
    )(q, k, v, beta, gc, gt, gcr)


def _gdn_prep_bwd(q, k, v, beta, gc, gt, gcr, tmat, dw, du, dqg, dkg, dattn, dgl):
    t = q.shape[0]
    gch = min(PREP_BWD_CHUNKS, t // CHUNK)
    rows = gch * CHUNK

    def body(q_ref, k_ref, v_ref, b_ref, gc_ref, gt_ref, gcr_ref, tm_ref, dw_ref, du_ref, dqg_ref, dkg_ref, dat_ref, dgl_ref,
             dq_ref, dk_ref, dv_ref, db_ref, dgc_ref, dgt_ref, dgcr_ref):
        h = pl.program_id(1)
        lane = _iota((1, LANES), 1)

        @pl.when(h == 0)
        def _():
            db_ref[...] = jnp.zeros_like(db_ref)
            dgc_ref[...] = jnp.zeros_like(dgc_ref)
            dgt_ref[...] = jnp.zeros_like(dgt_ref)

        sls = [pl.ds(c * CHUNK, CHUNK) for c in range(gch)]
        known = [tm_ref[0, sl, :] for sl in sls]
        _, vjp = jax.vjp(lambda *a: _prep_chunks(*a, tmats=known)[0], *_prep_inputs(q_ref, k_ref, v_ref, b_ref, gc_ref, gt_ref, gcr_ref, sls, h))
        cots = tuple((dw_ref[sl, :], du_ref[sl, :], dqg_ref[sl, :], dkg_ref[sl, :], dat_ref[0, sl, :], dgl_ref[0, pl.ds(c * 8, 8), :]) for c, sl in enumerate(sls))
        dqs, dks, dvs, dbs, dgcs, dgts, dgcrs = vjp(cots)
        for c, sl in enumerate(sls):
            dq_ref[sl, :] = dqs[c]
            dk_ref[sl, :] = dks[c]
            dv_ref[sl, :] = dvs[c]
            db_ref[sl, :] += jnp.where(lane == h, dbs[c], 0.0)
            dgc_ref[sl, :] += jnp.where(lane == h + HEADS, dgcs[c], 0.0)
            dgt_ref[sl, :] += jnp.where(lane == h + HEADS, dgts[c], 0.0)
            dgcr_ref[0, c] = dgcrs[c]

    hb, col, gcrow, square, glb = _prep_specs(rows, gch)
    wide = HEADS * HEAD_DIM
    return pl.pallas_call(
        body,
        name="gdn_prep_bwd",
        grid=(t // rows, HEADS),
        in_specs=[hb, hb, hb, col, col, col, gcrow, square, hb, hb, hb, hb, square, glb],
        out_specs=[hb, hb, hb, col, col, col, gcrow],
        out_shape=[jax.ShapeDtypeStruct((t, wide), F32)] * 3 + [jax.ShapeDtypeStruct((t, LANES), F32)] * 3 + [jax.ShapeDtypeStruct((HEADS, t // CHUNK, 1, CHUNK), F32)],
        compiler_params=_params("parallel", "arbitrary"),
    )(q, k, v, beta, gc, gt, gcr, tmat, dw, du, dqg, dkg, dattn, dgl)


def _gdn_scan(w, u, qg, kg, attn, gl):
    t = w.shape[0]
    n = t // CHUNK
    wide = HEADS * HEAD_DIM

    def body(w_ref, u_ref, qg_ref, kg_ref, at_ref, gl_ref, o_ref, st_ref, s_ref):
        @pl.when(pl.program_id(0) == 0)
        def _():
            s_ref[...] = jnp.zeros_like(s_ref)

        heads = range(HEADS)
        cols = [pl.ds(h * HEAD_DIM, HEAD_DIM) for h in heads]
        ss = [s_ref[h] for h in heads]
        sbs = [s.astype(BF16) for s in ss]
        vbs = [(u_ref[:, hs] - _dot(w_ref[:, hs], sb)).astype(BF16) for hs, sb in zip(cols, sbs)]
        outs = [_dot(qg_ref[:, hs], sb) + _dot(at_ref[h], vb) for h, hs, sb, vb in zip(heads, cols, sbs, vbs)]
        new = [s * jnp.tile(gl_ref[h], (HEAD_DIM // 8, 1)) + _dot(kg_ref[:, hs], vb, TN) for h, hs, s, vb in zip(heads, cols, ss, vbs)]
        for h, hs in zip(heads, cols):
            st_ref[0, h] = ss[h]
            o_ref[:, hs] = outs[h]
            s_ref[h] = new[h]

    row = pl.BlockSpec((CHUNK, wide), lambda i: (i, 0))
    return pl.pallas_call(
        body,
        name="gdn_scan",
        grid=(n,),
        in_specs=[row, row, row, row, pl.BlockSpec((HEADS, CHUNK, CHUNK), lambda i: (0, i, 0)), pl.BlockSpec((HEADS, 8, HEAD_DIM), lambda i: (0, i, 0))],
        out_specs=[row, pl.BlockSpec((1, HEADS, HEAD_DIM, HEAD_DIM), lambda i: (i, 0, 0, 0))],
        out_shape=[jax.ShapeDtypeStruct((t, wide), F32), jax.ShapeDtypeStruct((n, HEADS, HEAD_DIM, HEAD_DIM), F32)],
        scratch_shapes=[pltpu.VMEM((HEADS, HEAD_DIM, HEAD_DIM), F32)],
        compiler_params=_params("arbitrary"),
    )(w, u, qg, kg, attn, gl)


def _gdn_scan_bwd(w, u, qg, kg, attn, gl, states, do):
    t = w.shape[0]
    n = t // CHUNK
    wide = HEADS * HEAD_DIM

    def body(w_ref, u_ref, qg_ref, kg_ref, at_ref, gl_ref, st_ref, do_ref, dw_ref, du_ref, dqg_ref, dkg_ref, dat_ref, dgl_ref, ds_ref):
        @pl.when(pl.program_id(0) == 0)
        def _():
            ds_ref[...] = jnp.zeros_like(ds_ref)

        heads = range(HEADS)
        cols = [pl.ds(h * HEAD_DIM, HEAD_DIM) for h in heads]
        ss = [st_ref[0, h] for h in heads]
        sbs = [s.astype(BF16) for s in ss]
        dsns = [ds_ref[h] for h in heads]
        dsbs = [d.astype(BF16) for d in dsns]
        dobs = [do_ref[:, hs].astype(BF16) for hs in cols]
        vbs = [(u_ref[:, hs] - _dot(w_ref[:, hs], sb)).astype(BF16) for hs, sb in zip(cols, sbs)]
        dvns = [_dot(at_ref[h], dob, TN) + _dot(kg_ref[:, hs], dsb) for h, hs, dob, dsb in zip(heads, cols, dobs, dsbs)]
        dvbs = [d.astype(BF16) for d in dvns]
        for h, hs in zip(heads, cols):
            dat_ref[h] = _dot(dobs[h], vbs[h], NT)
            dqg_ref[:, hs] = _dot(dobs[h], sbs[h], NT)
            dkg_ref[:, hs] = _dot(vbs[h], dsbs[h], NT)
            du_ref[:, hs] = dvns[h]
            dw_ref[:, hs] = -_dot(dvbs[h], sbs[h], NT)
            dgl_ref[h] = jnp.sum((dsns[h] * ss[h]).reshape(HEAD_DIM // 8, 8, HEAD_DIM), axis=0)
        new = [dsn * jnp.tile(gl_ref[h], (HEAD_DIM // 8, 1)) + _dot(qg_ref[:, hs], dob, TN) - _dot(w_ref[:, hs], dvb, TN)
               for h, hs, dsn, dob, dvb in zip(heads, cols, dsns, dobs, dvbs)]
        for h in heads:
            ds_ref[h] = new[h]

    row = pl.BlockSpec((CHUNK, wide), lambda i: (n - 1 - i, 0))
    at = pl.BlockSpec((HEADS, CHUNK, CHUNK), lambda i: (0, n - 1 - i, 0))
    glb = pl.BlockSpec((HEADS, 8, HEAD_DIM), lambda i: (0, n - 1 - i, 0))
    return pl.pallas_call(
        body,
        name="gdn_scan_bwd",
        grid=(n,),
        in_specs=[row, row, row, row, at, glb, pl.BlockSpec((1, HEADS, HEAD_DIM, HEAD_DIM), lambda i: (n - 1 - i, 0, 0, 0)), row],
        out_specs=[row, row, row, row, at, glb],
        out_shape=[jax.ShapeDtypeStruct((t, wide), F32)] * 4 + [jax.ShapeDtypeStruct((HEADS, t, CHUNK), F32), jax.ShapeDtypeStruct((HEADS, n * 8, HEAD_DIM), F32)],
        scratch_shapes=[pltpu.VMEM((HEADS, HEAD_DIM, HEAD_DIM), F32)],
        compiler_params=_params("arbitrary"),
    )(w, u, qg, kg, attn, gl, states, do)


SB_Q = 1024
SB_K = 256


def _sb_scores(q, k):
    z = _dot(q, k, NT) * (HEAD_DIM ** -0.5)
    e = jnp.exp(-jnp.abs(z))
    lb = jnp.minimum(z, 0.0) - jnp.log(1.0 + e)
    return z, e, lb, lb - z


def _tri(n, rel):
    return jnp.where(rel(_iota((n, n), 0), _iota((n, n), 1)), 1.0, 0.0).astype(BF16)


def _lanes(col):
    return jnp.broadcast_to(col, (col.shape[0], LANES))


def _sb_fwd(q, k, v):
    t = q.shape[0]
    bq, bk = min(SB_Q, t), min(SB_K, t)
    nsub, rep = bq // bk, bk // LANES

    def body(q_ref, k_ref, v_ref, o_ref, rt_ref):
        i = pl.program_id(1)
        o_ref[...] = jnp.zeros_like(o_ref)
        rt_ref[...] = jnp.zeros_like(rt_ref)
        after = _tri(bk, lambda r, c: r > c)

        def block(j, r0, diag):
            st = pl.multiple_of(j * bk, bk)
            kv, vv = k_ref[pl.ds(st, bk), :], v_ref[pl.ds(st, bk), :]
            _, _, lb, l1m = _sb_scores(q_ref[r0:, :], kv)
            if diag:
                mask = _iota((bq - r0, bk), 1) + j * bk < _iota((bq - r0, bk), 0) + (r0 + i * bq)
                l1m = jnp.where(mask, l1m, 0.0)
            sums = _two_pass(l1m, after)
            run = rt_ref[r0:, :]
            a = jnp.exp(lb + jnp.tile(run, (1, rep)) + sums)
            if diag:
                a = jnp.where(mask, a, 0.0)
            o_ref[r0:, :] += _dot(a.astype(BF16), vv)
            rt_ref[r0:, :] = run + _lanes(sums[:, 0:1] + l1m[:, 0:1])

        def group(p, diag):
            for s in reversed(range(nsub)):
                block(p * nsub + s, s * bk if diag else 0, diag)

        def left_of_diagonal(jj, carry):
            group(i - jj, False)
            return carry

        group(i, True)
        lax.fori_loop(1, i + 1, left_of_diagonal, 0)

    qb = pl.BlockSpec((bq, HEAD_DIM), lambda h, i: (i, h))
    full = pl.BlockSpec((t, HEAD_DIM), lambda h, i: (0, h))
    return pl.pallas_call(
        body,
        name="sb_fwd",
        grid=(HEADS, t // bq),
        in_specs=[qb, full, full],
        out_specs=[qb, qb],
        out_shape=[jax.ShapeDtypeStruct(q.shape, F32), jax.ShapeDtypeStruct(q.shape, F32)],
        compiler_params=_params("parallel", "arbitrary"),
    )(q, k, v)


def _sb_bwd(q, k, v, rt, do):
    t = q.shape[0]
    bq, bk = min(SB_Q, t), min(SB_K, t)
    nsub, rep = bq // bk, bk // LANES
    scale = HEAD_DIM ** -0.5

    def body(q_ref, k_ref, v_ref, rt_ref, do_ref, dq_ref, dk_ref, dv_ref, left_ref, pg_ref):
        i = pl.program_id(1)

        @pl.when(i == 0)
        def _():
            dk_ref[...] = jnp.zeros_like(dk_ref)
            dv_ref[...] = jnp.zeros_like(dv_ref)

        dq_ref[...] = jnp.zeros_like(dq_ref)
        left_ref[...] = jnp.zeros_like(left_ref)
        pg_ref[...] = jnp.zeros_like(pg_ref)
        upto = _tri(bk, lambda r, c: r <= c)
        before = _tri(bk, lambda r, c: r < c)

        def block(j, r0, diag):
            st = pl.multiple_of(j * bk, bk)
            kv, vv = k_ref[pl.ds(st, bk), :], v_ref[pl.ds(st, bk), :]
            qv = q_ref[r0:, :]
            dob = do_ref[r0:, :].astype(BF16)
            z, e, lb, l1m = _sb_scores(qv, kv)
            if diag:
                mask = _iota((bq - r0, bk), 1) + j * bk < _iota((bq - r0, bk), 0) + (r0 + i * bq)
                l1m = jnp.where(mask, l1m, 0.0)
            beta = jnp.where(z >= 0.0, 1.0, e) / (1.0 + e)
            sums = _two_pass(l1m, upto)
            left = left_ref[r0:, :]
            a = jnp.exp(lb + jnp.tile(rt_ref[r0:, :] - left, (1, rep)) - sums)
            if diag:
                a = jnp.where(mask, a, 0.0)
            g = _dot(dob, vv, NT) * a
            dv_ref[pl.ds(st, bk), :] += _dot(a.astype(BF16), dob, TN)
            gsum = _two_pass(g, before)
            pg = pg_ref[r0:, :]
            dz = g * (1.0 - beta) - (jnp.tile(pg, (1, rep)) + gsum) * beta
            if diag:
                dz = jnp.where(mask, dz, 0.0)
            dzb = (dz * scale).astype(BF16)
            dk_ref[pl.ds(st, bk), :] += _dot(dzb, qv, TN)
            dq_ref[r0:, :] += _dot(dzb, kv)
            left_ref[r0:, :] = left + _lanes(sums[:, bk - 1:bk])
            pg_ref[r0:, :] = pg + _lanes(gsum[:, bk - 1:bk] + g[:, bk - 1:bk])

        def group(p, diag):
            for s in range(nsub):
                block(p * nsub + s, s * bk if diag else 0, diag)

        def left_of_diagonal(p, carry):
            group(p, False)
            return carry

        lax.fori_loop(0, i, left_of_diagonal, 0)
        group(i, True)

    qb = pl.BlockSpec((bq, HEAD_DIM), lambda h, i: (i, h))
    full = pl.BlockSpec((t, HEAD_DIM), lambda h, i: (0, h))
    return pl.pallas_call(
        body,
        name="sb_bwd",
        grid=(HEADS, t // bq),
        in_specs=[qb, full, full, qb, qb],
        out_specs=[qb, full, full],
        out_shape=[jax.ShapeDtypeStruct(q.shape, F32)] * 3,
        scratch_shapes=[pltpu.VMEM((bq, LANES), F32), pltpu.VMEM((bq, LANES), F32)],
        compiler_params=_params("parallel", "arbitrary"),
    )(q, k, v, rt, do)


def _adamw(w, g, m, v, name, tm=256):
    r, c = w.shape
    tm = tm if r % tm == 0 else r

    def body(w_ref, g_ref, m_ref, v_ref, d_ref, nm_ref, nv_ref):
        gv = g_ref[...]
        nm = ADAM_B1 * m_ref[...] + (1.0 - ADAM_B1) * gv
        nv = ADAM_B2 * v_ref[...] + (1.0 - ADAM_B2) * (gv * gv)
        m_hat = nm / (1.0 - ADAM_B1 ** ADAM_STEP)
        v_hat = nv / (1.0 - ADAM_B2 ** ADAM_STEP)
        d_ref[...] = -ADAM_LR * (m_hat / (jnp.sqrt(v_hat) + ADAM_EPS) + ADAM_WD * w_ref[...])
        nm_ref[...] = nm
        nv_ref[...] = nv

    blk = pl.BlockSpec((tm, c), lambda i: (i, 0))
    return pl.pallas_call(
        body,
        name=name,
        grid=(r // tm,),
        in_specs=[blk] * 4,
        out_specs=[blk] * 3,
        out_shape=[jax.ShapeDtypeStruct((r, c), F32)] * 3,
        compiler_params=_params("parallel"),
    )(w, g, m, v)


def _local_step(x, tgt, gains, wts, small):
    mix_pre, mix_post, mlp_pre, mlp_post, kv_gain = gains
    w_qkvg, w_ba, w_out, w_kv, w_q, w_o, w_up, w_down = wts
    conv_w, a_log, dt_bias, out_gain = small
    t, d = x.shape
    row = lambda a, i=None: a[i:i + 1] if i is not None else a
    al = jnp.zeros((1, LANES), F32).at[:, HEADS:2 * HEADS].set(a_log)
    dtb = jnp.zeros((1, LANES), F32).at[:, HEADS:2 * HEADS].set(dt_bias)
    og = jnp.tile(out_gain, (1, HEADS))
    full = lambda a: (a, a.shape[1], 0)

    (h0,) = _rowwise("norm_in", _fn_norm, [full(x)], [row(mix_pre, 0)], [(d, BF16)])
    qkvg = _matmul(h0, w_qkvg, "nn", F32, "mm_gdn_in", tk=1024)
    ba = _matmul(h0, w_ba, "nn", F32, "mm_gdn_ba", tk=1024)
    conv, gq, gk, gv = _conv_fwd(qkvg, conv_w)
    conv_qkv = [(conv, d, 0), (conv, d, 1), (conv, d, 2)]
    beta, gc, gt = _rowwise("gates", _fn_gates, [full(ba)], [al, dtb], [(LANES, F32)] * 3)
    gcr = jnp.swapaxes(gc[:, HEADS:2 * HEADS], 0, 1).reshape(HEADS, t // CHUNK, 1, CHUNK)
    pw, pu, pqg, pkg, pattn, pgl, ptm = _gdn_prep(gq, gk, gv, beta, gc, gt, gcr)
    o_gdn, states = _gdn_scan(pw, pu, pqg, pkg, pattn, pgl)
    (on,) = _rowwise("out_norm", _fn_outnorm, [full(o_gdn), (qkvg, d, 3)], [og], [(d, BF16)])
    mix0 = _matmul(on, w_out, "nn", F32, "mm_gdn_out", tk=1024)
    x1, h1 = _rowwise("res_a0", _fn_res_norm, [full(x), full(mix0)], [row(mix_post, 0), row(mlp_pre, 0)], [(d, F32), (d, BF16)])
    (a0,) = _matmul(h1, w_up[0], "nn", (BF16,), "mm_up0", tk=1024, epilogue=_relu2_of)
    d0 = _matmul(a0, w_down[0], "nn", F32, "mm_down0")
    x2, hkv, hq = _rowwise("res_b0", _fn_res_norm2, [full(x1), full(d0)], [row(mlp_post, 0), kv_gain, row(mix_pre, 1)], [(d, F32), (d, BF16), (d, BF16)])
    w_k, w_v = w_kv[:, :d], w_kv[:, d:]
    kp = _matmul(hkv, w_k, "nn", BF16, "mm_k", tk=1024)
    vp = _matmul(hkv, w_v, "nn", BF16, "mm_v", tk=1024)
    qp = _matmul(hq, w_q, "nn", BF16, "mm_q", tk=1024)
    o_sb, rt = _sb_fwd(qp, kp, vp)
    mix1 = _matmul(o_sb, w_o, "nn", F32, "mm_sb_out", tk=1024)
    x3, h3 = _rowwise("res_a1", _fn_res_norm, [full(x2), full(mix1)], [row(mix_post, 1), row(mlp_pre, 1)], [(d, F32), (d, BF16)])
    (a1,) = _matmul(h3, w_up[1], "nn", (BF16,), "mm_up1", tk=1024, epilogue=_relu2_of)
    d1 = _matmul(a1, w_down[1], "nn", F32, "mm_down1")

    loss, dx3, dd1, g_mlp_post1 = _loss_call(x3, d1, tgt, row(mlp_post, 1))
    (du1,) = _matmul(dd1, w_down[1], "nt", (BF16,), "mm_down1_dx", tk=1024, epilogue=_relu2_cotangent, extras=[a1])
    g_down1 = _matmul(a1, dd1, "tn", F32, "mm_down1_dw")
    dh3 = _matmul(du1, w_up[1], "nt", F32, "mm_up1_dx")
    g_up1 = _matmul(h3, du1, "tn", F32, "mm_up1_dw")
    (dx2, dmix1), (g_mix_post1, g_mlp_pre1) = _rowwise_bwd(
        "res_a1_bwd", _fn_res_norm, [full(x2), full(mix1)], [row(mix_post, 1), row(mlp_pre, 1)], [dx3, dh3], [F32, BF16])
    do_sb = _matmul(dmix1, w_o, "nt", BF16, "mm_sb_out_dx")
    g_o = _matmul(o_sb, dmix1, "tn", F32, "mm_sb_out_dw")
    dqp, dkp, dvp = _sb_bwd(qp, kp, vp, rt, do_sb)
    dhq = _matmul(dqp, w_q, "nt", F32, "mm_q_dx")
    g_q = _matmul(hq, dqp, "tn", F32, "mm_q_dw")
    dhkv = _matmul(dvp, w_v, "nt", F32, "mm_v_dx", add=_matmul(dkp, w_k, "nt", F32, "mm_k_dx"))
    g_kv = jnp.concatenate([_matmul(hkv, dkp, "tn", F32, "mm_k_dw"), _matmul(hkv, dvp, "tn", F32, "mm_v_dw")], axis=1)
    (dx1, dd0), (g_mlp_post0, g_kv_gain, g_mix_pre1) = _rowwise_bwd(
        "res_b0_bwd", _fn_res_norm2, [full(x1), full(d0)], [row(mlp_post, 0), kv_gain, row(mix_pre, 1)], [dx2, dhkv, dhq], [F32, BF16])
    (du0,) = _matmul(dd0, w_down[0], "nt", (BF16,), "mm_down0_dx", tk=1024, epilogue=_relu2_cotangent, extras=[a0])
    g_down0 = _matmul(a0, dd0, "tn", F32, "mm_down0_dw")
    dh1 = _matmul(du0, w_up[0], "nt", F32, "mm_up0_dx")
    g_up0 = _matmul(h1, du0, "tn", F32, "mm_up0_dw")
    (dx0, dmix0), (g_mix_post0, g_mlp_pre0) = _rowwise_bwd(
        "res_a0_bwd", _fn_res_norm, [full(x), full(mix0)], [row(mix_post, 0), row(mlp_pre, 0)], [dx1, dh1], [F32, BF16])
    don = _matmul(dmix0, w_out, "nt", F32, "mm_gdn_out_dx")
    g_out = _matmul(on, dmix0, "tn", F32, "mm_gdn_out_dw")
    (do_gdn, dgate), (g_og,) = _rowwise_bwd("out_norm_bwd", _fn_outnorm, [full(o_gdn), (qkvg, d, 3)], [og], [don], [F32, F32])
    dpw, dpu, dpqg, dpkg, dpattn, dpgl = _gdn_scan_bwd(pw, pu, pqg, pkg, pattn, pgl, states, do_gdn)
    dgq, dgk, dgv, dbeta, dgc, dgt, dgcr = _gdn_prep_bwd(gq, gk, gv, beta, gc, gt, gcr, ptm, dpw, dpu, dpqg, dpkg, dpattn, dpgl)
    dgcr_lanes = jnp.pad(jnp.swapaxes(dgcr.reshape(HEADS, t), 0, 1), ((0, 0), (HEADS, LANES - 2 * HEADS)))
    gate_cots = [dbeta, dgc + dgcr_lanes, dgt]
    (dba,), (g_al, g_dtb) = _rowwise_bwd("gates_bwd", _fn_gates, [full(ba)], [al, dtb], gate_cots, [BF16])
    (dconv,), _ = _rowwise_bwd("post_conv_bwd", _fn_post, conv_qkv, [], [dgq, dgk, dgv], [F32] * 3, joined=True)
    dqkvg, g_conv = _conv_bwd(dconv, dgate, qkvg, conv_w)
    dh0b = _matmul(dba, w_ba, "nt", F32, "mm_gdn_ba_dx", tk=LANES)
    dh0 = _matmul(dqkvg, w_qkvg, "nt", F32, "mm_gdn_in_dx", add=dh0b)
    g_qkvg = _matmul(h0, dqkvg, "tn", F32, "mm_gdn_in_dw")
    g_ba = _matmul(h0, dba, "tn", F32, "mm_gdn_ba_dw")
    (grad_x,), (g_mix_pre0,) = _rowwise_bwd("norm_in_bwd", lambda xx, gg: (_rms(xx, gg), xx), [full(x)], [row(mix_pre, 0)], [dh0, dx0], [F32])

    grads = dict(
        mix_pre_gain=jnp.concatenate([g_mix_pre0, g_mix_pre1], axis=0),
        mix_post_gain=jnp.concatenate([g_mix_post0, g_mix_post1], axis=0),
        mlp_pre_gain=jnp.concatenate([g_mlp_pre0, g_mlp_pre1], axis=0),
        mlp_post_gain=jnp.concatenate([g_mlp_post0, g_mlp_post1], axis=0),
        mlp_w_up=jnp.stack([g_up0, g_up1]),
        mlp_w_down=jnp.stack([g_down0, g_down1]),
        gdn_w_in=jnp.concatenate([g_qkvg, g_ba[:, :2 * HEADS]], axis=1)[None],
        gdn_conv_w=g_conv[None, :CONV_K],
        gdn_a_log=g_al[:, HEADS:2 * HEADS],
        gdn_dt_bias=g_dtb[:, HEADS:2 * HEADS],
        gdn_out_gain=jnp.sum(g_og.reshape(HEADS, HEAD_DIM), axis=0, keepdims=True),
        gdn_w_out=g_out[None],
        kv_gain=g_kv_gain[0],
        w_kv=g_kv,
        sb_w_q=g_q[None],
        sb_w_o=g_o[None],
    )
    return loss, grad_x, grads


N_DEV = 8
N_CHIPS = 4
PACK_COLS = 1024
PACK_ROW_TILE = 256

_HBM = pl.BlockSpec(memory_space=pltpu.HBM)


def _place():
    return lax.axis_index("x"), lax.axis_index("y"), lax.axis_index("c")


def _other_chips(x, y):
    return [(1 - x, y), (x, 1 - y), (1 - x, 1 - y)]


def _remote(src, dst, send_sem, recv_sem, to):
    return pltpu.make_async_remote_copy(src_ref=src, dst_ref=dst, send_sem=send_sem, recv_sem=recv_sem, device_id=to, device_id_type=MESH)


def _gather8(v, name):
    rows, cols = v.shape

    def body(v_ref, out_ref, sum_ref, send_sems, recv_sems, local_sem):
        x, y, c = _place()
        me, sibling = (x, y, c), (x, y, 1 - c)
        chips = _other_chips(x, y)

        def blk(px, py, pc):
            return out_ref.at[pl.ds((4 * px + 2 * py + pc) * rows, rows), :]

        def copy(k, block, to, src=None):
            return _remote(blk(*block) if src is None else src, blk(*block), send_sems.at[k], recv_sems.at[k], to)

        mine = pltpu.make_async_copy(v_ref, blk(*me), local_sem)
        mine.start()
        first = [copy(0, me, sibling, src=v_ref)] + [copy(1 + j, me, (*chip, c), src=v_ref) for j, chip in enumerate(chips)]
        for cp in first:
            cp.start()
        passed = [copy(4 + j, (*chip, c), sibling) for j, chip in enumerate(chips)]
        for j, chip in enumerate(chips):
            copy(1 + j, (*chip, c), me).wait_recv()
            passed[j].start()
        copy(0, sibling, me).wait_recv()
        for j, chip in enumerate(chips):
            copy(4 + j, (*chip, 1 - c), me).wait_recv()
        for cp in first + passed:
            cp.wait_send()
        mine.wait()
        acc = out_ref[pl.ds(0, rows), :]
        for dev in range(1, N_DEV):
            acc = acc + out_ref[pl.ds(dev * rows, rows), :]
        sum_ref[...] = acc

    vm = pl.BlockSpec(memory_space=pltpu.VMEM)
    return pl.pallas_call(
        body,
        name=name,
        out_shape=[jax.ShapeDtypeStruct((N_DEV * rows, cols), v.dtype), jax.ShapeDtypeStruct((rows, cols), v.dtype)],
        in_specs=[vm],
        out_specs=[vm, vm],
        scratch_shapes=[pltpu.SemaphoreType.DMA((7,)), pltpu.SemaphoreType.DMA((7,)), pltpu.SemaphoreType.DMA],
    )(v)


def _hbm_call(body, name, arrs, out_shapes, sem_counts):
    n = len(arrs)

    def wrapped(*refs):
        body(refs[:n], refs[n:2 * n], *refs[2 * n:])

    return pl.pallas_call(
        wrapped,
        name=name,
        out_shape=[jax.ShapeDtypeStruct(s, a.dtype) for s, a in zip(out_shapes, arrs)],
        in_specs=[_HBM] * n,
        out_specs=[_HBM] * n,
        scratch_shapes=[pltpu.SemaphoreType.DMA((k,)) for k in sem_counts],
    )(*arrs)


def _gather_weights(arrs):
    n = len(arrs)

    def body(w_refs, out_refs, send_sems, recv_sems, fsend_sems, frecv_sems):
        x, y, c = _place()
        chips = _other_chips(x, y)
        s_me = 2 * x + y
        pairs = list(zip(w_refs, out_refs))
        first = [_remote(w.at[c], o.at[s_me, c], send_sems.at[3 * a + j], recv_sems.at[3 * a + j], (px, py, c))
                 for a, (w, o) in enumerate(pairs) for j, (px, py) in enumerate(chips)]
        for cp in first:
            cp.start()
        passed = []
        for a, (w, o) in enumerate(pairs):
            for j, (px, py) in enumerate(chips):
                half = o.at[2 * px + py, c]
                _remote(half, half, send_sems.at[3 * a + j], recv_sems.at[3 * a + j], (px, py, c)).wait_recv()
                fwd = _remote(half, half, fsend_sems.at[3 * a + j], frecv_sems.at[3 * a + j], (x, y, 1 - c))
                fwd.start()
                passed.append(fwd)
        for a, (w, o) in enumerate(pairs):
            for j, (px, py) in enumerate(chips):
                half = o.at[2 * px + py, 1 - c]
                _remote(half, half, fsend_sems.at[3 * a + j], frecv_sems.at[3 * a + j], (x, y, 1 - c)).wait_recv()
        for cp in first + passed:
            cp.wait_send()

    return _hbm_call(body, "gather_weights", arrs, [(N_CHIPS,) + a.shape for a in arrs], [3 * n] * 4)


def _swap_halves(arrs):
    n = len(arrs)

    def body(g_refs, a_refs, send_sems, recv_sems):
        x, y, c = _place()
        cps = [_remote(g.at[1 - c], a, send_sems.at[i], recv_sems.at[i], (x, y, 1 - c)) for i, (g, a) in enumerate(zip(g_refs, a_refs))]
        for cp in cps:
            cp.start()
        for cp in cps:
            cp.wait()

    return _hbm_call(body, "grads_to_sibling", arrs, [a.shape[1:] for a in arrs], [n, n])


def _scatter_to_chips(arrs):
    n = len(arrs)

    def body(p_refs, b_refs, send_sems, recv_sems):
        x, y, c = _place()
        cps = [_remote(p.at[2 * px + py], b.at[j], send_sems.at[3 * i + j], recv_sems.at[3 * i + j], (px, py, c))
               for i, (p, b) in enumerate(zip(p_refs, b_refs)) for j, (px, py) in enumerate(_other_chips(x, y))]
        for cp in cps:
            cp.start()
        for cp in cps:
            cp.wait()

    return _hbm_call(body, "grads_to_chips", arrs, [(3,) + a.shape[1:] for a in arrs], [3 * n, 3 * n])


def _share_halves(arrs):
    n = len(arrs)

    def body(q_refs, out_refs, send_sems, recv_sems):
        x, y, c = _place()
        cps = [_remote(q, o, send_sems.at[i], recv_sems.at[i], (x, y, 1 - c)) for i, (q, o) in enumerate(zip(q_refs, out_refs))]
        for cp in cps:
            cp.start()
        for cp in cps:
            cp.wait()

    return _hbm_call(body, "grads_share", arrs, [a.shape for a in arrs], [n, n])


_BIG = (
    ("mlp_w_up", (2, 1024, 1024), "cols"),
    ("mlp_w_down", (2, 1024, 1024), "rows"),
    ("gdn_w_out", (1, 256, 1024), "rows"),
    ("w_kv", (1024, 512), "cols"),
    ("sb_w_q", (1, 256, 1024), "rows"),
    ("sb_w_o", (1, 256, 1024), "rows"),
)
_W_IN_SHARD = (1, 1024, 1028)


def _numel(shape):
    n = 1
    for s in shape:
        n *= s
    return n


_PACK_LEN = sum(_numel(s) for _, s, _ in _BIG)
_PACK_ROWS = -(-_PACK_LEN // (2 * PACK_COLS * PACK_ROW_TILE)) * PACK_ROW_TILE
_PACK_PAD = 2 * _PACK_ROWS * PACK_COLS - _PACK_LEN


def _pack_shards(shards, dtype):
    flat = jnp.concatenate([shards[n].astype(dtype).reshape(-1) for n, _, _ in _BIG] + [jnp.zeros((_PACK_PAD,), dtype)])
    return flat.reshape(2, _PACK_ROWS, PACK_COLS)


def _unpack_shards(packed):
    flat = packed.reshape(-1)
    out, off = {}, 0
    for n, shape, _ in _BIG:
        out[n] = flat[off:off + _numel(shape)].reshape(shape)
        off += _numel(shape)
    return out


def _join(stacked, how):
    nd = stacked.ndim - 1
    ax = nd - 1 if how == "cols" else nd - 2
    moved = jnp.moveaxis(stacked, 0, ax)
    shape = list(stacked.shape[1:])
    shape[ax] *= N_CHIPS
    return moved.reshape(shape)


def _split(full, shard_shape, how):
    nd = len(shard_shape)
    ax = nd - 1 if how == "cols" else nd - 2
    shape = list(shard_shape)
    shape.insert(ax, N_CHIPS)
    return jnp.moveaxis(full.reshape(shape), ax, 0)


def _unpack_full(gathered):
    flat = gathered.reshape(N_CHIPS, -1)
    out, off = {}, 0
    for n, shape, how in _BIG:
        out[n] = _join(flat[:, off:off + _numel(shape)].reshape((N_CHIPS,) + shape), how)
        off += _numel(shape)
    return out


def _pack_full(full):
    flat = jnp.concatenate([_split(full[n], shape, how).reshape(N_CHIPS, -1) for n, shape, how in _BIG] + [jnp.zeros((N_CHIPS, _PACK_PAD), F32)], axis=1)
    return jnp.swapaxes(flat.reshape(N_CHIPS, 2, _PACK_ROWS, PACK_COLS), 0, 1).reshape(2, N_CHIPS * _PACK_ROWS, PACK_COLS)


_SMALL = (
    ("mix_pre_gain", (2, 1024)),
    ("mix_post_gain", (2, 1024)),
    ("mlp_pre_gain", (2, 1024)),
    ("mlp_post_gain", (2, 1024)),
    ("kv_gain", (1024,)),
    ("gdn_out_gain", (1, 128)),
    ("gdn_a_log", (1, 8)),
    ("gdn_dt_bias", (1, 8)),
    ("gdn_conv_w", (1, 4, 3072)),
    ("loss", ()),
)


def _rows_of(shape):
    return -(-_numel(shape) // LANES)


_SMALL_ROWS = -(-sum(_rows_of(s) for _, s in _SMALL) // 8) * 8


def _pack_small(vals):
    parts = []
    for n, shape in _SMALL:
        flat = vals[n].reshape(-1)
        parts.append(jnp.pad(flat, (0, _rows_of(shape) * LANES - flat.shape[0])))
    flat = jnp.concatenate(parts)
    return jnp.pad(flat, (0, _SMALL_ROWS * LANES - flat.shape[0])).reshape(_SMALL_ROWS, LANES)


def _unpack_small(packed):
    flat = packed.reshape(-1)
    out, off = {}, 0
    for n, shape in _SMALL:
        out[n] = flat[off:off + _numel(shape)].reshape(shape)
        off += _rows_of(shape) * LANES
    return out


_WEIGHTS = ("mix_pre_gain", "mix_post_gain", "mlp_pre_gain", "mlp_post_gain", "mlp_w_up", "mlp_w_down", "gdn_w_in", "gdn_conv_w",
            "gdn_a_log", "gdn_dt_bias", "gdn_out_gain", "gdn_w_out", "kv_gain", "w_kv", "sb_w_q", "sb_w_o")


def _as2d(a):
    return a.reshape(1, -1) if a.ndim <= 1 else a.reshape(-1, a.shape[-1])


def kernel(x, mix_pre_gain, mix_post_gain, mlp_pre_gain, mlp_post_gain, mlp_w_up, mlp_w_down, gdn_w_in, gdn_conv_w, gdn_a_log, gdn_dt_bias, gdn_out_gain, gdn_w_out, kv_gain, w_kv, sb_w_q, sb_w_o, loss_target, m_mix_pre_gain, m_mix_post_gain, m_mlp_pre_gain, m_mlp_post_gain, m_mlp_w_up, m_mlp_w_down, m_gdn_w_in, m_gdn_conv_w, m_gdn_a_log, m_gdn_dt_bias, m_gdn_out_gain, m_gdn_w_out, m_kv_gain, m_w_kv, m_sb_w_q, m_sb_w_o, v_mix_pre_gain, v_mix_post_gain, v_mlp_pre_gain, v_mlp_post_gain, v_mlp_w_up, v_mlp_w_down, v_gdn_w_in, v_gdn_conv_w, v_gdn_a_log, v_gdn_dt_bias, v_gdn_out_gain, v_gdn_w_out, v_kv_gain, v_w_kv, v_sb_w_q, v_sb_w_o):
    w = dict(mix_pre_gain=mix_pre_gain, mix_post_gain=mix_post_gain, mlp_pre_gain=mlp_pre_gain, mlp_post_gain=mlp_post_gain, mlp_w_up=mlp_w_up, mlp_w_down=mlp_w_down, gdn_w_in=gdn_w_in, gdn_conv_w=gdn_conv_w, gdn_a_log=gdn_a_log, gdn_dt_bias=gdn_dt_bias, gdn_out_gain=gdn_out_gain, gdn_w_out=gdn_w_out, kv_gain=kv_gain, w_kv=w_kv, sb_w_q=sb_w_q, sb_w_o=sb_w_o)
    m = dict(mix_pre_gain=m_mix_pre_gain, mix_post_gain=m_mix_post_gain, mlp_pre_gain=m_mlp_pre_gain, mlp_post_gain=m_mlp_post_gain, mlp_w_up=m_mlp_w_up, mlp_w_down=m_mlp_w_down, gdn_w_in=m_gdn_w_in, gdn_conv_w=m_gdn_conv_w, gdn_a_log=m_gdn_a_log, gdn_dt_bias=m_gdn_dt_bias, gdn_out_gain=m_gdn_out_gain, gdn_w_out=m_gdn_w_out, kv_gain=m_kv_gain, w_kv=m_w_kv, sb_w_q=m_sb_w_q, sb_w_o=m_sb_w_o)
    v = dict(mix_pre_gain=v_mix_pre_gain, mix_post_gain=v_mix_post_gain, mlp_pre_gain=v_mlp_pre_gain, mlp_post_gain=v_mlp_post_gain, mlp_w_up=v_mlp_w_up, mlp_w_down=v_mlp_w_down, gdn_w_in=v_gdn_w_in, gdn_conv_w=v_gdn_conv_w, gdn_a_log=v_gdn_a_log, gdn_dt_bias=v_gdn_dt_bias, gdn_out_gain=v_gdn_out_gain, gdn_w_out=v_gdn_w_out, kv_gain=v_kv_gain, w_kv=v_w_kv, sb_w_q=v_sb_w_q, sb_w_o=v_sb_w_o)
    cx, cy, cc = _place()
    chip = 2 * cx + cy
    conv_cols = gdn_conv_w.shape[-1]

    in_rows = _W_IN_SHARD[1] // 2
    own = (_pack_shards(w, BF16), gdn_w_in.astype(BF16).reshape(2, in_rows, _W_IN_SHARD[2]))
    packed_all, w_in_all = [lax.dynamic_update_index_in_dim(others, mine, chip, 0) for others, mine in zip(_gather_weights(own), own)]
    full = _unpack_full(packed_all)
    conv_rows = jnp.pad(gdn_conv_w[0], ((0, 8 - CONV_K), (0, 0))).reshape(-1, LANES)
    conv_all, _ = _gather8(conv_rows, "gather_conv_w")
    conv_all = conv_all.reshape(N_CHIPS, 2, 8, conv_cols)[:, 0, :CONV_K]
    conv_full = jnp.swapaxes(conv_all, 0, 1).reshape(CONV_K, N_CHIPS * conv_cols)

    w_in = _join(w_in_all.reshape((N_CHIPS,) + _W_IN_SHARD), "cols")[0]
    wts = (w_in[:, :4 * HEADS * HEAD_DIM], jnp.pad(w_in[:, 4 * HEADS * HEAD_DIM:], ((0, 0), (0, LANES - 2 * HEADS))), full["gdn_w_out"][0], full["w_kv"],
           full["sb_w_q"][0], full["sb_w_o"][0], full["mlp_w_up"], full["mlp_w_down"])
    gains = (mix_pre_gain, mix_post_gain, mlp_pre_gain, mlp_post_gain, kv_gain[None])
    small = (conv_full, gdn_a_log, gdn_dt_bias, gdn_out_gain)
    loss_rows, grad_x, g_full = _local_step(x[0], loss_target[0], gains, wts, small)

    g_in = _split(g_full["gdn_w_in"], _W_IN_SHARD, "cols").reshape(N_CHIPS, 2, in_rows, _W_IN_SHARD[2])
    bufs = (_pack_full(g_full), jnp.swapaxes(g_in, 0, 1).reshape(2, N_CHIPS * in_rows, _W_IN_SHARD[2]))
    from_sibling = _swap_halves(bufs)
    partial, partial_bf16 = [], []
    for i, (buf, other) in enumerate(zip(bufs, from_sibling)):
        cols = buf.shape[-1]
        own_half = lax.dynamic_index_in_dim(buf, cc, 0, keepdims=False)
        p, pb = _rowwise(f"grads_add_sibling_{i}", lambda a, b: (a + b, a + b), [(own_half, cols, 0), (other, cols, 0)], [], [(cols, F32), (cols, BF16)], tm=PACK_ROW_TILE)
        partial.append(p.reshape(N_CHIPS, -1, cols))
        partial_bf16.append(pb.reshape(N_CHIPS, -1, cols))
    from_chips = _scatter_to_chips(tuple(partial_bf16))
    reduced = []
    for i, (p, others) in enumerate(zip(partial, from_chips)):
        cols = p.shape[-1]
        mine = lax.dynamic_index_in_dim(p, chip, 0, keepdims=False)
        (r,) = _rowwise(f"grads_add_chips_{i}", lambda a, b, c, d: (((a + b) + c) + d,),
                        [(mine, cols, 0), (others[0], cols, 0), (others[1], cols, 0), (others[2], cols, 0)], [], [(cols, F32)], tm=PACK_ROW_TILE)
        reduced.append(r)
    both = [jnp.where(cc == 0, jnp.stack([r, o]), jnp.stack([o, r])) for r, o in zip(reduced, _share_halves(tuple(reduced)))]
    g_shard = _unpack_shards(both[0])
    g_shard["gdn_w_in"] = both[1].reshape(_W_IN_SHARD)

    g_small_local = {n: g_full[n] for n, _ in _SMALL if n != "loss"}
    g_small_local["loss"] = loss_rows[0, 0]
    _, small_sum = _gather8(_pack_small(g_small_local), "allreduce_small")
    g_small = _unpack_small(small_sum)
    loss = g_small.pop("loss")
    g_small["gdn_conv_w"] = lax.dynamic_slice_in_dim(g_small["gdn_conv_w"], chip * conv_cols, conv_cols, axis=2)

    grads = {**g_shard, **g_small}
    deltas, new_m, new_v = {}, {}, {}
    for n in _WEIGHTS:
        d2, m2, v2 = _adamw(_as2d(w[n]), _as2d(grads[n]), _as2d(m[n]), _as2d(v[n]), "adamw_" + n)
        deltas[n], new_m[n], new_v[n] = d2.reshape(w[n].shape), m2.reshape(w[n].shape), v2.reshape(w[n].shape)
    return (loss, grad_x[None], *[grads[n].reshape(w[n].shape) for n in _WEIGHTS], *[deltas[n] for n in _WEIGHTS],
            *[new_m[n] for n in _WEIGHTS], *[new_v[n] for n in _WEIGHTS])
```

```python
import functools

import jax
import jax.numpy as jnp
from jax import lax
from jax.experimental import pallas as pl
from jax.experimental.pallas import tpu as pltpu

F32, BF16 = jnp.float32, jnp.bfloat16
HI = lax.Precision.HIGHEST
MESH = pl.DeviceIdType.MESH

EPS = 1e-6
D_MODEL = 1024
HEADS = 8
HEAD_DIM = 128
CHUNK = 64
CHUNK_SHIFT = CHUNK.bit_length() - 1
CONV_K = 4
D_FF = 4096
QKV = 3 * HEADS * HEAD_DIM

ADAM_LR, ADAM_B1, ADAM_B2, ADAM_EPS, ADAM_WD, ADAM_STEP = 0.001, 0.9, 0.999, 1e-08, 0.01, 10

VMEM_LIMIT_BYTES = 48 * 1024 * 1024
LANES = 128

NN = ((1,), (0,))
NT = ((1,), (1,))
TN = ((0,), (0,))


def _dot(a, b, dims=NN, precision=None):
    return lax.dot_general(a, b, (dims, ((), ())), precision=precision, preferred_element_type=F32)


def _params(*sem):
    return pltpu.CompilerParams(dimension_semantics=sem, vmem_limit_bytes=VMEM_LIMIT_BYTES)


def _iota(shape, axis):
    return lax.broadcasted_iota(jnp.int32, shape, axis)


def _matmul(a, b, mode, out_dtype, name, tm=1024, tn=1024, tk=1024, add=None, epilogue=None, extras=()):
    if mode == "nn":
        (m, k), (k2, n) = a.shape, b.shape
    elif mode == "nt":
        (m, k), (n, k2) = a.shape, b.shape
    else:
        (k, m), (k2, n) = a.shape, b.shape
    assert k == k2, (a.shape, b.shape, mode)
    tm, tn, tk = min(tm, m), min(tn, n), min(tk, k)
    assert m % tm == 0 and n % tn == 0 and k % tk == 0, (a.shape, b.shape, mode)
    nk = k // tk
    dims = {"nn": NN, "nt": NT, "tn": TN}[mode]
    tiles = ([add] if add is not None else []) + list(extras)
    out_dtypes = out_dtype if epilogue is not None else (out_dtype,)
    n_in = 2 + len(tiles)

    def body(*refs):
        a_ref, b_ref = refs[:2]
        extra_refs = refs[n_in - len(extras):n_in]
        o_refs, acc_ref = refs[n_in:-1], refs[-1]
        kk = pl.program_id(2)

        @pl.when(kk == 0)
        def _():
            acc_ref[...] = refs[2][...].astype(F32) if add is not None else jnp.zeros_like(acc_ref)

        acc_ref[...] += _dot(a_ref[...].astype(BF16), b_ref[...].astype(BF16), dims)

        @pl.when(kk == nk - 1)
        def _():
            res = (acc_ref[...],) if epilogue is None else epilogue(acc_ref[...], *[r[...] for r in extra_refs])
            for o_ref, r in zip(o_refs, res):
                o_ref[...] = r.astype(o_ref.dtype)

    a_spec = pl.BlockSpec((tk, tm), lambda i, j, kk: (kk, i)) if mode == "tn" else pl.BlockSpec((tm, tk), lambda i, j, kk: (i, kk))
    b_spec = pl.BlockSpec((tn, tk), lambda i, j, kk: (j, kk)) if mode == "nt" else pl.BlockSpec((tk, tn), lambda i, j, kk: (kk, j))
    o_spec = pl.BlockSpec((tm, tn), lambda i, j, kk: (i, j))
    res = pl.pallas_call(
        body,
        name=name,
        grid=(m // tm, n // tn, nk),
        in_specs=[a_spec, b_spec] + [o_spec] * len(tiles),
        out_specs=[o_spec] * len(out_dtypes),
        out_shape=[jax.ShapeDtypeStruct((m, n), dt) for dt in out_dtypes],
        scratch_shapes=[pltpu.VMEM((tm, tn), F32)],
        compiler_params=_params("parallel", "parallel", "arbitrary"),
    )(a, b, *tiles)
    return res if epilogue is not None else res[0]


def _row_specs(rows, tm):
    return [pl.BlockSpec((tm, w), lambda i, cb=cb: (i, cb)) for _, w, cb in rows]


def _full_spec(p):
    return pl.BlockSpec(p.shape, lambda i: (0,) * p.ndim)


def _rowwise(name, fn, rows, params, outs, tm=256):
    t = rows[0][0].shape[0]
    tm = min(tm, t)
    nr, npar = len(rows), len(params)

    def body(*refs):
        ins = [r[...].astype(F32) for r in refs[:nr]]
        ps = [p[...] for p in refs[nr:nr + npar]]
        res = fn(*ins, *ps)
        for o_ref, r in zip(refs[nr + npar:], res):
            o_ref[...] = r.astype(o_ref.dtype)

    return pl.pallas_call(
        body,
        name=name,
        grid=(t // tm,),
        in_specs=_row_specs(rows, tm) + [_full_spec(p) for p in params],
        out_specs=[pl.BlockSpec((tm, w), lambda i: (i, 0)) for w, _ in outs],
        out_shape=[jax.ShapeDtypeStruct((t, w), dt) for w, dt in outs],
        compiler_params=_params("parallel"),
    )(*[r[0] for r in rows], *params)


def _rowwise_bwd(name, fn, rows, params, cots, grad_dtypes, tm=256, joined=False):
    t = rows[0][0].shape[0]
    tm = min(tm, t)
    nr, npar, nc = len(rows), len(params), len(cots)
    want = [j for j, dt in enumerate(grad_dtypes) if dt is not None]
    widths = [rows[j][1] for j in want]
    n_row_outs = 1 if joined else len(want)

    def body(*refs):
        i = pl.program_id(0)
        ins = [r[...].astype(F32) for r in refs[:nr]]
        ps = [p[...] for p in refs[nr:nr + npar]]
        cs = tuple(c[...].astype(F32) for c in refs[nr + npar:nr + npar + nc])
        _, vjp = jax.vjp(fn, *ins, *ps)
        gs = vjp(cs)
        outs = refs[nr + npar + nc:]
        if joined:
            off = 0
            for j, w in zip(want, widths):
                outs[0][:, off:off + w] = gs[j].astype(outs[0].dtype)
                off += w
        else:
            for o_ref, j in zip(outs, want):
                o_ref[...] = gs[j].astype(o_ref.dtype)
        pg_refs = outs[n_row_outs:]

        @pl.when(i == 0)
        def _():
            for pg in pg_refs:
                pg[...] = jnp.zeros_like(pg)

        for pg, g in zip(pg_refs, gs[nr:]):
            pg[...] += g

    if joined:
        row_specs = [pl.BlockSpec((tm, sum(widths)), lambda i: (i, 0))]
        row_shapes = [jax.ShapeDtypeStruct((t, sum(widths)), grad_dtypes[want[0]])]
    else:
        row_specs = [pl.BlockSpec((tm, w), lambda i: (i, 0)) for w in widths]
        row_shapes = [jax.ShapeDtypeStruct((t, w), grad_dtypes[j]) for j, w in zip(want, widths)]
    res = pl.pallas_call(
        body,
        name=name,
        grid=(t // tm,),
        in_specs=_row_specs(rows, tm) + [_full_spec(p) for p in params] + [pl.BlockSpec((tm, c.shape[1]), lambda i: (i, 0)) for c in cots],
        out_specs=row_specs + [_full_spec(p) for p in params],
        out_shape=row_shapes + [jax.ShapeDtypeStruct(p.shape, F32) for p in params],
        compiler_params=_params("arbitrary"),
    )(*[r[0] for r in rows], *params, *cots)
    return res[:n_row_outs], res[n_row_outs:]


def _rms(x, g):
    return x * lax.rsqrt(jnp.mean(x * x, axis=-1, keepdims=True) + EPS) * g


def _sigmoid(x):
    return 1.0 / (1.0 + jnp.exp(-x))


def _softplus(x):
    return jnp.maximum(x, 0.0) + jnp.log1p(jnp.exp(-jnp.abs(x)))


def _two_pass(x, m):
    hi = x.astype(BF16)
    lo = (x - hi.astype(F32)).astype(BF16)
    return _dot(hi, m) + _dot(lo, m)


def _head_sum_impl(x):
    w = HEADS * HEAD_DIM
    fold = jnp.where((_iota((w, LANES), 0) >> 7) == _iota((w, LANES), 1), 1.0, 0.0).astype(BF16)
    spread = jnp.where(_iota((LANES, w), 0) == (_iota((LANES, w), 1) >> 7), 1.0, 0.0).astype(BF16)
    return _two_pass(_two_pass(x, fold), spread)


@jax.custom_vjp
def _head_sum(x):
    return _head_sum_impl(x)


_head_sum.defvjp(lambda x: (_head_sum_impl(x), None), lambda _, g: (_head_sum_impl(g),))


def _fn_norm(x, g):
    return (_rms(x, g),)


def _fn_gates(ba, al, dt):
    col = _iota((1, LANES), 1)
    g = jnp.where((col >= HEADS) & (col < 2 * HEADS), -jnp.exp(al) * _softplus(ba + dt), 0.0)
    rows = ba.shape[0]
    r, c = _iota((rows, rows), 0), _iota((rows, rows), 1)
    same = (r >> CHUNK_SHIFT) == (c >> CHUNK_SHIFT)
    gc = _dot(jnp.where(same & (r >= c), 1.0, 0.0), g, precision=HI)
    gtot = _dot(jnp.where(same, 1.0, 0.0), g, precision=HI)
    return _sigmoid(ba), gc, gtot


def _fn_post_q(c):
    s = c * _sigmoid(c)
    return (s * lax.rsqrt(_head_sum(s * s) + EPS) * (HEAD_DIM ** -0.5),)


def _fn_post_k(c):
    s = c * _sigmoid(c)
    return (s * lax.rsqrt(_head_sum(s * s) + EPS),)


def _fn_post_v(c):
    return (c * _sigmoid(c),)


def _fn_post(cq, ck, cv):
    return _fn_post_q(cq) + _fn_post_k(ck) + _fn_post_v(cv)


def _fn_outnorm(o, gate, og):
    y = o * lax.rsqrt(_head_sum(o * o) * (1.0 / HEAD_DIM) + EPS) * og
    return (y * (gate * _sigmoid(gate)),)


def _fn_res_norm(x, m, gp, gn):
    x1 = x + _rms(m, gp)
    return x1, _rms(x1, gn)


def _fn_res_norm2(x, m, gp, ga, gb):
    x1 = x + _rms(m, gp)
    return x1, _rms(x1, ga), _rms(x1, gb)


def _relu2_of(u):
    r = jnp.maximum(u, 0.0)
    return (r * r,)


def _relu2_cotangent(da, a):
    return (da * (2.0 * jnp.sqrt(a.astype(F32))),)


def _loss_call(x3, d1, tgt, g, tm=256):
    t, d = x3.shape
    tm = min(tm, t)

    def body(x_ref, d_ref, t_ref, g_ref, loss_ref, dx_ref, dd_ref, dg_ref):
        i = pl.program_id(0)
        y, vjp = jax.vjp(lambda x, dd, gg: x + _rms(dd, gg), x_ref[...], d_ref[...], g_ref[...])
        err = y - t_ref[...]
        lrow = 0.5 * jnp.mean(err * err, axis=-1, keepdims=True)
        dx, dd, dg = vjp(err * (1.0 / d))
        dx_ref[...] = dx
        dd_ref[...] = dd.astype(dd_ref.dtype)

        @pl.when(i == 0)
        def _():
            loss_ref[...] = jnp.zeros_like(loss_ref)
            dg_ref[...] = jnp.zeros_like(dg_ref)

        loss_ref[...] += jnp.broadcast_to(jnp.sum(lrow, axis=0, keepdims=True), loss_ref.shape)
        dg_ref[...] += dg

    row = pl.BlockSpec((tm, d), lambda i: (i, 0))
    return pl.pallas_call(
        body,
        name="loss_head",
        grid=(t // tm,),
        in_specs=[row, row, row, _full_spec(g)],
        out_specs=[pl.BlockSpec((8, LANES), lambda i: (0, 0)), row, row, _full_spec(g)],
        out_shape=[jax.ShapeDtypeStruct((8, LANES), F32), jax.ShapeDtypeStruct((t, d), F32), jax.ShapeDtypeStruct((t, d), BF16), jax.ShapeDtypeStruct(g.shape, F32)],
        compiler_params=_params("arbitrary"),
    )(x3, d1, tgt, g)


HALO = 8


def _conv_fwd(qkvg, conv_w, tm=256):
    t = qkvg.shape[0]
    tm = min(tm, t)

    wide = QKV // 3

    def body(cur_ref, prev_ref, w_ref, o_ref, q_ref, k_ref, v_ref, buf):
        i = pl.program_id(0)
        buf[0:HALO, :] = jnp.where(i > 0, prev_ref[...], 0.0)
        buf[HALO:, :] = cur_ref[...]
        acc = buf[pl.ds(HALO - CONV_K + 1, tm), :] * w_ref[pl.ds(0, 1), :]
        for j in range(1, CONV_K):
            acc = acc + buf[pl.ds(HALO - CONV_K + 1 + j, tm), :] * w_ref[pl.ds(j, 1), :]
        o_ref[...] = acc
        (q_ref[...], k_ref[...], v_ref[...]) = _fn_post(acc[:, 0:wide], acc[:, wide:2 * wide], acc[:, 2 * wide:])

    part = pl.BlockSpec((tm, wide), lambda i: (i, 0))
    return pl.pallas_call(
        body,
        name="conv_fwd",
        grid=(t // tm,),
        in_specs=[
            pl.BlockSpec((tm, QKV), lambda i: (i, 0)),
            pl.BlockSpec((HALO, QKV), lambda i: (jnp.maximum(i * (tm // HALO) - 1, 0), 0)),
            pl.BlockSpec((CONV_K, QKV), lambda i: (0, 0)),
        ],
        out_specs=[pl.BlockSpec((tm, QKV), lambda i: (i, 0)), part, part, part],
        out_shape=[jax.ShapeDtypeStruct((t, QKV), F32)] + [jax.ShapeDtypeStruct((t, wide), F32)] * 3,
        scratch_shapes=[pltpu.VMEM((tm + HALO, QKV), F32)],
        compiler_params=_params("parallel"),
    )(qkvg, qkvg, conv_w)


def _conv_bwd(dc, dgate, qkvg, conv_w, tm=256):
    t = dc.shape[0]
    tm = min(tm, t)
    n = t // tm
    wg = dgate.shape[1]

    def body(dc_ref, dcn_ref, dgate_ref, x_ref, xp_ref, w_ref, dx_ref, dw_ref, bufd, bufx):
        i = pl.program_id(0)
        bufd[0:tm, :] = dc_ref[...]
        bufd[tm:, :] = jnp.where(i < n - 1, dcn_ref[...], 0.0)
        bufx[0:HALO, :] = jnp.where(i > 0, xp_ref[...], 0.0)
        bufx[HALO:, :] = x_ref[...]

        @pl.when(i == 0)
        def _():
            dw_ref[...] = jnp.zeros_like(dw_ref)

        dcv = dc_ref[...]
        acc = bufd[pl.ds(CONV_K - 1, tm), :] * w_ref[pl.ds(0, 1), :]
        for j in range(1, CONV_K):
            acc = acc + bufd[pl.ds(CONV_K - 1 - j, tm), :] * w_ref[pl.ds(j, 1), :]
        dx_ref[:, 0:QKV] = acc.astype(dx_ref.dtype)
        dx_ref[:, QKV:] = dgate_ref[...].astype(dx_ref.dtype)
        for j in range(CONV_K):
            dw_ref[pl.ds(j, 1), :] += jnp.sum(dcv * bufx[pl.ds(HALO - CONV_K + 1 + j, tm), :], axis=0, keepdims=True)

    return pl.pallas_call(
        body,
        name="conv_bwd",
        grid=(n,),
        in_specs=[
            pl.BlockSpec((tm, QKV), lambda i: (i, 0)),
            pl.BlockSpec((HALO, QKV), lambda i: (jnp.minimum((i + 1) * (tm // HALO), t // HALO - 1), 0)),
            pl.BlockSpec((tm, wg), lambda i: (i, 0)),
            pl.BlockSpec((tm, QKV), lambda i: (i, 0)),
            pl.BlockSpec((HALO, QKV), lambda i: (jnp.maximum(i * (tm // HALO) - 1, 0), 0)),
            pl.BlockSpec((CONV_K, QKV), lambda i: (0, 0)),
        ],
        out_specs=[pl.BlockSpec((tm, QKV + wg), lambda i: (i, 0)), pl.BlockSpec((HALO, QKV), lambda i: (0, 0))],
        out_shape=[jax.ShapeDtypeStruct((t, QKV + wg), BF16), jax.ShapeDtypeStruct((HALO, QKV), F32)],
        scratch_shapes=[pltpu.VMEM((tm + HALO, QKV), F32), pltpu.VMEM((tm + HALO, QKV), F32)],
        compiler_params=_params("arbitrary"),
    )(dc, dc, dgate, qkvg, qkvg, conv_w)


PREP_CHUNKS = 16
PREP_BWD_CHUNKS = 4


def _hi_lo(x):
    hi = x.astype(BF16)
    return hi, (x - hi.astype(F32)).astype(BF16)


def _mm3(a, b, dims=NN):
    (ah, al), (bh, bl) = _hi_lo(a), _hi_lo(b)
    return _dot(ah, bh, dims) + (_dot(ah, bl, dims) + _dot(al, bh, dims))


def _neumann(lowers):
    c = lowers[0].shape[0]
    eye = jnp.where(_iota((c, c), 0) == _iota((c, c), 1), 1.0, 0.0)
    ps = [-low for low in lowers]
    tmats = [eye + p for p in ps]
    for _ in range(CHUNK_SHIFT - 1):
        ps = [_mm3(p, p) for p in ps]
        tmats = [t + _mm3(t, p) for t, p in zip(tmats, ps)]
    return tuple(tmats)


def _inv_cotangents(tmats, dts):
    half = [_mm3(t, dt, TN) for t, dt in zip(tmats, dts)]
    return tuple(-_mm3(hf, t, NT) for hf, t in zip(half, tmats))


@jax.custom_vjp
def _tri_inv(lowers):
    return _neumann(lowers)


def _tri_inv_fwd(lowers):
    tmats = _neumann(lowers)
    return tmats, tmats


_tri_inv.defvjp(_tri_inv_fwd, lambda tmats, dts: (_inv_cotangents(tmats, dts),))


@jax.custom_vjp
def _tri_inv_known(lowers, tmats):
    return tmats


_tri_inv_known.defvjp(lambda lowers, tmats: (tmats, tmats),
                      lambda tmats, dts: (_inv_cotangents(tmats, dts), tuple(jnp.zeros_like(t) for t in tmats)))


def _prep_chunks(qs, ks, vs, bs, gcs, gts, gcrs, tmats=None):
    c = CHUNK
    r, col = _iota((c, c), 0), _iota((c, c), 1)
    incl, strict = r >= col, r > col
    decays = [jnp.where(incl, jnp.exp(jnp.where(incl, gc - gcr, 0.0)), 0.0) for gc, gcr in zip(gcs, gcrs)]
    kbs = [k * b for k, b in zip(ks, bs)]
    kbfs = [k.astype(BF16) for k in ks]
    lowers = tuple(jnp.where(strict, _dot(kb.astype(BF16), kbf, NT) * decay, 0.0) for kb, kbf, decay in zip(kbs, kbfs, decays))
    tmats = _tri_inv(lowers) if tmats is None else _tri_inv_known(lowers, tuple(tmats))
    outs = []
    for q, k, v, b, gc, gt, kb, kbf, decay, tmat in zip(qs, ks, vs, bs, gcs, gts, kbs, kbfs, decays, tmats):
        tb = tmat.astype(BF16)
        egc = jnp.exp(gc)
        w = _dot(tb, (kb * egc).astype(BF16))
        u = _dot(tb, (v * b).astype(BF16))
        attn = _dot(q.astype(BF16), kbf, NT) * decay
        gl = jnp.broadcast_to(jnp.exp(jnp.mean(gt.reshape(c // 8, 8, 1), axis=0)), (8, HEAD_DIM))
        outs.append((w, u, q * egc, k * jnp.exp(gt - gc), attn, gl))
    return tuple(outs), tmats


def _prep_specs(rows, gch):
    head = pl.BlockSpec((rows, HEAD_DIM), lambda n, h: (n, h))
    gates = pl.BlockSpec((rows, LANES), lambda n, h: (n, 0))
    gcrow = pl.BlockSpec((1, gch, 1, CHUNK), lambda n, h: (h, n, 0, 0))
    square = pl.BlockSpec((1, rows, CHUNK), lambda n, h: (h, n, 0))
    gl = pl.BlockSpec((1, gch * 8, HEAD_DIM), lambda n, h: (h, n, 0))
    return head, gates, gcrow, square, gl


def _pick_lane(ref, sl, lane):
    return jnp.sum(jnp.where(_iota((1, LANES), 1) == lane, ref[sl, :], 0.0), axis=1, keepdims=True)


def _prep_inputs(q_ref, k_ref, v_ref, b_ref, gc_ref, gt_ref, gcr_ref, sls, h):
    return ([q_ref[sl, :] for sl in sls], [k_ref[sl, :] for sl in sls], [v_ref[sl, :] for sl in sls],
            [_pick_lane(b_ref, sl, h) for sl in sls], [_pick_lane(gc_ref, sl, h + HEADS) for sl in sls],
            [_pick_lane(gt_ref, sl, h + HEADS) for sl in sls], [gcr_ref[0, c] for c in range(len(sls))])


def _gdn_prep(q, k, v, beta, gc, gt, gcr):
    t = q.shape[0]
    gch = min(PREP_CHUNKS, t // CHUNK)
    rows = gch * CHUNK

    def body(q_ref, k_ref, v_ref, b_ref, gc_ref, gt_ref, gcr_ref, w_ref, u_ref, qg_ref, kg_ref, at_ref, gl_ref, tm_ref):
        h = pl.program_id(1)
        sls = [pl.ds(c * CHUNK, CHUNK) for c in range(gch)]
        outs, tmats = _prep_chunks(*_prep_inputs(q_ref, k_ref, v_ref, b_ref, gc_ref, gt_ref, gcr_ref, sls, h))
        for c, (sl, (w, u, qg, kg, attn, gl), tmat) in enumerate(zip(sls, outs, tmats)):
            w_ref[sl, :] = w.astype(BF16)
            u_ref[sl, :] = u
            qg_ref[sl, :] = qg.astype(BF16)
            kg_ref[sl, :] = kg.astype(BF16)
            at_ref[0, sl, :] = attn.astype(BF16)
            gl_ref[0, pl.ds(c * 8, 8), :] = gl
            tm_ref[0, sl, :] = tmat

    hb, col, gcrow, square, glb = _prep_specs(rows, gch)
    wide = HEADS * HEAD_DIM
    return pl.pallas_call(
        body,
        name="gdn_prep",
        grid=(t // rows, HEADS),
        in_specs=[hb, hb, hb, col, col, col, gcrow],
        out_specs=[hb, hb, hb, hb, square, glb, square],
        out_shape=[
            jax.ShapeDtypeStruct((t, wide), BF16),
            jax.ShapeDtypeStruct((t, wide), F32),
            jax.ShapeDtypeStruct((t, wide), BF16),
            jax.ShapeDtypeStruct((t, wide), BF16),
            jax.ShapeDtypeStruct((HEADS, t, CHUNK), BF16),
            jax.ShapeDtypeStruct((HEADS, t // CHUNK * 8, HEAD_DIM), F32),
            jax.ShapeDtypeStruct((HEADS, t, CHUNK), F32),
        ],
        compiler_params=_params("parallel", "parallel"),
    )(q, k, v, beta, gc, gt, gcr)


def _gdn_prep_bwd(q, k, v, beta, gc, gt, gcr, tmat, dw, du, dqg, dkg, dattn, dgl):
    t = q.shape[0]
    gch = min(PREP_BWD_CHUNKS, t // CHUNK)
    rows = gch * CHUNK

    def body(q_ref, k_ref, v_ref, b_ref, gc_ref, gt_ref, gcr_ref, tm_ref, dw_ref, du_ref, dqg_ref, dkg_ref, dat_ref, dgl_ref,
             dq_ref, dk_ref, dv_ref, db_ref, dgc_ref, dgt_ref, dgcr_ref):
        h = pl.program_id(1)
        lane = _iota((1, LANES), 1)

        @pl.when(h == 0)
        def _():
            db_ref[...] = jnp.zeros_like(db_ref)
            dgc_ref[...] = jnp.zeros_like(dgc_ref)
            dgt_ref[...] = jnp.zeros_like(dgt_ref)

        sls = [pl.ds(c * CHUNK, CHUNK) for c in range(gch)]
        known = [tm_ref[0, sl, :] for sl in sls]
        _, vjp = jax.vjp(lambda *a: _prep_chunks(*a, tmats=known)[0], *_prep_inputs(q_ref, k_ref, v_ref, b_ref, gc_ref, gt_ref, gcr_ref, sls, h))
        cots = tuple((dw_ref[sl, :], du_ref[sl, :], dqg_ref[sl, :], dkg_ref[sl, :], dat_ref[0, sl, :], dgl_ref[0, pl.ds(c * 8, 8), :]) for c, sl in enumerate(sls))
        dqs, dks, dvs, dbs, dgcs, dgts, dgcrs = vjp(cots)
        for c, sl in enumerate(sls):
            dq_ref[sl, :] = dqs[c]
            dk_ref[sl, :] = dks[c]
            dv_ref[sl, :] = dvs[c]
            db_ref[sl, :] += jnp.where(lane == h, dbs[c], 0.0)
            dgc_ref[sl, :] += jnp.where(lane == h + HEADS, dgcs[c], 0.0)
            dgt_ref[sl, :] += jnp.where(lane == h + HEADS, dgts[c], 0.0)
            dgcr_ref[0, c] = dgcrs[c]

    hb, col, gcrow, square, glb = _prep_specs(rows, gch)
    wide = HEADS * HEAD_DIM
    return pl.pallas_call(
        body,
        name="gdn_prep_bwd",
        grid=(t // rows, HEADS),
        in_specs=[hb, hb, hb, col, col, col, gcrow, square, hb, hb, hb, hb, square, glb],
        out_specs=[hb, hb, hb, col, col, col, gcrow],
        out_shape=[jax.ShapeDtypeStruct((t, wide), F32)] * 3 + [jax.ShapeDtypeStruct((t, LANES), F32)] * 3 + [jax.ShapeDtypeStruct((HEADS, t // CHUNK, 1, CHUNK), F32)],
        compiler_params=_params("parallel", "arbitrary"),
    )(q, k, v, beta, gc, gt, gcr, tmat, dw, du, dqg, dkg, dattn, dgl)


def _gdn_scan(w, u, qg, kg, attn, gl):
    t = w.shape[0]
    n = t // CHUNK
    wide = HEADS * HEAD_DIM

    def body(w_ref, u_ref, qg_ref, kg_ref, at_ref, gl_ref, o_ref, st_ref, s_ref):
        @pl.when(pl.program_id(0) == 0)
        def _():
            s_ref[...] = jnp.zeros_like(s_ref)

        heads = range(HEADS)
        cols = [pl.ds(h * HEAD_DIM, HEAD_DIM) for h in heads]
        ss = [s_ref[h] for h in heads]
        sbs = [s.astype(BF16) for s in ss]
        vbs = [(u_ref[:, hs] - _dot(w_ref[:, hs], sb)).astype(BF16) for hs, sb in zip(cols, sbs)]
        outs = [_dot(qg_ref[:, hs], sb) + _dot(at_ref[h], vb) for h, hs, sb, vb in zip(heads, cols, sbs, vbs)]
        new = [s * jnp.tile(gl_ref[h], (HEAD_DIM // 8, 1)) + _dot(kg_ref[:, hs], vb, TN) for h, hs, s, vb in zip(heads, cols, ss, vbs)]
        for h, hs in zip(heads, cols):
            st_ref[0, h] = ss[h]
            o_ref[:, hs] = outs[h]
            s_ref[h] = new[h]

    row = pl.BlockSpec((CHUNK, wide), lambda i: (i, 0))
    return pl.pallas_call(
        body,
        name="gdn_scan",
        grid=(n,),
        in_specs=[row, row, row, row, pl.BlockSpec((HEADS, CHUNK, CHUNK), lambda i: (0, i, 0)), pl.BlockSpec((HEADS, 8, HEAD_DIM), lambda i: (0, i, 0))],
        out_specs=[row, pl.BlockSpec((1, HEADS, HEAD_DIM, HEAD_DIM), lambda i: (i, 0, 0, 0))],
        out_shape=[jax.ShapeDtypeStruct((t, wide), F32), jax.ShapeDtypeStruct((n, HEADS, HEAD_DIM, HEAD_DIM), F32)],
        scratch_shapes=[pltpu.VMEM((HEADS, HEAD_DIM, HEAD_DIM), F32)],
        compiler_params=_params("arbitrary"),
    )(w, u, qg, kg, attn, gl)


def _gdn_scan_bwd(w, u, qg, kg, attn, gl, states, do):
    t = w.shape[0]
    n = t // CHUNK
    wide = HEADS * HEAD_DIM

    def body(w_ref, u_ref, qg_ref, kg_ref, at_ref, gl_ref, st_ref, do_ref, dw_ref, du_ref, dqg_ref, dkg_ref, dat_ref, dgl_ref, ds_ref):
        @pl.when(pl.program_id(0) == 0)
        def _():
            ds_ref[...] = jnp.zeros_like(ds_ref)

        heads = range(HEADS)
        cols = [pl.ds(h * HEAD_DIM, HEAD_DIM) for h in heads]
        ss = [st_ref[0, h] for h in heads]
        sbs = [s.astype(BF16) for s in ss]
        dsns = [ds_ref[h] for h in heads]
        dsbs = [d.astype(BF16) for d in dsns]
        dobs = [do_ref[:, hs].astype(BF16) for hs in cols]
        vbs = [(u_ref[:, hs] - _dot(w_ref[:, hs], sb)).astype(BF16) for hs, sb in zip(cols, sbs)]
        dvns = [_dot(at_ref[h], dob, TN) + _dot(kg_ref[:, hs], dsb) for h, hs, dob, dsb in zip(heads, cols, dobs, dsbs)]
        dvbs = [d.astype(BF16) for d in dvns]
        for h, hs in zip(heads, cols):
            dat_ref[h] = _dot(dobs[h], vbs[h], NT)
            dqg_ref[:, hs] = _dot(dobs[h], sbs[h], NT)
            dkg_ref[:, hs] = _dot(vbs[h], dsbs[h], NT)
            du_ref[:, hs] = dvns[h]
            dw_ref[:, hs] = -_dot(dvbs[h], sbs[h], NT)
            dgl_ref[h] = jnp.sum((dsns[h] * ss[h]).reshape(HEAD_DIM // 8, 8, HEAD_DIM), axis=0)
        new = [dsn * jnp.tile(gl_ref[h], (HEAD_DIM // 8, 1)) + _dot(qg_ref[:, hs], dob, TN) - _dot(w_ref[:, hs], dvb, TN)
               for h, hs, dsn, dob, dvb in zip(heads, cols, dsns, dobs, dvbs)]
        for h in heads:
            ds_ref[h] = new[h]

    row = pl.BlockSpec((CHUNK, wide), lambda i: (n - 1 - i, 0))
    at = pl.BlockSpec((HEADS, CHUNK, CHUNK), lambda i: (0, n - 1 - i, 0))
    glb = pl.BlockSpec((HEADS, 8, HEAD_DIM), lambda i: (0, n - 1 - i, 0))
    return pl.pallas_call(
        body,
        name="gdn_scan_bwd",
        grid=(n,),
        in_specs=[row, row, row, row, at, glb, pl.BlockSpec((1, HEADS, HEAD_DIM, HEAD_DIM), lambda i: (n - 1 - i, 0, 0, 0)), row],
        out_specs=[row, row, row, row, at, glb],
        out_shape=[jax.ShapeDtypeStruct((t, wide), F32)] * 4 + [jax.ShapeDtypeStruct((HEADS, t, CHUNK), F32), jax.ShapeDtypeStruct((HEADS, n * 8, HEAD_DIM), F32)],
        scratch_shapes=[pltpu.VMEM((HEADS, HEAD_DIM, HEAD_DIM), F32)],
        compiler_params=_params("arbitrary"),
    )(w, u, qg, kg, attn, gl, states, do)


SB_Q = 1024
SB_K = 256


def _sb_scores(q, k):
    z = _dot(q, k, NT) * (HEAD_DIM ** -0.5)
    e = jnp.exp(-jnp.abs(z))
    lb = jnp.minimum(z, 0.0) - jnp.log(1.0 + e)
    return z, e, lb, lb - z


def _tri(n, rel):
    return jnp.where(rel(_iota((n, n), 0), _iota((n, n), 1)), 1.0, 0.0).astype(BF16)


def _lanes(col):
    return jnp.broadcast_to(col, (col.shape[0], LANES))


def _sb_fwd(q, k, v):
    t = q.shape[0]
    bq, bk = min(SB_Q, t), min(SB_K, t)
    nsub, rep = bq // bk, bk // LANES

    def body(q_ref, k_ref, v_ref, o_ref, rt_ref):
        i = pl.program_id(1)
        o_ref[...] = jnp.zeros_like(o_ref)
        rt_ref[...] = jnp.zeros_like(rt_ref)
        after = _tri(bk, lambda r, c: r > c)

        def block(j, r0, diag):
            st = pl.multiple_of(j * bk, bk)
            kv, vv = k_ref[pl.ds(st, bk), :], v_ref[pl.ds(st, bk), :]
            _, _, lb, l1m = _sb_scores(q_ref[r0:, :], kv)
            if diag:
                mask = _iota((bq - r0, bk), 1) + j * bk < _iota((bq - r0, bk), 0) + (r0 + i * bq)
                l1m = jnp.where(mask, l1m, 0.0)
            sums = _two_pass(l1m, after)
            run = rt_ref[r0:, :]
            a = jnp.exp(lb + jnp.tile(run, (1, rep)) + sums)
            if diag:
                a = jnp.where(mask, a, 0.0)
            o_ref[r0:, :] += _dot(a.astype(BF16), vv)
            rt_ref[r0:, :] = run + _lanes(sums[:, 0:1] + l1m[:, 0:1])

        def group(p, diag):
            for s in reversed(range(nsub)):
                block(p * nsub + s, s * bk if diag else 0, diag)

        def left_of_diagonal(jj, carry):
            group(i - jj, False)
            return carry

        group(i, True)
        lax.fori_loop(1, i + 1, left_of_diagonal, 0)

    qb = pl.BlockSpec((bq, HEAD_DIM), lambda h, i: (i, h))
    full = pl.BlockSpec((t, HEAD_DIM), lambda h, i: (0, h))
    return pl.pallas_call(
        body,
        name="sb_fwd",
        grid=(HEADS, t // bq),
        in_specs=[qb, full, full],
        out_specs=[qb, qb],
        out_shape=[jax.ShapeDtypeStruct(q.shape, F32), jax.ShapeDtypeStruct(q.shape, F32)],
        compiler_params=_params("parallel", "arbitrary"),
    )(q, k, v)


def _sb_bwd(q, k, v, rt, do):
    t = q.shape[0]
    bq, bk = min(SB_Q, t), min(SB_K, t)
    nsub, rep = bq // bk, bk // LANES
    scale = HEAD_DIM ** -0.5

    def body(q_ref, k_ref, v_ref, rt_ref, do_ref, dq_ref, dk_ref, dv_ref, left_ref, pg_ref):
        i = pl.program_id(1)

        @pl.when(i == 0)
        def _():
            dk_ref[...] = jnp.zeros_like(dk_ref)
            dv_ref[...] = jnp.zeros_like(dv_ref)

        dq_ref[...] = jnp.zeros_like(dq_ref)
        left_ref[...] = jnp.zeros_like(left_ref)
        pg_ref[...] = jnp.zeros_like(pg_ref)
        upto = _tri(bk, lambda r, c: r <= c)

        def block(j, r0, diag):
            st = pl.multiple_of(j * bk, bk)
            kv, vv = k_ref[pl.ds(st, bk), :], v_ref[pl.ds(st, bk), :]
            qv = q_ref[r0:, :]
            dob = do_ref[r0:, :].astype(BF16)
            _, _, lb, l1m = _sb_scores(qv, kv)
            if diag:
                mask = _iota((bq - r0, bk), 1) + j * bk < _iota((bq - r0, bk), 0) + (r0 + i * bq)
                l1m = jnp.where(mask, l1m, 0.0)
            sums = _two_pass(l1m, upto)
            left = left_ref[r0:, :]
            a = jnp.exp(lb + jnp.tile(rt_ref[r0:, :] - left, (1, rep)) - sums)
            if diag:
                a = jnp.where(mask, a, 0.0)
            g = _dot(dob, vv, NT) * a
            dv_ref[pl.ds(st, bk), :] += _dot(a.astype(BF16), dob, TN)
            gsum = _two_pass(g, upto)
            pg = pg_ref[r0:, :]
            dz = g - jnp.exp(lb) * (jnp.tile(pg, (1, rep)) + gsum)
            if diag:
                dz = jnp.where(mask, dz, 0.0)
            dzb = (dz * scale).astype(BF16)
            dk_ref[pl.ds(st, bk), :] += _dot(dzb, qv, TN)
            dq_ref[r0:, :] += _dot(dzb, kv)
            left_ref[r0:, :] = left + _lanes(sums[:, bk - 1:bk])
            pg_ref[r0:, :] = pg + _lanes(gsum[:, bk - 1:bk])

        def group(p, diag):
            for s in range(nsub):
                block(p * nsub + s, s * bk if diag else 0, diag)

        def left_of_diagonal(p, carry):
            group(p, False)
            return carry

        lax.fori_loop(0, i, left_of_diagonal, 0)
        group(i, True)

    qb = pl.BlockSpec((bq, HEAD_DIM), lambda h, i: (i, h))
    full = pl.BlockSpec((t, HEAD_DIM), lambda h, i: (0, h))
    return pl.pallas_call(
        body,
        name="sb_bwd",
        grid=(HEADS, t // bq),
        in_specs=[qb, full, full, qb, qb],
        out_specs=[qb, full, full],
        out_shape=[jax.ShapeDtypeStruct(q.shape, F32)] * 3,
        scratch_shapes=[pltpu.VMEM((bq, LANES), F32), pltpu.VMEM((bq, LANES), F32)],
        compiler_params=_params("parallel", "arbitrary"),
    )(q, k, v, rt, do)


def _adamw(w, g, m, v, name, tm=256):
    r, c = w.shape
    tm = tm if r % tm == 0 else r

    def body(w_ref, g_ref, m_ref, v_ref, d_ref, nm_ref, nv_ref):
        gv = g_ref[...]
        nm = ADAM_B1 * m_ref[...] + (1.0 - ADAM_B1) * gv
        nv = ADAM_B2 * v_ref[...] + (1.0 - ADAM_B2) * (gv * gv)
        m_hat = nm / (1.0 - ADAM_B1 ** ADAM_STEP)
        v_hat = nv / (1.0 - ADAM_B2 ** ADAM_STEP)
        d_ref[...] = -ADAM_LR * (m_hat / (jnp.sqrt(v_hat) + ADAM_EPS) + ADAM_WD * w_ref[...])
        nm_ref[...] = nm
        nv_ref[...] = nv

    blk = pl.BlockSpec((tm, c), lambda i: (i, 0))
    return pl.pallas_call(
        body,
        name=name,
        grid=(r // tm,),
        in_specs=[blk] * 4,
        out_specs=[blk] * 3,
        out_shape=[jax.ShapeDtypeStruct((r, c), F32)] * 3,
        compiler_params=_params("parallel"),
    )(w, g, m, v)


def _local_step(x, tgt, gains, wts, small):
    mix_pre, mix_post, mlp_pre, mlp_post, kv_gain = gains
    w_qkvg, w_ba, w_out, w_kv, w_q, w_o, w_up, w_down = wts
    conv_w, a_log, dt_bias, out_gain = small
    t, d = x.shape
    row = lambda a, i=None: a[i:i + 1] if i is not None else a
    al = jnp.zeros((1, LANES), F32).at[:, HEADS:2 * HEADS].set(a_log)
    dtb = jnp.zeros((1, LANES), F32).at[:, HEADS:2 * HEADS].set(dt_bias)
    og = jnp.tile(out_gain, (1, HEADS))
    full = lambda a: (a, a.shape[1], 0)

    (h0,) = _rowwise("norm_in", _fn_norm, [full(x)], [row(mix_pre, 0)], [(d, BF16)])
    qkvg = _matmul(h0, w_qkvg, "nn", F32, "mm_gdn_in", tk=1024)
    ba = _matmul(h0, w_ba, "nn", F32, "mm_gdn_ba", tk=1024)
    conv, gq, gk, gv = _conv_fwd(qkvg, conv_w)
    conv_qkv = [(conv, d, 0), (conv, d, 1), (conv, d, 2)]
    beta, gc, gt = _rowwise("gates", _fn_gates, [full(ba)], [al, dtb], [(LANES, F32)] * 3)
    gcr = jnp.swapaxes(gc[:, HEADS:2 * HEADS], 0, 1).reshape(HEADS, t // CHUNK, 1, CHUNK)
    pw, pu, pqg, pkg, pattn, pgl, ptm = _gdn_prep(gq, gk, gv, beta, gc, gt, gcr)
    o_gdn, states = _gdn_scan(pw, pu, pqg, pkg, pattn, pgl)
    (on,) = _rowwise("out_norm", _fn_outnorm, [full(o_gdn), (qkvg, d, 3)], [og], [(d, BF16)])
    mix0 = _matmul(on, w_out, "nn", F32, "mm_gdn_out", tk=1024)
    x1, h1 = _rowwise("res_a0", _fn_res_norm, [full(x), full(mix0)], [row(mix_post, 0), row(mlp_pre, 0)], [(d, F32), (d, BF16)])
    (a0,) = _matmul(h1, w_up[0], "nn", (BF16,), "mm_up0", tk=1024, epilogue=_relu2_of)
    d0 = _matmul(a0, w_down[0], "nn", F32, "mm_down0")
    x2, hkv, hq = _rowwise("res_b0", _fn_res_norm2, [full(x1), full(d0)], [row(mlp_post, 0), kv_gain, row(mix_pre, 1)], [(d, F32), (d, BF16), (d, BF16)])
    w_k, w_v = w_kv[:, :d], w_kv[:, d:]
    kp = _matmul(hkv, w_k, "nn", BF16, "mm_k", tk=1024)
    vp = _matmul(hkv, w_v, "nn", BF16, "mm_v", tk=1024)
    qp = _matmul(hq, w_q, "nn", BF16, "mm_q", tk=1024)
    o_sb, rt = _sb_fwd(qp, kp, vp)
    mix1 = _matmul(o_sb, w_o, "nn", F32, "mm_sb_out", tk=1024)
    x3, h3 = _rowwise("res_a1", _fn_res_norm, [full(x2), full(mix1)], [row(mix_post, 1), row(mlp_pre, 1)], [(d, F32), (d, BF16)])
    (a1,) = _matmul(h3, w_up[1], "nn", (BF16,), "mm_up1", tk=1024, epilogue=_relu2_of)
    d1 = _matmul(a1, w_down[1], "nn", F32, "mm_down1")

    loss, dx3, dd1, g_mlp_post1 = _loss_call(x3, d1, tgt, row(mlp_post, 1))
    (du1,) = _matmul(dd1, w_down[1], "nt", (BF16,), "mm_down1_dx", tk=1024, epilogue=_relu2_cotangent, extras=[a1])
    g_down1 = _matmul(a1, dd1, "tn", F32, "mm_down1_dw")
    dh3 = _matmul(du1, w_up[1], "nt", F32, "mm_up1_dx")
    g_up1 = _matmul(h3, du1, "tn", F32, "mm_up1_dw")
    (dx2, dmix1), (g_mix_post1, g_mlp_pre1) = _rowwise_bwd(
        "res_a1_bwd", _fn_res_norm, [full(x2), full(mix1)], [row(mix_post, 1), row(mlp_pre, 1)], [dx3, dh3], [F32, BF16])
    do_sb = _matmul(dmix1, w_o, "nt", BF16, "mm_sb_out_dx")
    g_o = _matmul(o_sb, dmix1, "tn", F32, "mm_sb_out_dw")
    dqp, dkp, dvp = _sb_bwd(qp, kp, vp, rt, do_sb)
    dhq = _matmul(dqp, w_q, "nt", F32, "mm_q_dx")
    g_q = _matmul(hq, dqp, "tn", F32, "mm_q_dw")
    dhkv = _matmul(dvp, w_v, "nt", F32, "mm_v_dx", add=_matmul(dkp, w_k, "nt", F32, "mm_k_dx"))
    g_kv = jnp.concatenate([_matmul(hkv, dkp, "tn", F32, "mm_k_dw"), _matmul(hkv, dvp, "tn", F32, "mm_v_dw")], axis=1)
    (dx1, dd0), (g_mlp_post0, g_kv_gain, g_mix_pre1) = _rowwise_bwd(
        "res_b0_bwd", _fn_res_norm2, [full(x1), full(d0)], [row(mlp_post, 0), kv_gain, row(mix_pre, 1)], [dx2, dhkv, dhq], [F32, BF16])
    (du0,) = _matmul(dd0, w_down[0], "nt", (BF16,), "mm_down0_dx", tk=1024, epilogue=_relu2_cotangent, extras=[a0])
    g_down0 = _matmul(a0, dd0, "tn", F32, "mm_down0_dw")
    dh1 = _matmul(du0, w_up[0], "nt", F32, "mm_up0_dx")
    g_up0 = _matmul(h1, du0, "tn", F32, "mm_up0_dw")
    (dx0, dmix0), (g_mix_post0, g_mlp_pre0) = _rowwise_bwd(
        "res_a0_bwd", _fn_res_norm, [full(x), full(mix0)], [row(mix_post, 0), row(mlp_pre, 0)], [dx1, dh1], [F32, BF16])
    don = _matmul(dmix0, w_out, "nt", F32, "mm_gdn_out_dx")
    g_out = _matmul(on, dmix0, "tn", F32, "mm_gdn_out_dw")
    (do_gdn, dgate), (g_og,) = _rowwise_bwd("out_norm_bwd", _fn_outnorm, [full(o_gdn), (qkvg, d, 3)], [og], [don], [F32, F32])
    dpw, dpu, dpqg, dpkg, dpattn, dpgl = _gdn_scan_bwd(pw, pu, pqg, pkg, pattn, pgl, states, do_gdn)
    dgq, dgk, dgv, dbeta, dgc, dgt, dgcr = _gdn_prep_bwd(gq, gk, gv, beta, gc, gt, gcr, ptm, dpw, dpu, dpqg, dpkg, dpattn, dpgl)
    dgcr_lanes = jnp.pad(jnp.swapaxes(dgcr.reshape(HEADS, t), 0, 1), ((0, 0), (HEADS, LANES - 2 * HEADS)))
    gate_cots = [dbeta, dgc + dgcr_lanes, dgt]
    (dba,), (g_al, g_dtb) = _rowwise_bwd("gates_bwd", _fn_gates, [full(ba)], [al, dtb], gate_cots, [BF16])
    (dconv,), _ = _rowwise_bwd("post_conv_bwd", _fn_post, conv_qkv, [], [dgq, dgk, dgv], [F32] * 3, joined=True)
    dqkvg, g_conv = _conv_bwd(dconv, dgate, qkvg, conv_w)
    dh0b = _matmul(dba, w_ba, "nt", F32, "mm_gdn_ba_dx", tk=LANES)
    dh0 = _matmul(dqkvg, w_qkvg, "nt", F32, "mm_gdn_in_dx", add=dh0b)
    g_qkvg = _matmul(h0, dqkvg, "tn", F32, "mm_gdn_in_dw")
    g_ba = _matmul(h0, dba, "tn", F32, "mm_gdn_ba_dw")
    (grad_x,), (g_mix_pre0,) = _rowwise_bwd("norm_in_bwd", lambda xx, gg: (_rms(xx, gg), xx), [full(x)], [row(mix_pre, 0)], [dh0, dx0], [F32])

    grads = dict(
        mix_pre_gain=jnp.concatenate([g_mix_pre0, g_mix_pre1], axis=0),
        mix_post_gain=jnp.concatenate([g_mix_post0, g_mix_post1], axis=0),
        mlp_pre_gain=jnp.concatenate([g_mlp_pre0, g_mlp_pre1], axis=0),
        mlp_post_gain=jnp.concatenate([g_mlp_post0, g_mlp_post1], axis=0),
        mlp_w_up=jnp.stack([g_up0, g_up1]),
        mlp_w_down=jnp.stack([g_down0, g_down1]),
        gdn_w_in=jnp.concatenate([g_qkvg, g_ba[:, :2 * HEADS]], axis=1)[None],
        gdn_conv_w=g_conv[None, :CONV_K],
        gdn_a_log=g_al[:, HEADS:2 * HEADS],
        gdn_dt_bias=g_dtb[:, HEADS:2 * HEADS],
        gdn_out_gain=jnp.sum(g_og.reshape(HEADS, HEAD_DIM), axis=0, keepdims=True),
        gdn_w_out=g_out[None],
        kv_gain=g_kv_gain[0],
        w_kv=g_kv,
        sb_w_q=g_q[None],
        sb_w_o=g_o[None],
    )
    return loss, grad_x, grads


N_DEV = 8
N_CHIPS = 4
PACK_ROW_TILE = 128

_HBM = pl.BlockSpec(memory_space=pltpu.HBM)


def _place():
    return lax.axis_index("x"), lax.axis_index("y"), lax.axis_index("c")


def _other_chips(x, y):
    return [(1 - x, y), (x, 1 - y), (1 - x, 1 - y)]


def _remote(src, dst, send_sem, recv_sem, to):
    return pltpu.make_async_remote_copy(src_ref=src, dst_ref=dst, send_sem=send_sem, recv_sem=recv_sem, device_id=to, device_id_type=MESH)


def _gather8(v, name):
    rows, cols = v.shape

    def body(v_ref, out_ref, sum_ref, send_sems, recv_sems, local_sem):
        x, y, c = _place()
        me, sibling = (x, y, c), (x, y, 1 - c)
        chips = _other_chips(x, y)

        def blk(px, py, pc):
            return out_ref.at[pl.ds((4 * px + 2 * py + pc) * rows, rows), :]

        def copy(k, block, to, src=None):
            return _remote(blk(*block) if src is None else src, blk(*block), send_sems.at[k], recv_sems.at[k], to)

        mine = pltpu.make_async_copy(v_ref, blk(*me), local_sem)
        mine.start()
        first = [copy(0, me, sibling, src=v_ref)] + [copy(1 + j, me, (*chip, c), src=v_ref) for j, chip in enumerate(chips)]
        for cp in first:
            cp.start()
        passed = [copy(4 + j, (*chip, c), sibling) for j, chip in enumerate(chips)]
        for j, chip in enumerate(chips):
            copy(1 + j, (*chip, c), me).wait_recv()
            passed[j].start()
        copy(0, sibling, me).wait_recv()
        for j, chip in enumerate(chips):
            copy(4 + j, (*chip, 1 - c), me).wait_recv()
        for cp in first + passed:
            cp.wait_send()
        mine.wait()
        acc = out_ref[pl.ds(0, rows), :]
        for dev in range(1, N_DEV):
            acc = acc + out_ref[pl.ds(dev * rows, rows), :]
        sum_ref[...] = acc

    vm = pl.BlockSpec(memory_space=pltpu.VMEM)
    return pl.pallas_call(
        body,
        name=name,
        out_shape=[jax.ShapeDtypeStruct((N_DEV * rows, cols), v.dtype), jax.ShapeDtypeStruct((rows, cols), v.dtype)],
        in_specs=[vm],
        out_specs=[vm, vm],
        scratch_shapes=[pltpu.SemaphoreType.DMA((7,)), pltpu.SemaphoreType.DMA((7,)), pltpu.SemaphoreType.DMA],
    )(v)


def _hbm_call(body, name, arrs, out_shapes, sem_counts):
    n = len(arrs)

    def wrapped(*refs):
        body(refs[:n], refs[n:2 * n], *refs[2 * n:])

    return pl.pallas_call(
        wrapped,
        name=name,
        out_shape=[jax.ShapeDtypeStruct(s, a.dtype) for s, a in zip(out_shapes, arrs)],
        in_specs=[_HBM] * n,
        out_specs=[_HBM] * n,
        scratch_shapes=[pltpu.SemaphoreType.DMA((k,)) for k in sem_counts],
    )(*arrs)


def _gather_weights(arrs):
    n = len(arrs)

    def body(w_refs, out_refs, send_sems, recv_sems, fsend_sems, frecv_sems):
        x, y, c = _place()
        chips = _other_chips(x, y)
        s_me = 2 * x + y
        pairs = list(zip(w_refs, out_refs))
        first = [_remote(w.at[c], o.at[s_me, c], send_sems.at[3 * a + j], recv_sems.at[3 * a + j], (px, py, c))
                 for a, (w, o) in enumerate(pairs) for j, (px, py) in enumerate(chips)]
        for cp in first:
            cp.start()
        passed = []
        for a, (w, o) in enumerate(pairs):
            for j, (px, py) in enumerate(chips):
                half = o.at[2 * px + py, c]
                _remote(half, half, send_sems.at[3 * a + j], recv_sems.at[3 * a + j], (px, py, c)).wait_recv()
                fwd = _remote(half, half, fsend_sems.at[3 * a + j], frecv_sems.at[3 * a + j], (x, y, 1 - c))
                fwd.start()
                passed.append(fwd)
        for a, (w, o) in enumerate(pairs):
            for j, (px, py) in enumerate(chips):
                half = o.at[2 * px + py, 1 - c]
                _remote(half, half, fsend_sems.at[3 * a + j], frecv_sems.at[3 * a + j], (x, y, 1 - c)).wait_recv()
        for cp in first + passed:
            cp.wait_send()

    return _hbm_call(body, "gather_weights", arrs, [(N_CHIPS,) + a.shape for a in arrs], [3 * n] * 4)


def _swap_halves(arrs):
    n = len(arrs)

    def body(g_refs, a_refs, send_sems, recv_sems):
        x, y, c = _place()
        cps = [_remote(g.at[1 - c], a, send_sems.at[i], recv_sems.at[i], (x, y, 1 - c)) for i, (g, a) in enumerate(zip(g_refs, a_refs))]
        for cp in cps:
            cp.start()
        for cp in cps:
            cp.wait()

    return _hbm_call(body, "grads_to_sibling", arrs, [a.shape[1:] for a in arrs], [n, n])


def _scatter_to_chips(arrs):
    n = len(arrs)

    def body(p_refs, b_refs, send_sems, recv_sems):
        x, y, c = _place()
        cps = [_remote(p.at[2 * px + py], b.at[j], send_sems.at[3 * i + j], recv_sems.at[3 * i + j], (px, py, c))
               for i, (p, b) in enumerate(zip(p_refs, b_refs)) for j, (px, py) in enumerate(_other_chips(x, y))]
        for cp in cps:
            cp.start()
        for cp in cps:
            cp.wait()

    return _hbm_call(body, "grads_to_chips", arrs, [(3,) + a.shape[1:] for a in arrs], [3 * n, 3 * n])


def _share_halves(arrs):
    n = len(arrs)

    def body(q_refs, out_refs, send_sems, recv_sems):
        x, y, c = _place()
        cps = [_remote(q, o, send_sems.at[i], recv_sems.at[i], (x, y, 1 - c)) for i, (q, o) in enumerate(zip(q_refs, out_refs))]
        for cp in cps:
            cp.start()
        for cp in cps:
            cp.wait()

    return _hbm_call(body, "grads_share", arrs, [a.shape for a in arrs], [n, n])


_GROUPS = (
    (("mlp_w_up", (2, 1024, 1024), "cols"), ("mlp_w_down", (2, 1024, 1024), "rows"), ("gdn_w_out", (1, 256, 1024), "rows"),
     ("sb_w_q", (1, 256, 1024), "rows"), ("sb_w_o", (1, 256, 1024), "rows")),
    (("w_kv", (1024, 512), "cols"),),
    (("gdn_w_in", (1, 1024, 1028), "cols"),),
)


def _numel(shape):
    n = 1
    for s in shape:
        n *= s
    return n


def _half_rows(shape):
    return _numel(shape[:-1]) // 2


def _pack_shards(shards, dtype):
    return tuple(jnp.concatenate([shards[n].astype(dtype).reshape(2, _half_rows(shape), shape[-1]) for n, shape, _ in grp], axis=1) for grp in _GROUPS)


def _unpack_shards(bufs):
    out = {}
    for grp, buf in zip(_GROUPS, bufs):
        off = 0
        for n, shape, _ in grp:
            out[n] = buf[:, off:off + _half_rows(shape)].reshape(shape)
            off += _half_rows(shape)
    return out


def _join(stacked, how):
    nd = stacked.ndim - 1
    ax = nd - 1 if how == "cols" else nd - 2
    moved = jnp.moveaxis(stacked, 0, ax)
    shape = list(stacked.shape[1:])
    shape[ax] *= N_CHIPS
    return moved.reshape(shape)


def _split(full, shard_shape, how):
    nd = len(shard_shape)
    ax = nd - 1 if how == "cols" else nd - 2
    shape = list(shard_shape)
    shape.insert(ax, N_CHIPS)
    return jnp.moveaxis(full.reshape(shape), ax, 0)


def _unpack_full(gathered):
    out = {}
    for grp, buf in zip(_GROUPS, gathered):
        off = 0
        for n, shape, how in grp:
            out[n] = _join(buf[:, :, off:off + _half_rows(shape)].reshape((N_CHIPS,) + shape), how)
            off += _half_rows(shape)
    return out


def _pack_full(full):
    bufs = []
    for grp in _GROUPS:
        parts = [_split(full[n], shape, how).reshape(N_CHIPS, 2, _half_rows(shape), shape[-1]) for n, shape, how in grp]
        buf = jnp.swapaxes(jnp.concatenate(parts, axis=2), 0, 1)
        bufs.append(buf.reshape(2, -1, buf.shape[-1]))
    return tuple(bufs)


_SMALL = (
    ("mix_pre_gain", (2, 1024)),
    ("mix_post_gain", (2, 1024)),
    ("mlp_pre_gain", (2, 1024)),
    ("mlp_post_gain", (2, 1024)),
    ("kv_gain", (1024,)),
    ("gdn_out_gain", (1, 128)),
    ("gdn_a_log", (1, 8)),
    ("gdn_dt_bias", (1, 8)),
    ("gdn_conv_w", (1, 4, 3072)),
    ("loss", ()),
)


def _rows_of(shape):
    return -(-_numel(shape) // LANES)


_SMALL_ROWS = -(-sum(_rows_of(s) for _, s in _SMALL) // 8) * 8


def _pack_small(vals):
    parts = []
    for n, shape in _SMALL:
        flat = vals[n].reshape(-1)
        parts.append(jnp.pad(flat, (0, _rows_of(shape) * LANES - flat.shape[0])))
    flat = jnp.concatenate(parts)
    return jnp.pad(flat, (0, _SMALL_ROWS * LANES - flat.shape[0])).reshape(_SMALL_ROWS, LANES)


def _unpack_small(packed):
    flat = packed.reshape(-1)
    out, off = {}, 0
    for n, shape in _SMALL:
        out[n] = flat[off:off + _numel(shape)].reshape(shape)
        off += _rows_of(shape) * LANES
    return out


_WEIGHTS = ("mix_pre_gain", "mix_post_gain", "mlp_pre_gain", "mlp_post_gain", "mlp_w_up", "mlp_w_down", "gdn_w_in", "gdn_conv_w",
            "gdn_a_log", "gdn_dt_bias", "gdn_out_gain", "gdn_w_out", "kv_gain", "w_kv", "sb_w_q", "sb_w_o")


def _as2d(a):
    return a.reshape(1, -1) if a.ndim <= 1 else a.reshape(-1, a.shape[-1])


def kernel(x, mix_pre_gain, mix_post_gain, mlp_pre_gain, mlp_post_gain, mlp_w_up, mlp_w_down, gdn_w_in, gdn_conv_w, gdn_a_log, gdn_dt_bias, gdn_out_gain, gdn_w_out, kv_gain, w_kv, sb_w_q, sb_w_o, loss_target, m_mix_pre_gain, m_mix_post_gain, m_mlp_pre_gain, m_mlp_post_gain, m_mlp_w_up, m_mlp_w_down, m_gdn_w_in, m_gdn_conv_w, m_gdn_a_log, m_gdn_dt_bias, m_gdn_out_gain, m_gdn_w_out, m_kv_gain, m_w_kv, m_sb_w_q, m_sb_w_o, v_mix_pre_gain, v_mix_post_gain, v_mlp_pre_gain, v_mlp_post_gain, v_mlp_w_up, v_mlp_w_down, v_gdn_w_in, v_gdn_conv_w, v_gdn_a_log, v_gdn_dt_bias, v_gdn_out_gain, v_gdn_w_out, v_kv_gain, v_w_kv, v_sb_w_q, v_sb_w_o):
    w = dict(mix_pre_gain=mix_pre_gain, mix_post_gain=mix_post_gain, mlp_pre_gain=mlp_pre_gain, mlp_post_gain=mlp_post_gain, mlp_w_up=mlp_w_up, mlp_w_down=mlp_w_down, gdn_w_in=gdn_w_in, gdn_conv_w=gdn_conv_w, gdn_a_log=gdn_a_log, gdn_dt_bias=gdn_dt_bias, gdn_out_gain=gdn_out_gain, gdn_w_out=gdn_w_out, kv_gain=kv_gain, w_kv=w_kv, sb_w_q=sb_w_q, sb_w_o=sb_w_o)
    m = dict(mix_pre_gain=m_mix_pre_gain, mix_post_gain=m_mix_post_gain, mlp_pre_gain=m_mlp_pre_gain, mlp_post_gain=m_mlp_post_gain, mlp_w_up=m_mlp_w_up, mlp_w_down=m_mlp_w_down, gdn_w_in=m_gdn_w_in, gdn_conv_w=m_gdn_conv_w, gdn_a_log=m_gdn_a_log, gdn_dt_bias=m_gdn_dt_bias, gdn_out_gain=m_gdn_out_gain, gdn_w_out=m_gdn_w_out, kv_gain=m_kv_gain, w_kv=m_w_kv, sb_w_q=m_sb_w_q, sb_w_o=m_sb_w_o)
    v = dict(mix_pre_gain=v_mix_pre_gain, mix_post_gain=v_mix_post_gain, mlp_pre_gain=v_mlp_pre_gain, mlp_post_gain=v_mlp_post_gain, mlp_w_up=v_mlp_w_up, mlp_w_down=v_mlp_w_down, gdn_w_in=v_gdn_w_in, gdn_conv_w=v_gdn_conv_w, gdn_a_log=v_gdn_a_log, gdn_dt_bias=v_gdn_dt_bias, gdn_out_gain=v_gdn_out_gain, gdn_w_out=v_gdn_w_out, kv_gain=v_kv_gain, w_kv=v_w_kv, sb_w_q=v_sb_w_q, sb_w_o=v_sb_w_o)
    cx, cy, cc = _place()
    chip = 2 * cx + cy
    conv_cols = gdn_conv_w.shape[-1]

    own = _pack_shards(w, BF16)
    full = _unpack_full([lax.dynamic_update_index_in_dim(others, mine, chip, 0) for others, mine in zip(_gather_weights(own), own)])
    conv_rows = jnp.pad(gdn_conv_w[0], ((0, 8 - CONV_K), (0, 0))).reshape(-1, LANES)
    conv_all, _ = _gather8(conv_rows, "gather_conv_w")
    conv_all = conv_all.reshape(N_CHIPS, 2, 8, conv_cols)[:, 0, :CONV_K]
    conv_full = jnp.swapaxes(conv_all, 0, 1).reshape(CONV_K, N_CHIPS * conv_cols)

    w_in = full["gdn_w_in"][0]
    wts = (w_in[:, :4 * HEADS * HEAD_DIM], jnp.pad(w_in[:, 4 * HEADS * HEAD_DIM:], ((0, 0), (0, LANES - 2 * HEADS))), full["gdn_w_out"][0], full["w_kv"],
           full["sb_w_q"][0], full["sb_w_o"][0], full["mlp_w_up"], full["mlp_w_down"])
    gains = (mix_pre_gain, mix_post_gain, mlp_pre_gain, mlp_post_gain, kv_gain[None])
    small = (conv_full, gdn_a_log, gdn_dt_bias, gdn_out_gain)
    loss_rows, grad_x, g_full = _local_step(x[0], loss_target[0], gains, wts, small)

    bufs = _pack_full(g_full)
    from_sibling = _swap_halves(bufs)
    partial, partial_bf16 = [], []
    for i, (buf, other) in enumerate(zip(bufs, from_sibling)):
        cols = buf.shape[-1]
        own_half = lax.dynamic_index_in_dim(buf, cc, 0, keepdims=False)
        p, pb = _rowwise(f"grads_add_sibling_{i}", lambda a, b: (a + b, a + b), [(own_half, cols, 0), (other, cols, 0)], [], [(cols, F32), (cols, BF16)], tm=PACK_ROW_TILE)
        partial.append(p.reshape(N_CHIPS, -1, cols))
        partial_bf16.append(pb.reshape(N_CHIPS, -1, cols))
    from_chips = _scatter_to_chips(tuple(partial_bf16))
    reduced = []
    for i, (p, others) in enumerate(zip(partial, from_chips)):
        cols = p.shape[-1]
        mine = lax.dynamic_index_in_dim(p, chip, 0, keepdims=False)
        (r,) = _rowwise(f"grads_add_chips_{i}", lambda a, b, c, d: (((a + b) + c) + d,),
                        [(mine, cols, 0), (others[0], cols, 0), (others[1], cols, 0), (others[2], cols, 0)], [], [(cols, F32)], tm=PACK_ROW_TILE)
        reduced.append(r)
    g_shard = _unpack_shards([jnp.where(cc == 0, jnp.stack([r, o]), jnp.stack([o, r])) for r, o in zip(reduced, _share_halves(tuple(reduced)))])

    g_small_local = {n: g_full[n] for n, _ in _SMALL if n != "loss"}
    g_small_local["loss"] = loss_rows[0, 0]
    _, small_sum = _gather8(_pack_small(g_small_local), "allreduce_small")
    g_small = _unpack_small(small_sum)
    loss = g_small.pop("loss")
    g_small["gdn_conv_w"] = lax.dynamic_slice_in_dim(g_small["gdn_conv_w"], chip * conv_cols, conv_cols, axis=2)

    grads = {**g_shard, **g_small}
    deltas, new_m, new_v = {}, {}, {}
    for n in _WEIGHTS:
        d2, m2, v2 = _adamw(_as2d(w[n]), _as2d(grads[n]), _as2d(m[n]), _as2d(v[n]), "adamw_" + n)
        deltas[n], new_m[n], new_v[n] = d2.reshape(w[n].shape), m2.reshape(w[n].shape), v2.reshape(w[n].shape)
    return (loss, grad_x[None], *[grads[n].reshape(w[n].shape) for n in _WEIGHTS], *[deltas[n] for n in _WEIGHTS],
            *[new_m[n] for n in _WEIGHTS], *[new_v[n] for n in _WEIGHTS])
```

```python
import functools

import jax
import jax.numpy as jnp
from jax import lax
from jax.experimental import pallas as pl
from jax.experimental.pallas import tpu as pltpu

F32, BF16 = jnp.float32, jnp.bfloat16
HI = lax.Precision.HIGHEST
MESH = pl.DeviceIdType.MESH

EPS = 1e-6
D_MODEL = 1024
HEADS = 8
HEAD_DIM = 128
CHUNK = 64
CHUNK_SHIFT = CHUNK.bit_length() - 1
CONV_K = 4
D_FF = 4096
QKV = 3 * HEADS * HEAD_DIM

ADAM_LR, ADAM_B1, ADAM_B2, ADAM_EPS, ADAM_WD, ADAM_STEP = 0.001, 0.9, 0.999, 1e-08, 0.01, 10

VMEM_LIMIT_BYTES = 48 * 1024 * 1024
LANES = 128

NN = ((1,), (0,))
NT = ((1,), (1,))
TN = ((0,), (0,))


def _dot(a, b, dims=NN, precision=None):
    return lax.dot_general(a, b, (dims, ((), ())), precision=precision, preferred_element_type=F32)


def _params(*sem):
    return pltpu.CompilerParams(dimension_semantics=sem, vmem_limit_bytes=VMEM_LIMIT_BYTES)


def _iota(shape, axis):
    return lax.broadcasted_iota(jnp.int32, shape, axis)


def _matmul(a, b, mode, out_dtype, name, tm=1024, tn=1024, tk=1024, add=None, epilogue=None, extras=()):
    if mode == "nn":
        (m, k), (k2, n) = a.shape, b.shape
    elif mode == "nt":
        (m, k), (n, k2) = a.shape, b.shape
    else:
        (k, m), (k2, n) = a.shape, b.shape
    assert k == k2, (a.shape, b.shape, mode)
    tm, tn, tk = min(tm, m), min(tn, n), min(tk, k)
    assert m % tm == 0 and n % tn == 0 and k % tk == 0, (a.shape, b.shape, mode)
    nk = k // tk
    dims = {"nn": NN, "nt": NT, "tn": TN}[mode]
    tiles = ([add] if add is not None else []) + list(extras)
    out_dtypes = out_dtype if epilogue is not None else (out_dtype,)
    n_in = 2 + len(tiles)

    def body(*refs):
        a_ref, b_ref = refs[:2]
        extra_refs = refs[n_in - len(extras):n_in]
        o_refs, acc_ref = refs[n_in:-1], refs[-1]
        kk = pl.program_id(2)

        @pl.when(kk == 0)
        def _():
            acc_ref[...] = refs[2][...].astype(F32) if add is not None else jnp.zeros_like(acc_ref)

        acc_ref[...] += _dot(a_ref[...].astype(BF16), b_ref[...].astype(BF16), dims)

        @pl.when(kk == nk - 1)
        def _():
            res = (acc_ref[...],) if epilogue is None else epilogue(acc_ref[...], *[r[...] for r in extra_refs])
            for o_ref, r in zip(o_refs, res):
                o_ref[...] = r.astype(o_ref.dtype)

    a_spec = pl.BlockSpec((tk, tm), lambda i, j, kk: (kk, i)) if mode == "tn" else pl.BlockSpec((tm, tk), lambda i, j, kk: (i, kk))
    b_spec = pl.BlockSpec((tn, tk), lambda i, j, kk: (j, kk)) if mode == "nt" else pl.BlockSpec((tk, tn), lambda i, j, kk: (kk, j))
    o_spec = pl.BlockSpec((tm, tn), lambda i, j, kk: (i, j))
    res = pl.pallas_call(
        body,
        name=name,
        grid=(m // tm, n // tn, nk),
        in_specs=[a_spec, b_spec] + [o_spec] * len(tiles),
        out_specs=[o_spec] * len(out_dtypes),
        out_shape=[jax.ShapeDtypeStruct((m, n), dt) for dt in out_dtypes],
        scratch_shapes=[pltpu.VMEM((tm, tn), F32)],
        compiler_params=_params("parallel", "parallel", "arbitrary"),
    )(a, b, *tiles)
    return res if epilogue is not None else res[0]


def _row_specs(rows, tm):
    return [pl.BlockSpec((tm, w), lambda i, cb=cb: (i, cb)) for _, w, cb in rows]


def _full_spec(p):
    return pl.BlockSpec(p.shape, lambda i: (0,) * p.ndim)


def _rowwise(name, fn, rows, params, outs, tm=256):
    t = rows[0][0].shape[0]
    tm = min(tm, t)
    nr, npar = len(rows), len(params)

    def body(*refs):
        ins = [r[...].astype(F32) for r in refs[:nr]]
        ps = [p[...] for p in refs[nr:nr + npar]]
        res = fn(*ins, *ps)
        for o_ref, r in zip(refs[nr + npar:], res):
            o_ref[...] = r.astype(o_ref.dtype)

    return pl.pallas_call(
        body,
        name=name,
        grid=(t // tm,),
        in_specs=_row_specs(rows, tm) + [_full_spec(p) for p in params],
        out_specs=[pl.BlockSpec((tm, w), lambda i: (i, 0)) for w, _ in outs],
        out_shape=[jax.ShapeDtypeStruct((t, w), dt) for w, dt in outs],
        compiler_params=_params("parallel"),
    )(*[r[0] for r in rows], *params)


def _rowwise_bwd(name, fn, rows, params, cots, grad_dtypes, tm=256, joined=False):
    t = rows[0][0].shape[0]
    tm = min(tm, t)
    nr, npar, nc = len(rows), len(params), len(cots)
    want = [j for j, dt in enumerate(grad_dtypes) if dt is not None]
    widths = [rows[j][1] for j in want]
    n_row_outs = 1 if joined else len(want)

    def body(*refs):
        i = pl.program_id(0)
        ins = [r[...].astype(F32) for r in refs[:nr]]
        ps = [p[...] for p in refs[nr:nr + npar]]
        cs = tuple(c[...].astype(F32) for c in refs[nr + npar:nr + npar + nc])
        _, vjp = jax.vjp(fn, *ins, *ps)
        gs = vjp(cs)
        outs = refs[nr + npar + nc:]
        if joined:
            off = 0
            for j, w in zip(want, widths):
                outs[0][:, off:off + w] = gs[j].astype(outs[0].dtype)
                off += w
        else:
            for o_ref, j in zip(outs, want):
                o_ref[...] = gs[j].astype(o_ref.dtype)
        pg_refs = outs[n_row_outs:]

        @pl.when(i == 0)
        def _():
            for pg in pg_refs:
                pg[...] = jnp.zeros_like(pg)

        for pg, g in zip(pg_refs, gs[nr:]):
            pg[...] += g

    if joined:
        row_specs = [pl.BlockSpec((tm, sum(widths)), lambda i: (i, 0))]
        row_shapes = [jax.ShapeDtypeStruct((t, sum(widths)), grad_dtypes[want[0]])]
    else:
        row_specs = [pl.BlockSpec((tm, w), lambda i: (i, 0)) for w in widths]
        row_shapes = [jax.ShapeDtypeStruct((t, w), grad_dtypes[j]) for j, w in zip(want, widths)]
    res = pl.pallas_call(
        body,
        name=name,
        grid=(t // tm,),
        in_specs=_row_specs(rows, tm) + [_full_spec(p) for p in params] + [pl.BlockSpec((tm, c.shape[1]), lambda i: (i, 0)) for c in cots],
        out_specs=row_specs + [_full_spec(p) for p in params],
        out_shape=row_shapes + [jax.ShapeDtypeStruct(p.shape, F32) for p in params],
        compiler_params=_params("arbitrary"),
    )(*[r[0] for r in rows], *params, *cots)
    return res[:n_row_outs], res[n_row_outs:]


def _rms(x, g):
    return x * lax.rsqrt(jnp.mean(x * x, axis=-1, keepdims=True) + EPS) * g


def _sigmoid(x):
    return 1.0 / (1.0 + jnp.exp(-x))


def _softplus(x):
    return jnp.maximum(x, 0.0) + jnp.log1p(jnp.exp(-jnp.abs(x)))


def _two_pass(x, m):
    hi = x.astype(BF16)
    lo = (x - hi.astype(F32)).astype(BF16)
    return _dot(hi, m) + _dot(lo, m)


def _head_sum_impl(x):
    w = HEADS * HEAD_DIM
    fold = jnp.where((_iota((w, LANES), 0) >> 7) == _iota((w, LANES), 1), 1.0, 0.0).astype(BF16)
    spread = jnp.where(_iota((LANES, w), 0) == (_iota((LANES, w), 1) >> 7), 1.0, 0.0).astype(BF16)
    return _two_pass(_two_pass(x, fold), spread)


@jax.custom_vjp
def _head_sum(x):
    return _head_sum_impl(x)


_head_sum.defvjp(lambda x: (_head_sum_impl(x), None), lambda _, g: (_head_sum_impl(g),))


def _fn_norm(x, g):
    return (_rms(x, g),)


def _fn_gates(ba, al, dt):
    col = _iota((1, LANES), 1)
    g = jnp.where((col >= HEADS) & (col < 2 * HEADS), -jnp.exp(al) * _softplus(ba + dt), 0.0)
    rows = ba.shape[0]
    r, c = _iota((rows, rows), 0), _iota((rows, rows), 1)
    same = (r >> CHUNK_SHIFT) == (c >> CHUNK_SHIFT)
    gc = _dot(jnp.where(same & (r >= c), 1.0, 0.0), g, precision=HI)
    gtot = _dot(jnp.where(same, 1.0, 0.0), g, precision=HI)
    return _sigmoid(ba), gc, gtot


def _fn_post_q(c):
    s = c * _sigmoid(c)
    return (s * lax.rsqrt(_head_sum(s * s) + EPS) * (HEAD_DIM ** -0.5),)


def _fn_post_k(c):
    s = c * _sigmoid(c)
    return (s * lax.rsqrt(_head_sum(s * s) + EPS),)


def _fn_post_v(c):
    return (c * _sigmoid(c),)


def _fn_post(cq, ck, cv):
    return _fn_post_q(cq) + _fn_post_k(ck) + _fn_post_v(cv)


def _fn_outnorm(o, gate, og):
    y = o * lax.rsqrt(_head_sum(o * o) * (1.0 / HEAD_DIM) + EPS) * og
    return (y * (gate * _sigmoid(gate)),)


def _fn_res_norm(x, m, gp, gn):
    x1 = x + _rms(m, gp)
    return x1, _rms(x1, gn)


def _fn_res_norm2(x, m, gp, ga, gb):
    x1 = x + _rms(m, gp)
    return x1, _rms(x1, ga), _rms(x1, gb)


def _relu2_of(u):
    r = jnp.maximum(u, 0.0)
    return (r * r,)


def _relu2_cotangent(da, a):
    return (da * (2.0 * jnp.sqrt(a.astype(F32))),)


def _loss_call(x3, d1, tgt, g, tm=256):
    t, d = x3.shape
    tm = min(tm, t)

    def body(x_ref, d_ref, t_ref, g_ref, loss_ref, dx_ref, dd_ref, dg_ref):
        i = pl.program_id(0)
        y, vjp = jax.vjp(lambda x, dd, gg: x + _rms(dd, gg), x_ref[...], d_ref[...], g_ref[...])
        err = y - t_ref[...]
        lrow = 0.5 * jnp.mean(err * err, axis=-1, keepdims=True)
        dx, dd, dg = vjp(err * (1.0 / d))
        dx_ref[...] = dx
        dd_ref[...] = dd.astype(dd_ref.dtype)

        @pl.when(i == 0)
        def _():
            loss_ref[...] = jnp.zeros_like(loss_ref)
            dg_ref[...] = jnp.zeros_like(dg_ref)

        loss_ref[...] += jnp.broadcast_to(jnp.sum(lrow, axis=0, keepdims=True), loss_ref.shape)
        dg_ref[...] += dg

    row = pl.BlockSpec((tm, d), lambda i: (i, 0))
    return pl.pallas_call(
        body,
        name="loss_head",
        grid=(t // tm,),
        in_specs=[row, row, row, _full_spec(g)],
        out_specs=[pl.BlockSpec((8, LANES), lambda i: (0, 0)), row, row, _full_spec(g)],
        out_shape=[jax.ShapeDtypeStruct((8, LANES), F32), jax.ShapeDtypeStruct((t, d), F32), jax.ShapeDtypeStruct((t, d), BF16), jax.ShapeDtypeStruct(g.shape, F32)],
        compiler_params=_params("arbitrary"),
    )(x3, d1, tgt, g)


HALO = 8


def _conv_fwd(qkvg, conv_w, tm=256):
    t = qkvg.shape[0]
    tm = min(tm, t)

    wide = QKV // 3

    def body(cur_ref, prev_ref, w_ref, o_ref, q_ref, k_ref, v_ref, buf):
        i = pl.program_id(0)
        buf[0:HALO, :] = jnp.where(i > 0, prev_ref[...], 0.0)
        buf[HALO:, :] = cur_ref[...]
        acc = buf[pl.ds(HALO - CONV_K + 1, tm), :] * w_ref[pl.ds(0, 1), :]
        for j in range(1, CONV_K):
            acc = acc + buf[pl.ds(HALO - CONV_K + 1 + j, tm), :] * w_ref[pl.ds(j, 1), :]
        o_ref[...] = acc
        (q_ref[...], k_ref[...], v_ref[...]) = _fn_post(acc[:, 0:wide], acc[:, wide:2 * wide], acc[:, 2 * wide:])

    part = pl.BlockSpec((tm, wide), lambda i: (i, 0))
    return pl.pallas_call(
        body,
        name="conv_fwd",
        grid=(t // tm,),
        in_specs=[
            pl.BlockSpec((tm, QKV), lambda i: (i, 0)),
            pl.BlockSpec((HALO, QKV), lambda i: (jnp.maximum(i * (tm // HALO) - 1, 0), 0)),
            pl.BlockSpec((CONV_K, QKV), lambda i: (0, 0)),
        ],
        out_specs=[pl.BlockSpec((tm, QKV), lambda i: (i, 0)), part, part, part],
        out_shape=[jax.ShapeDtypeStruct((t, QKV), F32)] + [jax.ShapeDtypeStruct((t, wide), F32)] * 3,
        scratch_shapes=[pltpu.VMEM((tm + HALO, QKV), F32)],
        compiler_params=_params("parallel"),
    )(qkvg, qkvg, conv_w)


def _conv_bwd(dc, dgate, qkvg, conv_w, tm=256):
    t = dc.shape[0]
    tm = min(tm, t)
    n = t // tm
    wg = dgate.shape[1]

    def body(dc_ref, dcn_ref, dgate_ref, x_ref, xp_ref, w_ref, dx_ref, dw_ref, bufd, bufx):
        i = pl.program_id(0)
        bufd[0:tm, :] = dc_ref[...]
        bufd[tm:, :] = jnp.where(i < n - 1, dcn_ref[...], 0.0)
        bufx[0:HALO, :] = jnp.where(i > 0, xp_ref[...], 0.0)
        bufx[HALO:, :] = x_ref[...]

        @pl.when(i == 0)
        def _():
            dw_ref[...] = jnp.zeros_like(dw_ref)

        dcv = dc_ref[...]
        acc = bufd[pl.ds(CONV_K - 1, tm), :] * w_ref[pl.ds(0, 1), :]
        for j in range(1, CONV_K):
            acc = acc + bufd[pl.ds(CONV_K - 1 - j, tm), :] * w_ref[pl.ds(j, 1), :]
        dx_ref[:, 0:QKV] = acc.astype(dx_ref.dtype)
        dx_ref[:, QKV:] = dgate_ref[...].astype(dx_ref.dtype)
        for j in range(CONV_K):
            dw_ref[pl.ds(j, 1), :] += jnp.sum(dcv * bufx[pl.ds(HALO - CONV_K + 1 + j, tm), :], axis=0, keepdims=True)

    return pl.pallas_call(
        body,
        name="conv_bwd",
        grid=(n,),
        in_specs=[
            pl.BlockSpec((tm, QKV), lambda i: (i, 0)),
            pl.BlockSpec((HALO, QKV), lambda i: (jnp.minimum((i + 1) * (tm // HALO), t // HALO - 1), 0)),
            pl.BlockSpec((tm, wg), lambda i: (i, 0)),
            pl.BlockSpec((tm, QKV), lambda i: (i, 0)),
            pl.BlockSpec((HALO, QKV), lambda i: (jnp.maximum(i * (tm // HALO) - 1, 0), 0)),
            pl.BlockSpec((CONV_K, QKV), lambda i: (0, 0)),
        ],
        out_specs=[pl.BlockSpec((tm, QKV + wg), lambda i: (i, 0)), pl.BlockSpec((HALO, QKV), lambda i: (0, 0))],
        out_shape=[jax.ShapeDtypeStruct((t, QKV + wg), BF16), jax.ShapeDtypeStruct((HALO, QKV), F32)],
        scratch_shapes=[pltpu.VMEM((tm + HALO, QKV), F32), pltpu.VMEM((tm + HALO, QKV), F32)],
        compiler_params=_params("arbitrary"),
    )(dc, dc, dgate, qkvg, qkvg, conv_w)


PREP_CHUNKS = 16
PREP_BWD_CHUNKS = 4


def _hi_lo(x):
    hi = x.astype(BF16)
    return hi, (x - hi.astype(F32)).astype(BF16)


def _mm3(a, b, dims=NN):
    (ah, al), (bh, bl) = _hi_lo(a), _hi_lo(b)
    return _dot(ah, bh, dims) + (_dot(ah, bl, dims) + _dot(al, bh, dims))


def _neumann(lowers):
    c = lowers[0].shape[0]
    eye = jnp.where(_iota((c, c), 0) == _iota((c, c), 1), 1.0, 0.0)
    ps = [-low for low in lowers]
    tmats = [eye + p for p in ps]
    for _ in range(CHUNK_SHIFT - 1):
        ps = [_mm3(p, p) for p in ps]
        tmats = [t + _mm3(t, p) for t, p in zip(tmats, ps)]
    return tuple(tmats)


def _inv_cotangents(tmats, dts):
    half = [_mm3(t, dt, TN) for t, dt in zip(tmats, dts)]
    return tuple(-_mm3(hf, t, NT) for hf, t in zip(half, tmats))


@jax.custom_vjp
def _tri_inv(lowers):
    return _neumann(lowers)


def _tri_inv_fwd(lowers):
    tmats = _neumann(lowers)
    return tmats, tmats


_tri_inv.defvjp(_tri_inv_fwd, lambda tmats, dts: (_inv_cotangents(tmats, dts),))


@jax.custom_vjp
def _tri_inv_known(lowers, tmats):
    return tmats


_tri_inv_known.defvjp(lambda lowers, tmats: (tmats, tmats),
                      lambda tmats, dts: (_inv_cotangents(tmats, dts), tuple(jnp.zeros_like(t) for t in tmats)))


def _prep_chunks(qs, ks, vs, bs, gcs, gts, gcrs, tmats=None):
    c = CHUNK
    r, col = _iota((c, c), 0), _iota((c, c), 1)
    incl, strict = r >= col, r > col
    decays = [jnp.where(incl, jnp.exp(jnp.where(incl, gc - gcr, 0.0)), 0.0) for gc, gcr in zip(gcs, gcrs)]
    kbs = [k * b for k, b in zip(ks, bs)]
    kbfs = [k.astype(BF16) for k in ks]
    lowers = tuple(jnp.where(strict, _dot(kb.astype(BF16), kbf, NT) * decay, 0.0) for kb, kbf, decay in zip(kbs, kbfs, decays))
    tmats = _tri_inv(lowers) if tmats is None else _tri_inv_known(lowers, tuple(tmats))
    outs = []
    for q, k, v, b, gc, gt, kb, kbf, decay, tmat in zip(qs, ks, vs, bs, gcs, gts, kbs, kbfs, decays, tmats):
        tb = tmat.astype(BF16)
        egc = jnp.exp(gc)
        w = _dot(tb, (kb * egc).astype(BF16))
        u = _dot(tb, (v * b).astype(BF16))
        attn = _dot(q.astype(BF16), kbf, NT) * decay
        gl = jnp.broadcast_to(jnp.exp(jnp.mean(gt.reshape(c // 8, 8, 1), axis=0)), (8, HEAD_DIM))
        outs.append((w, u, q * egc, k * jnp.exp(gt - gc), attn, gl))
    return tuple(outs), tmats


def _prep_specs(rows, gch):
    head = pl.BlockSpec((rows, HEAD_DIM), lambda n, h: (n, h))
    gates = pl.BlockSpec((rows, LANES), lambda n, h: (n, 0))
    gcrow = pl.BlockSpec((1, gch, 1, CHUNK), lambda n, h: (h, n, 0, 0))
    square = pl.BlockSpec((1, rows, CHUNK), lambda n, h: (h, n, 0))
    gl = pl.BlockSpec((1, gch * 8, HEAD_DIM), lambda n, h: (h, n, 0))
    return head, gates, gcrow, square, gl


def _pick_lane(ref, sl, lane):
    return jnp.sum(jnp.where(_iota((1, LANES), 1) == lane, ref[sl, :], 0.0), axis=1, keepdims=True)


def _prep_inputs(q_ref, k_ref, v_ref, b_ref, gc_ref, gt_ref, gcr_ref, sls, h):
    return ([q_ref[sl, :] for sl in sls], [k_ref[sl, :] for sl in sls], [v_ref[sl, :] for sl in sls],
            [_pick_lane(b_ref, sl, h) for sl in sls], [_pick_lane(gc_ref, sl, h + HEADS) for sl in sls],
            [_pick_lane(gt_ref, sl, h + HEADS) for sl in sls], [gcr_ref[0, c] for c in range(len(sls))])


def _gdn_prep(q, k, v, beta, gc, gt, gcr):
    t = q.shape[0]
    gch = min(PREP_CHUNKS, t // CHUNK)
    rows = gch * CHUNK

    def body(q_ref, k_ref, v_ref, b_ref, gc_ref, gt_ref, gcr_ref, w_ref, u_ref, qg_ref, kg_ref, at_ref, gl_ref, tm_ref):
        h = pl.program_id(1)
        sls = [pl.ds(c * CHUNK, CHUNK) for c in range(gch)]
        outs, tmats = _prep_chunks(*_prep_inputs(q_ref, k_ref, v_ref, b_ref, gc_ref, gt_ref, gcr_ref, sls, h))
        for c, (sl, (w, u, qg, kg, attn, gl), tmat) in enumerate(zip(sls, outs, tmats)):
            w_ref[sl, :] = w.astype(BF16)
            u_ref[sl, :] = u
            qg_ref[sl, :] = qg.astype(BF16)
            kg_ref[sl, :] = kg.astype(BF16)
            at_ref[0, sl, :] = attn.astype(BF16)
            gl_ref[0, pl.ds(c * 8, 8), :] = gl
            tm_ref[0, sl, :] = tmat

    hb, col, gcrow, square, glb = _prep_specs(rows, gch)
    wide = HEADS * HEAD_DIM
    return pl.pallas_call(
        body,
        name="gdn_prep",
        grid=(t // rows, HEADS),
        in_specs=[hb, hb, hb, col, col, col, gcrow],
        out_specs=[hb, hb, hb, hb, square, glb, square],
        out_shape=[
            jax.ShapeDtypeStruct((t, wide), BF16),
            jax.ShapeDtypeStruct((t, wide), F32),
            jax.ShapeDtypeStruct((t, wide), BF16),
            jax.ShapeDtypeStruct((t, wide), BF16),
            jax.ShapeDtypeStruct((HEADS, t, CHUNK), BF16),
            jax.ShapeDtypeStruct((HEADS, t // CHUNK * 8, HEAD_DIM), F32),
            jax.ShapeDtypeStruct((HEADS, t, CHUNK), F32),
        ],
        compiler_params=_params("parallel", "parallel"),
    )(q, k, v, beta, gc, gt, gcr)


def _gdn_prep_bwd(q, k, v, beta, gc, gt, gcr, tmat, dw, du, dqg, dkg, dattn, dgl):
    t = q.shape[0]
    gch = min(PREP_BWD_CHUNKS, t // CHUNK)
    rows = gch * CHUNK

    def body(q_ref, k_ref, v_ref, b_ref, gc_ref, gt_ref, gcr_ref, tm_ref, dw_ref, du_ref, dqg_ref, dkg_ref, dat_ref, dgl_ref,
             dq_ref, dk_ref, dv_ref, db_ref, dgc_ref, dgt_ref, dgcr_ref):
        h = pl.program_id(1)
        lane = _iota((1, LANES), 1)

        @pl.when(h == 0)
        def _():
            db_ref[...] = jnp.zeros_like(db_ref)
            dgc_ref[...] = jnp.zeros_like(dgc_ref)
            dgt_ref[...] = jnp.zeros_like(dgt_ref)

        sls = [pl.ds(c * CHUNK, CHUNK) for c in range(gch)]
        known = [tm_ref[0, sl, :] for sl in sls]
        _, vjp = jax.vjp(lambda *a: _prep_chunks(*a, tmats=known)[0], *_prep_inputs(q_ref, k_ref, v_ref, b_ref, gc_ref, gt_ref, gcr_ref, sls, h))
        cots = tuple((dw_ref[sl, :], du_ref[sl, :], dqg_ref[sl, :], dkg_ref[sl, :], dat_ref[0, sl, :], dgl_ref[0, pl.ds(c * 8, 8), :]) for c, sl in enumerate(sls))
        dqs, dks, dvs, dbs, dgcs, dgts, dgcrs = vjp(cots)
        for c, sl in enumerate(sls):
            dq_ref[sl, :] = dqs[c]
            dk_ref[sl, :] = dks[c]
            dv_ref[sl, :] = dvs[c]
            db_ref[sl, :] += jnp.where(lane == h, dbs[c], 0.0)
            dgc_ref[sl, :] += jnp.where(lane == h + HEADS, dgcs[c], 0.0)
            dgt_ref[sl, :] += jnp.where(lane == h + HEADS, dgts[c], 0.0)
            dgcr_ref[0, c] = dgcrs[c]

    hb, col, gcrow, square, glb = _prep_specs(rows, gch)
    wide = HEADS * HEAD_DIM
    return pl.pallas_call(
        body,
        name="gdn_prep_bwd",
        grid=(t // rows, HEADS),
        in_specs=[hb, hb, hb, col, col, col, gcrow, square, hb, hb, hb, hb, square, glb],
        out_specs=[hb, hb, hb, col, col, col, gcrow],
        out_shape=[jax.ShapeDtypeStruct((t, wide), F32)] * 3 + [jax.ShapeDtypeStruct((t, LANES), F32)] * 3 + [jax.ShapeDtypeStruct((HEADS, t // CHUNK, 1, CHUNK), F32)],
        compiler_params=_params("parallel", "arbitrary"),
    )(q, k, v, beta, gc, gt, gcr, tmat, dw, du, dqg, dkg, dattn, dgl)


def _gdn_scan(w, u, qg, kg, attn, gl):
    t = w.shape[0]
    n = t // CHUNK
    wide = HEADS * HEAD_DIM

    def body(w_ref, u_ref, qg_ref, kg_ref, at_ref, gl_ref, o_ref, st_ref, s_ref):
        @pl.when(pl.program_id(0) == 0)
        def _():
            s_ref[...] = jnp.zeros_like(s_ref)

        heads = range(HEADS)
        cols = [pl.ds(h * HEAD_DIM, HEAD_DIM) for h in heads]
        ss = [s_ref[h] for h in heads]
        sbs = [s.astype(BF16) for s in ss]
        vbs = [(u_ref[:, hs] - _dot(w_ref[:, hs], sb)).astype(BF16) for hs, sb in zip(cols, sbs)]
        outs = [_dot(qg_ref[:, hs], sb) + _dot(at_ref[h], vb) for h, hs, sb, vb in zip(heads, cols, sbs, vbs)]
        new = [s * jnp.tile(gl_ref[h], (HEAD_DIM // 8, 1)) + _dot(kg_ref[:, hs], vb, TN) for h, hs, s, vb in zip(heads, cols, ss, vbs)]
        for h, hs in zip(heads, cols):
            st_ref[0, h] = ss[h]
            o_ref[:, hs] = outs[h]
            s_ref[h] = new[h]

    row = pl.BlockSpec((CHUNK, wide), lambda i: (i, 0))
    return pl.pallas_call(
        body,
        name="gdn_scan",
        grid=(n,),
        in_specs=[row, row, row, row, pl.BlockSpec((HEADS, CHUNK, CHUNK), lambda i: (0, i, 0)), pl.BlockSpec((HEADS, 8, HEAD_DIM), lambda i: (0, i, 0))],
        out_specs=[row, pl.BlockSpec((1, HEADS, HEAD_DIM, HEAD_DIM), lambda i: (i, 0, 0, 0))],
        out_shape=[jax.ShapeDtypeStruct((t, wide), F32), jax.ShapeDtypeStruct((n, HEADS, HEAD_DIM, HEAD_DIM), F32)],
        scratch_shapes=[pltpu.VMEM((HEADS, HEAD_DIM, HEAD_DIM), F32)],
        compiler_params=_params("arbitrary"),
    )(w, u, qg, kg, attn, gl)


def _gdn_scan_bwd(w, u, qg, kg, attn, gl, states, do):
    t = w.shape[0]
    n = t // CHUNK
    wide = HEADS * HEAD_DIM

    def body(w_ref, u_ref, qg_ref, kg_ref, at_ref, gl_ref, st_ref, do_ref, dw_ref, du_ref, dqg_ref, dkg_ref, dat_ref, dgl_ref, ds_ref):
        @pl.when(pl.program_id(0) == 0)
        def _():
            ds_ref[...] = jnp.zeros_like(ds_ref)

        heads = range(HEADS)
        cols = [pl.ds(h * HEAD_DIM, HEAD_DIM) for h in heads]
        ss = [st_ref[0, h] for h in heads]
        sbs = [s.astype(BF16) for s in ss]
        dsns = [ds_ref[h] for h in heads]
        dsbs = [d.astype(BF16) for d in dsns]
        dobs = [do_ref[:, hs].astype(BF16) for hs in cols]
        vbs = [(u_ref[:, hs] - _dot(w_ref[:, hs], sb)).astype(BF16) for hs, sb in zip(cols, sbs)]
        dvns = [_dot(at_ref[h], dob, TN) + _dot(kg_ref[:, hs], dsb) for h, hs, dob, dsb in zip(heads, cols, dobs, dsbs)]
        dvbs = [d.astype(BF16) for d in dvns]
        for h, hs in zip(heads, cols):
            dat_ref[h] = _dot(dobs[h], vbs[h], NT)
            dqg_ref[:, hs] = _dot(dobs[h], sbs[h], NT)
            dkg_ref[:, hs] = _dot(vbs[h], dsbs[h], NT)
            du_ref[:, hs] = dvns[h]
            dw_ref[:, hs] = -_dot(dvbs[h], sbs[h], NT)
            dgl_ref[h] = jnp.sum((dsns[h] * ss[h]).reshape(HEAD_DIM // 8, 8, HEAD_DIM), axis=0)
        new = [dsn * jnp.tile(gl_ref[h], (HEAD_DIM // 8, 1)) + _dot(qg_ref[:, hs], dob, TN) - _dot(w_ref[:, hs], dvb, TN)
               for h, hs, dsn, dob, dvb in zip(heads, cols, dsns, dobs, dvbs)]
        for h in heads:
            ds_ref[h] = new[h]

    row = pl.BlockSpec((CHUNK, wide), lambda i: (n - 1 - i, 0))
    at = pl.BlockSpec((HEADS, CHUNK, CHUNK), lambda i: (0, n - 1 - i, 0))
    glb = pl.BlockSpec((HEADS, 8, HEAD_DIM), lambda i: (0, n - 1 - i, 0))
    return pl.pallas_call(
        body,
        name="gdn_scan_bwd",
        grid=(n,),
        in_specs=[row, row, row, row, at, glb, pl.BlockSpec((1, HEADS, HEAD_DIM, HEAD_DIM), lambda i: (n - 1 - i, 0, 0, 0)), row],
        out_specs=[row, row, row, row, at, glb],
        out_shape=[jax.ShapeDtypeStruct((t, wide), F32)] * 4 + [jax.ShapeDtypeStruct((HEADS, t, CHUNK), F32), jax.ShapeDtypeStruct((HEADS, n * 8, HEAD_DIM), F32)],
        scratch_shapes=[pltpu.VMEM((HEADS, HEAD_DIM, HEAD_DIM), F32)],
        compiler_params=_params("arbitrary"),
    )(w, u, qg, kg, attn, gl, states, do)


SB_Q = 1024
SB_K = 256
SB_STEP = 2
SB_DEAD = -105.0


def _sb_scores(q, k):
    z = _dot(q, k, NT) * (HEAD_DIM ** -0.5)
    e = jnp.exp(-jnp.abs(z))
    lb = jnp.minimum(z, 0.0) - jnp.log(1.0 + e)
    return z, e, lb, lb - z


def _tri(n, rel):
    return jnp.where(rel(_iota((n, n), 0), _iota((n, n), 1)), 1.0, 0.0).astype(BF16)


def _lanes(col):
    return jnp.broadcast_to(col, (col.shape[0], LANES))


def _sb_fwd(q, k, v):
    t = q.shape[0]
    bq, bk = min(SB_Q, t), min(SB_K, t)
    nsub, rep = bq // bk, bk // LANES
    nstep = min(SB_STEP, nsub)
    steps_per_tile = nsub // nstep

    def body(q_ref, k_ref, v_ref, o_ref, rt_ref, first_ref):
        h = pl.program_id(0)
        i = pl.program_id(1)
        o_ref[...] = jnp.zeros_like(o_ref)
        rt_ref[...] = jnp.zeros_like(rt_ref)
        after = _tri(bk, lambda r, c: r > c)

        def block(j, r0, diag):
            st = pl.multiple_of(j * bk, bk)
            kv, vv = k_ref[pl.ds(st, bk), :], v_ref[pl.ds(st, bk), :]
            _, _, lb, l1m = _sb_scores(q_ref[r0:, :], kv)
            if diag:
                mask = _iota((bq - r0, bk), 1) + j * bk < _iota((bq - r0, bk), 0) + (r0 + i * bq)
                l1m = jnp.where(mask, l1m, 0.0)
            sums = _two_pass(l1m, after)
            run = rt_ref[r0:, :]
            a = jnp.exp(lb + jnp.tile(run, (1, rep)) + sums)
            if diag:
                a = jnp.where(mask, a, 0.0)
            o_ref[r0:, :] += _dot(a.astype(BF16), vv)
            rt_ref[r0:, :] = run + _lanes(sums[:, 0:1] + l1m[:, 0:1])

        for s in reversed(range(nsub)):
            block(i * nsub + s, s * bk, True)

        def alive(carry):
            u, highest = carry
            return jnp.logical_and(u >= 0, highest > SB_DEAD)

        def step(carry):
            u, _ = carry
            for s in reversed(range(nstep)):
                block(u * nstep + s, 0, False)
            return u - 1, jnp.max(rt_ref[...])

        u_end, _ = lax.while_loop(alive, step, (i * steps_per_tile - 1, jnp.max(rt_ref[...])))
        first_ref[h, i] = u_end + 1

    qb = pl.BlockSpec((bq, HEAD_DIM), lambda h, i: (i, h))
    full = pl.BlockSpec((t, HEAD_DIM), lambda h, i: (0, h))
    return pl.pallas_call(
        body,
        name="sb_fwd",
        grid=(HEADS, t // bq),
        in_specs=[qb, full, full],
        out_specs=[qb, qb, pl.BlockSpec(memory_space=pltpu.SMEM)],
        out_shape=[jax.ShapeDtypeStruct(q.shape, F32), jax.ShapeDtypeStruct(q.shape, F32), jax.ShapeDtypeStruct((HEADS, t // bq), jnp.int32)],
        compiler_params=_params("arbitrary", "arbitrary"),
    )(q, k, v)


def _sb_bwd(q, k, v, rt, first, do):
    t = q.shape[0]
    bq, bk = min(SB_Q, t), min(SB_K, t)
    nsub, rep = bq // bk, bk // LANES
    nstep = min(SB_STEP, nsub)
    steps_per_tile = nsub // nstep
    scale = HEAD_DIM ** -0.5

    def body(first_ref, q_ref, k_ref, v_ref, rt_ref, do_ref, dq_ref, dk_ref, dv_ref, left_ref, pg_ref):
        h = pl.program_id(0)
        i = pl.program_id(1)

        @pl.when(i == 0)
        def _():
            dk_ref[...] = jnp.zeros_like(dk_ref)
            dv_ref[...] = jnp.zeros_like(dv_ref)

        dq_ref[...] = jnp.zeros_like(dq_ref)
        left_ref[...] = jnp.zeros_like(left_ref)
        pg_ref[...] = jnp.zeros_like(pg_ref)
        upto = _tri(bk, lambda r, c: r <= c)

        def block(j, r0, diag):
            st = pl.multiple_of(j * bk, bk)
            kv, vv = k_ref[pl.ds(st, bk), :], v_ref[pl.ds(st, bk), :]
            qv = q_ref[r0:, :]
            dob = do_ref[r0:, :].astype(BF16)
            _, _, lb, l1m = _sb_scores(qv, kv)
            if diag:
                mask = _iota((bq - r0, bk), 1) + j * bk < _iota((bq - r0, bk), 0) + (r0 + i * bq)
                l1m = jnp.where(mask, l1m, 0.0)
            sums = _two_pass(l1m, upto)
            left = left_ref[r0:, :]
            a = jnp.exp(lb + jnp.tile(rt_ref[r0:, :] - left, (1, rep)) - sums)
            if diag:
                a = jnp.where(mask, a, 0.0)
            g = _dot(dob, vv, NT) * a
            dv_ref[pl.ds(st, bk), :] += _dot(a.astype(BF16), dob, TN)
            gsum = _two_pass(g, upto)
            pg = pg_ref[r0:, :]
            dz = g - jnp.exp(lb) * (jnp.tile(pg, (1, rep)) + gsum)
            if diag:
                dz = jnp.where(mask, dz, 0.0)
            dzb = (dz * scale).astype(BF16)
            dk_ref[pl.ds(st, bk), :] += _dot(dzb, qv, TN)
            dq_ref[r0:, :] += _dot(dzb, kv)
            left_ref[r0:, :] = left + _lanes(sums[:, bk - 1:bk])
            pg_ref[r0:, :] = pg + _lanes(gsum[:, bk - 1:bk])

        def step(u, carry):
            for s in range(nstep):
                block(u * nstep + s, 0, False)
            return carry

        lax.fori_loop(first_ref[h, i], i * steps_per_tile, step, 0)
        for s in range(nsub):
            block(i * nsub + s, s * bk, True)

    qb = pl.BlockSpec((bq, HEAD_DIM), lambda h, i: (i, h))
    full = pl.BlockSpec((t, HEAD_DIM), lambda h, i: (0, h))
    return pl.pallas_call(
        body,
        name="sb_bwd",
        grid=(HEADS, t // bq),
        in_specs=[pl.BlockSpec(memory_space=pltpu.SMEM), qb, full, full, qb, qb],
        out_specs=[qb, full, full],
        out_shape=[jax.ShapeDtypeStruct(q.shape, F32)] * 3,
        scratch_shapes=[pltpu.VMEM((bq, LANES), F32), pltpu.VMEM((bq, LANES), F32)],
        compiler_params=_params("arbitrary", "arbitrary"),
    )(first, q, k, v, rt, do)


def _adamw(w, g, m, v, name, tm=256):
    r, c = w.shape
    tm = tm if r % tm == 0 else r

    def body(w_ref, g_ref, m_ref, v_ref, d_ref, nm_ref, nv_ref):
        gv = g_ref[...]
        nm = ADAM_B1 * m_ref[...] + (1.0 - ADAM_B1) * gv
        nv = ADAM_B2 * v_ref[...] + (1.0 - ADAM_B2) * (gv * gv)
        m_hat = nm / (1.0 - ADAM_B1 ** ADAM_STEP)
        v_hat = nv / (1.0 - ADAM_B2 ** ADAM_STEP)
        d_ref[...] = -ADAM_LR * (m_hat / (jnp.sqrt(v_hat) + ADAM_EPS) + ADAM_WD * w_ref[...])
        nm_ref[...] = nm
        nv_ref[...] = nv

    blk = pl.BlockSpec((tm, c), lambda i: (i, 0))
    return pl.pallas_call(
        body,
        name=name,
        grid=(r // tm,),
        in_specs=[blk] * 4,
        out_specs=[blk] * 3,
        out_shape=[jax.ShapeDtypeStruct((r, c), F32)] * 3,
        compiler_params=_params("parallel"),
    )(w, g, m, v)


def _local_step(x, tgt, gains, wts, small):
    mix_pre, mix_post, mlp_pre, mlp_post, kv_gain = gains
    w_qkvg, w_ba, w_out, w_kv, w_q, w_o, w_up, w_down = wts
    conv_w, a_log, dt_bias, out_gain = small
    t, d = x.shape
    row = lambda a, i=None: a[i:i + 1] if i is not None else a
    al = jnp.zeros((1, LANES), F32).at[:, HEADS:2 * HEADS].set(a_log)
    dtb = jnp.zeros((1, LANES), F32).at[:, HEADS:2 * HEADS].set(dt_bias)
    og = jnp.tile(out_gain, (1, HEADS))
    full = lambda a: (a, a.shape[1], 0)

    (h0,) = _rowwise("norm_in", _fn_norm, [full(x)], [row(mix_pre, 0)], [(d, BF16)])
    qkvg = _matmul(h0, w_qkvg, "nn", F32, "mm_gdn_in", tk=1024)
    ba = _matmul(h0, w_ba, "nn", F32, "mm_gdn_ba", tk=1024)
    conv, gq, gk, gv = _conv_fwd(qkvg, conv_w)
    conv_qkv = [(conv, d, 0), (conv, d, 1), (conv, d, 2)]
    beta, gc, gt = _rowwise("gates", _fn_gates, [full(ba)], [al, dtb], [(LANES, F32)] * 3)
    gcr = jnp.swapaxes(gc[:, HEADS:2 * HEADS], 0, 1).reshape(HEADS, t // CHUNK, 1, CHUNK)
    pw, pu, pqg, pkg, pattn, pgl, ptm = _gdn_prep(gq, gk, gv, beta, gc, gt, gcr)
    o_gdn, states = _gdn_scan(pw, pu, pqg, pkg, pattn, pgl)
    (on,) = _rowwise("out_norm", _fn_outnorm, [full(o_gdn), (qkvg, d, 3)], [og], [(d, BF16)])
    mix0 = _matmul(on, w_out, "nn", F32, "mm_gdn_out", tk=1024)
    x1, h1 = _rowwise("res_a0", _fn_res_norm, [full(x), full(mix0)], [row(mix_post, 0), row(mlp_pre, 0)], [(d, F32), (d, BF16)])
    (a0,) = _matmul(h1, w_up[0], "nn", (BF16,), "mm_up0", tk=1024, epilogue=_relu2_of)
    d0 = _matmul(a0, w_down[0], "nn", F32, "mm_down0")
    x2, hkv, hq = _rowwise("res_b0", _fn_res_norm2, [full(x1), full(d0)], [row(mlp_post, 0), kv_gain, row(mix_pre, 1)], [(d, F32), (d, BF16), (d, BF16)])
    w_k, w_v = w_kv[:, :d], w_kv[:, d:]
    kp = _matmul(hkv, w_k, "nn", BF16, "mm_k", tk=1024)
    vp = _matmul(hkv, w_v, "nn", BF16, "mm_v", tk=1024)
    qp = _matmul(hq, w_q, "nn", BF16, "mm_q", tk=1024)
    o_sb, rt, sb_first = _sb_fwd(qp, kp, vp)
    mix1 = _matmul(o_sb, w_o, "nn", F32, "mm_sb_out", tk=1024)
    x3, h3 = _rowwise("res_a1", _fn_res_norm, [full(x2), full(mix1)], [row(mix_post, 1), row(mlp_pre, 1)], [(d, F32), (d, BF16)])
    (a1,) = _matmul(h3, w_up[1], "nn", (BF16,), "mm_up1", tk=1024, epilogue=_relu2_of)
    d1 = _matmul(a1, w_down[1], "nn", F32, "mm_down1")

    loss, dx3, dd1, g_mlp_post1 = _loss_call(x3, d1, tgt, row(mlp_post, 1))
    (du1,) = _matmul(dd1, w_down[1], "nt", (BF16,), "mm_down1_dx", tk=1024, epilogue=_relu2_cotangent, extras=[a1])
    g_down1 = _matmul(a1, dd1, "tn", F32, "mm_down1_dw")
    dh3 = _matmul(du1, w_up[1], "nt", F32, "mm_up1_dx")
    g_up1 = _matmul(h3, du1, "tn", F32, "mm_up1_dw")
    (dx2, dmix1), (g_mix_post1, g_mlp_pre1) = _rowwise_bwd(
        "res_a1_bwd", _fn_res_norm, [full(x2), full(mix1)], [row(mix_post, 1), row(mlp_pre, 1)], [dx3, dh3], [F32, BF16])
    do_sb = _matmul(dmix1, w_o, "nt", BF16, "mm_sb_out_dx")
    g_o = _matmul(o_sb, dmix1, "tn", F32, "mm_sb_out_dw")
    dqp, dkp, dvp = _sb_bwd(qp, kp, vp, rt, sb_first, do_sb)
    dhq = _matmul(dqp, w_q, "nt", F32, "mm_q_dx")
    g_q = _matmul(hq, dqp, "tn", F32, "mm_q_dw")
    dhkv = _matmul(dvp, w_v, "nt", F32, "mm_v_dx", add=_matmul(dkp, w_k, "nt", F32, "mm_k_dx"))
    g_kv = jnp.concatenate([_matmul(hkv, dkp, "tn", F32, "mm_k_dw"), _matmul(hkv, dvp, "tn", F32, "mm_v_dw")], axis=1)
    (dx1, dd0), (g_mlp_post0, g_kv_gain, g_mix_pre1) = _rowwise_bwd(
        "res_b0_bwd", _fn_res_norm2, [full(x1), full(d0)], [row(mlp_post, 0), kv_gain, row(mix_pre, 1)], [dx2, dhkv, dhq], [F32, BF16])
    (du0,) = _matmul(dd0, w_down[0], "nt", (BF16,), "mm_down0_dx", tk=1024, epilogue=_relu2_cotangent, extras=[a0])
    g_down0 = _matmul(a0, dd0, "tn", F32, "mm_down0_dw")
    dh1 = _matmul(du0, w_up[0], "nt", F32, "mm_up0_dx")
    g_up0 = _matmul(h1, du0, "tn", F32, "mm_up0_dw")
    (dx0, dmix0), (g_mix_post0, g_mlp_pre0) = _rowwise_bwd(
        "res_a0_bwd", _fn_res_norm, [full(x), full(mix0)], [row(mix_post, 0), row(mlp_pre, 0)], [dx1, dh1], [F32, BF16])
    don = _matmul(dmix0, w_out, "nt", F32, "mm_gdn_out_dx")
    g_out = _matmul(on, dmix0, "tn", F32, "mm_gdn_out_dw")
    (do_gdn, dgate), (g_og,) = _rowwise_bwd("out_norm_bwd", _fn_outnorm, [full(o_gdn), (qkvg, d, 3)], [og], [don], [F32, F32])
    dpw, dpu, dpqg, dpkg, dpattn, dpgl = _gdn_scan_bwd(pw, pu, pqg, pkg, pattn, pgl, states, do_gdn)
    dgq, dgk, dgv, dbeta, dgc, dgt, dgcr = _gdn_prep_bwd(gq, gk, gv, beta, gc, gt, gcr, ptm, dpw, dpu, dpqg, dpkg, dpattn, dpgl)
    dgcr_lanes = jnp.pad(jnp.swapaxes(dgcr.reshape(HEADS, t), 0, 1), ((0, 0), (HEADS, LANES - 2 * HEADS)))
    gate_cots = [dbeta, dgc + dgcr_lanes, dgt]
    (dba,), (g_al, g_dtb) = _rowwise_bwd("gates_bwd", _fn_gates, [full(ba)], [al, dtb], gate_cots, [BF16])
    (dconv,), _ = _rowwise_bwd("post_conv_bwd", _fn_post, conv_qkv, [], [dgq, dgk, dgv], [F32] * 3, joined=True)
    dqkvg, g_conv = _conv_bwd(dconv, dgate, qkvg, conv_w)
    dh0b = _matmul(dba, w_ba, "nt", F32, "mm_gdn_ba_dx", tk=LANES)
    dh0 = _matmul(dqkvg, w_qkvg, "nt", F32, "mm_gdn_in_dx", add=dh0b)
    g_qkvg = _matmul(h0, dqkvg, "tn", F32, "mm_gdn_in_dw")
    g_ba = _matmul(h0, dba, "tn", F32, "mm_gdn_ba_dw")
    (grad_x,), (g_mix_pre0,) = _rowwise_bwd("norm_in_bwd", lambda xx, gg: (_rms(xx, gg), xx), [full(x)], [row(mix_pre, 0)], [dh0, dx0], [F32])

    grads = dict(
        mix_pre_gain=jnp.concatenate([g_mix_pre0, g_mix_pre1], axis=0),
        mix_post_gain=jnp.concatenate([g_mix_post0, g_mix_post1], axis=0),
        mlp_pre_gain=jnp.concatenate([g_mlp_pre0, g_mlp_pre1], axis=0),
        mlp_post_gain=jnp.concatenate([g_mlp_post0, g_mlp_post1], axis=0),
        mlp_w_up=jnp.stack([g_up0, g_up1]),
        mlp_w_down=jnp.stack([g_down0, g_down1]),
        gdn_w_in=jnp.concatenate([g_qkvg, g_ba[:, :2 * HEADS]], axis=1)[None],
        gdn_conv_w=g_conv[None, :CONV_K],
        gdn_a_log=g_al[:, HEADS:2 * HEADS],
        gdn_dt_bias=g_dtb[:, HEADS:2 * HEADS],
        gdn_out_gain=jnp.sum(g_og.reshape(HEADS, HEAD_DIM), axis=0, keepdims=True),
        gdn_w_out=g_out[None],
        kv_gain=g_kv_gain[0],
        w_kv=g_kv,
        sb_w_q=g_q[None],
        sb_w_o=g_o[None],
    )
    return loss, grad_x, grads


N_DEV = 8
N_CHIPS = 4
PACK_ROW_TILE = 128

_HBM = pl.BlockSpec(memory_space=pltpu.HBM)


def _place():
    return lax.axis_index("x"), lax.axis_index("y"), lax.axis_index("c")


def _other_chips(x, y):
    return [(1 - x, y), (x, 1 - y), (1 - x, 1 - y)]


def _remote(src, dst, send_sem, recv_sem, to):
    return pltpu.make_async_remote_copy(src_ref=src, dst_ref=dst, send_sem=send_sem, recv_sem=recv_sem, device_id=to, device_id_type=MESH)


def _gather8(v, name):
    rows, cols = v.shape

    def body(v_ref, out_ref, sum_ref, send_sems, recv_sems, local_sem):
        x, y, c = _place()
        me, sibling = (x, y, c), (x, y, 1 - c)
        chips = _other_chips(x, y)

        def blk(px, py, pc):
            return out_ref.at[pl.ds((4 * px + 2 * py + pc) * rows, rows), :]

        def copy(k, block, to, src=None):
            return _remote(blk(*block) if src is None else src, blk(*block), send_sems.at[k], recv_sems.at[k], to)

        mine = pltpu.make_async_copy(v_ref, blk(*me), local_sem)
        mine.start()
        first = [copy(0, me, sibling, src=v_ref)] + [copy(1 + j, me, (*chip, c), src=v_ref) for j, chip in enumerate(chips)]
        for cp in first:
            cp.start()
        passed = [copy(4 + j, (*chip, c), sibling) for j, chip in enumerate(chips)]
        for j, chip in enumerate(chips):
            copy(1 + j, (*chip, c), me).wait_recv()
            passed[j].start()
        copy(0, sibling, me).wait_recv()
        for j, chip in enumerate(chips):
            copy(4 + j, (*chip, 1 - c), me).wait_recv()
        for cp in first + passed:
            cp.wait_send()
        mine.wait()
        acc = out_ref[pl.ds(0, rows), :]
        for dev in range(1, N_DEV):
            acc = acc + out_ref[pl.ds(dev * rows, rows), :]
        sum_ref[...] = acc

    vm = pl.BlockSpec(memory_space=pltpu.VMEM)
    return pl.pallas_call(
        body,
        name=name,
        out_shape=[jax.ShapeDtypeStruct((N_DEV * rows, cols), v.dtype), jax.ShapeDtypeStruct((rows, cols), v.dtype)],
        in_specs=[vm],
        out_specs=[vm, vm],
        scratch_shapes=[pltpu.SemaphoreType.DMA((7,)), pltpu.SemaphoreType.DMA((7,)), pltpu.SemaphoreType.DMA],
    )(v)


def _hbm_call(body, name, arrs, out_shapes, sem_counts):
    n = len(arrs)

    def wrapped(*refs):
        body(refs[:n], refs[n:2 * n], *refs[2 * n:])

    return pl.pallas_call(
        wrapped,
        name=name,
        out_shape=[jax.ShapeDtypeStruct(s, a.dtype) for s, a in zip(out_shapes, arrs)],
        in_specs=[_HBM] * n,
        out_specs=[_HBM] * n,
        scratch_shapes=[pltpu.SemaphoreType.DMA((k,)) for k in sem_counts],
    )(*arrs)


def _gather_weights(arrs):
    n = len(arrs)

    def body(w_refs, out_refs, send_sems, recv_sems, fsend_sems, frecv_sems):
        x, y, c = _place()
        chips = _other_chips(x, y)
        s_me = 2 * x + y
        pairs = list(zip(w_refs, out_refs))
        first = [_remote(w.at[c], o.at[s_me, c], send_sems.at[3 * a + j], recv_sems.at[3 * a + j], (px, py, c))
                 for a, (w, o) in enumerate(pairs) for j, (px, py) in enumerate(chips)]
        for cp in first:
            cp.start()
        passed = []
        for a, (w, o) in enumerate(pairs):
            for j, (px, py) in enumerate(chips):
                half = o.at[2 * px + py, c]
                _remote(half, half, send_sems.at[3 * a + j], recv_sems.at[3 * a + j], (px, py, c)).wait_recv()
                fwd = _remote(half, half, fsend_sems.at[3 * a + j], frecv_sems.at[3 * a + j], (x, y, 1 - c))
                fwd.start()
                passed.append(fwd)
        for a, (w, o) in enumerate(pairs):
            for j, (px, py) in enumerate(chips):
                half = o.at[2 * px + py, 1 - c]
                _remote(half, half, fsend_sems.at[3 * a + j], frecv_sems.at[3 * a + j], (x, y, 1 - c)).wait_recv()
        for cp in first + passed:
            cp.wait_send()

    return _hbm_call(body, "gather_weights", arrs, [(N_CHIPS,) + a.shape for a in arrs], [3 * n] * 4)


def _swap_halves(arrs):
    n = len(arrs)

    def body(g_refs, a_refs, send_sems, recv_sems):
        x, y, c = _place()
        cps = [_remote(g.at[1 - c], a, send_sems.at[i], recv_sems.at[i], (x, y, 1 - c)) for i, (g, a) in enumerate(zip(g_refs, a_refs))]
        for cp in cps:
            cp.start()
        for cp in cps:
            cp.wait()

    return _hbm_call(body, "grads_to_sibling", arrs, [a.shape[1:] for a in arrs], [n, n])


def _scatter_to_chips(arrs):
    n = len(arrs)

    def body(p_refs, b_refs, send_sems, recv_sems):
        x, y, c = _place()
        cps = [_remote(p.at[2 * px + py], b.at[j], send_sems.at[3 * i + j], recv_sems.at[3 * i + j], (px, py, c))
               for i, (p, b) in enumerate(zip(p_refs, b_refs)) for j, (px, py) in enumerate(_other_chips(x, y))]
        for cp in cps:
            cp.start()
        for cp in cps:
            cp.wait()

    return _hbm_call(body, "grads_to_chips", arrs, [(3,) + a.shape[1:] for a in arrs], [3 * n, 3 * n])


def _share_halves(arrs):
    n = len(arrs)

    def body(q_refs, out_refs, send_sems, recv_sems):
        x, y, c = _place()
        cps = [_remote(q, o, send_sems.at[i], recv_sems.at[i], (x, y, 1 - c)) for i, (q, o) in enumerate(zip(q_refs, out_refs))]
        for cp in cps:
            cp.start()
        for cp in cps:
            cp.wait()

    return _hbm_call(body, "grads_share", arrs, [a.shape for a in arrs], [n, n])


_GROUPS = (
    (("mlp_w_up", (2, 1024, 1024), "cols"), ("mlp_w_down", (2, 1024, 1024), "rows"), ("gdn_w_out", (1, 256, 1024), "rows"),
     ("sb_w_q", (1, 256, 1024), "rows"), ("sb_w_o", (1, 256, 1024), "rows")),
    (("w_kv", (1024, 512), "cols"),),
    (("gdn_w_in", (1, 1024, 1028), "cols"),),
)


def _numel(shape):
    n = 1
    for s in shape:
        n *= s
    return n


def _half_rows(shape):
    return _numel(shape[:-1]) // 2


def _pack_shards(shards, dtype):
    return tuple(jnp.concatenate([shards[n].astype(dtype).reshape(2, _half_rows(shape), shape[-1]) for n, shape, _ in grp], axis=1) for grp in _GROUPS)


def _unpack_shards(bufs):
    out = {}
    for grp, buf in zip(_GROUPS, bufs):
        off = 0
        for n, shape, _ in grp:
            out[n] = buf[:, off:off + _half_rows(shape)].reshape(shape)
            off += _half_rows(shape)
    return out


def _join(stacked, how):
    nd = stacked.ndim - 1
    ax = nd - 1 if how == "cols" else nd - 2
    moved = jnp.moveaxis(stacked, 0, ax)
    shape = list(stacked.shape[1:])
    shape[ax] *= N_CHIPS
    return moved.reshape(shape)


def _split(full, shard_shape, how):
    nd = len(shard_shape)
    ax = nd - 1 if how == "cols" else nd - 2
    shape = list(shard_shape)
    shape.insert(ax, N_CHIPS)
    return jnp.moveaxis(full.reshape(shape), ax, 0)


def _unpack_full(gathered):
    out = {}
    for grp, buf in zip(_GROUPS, gathered):
        off = 0
        for n, shape, how in grp:
            out[n] = _join(buf[:, :, off:off + _half_rows(shape)].reshape((N_CHIPS,) + shape), how)
            off += _half_rows(shape)
    return out


def _pack_full(full):
    bufs = []
    for grp in _GROUPS:
        parts = [_split(full[n], shape, how).reshape(N_CHIPS, 2, _half_rows(shape), shape[-1]) for n, shape, how in grp]
        buf = jnp.swapaxes(jnp.concatenate(parts, axis=2), 0, 1)
        bufs.append(buf.reshape(2, -1, buf.shape[-1]))
    return tuple(bufs)


_SMALL = (
    ("mix_pre_gain", (2, 1024)),
    ("mix_post_gain", (2, 1024)),
    ("mlp_pre_gain", (2, 1024)),
    ("mlp_post_gain", (2, 1024)),
    ("kv_gain", (1024,)),
    ("gdn_out_gain", (1, 128)),
    ("gdn_a_log", (1, 8)),
    ("gdn_dt_bias", (1, 8)),
    ("gdn_conv_w", (1, 4, 3072)),
    ("loss", ()),
)


def _rows_of(shape):
    return -(-_numel(shape) // LANES)


_SMALL_ROWS = -(-sum(_rows_of(s) for _, s in _SMALL) // 8) * 8


def _pack_small(vals):
    parts = []
    for n, shape in _SMALL:
        flat = vals[n].reshape(-1)
        parts.append(jnp.pad(flat, (0, _rows_of(shape) * LANES - flat.shape[0])))
    flat = jnp.concatenate(parts)
    return jnp.pad(flat, (0, _SMALL_ROWS * LANES - flat.shape[0])).reshape(_SMALL_ROWS, LANES)


def _unpack_small(packed):
    flat = packed.reshape(-1)
    out, off = {}, 0
    for n, shape in _SMALL:
        out[n] = flat[off:off + _numel(shape)].reshape(shape)
        off += _rows_of(shape) * LANES
    return out


_WEIGHTS = ("mix_pre_gain", "mix_post_gain", "mlp_pre_gain", "mlp_post_gain", "mlp_w_up", "mlp_w_down", "gdn_w_in", "gdn_conv_w",
            "gdn_a_log", "gdn_dt_bias", "gdn_out_gain", "gdn_w_out", "kv_gain", "w_kv", "sb_w_q", "sb_w_o")


def _as2d(a):
    return a.reshape(1, -1) if a.ndim <= 1 else a.reshape(-1, a.shape[-1])


def kernel(x, mix_pre_gain, mix_post_gain, mlp_pre_gain, mlp_post_gain, mlp_w_up, mlp_w_down, gdn_w_in, gdn_conv_w, gdn_a_log, gdn_dt_bias, gdn_out_gain, gdn_w_out, kv_gain, w_kv, sb_w_q, sb_w_o, loss_target, m_mix_pre_gain, m_mix_post_gain, m_mlp_pre_gain, m_mlp_post_gain, m_mlp_w_up, m_mlp_w_down, m_gdn_w_in, m_gdn_conv_w, m_gdn_a_log, m_gdn_dt_bias, m_gdn_out_gain, m_gdn_w_out, m_kv_gain, m_w_kv, m_sb_w_q, m_sb_w_o, v_mix_pre_gain, v_mix_post_gain, v_mlp_pre_gain, v_mlp_post_gain, v_mlp_w_up, v_mlp_w_down, v_gdn_w_in, v_gdn_conv_w, v_gdn_a_log, v_gdn_dt_bias, v_gdn_out_gain, v_gdn_w_out, v_kv_gain, v_w_kv, v_sb_w_q, v_sb_w_o):
    w = dict(mix_pre_gain=mix_pre_gain, mix_post_gain=mix_post_gain, mlp_pre_gain=mlp_pre_gain, mlp_post_gain=mlp_post_gain, mlp_w_up=mlp_w_up, mlp_w_down=mlp_w_down, gdn_w_in=gdn_w_in, gdn_conv_w=gdn_conv_w, gdn_a_log=gdn_a_log, gdn_dt_bias=gdn_dt_bias, gdn_out_gain=gdn_out_gain, gdn_w_out=gdn_w_out, kv_gain=kv_gain, w_kv=w_kv, sb_w_q=sb_w_q, sb_w_o=sb_w_o)
    m = dict(mix_pre_gain=m_mix_pre_gain, mix_post_gain=m_mix_post_gain, mlp_pre_gain=m_mlp_pre_gain, mlp_post_gain=m_mlp_post_gain, mlp_w_up=m_mlp_w_up, mlp_w_down=m_mlp_w_down, gdn_w_in=m_gdn_w_in, gdn_conv_w=m_gdn_conv_w, gdn_a_log=m_gdn_a_log, gdn_dt_bias=m_gdn_dt_bias, gdn_out_gain=m_gdn_out_gain, gdn_w_out=m_gdn_w_out, kv_gain=m_kv_gain, w_kv=m_w_kv, sb_w_q=m_sb_w_q, sb_w_o=m_sb_w_o)
    v = dict(mix_pre_gain=v_mix_pre_gain, mix_post_gain=v_mix_post_gain, mlp_pre_gain=v_mlp_pre_gain, mlp_post_gain=v_mlp_post_gain, mlp_w_up=v_mlp_w_up, mlp_w_down=v_mlp_w_down, gdn_w_in=v_gdn_w_in, gdn_conv_w=v_gdn_conv_w, gdn_a_log=v_gdn_a_log, gdn_dt_bias=v_gdn_dt_bias, gdn_out_gain=v_gdn_out_gain, gdn_w_out=v_gdn_w_out, kv_gain=v_kv_gain, w_kv=v_w_kv, sb_w_q=v_sb_w_q, sb_w_o=v_sb_w_o)
    cx, cy, cc = _place()
    chip = 2 * cx + cy
    conv_cols = gdn_conv_w.shape[-1]

    own = _pack_shards(w, BF16)
    full = _unpack_full([lax.dynamic_update_index_in_dim(others, mine, chip, 0) for others, mine in zip(_gather_weights(own), own)])
    conv_rows = jnp.pad(gdn_conv_w[0], ((0, 8 - CONV_K), (0, 0))).reshape(-1, LANES)
    conv_all, _ = _gather8(conv_rows, "gather_conv_w")
    conv_all = conv_all.reshape(N_CHIPS, 2, 8, conv_cols)[:, 0, :CONV_K]
    conv_full = jnp.swapaxes(conv_all, 0, 1).reshape(CONV_K, N_CHIPS * conv_cols)

    w_in = full["gdn_w_in"][0]
    wts = (w_in[:, :4 * HEADS * HEAD_DIM], jnp.pad(w_in[:, 4 * HEADS * HEAD_DIM:], ((0, 0), (0, LANES - 2 * HEADS))), full["gdn_w_out"][0], full["w_kv"],
           full["sb_w_q"][0], full["sb_w_o"][0], full["mlp_w_up"], full["mlp_w_down"])
    gains = (mix_pre_gain, mix_post_gain, mlp_pre_gain, mlp_post_gain, kv_gain[None])
    small = (conv_full, gdn_a_log, gdn_dt_bias, gdn_out_gain)
    loss_rows, grad_x, g_full = _local_step(x[0], loss_target[0], gains, wts, small)

    bufs = _pack_full(g_full)
    from_sibling = _swap_halves(bufs)
    partial, partial_bf16 = [], []
    for i, (buf, other) in enumerate(zip(bufs, from_sibling)):
        cols = buf.shape[-1]
        own_half = lax.dynamic_index_in_dim(buf, cc, 0, keepdims=False)
        p, pb = _rowwise(f"grads_add_sibling_{i}", lambda a, b: (a + b, a + b), [(own_half, cols, 0), (other, cols, 0)], [], [(cols, F32), (cols, BF16)], tm=PACK_ROW_TILE)
        partial.append(p.reshape(N_CHIPS, -1, cols))
        partial_bf16.append(pb.reshape(N_CHIPS, -1, cols))
    from_chips = _scatter_to_chips(tuple(partial_bf16))
    reduced = []
    for i, (p, others) in enumerate(zip(partial, from_chips)):
        cols = p.shape[-1]
        mine = lax.dynamic_index_in_dim(p, chip, 0, keepdims=False)
        (r,) = _rowwise(f"grads_add_chips_{i}", lambda a, b, c, d: (((a + b) + c) + d,),
                        [(mine, cols, 0), (others[0], cols, 0), (others[1], cols, 0), (others[2], cols, 0)], [], [(cols, F32)], tm=PACK_ROW_TILE)
        reduced.append(r)
    g_shard = _unpack_shards([jnp.where(cc == 0, jnp.stack([r, o]), jnp.stack([o, r])) for r, o in zip(reduced, _share_halves(tuple(reduced)))])

    g_small_local = {n: g_full[n] for n, _ in _SMALL if n != "loss"}
    g_small_local["loss"] = loss_rows[0, 0]
    _, small_sum = _gather8(_pack_small(g_small_local), "allreduce_small")
    g_small = _unpack_small(small_sum)
    loss = g_small.pop("loss")
    g_small["gdn_conv_w"] = lax.dynamic_slice_in_dim(g_small["gdn_conv_w"], chip * conv_cols, conv_cols, axis=2)

    grads = {**g_shard, **g_small}
    deltas, new_m, new_v = {}, {}, {}
    for n in _WEIGHTS:
        d2, m2, v2 = _adamw(_as2d(w[n]), _as2d(grads[n]), _as2d(m[n]), _as2d(v[n]), "adamw_" + n)
        deltas[n], new_m[n], new_v[n] = d2.reshape(w[n].shape), m2.reshape(w[n].shape), v2.reshape(w[n].shape)
    return (loss, grad_x[None], *[grads[n].reshape(w[n].shape) for n in _WEIGHTS], *[deltas[n] for n in _WEIGHTS],
            *[new_m[n] for n in _WEIGHTS], *[new_v[n] for n in _WEIGHTS])
```

```python
import functools

import jax
import jax.numpy as jnp
from jax import lax
from jax.experimental import pallas as pl
from jax.experimental.pallas import tpu as pltpu

F32, BF16 = jnp.float32, jnp.bfloat16
HI = lax.Precision.HIGHEST
MESH = pl.DeviceIdType.MESH

EPS = 1e-6
D_MODEL = 1024
HEADS = 8
HEAD_DIM = 128
CHUNK = 64
CHUNK_SHIFT = CHUNK.bit_length() - 1
CONV_K = 4
D_FF = 4096
QKV = 3 * HEADS * HEAD_DIM

ADAM_LR, ADAM_B1, ADAM_B2, ADAM_EPS, ADAM_WD, ADAM_STEP = 0.001, 0.9, 0.999, 1e-08, 0.01, 10

VMEM_LIMIT_BYTES = 48 * 1024 * 1024
LANES = 128

NN = ((1,), (0,))
NT = ((1,), (1,))
TN = ((0,), (0,))


def _dot(a, b, dims=NN, precision=None):
    return lax.dot_general(a, b, (dims, ((), ())), precision=precision, preferred_element_type=F32)


def _params(*sem):
    return pltpu.CompilerParams(dimension_semantics=sem, vmem_limit_bytes=VMEM_LIMIT_BYTES)


def _iota(shape, axis):
    return lax.broadcasted_iota(jnp.int32, shape, axis)


def _matmul(a, b, mode, out_dtype, name, tm=1024, tn=1024, tk=1024, add=None, epilogue=None, extras=()):
    if mode == "nn":
        (m, k), (k2, n) = a.shape, b.shape
    elif mode == "nt":
        (m, k), (n, k2) = a.shape, b.shape
    else:
        (k, m), (k2, n) = a.shape, b.shape
    assert k == k2, (a.shape, b.shape, mode)
    tm, tn, tk = min(tm, m), min(tn, n), min(tk, k)
    assert m % tm == 0 and n % tn == 0 and k % tk == 0, (a.shape, b.shape, mode)
    nk = k // tk
    dims = {"nn": NN, "nt": NT, "tn": TN}[mode]
    tiles = ([add] if add is not None else []) + list(extras)
    out_dtypes = out_dtype if epilogue is not None else (out_dtype,)
    n_in = 2 + len(tiles)

    def finish(acc, extra_refs, o_refs):
        res = (acc,) if epilogue is None else epilogue(acc, *[r[...] for r in extra_refs])
        for o_ref, r in zip(o_refs, res):
            o_ref[...] = r.astype(o_ref.dtype)

    def body(*refs):
        a_ref, b_ref = refs[:2]
        extra_refs = refs[n_in - len(extras):n_in]
        o_refs, acc_ref = refs[n_in:-1], refs[-1]
        prod = _dot(a_ref[...].astype(BF16), b_ref[...].astype(BF16), dims)
        if nk == 1:
            finish(prod + refs[2][...].astype(F32) if add is not None else prod, extra_refs, o_refs)
            return
        kk = pl.program_id(2)

        @pl.when(kk == 0)
        def _():
            acc_ref[...] = refs[2][...].astype(F32) if add is not None else jnp.zeros_like(acc_ref)

        acc_ref[...] += prod

        @pl.when(kk == nk - 1)
        def _():
            finish(acc_ref[...], extra_refs, o_refs)

    a_spec = pl.BlockSpec((tk, tm), lambda i, j, kk: (kk, i)) if mode == "tn" else pl.BlockSpec((tm, tk), lambda i, j, kk: (i, kk))
    b_spec = pl.BlockSpec((tn, tk), lambda i, j, kk: (j, kk)) if mode == "nt" else pl.BlockSpec((tk, tn), lambda i, j, kk: (kk, j))
    o_spec = pl.BlockSpec((tm, tn), lambda i, j, kk: (i, j))
    res = pl.pallas_call(
        body,
        name=name,
        grid=(m // tm, n // tn, nk),
        in_specs=[a_spec, b_spec] + [o_spec] * len(tiles),
        out_specs=[o_spec] * len(out_dtypes),
        out_shape=[jax.ShapeDtypeStruct((m, n), dt) for dt in out_dtypes],
        scratch_shapes=[pltpu.VMEM((tm, tn), F32)],
        compiler_params=_params("parallel", "parallel", "arbitrary"),
    )(a, b, *tiles)
    return res if epilogue is not None else res[0]


def _row_specs(rows, tm):
    return [pl.BlockSpec((tm, w), lambda i, cb=cb: (i, cb)) for _, w, cb in rows]


def _full_spec(p):
    return pl.BlockSpec(p.shape, lambda i: (0,) * p.ndim)


def _rowwise(name, fn, rows, params, outs, tm=256):
    t = rows[0][0].shape[0]
    tm = min(tm, t)
    nr, npar = len(rows), len(params)

    def body(*refs):
        ins = [r[...].astype(F32) for r in refs[:nr]]
        ps = [p[...] for p in refs[nr:nr + npar]]
        res = fn(*ins, *ps)
        for o_ref, r in zip(refs[nr + npar:], res):
            o_ref[...] = r.astype(o_ref.dtype)

    return pl.pallas_call(
        body,
        name=name,
        grid=(t // tm,),
        in_specs=_row_specs(rows, tm) + [_full_spec(p) for p in params],
        out_specs=[pl.BlockSpec((tm, w), lambda i: (i, 0)) for w, _ in outs],
        out_shape=[jax.ShapeDtypeStruct((t, w), dt) for w, dt in outs],
        compiler_params=_params("parallel"),
    )(*[r[0] for r in rows], *params)


def _rowwise_bwd(name, fn, rows, params, cots, grad_dtypes, tm=256, joined=False):
    t = rows[0][0].shape[0]
    tm = min(tm, t)
    nr, npar, nc = len(rows), len(params), len(cots)
    want = [j for j, dt in enumerate(grad_dtypes) if dt is not None]
    widths = [rows[j][1] for j in want]
    n_row_outs = 1 if joined else len(want)

    def body(*refs):
        i = pl.program_id(0)
        ins = [r[...].astype(F32) for r in refs[:nr]]
        ps = [p[...] for p in refs[nr:nr + npar]]
        cs = tuple(c[...].astype(F32) for c in refs[nr + npar:nr + npar + nc])
        _, vjp = jax.vjp(fn, *ins, *ps)
        gs = vjp(cs)
        outs = refs[nr + npar + nc:]
        if joined:
            off = 0
            for j, w in zip(want, widths):
                outs[0][:, off:off + w] = gs[j].astype(outs[0].dtype)
                off += w
        else:
            for o_ref, j in zip(outs, want):
                o_ref[...] = gs[j].astype(o_ref.dtype)
        pg_refs = outs[n_row_outs:]

        @pl.when(i == 0)
        def _():
            for pg in pg_refs:
                pg[...] = jnp.zeros_like(pg)

        for pg, g in zip(pg_refs, gs[nr:]):
            pg[...] += g

    if joined:
        row_specs = [pl.BlockSpec((tm, sum(widths)), lambda i: (i, 0))]
        row_shapes = [jax.ShapeDtypeStruct((t, sum(widths)), grad_dtypes[want[0]])]
    else:
        row_specs = [pl.BlockSpec((tm, w), lambda i: (i, 0)) for w in widths]
        row_shapes = [jax.ShapeDtypeStruct((t, w), grad_dtypes[j]) for j, w in zip(want, widths)]
    res = pl.pallas_call(
        body,
        name=name,
        grid=(t // tm,),
        in_specs=_row_specs(rows, tm) + [_full_spec(p) for p in params] + [pl.BlockSpec((tm, c.shape[1]), lambda i: (i, 0)) for c in cots],
        out_specs=row_specs + [_full_spec(p) for p in params],
        out_shape=row_shapes + [jax.ShapeDtypeStruct(p.shape, F32) for p in params],
        compiler_params=_params("arbitrary"),
    )(*[r[0] for r in rows], *params, *cots)
    return res[:n_row_outs], res[n_row_outs:]


def _rms(x, g):
    return x * lax.rsqrt(jnp.mean(x * x, axis=-1, keepdims=True) + EPS) * g


def _sigmoid(x):
    return 1.0 / (1.0 + jnp.exp(-x))


def _softplus(x):
    return jnp.maximum(x, 0.0) + jnp.log1p(jnp.exp(-jnp.abs(x)))


def _two_pass(x, m):
    hi = x.astype(BF16)
    lo = (x - hi.astype(F32)).astype(BF16)
    return _dot(hi, m) + _dot(lo, m)


def _head_sum_impl(x):
    w = HEADS * HEAD_DIM
    fold = jnp.where((_iota((w, LANES), 0) >> 7) == _iota((w, LANES), 1), 1.0, 0.0).astype(BF16)
    spread = jnp.where(_iota((LANES, w), 0) == (_iota((LANES, w), 1) >> 7), 1.0, 0.0).astype(BF16)
    return _two_pass(_two_pass(x, fold), spread)


@jax.custom_vjp
def _head_sum(x):
    return _head_sum_impl(x)


_head_sum.defvjp(lambda x: (_head_sum_impl(x), None), lambda _, g: (_head_sum_impl(g),))


def _fn_norm(x, g):
    return (_rms(x, g),)


def _fn_gates(ba, al, dt):
    col = _iota((1, LANES), 1)
    g = jnp.where((col >= HEADS) & (col < 2 * HEADS), -jnp.exp(al) * _softplus(ba + dt), 0.0)
    rows = ba.shape[0]
    r, c = _iota((rows, rows), 0), _iota((rows, rows), 1)
    same = (r >> CHUNK_SHIFT) == (c >> CHUNK_SHIFT)
    gc = _dot(jnp.where(same & (r >= c), 1.0, 0.0), g, precision=HI)
    gtot = _dot(jnp.where(same, 1.0, 0.0), g, precision=HI)
    return _sigmoid(ba), gc, gtot


def _fn_post_q(c):
    s = c * _sigmoid(c)
    return (s * lax.rsqrt(_head_sum(s * s) + EPS) * (HEAD_DIM ** -0.5),)


def _fn_post_k(c):
    s = c * _sigmoid(c)
    return (s * lax.rsqrt(_head_sum(s * s) + EPS),)


def _fn_post_v(c):
    return (c * _sigmoid(c),)


def _fn_post(cq, ck, cv):
    return _fn_post_q(cq) + _fn_post_k(ck) + _fn_post_v(cv)


def _fn_outnorm(o, gate, og):
    y = o * lax.rsqrt(_head_sum(o * o) * (1.0 / HEAD_DIM) + EPS) * og
    return (y * (gate * _sigmoid(gate)),)


def _fn_res_norm(x, m, gp, gn):
    x1 = x + _rms(m, gp)
    return x1, _rms(x1, gn)


def _fn_res_norm2(x, m, gp, ga, gb):
    x1 = x + _rms(m, gp)
    return x1, _rms(x1, ga), _rms(x1, gb)


def _relu2_of(u):
    r = jnp.maximum(u, 0.0)
    return (r * r,)


def _relu2_cotangent(da, a):
    return (da * (2.0 * jnp.sqrt(a.astype(F32))),)


def _loss_call(x3, d1, tgt, g, tm=256):
    t, d = x3.shape
    tm = min(tm, t)

    def body(x_ref, d_ref, t_ref, g_ref, loss_ref, dx_ref, dd_ref, dg_ref):
        i = pl.program_id(0)
        y, vjp = jax.vjp(lambda x, dd, gg: x + _rms(dd, gg), x_ref[...], d_ref[...], g_ref[...])
        err = y - t_ref[...]
        lrow = 0.5 * jnp.mean(err * err, axis=-1, keepdims=True)
        dx, dd, dg = vjp(err * (1.0 / d))
        dx_ref[...] = dx
        dd_ref[...] = dd.astype(dd_ref.dtype)

        @pl.when(i == 0)
        def _():
            loss_ref[...] = jnp.zeros_like(loss_ref)
            dg_ref[...] = jnp.zeros_like(dg_ref)

        loss_ref[...] += jnp.broadcast_to(jnp.sum(lrow, axis=0, keepdims=True), loss_ref.shape)
        dg_ref[...] += dg

    row = pl.BlockSpec((tm, d), lambda i: (i, 0))
    return pl.pallas_call(
        body,
        name="loss_head",
        grid=(t // tm,),
        in_specs=[row, row, row, _full_spec(g)],
        out_specs=[pl.BlockSpec((8, LANES), lambda i: (0, 0)), row, row, _full_spec(g)],
        out_shape=[jax.ShapeDtypeStruct((8, LANES), F32), jax.ShapeDtypeStruct((t, d), F32), jax.ShapeDtypeStruct((t, d), BF16), jax.ShapeDtypeStruct(g.shape, F32)],
        compiler_params=_params("arbitrary"),
    )(x3, d1, tgt, g)


HALO = 8


def _conv_fwd(qkvg, conv_w, shards, tm=256):
    t = qkvg.shape[0]
    tm = min(tm, t)
    steps = t // tm
    wide = QKV // 3
    n = len(shards)

    def body(*refs):
        cur_ref, prev_ref, w_ref = refs[:3]
        shard_refs = refs[3:3 + n]
        o_ref, q_ref, k_ref, v_ref = refs[3 + n:7 + n]
        all_refs = refs[7 + n:7 + 2 * n]
        buf, sems = refs[7 + 2 * n], refs[8 + 2 * n:]
        i = pl.program_id(0)

        if n:
            @pl.when(i == 0)
            def _():
                for cp in _gather_sends(shard_refs, all_refs, *sems[:2]):
                    cp.start()

        buf[0:HALO, :] = jnp.where(i > 0, prev_ref[...], 0.0)
        buf[HALO:, :] = cur_ref[...]
        acc = buf[pl.ds(HALO - CONV_K + 1, tm), :] * w_ref[pl.ds(0, 1), :]
        for j in range(1, CONV_K):
            acc = acc + buf[pl.ds(HALO - CONV_K + 1 + j, tm), :] * w_ref[pl.ds(j, 1), :]
        o_ref[...] = acc
        (q_ref[...], k_ref[...], v_ref[...]) = _fn_post(acc[:, 0:wide], acc[:, wide:2 * wide], acc[:, 2 * wide:])

        if n:
            @pl.when(i == steps - 1)
            def _():
                _gather_finish(shard_refs, all_refs, *sems)

    part = pl.BlockSpec((tm, wide), lambda i: (i, 0))
    res = pl.pallas_call(
        body,
        name="conv_fwd",
        grid=(steps,),
        in_specs=[
            pl.BlockSpec((tm, QKV), lambda i: (i, 0)),
            pl.BlockSpec((HALO, QKV), lambda i: (jnp.maximum(i * (tm // HALO) - 1, 0), 0)),
            pl.BlockSpec((CONV_K, QKV), lambda i: (0, 0)),
        ] + [_HBM] * n,
        out_specs=[pl.BlockSpec((tm, QKV), lambda i: (i, 0)), part, part, part] + [_HBM] * n,
        out_shape=[jax.ShapeDtypeStruct((t, QKV), F32)] + [jax.ShapeDtypeStruct((t, wide), F32)] * 3
        + [jax.ShapeDtypeStruct((N_CHIPS,) + s.shape, s.dtype) for s in shards],
        scratch_shapes=[pltpu.VMEM((tm + HALO, QKV), F32)] + [pltpu.SemaphoreType.DMA((3 * n,))] * (4 if n else 0),
        compiler_params=_params("arbitrary"),
    )(qkvg, qkvg, conv_w, *shards)
    return res[:4], res[4:]


def _conv_bwd(dc, dgate, qkvg, conv_w, tm=256):
    t = dc.shape[0]
    tm = min(tm, t)
    n = t // tm
    wg = dgate.shape[1]

    def body(dc_ref, dcn_ref, dgate_ref, x_ref, xp_ref, w_ref, dx_ref, dw_ref, bufd, bufx):
        i = pl.program_id(0)
        bufd[0:tm, :] = dc_ref[...]
        bufd[tm:, :] = jnp.where(i < n - 1, dcn_ref[...], 0.0)
        bufx[0:HALO, :] = jnp.where(i > 0, xp_ref[...], 0.0)
        bufx[HALO:, :] = x_ref[...]

        @pl.when(i == 0)
        def _():
            dw_ref[...] = jnp.zeros_like(dw_ref)

        dcv = dc_ref[...]
        acc = bufd[pl.ds(CONV_K - 1, tm), :] * w_ref[pl.ds(0, 1), :]
        for j in range(1, CONV_K):
            acc = acc + bufd[pl.ds(CONV_K - 1 - j, tm), :] * w_ref[pl.ds(j, 1), :]
        dx_ref[:, 0:QKV] = acc.astype(dx_ref.dtype)
        dx_ref[:, QKV:] = dgate_ref[...].astype(dx_ref.dtype)
        for j in range(CONV_K):
            dw_ref[pl.ds(j, 1), :] += jnp.sum(dcv * bufx[pl.ds(HALO - CONV_K + 1 + j, tm), :], axis=0, keepdims=True)

    return pl.pallas_call(
        body,
        name="conv_bwd",
        grid=(n,),
        in_specs=[
            pl.BlockSpec((tm, QKV), lambda i: (i, 0)),
            pl.BlockSpec((HALO, QKV), lambda i: (jnp.minimum((i + 1) * (tm // HALO), t // HALO - 1), 0)),
            pl.BlockSpec((tm, wg), lambda i: (i, 0)),
            pl.BlockSpec((tm, QKV), lambda i: (i, 0)),
            pl.BlockSpec((HALO, QKV), lambda i: (jnp.maximum(i * (tm // HALO) - 1, 0), 0)),
            pl.BlockSpec((CONV_K, QKV), lambda i: (0, 0)),
        ],
        out_specs=[pl.BlockSpec((tm, QKV + wg), lambda i: (i, 0)), pl.BlockSpec((HALO, QKV), lambda i: (0, 0))],
        out_shape=[jax.ShapeDtypeStruct((t, QKV + wg), BF16), jax.ShapeDtypeStruct((HALO, QKV), F32)],
        scratch_shapes=[pltpu.VMEM((tm + HALO, QKV), F32), pltpu.VMEM((tm + HALO, QKV), F32)],
        compiler_params=_params("arbitrary"),
    )(dc, dc, dgate, qkvg, qkvg, conv_w)


PREP_CHUNKS = 16
PREP_BWD_CHUNKS = 4


def _hi_lo(x):
    hi = x.astype(BF16)
    return hi, (x - hi.astype(F32)).astype(BF16)


def _mm3(a, b, dims=NN):
    (ah, al), (bh, bl) = _hi_lo(a), _hi_lo(b)
    return _dot(ah, bh, dims) + (_dot(ah, bl, dims) + _dot(al, bh, dims))


def _neumann(lowers):
    c = lowers[0].shape[0]
    eye = jnp.where(_iota((c, c), 0) == _iota((c, c), 1), 1.0, 0.0)
    ps = [-low for low in lowers]
    tmats = [eye + p for p in ps]
    for _ in range(CHUNK_SHIFT - 1):
        ps = [_mm3(p, p) for p in ps]
        tmats = [t + _mm3(t, p) for t, p in zip(tmats, ps)]
    return tuple(tmats)


def _inv_cotangents(tmats, dts):
    half = [_mm3(t, dt, TN) for t, dt in zip(tmats, dts)]
    return tuple(-_mm3(hf, t, NT) for hf, t in zip(half, tmats))


@jax.custom_vjp
def _tri_inv(lowers):
    return _neumann(lowers)


def _tri_inv_fwd(lowers):
    tmats = _neumann(lowers)
    return tmats, tmats


_tri_inv.defvjp(_tri_inv_fwd, lambda tmats, dts: (_inv_cotangents(tmats, dts),))


@jax.custom_vjp
def _tri_inv_known(lowers, tmats):
    return tmats


_tri_inv_known.defvjp(lambda lowers, tmats: (tmats, tmats),
                      lambda tmats, dts: (_inv_cotangents(tmats, dts), tuple(jnp.zeros_like(t) for t in tmats)))


def _prep_chunks(qs, ks, vs, bs, gcs, gts, gcrs, tmats=None):
    c = CHUNK
    r, col = _iota((c, c), 0), _iota((c, c), 1)
    incl, strict = r >= col, r > col
    decays = [jnp.where(incl, jnp.exp(jnp.where(incl, gc - gcr, 0.0)), 0.0) for gc, gcr in zip(gcs, gcrs)]
    kbs = [k * b for k, b in zip(ks, bs)]
    kbfs = [k.astype(BF16) for k in ks]
    lowers = tuple(jnp.where(strict, _dot(kb.astype(BF16), kbf, NT) * decay, 0.0) for kb, kbf, decay in zip(kbs, kbfs, decays))
    tmats = _tri_inv(lowers) if tmats is None else _tri_inv_known(lowers, tuple(tmats))
    outs = []
    for q, k, v, b, gc, gt, kb, kbf, decay, tmat in zip(qs, ks, vs, bs, gcs, gts, kbs, kbfs, decays, tmats):
        tb = tmat.astype(BF16)
        egc = jnp.exp(gc)
        w = _dot(tb, (kb * egc).astype(BF16))
        u = _dot(tb, (v * b).astype(BF16))
        attn = _dot(q.astype(BF16), kbf, NT) * decay
        gl = jnp.broadcast_to(jnp.exp(jnp.mean(gt.reshape(c // 8, 8, 1), axis=0)), (8, HEAD_DIM))
        outs.append((w, u, q * egc, k * jnp.exp(gt - gc), attn, gl))
    return tuple(outs), tmats


def _prep_specs(rows, gch):
    head = pl.BlockSpec((rows, HEAD_DIM), lambda n, h: (n, h))
    gates = pl.BlockSpec((rows, LANES), lambda n, h: (n, 0))
    gcrow = pl.BlockSpec((1, gch, 1, CHUNK), lambda n, h: (h, n, 0, 0))
    square = pl.BlockSpec((1, rows, CHUNK), lambda n, h: (h, n, 0))
    gl = pl.BlockSpec((1, gch * 8, HEAD_DIM), lambda n, h: (h, n, 0))
    return head, gates, gcrow, square, gl


def _pick_lane(ref, sl, lane):
    return jnp.sum(jnp.where(_iota((1, LANES), 1) == lane, ref[sl, :], 0.0), axis=1, keepdims=True)


def _prep_inputs(q_ref, k_ref, v_ref, b_ref, gc_ref, gt_ref, gcr_ref, sls, h):
    return ([q_ref[sl, :] for sl in sls], [k_ref[sl, :] for sl in sls], [v_ref[sl, :] for sl in sls],
            [_pick_lane(b_ref, sl, h) for sl in sls], [_pick_lane(gc_ref, sl, h + HEADS) for sl in sls],
            [_pick_lane(gt_ref, sl, h + HEADS) for sl in sls], [gcr_ref[0, c] for c in range(len(sls))])


def _gdn_prep(q, k, v, beta, gc, gt, gcr):
    t = q.shape[0]
    gch = min(PREP_CHUNKS, t // CHUNK)
    rows = gch * CHUNK

    def body(q_ref, k_ref, v_ref, b_ref, gc_ref, gt_ref, gcr_ref, w_ref, u_ref, qg_ref, kg_ref, at_ref, gl_ref, tm_ref):
        h = pl.program_id(1)
        sls = [pl.ds(c * CHUNK, CHUNK) for c in range(gch)]
        outs, tmats = _prep_chunks(*_prep_inputs(q_ref, k_ref, v_ref, b_ref, gc_ref, gt_ref, gcr_ref, sls, h))
        for c, (sl, (w, u, qg, kg, attn, gl), tmat) in enumerate(zip(sls, outs, tmats)):
            w_ref[sl, :] = w.astype(BF16)
            u_ref[sl, :] = u
            qg_ref[sl, :] = qg.astype(BF16)
            kg_ref[sl, :] = kg.astype(BF16)
            at_ref[0, sl, :] = attn.astype(BF16)
            gl_ref[0, pl.ds(c * 8, 8), :] = gl
            tm_ref[0, sl, :] = tmat

    hb, col, gcrow, square, glb = _prep_specs(rows, gch)
    wide = HEADS * HEAD_DIM
    return pl.pallas_call(
        body,
        name="gdn_prep",
        grid=(t // rows, HEADS),
        in_specs=[hb, hb, hb, col, col, col, gcrow],
        out_specs=[hb, hb, hb, hb, square, glb, square],
        out_shape=[
            jax.ShapeDtypeStruct((t, wide), BF16),
            jax.ShapeDtypeStruct((t, wide), F32),
            jax.ShapeDtypeStruct((t, wide), BF16),
            jax.ShapeDtypeStruct((t, wide), BF16),
            jax.ShapeDtypeStruct((HEADS, t, CHUNK), BF16),
            jax.ShapeDtypeStruct((HEADS, t // CHUNK * 8, HEAD_DIM), F32),
            jax.ShapeDtypeStruct((HEADS, t, CHUNK), F32),
        ],
        compiler_params=_params("parallel", "parallel"),
    )(q, k, v, beta, gc, gt, gcr)


def _gdn_prep_bwd(q, k, v, beta, gc, gt, gcr, tmat, dw, du, dqg, dkg, dattn, dgl):
    t = q.shape[0]
    gch = min(PREP_BWD_CHUNKS, t // CHUNK)
    rows = gch * CHUNK

    def body(q_ref, k_ref, v_ref, b_ref, gc_ref, gt_ref, gcr_ref, tm_ref, dw_ref, du_ref, dqg_ref, dkg_ref, dat_ref, dgl_ref,
             dq_ref, dk_ref, dv_ref, db_ref, dgc_ref, dgt_ref, dgcr_ref):
        h = pl.program_id(1)
        lane = _iota((1, LANES), 1)

        @pl.when(h == 0)
        def _():
            db_ref[...] = jnp.zeros_like(db_ref)
            dgc_ref[...] = jnp.zeros_like(dgc_ref)
            dgt_ref[...] = jnp.zeros_like(dgt_ref)

        sls = [pl.ds(c * CHUNK, CHUNK) for c in range(gch)]
        known = [tm_ref[0, sl, :] for sl in sls]
        _, vjp = jax.vjp(lambda *a: _prep_chunks(*a, tmats=known)[0], *_prep_inputs(q_ref, k_ref, v_ref, b_ref, gc_ref, gt_ref, gcr_ref, sls, h))
        cots = tuple((dw_ref[sl, :], du_ref[sl, :], dqg_ref[sl, :], dkg_ref[sl, :], dat_ref[0, sl, :], dgl_ref[0, pl.ds(c * 8, 8), :]) for c, sl in enumerate(sls))
        dqs, dks, dvs, dbs, dgcs, dgts, dgcrs = vjp(cots)
        for c, sl in enumerate(sls):
            dq_ref[sl, :] = dqs[c]
            dk_ref[sl, :] = dks[c]
            dv_ref[sl, :] = dvs[c]
            db_ref[sl, :] += jnp.where(lane == h, dbs[c], 0.0)
            dgc_ref[sl, :] += jnp.where(lane == h + HEADS, dgcs[c], 0.0)
            dgt_ref[sl, :] += jnp.where(lane == h + HEADS, dgts[c], 0.0)
            dgcr_ref[0, c] = dgcrs[c]

    hb, col, gcrow, square, glb = _prep_specs(rows, gch)
    wide = HEADS * HEAD_DIM
    return pl.pallas_call(
        body,
        name="gdn_prep_bwd",
        grid=(t // rows, HEADS),
        in_specs=[hb, hb, hb, col, col, col, gcrow, square, hb, hb, hb, hb, square, glb],
        out_specs=[hb, hb, hb, col, col, col, gcrow],
        out_shape=[jax.ShapeDtypeStruct((t, wide), F32)] * 3 + [jax.ShapeDtypeStruct((t, LANES), F32)] * 3 + [jax.ShapeDtypeStruct((HEADS, t // CHUNK, 1, CHUNK), F32)],
        compiler_params=_params("parallel", "arbitrary"),
    )(q, k, v, beta, gc, gt, gcr, tmat, dw, du, dqg, dkg, dattn, dgl)


def _gdn_scan(w, u, qg, kg, attn, gl):
    t = w.shape[0]
    n = t // CHUNK
    wide = HEADS * HEAD_DIM

    def body(w_ref, u_ref, qg_ref, kg_ref, at_ref, gl_ref, o_ref, st_ref, s_ref):
        @pl.when(pl.program_id(0) == 0)
        def _():
            s_ref[...] = jnp.zeros_like(s_ref)

        heads = range(HEADS)
        cols = [pl.ds(h * HEAD_DIM, HEAD_DIM) for h in heads]
        ss = [s_ref[h] for h in heads]
        sbs = [s.astype(BF16) for s in ss]
        vbs = [(u_ref[:, hs] - _dot(w_ref[:, hs], sb)).astype(BF16) for hs, sb in zip(cols, sbs)]
        outs = [_dot(qg_ref[:, hs], sb) + _dot(at_ref[h], vb) for h, hs, sb, vb in zip(heads, cols, sbs, vbs)]
        new = [s * jnp.tile(gl_ref[h], (HEAD_DIM // 8, 1)) + _dot(kg_ref[:, hs], vb, TN) for h, hs, s, vb in zip(heads, cols, ss, vbs)]
        for h, hs in zip(heads, cols):
            st_ref[0, h] = ss[h]
            o_ref[:, hs] = outs[h]
            s_ref[h] = new[h]

    row = pl.BlockSpec((CHUNK, wide), lambda i: (i, 0))
    return pl.pallas_call(
        body,
        name="gdn_scan",
        grid=(n,),
        in_specs=[row, row, row, row, pl.BlockSpec((HEADS, CHUNK, CHUNK), lambda i: (0, i, 0)), pl.BlockSpec((HEADS, 8, HEAD_DIM), lambda i: (0, i, 0))],
        out_specs=[row, pl.BlockSpec((1, HEADS, HEAD_DIM, HEAD_DIM), lambda i: (i, 0, 0, 0))],
        out_shape=[jax.ShapeDtypeStruct((t, wide), F32), jax.ShapeDtypeStruct((n, HEADS, HEAD_DIM, HEAD_DIM), F32)],
        scratch_shapes=[pltpu.VMEM((HEADS, HEAD_DIM, HEAD_DIM), F32)],
        compiler_params=_params("arbitrary"),
    )(w, u, qg, kg, attn, gl)


def _gdn_scan_bwd(w, u, qg, kg, attn, gl, states, do):
    t = w.shape[0]
    n = t // CHUNK
    wide = HEADS * HEAD_DIM

    def body(w_ref, u_ref, qg_ref, kg_ref, at_ref, gl_ref, st_ref, do_ref, dw_ref, du_ref, dqg_ref, dkg_ref, dat_ref, dgl_ref, ds_ref):
        @pl.when(pl.program_id(0) == 0)
        def _():
            ds_ref[...] = jnp.zeros_like(ds_ref)

        heads = range(HEADS)
        cols = [pl.ds(h * HEAD_DIM, HEAD_DIM) for h in heads]
        ss = [st_ref[0, h] for h in heads]
        sbs = [s.astype(BF16) for s in ss]
        dsns = [ds_ref[h] for h in heads]
        dsbs = [d.astype(BF16) for d in dsns]
        dobs = [do_ref[:, hs].astype(BF16) for hs in cols]
        vbs = [(u_ref[:, hs] - _dot(w_ref[:, hs], sb)).astype(BF16) for hs, sb in zip(cols, sbs)]
        dvns = [_dot(at_ref[h], dob, TN) + _dot(kg_ref[:, hs], dsb) for h, hs, dob, dsb in zip(heads, cols, dobs, dsbs)]
        dvbs = [d.astype(BF16) for d in dvns]
        for h, hs in zip(heads, cols):
            dat_ref[h] = _dot(dobs[h], vbs[h], NT)
            dqg_ref[:, hs] = _dot(dobs[h], sbs[h], NT)
            dkg_ref[:, hs] = _dot(vbs[h], dsbs[h], NT)
            du_ref[:, hs] = dvns[h]
            dw_ref[:, hs] = -_dot(dvbs[h], sbs[h], NT)
            dgl_ref[h] = jnp.sum((dsns[h] * ss[h]).reshape(HEAD_DIM // 8, 8, HEAD_DIM), axis=0)
        new = [dsn * jnp.tile(gl_ref[h], (HEAD_DIM // 8, 1)) + _dot(qg_ref[:, hs], dob, TN) - _dot(w_ref[:, hs], dvb, TN)
               for h, hs, dsn, dob, dvb in zip(heads, cols, dsns, dobs, dvbs)]
        for h in heads:
            ds_ref[h] = new[h]

    row = pl.BlockSpec((CHUNK, wide), lambda i: (n - 1 - i, 0))
    at = pl.BlockSpec((HEADS, CHUNK, CHUNK), lambda i: (0, n - 1 - i, 0))
    glb = pl.BlockSpec((HEADS, 8, HEAD_DIM), lambda i: (0, n - 1 - i, 0))
    return pl.pallas_call(
        body,
        name="gdn_scan_bwd",
        grid=(n,),
        in_specs=[row, row, row, row, at, glb, pl.BlockSpec((1, HEADS, HEAD_DIM, HEAD_DIM), lambda i: (n - 1 - i, 0, 0, 0)), row],
        out_specs=[row, row, row, row, at, glb],
        out_shape=[jax.ShapeDtypeStruct((t, wide), F32)] * 4 + [jax.ShapeDtypeStruct((HEADS, t, CHUNK), F32), jax.ShapeDtypeStruct((HEADS, n * 8, HEAD_DIM), F32)],
        scratch_shapes=[pltpu.VMEM((HEADS, HEAD_DIM, HEAD_DIM), F32)],
        compiler_params=_params("arbitrary"),
    )(w, u, qg, kg, attn, gl, states, do)


SB_Q = 512
SB_K = 256
SB_STEP = 1
SB_DEAD = -105.0


def _sb_scores(q, k):
    z = _dot(q, k, NT) * (HEAD_DIM ** -0.5)
    e = jnp.exp(-jnp.abs(z))
    lb = jnp.minimum(z, 0.0) - jnp.log(1.0 + e)
    return z, e, lb, lb - z


def _tri(n, rel):
    return jnp.where(rel(_iota((n, n), 0), _iota((n, n), 1)), 1.0, 0.0).astype(BF16)


def _lanes(col):
    return jnp.broadcast_to(col, (col.shape[0], LANES))


def _sb_fwd(q, k, v):
    t = q.shape[0]
    bq, bk = min(SB_Q, t), min(SB_K, t)
    nsub, rep = bq // bk, bk // LANES
    nstep = min(SB_STEP, nsub)
    steps_per_tile = nsub // nstep

    def body(q_ref, k_ref, v_ref, o_ref, rt_ref, first_ref):
        h = pl.program_id(0)
        i = pl.program_id(1)
        o_ref[...] = jnp.zeros_like(o_ref)
        rt_ref[...] = jnp.zeros_like(rt_ref)
        after = _tri(bk, lambda r, c: r > c)

        def block(j, r0, diag):
            st = pl.multiple_of(j * bk, bk)
            kv, vv = k_ref[pl.ds(st, bk), :], v_ref[pl.ds(st, bk), :]
            _, _, lb, l1m = _sb_scores(q_ref[r0:, :], kv)
            if diag:
                mask = _iota((bq - r0, bk), 1) + j * bk < _iota((bq - r0, bk), 0) + (r0 + i * bq)
                l1m = jnp.where(mask, l1m, 0.0)
            sums = _two_pass(l1m, after)
            run = rt_ref[r0:, :]
            a = jnp.exp(lb + jnp.tile(run, (1, rep)) + sums)
            if diag:
                a = jnp.where(mask, a, 0.0)
            o_ref[r0:, :] += _dot(a.astype(BF16), vv)
            rt_ref[r0:, :] = run + _lanes(sums[:, 0:1] + l1m[:, 0:1])

        for s in reversed(range(nsub)):
            block(i * nsub + s, s * bk, True)

        def alive(carry):
            u, highest = carry
            return jnp.logical_and(u >= 0, highest > SB_DEAD)

        def step(carry):
            u, _ = carry
            for s in reversed(range(nstep)):
                block(u * nstep + s, 0, False)
            return u - 1, jnp.max(rt_ref[...])

        u_end, _ = lax.while_loop(alive, step, (i * steps_per_tile - 1, jnp.max(rt_ref[...])))
        first_ref[h, i] = u_end + 1

    qb = pl.BlockSpec((bq, HEAD_DIM), lambda h, i: (i, h))
    full = pl.BlockSpec((t, HEAD_DIM), lambda h, i: (0, h))
    return pl.pallas_call(
        body,
        name="sb_fwd",
        grid=(HEADS, t // bq),
        in_specs=[qb, full, full],
        out_specs=[qb, qb, pl.BlockSpec(memory_space=pltpu.SMEM)],
        out_shape=[jax.ShapeDtypeStruct(q.shape, F32), jax.ShapeDtypeStruct(q.shape, F32), jax.ShapeDtypeStruct((HEADS, t // bq), jnp.int32)],
        compiler_params=_params("arbitrary", "arbitrary"),
    )(q, k, v)


def _sb_bwd(q, k, v, rt, first, do):
    t = q.shape[0]
    bq, bk = min(SB_Q, t), min(SB_K, t)
    nsub, rep = bq // bk, bk // LANES
    nstep = min(SB_STEP, nsub)
    steps_per_tile = nsub // nstep
    scale = HEAD_DIM ** -0.5

    def body(first_ref, q_ref, k_ref, v_ref, rt_ref, do_ref, dq_ref, dk_ref, dv_ref, left_ref, pg_ref):
        h = pl.program_id(0)
        i = pl.program_id(1)

        @pl.when(i == 0)
        def _():
            dk_ref[...] = jnp.zeros_like(dk_ref)
            dv_ref[...] = jnp.zeros_like(dv_ref)

        dq_ref[...] = jnp.zeros_like(dq_ref)
        left_ref[...] = jnp.zeros_like(left_ref)
        pg_ref[...] = jnp.zeros_like(pg_ref)
        upto = _tri(bk, lambda r, c: r <= c)

        def block(j, r0, diag):
            st = pl.multiple_of(j * bk, bk)
            kv, vv = k_ref[pl.ds(st, bk), :], v_ref[pl.ds(st, bk), :]
            qv = q_ref[r0:, :]
            dob = do_ref[r0:, :].astype(BF16)
            _, _, lb, l1m = _sb_scores(qv, kv)
            if diag:
                mask = _iota((bq - r0, bk), 1) + j * bk < _iota((bq - r0, bk), 0) + (r0 + i * bq)
                l1m = jnp.where(mask, l1m, 0.0)
            sums = _two_pass(l1m, upto)
            left = left_ref[r0:, :]
            a = jnp.exp(lb + jnp.tile(rt_ref[r0:, :] - left, (1, rep)) - sums)
            if diag:
                a = jnp.where(mask, a, 0.0)
            g = _dot(dob, vv, NT) * a
            dv_ref[pl.ds(st, bk), :] += _dot(a.astype(BF16), dob, TN)
            gsum = _two_pass(g, upto)
            pg = pg_ref[r0:, :]
            dz = g - jnp.exp(lb) * (jnp.tile(pg, (1, rep)) + gsum)
            if diag:
                dz = jnp.where(mask, dz, 0.0)
            dzb = (dz * scale).astype(BF16)
            dk_ref[pl.ds(st, bk), :] += _dot(dzb, qv, TN)
            dq_ref[r0:, :] += _dot(dzb, kv)
            left_ref[r0:, :] = left + _lanes(sums[:, bk - 1:bk])
            pg_ref[r0:, :] = pg + _lanes(gsum[:, bk - 1:bk])

        def step(u, carry):
            for s in range(nstep):
                block(u * nstep + s, 0, False)
            return carry

        lax.fori_loop(first_ref[h, i], i * steps_per_tile, step, 0)
        for s in range(nsub):
            block(i * nsub + s, s * bk, True)

    qb = pl.BlockSpec((bq, HEAD_DIM), lambda h, i: (i, h))
    full = pl.BlockSpec((t, HEAD_DIM), lambda h, i: (0, h))
    return pl.pallas_call(
        body,
        name="sb_bwd",
        grid=(HEADS, t // bq),
        in_specs=[pl.BlockSpec(memory_space=pltpu.SMEM), qb, full, full, qb, qb],
        out_specs=[qb, full, full],
        out_shape=[jax.ShapeDtypeStruct(q.shape, F32)] * 3,
        scratch_shapes=[pltpu.VMEM((bq, LANES), F32), pltpu.VMEM((bq, LANES), F32)],
        compiler_params=_params("arbitrary", "arbitrary"),
    )(first, q, k, v, rt, do)


def _adamw(w, g, m, v, name, tm=256):
    r, c = w.shape
    tm = tm if r % tm == 0 else r

    def body(w_ref, g_ref, m_ref, v_ref, d_ref, nm_ref, nv_ref):
        gv = g_ref[...]
        nm = ADAM_B1 * m_ref[...] + (1.0 - ADAM_B1) * gv
        nv = ADAM_B2 * v_ref[...] + (1.0 - ADAM_B2) * (gv * gv)
        m_hat = nm / (1.0 - ADAM_B1 ** ADAM_STEP)
        v_hat = nv / (1.0 - ADAM_B2 ** ADAM_STEP)
        d_ref[...] = -ADAM_LR * (m_hat / (jnp.sqrt(v_hat) + ADAM_EPS) + ADAM_WD * w_ref[...])
        nm_ref[...] = nm
        nv_ref[...] = nv

    blk = pl.BlockSpec((tm, c), lambda i: (i, 0))
    return pl.pallas_call(
        body,
        name=name,
        grid=(r // tm,),
        in_specs=[blk] * 4,
        out_specs=[blk] * 3,
        out_shape=[jax.ShapeDtypeStruct((r, c), F32)] * 3,
        compiler_params=_params("parallel"),
    )(w, g, m, v)


def _local_step(x, tgt, gains, w_in, small, shards, assemble):
    mix_pre, mix_post, mlp_pre, mlp_post, kv_gain = gains
    w_qkvg, w_ba = w_in
    conv_w, a_log, dt_bias, out_gain = small
    t, d = x.shape
    row = lambda a, i=None: a[i:i + 1] if i is not None else a
    al = jnp.zeros((1, LANES), F32).at[:, HEADS:2 * HEADS].set(a_log)
    dtb = jnp.zeros((1, LANES), F32).at[:, HEADS:2 * HEADS].set(dt_bias)
    og = jnp.tile(out_gain, (1, HEADS))
    full = lambda a: (a, a.shape[1], 0)

    (h0,) = _rowwise("norm_in", _fn_norm, [full(x)], [row(mix_pre, 0)], [(d, BF16)])
    qkvg = _matmul(h0, w_qkvg, "nn", F32, "mm_gdn_in", tk=1024)
    ba = _matmul(h0, w_ba, "nn", F32, "mm_gdn_ba", tk=1024)
    (conv, gq, gk, gv), gathered = _conv_fwd(qkvg, conv_w, shards)
    w_out, w_kv, w_q, w_o, w_up, w_down = assemble(gathered)
    conv_qkv = [(conv, d, 0), (conv, d, 1), (conv, d, 2)]
    beta, gc, gt = _rowwise("gates", _fn_gates, [full(ba)], [al, dtb], [(LANES, F32)] * 3)
    gcr = jnp.swapaxes(gc[:, HEADS:2 * HEADS], 0, 1).reshape(HEADS, t // CHUNK, 1, CHUNK)
    pw, pu, pqg, pkg, pattn, pgl, ptm = _gdn_prep(gq, gk, gv, beta, gc, gt, gcr)
    o_gdn, states = _gdn_scan(pw, pu, pqg, pkg, pattn, pgl)
    (on,) = _rowwise("out_norm", _fn_outnorm, [full(o_gdn), (qkvg, d, 3)], [og], [(d, BF16)])
    mix0 = _matmul(on, w_out, "nn", F32, "mm_gdn_out", tk=1024)
    x1, h1 = _rowwise("res_a0", _fn_res_norm, [full(x), full(mix0)], [row(mix_post, 0), row(mlp_pre, 0)], [(d, F32), (d, BF16)])
    (a0,) = _matmul(h1, w_up[0], "nn", (BF16,), "mm_up0", tk=1024, epilogue=_relu2_of)
    d0 = _matmul(a0, w_down[0], "nn", F32, "mm_down0")
    x2, hkv, hq = _rowwise("res_b0", _fn_res_norm2, [full(x1), full(d0)], [row(mlp_post, 0), kv_gain, row(mix_pre, 1)], [(d, F32), (d, BF16), (d, BF16)])
    w_k, w_v = w_kv[:, :d], w_kv[:, d:]
    kp = _matmul(hkv, w_k, "nn", BF16, "mm_k", tk=1024)
    vp = _matmul(hkv, w_v, "nn", BF16, "mm_v", tk=1024)
    qp = _matmul(hq, w_q, "nn", BF16, "mm_q", tk=1024)
    o_sb, rt, sb_first = _sb_fwd(qp, kp, vp)
    mix1 = _matmul(o_sb, w_o, "nn", F32, "mm_sb_out", tk=1024)
    x3, h3 = _rowwise("res_a1", _fn_res_norm, [full(x2), full(mix1)], [row(mix_post, 1), row(mlp_pre, 1)], [(d, F32), (d, BF16)])
    (a1,) = _matmul(h3, w_up[1], "nn", (BF16,), "mm_up1", tk=1024, epilogue=_relu2_of)
    d1 = _matmul(a1, w_down[1], "nn", F32, "mm_down1")

    loss, dx3, dd1, g_mlp_post1 = _loss_call(x3, d1, tgt, row(mlp_post, 1))
    (du1,) = _matmul(dd1, w_down[1], "nt", (BF16,), "mm_down1_dx", tk=1024, epilogue=_relu2_cotangent, extras=[a1])
    g_down1 = _matmul(a1, dd1, "tn", F32, "mm_down1_dw")
    dh3 = _matmul(du1, w_up[1], "nt", F32, "mm_up1_dx")
    g_up1 = _matmul(h3, du1, "tn", F32, "mm_up1_dw")
    (dx2, dmix1), (g_mix_post1, g_mlp_pre1) = _rowwise_bwd(
        "res_a1_bwd", _fn_res_norm, [full(x2), full(mix1)], [row(mix_post, 1), row(mlp_pre, 1)], [dx3, dh3], [F32, BF16])
    do_sb = _matmul(dmix1, w_o, "nt", BF16, "mm_sb_out_dx")
    g_o = _matmul(o_sb, dmix1, "tn", F32, "mm_sb_out_dw")
    dqp, dkp, dvp = _sb_bwd(qp, kp, vp, rt, sb_first, do_sb)
    dhq = _matmul(dqp, w_q, "nt", F32, "mm_q_dx")
    g_q = _matmul(hq, dqp, "tn", F32, "mm_q_dw")
    dhkv = _matmul(dvp, w_v, "nt", F32, "mm_v_dx", add=_matmul(dkp, w_k, "nt", F32, "mm_k_dx"))
    g_kv = jnp.concatenate([_matmul(hkv, dkp, "tn", F32, "mm_k_dw"), _matmul(hkv, dvp, "tn", F32, "mm_v_dw")], axis=1)
    (dx1, dd0), (g_mlp_post0, g_kv_gain, g_mix_pre1) = _rowwise_bwd(
        "res_b0_bwd", _fn_res_norm2, [full(x1), full(d0)], [row(mlp_post, 0), kv_gain, row(mix_pre, 1)], [dx2, dhkv, dhq], [F32, BF16])
    (du0,) = _matmul(dd0, w_down[0], "nt", (BF16,), "mm_down0_dx", tk=1024, epilogue=_relu2_cotangent, extras=[a0])
    g_down0 = _matmul(a0, dd0, "tn", F32, "mm_down0_dw")
    dh1 = _matmul(du0, w_up[0], "nt", F32, "mm_up0_dx")
    g_up0 = _matmul(h1, du0, "tn", F32, "mm_up0_dw")
    (dx0, dmix0), (g_mix_post0, g_mlp_pre0) = _rowwise_bwd(
        "res_a0_bwd", _fn_res_norm, [full(x), full(mix0)], [row(mix_post, 0), row(mlp_pre, 0)], [dx1, dh1], [F32, BF16])
    don = _matmul(dmix0, w_out, "nt", F32, "mm_gdn_out_dx")
    g_out = _matmul(on, dmix0, "tn", F32, "mm_gdn_out_dw")
    (do_gdn, dgate), (g_og,) = _rowwise_bwd("out_norm_bwd", _fn_outnorm, [full(o_gdn), (qkvg, d, 3)], [og], [don], [F32, F32])
    dpw, dpu, dpqg, dpkg, dpattn, dpgl = _gdn_scan_bwd(pw, pu, pqg, pkg, pattn, pgl, states, do_gdn)
    dgq, dgk, dgv, dbeta, dgc, dgt, dgcr = _gdn_prep_bwd(gq, gk, gv, beta, gc, gt, gcr, ptm, dpw, dpu, dpqg, dpkg, dpattn, dpgl)
    dgcr_lanes = jnp.pad(jnp.swapaxes(dgcr.reshape(HEADS, t), 0, 1), ((0, 0), (HEADS, LANES - 2 * HEADS)))
    gate_cots = [dbeta, dgc + dgcr_lanes, dgt]
    (dba,), (g_al, g_dtb) = _rowwise_bwd("gates_bwd", _fn_gates, [full(ba)], [al, dtb], gate_cots, [BF16])
    (dconv,), _ = _rowwise_bwd("post_conv_bwd", _fn_post, conv_qkv, [], [dgq, dgk, dgv], [F32] * 3, joined=True)
    dqkvg, g_conv = _conv_bwd(dconv, dgate, qkvg, conv_w)
    dh0b = _matmul(dba, w_ba, "nt", F32, "mm_gdn_ba_dx", tk=LANES)
    dh0 = _matmul(dqkvg, w_qkvg, "nt", F32, "mm_gdn_in_dx", add=dh0b)
    g_qkvg = _matmul(h0, dqkvg, "tn", F32, "mm_gdn_in_dw")
    g_ba = _matmul(h0, dba, "tn", F32, "mm_gdn_ba_dw")
    (grad_x,), (g_mix_pre0,) = _rowwise_bwd("norm_in_bwd", lambda xx, gg: (_rms(xx, gg), xx), [full(x)], [row(mix_pre, 0)], [dh0, dx0], [F32])

    grads = dict(
        mix_pre_gain=jnp.concatenate([g_mix_pre0, g_mix_pre1], axis=0),
        mix_post_gain=jnp.concatenate([g_mix_post0, g_mix_post1], axis=0),
        mlp_pre_gain=jnp.concatenate([g_mlp_pre0, g_mlp_pre1], axis=0),
        mlp_post_gain=jnp.concatenate([g_mlp_post0, g_mlp_post1], axis=0),
        mlp_w_up=(g_up0, g_up1),
        mlp_w_down=(g_down0, g_down1),
        gdn_w_in=jnp.concatenate([g_qkvg, g_ba[:, :2 * HEADS]], axis=1)[None],
        gdn_conv_w=g_conv[None, :CONV_K],
        gdn_a_log=g_al[:, HEADS:2 * HEADS],
        gdn_dt_bias=g_dtb[:, HEADS:2 * HEADS],
        gdn_out_gain=jnp.sum(g_og.reshape(HEADS, HEAD_DIM), axis=0, keepdims=True),
        gdn_w_out=g_out[None],
        kv_gain=g_kv_gain[0],
        w_kv=g_kv,
        sb_w_q=g_q[None],
        sb_w_o=g_o[None],
    )
    return loss, grad_x, grads


N_DEV = 8
N_CHIPS = 4
PACK_ROW_TILE = 128

_HBM = pl.BlockSpec(memory_space=pltpu.HBM)


def _place():
    return lax.axis_index("x"), lax.axis_index("y"), lax.axis_index("c")


def _other_chips(x, y):
    return [(1 - x, y), (x, 1 - y), (1 - x, 1 - y)]


def _remote(src, dst, send_sem, recv_sem, to):
    return pltpu.make_async_remote_copy(src_ref=src, dst_ref=dst, send_sem=send_sem, recv_sem=recv_sem, device_id=to, device_id_type=MESH)


def _gather8(v, name):
    rows, cols = v.shape

    def body(v_ref, out_ref, sum_ref, send_sems, recv_sems, local_sem):
        x, y, c = _place()
        me, sibling = (x, y, c), (x, y, 1 - c)
        chips = _other_chips(x, y)

        def blk(px, py, pc):
            return out_ref.at[pl.ds((4 * px + 2 * py + pc) * rows, rows), :]

        def copy(k, block, to, src=None):
            return _remote(blk(*block) if src is None else src, blk(*block), send_sems.at[k], recv_sems.at[k], to)

        mine = pltpu.make_async_copy(v_ref, blk(*me), local_sem)
        mine.start()
        first = [copy(0, me, sibling, src=v_ref)] + [copy(1 + j, me, (*chip, c), src=v_ref) for j, chip in enumerate(chips)]
        for cp in first:
            cp.start()
        passed = [copy(4 + j, (*chip, c), sibling) for j, chip in enumerate(chips)]
        for j, chip in enumerate(chips):
            copy(1 + j, (*chip, c), me).wait_recv()
            passed[j].start()
        copy(0, sibling, me).wait_recv()
        for j, chip in enumerate(chips):
            copy(4 + j, (*chip, 1 - c), me).wait_recv()
        for cp in first + passed:
            cp.wait_send()
        mine.wait()
        acc = out_ref[pl.ds(0, rows), :]
        for dev in range(1, N_DEV):
            acc = acc + out_ref[pl.ds(dev * rows, rows), :]
        sum_ref[...] = acc

    vm = pl.BlockSpec(memory_space=pltpu.VMEM)
    return pl.pallas_call(
        body,
        name=name,
        out_shape=[jax.ShapeDtypeStruct((N_DEV * rows, cols), v.dtype), jax.ShapeDtypeStruct((rows, cols), v.dtype)],
        in_specs=[vm],
        out_specs=[vm, vm],
        scratch_shapes=[pltpu.SemaphoreType.DMA((7,)), pltpu.SemaphoreType.DMA((7,)), pltpu.SemaphoreType.DMA],
    )(v)


def _hbm_call(body, name, arrs, out_shapes, sem_counts):
    n = len(arrs)

    def wrapped(*refs):
        body(refs[:n], refs[n:2 * n], *refs[2 * n:])

    return pl.pallas_call(
        wrapped,
        name=name,
        out_shape=[jax.ShapeDtypeStruct(s, a.dtype) for s, a in zip(out_shapes, arrs)],
        in_specs=[_HBM] * n,
        out_specs=[_HBM] * n,
        scratch_shapes=[pltpu.SemaphoreType.DMA((k,)) for k in sem_counts],
    )(*arrs)


def _gather_sends(w_refs, out_refs, send_sems, recv_sems):
    x, y, c = _place()
    s_me = 2 * x + y
    return [_remote(w.at[c], o.at[s_me, c], send_sems.at[3 * a + j], recv_sems.at[3 * a + j], (px, py, c))
            for a, (w, o) in enumerate(zip(w_refs, out_refs)) for j, (px, py) in enumerate(_other_chips(x, y))]


def _gather_finish(w_refs, out_refs, send_sems, recv_sems, fsend_sems, frecv_sems):
    x, y, c = _place()
    chips = _other_chips(x, y)
    passed = []
    for a, o in enumerate(out_refs):
        for j, (px, py) in enumerate(chips):
            half = o.at[2 * px + py, c]
            _remote(half, half, send_sems.at[3 * a + j], recv_sems.at[3 * a + j], (px, py, c)).wait_recv()
            fwd = _remote(half, half, fsend_sems.at[3 * a + j], frecv_sems.at[3 * a + j], (x, y, 1 - c))
            fwd.start()
            passed.append(fwd)
    for a, o in enumerate(out_refs):
        for j, (px, py) in enumerate(chips):
            half = o.at[2 * px + py, 1 - c]
            _remote(half, half, fsend_sems.at[3 * a + j], frecv_sems.at[3 * a + j], (x, y, 1 - c)).wait_recv()
    for cp in _gather_sends(w_refs, out_refs, send_sems, recv_sems) + passed:
        cp.wait_send()


def _gather_weights(arrs):
    n = len(arrs)

    def body(w_refs, out_refs, send_sems, recv_sems, fsend_sems, frecv_sems):
        for cp in _gather_sends(w_refs, out_refs, send_sems, recv_sems):
            cp.start()
        _gather_finish(w_refs, out_refs, send_sems, recv_sems, fsend_sems, frecv_sems)

    return _hbm_call(body, "gather_weights", arrs, [(N_CHIPS,) + a.shape for a in arrs], [3 * n] * 4)


def _swap_halves(arrs):
    n = len(arrs)

    def body(g_refs, a_refs, send_sems, recv_sems):
        x, y, c = _place()
        cps = [_remote(g.at[1 - c], a, send_sems.at[i], recv_sems.at[i], (x, y, 1 - c)) for i, (g, a) in enumerate(zip(g_refs, a_refs))]
        for cp in cps:
            cp.start()
        for cp in cps:
            cp.wait()

    return _hbm_call(body, "grads_to_sibling", arrs, [a.shape[1:] for a in arrs], [n, n])


def _scatter_to_chips(arrs):
    n = len(arrs)

    def body(p_refs, b_refs, send_sems, recv_sems):
        x, y, c = _place()
        cps = [_remote(p.at[2 * px + py], b.at[j], send_sems.at[3 * i + j], recv_sems.at[3 * i + j], (px, py, c))
               for i, (p, b) in enumerate(zip(p_refs, b_refs)) for j, (px, py) in enumerate(_other_chips(x, y))]
        for cp in cps:
            cp.start()
        for cp in cps:
            cp.wait()

    return _hbm_call(body, "grads_to_chips", arrs, [(3,) + a.shape[1:] for a in arrs], [3 * n, 3 * n])


def _share_halves(arrs):
    n = len(arrs)

    def body(q_refs, out_refs, send_sems, recv_sems):
        x, y, c = _place()
        cps = [_remote(q, o, send_sems.at[i], recv_sems.at[i], (x, y, 1 - c)) for i, (q, o) in enumerate(zip(q_refs, out_refs))]
        for cp in cps:
            cp.start()
        for cp in cps:
            cp.wait()

    return _hbm_call(body, "grads_share", arrs, [a.shape for a in arrs], [n, n])


_GROUPS = (
    (("mlp_w_up", (2, 1024, 1024), "cols"), ("mlp_w_down", (2, 1024, 1024), "rows"), ("gdn_w_out", (1, 256, 1024), "rows"),
     ("sb_w_q", (1, 256, 1024), "rows"), ("sb_w_o", (1, 256, 1024), "rows")),
    (("w_kv", (1024, 512), "cols"),),
    (("gdn_w_in", (1, 1024, 1028), "cols"),),
)


def _numel(shape):
    n = 1
    for s in shape:
        n *= s
    return n


def _half_rows(shape):
    return _numel(shape[:-1]) // 2


def _pack_shards(shards, dtype):
    return tuple(jnp.concatenate([shards[n].astype(dtype).reshape(2, _half_rows(shape), shape[-1]) for n, shape, _ in grp], axis=1) for grp in _GROUPS)


def _unpack_shards(bufs):
    out = {}
    for grp, buf in zip(_GROUPS, bufs):
        off = 0
        for n, shape, _ in grp:
            out[n] = buf[:, off:off + _half_rows(shape)].reshape(shape)
            off += _half_rows(shape)
    return out


def _join(stacked, how):
    nd = stacked.ndim - 1
    ax = nd - 1 if how == "cols" else nd - 2
    moved = jnp.moveaxis(stacked, 0, ax)
    shape = list(stacked.shape[1:])
    shape[ax] *= N_CHIPS
    return moved.reshape(shape)


def _split(full, shard_shape, how):
    nd = len(shard_shape)
    ax = nd - 1 if how == "cols" else nd - 2
    shape = list(shard_shape)
    shape.insert(ax, N_CHIPS)
    return jnp.moveaxis(full.reshape(shape), ax, 0)


def _unpack_full(gathered, groups):
    out = {}
    for grp, buf in zip(groups, gathered):
        off = 0
        for n, shape, how in grp:
            out[n] = _join(buf[:, :, off:off + _half_rows(shape)].reshape((N_CHIPS,) + shape), how)
            off += _half_rows(shape)
    return out


def _pack_full(full):
    bufs = []
    for grp in _GROUPS:
        parts = []
        for n, shape, how in grp:
            if isinstance(full[n], tuple):
                assert len(full[n]) == shape[0] == 2
                parts.append(jnp.stack([_split(layer, shape[1:], how) for layer in full[n]], axis=1))
            else:
                parts.append(_split(full[n], shape, how).reshape(N_CHIPS, 2, _half_rows(shape), shape[-1]))
        buf = jnp.swapaxes(jnp.concatenate(parts, axis=2), 0, 1)
        bufs.append(buf.reshape(2, -1, buf.shape[-1]))
    return tuple(bufs)


_SMALL = (
    ("mix_pre_gain", (2, 1024)),
    ("mix_post_gain", (2, 1024)),
    ("mlp_pre_gain", (2, 1024)),
    ("mlp_post_gain", (2, 1024)),
    ("kv_gain", (1024,)),
    ("gdn_out_gain", (1, 128)),
    ("gdn_a_log", (1, 8)),
    ("gdn_dt_bias", (1, 8)),
    ("gdn_conv_w", (1, 4, 3072)),
    ("loss", ()),
)


def _rows_of(shape):
    return -(-_numel(shape) // LANES)


_SMALL_ROWS = -(-sum(_rows_of(s) for _, s in _SMALL) // 8) * 8


def _pack_small(vals):
    parts = []
    for n, shape in _SMALL:
        flat = vals[n].reshape(-1)
        parts.append(jnp.pad(flat, (0, _rows_of(shape) * LANES - flat.shape[0])))
    flat = jnp.concatenate(parts)
    return jnp.pad(flat, (0, _SMALL_ROWS * LANES - flat.shape[0])).reshape(_SMALL_ROWS, LANES)


def _unpack_small(packed):
    flat = packed.reshape(-1)
    out, off = {}, 0
    for n, shape in _SMALL:
        out[n] = flat[off:off + _numel(shape)].reshape(shape)
        off += _rows_of(shape) * LANES
    return out


_WEIGHTS = ("mix_pre_gain", "mix_post_gain", "mlp_pre_gain", "mlp_post_gain", "mlp_w_up", "mlp_w_down", "gdn_w_in", "gdn_conv_w",
            "gdn_a_log", "gdn_dt_bias", "gdn_out_gain", "gdn_w_out", "kv_gain", "w_kv", "sb_w_q", "sb_w_o")


def _as2d(a):
    return a.reshape(1, -1) if a.ndim <= 1 else a.reshape(-1, a.shape[-1])


def kernel(x, mix_pre_gain, mix_post_gain, mlp_pre_gain, mlp_post_gain, mlp_w_up, mlp_w_down, gdn_w_in, gdn_conv_w, gdn_a_log, gdn_dt_bias, gdn_out_gain, gdn_w_out, kv_gain, w_kv, sb_w_q, sb_w_o, loss_target, m_mix_pre_gain, m_mix_post_gain, m_mlp_pre_gain, m_mlp_post_gain, m_mlp_w_up, m_mlp_w_down, m_gdn_w_in, m_gdn_conv_w, m_gdn_a_log, m_gdn_dt_bias, m_gdn_out_gain, m_gdn_w_out, m_kv_gain, m_w_kv, m_sb_w_q, m_sb_w_o, v_mix_pre_gain, v_mix_post_gain, v_mlp_pre_gain, v_mlp_post_gain, v_mlp_w_up, v_mlp_w_down, v_gdn_w_in, v_gdn_conv_w, v_gdn_a_log, v_gdn_dt_bias, v_gdn_out_gain, v_gdn_w_out, v_kv_gain, v_w_kv, v_sb_w_q, v_sb_w_o):
    w = dict(mix_pre_gain=mix_pre_gain, mix_post_gain=mix_post_gain, mlp_pre_gain=mlp_pre_gain, mlp_post_gain=mlp_post_gain, mlp_w_up=mlp_w_up, mlp_w_down=mlp_w_down, gdn_w_in=gdn_w_in, gdn_conv_w=gdn_conv_w, gdn_a_log=gdn_a_log, gdn_dt_bias=gdn_dt_bias, gdn_out_gain=gdn_out_gain, gdn_w_out=gdn_w_out, kv_gain=kv_gain, w_kv=w_kv, sb_w_q=sb_w_q, sb_w_o=sb_w_o)
    m = dict(mix_pre_gain=m_mix_pre_gain, mix_post_gain=m_mix_post_gain, mlp_pre_gain=m_mlp_pre_gain, mlp_post_gain=m_mlp_post_gain, mlp_w_up=m_mlp_w_up, mlp_w_down=m_mlp_w_down, gdn_w_in=m_gdn_w_in, gdn_conv_w=m_gdn_conv_w, gdn_a_log=m_gdn_a_log, gdn_dt_bias=m_gdn_dt_bias, gdn_out_gain=m_gdn_out_gain, gdn_w_out=m_gdn_w_out, kv_gain=m_kv_gain, w_kv=m_w_kv, sb_w_q=m_sb_w_q, sb_w_o=m_sb_w_o)
    v = dict(mix_pre_gain=v_mix_pre_gain, mix_post_gain=v_mix_post_gain, mlp_pre_gain=v_mlp_pre_gain, mlp_post_gain=v_mlp_post_gain, mlp_w_up=v_mlp_w_up, mlp_w_down=v_mlp_w_down, gdn_w_in=v_gdn_w_in, gdn_conv_w=v_gdn_conv_w, gdn_a_log=v_gdn_a_log, gdn_dt_bias=v_gdn_dt_bias, gdn_out_gain=v_gdn_out_gain, gdn_w_out=v_gdn_w_out, kv_gain=v_kv_gain, w_kv=v_w_kv, sb_w_q=v_sb_w_q, sb_w_o=v_sb_w_o)
    cx, cy, cc = _place()
    chip = 2 * cx + cy
    conv_cols = gdn_conv_w.shape[-1]

    own = _pack_shards(w, BF16)
    with_own = lambda gathered, mine: [lax.dynamic_update_index_in_dim(g, m, chip, 0) for g, m in zip(gathered, mine)]
    w_in = _unpack_full(with_own(_gather_weights(own[2:]), own[2:]), _GROUPS[2:])["gdn_w_in"][0]

    def assemble(gathered):
        full = _unpack_full(with_own(gathered, own[:2]), _GROUPS[:2])
        return full["gdn_w_out"][0], full["w_kv"], full["sb_w_q"][0], full["sb_w_o"][0], full["mlp_w_up"], full["mlp_w_down"]

    conv_rows = jnp.pad(gdn_conv_w[0], ((0, 8 - CONV_K), (0, 0))).reshape(-1, LANES)
    conv_all, _ = _gather8(conv_rows, "gather_conv_w")
    conv_all = conv_all.reshape(N_CHIPS, 2, 8, conv_cols)[:, 0, :CONV_K]
    conv_full = jnp.swapaxes(conv_all, 0, 1).reshape(CONV_K, N_CHIPS * conv_cols)

    w_in = (w_in[:, :4 * HEADS * HEAD_DIM], jnp.pad(w_in[:, 4 * HEADS * HEAD_DIM:], ((0, 0), (0, LANES - 2 * HEADS))))
    gains = (mix_pre_gain, mix_post_gain, mlp_pre_gain, mlp_post_gain, kv_gain[None])
    small = (conv_full, gdn_a_log, gdn_dt_bias, gdn_out_gain)
    loss_rows, grad_x, g_full = _local_step(x[0], loss_target[0], gains, w_in, small, own[:2], assemble)

    bufs = _pack_full(g_full)
    from_sibling = _swap_halves(bufs)
    partial, partial_bf16 = [], []
    for i, (buf, other) in enumerate(zip(bufs, from_sibling)):
        cols = buf.shape[-1]
        own_half = lax.dynamic_index_in_dim(buf, cc, 0, keepdims=False)
        p, pb = _rowwise(f"grads_add_sibling_{i}", lambda a, b: (a + b, a + b), [(own_half, cols, 0), (other, cols, 0)], [], [(cols, F32), (cols, BF16)], tm=PACK_ROW_TILE)
        partial.append(p.reshape(N_CHIPS, -1, cols))
        partial_bf16.append(pb.reshape(N_CHIPS, -1, cols))
    from_chips = _scatter_to_chips(tuple(partial_bf16))
    reduced = []
    for i, (p, others) in enumerate(zip(partial, from_chips)):
        cols = p.shape[-1]
        mine = lax.dynamic_index_in_dim(p, chip, 0, keepdims=False)
        (r,) = _rowwise(f"grads_add_chips_{i}", lambda a, b, c, d: (((a + b) + c) + d,),
                        [(mine, cols, 0), (others[0], cols, 0), (others[1], cols, 0), (others[2], cols, 0)], [], [(cols, F32)], tm=PACK_ROW_TILE)
        reduced.append(r)
    g_shard = _unpack_shards([jnp.where(cc == 0, jnp.stack([r, o]), jnp.stack([o, r])) for r, o in zip(reduced, _share_halves(tuple(reduced)))])

    g_small_local = {n: g_full[n] for n, _ in _SMALL if n != "loss"}
    g_small_local["loss"] = loss_rows[0, 0]
    _, small_sum = _gather8(_pack_small(g_small_local), "allreduce_small")
    g_small = _unpack_small(small_sum)
    loss = g_small.pop("loss")
    g_small["gdn_conv_w"] = lax.dynamic_slice_in_dim(g_small["gdn_conv_w"], chip * conv_cols, conv_cols, axis=2)

    grads = {**g_shard, **g_small}
    deltas, new_m, new_v = {}, {}, {}
    for n in _WEIGHTS:
        d2, m2, v2 = _adamw(_as2d(w[n]), _as2d(grads[n]), _as2d(m[n]), _as2d(v[n]), "adamw_" + n)
        deltas[n], new_m[n], new_v[n] = d2.reshape(w[n].shape), m2.reshape(w[n].shape), v2.reshape(w[n].shape)
    return (loss, grad_x[None], *[grads[n].reshape(w[n].shape) for n in _WEIGHTS], *[deltas[n] for n in _WEIGHTS],
            *[new_m[n] for n in _WEIGHTS], *[new_v[n] for n in _WEIGHTS])
```

```python
import functools

import jax
import jax.numpy as jnp
from jax import lax
from jax.experimental import pallas as pl
from jax.experimental.pallas import tpu as pltpu

F32, BF16 = jnp.float32, jnp.bfloat16
HI = lax.Precision.HIGHEST
MESH = pl.DeviceIdType.MESH

EPS = 1e-6
D_MODEL = 1024
HEADS = 8
HEAD_DIM = 128
CHUNK = 64
CHUNK_SHIFT = CHUNK.bit_length() - 1
CONV_K = 4
D_FF = 4096
QKV = 3 * HEADS * HEAD_DIM

ADAM_LR, ADAM_B1, ADAM_B2, ADAM_EPS, ADAM_WD, ADAM_STEP = 0.001, 0.9, 0.999, 1e-08, 0.01, 10

VMEM_LIMIT_BYTES = 48 * 1024 * 1024
LANES = 128

NN = ((1,), (0,))
NT = ((1,), (1,))
TN = ((0,), (0,))


def _dot(a, b, dims=NN, precision=None):
    return lax.dot_general(a, b, (dims, ((), ())), precision=precision, preferred_element_type=F32)


def _params(*sem):
    return pltpu.CompilerParams(dimension_semantics=sem, vmem_limit_bytes=VMEM_LIMIT_BYTES)


def _iota(shape, axis):
    return lax.broadcasted_iota(jnp.int32, shape, axis)


def _matmul(a, b, mode, out_dtype, name, tm=1024, tn=1024, tk=1024, add=None, epilogue=None, extras=()):
    if mode == "nn":
        (m, k), (k2, n) = a.shape, b.shape
    elif mode == "nt":
        (m, k), (n, k2) = a.shape, b.shape
    else:
        (k, m), (k2, n) = a.shape, b.shape
    assert k == k2, (a.shape, b.shape, mode)
    tm, tn, tk = min(tm, m), min(tn, n), min(tk, k)
    assert m % tm == 0 and n % tn == 0 and k % tk == 0, (a.shape, b.shape, mode)
    nk = k // tk
    dims = {"nn": NN, "nt": NT, "tn": TN}[mode]
    tiles = ([add] if add is not None else []) + list(extras)
    out_dtypes = out_dtype if epilogue is not None else (out_dtype,)
    n_in = 2 + len(tiles)

    def finish(acc, extra_refs, o_refs):
        res = (acc,) if epilogue is None else epilogue(acc, *[r[...] for r in extra_refs])
        for o_ref, r in zip(o_refs, res):
            o_ref[...] = r.astype(o_ref.dtype)

    def body(*refs):
        a_ref, b_ref = refs[:2]
        extra_refs = refs[n_in - len(extras):n_in]
        o_refs, acc_ref = refs[n_in:-1], refs[-1]
        prod = _dot(a_ref[...].astype(BF16), b_ref[...].astype(BF16), dims)
        if nk == 1:
            finish(prod + refs[2][...].astype(F32) if add is not None else prod, extra_refs, o_refs)
            return
        kk = pl.program_id(2)

        @pl.when(kk == 0)
        def _():
            acc_ref[...] = refs[2][...].astype(F32) if add is not None else jnp.zeros_like(acc_ref)

        acc_ref[...] += prod

        @pl.when(kk == nk - 1)
        def _():
            finish(acc_ref[...], extra_refs, o_refs)

    a_spec = pl.BlockSpec((tk, tm), lambda i, j, kk: (kk, i)) if mode == "tn" else pl.BlockSpec((tm, tk), lambda i, j, kk: (i, kk))
    b_spec = pl.BlockSpec((tn, tk), lambda i, j, kk: (j, kk)) if mode == "nt" else pl.BlockSpec((tk, tn), lambda i, j, kk: (kk, j))
    o_spec = pl.BlockSpec((tm, tn), lambda i, j, kk: (i, j))
    res = pl.pallas_call(
        body,
        name=name,
        grid=(m // tm, n // tn, nk),
        in_specs=[a_spec, b_spec] + [o_spec] * len(tiles),
        out_specs=[o_spec] * len(out_dtypes),
        out_shape=[jax.ShapeDtypeStruct((m, n), dt) for dt in out_dtypes],
        scratch_shapes=[pltpu.VMEM((tm, tn), F32)],
        compiler_params=_params("parallel", "parallel", "arbitrary"),
    )(a, b, *tiles)
    return res if epilogue is not None else res[0]


def _row_specs(rows, tm):
    return [pl.BlockSpec((tm, w), lambda i, cb=cb: (i, cb)) for _, w, cb in rows]


def _full_spec(p):
    return pl.BlockSpec(p.shape, lambda i: (0,) * p.ndim)


def _rowwise(name, fn, rows, params, outs, tm=256):
    t = rows[0][0].shape[0]
    tm = min(tm, t)
    nr, npar = len(rows), len(params)

    def body(*refs):
        ins = [r[...].astype(F32) for r in refs[:nr]]
        ps = [p[...] for p in refs[nr:nr + npar]]
        res = fn(*ins, *ps)
        for o_ref, r in zip(refs[nr + npar:], res):
            o_ref[...] = r.astype(o_ref.dtype)

    return pl.pallas_call(
        body,
        name=name,
        grid=(t // tm,),
        in_specs=_row_specs(rows, tm) + [_full_spec(p) for p in params],
        out_specs=[pl.BlockSpec((tm, w), lambda i: (i, 0)) for w, _ in outs],
        out_shape=[jax.ShapeDtypeStruct((t, w), dt) for w, dt in outs],
        compiler_params=_params("parallel"),
    )(*[r[0] for r in rows], *params)


def _rowwise_bwd(name, fn, rows, params, cots, grad_dtypes, tm=256):
    t = rows[0][0].shape[0]
    tm = min(tm, t)
    nr, npar, nc = len(rows), len(params), len(cots)
    want = [j for j, dt in enumerate(grad_dtypes) if dt is not None]
    widths = [rows[j][1] for j in want]
    n_row_outs = len(want)

    def body(*refs):
        i = pl.program_id(0)
        ins = [r[...].astype(F32) for r in refs[:nr]]
        ps = [p[...] for p in refs[nr:nr + npar]]
        cs = tuple(c[...].astype(F32) for c in refs[nr + npar:nr + npar + nc])
        _, vjp = jax.vjp(fn, *ins, *ps)
        gs = vjp(cs)
        outs = refs[nr + npar + nc:]
        for o_ref, j in zip(outs, want):
            o_ref[...] = gs[j].astype(o_ref.dtype)
        pg_refs = outs[n_row_outs:]

        @pl.when(i == 0)
        def _():
            for pg in pg_refs:
                pg[...] = jnp.zeros_like(pg)

        for pg, g in zip(pg_refs, gs[nr:]):
            pg[...] += g

    row_specs = [pl.BlockSpec((tm, w), lambda i: (i, 0)) for w in widths]
    row_shapes = [jax.ShapeDtypeStruct((t, w), grad_dtypes[j]) for j, w in zip(want, widths)]
    res = pl.pallas_call(
        body,
        name=name,
        grid=(t // tm,),
        in_specs=_row_specs(rows, tm) + [_full_spec(p) for p in params] + [pl.BlockSpec((tm, c.shape[1]), lambda i: (i, 0)) for c in cots],
        out_specs=row_specs + [_full_spec(p) for p in params],
        out_shape=row_shapes + [jax.ShapeDtypeStruct(p.shape, F32) for p in params],
        compiler_params=_params("arbitrary"),
    )(*[r[0] for r in rows], *params, *cots)
    return res[:n_row_outs], res[n_row_outs:]


def _rms(x, g):
    return x * lax.rsqrt(jnp.mean(x * x, axis=-1, keepdims=True) + EPS) * g


def _sigmoid(x):
    return 1.0 / (1.0 + jnp.exp(-x))


def _softplus(x):
    return jnp.maximum(x, 0.0) + jnp.log1p(jnp.exp(-jnp.abs(x)))


def _two_pass(x, m):
    hi = x.astype(BF16)
    lo = (x - hi.astype(F32)).astype(BF16)
    return _dot(hi, m) + _dot(lo, m)


def _head_sum_impl(x):
    w = HEADS * HEAD_DIM
    fold = jnp.where((_iota((w, LANES), 0) >> 7) == _iota((w, LANES), 1), 1.0, 0.0).astype(BF16)
    spread = jnp.where(_iota((LANES, w), 0) == (_iota((LANES, w), 1) >> 7), 1.0, 0.0).astype(BF16)
    return _two_pass(_two_pass(x, fold), spread)


@jax.custom_vjp
def _head_sum(x):
    return _head_sum_impl(x)


_head_sum.defvjp(lambda x: (_head_sum_impl(x), None), lambda _, g: (_head_sum_impl(g),))


def _fn_norm(x, g):
    return (_rms(x, g),)


def _fn_gates(ba, al, dt):
    col = _iota((1, LANES), 1)
    g = jnp.where((col >= HEADS) & (col < 2 * HEADS), -jnp.exp(al) * _softplus(ba + dt), 0.0)
    rows = ba.shape[0]
    r, c = _iota((rows, rows), 0), _iota((rows, rows), 1)
    same = (r >> CHUNK_SHIFT) == (c >> CHUNK_SHIFT)
    gc = _dot(jnp.where(same & (r >= c), 1.0, 0.0), g, precision=HI)
    gtot = _dot(jnp.where(same, 1.0, 0.0), g, precision=HI)
    return _sigmoid(ba), gc, gtot


def _fn_post_q(c):
    s = c * _sigmoid(c)
    return (s * lax.rsqrt(_head_sum(s * s) + EPS) * (HEAD_DIM ** -0.5),)


def _fn_post_k(c):
    s = c * _sigmoid(c)
    return (s * lax.rsqrt(_head_sum(s * s) + EPS),)


def _fn_post_v(c):
    return (c * _sigmoid(c),)


def _fn_post(cq, ck, cv):
    return _fn_post_q(cq) + _fn_post_k(ck) + _fn_post_v(cv)


def _fn_outnorm(o, gate, og):
    y = o * lax.rsqrt(_head_sum(o * o) * (1.0 / HEAD_DIM) + EPS) * og
    return (y * (gate * _sigmoid(gate)),)


def _fn_res_norm(x, m, gp, gn):
    x1 = x + _rms(m, gp)
    return x1, _rms(x1, gn)


def _fn_res_norm2(x, m, gp, ga, gb):
    x1 = x + _rms(m, gp)
    return x1, _rms(x1, ga), _rms(x1, gb)


def _relu2_of(u):
    r = jnp.maximum(u, 0.0)
    return (r * r,)


def _relu2_cotangent(da, a):
    return (da * (2.0 * jnp.sqrt(a.astype(F32))),)


def _loss_call(x3, d1, tgt, g, tm=256):
    t, d = x3.shape
    tm = min(tm, t)

    def body(x_ref, d_ref, t_ref, g_ref, loss_ref, dx_ref, dd_ref, dg_ref):
        i = pl.program_id(0)
        y, vjp = jax.vjp(lambda x, dd, gg: x + _rms(dd, gg), x_ref[...], d_ref[...], g_ref[...])
        err = y - t_ref[...]
        lrow = 0.5 * jnp.mean(err * err, axis=-1, keepdims=True)
        dx, dd, dg = vjp(err * (1.0 / d))
        dx_ref[...] = dx
        dd_ref[...] = dd.astype(dd_ref.dtype)

        @pl.when(i == 0)
        def _():
            loss_ref[...] = jnp.zeros_like(loss_ref)
            dg_ref[...] = jnp.zeros_like(dg_ref)

        loss_ref[...] += jnp.broadcast_to(jnp.sum(lrow, axis=0, keepdims=True), loss_ref.shape)
        dg_ref[...] += dg

    row = pl.BlockSpec((tm, d), lambda i: (i, 0))
    return pl.pallas_call(
        body,
        name="loss_head",
        grid=(t // tm,),
        in_specs=[row, row, row, _full_spec(g)],
        out_specs=[pl.BlockSpec((8, LANES), lambda i: (0, 0)), row, row, _full_spec(g)],
        out_shape=[jax.ShapeDtypeStruct((8, LANES), F32), jax.ShapeDtypeStruct((t, d), F32), jax.ShapeDtypeStruct((t, d), BF16), jax.ShapeDtypeStruct(g.shape, F32)],
        compiler_params=_params("arbitrary"),
    )(x3, d1, tgt, g)


HALO = 8


def _conv_fwd(qkvg, conv_w, shards, tm=256):
    t = qkvg.shape[0]
    tm = min(tm, t)
    steps = t // tm
    wide = QKV // 3
    n = len(shards)

    def body(*refs):
        cur_ref, prev_ref, w_ref = refs[:3]
        shard_refs = refs[3:3 + n]
        o_ref, q_ref, k_ref, v_ref = refs[3 + n:7 + n]
        all_refs = refs[7 + n:7 + 2 * n]
        buf, sems = refs[7 + 2 * n], refs[8 + 2 * n:]
        i = pl.program_id(0)

        if n:
            @pl.when(i == 0)
            def _():
                for cp in _gather_sends(shard_refs, all_refs, *sems[:2]):
                    cp.start()

        buf[0:HALO, :] = jnp.where(i > 0, prev_ref[...], 0.0)
        buf[HALO:, :] = cur_ref[...]
        acc = buf[pl.ds(HALO - CONV_K + 1, tm), :] * w_ref[pl.ds(0, 1), :]
        for j in range(1, CONV_K):
            acc = acc + buf[pl.ds(HALO - CONV_K + 1 + j, tm), :] * w_ref[pl.ds(j, 1), :]
        o_ref[...] = acc
        (q_ref[...], k_ref[...], v_ref[...]) = _fn_post(acc[:, 0:wide], acc[:, wide:2 * wide], acc[:, 2 * wide:])

        if n:
            @pl.when(i == steps - 1)
            def _():
                _gather_finish(shard_refs, all_refs, *sems)

    part = pl.BlockSpec((tm, wide), lambda i: (i, 0))
    res = pl.pallas_call(
        body,
        name="conv_fwd",
        grid=(steps,),
        in_specs=[
            pl.BlockSpec((tm, QKV), lambda i: (i, 0)),
            pl.BlockSpec((HALO, QKV), lambda i: (jnp.maximum(i * (tm // HALO) - 1, 0), 0)),
            pl.BlockSpec((CONV_K, QKV), lambda i: (0, 0)),
        ] + [_HBM] * n,
        out_specs=[pl.BlockSpec((tm, QKV), lambda i: (i, 0)), part, part, part] + [_HBM] * n,
        out_shape=[jax.ShapeDtypeStruct((t, QKV), F32)] + [jax.ShapeDtypeStruct((t, wide), F32)] * 3
        + [jax.ShapeDtypeStruct((N_CHIPS,) + s.shape, s.dtype) for s in shards],
        scratch_shapes=[pltpu.VMEM((tm + HALO, QKV), F32)] + [pltpu.SemaphoreType.DMA((3 * n,))] * (4 if n else 0),
        compiler_params=_params("arbitrary"),
    )(qkvg, qkvg, conv_w, *shards)
    return res[:4], res[4:]


def _conv_bwd(conv, dqkv, dgate, qkvg, conv_w, tm=256):
    t = conv.shape[0]
    tm = min(tm, t)
    n = t // tm
    wg = dgate.shape[1]
    wide = QKV // 3

    def conv_cotangent(c_ref, g_refs):
        parts = [c_ref[:, j * wide:(j + 1) * wide] for j in range(3)]
        _, vjp = jax.vjp(_fn_post, *parts)
        return vjp(tuple(g[...] for g in g_refs))

    def body(c_ref, cn_ref, dq_ref, dk_ref, dv_ref, dqn_ref, dkn_ref, dvn_ref, dgate_ref, x_ref, xp_ref, w_ref, dx_ref, dw_ref, bufd, bufx):
        i = pl.program_id(0)
        for j, (cur, nxt) in enumerate(zip(conv_cotangent(c_ref, (dq_ref, dk_ref, dv_ref)), conv_cotangent(cn_ref, (dqn_ref, dkn_ref, dvn_ref)))):
            bufd[0:tm, j * wide:(j + 1) * wide] = cur
            bufd[tm:, j * wide:(j + 1) * wide] = jnp.where(i < n - 1, nxt, 0.0)
        bufx[0:HALO, :] = jnp.where(i > 0, xp_ref[...], 0.0)
        bufx[HALO:, :] = x_ref[...]

        @pl.when(i == 0)
        def _():
            dw_ref[...] = jnp.zeros_like(dw_ref)

        dcv = bufd[0:tm, :]
        acc = bufd[pl.ds(CONV_K - 1, tm), :] * w_ref[pl.ds(0, 1), :]
        for j in range(1, CONV_K):
            acc = acc + bufd[pl.ds(CONV_K - 1 - j, tm), :] * w_ref[pl.ds(j, 1), :]
        dx_ref[:, 0:QKV] = acc.astype(dx_ref.dtype)
        dx_ref[:, QKV:] = dgate_ref[...].astype(dx_ref.dtype)
        for j in range(CONV_K):
            dw_ref[pl.ds(j, 1), :] += jnp.sum(dcv * bufx[pl.ds(HALO - CONV_K + 1 + j, tm), :], axis=0, keepdims=True)

    def cur(width):
        return pl.BlockSpec((tm, width), lambda i: (i, 0))

    def nxt(width):
        return pl.BlockSpec((HALO, width), lambda i: (jnp.minimum((i + 1) * (tm // HALO), t // HALO - 1), 0))

    return pl.pallas_call(
        body,
        name="conv_bwd",
        grid=(n,),
        in_specs=[cur(QKV), nxt(QKV)] + [cur(wide)] * 3 + [nxt(wide)] * 3 + [
            cur(wg),
            cur(QKV),
            pl.BlockSpec((HALO, QKV), lambda i: (jnp.maximum(i * (tm // HALO) - 1, 0), 0)),
            pl.BlockSpec((CONV_K, QKV), lambda i: (0, 0)),
        ],
        out_specs=[pl.BlockSpec((tm, QKV + wg), lambda i: (i, 0)), pl.BlockSpec((HALO, QKV), lambda i: (0, 0))],
        out_shape=[jax.ShapeDtypeStruct((t, QKV + wg), BF16), jax.ShapeDtypeStruct((HALO, QKV), F32)],
        scratch_shapes=[pltpu.VMEM((tm + HALO, QKV), F32), pltpu.VMEM((tm + HALO, QKV), F32)],
        compiler_params=_params("arbitrary"),
    )(conv, conv, *dqkv, *dqkv, dgate, qkvg, qkvg, conv_w)


PREP_CHUNKS = 16
PREP_BWD_CHUNKS = 4
SCAN_CHUNKS = 4


def _hi_lo(x):
    hi = x.astype(BF16)
    return hi, (x - hi.astype(F32)).astype(BF16)


def _mm3(a, b, dims=NN):
    (ah, al), (bh, bl) = _hi_lo(a), _hi_lo(b)
    return _dot(ah, bh, dims) + (_dot(ah, bl, dims) + _dot(al, bh, dims))


def _neumann(lowers):
    c = lowers[0].shape[0]
    eye = jnp.where(_iota((c, c), 0) == _iota((c, c), 1), 1.0, 0.0)
    ps = [-low for low in lowers]
    tmats = [eye + p for p in ps]
    for _ in range(CHUNK_SHIFT - 1):
        ps = [_mm3(p, p) for p in ps]
        tmats = [t + _mm3(t, p) for t, p in zip(tmats, ps)]
    return tuple(tmats)


def _inv_cotangents(tmats, dts):
    half = [_mm3(t, dt, TN) for t, dt in zip(tmats, dts)]
    return tuple(-_mm3(hf, t, NT) for hf, t in zip(half, tmats))


@jax.custom_vjp
def _tri_inv(lowers):
    return _neumann(lowers)


def _tri_inv_fwd(lowers):
    tmats = _neumann(lowers)
    return tmats, tmats


_tri_inv.defvjp(_tri_inv_fwd, lambda tmats, dts: (_inv_cotangents(tmats, dts),))


@jax.custom_vjp
def _tri_inv_known(lowers, tmats):
    return tmats


_tri_inv_known.defvjp(lambda lowers, tmats: (tmats, tmats),
                      lambda tmats, dts: (_inv_cotangents(tmats, dts), tuple(jnp.zeros_like(t) for t in tmats)))


def _prep_chunks(qs, ks, vs, bs, gcs, gts, gcrs, tmats=None):
    c = CHUNK
    r, col = _iota((c, c), 0), _iota((c, c), 1)
    incl, strict = r >= col, r > col
    decays = [jnp.where(incl, jnp.exp(jnp.where(incl, gc - gcr, 0.0)), 0.0) for gc, gcr in zip(gcs, gcrs)]
    kbs = [k * b for k, b in zip(ks, bs)]
    kbfs = [k.astype(BF16) for k in ks]
    lowers = tuple(jnp.where(strict, _dot(kb.astype(BF16), kbf, NT) * decay, 0.0) for kb, kbf, decay in zip(kbs, kbfs, decays))
    tmats = _tri_inv(lowers) if tmats is None else _tri_inv_known(lowers, tuple(tmats))
    outs = []
    for q, k, v, b, gc, gt, kb, kbf, decay, tmat in zip(qs, ks, vs, bs, gcs, gts, kbs, kbfs, decays, tmats):
        tb = tmat.astype(BF16)
        egc = jnp.exp(gc)
        w = _dot(tb, (kb * egc).astype(BF16))
        u = _dot(tb, (v * b).astype(BF16))
        attn = _dot(q.astype(BF16), kbf, NT) * decay
        gl = jnp.broadcast_to(jnp.exp(jnp.mean(gt.reshape(c // 8, 8, 1), axis=0)), (8, HEAD_DIM))
        outs.append((w, u, q * egc, k * jnp.exp(gt - gc), attn, gl))
    return tuple(outs), tmats


def _prep_specs(rows, gch):
    head = pl.BlockSpec((rows, HEAD_DIM), lambda n, h: (n, h))
    gates = pl.BlockSpec((rows, LANES), lambda n, h: (n, 0))
    gcrow = pl.BlockSpec((1, gch, 1, CHUNK), lambda n, h: (h, n, 0, 0))
    square = pl.BlockSpec((1, rows, CHUNK), lambda n, h: (h, n, 0))
    gl = pl.BlockSpec((1, gch * 8, HEAD_DIM), lambda n, h: (h, n, 0))
    return head, gates, gcrow, square, gl


def _pick_lane(ref, sl, lane):
    return jnp.sum(jnp.where(_iota((1, LANES), 1) == lane, ref[sl, :], 0.0), axis=1, keepdims=True)


def _prep_inputs(q_ref, k_ref, v_ref, b_ref, gc_ref, gt_ref, gcr_ref, sls, h):
    return ([q_ref[sl, :] for sl in sls], [k_ref[sl, :] for sl in sls], [v_ref[sl, :] for sl in sls],
            [_pick_lane(b_ref, sl, h) for sl in sls], [_pick_lane(gc_ref, sl, h + HEADS) for sl in sls],
            [_pick_lane(gt_ref, sl, h + HEADS) for sl in sls], [gcr_ref[0, c] for c in range(len(sls))])


def _gdn_prep(q, k, v, beta, gc, gt, gcr):
    t = q.shape[0]
    gch = min(PREP_CHUNKS, t // CHUNK)
    rows = gch * CHUNK

    def body(q_ref, k_ref, v_ref, b_ref, gc_ref, gt_ref, gcr_ref, w_ref, u_ref, qg_ref, kg_ref, at_ref, gl_ref, tm_ref):
        h = pl.program_id(1)
        sls = [pl.ds(c * CHUNK, CHUNK) for c in range(gch)]
        outs, tmats = _prep_chunks(*_prep_inputs(q_ref, k_ref, v_ref, b_ref, gc_ref, gt_ref, gcr_ref, sls, h))
        for c, (sl, (w, u, qg, kg, attn, gl), tmat) in enumerate(zip(sls, outs, tmats)):
            w_ref[sl, :] = w.astype(BF16)
            u_ref[sl, :] = u
            qg_ref[sl, :] = qg.astype(BF16)
            kg_ref[sl, :] = kg.astype(BF16)
            at_ref[0, sl, :] = attn.astype(BF16)
            gl_ref[0, pl.ds(c * 8, 8), :] = gl
            tm_ref[0, sl, :] = tmat

    hb, col, gcrow, square, glb = _prep_specs(rows, gch)
    wide = HEADS * HEAD_DIM
    return pl.pallas_call(
        body,
        name="gdn_prep",
        grid=(t // rows, HEADS),
        in_specs=[hb, hb, hb, col, col, col, gcrow],
        out_specs=[hb, hb, hb, hb, square, glb, square],
        out_shape=[
            jax.ShapeDtypeStruct((t, wide), BF16),
            jax.ShapeDtypeStruct((t, wide), F32),
            jax.ShapeDtypeStruct((t, wide), BF16),
            jax.ShapeDtypeStruct((t, wide), BF16),
            jax.ShapeDtypeStruct((HEADS, t, CHUNK), BF16),
            jax.ShapeDtypeStruct((HEADS, t // CHUNK * 8, HEAD_DIM), F32),
            jax.ShapeDtypeStruct((HEADS, t, CHUNK), F32),
        ],
        compiler_params=_params("parallel", "parallel"),
    )(q, k, v, beta, gc, gt, gcr)


def _gdn_prep_bwd(q, k, v, beta, gc, gt, gcr, tmat, dw, du, dqg, dkg, dattn, dgl):
    t = q.shape[0]
    gch = min(PREP_BWD_CHUNKS, t // CHUNK)
    rows = gch * CHUNK

    def body(q_ref, k_ref, v_ref, b_ref, gc_ref, gt_ref, gcr_ref, tm_ref, dw_ref, du_ref, dqg_ref, dkg_ref, dat_ref, dgl_ref,
             dq_ref, dk_ref, dv_ref, db_ref, dgc_ref, dgt_ref, dgcr_ref):
        h = pl.program_id(1)
        lane = _iota((1, LANES), 1)

        @pl.when(h == 0)
        def _():
            db_ref[...] = jnp.zeros_like(db_ref)
            dgc_ref[...] = jnp.zeros_like(dgc_ref)
            dgt_ref[...] = jnp.zeros_like(dgt_ref)

        sls = [pl.ds(c * CHUNK, CHUNK) for c in range(gch)]
        known = [tm_ref[0, sl, :] for sl in sls]
        _, vjp = jax.vjp(lambda *a: _prep_chunks(*a, tmats=known)[0], *_prep_inputs(q_ref, k_ref, v_ref, b_ref, gc_ref, gt_ref, gcr_ref, sls, h))
        cots = tuple((dw_ref[sl, :], du_ref[sl, :], dqg_ref[sl, :], dkg_ref[sl, :], dat_ref[0, sl, :], dgl_ref[0, pl.ds(c * 8, 8), :]) for c, sl in enumerate(sls))
        dqs, dks, dvs, dbs, dgcs, dgts, dgcrs = vjp(cots)
        for c, sl in enumerate(sls):
            dq_ref[sl, :] = dqs[c]
            dk_ref[sl, :] = dks[c]
            dv_ref[sl, :] = dvs[c]
            db_ref[sl, :] += jnp.where(lane == h, dbs[c], 0.0)
            dgc_ref[sl, :] += jnp.where(lane == h + HEADS, dgcs[c], 0.0)
            dgt_ref[sl, :] += jnp.where(lane == h + HEADS, dgts[c], 0.0)
            dgcr_ref[0, c] = dgcrs[c]

    hb, col, gcrow, square, glb = _prep_specs(rows, gch)
    wide = HEADS * HEAD_DIM
    return pl.pallas_call(
        body,
        name="gdn_prep_bwd",
        grid=(t // rows, HEADS),
        in_specs=[hb, hb, hb, col, col, col, gcrow, square, hb, hb, hb, hb, square, glb],
        out_specs=[hb, hb, hb, col, col, col, gcrow],
        out_shape=[jax.ShapeDtypeStruct((t, wide), F32)] * 3 + [jax.ShapeDtypeStruct((t, LANES), F32)] * 3 + [jax.ShapeDtypeStruct((HEADS, t // CHUNK, 1, CHUNK), F32)],
        compiler_params=_params("parallel", "arbitrary"),
    )(q, k, v, beta, gc, gt, gcr, tmat, dw, du, dqg, dkg, dattn, dgl)


def _gdn_scan(w, u, qg, kg, attn, gl):
    t = w.shape[0]
    n = t // CHUNK
    nch = min(SCAN_CHUNKS, n)
    wide = HEADS * HEAD_DIM

    def body(w_ref, u_ref, qg_ref, kg_ref, at_ref, gl_ref, o_ref, st_ref, s_ref):
        @pl.when(pl.program_id(0) == 0)
        def _():
            s_ref[...] = jnp.zeros_like(s_ref)

        heads = range(HEADS)
        cols = [pl.ds(h * HEAD_DIM, HEAD_DIM) for h in heads]
        for c in range(nch):
            rows, gl_rows = pl.ds(c * CHUNK, CHUNK), pl.ds(c * 8, 8)
            ss = [s_ref[h] for h in heads]
            sbs = [s.astype(BF16) for s in ss]
            vbs = [(u_ref[rows, hs] - _dot(w_ref[rows, hs], sb)).astype(BF16) for hs, sb in zip(cols, sbs)]
            outs = [_dot(qg_ref[rows, hs], sb) + _dot(at_ref[h, rows, :], vb) for h, hs, sb, vb in zip(heads, cols, sbs, vbs)]
            new = [s * jnp.tile(gl_ref[h, gl_rows, :], (HEAD_DIM // 8, 1)) + _dot(kg_ref[rows, hs], vb, TN) for h, hs, s, vb in zip(heads, cols, ss, vbs)]
            for h, hs in zip(heads, cols):
                st_ref[c, h] = ss[h]
                o_ref[rows, hs] = outs[h]
                s_ref[h] = new[h]

    row = pl.BlockSpec((nch * CHUNK, wide), lambda i: (i, 0))
    return pl.pallas_call(
        body,
        name="gdn_scan",
        grid=(n // nch,),
        in_specs=[row, row, row, row, pl.BlockSpec((HEADS, nch * CHUNK, CHUNK), lambda i: (0, i, 0)), pl.BlockSpec((HEADS, nch * 8, HEAD_DIM), lambda i: (0, i, 0))],
        out_specs=[row, pl.BlockSpec((nch, HEADS, HEAD_DIM, HEAD_DIM), lambda i: (i, 0, 0, 0))],
        out_shape=[jax.ShapeDtypeStruct((t, wide), F32), jax.ShapeDtypeStruct((n, HEADS, HEAD_DIM, HEAD_DIM), F32)],
        scratch_shapes=[pltpu.VMEM((HEADS, HEAD_DIM, HEAD_DIM), F32)],
        compiler_params=_params("arbitrary"),
    )(w, u, qg, kg, attn, gl)


def _gdn_scan_bwd(w, u, qg, kg, attn, gl, states, do):
    t = w.shape[0]
    n = t // CHUNK
    nch = min(SCAN_CHUNKS, n)
    steps = n // nch
    wide = HEADS * HEAD_DIM

    def body(w_ref, u_ref, qg_ref, kg_ref, at_ref, gl_ref, st_ref, do_ref, dw_ref, du_ref, dqg_ref, dkg_ref, dat_ref, dgl_ref, ds_ref):
        @pl.when(pl.program_id(0) == 0)
        def _():
            ds_ref[...] = jnp.zeros_like(ds_ref)

        heads = range(HEADS)
        cols = [pl.ds(h * HEAD_DIM, HEAD_DIM) for h in heads]
        for c in reversed(range(nch)):
            rows, gl_rows = pl.ds(c * CHUNK, CHUNK), pl.ds(c * 8, 8)
            ss = [st_ref[c, h] for h in heads]
            sbs = [s.astype(BF16) for s in ss]
            dsns = [ds_ref[h] for h in heads]
            dsbs = [d.astype(BF16) for d in dsns]
            dobs = [do_ref[rows, hs].astype(BF16) for hs in cols]
            vbs = [(u_ref[rows, hs] - _dot(w_ref[rows, hs], sb)).astype(BF16) for hs, sb in zip(cols, sbs)]
            dvns = [_dot(at_ref[h, rows, :], dob, TN) + _dot(kg_ref[rows, hs], dsb) for h, hs, dob, dsb in zip(heads, cols, dobs, dsbs)]
            dvbs = [d.astype(BF16) for d in dvns]
            for h, hs in zip(heads, cols):
                dat_ref[h, rows, :] = _dot(dobs[h], vbs[h], NT)
                dqg_ref[rows, hs] = _dot(dobs[h], sbs[h], NT)
                dkg_ref[rows, hs] = _dot(vbs[h], dsbs[h], NT)
                du_ref[rows, hs] = dvns[h]
                dw_ref[rows, hs] = -_dot(dvbs[h], sbs[h], NT)
                dgl_ref[h, gl_rows, :] = jnp.sum((dsns[h] * ss[h]).reshape(HEAD_DIM // 8, 8, HEAD_DIM), axis=0)
            new = [dsn * jnp.tile(gl_ref[h, gl_rows, :], (HEAD_DIM // 8, 1)) + _dot(qg_ref[rows, hs], dob, TN) - _dot(w_ref[rows, hs], dvb, TN)
                   for h, hs, dsn, dob, dvb in zip(heads, cols, dsns, dobs, dvbs)]
            for h in heads:
                ds_ref[h] = new[h]

    row = pl.BlockSpec((nch * CHUNK, wide), lambda i: (steps - 1 - i, 0))
    at = pl.BlockSpec((HEADS, nch * CHUNK, CHUNK), lambda i: (0, steps - 1 - i, 0))
    glb = pl.BlockSpec((HEADS, nch * 8, HEAD_DIM), lambda i: (0, steps - 1 - i, 0))
    return pl.pallas_call(
        body,
        name="gdn_scan_bwd",
        grid=(steps,),
        in_specs=[row, row, row, row, at, glb, pl.BlockSpec((nch, HEADS, HEAD_DIM, HEAD_DIM), lambda i: (steps - 1 - i, 0, 0, 0)), row],
        out_specs=[row, row, row, row, at, glb],
        out_shape=[jax.ShapeDtypeStruct((t, wide), F32)] * 4 + [jax.ShapeDtypeStruct((HEADS, t, CHUNK), F32), jax.ShapeDtypeStruct((HEADS, n * 8, HEAD_DIM), F32)],
        scratch_shapes=[pltpu.VMEM((HEADS, HEAD_DIM, HEAD_DIM), F32)],
        compiler_params=_params("arbitrary"),
    )(w, u, qg, kg, attn, gl, states, do)


SB_Q = 512
SB_K = 256
SB_STEP = 1
SB_DEAD = -105.0


def _sb_scores(q, k):
    z = _dot(q, k, NT) * (HEAD_DIM ** -0.5)
    e = jnp.exp(-jnp.abs(z))
    lb = jnp.minimum(z, 0.0) - jnp.log(1.0 + e)
    return z, e, lb, lb - z


def _tri(n, rel):
    return jnp.where(rel(_iota((n, n), 0), _iota((n, n), 1)), 1.0, 0.0).astype(BF16)


def _lanes(col):
    return jnp.broadcast_to(col, (col.shape[0], LANES))


def _sb_fwd(q, k, v):
    t = q.shape[0]
    bq, bk = min(SB_Q, t), min(SB_K, t)
    nsub, rep = bq // bk, bk // LANES
    nstep = min(SB_STEP, nsub)
    steps_per_tile = nsub // nstep

    def body(q_ref, k_ref, v_ref, o_ref, rt_ref, first_ref):
        h = pl.program_id(0)
        i = pl.program_id(1)
        o_ref[...] = jnp.zeros_like(o_ref)
        rt_ref[...] = jnp.zeros_like(rt_ref)
        after = _tri(bk, lambda r, c: r > c)

        def block(j, r0, diag):
            st = pl.multiple_of(j * bk, bk)
            kv, vv = k_ref[pl.ds(st, bk), :], v_ref[pl.ds(st, bk), :]
            _, _, lb, l1m = _sb_scores(q_ref[r0:, :], kv)
            if diag:
                mask = _iota((bq - r0, bk), 1) + j * bk < _iota((bq - r0, bk), 0) + (r0 + i * bq)
                l1m = jnp.where(mask, l1m, 0.0)
            sums = _two_pass(l1m, after)
            run = rt_ref[r0:, :]
            a = jnp.exp(lb + jnp.tile(run, (1, rep)) + sums)
            if diag:
                a = jnp.where(mask, a, 0.0)
            o_ref[r0:, :] += _dot(a.astype(BF16), vv)
            rt_ref[r0:, :] = run + _lanes(sums[:, 0:1] + l1m[:, 0:1])

        for s in reversed(range(nsub)):
            block(i * nsub + s, s * bk, True)

        def alive(carry):
            u, highest = carry
            return jnp.logical_and(u >= 0, highest > SB_DEAD)

        def step(carry):
            u, _ = carry
            for s in reversed(range(nstep)):
                block(u * nstep + s, 0, False)
            return u - 1, jnp.max(rt_ref[...])

        u_end, _ = lax.while_loop(alive, step, (i * steps_per_tile - 1, jnp.max(rt_ref[...])))
        first_ref[h, i] = u_end + 1

    qb = pl.BlockSpec((bq, HEAD_DIM), lambda h, i: (i, h))
    full = pl.BlockSpec((t, HEAD_DIM), lambda h, i: (0, h))
    return pl.pallas_call(
        body,
        name="sb_fwd",
        grid=(HEADS, t // bq),
        in_specs=[qb, full, full],
        out_specs=[qb, qb, pl.BlockSpec(memory_space=pltpu.SMEM)],
        out_shape=[jax.ShapeDtypeStruct(q.shape, F32), jax.ShapeDtypeStruct(q.shape, F32), jax.ShapeDtypeStruct((HEADS, t // bq), jnp.int32)],
        compiler_params=_params("arbitrary", "arbitrary"),
    )(q, k, v)


def _sb_bwd(q, k, v, rt, first, do):
    t = q.shape[0]
    bq, bk = min(SB_Q, t), min(SB_K, t)
    nsub, rep = bq // bk, bk // LANES
    nstep = min(SB_STEP, nsub)
    steps_per_tile = nsub // nstep
    scale = HEAD_DIM ** -0.5

    def body(first_ref, q_ref, k_ref, v_ref, rt_ref, do_ref, dq_ref, dk_ref, dv_ref, left_ref, pg_ref):
        h = pl.program_id(0)
        i = pl.program_id(1)

        @pl.when(i == 0)
        def _():
            dk_ref[...] = jnp.zeros_like(dk_ref)
            dv_ref[...] = jnp.zeros_like(dv_ref)

        dq_ref[...] = jnp.zeros_like(dq_ref)
        left_ref[...] = jnp.zeros_like(left_ref)
        pg_ref[...] = jnp.zeros_like(pg_ref)
        upto = _tri(bk, lambda r, c: r <= c)

        def block(j, r0, diag):
            st = pl.multiple_of(j * bk, bk)
            kv, vv = k_ref[pl.ds(st, bk), :], v_ref[pl.ds(st, bk), :]
            qv = q_ref[r0:, :]
            dob = do_ref[r0:, :].astype(BF16)
            _, _, lb, l1m = _sb_scores(qv, kv)
            if diag:
                mask = _iota((bq - r0, bk), 1) + j * bk < _iota((bq - r0, bk), 0) + (r0 + i * bq)
                l1m = jnp.where(mask, l1m, 0.0)
            sums = _two_pass(l1m, upto)
            left = left_ref[r0:, :]
            a = jnp.exp(lb + jnp.tile(rt_ref[r0:, :] - left, (1, rep)) - sums)
            if diag:
                a = jnp.where(mask, a, 0.0)
            g = _dot(dob, vv, NT) * a
            dv_ref[pl.ds(st, bk), :] += _dot(a.astype(BF16), dob, TN)
            gsum = _two_pass(g, upto)
            pg = pg_ref[r0:, :]
            dz = g - jnp.exp(lb) * (jnp.tile(pg, (1, rep)) + gsum)
            if diag:
                dz = jnp.where(mask, dz, 0.0)
            dzb = (dz * scale).astype(BF16)
            dk_ref[pl.ds(st, bk), :] += _dot(dzb, qv, TN)
            dq_ref[r0:, :] += _dot(dzb, kv)
            left_ref[r0:, :] = left + _lanes(sums[:, bk - 1:bk])
            pg_ref[r0:, :] = pg + _lanes(gsum[:, bk - 1:bk])

        def step(u, carry):
            for s in range(nstep):
                block(u * nstep + s, 0, False)
            return carry

        lax.fori_loop(first_ref[h, i], i * steps_per_tile, step, 0)
        for s in range(nsub):
            block(i * nsub + s, s * bk, True)

    qb = pl.BlockSpec((bq, HEAD_DIM), lambda h, i: (i, h))
    full = pl.BlockSpec((t, HEAD_DIM), lambda h, i: (0, h))
    return pl.pallas_call(
        body,
        name="sb_bwd",
        grid=(HEADS, t // bq),
        in_specs=[pl.BlockSpec(memory_space=pltpu.SMEM), qb, full, full, qb, qb],
        out_specs=[qb, full, full],
        out_shape=[jax.ShapeDtypeStruct(q.shape, F32)] * 3,
        scratch_shapes=[pltpu.VMEM((bq, LANES), F32), pltpu.VMEM((bq, LANES), F32)],
        compiler_params=_params("arbitrary", "arbitrary"),
    )(first, q, k, v, rt, do)


def _adamw(w, g, m, v, name, tm=256):
    r, c = w.shape
    tm = tm if r % tm == 0 else r

    def body(w_ref, g_ref, m_ref, v_ref, d_ref, nm_ref, nv_ref):
        gv = g_ref[...]
        nm = ADAM_B1 * m_ref[...] + (1.0 - ADAM_B1) * gv
        nv = ADAM_B2 * v_ref[...] + (1.0 - ADAM_B2) * (gv * gv)
        m_hat = nm / (1.0 - ADAM_B1 ** ADAM_STEP)
        v_hat = nv / (1.0 - ADAM_B2 ** ADAM_STEP)
        d_ref[...] = -ADAM_LR * (m_hat / (jnp.sqrt(v_hat) + ADAM_EPS) + ADAM_WD * w_ref[...])
        nm_ref[...] = nm
        nv_ref[...] = nv

    blk = pl.BlockSpec((tm, c), lambda i: (i, 0))
    return pl.pallas_call(
        body,
        name=name,
        grid=(r // tm,),
        in_specs=[blk] * 4,
        out_specs=[blk] * 3,
        out_shape=[jax.ShapeDtypeStruct((r, c), F32)] * 3,
        compiler_params=_params("parallel"),
    )(w, g, m, v)


def _local_step(x, tgt, gains, w_in, small, shards, assemble):
    mix_pre, mix_post, mlp_pre, mlp_post, kv_gain = gains
    w_qkvg, w_ba = w_in
    conv_w, a_log, dt_bias, out_gain = small
    t, d = x.shape
    row = lambda a, i=None: a[i:i + 1] if i is not None else a
    al = jnp.zeros((1, LANES), F32).at[:, HEADS:2 * HEADS].set(a_log)
    dtb = jnp.zeros((1, LANES), F32).at[:, HEADS:2 * HEADS].set(dt_bias)
    og = jnp.tile(out_gain, (1, HEADS))
    full = lambda a: (a, a.shape[1], 0)

    (h0,) = _rowwise("norm_in", _fn_norm, [full(x)], [row(mix_pre, 0)], [(d, BF16)])
    qkvg = _matmul(h0, w_qkvg, "nn", F32, "mm_gdn_in", tk=1024)
    ba = _matmul(h0, w_ba, "nn", F32, "mm_gdn_ba", tk=1024)
    (conv, gq, gk, gv), gathered = _conv_fwd(qkvg, conv_w, shards)
    w_out, w_kv, w_q, w_o, w_up, w_down = assemble(gathered)
    beta, gc, gt = _rowwise("gates", _fn_gates, [full(ba)], [al, dtb], [(LANES, F32)] * 3)
    gcr = jnp.swapaxes(gc[:, HEADS:2 * HEADS], 0, 1).reshape(HEADS, t // CHUNK, 1, CHUNK)
    pw, pu, pqg, pkg, pattn, pgl, ptm = _gdn_prep(gq, gk, gv, beta, gc, gt, gcr)
    o_gdn, states = _gdn_scan(pw, pu, pqg, pkg, pattn, pgl)
    (on,) = _rowwise("out_norm", _fn_outnorm, [full(o_gdn), (qkvg, d, 3)], [og], [(d, BF16)])
    mix0 = _matmul(on, w_out, "nn", F32, "mm_gdn_out", tk=1024)
    x1, h1 = _rowwise("res_a0", _fn_res_norm, [full(x), full(mix0)], [row(mix_post, 0), row(mlp_pre, 0)], [(d, F32), (d, BF16)])
    (a0,) = _matmul(h1, w_up[0], "nn", (BF16,), "mm_up0", tk=1024, epilogue=_relu2_of)
    d0 = _matmul(a0, w_down[0], "nn", F32, "mm_down0")
    x2, hkv, hq = _rowwise("res_b0", _fn_res_norm2, [full(x1), full(d0)], [row(mlp_post, 0), kv_gain, row(mix_pre, 1)], [(d, F32), (d, BF16), (d, BF16)])
    w_k, w_v = w_kv[:, :d], w_kv[:, d:]
    kp = _matmul(hkv, w_k, "nn", BF16, "mm_k", tk=1024)
    vp = _matmul(hkv, w_v, "nn", BF16, "mm_v", tk=1024)
    qp = _matmul(hq, w_q, "nn", BF16, "mm_q", tk=1024)
    o_sb, rt, sb_first = _sb_fwd(qp, kp, vp)
    mix1 = _matmul(o_sb, w_o, "nn", F32, "mm_sb_out", tk=1024)
    x3, h3 = _rowwise("res_a1", _fn_res_norm, [full(x2), full(mix1)], [row(mix_post, 1), row(mlp_pre, 1)], [(d, F32), (d, BF16)])
    (a1,) = _matmul(h3, w_up[1], "nn", (BF16,), "mm_up1", tk=1024, epilogue=_relu2_of)
    d1 = _matmul(a1, w_down[1], "nn", F32, "mm_down1")

    loss, dx3, dd1, g_mlp_post1 = _loss_call(x3, d1, tgt, row(mlp_post, 1))
    (du1,) = _matmul(dd1, w_down[1], "nt", (BF16,), "mm_down1_dx", tk=1024, epilogue=_relu2_cotangent, extras=[a1])
    g_down1 = _matmul(a1, dd1, "tn", F32, "mm_down1_dw")
    dh3 = _matmul(du1, w_up[1], "nt", F32, "mm_up1_dx")
    g_up1 = _matmul(h3, du1, "tn", F32, "mm_up1_dw")
    (dx2, dmix1), (g_mix_post1, g_mlp_pre1) = _rowwise_bwd(
        "res_a1_bwd", _fn_res_norm, [full(x2), full(mix1)], [row(mix_post, 1), row(mlp_pre, 1)], [dx3, dh3], [F32, BF16])
    do_sb = _matmul(dmix1, w_o, "nt", BF16, "mm_sb_out_dx")
    g_o = _matmul(o_sb, dmix1, "tn", F32, "mm_sb_out_dw")
    dqp, dkp, dvp = _sb_bwd(qp, kp, vp, rt, sb_first, do_sb)
    dhq = _matmul(dqp, w_q, "nt", F32, "mm_q_dx")
    g_q = _matmul(hq, dqp, "tn", F32, "mm_q_dw")
    dhkv = _matmul(dvp, w_v, "nt", F32, "mm_v_dx", add=_matmul(dkp, w_k, "nt", F32, "mm_k_dx"))
    g_kv = jnp.concatenate([_matmul(hkv, dkp, "tn", F32, "mm_k_dw"), _matmul(hkv, dvp, "tn", F32, "mm_v_dw")], axis=1)
    (dx1, dd0), (g_mlp_post0, g_kv_gain, g_mix_pre1) = _rowwise_bwd(
        "res_b0_bwd", _fn_res_norm2, [full(x1), full(d0)], [row(mlp_post, 0), kv_gain, row(mix_pre, 1)], [dx2, dhkv, dhq], [F32, BF16])
    (du0,) = _matmul(dd0, w_down[0], "nt", (BF16,), "mm_down0_dx", tk=1024, epilogue=_relu2_cotangent, extras=[a0])
    g_down0 = _matmul(a0, dd0, "tn", F32, "mm_down0_dw")
    dh1 = _matmul(du0, w_up[0], "nt", F32, "mm_up0_dx")
    g_up0 = _matmul(h1, du0, "tn", F32, "mm_up0_dw")
    (dx0, dmix0), (g_mix_post0, g_mlp_pre0) = _rowwise_bwd(
        "res_a0_bwd", _fn_res_norm, [full(x), full(mix0)], [row(mix_post, 0), row(mlp_pre, 0)], [dx1, dh1], [F32, BF16])
    don = _matmul(dmix0, w_out, "nt", F32, "mm_gdn_out_dx")
    g_out = _matmul(on, dmix0, "tn", F32, "mm_gdn_out_dw")
    (do_gdn, dgate), (g_og,) = _rowwise_bwd("out_norm_bwd", _fn_outnorm, [full(o_gdn), (qkvg, d, 3)], [og], [don], [F32, F32])
    dpw, dpu, dpqg, dpkg, dpattn, dpgl = _gdn_scan_bwd(pw, pu, pqg, pkg, pattn, pgl, states, do_gdn)
    dgq, dgk, dgv, dbeta, dgc, dgt, dgcr = _gdn_prep_bwd(gq, gk, gv, beta, gc, gt, gcr, ptm, dpw, dpu, dpqg, dpkg, dpattn, dpgl)
    dgcr_lanes = jnp.pad(jnp.swapaxes(dgcr.reshape(HEADS, t), 0, 1), ((0, 0), (HEADS, LANES - 2 * HEADS)))
    gate_cots = [dbeta, dgc + dgcr_lanes, dgt]
    (dba,), (g_al, g_dtb) = _rowwise_bwd("gates_bwd", _fn_gates, [full(ba)], [al, dtb], gate_cots, [BF16])
    dqkvg, g_conv = _conv_bwd(conv, (dgq, dgk, dgv), dgate, qkvg, conv_w)
    dh0b = _matmul(dba, w_ba, "nt", F32, "mm_gdn_ba_dx", tk=LANES)
    dh0 = _matmul(dqkvg, w_qkvg, "nt", F32, "mm_gdn_in_dx", add=dh0b)
    g_qkvg = _matmul(h0, dqkvg, "tn", F32, "mm_gdn_in_dw")
    g_ba = _matmul(h0, dba, "tn", F32, "mm_gdn_ba_dw")
    (grad_x,), (g_mix_pre0,) = _rowwise_bwd("norm_in_bwd", lambda xx, gg: (_rms(xx, gg), xx), [full(x)], [row(mix_pre, 0)], [dh0, dx0], [F32])

    grads = dict(
        mix_pre_gain=jnp.concatenate([g_mix_pre0, g_mix_pre1], axis=0),
        mix_post_gain=jnp.concatenate([g_mix_post0, g_mix_post1], axis=0),
        mlp_pre_gain=jnp.concatenate([g_mlp_pre0, g_mlp_pre1], axis=0),
        mlp_post_gain=jnp.concatenate([g_mlp_post0, g_mlp_post1], axis=0),
        mlp_w_up=(g_up0, g_up1),
        mlp_w_down=(g_down0, g_down1),
        gdn_w_in=jnp.concatenate([g_qkvg, g_ba[:, :2 * HEADS]], axis=1)[None],
        gdn_conv_w=g_conv[None, :CONV_K],
        gdn_a_log=g_al[:, HEADS:2 * HEADS],
        gdn_dt_bias=g_dtb[:, HEADS:2 * HEADS],
        gdn_out_gain=jnp.sum(g_og.reshape(HEADS, HEAD_DIM), axis=0, keepdims=True),
        gdn_w_out=g_out[None],
        kv_gain=g_kv_gain[0],
        w_kv=g_kv,
        sb_w_q=g_q[None],
        sb_w_o=g_o[None],
    )
    return loss, grad_x, grads


N_DEV = 8
N_CHIPS = 4
PACK_ROW_TILE = 128

_HBM = pl.BlockSpec(memory_space=pltpu.HBM)


def _place():
    return lax.axis_index("x"), lax.axis_index("y"), lax.axis_index("c")


def _other_chips(x, y):
    return [(1 - x, y), (x, 1 - y), (1 - x, 1 - y)]


def _remote(src, dst, send_sem, recv_sem, to):
    return pltpu.make_async_remote_copy(src_ref=src, dst_ref=dst, send_sem=send_sem, recv_sem=recv_sem, device_id=to, device_id_type=MESH)


def _gather8(v, name):
    rows, cols = v.shape

    def body(v_ref, out_ref, sum_ref, send_sems, recv_sems, local_sem):
        x, y, c = _place()
        me, sibling = (x, y, c), (x, y, 1 - c)
        chips = _other_chips(x, y)

        def blk(px, py, pc):
            return out_ref.at[pl.ds((4 * px + 2 * py + pc) * rows, rows), :]

        def copy(k, block, to, src=None):
            return _remote(blk(*block) if src is None else src, blk(*block), send_sems.at[k], recv_sems.at[k], to)

        mine = pltpu.make_async_copy(v_ref, blk(*me), local_sem)
        mine.start()
        first = [copy(0, me, sibling, src=v_ref)] + [copy(1 + j, me, (*chip, c), src=v_ref) for j, chip in enumerate(chips)]
        for cp in first:
            cp.start()
        passed = [copy(4 + j, (*chip, c), sibling) for j, chip in enumerate(chips)]
        for j, chip in enumerate(chips):
            copy(1 + j, (*chip, c), me).wait_recv()
            passed[j].start()
        copy(0, sibling, me).wait_recv()
        for j, chip in enumerate(chips):
            copy(4 + j, (*chip, 1 - c), me).wait_recv()
        for cp in first + passed:
            cp.wait_send()
        mine.wait()
        acc = out_ref[pl.ds(0, rows), :]
        for dev in range(1, N_DEV):
            acc = acc + out_ref[pl.ds(dev * rows, rows), :]
        sum_ref[...] = acc

    vm = pl.BlockSpec(memory_space=pltpu.VMEM)
    return pl.pallas_call(
        body,
        name=name,
        out_shape=[jax.ShapeDtypeStruct((N_DEV * rows, cols), v.dtype), jax.ShapeDtypeStruct((rows, cols), v.dtype)],
        in_specs=[vm],
        out_specs=[vm, vm],
        scratch_shapes=[pltpu.SemaphoreType.DMA((7,)), pltpu.SemaphoreType.DMA((7,)), pltpu.SemaphoreType.DMA],
    )(v)


def _hbm_call(body, name, arrs, out_shapes, sem_counts):
    n = len(arrs)

    def wrapped(*refs):
        body(refs[:n], refs[n:2 * n], *refs[2 * n:])

    return pl.pallas_call(
        wrapped,
        name=name,
        out_shape=[jax.ShapeDtypeStruct(s, a.dtype) for s, a in zip(out_shapes, arrs)],
        in_specs=[_HBM] * n,
        out_specs=[_HBM] * n,
        scratch_shapes=[pltpu.SemaphoreType.DMA((k,)) for k in sem_counts],
    )(*arrs)


def _gather_sends(w_refs, out_refs, send_sems, recv_sems):
    x, y, c = _place()
    s_me = 2 * x + y
    return [_remote(w.at[c], o.at[s_me, c], send_sems.at[3 * a + j], recv_sems.at[3 * a + j], (px, py, c))
            for a, (w, o) in enumerate(zip(w_refs, out_refs)) for j, (px, py) in enumerate(_other_chips(x, y))]


def _gather_finish(w_refs, out_refs, send_sems, recv_sems, fsend_sems, frecv_sems):
    x, y, c = _place()
    chips = _other_chips(x, y)
    passed = []
    for a, o in enumerate(out_refs):
        for j, (px, py) in enumerate(chips):
            half = o.at[2 * px + py, c]
            _remote(half, half, send_sems.at[3 * a + j], recv_sems.at[3 * a + j], (px, py, c)).wait_recv()
            fwd = _remote(half, half, fsend_sems.at[3 * a + j], frecv_sems.at[3 * a + j], (x, y, 1 - c))
            fwd.start()
            passed.append(fwd)
    for a, o in enumerate(out_refs):
        for j, (px, py) in enumerate(chips):
            half = o.at[2 * px + py, 1 - c]
            _remote(half, half, fsend_sems.at[3 * a + j], frecv_sems.at[3 * a + j], (x, y, 1 - c)).wait_recv()
    for cp in _gather_sends(w_refs, out_refs, send_sems, recv_sems) + passed:
        cp.wait_send()


def _gather_weights(arrs):
    n = len(arrs)

    def body(w_refs, out_refs, send_sems, recv_sems, fsend_sems, frecv_sems):
        for cp in _gather_sends(w_refs, out_refs, send_sems, recv_sems):
            cp.start()
        _gather_finish(w_refs, out_refs, send_sems, recv_sems, fsend_sems, frecv_sems)

    return _hbm_call(body, "gather_weights", arrs, [(N_CHIPS,) + a.shape for a in arrs], [3 * n] * 4)


def _swap_halves(arrs):
    n = len(arrs)

    def body(g_refs, a_refs, send_sems, recv_sems):
        x, y, c = _place()
        cps = [_remote(g.at[1 - c], a, send_sems.at[i], recv_sems.at[i], (x, y, 1 - c)) for i, (g, a) in enumerate(zip(g_refs, a_refs))]
        for cp in cps:
            cp.start()
        for cp in cps:
            cp.wait()

    return _hbm_call(body, "grads_to_sibling", arrs, [a.shape[1:] for a in arrs], [n, n])


def _scatter_to_chips(arrs):
    n = len(arrs)

    def body(p_refs, b_refs, send_sems, recv_sems):
        x, y, c = _place()
        cps = [_remote(p.at[2 * px + py], b.at[j], send_sems.at[3 * i + j], recv_sems.at[3 * i + j], (px, py, c))
               for i, (p, b) in enumerate(zip(p_refs, b_refs)) for j, (px, py) in enumerate(_other_chips(x, y))]
        for cp in cps:
            cp.start()
        for cp in cps:
            cp.wait()

    return _hbm_call(body, "grads_to_chips", arrs, [(3,) + a.shape[1:] for a in arrs], [3 * n, 3 * n])


def _share_halves(arrs):
    n = len(arrs)

    def body(q_refs, out_refs, send_sems, recv_sems):
        x, y, c = _place()
        cps = [_remote(q, o, send_sems.at[i], recv_sems.at[i], (x, y, 1 - c)) for i, (q, o) in enumerate(zip(q_refs, out_refs))]
        for cp in cps:
            cp.start()
        for cp in cps:
            cp.wait()

    return _hbm_call(body, "grads_share", arrs, [a.shape for a in arrs], [n, n])


_GROUPS = (
    (("mlp_w_up", (2, 1024, 1024), "cols"), ("mlp_w_down", (2, 1024, 1024), "rows"), ("gdn_w_out", (1, 256, 1024), "rows"),
     ("sb_w_q", (1, 256, 1024), "rows"), ("sb_w_o", (1, 256, 1024), "rows")),
    (("w_kv", (1024, 512), "cols"),),
    (("gdn_w_in", (1, 1024, 1028), "cols"),),
)


def _numel(shape):
    n = 1
    for s in shape:
        n *= s
    return n


def _half_rows(shape):
    return _numel(shape[:-1]) // 2


def _pack_shards(shards, dtype):
    return tuple(jnp.concatenate([shards[n].astype(dtype).reshape(2, _half_rows(shape), shape[-1]) for n, shape, _ in grp], axis=1) for grp in _GROUPS)


def _unpack_shards(bufs):
    out = {}
    for grp, buf in zip(_GROUPS, bufs):
        off = 0
        for n, shape, _ in grp:
            out[n] = buf[:, off:off + _half_rows(shape)].reshape(shape)
            off += _half_rows(shape)
    return out


def _join(stacked, how):
    nd = stacked.ndim - 1
    ax = nd - 1 if how == "cols" else nd - 2
    moved = jnp.moveaxis(stacked, 0, ax)
    shape = list(stacked.shape[1:])
    shape[ax] *= N_CHIPS
    return moved.reshape(shape)


def _split(full, shard_shape, how):
    nd = len(shard_shape)
    ax = nd - 1 if how == "cols" else nd - 2
    shape = list(shard_shape)
    shape.insert(ax, N_CHIPS)
    return jnp.moveaxis(full.reshape(shape), ax, 0)


def _unpack_full(gathered, groups):
    out = {}
    for grp, buf in zip(groups, gathered):
        off = 0
        for n, shape, how in grp:
            out[n] = _join(buf[:, :, off:off + _half_rows(shape)].reshape((N_CHIPS,) + shape), how)
            off += _half_rows(shape)
    return out


def _pack_full(full):
    bufs = []
    for grp in _GROUPS:
        parts = []
        for n, shape, how in grp:
            if isinstance(full[n], tuple):
                assert len(full[n]) == shape[0] == 2
                parts.append(jnp.stack([_split(layer, shape[1:], how) for layer in full[n]], axis=1))
            else:
                parts.append(_split(full[n], shape, how).reshape(N_CHIPS, 2, _half_rows(shape), shape[-1]))
        buf = jnp.swapaxes(jnp.concatenate(parts, axis=2), 0, 1)
        bufs.append(buf.reshape(2, -1, buf.shape[-1]))
    return tuple(bufs)


_SMALL = (
    ("mix_pre_gain", (2, 1024)),
    ("mix_post_gain", (2, 1024)),
    ("mlp_pre_gain", (2, 1024)),
    ("mlp_post_gain", (2, 1024)),
    ("kv_gain", (1024,)),
    ("gdn_out_gain", (1, 128)),
    ("gdn_a_log", (1, 8)),
    ("gdn_dt_bias", (1, 8)),
    ("gdn_conv_w", (1, 4, 3072)),
    ("loss", ()),
)


def _rows_of(shape):
    return -(-_numel(shape) // LANES)


_SMALL_ROWS = -(-sum(_rows_of(s) for _, s in _SMALL) // 8) * 8


def _pack_small(vals):
    parts = []
    for n, shape in _SMALL:
        flat = vals[n].reshape(-1)
        parts.append(jnp.pad(flat, (0, _rows_of(shape) * LANES - flat.shape[0])))
    flat = jnp.concatenate(parts)
    return jnp.pad(flat, (0, _SMALL_ROWS * LANES - flat.shape[0])).reshape(_SMALL_ROWS, LANES)


def _unpack_small(packed):
    flat = packed.reshape(-1)
    out, off = {}, 0
    for n, shape in _SMALL:
        out[n] = flat[off:off + _numel(shape)].reshape(shape)
        off += _rows_of(shape) * LANES
    return out


_WEIGHTS = ("mix_pre_gain", "mix_post_gain", "mlp_pre_gain", "mlp_post_gain", "mlp_w_up", "mlp_w_down", "gdn_w_in", "gdn_conv_w",
            "gdn_a_log", "gdn_dt_bias", "gdn_out_gain", "gdn_w_out", "kv_gain", "w_kv", "sb_w_q", "sb_w_o")


def _as2d(a):
    return a.reshape(1, -1) if a.ndim <= 1 else a.reshape(-1, a.shape[-1])


def kernel(x, mix_pre_gain, mix_post_gain, mlp_pre_gain, mlp_post_gain, mlp_w_up, mlp_w_down, gdn_w_in, gdn_conv_w, gdn_a_log, gdn_dt_bias, gdn_out_gain, gdn_w_out, kv_gain, w_kv, sb_w_q, sb_w_o, loss_target, m_mix_pre_gain, m_mix_post_gain, m_mlp_pre_gain, m_mlp_post_gain, m_mlp_w_up, m_mlp_w_down, m_gdn_w_in, m_gdn_conv_w, m_gdn_a_log, m_gdn_dt_bias, m_gdn_out_gain, m_gdn_w_out, m_kv_gain, m_w_kv, m_sb_w_q, m_sb_w_o, v_mix_pre_gain, v_mix_post_gain, v_mlp_pre_gain, v_mlp_post_gain, v_mlp_w_up, v_mlp_w_down, v_gdn_w_in, v_gdn_conv_w, v_gdn_a_log, v_gdn_dt_bias, v_gdn_out_gain, v_gdn_w_out, v_kv_gain, v_w_kv, v_sb_w_q, v_sb_w_o):
    w = dict(mix_pre_gain=mix_pre_gain, mix_post_gain=mix_post_gain, mlp_pre_gain=mlp_pre_gain, mlp_post_gain=mlp_post_gain, mlp_w_up=mlp_w_up, mlp_w_down=mlp_w_down, gdn_w_in=gdn_w_in, gdn_conv_w=gdn_conv_w, gdn_a_log=gdn_a_log, gdn_dt_bias=gdn_dt_bias, gdn_out_gain=gdn_out_gain, gdn_w_out=gdn_w_out, kv_gain=kv_gain, w_kv=w_kv, sb_w_q=sb_w_q, sb_w_o=sb_w_o)
    m = dict(mix_pre_gain=m_mix_pre_gain, mix_post_gain=m_mix_post_gain, mlp_pre_gain=m_mlp_pre_gain, mlp_post_gain=m_mlp_post_gain, mlp_w_up=m_mlp_w_up, mlp_w_down=m_mlp_w_down, gdn_w_in=m_gdn_w_in, gdn_conv_w=m_gdn_conv_w, gdn_a_log=m_gdn_a_log, gdn_dt_bias=m_gdn_dt_bias, gdn_out_gain=m_gdn_out_gain, gdn_w_out=m_gdn_w_out, kv_gain=m_kv_gain, w_kv=m_w_kv, sb_w_q=m_sb_w_q, sb_w_o=m_sb_w_o)
    v = dict(mix_pre_gain=v_mix_pre_gain, mix_post_gain=v_mix_post_gain, mlp_pre_gain=v_mlp_pre_gain, mlp_post_gain=v_mlp_post_gain, mlp_w_up=v_mlp_w_up, mlp_w_down=v_mlp_w_down, gdn_w_in=v_gdn_w_in, gdn_conv_w=v_gdn_conv_w, gdn_a_log=v_gdn_a_log, gdn_dt_bias=v_gdn_dt_bias, gdn_out_gain=v_gdn_out_gain, gdn_w_out=v_gdn_w_out, kv_gain=v_kv_gain, w_kv=v_w_kv, sb_w_q=v_sb_w_q, sb_w_o=v_sb_w_o)
    cx, cy, cc = _place()
    chip = 2 * cx + cy
    conv_cols = gdn_conv_w.shape[-1]

    own = _pack_shards(w, BF16)
    with_own = lambda gathered, mine: [lax.dynamic_update_index_in_dim(g, m, chip, 0) for g, m in zip(gathered, mine)]
    w_in = _unpack_full(with_own(_gather_weights(own[2:]), own[2:]), _GROUPS[2:])["gdn_w_in"][0]

    def assemble(gathered):
        full = _unpack_full(with_own(gathered, own[:2]), _GROUPS[:2])
        return full["gdn_w_out"][0], full["w_kv"], full["sb_w_q"][0], full["sb_w_o"][0], full["mlp_w_up"], full["mlp_w_down"]

    conv_rows = jnp.pad(gdn_conv_w[0], ((0, 8 - CONV_K), (0, 0))).reshape(-1, LANES)
    conv_all, _ = _gather8(conv_rows, "gather_conv_w")
    conv_all = conv_all.reshape(N_CHIPS, 2, 8, conv_cols)[:, 0, :CONV_K]
    conv_full = jnp.swapaxes(conv_all, 0, 1).reshape(CONV_K, N_CHIPS * conv_cols)

    w_in = (w_in[:, :4 * HEADS * HEAD_DIM], jnp.pad(w_in[:, 4 * HEADS * HEAD_DIM:], ((0, 0), (0, LANES - 2 * HEADS))))
    gains = (mix_pre_gain, mix_post_gain, mlp_pre_gain, mlp_post_gain, kv_gain[None])
    small = (conv_full, gdn_a_log, gdn_dt_bias, gdn_out_gain)
    loss_rows, grad_x, g_full = _local_step(x[0], loss_target[0], gains, w_in, small, own[:2], assemble)

    bufs = _pack_full(g_full)
    from_sibling = _swap_halves(bufs)
    partial, partial_bf16 = [], []
    for i, (buf, other) in enumerate(zip(bufs, from_sibling)):
        cols = buf.shape[-1]
        own_half = lax.dynamic_index_in_dim(buf, cc, 0, keepdims=False)
        p, pb = _rowwise(f"grads_add_sibling_{i}", lambda a, b: (a + b, a + b), [(own_half, cols, 0), (other, cols, 0)], [], [(cols, F32), (cols, BF16)], tm=PACK_ROW_TILE)
        partial.append(p.reshape(N_CHIPS, -1, cols))
        partial_bf16.append(pb.reshape(N_CHIPS, -1, cols))
    from_chips = _scatter_to_chips(tuple(partial_bf16))
    reduced = []
    for i, (p, others) in enumerate(zip(partial, from_chips)):
        cols = p.shape[-1]
        mine = lax.dynamic_index_in_dim(p, chip, 0, keepdims=False)
        (r,) = _rowwise(f"grads_add_chips_{i}", lambda a, b, c, d: (((a + b) + c) + d,),
                        [(mine, cols, 0), (others[0], cols, 0), (others[1], cols, 0), (others[2], cols, 0)], [], [(cols, F32)], tm=PACK_ROW_TILE)
        reduced.append(r)
    g_shard = _unpack_shards([jnp.where(cc == 0, jnp.stack([r, o]), jnp.stack([o, r])) for r, o in zip(reduced, _share_halves(tuple(reduced)))])

    g_small_local = {n: g_full[n] for n, _ in _SMALL if n != "loss"}
    g_small_local["loss"] = loss_rows[0, 0]
    _, small_sum = _gather8(_pack_small(g_small_local), "allreduce_small")
    g_small = _unpack_small(small_sum)
    loss = g_small.pop("loss")
    g_small["gdn_conv_w"] = lax.dynamic_slice_in_dim(g_small["gdn_conv_w"], chip * conv_cols, conv_cols, axis=2)

    grads = {**g_shard, **g_small}
    deltas, new_m, new_v = {}, {}, {}
    for n in _WEIGHTS:
        d2, m2, v2 = _adamw(_as2d(w[n]), _as2d(grads[n]), _as2d(m[n]), _as2d(v[n]), "adamw_" + n)
        deltas[n], new_m[n], new_v[n] = d2.reshape(w[n].shape), m2.reshape(w[n].shape), v2.reshape(w[n].shape)
    return (loss, grad_x[None], *[grads[n].reshape(w[n].shape) for n in _WEIGHTS], *[deltas[n] for n in _WEIGHTS],
            *[new_m[n] for n in _WEIGHTS], *[new_v[n] for n in _WEIGHTS])
```

```python
import functools

import jax
import jax.numpy as jnp
from jax import lax
from jax.experimental import pallas as pl
from jax.experimental.pallas import tpu as pltpu

F32, BF16 = jnp.float32, jnp.bfloat16
HI = lax.Precision.HIGHEST
MESH = pl.DeviceIdType.MESH

EPS = 1e-6
D_MODEL = 1024
HEADS = 8
HEAD_DIM = 128
CHUNK = 64
CHUNK_SHIFT = CHUNK.bit_length() - 1
CONV_K = 4
D_FF = 4096
QKV = 3 * HEADS * HEAD_DIM

ADAM_LR, ADAM_B1, ADAM_B2, ADAM_EPS, ADAM_WD, ADAM_STEP = 0.001, 0.9, 0.999, 1e-08, 0.01, 10

VMEM_LIMIT_BYTES = 48 * 1024 * 1024
LANES = 128

NN = ((1,), (0,))
NT = ((1,), (1,))
TN = ((0,), (0,))


def _dot(a, b, dims=NN, precision=None):
    return lax.dot_general(a, b, (dims, ((), ())), precision=precision, preferred_element_type=F32)


def _params(*sem):
    return pltpu.CompilerParams(dimension_semantics=sem, vmem_limit_bytes=VMEM_LIMIT_BYTES)


def _iota(shape, axis):
    return lax.broadcasted_iota(jnp.int32, shape, axis)


def _matmul(a, b, mode, out_dtype, name, tm=1024, tn=1024, tk=1024, add=None, epilogue=None, extras=()):
    if mode == "nn":
        (m, k), (k2, n) = a.shape, b.shape
    elif mode == "nt":
        (m, k), (n, k2) = a.shape, b.shape
    else:
        (k, m), (k2, n) = a.shape, b.shape
    assert k == k2, (a.shape, b.shape, mode)
    tm, tn, tk = min(tm, m), min(tn, n), min(tk, k)
    assert m % tm == 0 and n % tn == 0 and k % tk == 0, (a.shape, b.shape, mode)
    nk = k // tk
    dims = {"nn": NN, "nt": NT, "tn": TN}[mode]
    tiles = ([add] if add is not None else []) + list(extras)
    out_dtypes = out_dtype if epilogue is not None else (out_dtype,)
    n_in = 2 + len(tiles)

    def finish(acc, extra_refs, o_refs):
        res = (acc,) if epilogue is None else epilogue(acc, *[r[...] for r in extra_refs])
        for o_ref, r in zip(o_refs, res):
            o_ref[...] = r.astype(o_ref.dtype)

    def body(*refs):
        a_ref, b_ref = refs[:2]
        extra_refs = refs[n_in - len(extras):n_in]
        o_refs, acc_ref = refs[n_in:-1], refs[-1]
        prod = _dot(a_ref[...].astype(BF16), b_ref[...].astype(BF16), dims)
        if nk == 1:
            finish(prod + refs[2][...].astype(F32) if add is not None else prod, extra_refs, o_refs)
            return
        kk = pl.program_id(2)

        @pl.when(kk == 0)
        def _():
            acc_ref[...] = refs[2][...].astype(F32) if add is not None else jnp.zeros_like(acc_ref)

        acc_ref[...] += prod

        @pl.when(kk == nk - 1)
        def _():
            finish(acc_ref[...], extra_refs, o_refs)

    a_spec = pl.BlockSpec((tk, tm), lambda i, j, kk: (kk, i)) if mode == "tn" else pl.BlockSpec((tm, tk), lambda i, j, kk: (i, kk))
    b_spec = pl.BlockSpec((tn, tk), lambda i, j, kk: (j, kk)) if mode == "nt" else pl.BlockSpec((tk, tn), lambda i, j, kk: (kk, j))
    o_spec = pl.BlockSpec((tm, tn), lambda i, j, kk: (i, j))
    res = pl.pallas_call(
        body,
        name=name,
        grid=(m // tm, n // tn, nk),
        in_specs=[a_spec, b_spec] + [o_spec] * len(tiles),
        out_specs=[o_spec] * len(out_dtypes),
        out_shape=[jax.ShapeDtypeStruct((m, n), dt) for dt in out_dtypes],
        scratch_shapes=[pltpu.VMEM((tm, tn), F32)],
        compiler_params=_params("parallel", "parallel", "arbitrary"),
    )(a, b, *tiles)
    return res if epilogue is not None else res[0]


def _row_specs(rows, tm):
    return [pl.BlockSpec((tm, w), lambda i, cb=cb: (i, cb)) for _, w, cb in rows]


def _full_spec(p):
    return pl.BlockSpec(p.shape, lambda i: (0,) * p.ndim)


def _rowwise(name, fn, rows, params, outs, tm=256):
    t = rows[0][0].shape[0]
    tm = min(tm, t)
    nr, npar = len(rows), len(params)

    def body(*refs):
        ins = [r[...].astype(F32) for r in refs[:nr]]
        ps = [p[...] for p in refs[nr:nr + npar]]
        res = fn(*ins, *ps)
        for o_ref, r in zip(refs[nr + npar:], res):
            o_ref[...] = r.astype(o_ref.dtype)

    return pl.pallas_call(
        body,
        name=name,
        grid=(t // tm,),
        in_specs=_row_specs(rows, tm) + [_full_spec(p) for p in params],
        out_specs=[pl.BlockSpec((tm, w), lambda i: (i, 0)) for w, _ in outs],
        out_shape=[jax.ShapeDtypeStruct((t, w), dt) for w, dt in outs],
        compiler_params=_params("parallel"),
    )(*[r[0] for r in rows], *params)


def _add_rows(name, terms, n_rows, out_dtypes, tm):
    cols = terms[0][0].shape[1]
    firsts = jnp.stack([jnp.asarray(first, jnp.int32) for _, first in terms])

    def body(firsts_ref, *refs):
        acc = refs[0][...].astype(F32)
        for r in refs[1:len(terms)]:
            acc = acc + r[...].astype(F32)
        for o_ref in refs[len(terms):]:
            o_ref[...] = acc.astype(o_ref.dtype)

    return pl.pallas_call(
        body,
        name=name,
        grid_spec=pltpu.PrefetchScalarGridSpec(
            num_scalar_prefetch=1,
            grid=(n_rows // tm,),
            in_specs=[pl.BlockSpec((tm, cols), lambda i, firsts_ref, k=k: (firsts_ref[k] + i, 0)) for k in range(len(terms))],
            out_specs=[pl.BlockSpec((tm, cols), lambda i, firsts_ref: (i, 0)) for _ in out_dtypes],
        ),
        out_shape=[jax.ShapeDtypeStruct((n_rows, cols), dt) for dt in out_dtypes],
        compiler_params=_params("parallel"),
    )(firsts, *[a for a, _ in terms])


def _rowwise_bwd(name, fn, rows, params, cots, grad_dtypes, tm=256):
    t = rows[0][0].shape[0]
    tm = min(tm, t)
    nr, npar, nc = len(rows), len(params), len(cots)
    want = [j for j, dt in enumerate(grad_dtypes) if dt is not None]
    widths = [rows[j][1] for j in want]
    n_row_outs = len(want)

    def body(*refs):
        i = pl.program_id(0)
        ins = [r[...].astype(F32) for r in refs[:nr]]
        ps = [p[...] for p in refs[nr:nr + npar]]
        cs = tuple(c[...].astype(F32) for c in refs[nr + npar:nr + npar + nc])
        _, vjp = jax.vjp(fn, *ins, *ps)
        gs = vjp(cs)
        outs = refs[nr + npar + nc:]
        for o_ref, j in zip(outs, want):
            o_ref[...] = gs[j].astype(o_ref.dtype)
        pg_refs = outs[n_row_outs:]

        @pl.when(i == 0)
        def _():
            for pg in pg_refs:
                pg[...] = jnp.zeros_like(pg)

        for pg, g in zip(pg_refs, gs[nr:]):
            pg[...] += g

    row_specs = [pl.BlockSpec((tm, w), lambda i: (i, 0)) for w in widths]
    row_shapes = [jax.ShapeDtypeStruct((t, w), grad_dtypes[j]) for j, w in zip(want, widths)]
    res = pl.pallas_call(
        body,
        name=name,
        grid=(t // tm,),
        in_specs=_row_specs(rows, tm) + [_full_spec(p) for p in params] + [pl.BlockSpec((tm, c.shape[1]), lambda i: (i, 0)) for c in cots],
        out_specs=row_specs + [_full_spec(p) for p in params],
        out_shape=row_shapes + [jax.ShapeDtypeStruct(p.shape, F32) for p in params],
        compiler_params=_params("arbitrary"),
    )(*[r[0] for r in rows], *params, *cots)
    return res[:n_row_outs], res[n_row_outs:]


def _rms(x, g):
    return x * lax.rsqrt(jnp.mean(x * x, axis=-1, keepdims=True) + EPS) * g


def _sigmoid(x):
    return 1.0 / (1.0 + jnp.exp(-x))


def _softplus(x):
    return jnp.maximum(x, 0.0) + jnp.log1p(jnp.exp(-jnp.abs(x)))


def _two_pass(x, m):
    hi = x.astype(BF16)
    lo = (x - hi.astype(F32)).astype(BF16)
    return _dot(hi, m) + _dot(lo, m)


def _head_sum_impl(x):
    sums = [jnp.sum(x[:, h * HEAD_DIM:(h + 1) * HEAD_DIM], axis=-1, keepdims=True) for h in range(HEADS)]
    return jnp.concatenate([jnp.broadcast_to(s, (x.shape[0], HEAD_DIM)) for s in sums], axis=1)


@jax.custom_vjp
def _head_sum(x):
    return _head_sum_impl(x)


_head_sum.defvjp(lambda x: (_head_sum_impl(x), None), lambda _, g: (_head_sum_impl(g),))


def _fn_norm(x, g):
    return (_rms(x, g),)


def _fn_gates(ba, al, dt):
    col = _iota((1, LANES), 1)
    g = jnp.where((col >= HEADS) & (col < 2 * HEADS), -jnp.exp(al) * _softplus(ba + dt), 0.0)
    rows = ba.shape[0]
    r, c = _iota((rows, rows), 0), _iota((rows, rows), 1)
    same = (r >> CHUNK_SHIFT) == (c >> CHUNK_SHIFT)
    gc = _dot(jnp.where(same & (r >= c), 1.0, 0.0), g, precision=HI)
    gtot = _dot(jnp.where(same, 1.0, 0.0), g, precision=HI)
    return _sigmoid(ba), gc, gtot


def _fn_post_q(c):
    s = c * _sigmoid(c)
    return (s * lax.rsqrt(_head_sum(s * s) + EPS) * (HEAD_DIM ** -0.5),)


def _fn_post_k(c):
    s = c * _sigmoid(c)
    return (s * lax.rsqrt(_head_sum(s * s) + EPS),)


def _fn_post_v(c):
    return (c * _sigmoid(c),)


def _fn_post(cq, ck, cv):
    return _fn_post_q(cq) + _fn_post_k(ck) + _fn_post_v(cv)


def _fn_outnorm(o, gate, og):
    y = o * lax.rsqrt(_head_sum(o * o) * (1.0 / HEAD_DIM) + EPS) * og
    return (y * (gate * _sigmoid(gate)),)


def _fn_res_norm(x, m, gp, gn):
    x1 = x + _rms(m, gp)
    return x1, _rms(x1, gn)


def _fn_res_norm2(x, m, gp, ga, gb):
    x1 = x + _rms(m, gp)
    return x1, _rms(x1, ga), _rms(x1, gb)


def _relu2_of(u):
    r = jnp.maximum(u, 0.0)
    return (r * r,)


def _relu2_cotangent(da, a):
    return (da * (2.0 * jnp.sqrt(a.astype(F32))),)


def _loss_call(x3, d1, tgt, g, tm=256):
    t, d = x3.shape
    tm = min(tm, t)

    def body(x_ref, d_ref, t_ref, g_ref, loss_ref, dx_ref, dd_ref, dg_ref):
        i = pl.program_id(0)
        y, vjp = jax.vjp(lambda x, dd, gg: x + _rms(dd, gg), x_ref[...], d_ref[...], g_ref[...])
        err = y - t_ref[...]
        lrow = 0.5 * jnp.mean(err * err, axis=-1, keepdims=True)
        dx, dd, dg = vjp(err * (1.0 / d))
        dx_ref[...] = dx
        dd_ref[...] = dd.astype(dd_ref.dtype)

        @pl.when(i == 0)
        def _():
            loss_ref[...] = jnp.zeros_like(loss_ref)
            dg_ref[...] = jnp.zeros_like(dg_ref)

        loss_ref[...] += jnp.broadcast_to(jnp.sum(lrow, axis=0, keepdims=True), loss_ref.shape)
        dg_ref[...] += dg

    row = pl.BlockSpec((tm, d), lambda i: (i, 0))
    return pl.pallas_call(
        body,
        name="loss_head",
        grid=(t // tm,),
        in_specs=[row, row, row, _full_spec(g)],
        out_specs=[pl.BlockSpec((8, LANES), lambda i: (0, 0)), row, row, _full_spec(g)],
        out_shape=[jax.ShapeDtypeStruct((8, LANES), F32), jax.ShapeDtypeStruct((t, d), F32), jax.ShapeDtypeStruct((t, d), BF16), jax.ShapeDtypeStruct(g.shape, F32)],
        compiler_params=_params("arbitrary"),
    )(x3, d1, tgt, g)


HALO = 8


def _conv_fwd(qkvg, conv_w, shards, tm=256):
    t = qkvg.shape[0]
    tm = min(tm, t)
    steps = t // tm
    wide = QKV // 3
    n = len(shards)

    def body(*refs):
        cur_ref, prev_ref, w_ref = refs[:3]
        shard_refs = refs[3:3 + n]
        o_ref, q_ref, k_ref, v_ref = refs[3 + n:7 + n]
        all_refs = refs[7 + n:7 + 2 * n]
        buf, sems = refs[7 + 2 * n], refs[8 + 2 * n:]
        i = pl.program_id(0)

        if n:
            @pl.when(i == 0)
            def _():
                for cp in _gather_sends(shard_refs, all_refs, *sems[:2]):
                    cp.start()

        buf[0:HALO, :] = jnp.where(i > 0, prev_ref[...], 0.0)
        buf[HALO:, :] = cur_ref[...]
        acc = buf[pl.ds(HALO - CONV_K + 1, tm), :] * w_ref[pl.ds(0, 1), :]
        for j in range(1, CONV_K):
            acc = acc + buf[pl.ds(HALO - CONV_K + 1 + j, tm), :] * w_ref[pl.ds(j, 1), :]
        o_ref[...] = acc
        (q_ref[...], k_ref[...], v_ref[...]) = _fn_post(acc[:, 0:wide], acc[:, wide:2 * wide], acc[:, 2 * wide:])

        if n:
            @pl.when(i == steps - 1)
            def _():
                _gather_finish(shard_refs, all_refs, *sems)

    part = pl.BlockSpec((tm, wide), lambda i: (i, 0))
    res = pl.pallas_call(
        body,
        name="conv_fwd",
        grid=(steps,),
        in_specs=[
            pl.BlockSpec((tm, QKV), lambda i: (i, 0)),
            pl.BlockSpec((HALO, QKV), lambda i: (jnp.maximum(i * (tm // HALO) - 1, 0), 0)),
            pl.BlockSpec((CONV_K, QKV), lambda i: (0, 0)),
        ] + [_HBM] * n,
        out_specs=[pl.BlockSpec((tm, QKV), lambda i: (i, 0)), part, part, part] + [_HBM] * n,
        out_shape=[jax.ShapeDtypeStruct((t, QKV), F32)] + [jax.ShapeDtypeStruct((t, wide), F32)] * 3
        + [jax.ShapeDtypeStruct((N_CHIPS,) + s.shape, s.dtype) for s in shards],
        scratch_shapes=[pltpu.VMEM((tm + HALO, QKV), F32)] + [pltpu.SemaphoreType.DMA((3 * n,))] * (4 if n else 0),
        compiler_params=_params("arbitrary"),
    )(qkvg, qkvg, conv_w, *shards)
    return res[:4], res[4:]


def _conv_bwd(conv, dqkv, dgate, qkvg, conv_w, tm=256):
    t = conv.shape[0]
    tm = min(tm, t)
    n = t // tm
    wg = dgate.shape[1]
    wide = QKV // 3

    def conv_cotangent(c_ref, g_refs):
        parts = [c_ref[:, j * wide:(j + 1) * wide] for j in range(3)]
        _, vjp = jax.vjp(_fn_post, *parts)
        return vjp(tuple(g[...] for g in g_refs))

    def body(c_ref, cn_ref, dq_ref, dk_ref, dv_ref, dqn_ref, dkn_ref, dvn_ref, dgate_ref, x_ref, xp_ref, w_ref, dx_ref, dw_ref, bufd, bufx):
        i = pl.program_id(0)
        for j, (cur, nxt) in enumerate(zip(conv_cotangent(c_ref, (dq_ref, dk_ref, dv_ref)), conv_cotangent(cn_ref, (dqn_ref, dkn_ref, dvn_ref)))):
            bufd[0:tm, j * wide:(j + 1) * wide] = cur
            bufd[tm:, j * wide:(j + 1) * wide] = jnp.where(i < n - 1, nxt, 0.0)
        bufx[0:HALO, :] = jnp.where(i > 0, xp_ref[...], 0.0)
        bufx[HALO:, :] = x_ref[...]

        @pl.when(i == 0)
        def _():
            dw_ref[...] = jnp.zeros_like(dw_ref)

        dcv = bufd[0:tm, :]
        acc = bufd[pl.ds(CONV_K - 1, tm), :] * w_ref[pl.ds(0, 1), :]
        for j in range(1, CONV_K):
            acc = acc + bufd[pl.ds(CONV_K - 1 - j, tm), :] * w_ref[pl.ds(j, 1), :]
        dx_ref[:, 0:QKV] = acc.astype(dx_ref.dtype)
        dx_ref[:, QKV:] = dgate_ref[...].astype(dx_ref.dtype)
        for j in range(CONV_K):
            dw_ref[pl.ds(j, 1), :] += jnp.sum(dcv * bufx[pl.ds(HALO - CONV_K + 1 + j, tm), :], axis=0, keepdims=True)

    def cur(width):
        return pl.BlockSpec((tm, width), lambda i: (i, 0))

    def nxt(width):
        return pl.BlockSpec((HALO, width), lambda i: (jnp.minimum((i + 1) * (tm // HALO), t // HALO - 1), 0))

    return pl.pallas_call(
        body,
        name="conv_bwd",
        grid=(n,),
        in_specs=[cur(QKV), nxt(QKV)] + [cur(wide)] * 3 + [nxt(wide)] * 3 + [
            cur(wg),
            cur(QKV),
            pl.BlockSpec((HALO, QKV), lambda i: (jnp.maximum(i * (tm // HALO) - 1, 0), 0)),
            pl.BlockSpec((CONV_K, QKV), lambda i: (0, 0)),
        ],
        out_specs=[pl.BlockSpec((tm, QKV + wg), lambda i: (i, 0)), pl.BlockSpec((HALO, QKV), lambda i: (0, 0))],
        out_shape=[jax.ShapeDtypeStruct((t, QKV + wg), BF16), jax.ShapeDtypeStruct((HALO, QKV), F32)],
        scratch_shapes=[pltpu.VMEM((tm + HALO, QKV), F32), pltpu.VMEM((tm + HALO, QKV), F32)],
        compiler_params=_params("arbitrary"),
    )(conv, conv, *dqkv, *dqkv, dgate, qkvg, qkvg, conv_w)


PREP_CHUNKS = 16
PREP_BWD_CHUNKS = 4
SCAN_CHUNKS = 4


def _hi_lo(x):
    hi = x.astype(BF16)
    return hi, (x - hi.astype(F32)).astype(BF16)


def _mm3(a, b, dims=NN):
    (ah, al), (bh, bl) = _hi_lo(a), _hi_lo(b)
    return _dot(ah, bh, dims) + (_dot(ah, bl, dims) + _dot(al, bh, dims))


def _neumann(lowers):
    c = lowers[0].shape[0]
    eye = jnp.where(_iota((c, c), 0) == _iota((c, c), 1), 1.0, 0.0)
    ps = [-low for low in lowers]
    tmats = [eye + p for p in ps]
    for _ in range(CHUNK_SHIFT - 1):
        ps = [_mm3(p, p) for p in ps]
        tmats = [t + _mm3(t, p) for t, p in zip(tmats, ps)]
    return tuple(tmats)


def _inv_cotangents(tmats, dts):
    half = [_mm3(t, dt, TN) for t, dt in zip(tmats, dts)]
    return tuple(-_mm3(hf, t, NT) for hf, t in zip(half, tmats))


@jax.custom_vjp
def _tri_inv(lowers):
    return _neumann(lowers)


def _tri_inv_fwd(lowers):
    tmats = _neumann(lowers)
    return tmats, tmats


_tri_inv.defvjp(_tri_inv_fwd, lambda tmats, dts: (_inv_cotangents(tmats, dts),))


@jax.custom_vjp
def _tri_inv_known(lowers, tmats):
    return tmats


_tri_inv_known.defvjp(lambda lowers, tmats: (tmats, tmats),
                      lambda tmats, dts: (_inv_cotangents(tmats, dts), tuple(jnp.zeros_like(t) for t in tmats)))


def _prep_chunks(qs, ks, vs, bs, gcs, gts, gcrs, tmats=None):
    c = CHUNK
    r, col = _iota((c, c), 0), _iota((c, c), 1)
    incl, strict = r >= col, r > col
    decays = [jnp.where(incl, jnp.exp(jnp.where(incl, gc - gcr, 0.0)), 0.0) for gc, gcr in zip(gcs, gcrs)]
    kbs = [k * b for k, b in zip(ks, bs)]
    kbfs = [k.astype(BF16) for k in ks]
    lowers = tuple(jnp.where(strict, _dot(kb.astype(BF16), kbf, NT) * decay, 0.0) for kb, kbf, decay in zip(kbs, kbfs, decays))
    tmats = _tri_inv(lowers) if tmats is None else _tri_inv_known(lowers, tuple(tmats))
    outs = []
    for q, k, v, b, gc, gt, kb, kbf, decay, tmat in zip(qs, ks, vs, bs, gcs, gts, kbs, kbfs, decays, tmats):
        tb = tmat.astype(BF16)
        egc = jnp.exp(gc)
        w = _dot(tb, (kb * egc).astype(BF16))
        u = _dot(tb, (v * b).astype(BF16))
        attn = _dot(q.astype(BF16), kbf, NT) * decay
        gl = jnp.broadcast_to(jnp.exp(jnp.mean(gt.reshape(c // 8, 8, 1), axis=0)), (8, HEAD_DIM))
        outs.append((w, u, q * egc, k * jnp.exp(gt - gc), attn, gl))
    return tuple(outs), tmats


def _prep_specs(rows, gch):
    head = pl.BlockSpec((rows, HEAD_DIM), lambda n, h: (n, h))
    gates = pl.BlockSpec((rows, LANES), lambda n, h: (n, 0))
    gcrow = pl.BlockSpec((1, gch, 1, CHUNK), lambda n, h: (h, n, 0, 0))
    square = pl.BlockSpec((1, rows, CHUNK), lambda n, h: (h, n, 0))
    gl = pl.BlockSpec((1, gch * 8, HEAD_DIM), lambda n, h: (h, n, 0))
    return head, gates, gcrow, square, gl


def _pick_lane(ref, sl, lane):
    return jnp.sum(jnp.where(_iota((1, LANES), 1) == lane, ref[sl, :], 0.0), axis=1, keepdims=True)


def _prep_inputs(q_ref, k_ref, v_ref, b_ref, gc_ref, gt_ref, gcr_ref, sls, h):
    return ([q_ref[sl, :] for sl in sls], [k_ref[sl, :] for sl in sls], [v_ref[sl, :] for sl in sls],
            [_pick_lane(b_ref, sl, h) for sl in sls], [_pick_lane(gc_ref, sl, h + HEADS) for sl in sls],
            [_pick_lane(gt_ref, sl, h + HEADS) for sl in sls], [gcr_ref[0, c] for c in range(len(sls))])


def _gdn_prep(q, k, v, beta, gc, gt, gcr):
    t = q.shape[0]
    gch = min(PREP_CHUNKS, t // CHUNK)
    rows = gch * CHUNK

    def body(q_ref, k_ref, v_ref, b_ref, gc_ref, gt_ref, gcr_ref, w_ref, u_ref, qg_ref, kg_ref, at_ref, gl_ref, tm_ref):
        h = pl.program_id(1)
        sls = [pl.ds(c * CHUNK, CHUNK) for c in range(gch)]
        outs, tmats = _prep_chunks(*_prep_inputs(q_ref, k_ref, v_ref, b_ref, gc_ref, gt_ref, gcr_ref, sls, h))
        for c, (sl, (w, u, qg, kg, attn, gl), tmat) in enumerate(zip(sls, outs, tmats)):
            w_ref[sl, :] = w.astype(BF16)
            u_ref[sl, :] = u
            qg_ref[sl, :] = qg.astype(BF16)
            kg_ref[sl, :] = kg.astype(BF16)
            at_ref[0, sl, :] = attn.astype(BF16)
            gl_ref[0, pl.ds(c * 8, 8), :] = gl
            tm_ref[0, sl, :] = tmat

    hb, col, gcrow, square, glb = _prep_specs(rows, gch)
    wide = HEADS * HEAD_DIM
    return pl.pallas_call(
        body,
        name="gdn_prep",
        grid=(t // rows, HEADS),
        in_specs=[hb, hb, hb, col, col, col, gcrow],
        out_specs=[hb, hb, hb, hb, square, glb, square],
        out_shape=[
            jax.ShapeDtypeStruct((t, wide), BF16),
            jax.ShapeDtypeStruct((t, wide), F32),
            jax.ShapeDtypeStruct((t, wide), BF16),
            jax.ShapeDtypeStruct((t, wide), BF16),
            jax.ShapeDtypeStruct((HEADS, t, CHUNK), BF16),
            jax.ShapeDtypeStruct((HEADS, t // CHUNK * 8, HEAD_DIM), F32),
            jax.ShapeDtypeStruct((HEADS, t, CHUNK), F32),
        ],
        compiler_params=_params("parallel", "parallel"),
    )(q, k, v, beta, gc, gt, gcr)


def _gdn_prep_bwd(q, k, v, beta, gc, gt, gcr, tmat, dw, du, dqg, dkg, dattn, dgl):
    t = q.shape[0]
    gch = min(PREP_BWD_CHUNKS, t // CHUNK)
    rows = gch * CHUNK

    def body(q_ref, k_ref, v_ref, b_ref, gc_ref, gt_ref, gcr_ref, tm_ref, dw_ref, du_ref, dqg_ref, dkg_ref, dat_ref, dgl_ref,
             dq_ref, dk_ref, dv_ref, db_ref, dgc_ref, dgt_ref, dgcr_ref):
        h = pl.program_id(1)
        lane = _iota((1, LANES), 1)

        @pl.when(h == 0)
        def _():
            db_ref[...] = jnp.zeros_like(db_ref)
            dgc_ref[...] = jnp.zeros_like(dgc_ref)
            dgt_ref[...] = jnp.zeros_like(dgt_ref)

        sls = [pl.ds(c * CHUNK, CHUNK) for c in range(gch)]
        known = [tm_ref[0, sl, :] for sl in sls]
        _, vjp = jax.vjp(lambda *a: _prep_chunks(*a, tmats=known)[0], *_prep_inputs(q_ref, k_ref, v_ref, b_ref, gc_ref, gt_ref, gcr_ref, sls, h))
        cots = tuple((dw_ref[sl, :], du_ref[sl, :], dqg_ref[sl, :], dkg_ref[sl, :], dat_ref[0, sl, :], dgl_ref[0, pl.ds(c * 8, 8), :]) for c, sl in enumerate(sls))
        dqs, dks, dvs, dbs, dgcs, dgts, dgcrs = vjp(cots)
        for c, sl in enumerate(sls):
            dq_ref[sl, :] = dqs[c]
            dk_ref[sl, :] = dks[c]
            dv_ref[sl, :] = dvs[c]
            db_ref[sl, :] += jnp.where(lane == h, dbs[c], 0.0)
            dgc_ref[sl, :] += jnp.where(lane == h + HEADS, dgcs[c], 0.0)
            dgt_ref[sl, :] += jnp.where(lane == h + HEADS, dgts[c], 0.0)
            dgcr_ref[0, c] = dgcrs[c]

    hb, col, gcrow, square, glb = _prep_specs(rows, gch)
    wide = HEADS * HEAD_DIM
    return pl.pallas_call(
        body,
        name="gdn_prep_bwd",
        grid=(t // rows, HEADS),
        in_specs=[hb, hb, hb, col, col, col, gcrow, square, hb, hb, hb, hb, square, glb],
        out_specs=[hb, hb, hb, col, col, col, gcrow],
        out_shape=[jax.ShapeDtypeStruct((t, wide), F32)] * 3 + [jax.ShapeDtypeStruct((t, LANES), F32)] * 3 + [jax.ShapeDtypeStruct((HEADS, t // CHUNK, 1, CHUNK), F32)],
        compiler_params=_params("parallel", "arbitrary"),
    )(q, k, v, beta, gc, gt, gcr, tmat, dw, du, dqg, dkg, dattn, dgl)


def _gdn_scan(w, u, qg, kg, attn, gl):
    t = w.shape[0]
    n = t // CHUNK
    nch = min(SCAN_CHUNKS, n)
    wide = HEADS * HEAD_DIM

    def body(w_ref, u_ref, qg_ref, kg_ref, at_ref, gl_ref, o_ref, st_ref, s_ref):
        @pl.when(pl.program_id(0) == 0)
        def _():
            s_ref[...] = jnp.zeros_like(s_ref)

        heads = range(HEADS)
        cols = [pl.ds(h * HEAD_DIM, HEAD_DIM) for h in heads]
        for c in range(nch):
            rows, gl_rows = pl.ds(c * CHUNK, CHUNK), pl.ds(c * 8, 8)
            ss = [s_ref[h] for h in heads]
            sbs = [s.astype(BF16) for s in ss]
            vbs = [(u_ref[rows, hs] - _dot(w_ref[rows, hs], sb)).astype(BF16) for hs, sb in zip(cols, sbs)]
            outs = [_dot(qg_ref[rows, hs], sb) + _dot(at_ref[h, rows, :], vb) for h, hs, sb, vb in zip(heads, cols, sbs, vbs)]
            new = [s * jnp.tile(gl_ref[h, gl_rows, :], (HEAD_DIM // 8, 1)) + _dot(kg_ref[rows, hs], vb, TN) for h, hs, s, vb in zip(heads, cols, ss, vbs)]
            for h, hs in zip(heads, cols):
                st_ref[c, h] = ss[h]
                o_ref[rows, hs] = outs[h]
                s_ref[h] = new[h]

    row = pl.BlockSpec((nch * CHUNK, wide), lambda i: (i, 0))
    return pl.pallas_call(
        body,
        name="gdn_scan",
        grid=(n // nch,),
        in_specs=[row, row, row, row, pl.BlockSpec((HEADS, nch * CHUNK, CHUNK), lambda i: (0, i, 0)), pl.BlockSpec((HEADS, nch * 8, HEAD_DIM), lambda i: (0, i, 0))],
        out_specs=[row, pl.BlockSpec((nch, HEADS, HEAD_DIM, HEAD_DIM), lambda i: (i, 0, 0, 0))],
        out_shape=[jax.ShapeDtypeStruct((t, wide), F32), jax.ShapeDtypeStruct((n, HEADS, HEAD_DIM, HEAD_DIM), F32)],
        scratch_shapes=[pltpu.VMEM((HEADS, HEAD_DIM, HEAD_DIM), F32)],
        compiler_params=_params("arbitrary"),
    )(w, u, qg, kg, attn, gl)


def _gdn_scan_bwd(w, u, qg, kg, attn, gl, states, do):
    t = w.shape[0]
    n = t // CHUNK
    nch = min(SCAN_CHUNKS, n)
    steps = n // nch
    wide = HEADS * HEAD_DIM

    def body(w_ref, u_ref, qg_ref, kg_ref, at_ref, gl_ref, st_ref, do_ref, dw_ref, du_ref, dqg_ref, dkg_ref, dat_ref, dgl_ref, ds_ref):
        @pl.when(pl.program_id(0) == 0)
        def _():
            ds_ref[...] = jnp.zeros_like(ds_ref)

        heads = range(HEADS)
        cols = [pl.ds(h * HEAD_DIM, HEAD_DIM) for h in heads]
        for c in reversed(range(nch)):
            rows, gl_rows = pl.ds(c * CHUNK, CHUNK), pl.ds(c * 8, 8)
            ss = [st_ref[c, h] for h in heads]
            sbs = [s.astype(BF16) for s in ss]
            dsns = [ds_ref[h] for h in heads]
            dsbs = [d.astype(BF16) for d in dsns]
            dobs = [do_ref[rows, hs].astype(BF16) for hs in cols]
            vbs = [(u_ref[rows, hs] - _dot(w_ref[rows, hs], sb)).astype(BF16) for hs, sb in zip(cols, sbs)]
            dvns = [_dot(at_ref[h, rows, :], dob, TN) + _dot(kg_ref[rows, hs], dsb) for h, hs, dob, dsb in zip(heads, cols, dobs, dsbs)]
            dvbs = [d.astype(BF16) for d in dvns]
            for h, hs in zip(heads, cols):
                dat_ref[h, rows, :] = _dot(dobs[h], vbs[h], NT)
                dqg_ref[rows, hs] = _dot(dobs[h], sbs[h], NT)
                dkg_ref[rows, hs] = _dot(vbs[h], dsbs[h], NT)
                du_ref[rows, hs] = dvns[h]
                dw_ref[rows, hs] = -_dot(dvbs[h], sbs[h], NT)
                dgl_ref[h, gl_rows, :] = jnp.sum((dsns[h] * ss[h]).reshape(HEAD_DIM // 8, 8, HEAD_DIM), axis=0)
            new = [dsn * jnp.tile(gl_ref[h, gl_rows, :], (HEAD_DIM // 8, 1)) + _dot(qg_ref[rows, hs], dob, TN) - _dot(w_ref[rows, hs], dvb, TN)
                   for h, hs, dsn, dob, dvb in zip(heads, cols, dsns, dobs, dvbs)]
            for h in heads:
                ds_ref[h] = new[h]

    row = pl.BlockSpec((nch * CHUNK, wide), lambda i: (steps - 1 - i, 0))
    at = pl.BlockSpec((HEADS, nch * CHUNK, CHUNK), lambda i: (0, steps - 1 - i, 0))
    glb = pl.BlockSpec((HEADS, nch * 8, HEAD_DIM), lambda i: (0, steps - 1 - i, 0))
    return pl.pallas_call(
        body,
        name="gdn_scan_bwd",
        grid=(steps,),
        in_specs=[row, row, row, row, at, glb, pl.BlockSpec((nch, HEADS, HEAD_DIM, HEAD_DIM), lambda i: (steps - 1 - i, 0, 0, 0)), row],
        out_specs=[row, row, row, row, at, glb],
        out_shape=[jax.ShapeDtypeStruct((t, wide), F32)] * 4 + [jax.ShapeDtypeStruct((HEADS, t, CHUNK), F32), jax.ShapeDtypeStruct((HEADS, n * 8, HEAD_DIM), F32)],
        scratch_shapes=[pltpu.VMEM((HEADS, HEAD_DIM, HEAD_DIM), F32)],
        compiler_params=_params("arbitrary"),
    )(w, u, qg, kg, attn, gl, states, do)


SB_Q = 512
SB_K = 256
SB_STEP = 1
SB_DEAD = -105.0


def _sb_scores(q, k):
    z = _dot(q, k, NT) * (HEAD_DIM ** -0.5)
    e = jnp.exp(-jnp.abs(z))
    lb = jnp.minimum(z, 0.0) - jnp.log(1.0 + e)
    return z, e, lb, lb - z


def _tri(n, rel):
    return jnp.where(rel(_iota((n, n), 0), _iota((n, n), 1)), 1.0, 0.0).astype(BF16)


def _lanes(col):
    return jnp.broadcast_to(col, (col.shape[0], LANES))


def _sb_fwd(q, k, v):
    t = q.shape[0]
    bq, bk = min(SB_Q, t), min(SB_K, t)
    nsub, rep = bq // bk, bk // LANES
    nstep = min(SB_STEP, nsub)
    steps_per_tile = nsub // nstep

    def body(q_ref, k_ref, v_ref, o_ref, rt_ref, first_ref):
        h = pl.program_id(0)
        i = pl.program_id(1)
        o_ref[...] = jnp.zeros_like(o_ref)
        rt_ref[...] = jnp.zeros_like(rt_ref)
        after = _tri(bk, lambda r, c: r > c)

        def block(j, r0, diag):
            st = pl.multiple_of(j * bk, bk)
            kv, vv = k_ref[pl.ds(st, bk), :], v_ref[pl.ds(st, bk), :]
            _, _, lb, l1m = _sb_scores(q_ref[r0:, :], kv)
            if diag:
                mask = _iota((bq - r0, bk), 1) + j * bk < _iota((bq - r0, bk), 0) + (r0 + i * bq)
                l1m = jnp.where(mask, l1m, 0.0)
            sums = _two_pass(l1m, after)
            run = rt_ref[r0:, :]
            a = jnp.exp(lb + jnp.tile(run, (1, rep)) + sums)
            if diag:
                a = jnp.where(mask, a, 0.0)
            o_ref[r0:, :] += _dot(a.astype(BF16), vv)
            rt_ref[r0:, :] = run + _lanes(sums[:, 0:1] + l1m[:, 0:1])

        for s in reversed(range(nsub)):
            block(i * nsub + s, s * bk, True)

        def alive(carry):
            u, highest = carry
            return jnp.logical_and(u >= 0, highest > SB_DEAD)

        def step(carry):
            u, _ = carry
            for s in reversed(range(nstep)):
                block(u * nstep + s, 0, False)
            return u - 1, jnp.max(rt_ref[...])

        u_end, _ = lax.while_loop(alive, step, (i * steps_per_tile - 1, jnp.max(rt_ref[...])))
        first_ref[h, i] = u_end + 1

    qb = pl.BlockSpec((bq, HEAD_DIM), lambda h, i: (i, h))
    full = pl.BlockSpec((t, HEAD_DIM), lambda h, i: (0, h))
    return pl.pallas_call(
        body,
        name="sb_fwd",
        grid=(HEADS, t // bq),
        in_specs=[qb, full, full],
        out_specs=[qb, qb, pl.BlockSpec(memory_space=pltpu.SMEM)],
        out_shape=[jax.ShapeDtypeStruct(q.shape, F32), jax.ShapeDtypeStruct(q.shape, F32), jax.ShapeDtypeStruct((HEADS, t // bq), jnp.int32)],
        compiler_params=_params("arbitrary", "arbitrary"),
    )(q, k, v)


def _sb_bwd(q, k, v, rt, first, do):
    t = q.shape[0]
    bq, bk = min(SB_Q, t), min(SB_K, t)
    nsub, rep = bq // bk, bk // LANES
    nstep = min(SB_STEP, nsub)
    steps_per_tile = nsub // nstep
    scale = HEAD_DIM ** -0.5

    def body(first_ref, q_ref, k_ref, v_ref, rt_ref, do_ref, dq_ref, dk_ref, dv_ref, left_ref, pg_ref):
        h = pl.program_id(0)
        i = pl.program_id(1)

        @pl.when(i == 0)
        def _():
            dk_ref[...] = jnp.zeros_like(dk_ref)
            dv_ref[...] = jnp.zeros_like(dv_ref)

        dq_ref[...] = jnp.zeros_like(dq_ref)
        left_ref[...] = jnp.zeros_like(left_ref)
        pg_ref[...] = jnp.zeros_like(pg_ref)
        upto = _tri(bk, lambda r, c: r <= c)

        def block(j, r0, diag):
            st = pl.multiple_of(j * bk, bk)
            kv, vv = k_ref[pl.ds(st, bk), :], v_ref[pl.ds(st, bk), :]
            qv = q_ref[r0:, :]
            dob = do_ref[r0:, :].astype(BF16)
            _, _, lb, l1m = _sb_scores(qv, kv)
            if diag:
                mask = _iota((bq - r0, bk), 1) + j * bk < _iota((bq - r0, bk), 0) + (r0 + i * bq)
                l1m = jnp.where(mask, l1m, 0.0)
            sums = _two_pass(l1m, upto)
            left = left_ref[r0:, :]
            a = jnp.exp(lb + jnp.tile(rt_ref[r0:, :] - left, (1, rep)) - sums)
            if diag:
                a = jnp.where(mask, a, 0.0)
            g = _dot(dob, vv, NT) * a
            dv_ref[pl.ds(st, bk), :] += _dot(a.astype(BF16), dob, TN)
            gsum = _two_pass(g, upto)
            pg = pg_ref[r0:, :]
            dz = g - jnp.exp(lb) * (jnp.tile(pg, (1, rep)) + gsum)
            if diag:
                dz = jnp.where(mask, dz, 0.0)
            dzb = (dz * scale).astype(BF16)
            dk_ref[pl.ds(st, bk), :] += _dot(dzb, qv, TN)
            dq_ref[r0:, :] += _dot(dzb, kv)
            left_ref[r0:, :] = left + _lanes(sums[:, bk - 1:bk])
            pg_ref[r0:, :] = pg + _lanes(gsum[:, bk - 1:bk])

        def step(u, carry):
            for s in range(nstep):
                block(u * nstep + s, 0, False)
            return carry

        lax.fori_loop(first_ref[h, i], i * steps_per_tile, step, 0)
        for s in range(nsub):
            block(i * nsub + s, s * bk, True)

    qb = pl.BlockSpec((bq, HEAD_DIM), lambda h, i: (i, h))
    full = pl.BlockSpec((t, HEAD_DIM), lambda h, i: (0, h))
    return pl.pallas_call(
        body,
        name="sb_bwd",
        grid=(HEADS, t // bq),
        in_specs=[pl.BlockSpec(memory_space=pltpu.SMEM), qb, full, full, qb, qb],
        out_specs=[qb, full, full],
        out_shape=[jax.ShapeDtypeStruct(q.shape, F32)] * 3,
        scratch_shapes=[pltpu.VMEM((bq, LANES), F32), pltpu.VMEM((bq, LANES), F32)],
        compiler_params=_params("arbitrary", "arbitrary"),
    )(first, q, k, v, rt, do)


def _adamw(w, g, m, v, name, tm=256):
    r, c = w.shape
    tm = tm if r % tm == 0 else r

    def body(w_ref, g_ref, m_ref, v_ref, d_ref, nm_ref, nv_ref):
        gv = g_ref[...]
        nm = ADAM_B1 * m_ref[...] + (1.0 - ADAM_B1) * gv
        nv = ADAM_B2 * v_ref[...] + (1.0 - ADAM_B2) * (gv * gv)
        m_hat = nm / (1.0 - ADAM_B1 ** ADAM_STEP)
        v_hat = nv / (1.0 - ADAM_B2 ** ADAM_STEP)
        d_ref[...] = -ADAM_LR * (m_hat / (jnp.sqrt(v_hat) + ADAM_EPS) + ADAM_WD * w_ref[...])
        nm_ref[...] = nm
        nv_ref[...] = nv

    blk = pl.BlockSpec((tm, c), lambda i: (i, 0))
    return pl.pallas_call(
        body,
        name=name,
        grid=(r // tm,),
        in_specs=[blk] * 4,
        out_specs=[blk] * 3,
        out_shape=[jax.ShapeDtypeStruct((r, c), F32)] * 3,
        compiler_params=_params("parallel"),
    )(w, g, m, v)


def _local_step(x, tgt, gains, w_in, small, shards, assemble):
    mix_pre, mix_post, mlp_pre, mlp_post, kv_gain = gains
    w_qkvg, w_ba = w_in
    conv_w, a_log, dt_bias, out_gain = small
    t, d = x.shape
    row = lambda a, i=None: a[i:i + 1] if i is not None else a
    al = jnp.zeros((1, LANES), F32).at[:, HEADS:2 * HEADS].set(a_log)
    dtb = jnp.zeros((1, LANES), F32).at[:, HEADS:2 * HEADS].set(dt_bias)
    og = jnp.tile(out_gain, (1, HEADS))
    full = lambda a: (a, a.shape[1], 0)

    (h0,) = _rowwise("norm_in", _fn_norm, [full(x)], [row(mix_pre, 0)], [(d, BF16)])
    qkvg = _matmul(h0, w_qkvg, "nn", F32, "mm_gdn_in", tk=1024)
    ba = _matmul(h0, w_ba, "nn", F32, "mm_gdn_ba", tk=1024)
    (conv, gq, gk, gv), gathered = _conv_fwd(qkvg, conv_w, shards)
    w_out, w_kv, w_q, w_o, w_up, w_down = assemble(gathered)
    beta, gc, gt = _rowwise("gates", _fn_gates, [full(ba)], [al, dtb], [(LANES, F32)] * 3)
    gcr = jnp.swapaxes(gc[:, HEADS:2 * HEADS], 0, 1).reshape(HEADS, t // CHUNK, 1, CHUNK)
    pw, pu, pqg, pkg, pattn, pgl, ptm = _gdn_prep(gq, gk, gv, beta, gc, gt, gcr)
    o_gdn, states = _gdn_scan(pw, pu, pqg, pkg, pattn, pgl)
    (on,) = _rowwise("out_norm", _fn_outnorm, [full(o_gdn), (qkvg, d, 3)], [og], [(d, BF16)])
    mix0 = _matmul(on, w_out, "nn", F32, "mm_gdn_out", tk=1024)
    x1, h1 = _rowwise("res_a0", _fn_res_norm, [full(x), full(mix0)], [row(mix_post, 0), row(mlp_pre, 0)], [(d, F32), (d, BF16)])
    (a0,) = _matmul(h1, w_up[0], "nn", (BF16,), "mm_up0", tk=1024, epilogue=_relu2_of)
    d0 = _matmul(a0, w_down[0], "nn", F32, "mm_down0")
    x2, hkv, hq = _rowwise("res_b0", _fn_res_norm2, [full(x1), full(d0)], [row(mlp_post, 0), kv_gain, row(mix_pre, 1)], [(d, F32), (d, BF16), (d, BF16)])
    w_k, w_v = w_kv[:, :d], w_kv[:, d:]
    kp = _matmul(hkv, w_k, "nn", BF16, "mm_k", tk=1024)
    vp = _matmul(hkv, w_v, "nn", BF16, "mm_v", tk=1024)
    qp = _matmul(hq, w_q, "nn", BF16, "mm_q", tk=1024)
    o_sb, rt, sb_first = _sb_fwd(qp, kp, vp)
    mix1 = _matmul(o_sb, w_o, "nn", F32, "mm_sb_out", tk=1024)
    x3, h3 = _rowwise("res_a1", _fn_res_norm, [full(x2), full(mix1)], [row(mix_post, 1), row(mlp_pre, 1)], [(d, F32), (d, BF16)])
    (a1,) = _matmul(h3, w_up[1], "nn", (BF16,), "mm_up1", tk=1024, epilogue=_relu2_of)
    d1 = _matmul(a1, w_down[1], "nn", F32, "mm_down1")

    loss, dx3, dd1, g_mlp_post1 = _loss_call(x3, d1, tgt, row(mlp_post, 1))
    (du1,) = _matmul(dd1, w_down[1], "nt", (BF16,), "mm_down1_dx", tk=1024, epilogue=_relu2_cotangent, extras=[a1])
    g_down1 = _matmul(a1, dd1, "tn", F32, "mm_down1_dw")
    dh3 = _matmul(du1, w_up[1], "nt", F32, "mm_up1_dx")
    g_up1 = _matmul(h3, du1, "tn", F32, "mm_up1_dw")
    (dx2, dmix1), (g_mix_post1, g_mlp_pre1) = _rowwise_bwd(
        "res_a1_bwd", _fn_res_norm, [full(x2), full(mix1)], [row(mix_post, 1), row(mlp_pre, 1)], [dx3, dh3], [F32, BF16])
    do_sb = _matmul(dmix1, w_o, "nt", BF16, "mm_sb_out_dx")
    g_o = _matmul(o_sb, dmix1, "tn", F32, "mm_sb_out_dw")
    dqp, dkp, dvp = _sb_bwd(qp, kp, vp, rt, sb_first, do_sb)
    dhq = _matmul(dqp, w_q, "nt", F32, "mm_q_dx")
    g_q = _matmul(hq, dqp, "tn", F32, "mm_q_dw")
    dhkv = _matmul(dvp, w_v, "nt", F32, "mm_v_dx", add=_matmul(dkp, w_k, "nt", F32, "mm_k_dx"))
    g_kv = jnp.concatenate([_matmul(hkv, dkp, "tn", F32, "mm_k_dw"), _matmul(hkv, dvp, "tn", F32, "mm_v_dw")], axis=1)
    (dx1, dd0), (g_mlp_post0, g_kv_gain, g_mix_pre1) = _rowwise_bwd(
        "res_b0_bwd", _fn_res_norm2, [full(x1), full(d0)], [row(mlp_post, 0), kv_gain, row(mix_pre, 1)], [dx2, dhkv, dhq], [F32, BF16])
    (du0,) = _matmul(dd0, w_down[0], "nt", (BF16,), "mm_down0_dx", tk=1024, epilogue=_relu2_cotangent, extras=[a0])
    g_down0 = _matmul(a0, dd0, "tn", F32, "mm_down0_dw")
    dh1 = _matmul(du0, w_up[0], "nt", F32, "mm_up0_dx")
    g_up0 = _matmul(h1, du0, "tn", F32, "mm_up0_dw")
    (dx0, dmix0), (g_mix_post0, g_mlp_pre0) = _rowwise_bwd(
        "res_a0_bwd", _fn_res_norm, [full(x), full(mix0)], [row(mix_post, 0), row(mlp_pre, 0)], [dx1, dh1], [F32, BF16])
    don = _matmul(dmix0, w_out, "nt", F32, "mm_gdn_out_dx")
    g_out = _matmul(on, dmix0, "tn", F32, "mm_gdn_out_dw")
    (do_gdn, dgate), (g_og,) = _rowwise_bwd("out_norm_bwd", _fn_outnorm, [full(o_gdn), (qkvg, d, 3)], [og], [don], [F32, F32])
    dpw, dpu, dpqg, dpkg, dpattn, dpgl = _gdn_scan_bwd(pw, pu, pqg, pkg, pattn, pgl, states, do_gdn)
    dgq, dgk, dgv, dbeta, dgc, dgt, dgcr = _gdn_prep_bwd(gq, gk, gv, beta, gc, gt, gcr, ptm, dpw, dpu, dpqg, dpkg, dpattn, dpgl)
    dgcr_lanes = jnp.pad(jnp.swapaxes(dgcr.reshape(HEADS, t), 0, 1), ((0, 0), (HEADS, LANES - 2 * HEADS)))
    gate_cots = [dbeta, dgc + dgcr_lanes, dgt]
    (dba,), (g_al, g_dtb) = _rowwise_bwd("gates_bwd", _fn_gates, [full(ba)], [al, dtb], gate_cots, [BF16])
    dqkvg, g_conv = _conv_bwd(conv, (dgq, dgk, dgv), dgate, qkvg, conv_w)
    dh0b = _matmul(dba, w_ba, "nt", F32, "mm_gdn_ba_dx", tk=LANES)
    dh0 = _matmul(dqkvg, w_qkvg, "nt", F32, "mm_gdn_in_dx", add=dh0b)
    g_qkvg = _matmul(h0, dqkvg, "tn", F32, "mm_gdn_in_dw")
    g_ba = _matmul(h0, dba, "tn", F32, "mm_gdn_ba_dw")
    (grad_x,), (g_mix_pre0,) = _rowwise_bwd("norm_in_bwd", lambda xx, gg: (_rms(xx, gg), xx), [full(x)], [row(mix_pre, 0)], [dh0, dx0], [F32])

    grads = dict(
        mix_pre_gain=jnp.concatenate([g_mix_pre0, g_mix_pre1], axis=0),
        mix_post_gain=jnp.concatenate([g_mix_post0, g_mix_post1], axis=0),
        mlp_pre_gain=jnp.concatenate([g_mlp_pre0, g_mlp_pre1], axis=0),
        mlp_post_gain=jnp.concatenate([g_mlp_post0, g_mlp_post1], axis=0),
        mlp_w_up=(g_up0, g_up1),
        mlp_w_down=(g_down0, g_down1),
        gdn_w_in=jnp.concatenate([g_qkvg, g_ba[:, :2 * HEADS]], axis=1)[None],
        gdn_conv_w=g_conv[None, :CONV_K],
        gdn_a_log=g_al[:, HEADS:2 * HEADS],
        gdn_dt_bias=g_dtb[:, HEADS:2 * HEADS],
        gdn_out_gain=jnp.sum(g_og.reshape(HEADS, HEAD_DIM), axis=0, keepdims=True),
        gdn_w_out=g_out[None],
        kv_gain=g_kv_gain[0],
        w_kv=g_kv,
        sb_w_q=g_q[None],
        sb_w_o=g_o[None],
    )
    return loss, grad_x, grads


N_DEV = 8
N_CHIPS = 4
PACK_ROW_TILE = 128

_HBM = pl.BlockSpec(memory_space=pltpu.HBM)


def _place():
    return lax.axis_index("x"), lax.axis_index("y"), lax.axis_index("c")


def _other_chips(x, y):
    return [(1 - x, y), (x, 1 - y), (1 - x, 1 - y)]


def _remote(src, dst, send_sem, recv_sem, to):
    return pltpu.make_async_remote_copy(src_ref=src, dst_ref=dst, send_sem=send_sem, recv_sem=recv_sem, device_id=to, device_id_type=MESH)


def _gather8(v, name):
    rows, cols = v.shape

    def body(v_ref, out_ref, sum_ref, send_sems, recv_sems, local_sem):
        x, y, c = _place()
        me, sibling = (x, y, c), (x, y, 1 - c)
        chips = _other_chips(x, y)

        def blk(px, py, pc):
            return out_ref.at[pl.ds((4 * px + 2 * py + pc) * rows, rows), :]

        def copy(k, block, to, src=None):
            return _remote(blk(*block) if src is None else src, blk(*block), send_sems.at[k], recv_sems.at[k], to)

        mine = pltpu.make_async_copy(v_ref, blk(*me), local_sem)
        mine.start()
        first = [copy(0, me, sibling, src=v_ref)] + [copy(1 + j, me, (*chip, c), src=v_ref) for j, chip in enumerate(chips)]
        for cp in first:
            cp.start()
        passed = [copy(4 + j, (*chip, c), sibling) for j, chip in enumerate(chips)]
        for j, chip in enumerate(chips):
            copy(1 + j, (*chip, c), me).wait_recv()
            passed[j].start()
        copy(0, sibling, me).wait_recv()
        for j, chip in enumerate(chips):
            copy(4 + j, (*chip, 1 - c), me).wait_recv()
        for cp in first + passed:
            cp.wait_send()
        mine.wait()
        acc = out_ref[pl.ds(0, rows), :]
        for dev in range(1, N_DEV):
            acc = acc + out_ref[pl.ds(dev * rows, rows), :]
        sum_ref[...] = acc

    vm = pl.BlockSpec(memory_space=pltpu.VMEM)
    return pl.pallas_call(
        body,
        name=name,
        out_shape=[jax.ShapeDtypeStruct((N_DEV * rows, cols), v.dtype), jax.ShapeDtypeStruct((rows, cols), v.dtype)],
        in_specs=[vm],
        out_specs=[vm, vm],
        scratch_shapes=[pltpu.SemaphoreType.DMA((7,)), pltpu.SemaphoreType.DMA((7,)), pltpu.SemaphoreType.DMA],
    )(v)


def _hbm_call(body, name, arrs, out_shapes, sem_counts):
    n = len(arrs)

    def wrapped(*refs):
        body(refs[:n], refs[n:2 * n], *refs[2 * n:])

    return pl.pallas_call(
        wrapped,
        name=name,
        out_shape=[jax.ShapeDtypeStruct(s, a.dtype) for s, a in zip(out_shapes, arrs)],
        in_specs=[_HBM] * n,
        out_specs=[_HBM] * n,
        scratch_shapes=[pltpu.SemaphoreType.DMA((k,)) for k in sem_counts],
    )(*arrs)


def _gather_sends(w_refs, out_refs, send_sems, recv_sems):
    x, y, c = _place()
    s_me = 2 * x + y
    return [_remote(w.at[c], o.at[s_me, c], send_sems.at[3 * a + j], recv_sems.at[3 * a + j], (px, py, c))
            for a, (w, o) in enumerate(zip(w_refs, out_refs)) for j, (px, py) in enumerate(_other_chips(x, y))]


def _gather_finish(w_refs, out_refs, send_sems, recv_sems, fsend_sems, frecv_sems):
    x, y, c = _place()
    chips = _other_chips(x, y)
    passed = []
    for a, o in enumerate(out_refs):
        for j, (px, py) in enumerate(chips):
            half = o.at[2 * px + py, c]
            _remote(half, half, send_sems.at[3 * a + j], recv_sems.at[3 * a + j], (px, py, c)).wait_recv()
            fwd = _remote(half, half, fsend_sems.at[3 * a + j], frecv_sems.at[3 * a + j], (x, y, 1 - c))
            fwd.start()
            passed.append(fwd)
    for a, o in enumerate(out_refs):
        for j, (px, py) in enumerate(chips):
            half = o.at[2 * px + py, 1 - c]
            _remote(half, half, fsend_sems.at[3 * a + j], frecv_sems.at[3 * a + j], (x, y, 1 - c)).wait_recv()
    for cp in _gather_sends(w_refs, out_refs, send_sems, recv_sems) + passed:
        cp.wait_send()


def _gather_weights(arrs):
    n = len(arrs)

    def body(w_refs, out_refs, send_sems, recv_sems, fsend_sems, frecv_sems):
        for cp in _gather_sends(w_refs, out_refs, send_sems, recv_sems):
            cp.start()
        _gather_finish(w_refs, out_refs, send_sems, recv_sems, fsend_sems, frecv_sems)

    return _hbm_call(body, "gather_weights", arrs, [(N_CHIPS,) + a.shape for a in arrs], [3 * n] * 4)


def _swap_halves(arrs):
    n = len(arrs)

    def body(g_refs, a_refs, send_sems, recv_sems):
        x, y, c = _place()
        cps = [_remote(g.at[1 - c], a, send_sems.at[i], recv_sems.at[i], (x, y, 1 - c)) for i, (g, a) in enumerate(zip(g_refs, a_refs))]
        for cp in cps:
            cp.start()
        for cp in cps:
            cp.wait()

    return _hbm_call(body, "grads_to_sibling", arrs, [a.shape[1:] for a in arrs], [n, n])


def _scatter_to_chips(arrs):
    n = len(arrs)

    def body(p_refs, b_refs, send_sems, recv_sems):
        x, y, c = _place()
        cps = [_remote(p.at[2 * px + py], b.at[j], send_sems.at[3 * i + j], recv_sems.at[3 * i + j], (px, py, c))
               for i, (p, b) in enumerate(zip(p_refs, b_refs)) for j, (px, py) in enumerate(_other_chips(x, y))]
        for cp in cps:
            cp.start()
        for cp in cps:
            cp.wait()

    return _hbm_call(body, "grads_to_chips", arrs, [(3,) + a.shape[1:] for a in arrs], [3 * n, 3 * n])


def _share_halves(arrs):
    n = len(arrs)

    def body(q_refs, out_refs, send_sems, recv_sems):
        x, y, c = _place()
        cps = [_remote(q, o, send_sems.at[i], recv_sems.at[i], (x, y, 1 - c)) for i, (q, o) in enumerate(zip(q_refs, out_refs))]
        for cp in cps:
            cp.start()
        for cp in cps:
            cp.wait()

    return _hbm_call(body, "grads_share", arrs, [a.shape for a in arrs], [n, n])


_GROUPS = (
    (("mlp_w_up", (2, 1024, 1024), "cols"), ("mlp_w_down", (2, 1024, 1024), "rows"), ("gdn_w_out", (1, 256, 1024), "rows"),
     ("sb_w_q", (1, 256, 1024), "rows"), ("sb_w_o", (1, 256, 1024), "rows")),
    (("w_kv", (1024, 512), "cols"),),
    (("gdn_w_in", (1, 1024, 1028), "cols"),),
)


def _numel(shape):
    n = 1
    for s in shape:
        n *= s
    return n


def _half_rows(shape):
    return _numel(shape[:-1]) // 2


def _pack_shards(shards, dtype):
    return tuple(jnp.concatenate([shards[n].astype(dtype).reshape(2, _half_rows(shape), shape[-1]) for n, shape, _ in grp], axis=1) for grp in _GROUPS)


def _unpack_shards(bufs):
    out = {}
    for grp, buf in zip(_GROUPS, bufs):
        off = 0
        for n, shape, _ in grp:
            out[n] = buf[:, off:off + _half_rows(shape)].reshape(shape)
            off += _half_rows(shape)
    return out


def _join(stacked, how):
    nd = stacked.ndim - 1
    ax = nd - 1 if how == "cols" else nd - 2
    moved = jnp.moveaxis(stacked, 0, ax)
    shape = list(stacked.shape[1:])
    shape[ax] *= N_CHIPS
    return moved.reshape(shape)


def _split(full, shard_shape, how):
    nd = len(shard_shape)
    ax = nd - 1 if how == "cols" else nd - 2
    shape = list(shard_shape)
    shape.insert(ax, N_CHIPS)
    return jnp.moveaxis(full.reshape(shape), ax, 0)


def _unpack_full(gathered, groups):
    out = {}
    for grp, buf in zip(groups, gathered):
        off = 0
        for n, shape, how in grp:
            out[n] = _join(buf[:, :, off:off + _half_rows(shape)].reshape((N_CHIPS,) + shape), how)
            off += _half_rows(shape)
    return out


def _pack_full(full):
    bufs = []
    for grp in _GROUPS:
        parts = []
        for n, shape, how in grp:
            if isinstance(full[n], tuple):
                assert len(full[n]) == shape[0] == 2
                parts.append(jnp.stack([_split(layer, shape[1:], how) for layer in full[n]], axis=1))
            else:
                parts.append(_split(full[n], shape, how).reshape(N_CHIPS, 2, _half_rows(shape), shape[-1]))
        buf = jnp.swapaxes(jnp.concatenate(parts, axis=2), 0, 1)
        bufs.append(buf.reshape(2, -1, buf.shape[-1]))
    return tuple(bufs)


_SMALL = (
    ("mix_pre_gain", (2, 1024)),
    ("mix_post_gain", (2, 1024)),
    ("mlp_pre_gain", (2, 1024)),
    ("mlp_post_gain", (2, 1024)),
    ("kv_gain", (1024,)),
    ("gdn_out_gain", (1, 128)),
    ("gdn_a_log", (1, 8)),
    ("gdn_dt_bias", (1, 8)),
    ("gdn_conv_w", (1, 4, 3072)),
    ("loss", ()),
)


def _rows_of(shape):
    return -(-_numel(shape) // LANES)


_SMALL_ROWS = -(-sum(_rows_of(s) for _, s in _SMALL) // 8) * 8


def _pack_small(vals):
    parts = []
    for n, shape in _SMALL:
        flat = vals[n].reshape(-1)
        parts.append(jnp.pad(flat, (0, _rows_of(shape) * LANES - flat.shape[0])))
    flat = jnp.concatenate(parts)
    return jnp.pad(flat, (0, _SMALL_ROWS * LANES - flat.shape[0])).reshape(_SMALL_ROWS, LANES)


def _unpack_small(packed):
    flat = packed.reshape(-1)
    out, off = {}, 0
    for n, shape in _SMALL:
        out[n] = flat[off:off + _numel(shape)].reshape(shape)
        off += _rows_of(shape) * LANES
    return out


_WEIGHTS = ("mix_pre_gain", "mix_post_gain", "mlp_pre_gain", "mlp_post_gain", "mlp_w_up", "mlp_w_down", "gdn_w_in", "gdn_conv_w",
            "gdn_a_log", "gdn_dt_bias", "gdn_out_gain", "gdn_w_out", "kv_gain", "w_kv", "sb_w_q", "sb_w_o")


def _as2d(a):
    return a.reshape(1, -1) if a.ndim <= 1 else a.reshape(-1, a.shape[-1])


def kernel(x, mix_pre_gain, mix_post_gain, mlp_pre_gain, mlp_post_gain, mlp_w_up, mlp_w_down, gdn_w_in, gdn_conv_w, gdn_a_log, gdn_dt_bias, gdn_out_gain, gdn_w_out, kv_gain, w_kv, sb_w_q, sb_w_o, loss_target, m_mix_pre_gain, m_mix_post_gain, m_mlp_pre_gain, m_mlp_post_gain, m_mlp_w_up, m_mlp_w_down, m_gdn_w_in, m_gdn_conv_w, m_gdn_a_log, m_gdn_dt_bias, m_gdn_out_gain, m_gdn_w_out, m_kv_gain, m_w_kv, m_sb_w_q, m_sb_w_o, v_mix_pre_gain, v_mix_post_gain, v_mlp_pre_gain, v_mlp_post_gain, v_mlp_w_up, v_mlp_w_down, v_gdn_w_in, v_gdn_conv_w, v_gdn_a_log, v_gdn_dt_bias, v_gdn_out_gain, v_gdn_w_out, v_kv_gain, v_w_kv, v_sb_w_q, v_sb_w_o):
    w = dict(mix_pre_gain=mix_pre_gain, mix_post_gain=mix_post_gain, mlp_pre_gain=mlp_pre_gain, mlp_post_gain=mlp_post_gain, mlp_w_up=mlp_w_up, mlp_w_down=mlp_w_down, gdn_w_in=gdn_w_in, gdn_conv_w=gdn_conv_w, gdn_a_log=gdn_a_log, gdn_dt_bias=gdn_dt_bias, gdn_out_gain=gdn_out_gain, gdn_w_out=gdn_w_out, kv_gain=kv_gain, w_kv=w_kv, sb_w_q=sb_w_q, sb_w_o=sb_w_o)
    m = dict(mix_pre_gain=m_mix_pre_gain, mix_post_gain=m_mix_post_gain, mlp_pre_gain=m_mlp_pre_gain, mlp_post_gain=m_mlp_post_gain, mlp_w_up=m_mlp_w_up, mlp_w_down=m_mlp_w_down, gdn_w_in=m_gdn_w_in, gdn_conv_w=m_gdn_conv_w, gdn_a_log=m_gdn_a_log, gdn_dt_bias=m_gdn_dt_bias, gdn_out_gain=m_gdn_out_gain, gdn_w_out=m_gdn_w_out, kv_gain=m_kv_gain, w_kv=m_w_kv, sb_w_q=m_sb_w_q, sb_w_o=m_sb_w_o)
    v = dict(mix_pre_gain=v_mix_pre_gain, mix_post_gain=v_mix_post_gain, mlp_pre_gain=v_mlp_pre_gain, mlp_post_gain=v_mlp_post_gain, mlp_w_up=v_mlp_w_up, mlp_w_down=v_mlp_w_down, gdn_w_in=v_gdn_w_in, gdn_conv_w=v_gdn_conv_w, gdn_a_log=v_gdn_a_log, gdn_dt_bias=v_gdn_dt_bias, gdn_out_gain=v_gdn_out_gain, gdn_w_out=v_gdn_w_out, kv_gain=v_kv_gain, w_kv=v_w_kv, sb_w_q=v_sb_w_q, sb_w_o=v_sb_w_o)
    cx, cy, cc = _place()
    chip = 2 * cx + cy
    conv_cols = gdn_conv_w.shape[-1]

    own = _pack_shards(w, BF16)
    with_own = lambda gathered, mine: [lax.dynamic_update_index_in_dim(g, m, chip, 0) for g, m in zip(gathered, mine)]
    w_in = _unpack_full(with_own(_gather_weights(own[2:]), own[2:]), _GROUPS[2:])["gdn_w_in"][0]

    def assemble(gathered):
        full = _unpack_full(with_own(gathered, own[:2]), _GROUPS[:2])
        return full["gdn_w_out"][0], full["w_kv"], full["sb_w_q"][0], full["sb_w_o"][0], full["mlp_w_up"], full["mlp_w_down"]

    conv_rows = jnp.pad(gdn_conv_w[0], ((0, 8 - CONV_K), (0, 0))).reshape(-1, LANES)
    conv_all, _ = _gather8(conv_rows, "gather_conv_w")
    conv_all = conv_all.reshape(N_CHIPS, 2, 8, conv_cols)[:, 0, :CONV_K]
    conv_full = jnp.swapaxes(conv_all, 0, 1).reshape(CONV_K, N_CHIPS * conv_cols)

    w_in = (w_in[:, :4 * HEADS * HEAD_DIM], jnp.pad(w_in[:, 4 * HEADS * HEAD_DIM:], ((0, 0), (0, LANES - 2 * HEADS))))
    gains = (mix_pre_gain, mix_post_gain, mlp_pre_gain, mlp_post_gain, kv_gain[None])
    small = (conv_full, gdn_a_log, gdn_dt_bias, gdn_out_gain)
    loss_rows, grad_x, g_full = _local_step(x[0], loss_target[0], gains, w_in, small, own[:2], assemble)

    bufs = _pack_full(g_full)
    from_sibling = _swap_halves(bufs)
    partial, partial_bf16 = [], []
    tile = PACK_ROW_TILE
    for i, (buf, other) in enumerate(zip(bufs, from_sibling)):
        _, n, cols = buf.shape
        p, pb = _add_rows(f"grads_add_sibling_{i}", [(buf.reshape(2 * n, cols), cc * (n // tile)), (other, 0)], n, (F32, BF16), tile)
        partial.append(p.reshape(N_CHIPS, -1, cols))
        partial_bf16.append(pb.reshape(N_CHIPS, -1, cols))
    from_chips = _scatter_to_chips(tuple(partial_bf16))
    reduced = []
    for i, (p, others) in enumerate(zip(partial, from_chips)):
        _, r, cols = p.shape
        terms = [(p.reshape(N_CHIPS * r, cols), chip * (r // tile))] + [(others.reshape(3 * r, cols), j * (r // tile)) for j in range(3)]
        reduced.append(_add_rows(f"grads_add_chips_{i}", terms, r, (F32,), tile)[0])
    g_shard = _unpack_shards([jnp.where(cc == 0, jnp.stack([r, o]), jnp.stack([o, r])) for r, o in zip(reduced, _share_halves(tuple(reduced)))])

    g_small_local = {n: g_full[n] for n, _ in _SMALL if n != "loss"}
    g_small_local["loss"] = loss_rows[0, 0]
    _, small_sum = _gather8(_pack_small(g_small_local), "allreduce_small")
    g_small = _unpack_small(small_sum)
    loss = g_small.pop("loss")
    g_small["gdn_conv_w"] = lax.dynamic_slice_in_dim(g_small["gdn_conv_w"], chip * conv_cols, conv_cols, axis=2)

    grads = {**g_shard, **g_small}
    deltas, new_m, new_v = {}, {}, {}
    for n in _WEIGHTS:
        d2, m2, v2 = _adamw(_as2d(w[n]), _as2d(grads[n]), _as2d(m[n]), _as2d(v[n]), "adamw_" + n)
        deltas[n], new_m[n], new_v[n] = d2.reshape(w[n].shape), m2.reshape(w[n].shape), v2.reshape(w[n].shape)
    return (loss, grad_x[None], *[grads[n].reshape(w[n].shape) for n in _WEIGHTS], *[deltas[n] for n in _WEIGHTS],
            *[new_m[n] for n in _WEIGHTS], *[new_v[n] for n in _WEIGHTS])
```

```python
import functools

import jax
import jax.numpy as jnp
from jax import lax
from jax.experimental import pallas as pl
from jax.experimental.pallas import tpu as pltpu

F32, BF16 = jnp.float32, jnp.bfloat16
HI = lax.Precision.HIGHEST
MESH = pl.DeviceIdType.MESH

EPS = 1e-6
D_MODEL = 1024
HEADS = 8
HEAD_DIM = 128
CHUNK = 64
CHUNK_SHIFT = CHUNK.bit_length() - 1
CONV_K = 4
D_FF = 4096
QKV = 3 * HEADS * HEAD_DIM

ADAM_LR, ADAM_B1, ADAM_B2, ADAM_EPS, ADAM_WD, ADAM_STEP = 0.001, 0.9, 0.999, 1e-08, 0.01, 10

VMEM_LIMIT_BYTES = 48 * 1024 * 1024
LANES = 128

NN = ((1,), (0,))
NT = ((1,), (1,))
TN = ((0,), (0,))


def _dot(a, b, dims=NN, precision=None):
    return lax.dot_general(a, b, (dims, ((), ())), precision=precision, preferred_element_type=F32)


def _params(*sem):
    return pltpu.CompilerParams(dimension_semantics=sem, vmem_limit_bytes=VMEM_LIMIT_BYTES)


def _iota(shape, axis):
    return lax.broadcasted_iota(jnp.int32, shape, axis)


def _matmul(a, b, mode, out_dtype, name, tm=1024, tn=1024, tk=1024, add=None, epilogue=None, extras=()):
    if mode == "nn":
        (m, k), (k2, n) = a.shape, b.shape
    elif mode == "nt":
        (m, k), (n, k2) = a.shape, b.shape
    else:
        (k, m), (k2, n) = a.shape, b.shape
    assert k == k2, (a.shape, b.shape, mode)
    tm, tn, tk = min(tm, m), min(tn, n), min(tk, k)
    assert m % tm == 0 and n % tn == 0 and k % tk == 0, (a.shape, b.shape, mode)
    nk = k // tk
    dims = {"nn": NN, "nt": NT, "tn": TN}[mode]
    tiles = ([add] if add is not None else []) + list(extras)
    out_dtypes = out_dtype if epilogue is not None else (out_dtype,)
    n_in = 2 + len(tiles)

    def finish(acc, extra_refs, o_refs):
        res = (acc,) if epilogue is None else epilogue(acc, *[r[...] for r in extra_refs])
        for o_ref, r in zip(o_refs, res):
            o_ref[...] = r.astype(o_ref.dtype)

    def body(*refs):
        a_ref, b_ref = refs[:2]
        extra_refs = refs[n_in - len(extras):n_in]
        o_refs, acc_ref = refs[n_in:-1], refs[-1]
        prod = _dot(a_ref[...].astype(BF16), b_ref[...].astype(BF16), dims)
        if nk == 1:
            finish(prod + refs[2][...].astype(F32) if add is not None else prod, extra_refs, o_refs)
            return
        kk = pl.program_id(2)

        @pl.when(kk == 0)
        def _():
            acc_ref[...] = refs[2][...].astype(F32) if add is not None else jnp.zeros_like(acc_ref)

        acc_ref[...] += prod

        @pl.when(kk == nk - 1)
        def _():
            finish(acc_ref[...], extra_refs, o_refs)

    a_spec = pl.BlockSpec((tk, tm), lambda i, j, kk: (kk, i)) if mode == "tn" else pl.BlockSpec((tm, tk), lambda i, j, kk: (i, kk))
    b_spec = pl.BlockSpec((tn, tk), lambda i, j, kk: (j, kk)) if mode == "nt" else pl.BlockSpec((tk, tn), lambda i, j, kk: (kk, j))
    o_spec = pl.BlockSpec((tm, tn), lambda i, j, kk: (i, j))
    res = pl.pallas_call(
        body,
        name=name,
        grid=(m // tm, n // tn, nk),
        in_specs=[a_spec, b_spec] + [o_spec] * len(tiles),
        out_specs=[o_spec] * len(out_dtypes),
        out_shape=[jax.ShapeDtypeStruct((m, n), dt) for dt in out_dtypes],
        scratch_shapes=[pltpu.VMEM((tm, tn), F32)],
        compiler_params=_params("parallel", "parallel", "arbitrary"),
    )(a, b, *tiles)
    return res if epilogue is not None else res[0]


def _row_specs(rows, tm):
    return [pl.BlockSpec((tm, w), lambda i, cb=cb: (i, cb)) for _, w, cb in rows]


def _full_spec(p):
    return pl.BlockSpec(p.shape, lambda i: (0,) * p.ndim)


def _rowwise(name, fn, rows, params, outs, tm=256):
    t = rows[0][0].shape[0]
    tm = min(tm, t)
    nr, npar = len(rows), len(params)

    def body(*refs):
        ins = [r[...].astype(F32) for r in refs[:nr]]
        ps = [p[...] for p in refs[nr:nr + npar]]
        res = fn(*ins, *ps)
        for o_ref, r in zip(refs[nr + npar:], res):
            o_ref[...] = r.astype(o_ref.dtype)

    return pl.pallas_call(
        body,
        name=name,
        grid=(t // tm,),
        in_specs=_row_specs(rows, tm) + [_full_spec(p) for p in params],
        out_specs=[pl.BlockSpec((tm, w), lambda i: (i, 0)) for w, _ in outs],
        out_shape=[jax.ShapeDtypeStruct((t, w), dt) for w, dt in outs],
        compiler_params=_params("parallel"),
    )(*[r[0] for r in rows], *params)


def _add_rows(name, terms, n_rows, out_dtypes, tm):
    cols = terms[0][0].shape[1]
    firsts = jnp.stack([jnp.asarray(first, jnp.int32) for _, first in terms])

    def body(firsts_ref, *refs):
        acc = refs[0][...].astype(F32)
        for r in refs[1:len(terms)]:
            acc = acc + r[...].astype(F32)
        for o_ref in refs[len(terms):]:
            o_ref[...] = acc.astype(o_ref.dtype)

    return pl.pallas_call(
        body,
        name=name,
        grid_spec=pltpu.PrefetchScalarGridSpec(
            num_scalar_prefetch=1,
            grid=(n_rows // tm,),
            in_specs=[pl.BlockSpec((tm, cols), lambda i, firsts_ref, k=k: (firsts_ref[k] + i, 0)) for k in range(len(terms))],
            out_specs=[pl.BlockSpec((tm, cols), lambda i, firsts_ref: (i, 0)) for _ in out_dtypes],
        ),
        out_shape=[jax.ShapeDtypeStruct((n_rows, cols), dt) for dt in out_dtypes],
        compiler_params=_params("parallel"),
    )(firsts, *[a for a, _ in terms])


def _rowwise_bwd(name, fn, rows, params, cots, grad_dtypes, tm=256):
    t = rows[0][0].shape[0]
    tm = min(tm, t)
    nr, npar, nc = len(rows), len(params), len(cots)
    want = [j for j, dt in enumerate(grad_dtypes) if dt is not None]
    widths = [rows[j][1] for j in want]
    n_row_outs = len(want)

    def body(*refs):
        i = pl.program_id(0)
        ins = [r[...].astype(F32) for r in refs[:nr]]
        ps = [p[...] for p in refs[nr:nr + npar]]
        cs = tuple(c[...].astype(F32) for c in refs[nr + npar:nr + npar + nc])
        _, vjp = jax.vjp(fn, *ins, *ps)
        gs = vjp(cs)
        outs = refs[nr + npar + nc:]
        for o_ref, j in zip(outs, want):
            o_ref[...] = gs[j].astype(o_ref.dtype)
        pg_refs = outs[n_row_outs:]

        @pl.when(i == 0)
        def _():
            for pg in pg_refs:
                pg[...] = jnp.zeros_like(pg)

        for pg, g in zip(pg_refs, gs[nr:]):
            pg[...] += g

    row_specs = [pl.BlockSpec((tm, w), lambda i: (i, 0)) for w in widths]
    row_shapes = [jax.ShapeDtypeStruct((t, w), grad_dtypes[j]) for j, w in zip(want, widths)]
    res = pl.pallas_call(
        body,
        name=name,
        grid=(t // tm,),
        in_specs=_row_specs(rows, tm) + [_full_spec(p) for p in params] + [pl.BlockSpec((tm, c.shape[1]), lambda i: (i, 0)) for c in cots],
        out_specs=row_specs + [_full_spec(p) for p in params],
        out_shape=row_shapes + [jax.ShapeDtypeStruct(p.shape, F32) for p in params],
        compiler_params=_params("arbitrary"),
    )(*[r[0] for r in rows], *params, *cots)
    return res[:n_row_outs], res[n_row_outs:]


def _rms(x, g):
    return x * lax.rsqrt(jnp.mean(x * x, axis=-1, keepdims=True) + EPS) * g


def _sigmoid(x):
    return 1.0 / (1.0 + jnp.exp(-x))


def _softplus(x):
    return jnp.maximum(x, 0.0) + jnp.log1p(jnp.exp(-jnp.abs(x)))


def _two_pass(x, m):
    hi = x.astype(BF16)
    lo = (x - hi.astype(F32)).astype(BF16)
    return _dot(hi, m) + _dot(lo, m)


def _head_sum_impl(x):
    sums = [jnp.sum(x[:, h * HEAD_DIM:(h + 1) * HEAD_DIM], axis=-1, keepdims=True) for h in range(HEADS)]
    return jnp.concatenate([jnp.broadcast_to(s, (x.shape[0], HEAD_DIM)) for s in sums], axis=1)


@jax.custom_vjp
def _head_sum(x):
    return _head_sum_impl(x)


_head_sum.defvjp(lambda x: (_head_sum_impl(x), None), lambda _, g: (_head_sum_impl(g),))


def _fn_norm(x, g):
    return (_rms(x, g),)


def _fn_gates(ba, al, dt):
    col = _iota((1, LANES), 1)
    g = jnp.where((col >= HEADS) & (col < 2 * HEADS), -jnp.exp(al) * _softplus(ba + dt), 0.0)
    rows = ba.shape[0]
    r, c = _iota((rows, rows), 0), _iota((rows, rows), 1)
    same = (r >> CHUNK_SHIFT) == (c >> CHUNK_SHIFT)
    gc = _dot(jnp.where(same & (r >= c), 1.0, 0.0), g, precision=HI)
    gtot = _dot(jnp.where(same, 1.0, 0.0), g, precision=HI)
    return _sigmoid(ba), gc, gtot


def _fn_post_q(c):
    s = c * _sigmoid(c)
    return (s * lax.rsqrt(_head_sum(s * s) + EPS) * (HEAD_DIM ** -0.5),)


def _fn_post_k(c):
    s = c * _sigmoid(c)
    return (s * lax.rsqrt(_head_sum(s * s) + EPS),)


def _fn_post_v(c):
    return (c * _sigmoid(c),)


def _fn_post(cq, ck, cv):
    return _fn_post_q(cq) + _fn_post_k(ck) + _fn_post_v(cv)


def _fn_outnorm(o, gate, og):
    y = o * lax.rsqrt(_head_sum(o * o) * (1.0 / HEAD_DIM) + EPS) * og
    return (y * (gate * _sigmoid(gate)),)


def _fn_res_norm(x, m, gp, gn):
    x1 = x + _rms(m, gp)
    return x1, _rms(x1, gn)


def _fn_res_norm2(x, m, gp, ga, gb):
    x1 = x + _rms(m, gp)
    return x1, _rms(x1, ga), _rms(x1, gb)


def _relu2_of(u):
    r = jnp.maximum(u, 0.0)
    return (r * r,)


def _relu2_cotangent(da, a):
    return (da * (2.0 * jnp.sqrt(a.astype(F32))),)


def _loss_call(x3, d1, tgt, g, tm=256):
    t, d = x3.shape
    tm = min(tm, t)

    def body(x_ref, d_ref, t_ref, g_ref, loss_ref, dx_ref, dd_ref, dg_ref):
        i = pl.program_id(0)
        y, vjp = jax.vjp(lambda x, dd, gg: x + _rms(dd, gg), x_ref[...], d_ref[...], g_ref[...])
        err = y - t_ref[...]
        lrow = 0.5 * jnp.mean(err * err, axis=-1, keepdims=True)
        dx, dd, dg = vjp(err * (1.0 / d))
        dx_ref[...] = dx
        dd_ref[...] = dd.astype(dd_ref.dtype)

        @pl.when(i == 0)
        def _():
            loss_ref[...] = jnp.zeros_like(loss_ref)
            dg_ref[...] = jnp.zeros_like(dg_ref)

        loss_ref[...] += jnp.broadcast_to(jnp.sum(lrow, axis=0, keepdims=True), loss_ref.shape)
        dg_ref[...] += dg

    row = pl.BlockSpec((tm, d), lambda i: (i, 0))
    return pl.pallas_call(
        body,
        name="loss_head",
        grid=(t // tm,),
        in_specs=[row, row, row, _full_spec(g)],
        out_specs=[pl.BlockSpec((8, LANES), lambda i: (0, 0)), row, row, _full_spec(g)],
        out_shape=[jax.ShapeDtypeStruct((8, LANES), F32), jax.ShapeDtypeStruct((t, d), F32), jax.ShapeDtypeStruct((t, d), BF16), jax.ShapeDtypeStruct(g.shape, F32)],
        compiler_params=_params("arbitrary"),
    )(x3, d1, tgt, g)


HALO = 8


def _conv_fwd(qkvg, conv_w, shards, tm=256):
    t = qkvg.shape[0]
    tm = min(tm, t)
    steps = t // tm
    wide = QKV // 3
    n = len(shards)

    def body(*refs):
        cur_ref, prev_ref, w_ref = refs[:3]
        shard_refs = refs[3:3 + n]
        o_ref, q_ref, k_ref, v_ref = refs[3 + n:7 + n]
        all_refs = refs[7 + n:7 + 2 * n]
        buf, sems = refs[7 + 2 * n], refs[8 + 2 * n:]
        i = pl.program_id(0)

        if n:
            @pl.when(i == 0)
            def _():
                for cp in _gather_sends(shard_refs, all_refs, *sems[:2]):
                    cp.start()

        buf[0:HALO, :] = jnp.where(i > 0, prev_ref[...], 0.0)
        buf[HALO:, :] = cur_ref[...]
        acc = buf[pl.ds(HALO - CONV_K + 1, tm), :] * w_ref[pl.ds(0, 1), :]
        for j in range(1, CONV_K):
            acc = acc + buf[pl.ds(HALO - CONV_K + 1 + j, tm), :] * w_ref[pl.ds(j, 1), :]
        o_ref[...] = acc
        (q_ref[...], k_ref[...], v_ref[...]) = _fn_post(acc[:, 0:wide], acc[:, wide:2 * wide], acc[:, 2 * wide:])

        if n:
            @pl.when(i == steps - 1)
            def _():
                _gather_finish(shard_refs, all_refs, *sems)

    part = pl.BlockSpec((tm, wide), lambda i: (i, 0))
    res = pl.pallas_call(
        body,
        name="conv_fwd",
        grid=(steps,),
        in_specs=[
            pl.BlockSpec((tm, QKV), lambda i: (i, 0)),
            pl.BlockSpec((HALO, QKV), lambda i: (jnp.maximum(i * (tm // HALO) - 1, 0), 0)),
            pl.BlockSpec((CONV_K, QKV), lambda i: (0, 0)),
        ] + [_HBM] * n,
        out_specs=[pl.BlockSpec((tm, QKV), lambda i: (i, 0)), part, part, part] + [_HBM] * n,
        out_shape=[jax.ShapeDtypeStruct((t, QKV), F32)] + [jax.ShapeDtypeStruct((t, wide), F32)] * 3
        + [jax.ShapeDtypeStruct((N_CHIPS,) + s.shape, s.dtype) for s in shards],
        scratch_shapes=[pltpu.VMEM((tm + HALO, QKV), F32)] + [pltpu.SemaphoreType.DMA((3 * n,))] * (4 if n else 0),
        compiler_params=_params("arbitrary"),
    )(qkvg, qkvg, conv_w, *shards)
    return res[:4], res[4:]


def _conv_bwd(conv, dqkv, dgate, qkvg, conv_w, tm=256):
    t = conv.shape[0]
    tm = min(tm, t)
    n = t // tm
    wg = dgate.shape[1]
    wide = QKV // 3

    def conv_cotangent(c_ref, g_refs):
        parts = [c_ref[:, j * wide:(j + 1) * wide] for j in range(3)]
        _, vjp = jax.vjp(_fn_post, *parts)
        return vjp(tuple(g[...] for g in g_refs))

    def body(c_ref, cn_ref, dq_ref, dk_ref, dv_ref, dqn_ref, dkn_ref, dvn_ref, dgate_ref, x_ref, xp_ref, w_ref, dx_ref, dw_ref, bufd, bufx):
        i = pl.program_id(0)
        for j, (cur, nxt) in enumerate(zip(conv_cotangent(c_ref, (dq_ref, dk_ref, dv_ref)), conv_cotangent(cn_ref, (dqn_ref, dkn_ref, dvn_ref)))):
            bufd[0:tm, j * wide:(j + 1) * wide] = cur
            bufd[tm:, j * wide:(j + 1) * wide] = jnp.where(i < n - 1, nxt, 0.0)
        bufx[0:HALO, :] = jnp.where(i > 0, xp_ref[...], 0.0)
        bufx[HALO:, :] = x_ref[...]

        @pl.when(i == 0)
        def _():
            dw_ref[...] = jnp.zeros_like(dw_ref)

        dcv = bufd[0:tm, :]
        acc = bufd[pl.ds(CONV_K - 1, tm), :] * w_ref[pl.ds(0, 1), :]
        for j in range(1, CONV_K):
            acc = acc + bufd[pl.ds(CONV_K - 1 - j, tm), :] * w_ref[pl.ds(j, 1), :]
        dx_ref[:, 0:QKV] = acc.astype(dx_ref.dtype)
        dx_ref[:, QKV:] = dgate_ref[...].astype(dx_ref.dtype)
        for j in range(CONV_K):
            dw_ref[pl.ds(j, 1), :] += jnp.sum(dcv * bufx[pl.ds(HALO - CONV_K + 1 + j, tm), :], axis=0, keepdims=True)

    def cur(width):
        return pl.BlockSpec((tm, width), lambda i: (i, 0))

    def nxt(width):
        return pl.BlockSpec((HALO, width), lambda i: (jnp.minimum((i + 1) * (tm // HALO), t // HALO - 1), 0))

    return pl.pallas_call(
        body,
        name="conv_bwd",
        grid=(n,),
        in_specs=[cur(QKV), nxt(QKV)] + [cur(wide)] * 3 + [nxt(wide)] * 3 + [
            cur(wg),
            cur(QKV),
            pl.BlockSpec((HALO, QKV), lambda i: (jnp.maximum(i * (tm // HALO) - 1, 0), 0)),
            pl.BlockSpec((CONV_K, QKV), lambda i: (0, 0)),
        ],
        out_specs=[pl.BlockSpec((tm, QKV + wg), lambda i: (i, 0)), pl.BlockSpec((HALO, QKV), lambda i: (0, 0))],
        out_shape=[jax.ShapeDtypeStruct((t, QKV + wg), BF16), jax.ShapeDtypeStruct((HALO, QKV), F32)],
        scratch_shapes=[pltpu.VMEM((tm + HALO, QKV), F32), pltpu.VMEM((tm + HALO, QKV), F32)],
        compiler_params=_params("arbitrary"),
    )(conv, conv, *dqkv, *dqkv, dgate, qkvg, qkvg, conv_w)


PREP_CHUNKS = 16
PREP_BWD_CHUNKS = 4
SCAN_CHUNKS = 4


def _hi_lo(x):
    hi = x.astype(BF16)
    return hi, (x - hi.astype(F32)).astype(BF16)


def _mm3(a, b, dims=NN):
    (ah, al), (bh, bl) = _hi_lo(a), _hi_lo(b)
    return _dot(ah, bh, dims) + (_dot(ah, bl, dims) + _dot(al, bh, dims))


def _neumann(lowers):
    c = lowers[0].shape[0]
    eye = jnp.where(_iota((c, c), 0) == _iota((c, c), 1), 1.0, 0.0)
    ps = [-low for low in lowers]
    tmats = [eye + p for p in ps]
    for _ in range(CHUNK_SHIFT - 1):
        ps = [_mm3(p, p) for p in ps]
        tmats = [t + _mm3(t, p) for t, p in zip(tmats, ps)]
    return tuple(tmats)


def _inv_cotangents(tmats, dts):
    half = [_mm3(t, dt, TN) for t, dt in zip(tmats, dts)]
    return tuple(-_mm3(hf, t, NT) for hf, t in zip(half, tmats))


@jax.custom_vjp
def _tri_inv(lowers):
    return _neumann(lowers)


def _tri_inv_fwd(lowers):
    tmats = _neumann(lowers)
    return tmats, tmats


_tri_inv.defvjp(_tri_inv_fwd, lambda tmats, dts: (_inv_cotangents(tmats, dts),))


@jax.custom_vjp
def _tri_inv_known(lowers, tmats):
    return tmats


_tri_inv_known.defvjp(lambda lowers, tmats: (tmats, tmats),
                      lambda tmats, dts: (_inv_cotangents(tmats, dts), tuple(jnp.zeros_like(t) for t in tmats)))


def _prep_chunks(qs, ks, vs, bs, gcs, gts, gcrs, tmats=None):
    c = CHUNK
    r, col = _iota((c, c), 0), _iota((c, c), 1)
    incl, strict = r >= col, r > col
    decays = [jnp.where(incl, jnp.exp(jnp.where(incl, gc - gcr, 0.0)), 0.0) for gc, gcr in zip(gcs, gcrs)]
    kbs = [k * b for k, b in zip(ks, bs)]
    kbfs = [k.astype(BF16) for k in ks]
    lowers = tuple(jnp.where(strict, _dot(kb.astype(BF16), kbf, NT) * decay, 0.0) for kb, kbf, decay in zip(kbs, kbfs, decays))
    tmats = _tri_inv(lowers) if tmats is None else _tri_inv_known(lowers, tuple(tmats))
    outs = []
    for q, k, v, b, gc, gt, kb, kbf, decay, tmat in zip(qs, ks, vs, bs, gcs, gts, kbs, kbfs, decays, tmats):
        tb = tmat.astype(BF16)
        egc = jnp.exp(gc)
        w = _dot(tb, (kb * egc).astype(BF16))
        u = _dot(tb, (v * b).astype(BF16))
        attn = _dot(q.astype(BF16), kbf, NT) * decay
        gl = jnp.broadcast_to(jnp.exp(jnp.mean(gt.reshape(c // 8, 8, 1), axis=0)), (8, HEAD_DIM))
        outs.append((w, u, q * egc, k * jnp.exp(gt - gc), attn, gl))
    return tuple(outs), tmats


def _prep_specs(rows, gch):
    head = pl.BlockSpec((rows, HEAD_DIM), lambda n, h: (n, h))
    gates = pl.BlockSpec((rows, LANES), lambda n, h: (n, 0))
    gcrow = pl.BlockSpec((1, gch, 1, CHUNK), lambda n, h: (h, n, 0, 0))
    square = pl.BlockSpec((1, rows, CHUNK), lambda n, h: (h, n, 0))
    gl = pl.BlockSpec((1, gch * 8, HEAD_DIM), lambda n, h: (h, n, 0))
    return head, gates, gcrow, square, gl


def _pick_lane(ref, sl, lane):
    return jnp.sum(jnp.where(_iota((1, LANES), 1) == lane, ref[sl, :], 0.0), axis=1, keepdims=True)


def _prep_inputs(q_ref, k_ref, v_ref, b_ref, gc_ref, gt_ref, gcr_ref, sls, h):
    return ([q_ref[sl, :] for sl in sls], [k_ref[sl, :] for sl in sls], [v_ref[sl, :] for sl in sls],
            [_pick_lane(b_ref, sl, h) for sl in sls], [_pick_lane(gc_ref, sl, h + HEADS) for sl in sls],
            [_pick_lane(gt_ref, sl, h + HEADS) for sl in sls], [gcr_ref[0, c] for c in range(len(sls))])


def _gdn_prep(q, k, v, beta, gc, gt, gcr, shards=()):
    t = q.shape[0]
    gch = min(PREP_CHUNKS, t // CHUNK)
    rows = gch * CHUNK
    steps = t // rows
    n = len(shards)

    def body(*refs):
        q_ref, k_ref, v_ref, b_ref, gc_ref, gt_ref, gcr_ref = refs[:7]
        shard_refs = refs[7:7 + n]
        w_ref, u_ref, qg_ref, kg_ref, at_ref, gl_ref, tm_ref = refs[7 + n:14 + n]
        all_refs, sems = refs[14 + n:14 + 2 * n], refs[14 + 2 * n:]
        h = pl.program_id(1)

        if n:
            @pl.when(jnp.logical_and(pl.program_id(0) == 0, h == 0))
            def _():
                for cp in _gather_sends(shard_refs, all_refs, *sems[:2]):
                    cp.start()

        sls = [pl.ds(c * CHUNK, CHUNK) for c in range(gch)]
        outs, tmats = _prep_chunks(*_prep_inputs(q_ref, k_ref, v_ref, b_ref, gc_ref, gt_ref, gcr_ref, sls, h))
        for c, (sl, (w, u, qg, kg, attn, gl), tmat) in enumerate(zip(sls, outs, tmats)):
            w_ref[sl, :] = w.astype(BF16)
            u_ref[sl, :] = u
            qg_ref[sl, :] = qg.astype(BF16)
            kg_ref[sl, :] = kg.astype(BF16)
            at_ref[0, sl, :] = attn.astype(BF16)
            gl_ref[0, pl.ds(c * 8, 8), :] = gl
            tm_ref[0, sl, :] = tmat

        if n:
            @pl.when(jnp.logical_and(pl.program_id(0) == steps - 1, h == HEADS - 1))
            def _():
                _gather_finish(shard_refs, all_refs, *sems)

    hb, col, gcrow, square, glb = _prep_specs(rows, gch)
    wide = HEADS * HEAD_DIM
    res = pl.pallas_call(
        body,
        name="gdn_prep",
        grid=(steps, HEADS),
        in_specs=[hb, hb, hb, col, col, col, gcrow] + [_HBM] * n,
        out_specs=[hb, hb, hb, hb, square, glb, square] + [_HBM] * n,
        out_shape=[
            jax.ShapeDtypeStruct((t, wide), BF16),
            jax.ShapeDtypeStruct((t, wide), F32),
            jax.ShapeDtypeStruct((t, wide), BF16),
            jax.ShapeDtypeStruct((t, wide), BF16),
            jax.ShapeDtypeStruct((HEADS, t, CHUNK), BF16),
            jax.ShapeDtypeStruct((HEADS, t // CHUNK * 8, HEAD_DIM), F32),
            jax.ShapeDtypeStruct((HEADS, t, CHUNK), F32),
        ] + [jax.ShapeDtypeStruct((N_CHIPS,) + s.shape, s.dtype) for s in shards],
        scratch_shapes=[pltpu.SemaphoreType.DMA((3 * n,))] * (4 if n else 0),
        compiler_params=_params("arbitrary", "arbitrary") if n else _params("parallel", "parallel"),
    )(q, k, v, beta, gc, gt, gcr, *shards)
    return res[:7], res[7:]


def _gdn_prep_bwd(q, k, v, beta, gc, gt, gcr, tmat, dw, du, dqg, dkg, dattn, dgl, partials=()):
    t = q.shape[0]
    gch = min(PREP_BWD_CHUNKS, t // CHUNK)
    rows = gch * CHUNK
    steps = t // rows
    n_sc = len(partials)

    def body(*refs):
        (q_ref, k_ref, v_ref, b_ref, gc_ref, gt_ref, gcr_ref, tm_ref, dw_ref, du_ref, dqg_ref, dkg_ref, dat_ref, dgl_ref) = refs[:14]
        p_refs = refs[14:14 + n_sc]
        dq_ref, dk_ref, dv_ref, db_ref, dgc_ref, dgt_ref, dgcr_ref = refs[14 + n_sc:21 + n_sc]
        from_refs, sems = refs[21 + n_sc:21 + 2 * n_sc], refs[21 + 2 * n_sc:]
        h = pl.program_id(1)
        lane = _iota((1, LANES), 1)

        if n_sc:
            @pl.when(jnp.logical_and(pl.program_id(0) == 0, h == 0))
            def _():
                for cp in _scatter_copies(p_refs, from_refs, *sems):
                    cp.start()

        @pl.when(h == 0)
        def _():
            db_ref[...] = jnp.zeros_like(db_ref)
            dgc_ref[...] = jnp.zeros_like(dgc_ref)
            dgt_ref[...] = jnp.zeros_like(dgt_ref)

        sls = [pl.ds(c * CHUNK, CHUNK) for c in range(gch)]
        known = [tm_ref[0, sl, :] for sl in sls]
        _, vjp = jax.vjp(lambda *a: _prep_chunks(*a, tmats=known)[0], *_prep_inputs(q_ref, k_ref, v_ref, b_ref, gc_ref, gt_ref, gcr_ref, sls, h))
        cots = tuple((dw_ref[sl, :], du_ref[sl, :], dqg_ref[sl, :], dkg_ref[sl, :], dat_ref[0, sl, :], dgl_ref[0, pl.ds(c * 8, 8), :]) for c, sl in enumerate(sls))
        dqs, dks, dvs, dbs, dgcs, dgts, dgcrs = vjp(cots)
        for c, sl in enumerate(sls):
            dq_ref[sl, :] = dqs[c]
            dk_ref[sl, :] = dks[c]
            dv_ref[sl, :] = dvs[c]
            db_ref[sl, :] += jnp.where(lane == h, dbs[c], 0.0)
            dgc_ref[sl, :] += jnp.where(lane == h + HEADS, dgcs[c], 0.0)
            dgt_ref[sl, :] += jnp.where(lane == h + HEADS, dgts[c], 0.0)
            dgcr_ref[0, c] = dgcrs[c]

        if n_sc:
            @pl.when(jnp.logical_and(pl.program_id(0) == steps - 1, h == HEADS - 1))
            def _():
                for cp in _scatter_copies(p_refs, from_refs, *sems):
                    cp.wait()

    hb, col, gcrow, square, glb = _prep_specs(rows, gch)
    wide = HEADS * HEAD_DIM
    res = pl.pallas_call(
        body,
        name="gdn_prep_bwd",
        grid=(steps, HEADS),
        in_specs=[hb, hb, hb, col, col, col, gcrow, square, hb, hb, hb, hb, square, glb] + [_HBM] * n_sc,
        out_specs=[hb, hb, hb, col, col, col, gcrow] + [_HBM] * n_sc,
        out_shape=[jax.ShapeDtypeStruct((t, wide), F32)] * 3 + [jax.ShapeDtypeStruct((t, LANES), F32)] * 3 + [jax.ShapeDtypeStruct((HEADS, t // CHUNK, 1, CHUNK), F32)]
        + [jax.ShapeDtypeStruct((3,) + p.shape[1:], p.dtype) for p in partials],
        scratch_shapes=[pltpu.SemaphoreType.DMA((3 * n_sc,))] * (2 if n_sc else 0),
        compiler_params=_params("arbitrary", "arbitrary"),
    )(q, k, v, beta, gc, gt, gcr, tmat, dw, du, dqg, dkg, dattn, dgl, *partials)
    return res[:7], res[7:]


def _gdn_scan(w, u, qg, kg, attn, gl):
    t = w.shape[0]
    n = t // CHUNK
    nch = min(SCAN_CHUNKS, n)
    wide = HEADS * HEAD_DIM

    def body(w_ref, u_ref, qg_ref, kg_ref, at_ref, gl_ref, o_ref, st_ref, s_ref):
        @pl.when(pl.program_id(0) == 0)
        def _():
            s_ref[...] = jnp.zeros_like(s_ref)

        heads = range(HEADS)
        cols = [pl.ds(h * HEAD_DIM, HEAD_DIM) for h in heads]
        for c in range(nch):
            rows, gl_rows = pl.ds(c * CHUNK, CHUNK), pl.ds(c * 8, 8)
            ss = [s_ref[h] for h in heads]
            sbs = [s.astype(BF16) for s in ss]
            vbs = [(u_ref[rows, hs] - _dot(w_ref[rows, hs], sb)).astype(BF16) for hs, sb in zip(cols, sbs)]
            outs = [_dot(qg_ref[rows, hs], sb) + _dot(at_ref[h, rows, :], vb) for h, hs, sb, vb in zip(heads, cols, sbs, vbs)]
            new = [s * jnp.tile(gl_ref[h, gl_rows, :], (HEAD_DIM // 8, 1)) + _dot(kg_ref[rows, hs], vb, TN) for h, hs, s, vb in zip(heads, cols, ss, vbs)]
            for h, hs in zip(heads, cols):
                st_ref[c, h] = ss[h]
                o_ref[rows, hs] = outs[h]
                s_ref[h] = new[h]

    row = pl.BlockSpec((nch * CHUNK, wide), lambda i: (i, 0))
    return pl.pallas_call(
        body,
        name="gdn_scan",
        grid=(n // nch,),
        in_specs=[row, row, row, row, pl.BlockSpec((HEADS, nch * CHUNK, CHUNK), lambda i: (0, i, 0)), pl.BlockSpec((HEADS, nch * 8, HEAD_DIM), lambda i: (0, i, 0))],
        out_specs=[row, pl.BlockSpec((nch, HEADS, HEAD_DIM, HEAD_DIM), lambda i: (i, 0, 0, 0))],
        out_shape=[jax.ShapeDtypeStruct((t, wide), F32), jax.ShapeDtypeStruct((n, HEADS, HEAD_DIM, HEAD_DIM), F32)],
        scratch_shapes=[pltpu.VMEM((HEADS, HEAD_DIM, HEAD_DIM), F32)],
        compiler_params=_params("arbitrary"),
    )(w, u, qg, kg, attn, gl)


def _gdn_scan_bwd(w, u, qg, kg, attn, gl, states, do):
    t = w.shape[0]
    n = t // CHUNK
    nch = min(SCAN_CHUNKS, n)
    steps = n // nch
    wide = HEADS * HEAD_DIM

    def body(w_ref, u_ref, qg_ref, kg_ref, at_ref, gl_ref, st_ref, do_ref, dw_ref, du_ref, dqg_ref, dkg_ref, dat_ref, dgl_ref, ds_ref):
        @pl.when(pl.program_id(0) == 0)
        def _():
            ds_ref[...] = jnp.zeros_like(ds_ref)

        heads = range(HEADS)
        cols = [pl.ds(h * HEAD_DIM, HEAD_DIM) for h in heads]
        for c in reversed(range(nch)):
            rows, gl_rows = pl.ds(c * CHUNK, CHUNK), pl.ds(c * 8, 8)
            ss = [st_ref[c, h] for h in heads]
            sbs = [s.astype(BF16) for s in ss]
            dsns = [ds_ref[h] for h in heads]
            dsbs = [d.astype(BF16) for d in dsns]
            dobs = [do_ref[rows, hs].astype(BF16) for hs in cols]
            vbs = [(u_ref[rows, hs] - _dot(w_ref[rows, hs], sb)).astype(BF16) for hs, sb in zip(cols, sbs)]
            dvns = [_dot(at_ref[h, rows, :], dob, TN) + _dot(kg_ref[rows, hs], dsb) for h, hs, dob, dsb in zip(heads, cols, dobs, dsbs)]
            dvbs = [d.astype(BF16) for d in dvns]
            for h, hs in zip(heads, cols):
                dat_ref[h, rows, :] = _dot(dobs[h], vbs[h], NT)
                dqg_ref[rows, hs] = _dot(dobs[h], sbs[h], NT)
                dkg_ref[rows, hs] = _dot(vbs[h], dsbs[h], NT)
                du_ref[rows, hs] = dvns[h]
                dw_ref[rows, hs] = -_dot(dvbs[h], sbs[h], NT)
                dgl_ref[h, gl_rows, :] = jnp.sum((dsns[h] * ss[h]).reshape(HEAD_DIM // 8, 8, HEAD_DIM), axis=0)
            new = [dsn * jnp.tile(gl_ref[h, gl_rows, :], (HEAD_DIM // 8, 1)) + _dot(qg_ref[rows, hs], dob, TN) - _dot(w_ref[rows, hs], dvb, TN)
                   for h, hs, dsn, dob, dvb in zip(heads, cols, dsns, dobs, dvbs)]
            for h in heads:
                ds_ref[h] = new[h]

    row = pl.BlockSpec((nch * CHUNK, wide), lambda i: (steps - 1 - i, 0))
    at = pl.BlockSpec((HEADS, nch * CHUNK, CHUNK), lambda i: (0, steps - 1 - i, 0))
    glb = pl.BlockSpec((HEADS, nch * 8, HEAD_DIM), lambda i: (0, steps - 1 - i, 0))
    return pl.pallas_call(
        body,
        name="gdn_scan_bwd",
        grid=(steps,),
        in_specs=[row, row, row, row, at, glb, pl.BlockSpec((nch, HEADS, HEAD_DIM, HEAD_DIM), lambda i: (steps - 1 - i, 0, 0, 0)), row],
        out_specs=[row, row, row, row, at, glb],
        out_shape=[jax.ShapeDtypeStruct((t, wide), F32)] * 4 + [jax.ShapeDtypeStruct((HEADS, t, CHUNK), F32), jax.ShapeDtypeStruct((HEADS, n * 8, HEAD_DIM), F32)],
        scratch_shapes=[pltpu.VMEM((HEADS, HEAD_DIM, HEAD_DIM), F32)],
        compiler_params=_params("arbitrary"),
    )(w, u, qg, kg, attn, gl, states, do)


SB_Q = 512
SB_K = 256
SB_STEP = 1
SB_DEAD = -105.0


def _sb_scores(q, k):
    z = _dot(q, k, NT) * (HEAD_DIM ** -0.5)
    e = jnp.exp(-jnp.abs(z))
    lb = jnp.minimum(z, 0.0) - jnp.log(1.0 + e)
    return z, e, lb, lb - z


def _tri(n, rel):
    return jnp.where(rel(_iota((n, n), 0), _iota((n, n), 1)), 1.0, 0.0).astype(BF16)


def _lanes(col):
    return jnp.broadcast_to(col, (col.shape[0], LANES))


def _sb_fwd(q, k, v):
    t = q.shape[0]
    bq, bk = min(SB_Q, t), min(SB_K, t)
    nsub, rep = bq // bk, bk // LANES
    nstep = min(SB_STEP, nsub)
    steps_per_tile = nsub // nstep

    def body(q_ref, k_ref, v_ref, o_ref, rt_ref, first_ref):
        h = pl.program_id(0)
        i = pl.program_id(1)
        o_ref[...] = jnp.zeros_like(o_ref)
        rt_ref[...] = jnp.zeros_like(rt_ref)
        after = _tri(bk, lambda r, c: r > c)

        def block(j, r0, diag):
            st = pl.multiple_of(j * bk, bk)
            kv, vv = k_ref[pl.ds(st, bk), :], v_ref[pl.ds(st, bk), :]
            _, _, lb, l1m = _sb_scores(q_ref[r0:, :], kv)
            if diag:
                mask = _iota((bq - r0, bk), 1) + j * bk < _iota((bq - r0, bk), 0) + (r0 + i * bq)
                l1m = jnp.where(mask, l1m, 0.0)
            sums = _two_pass(l1m, after)
            run = rt_ref[r0:, :]
            a = jnp.exp(lb + jnp.tile(run, (1, rep)) + sums)
            if diag:
                a = jnp.where(mask, a, 0.0)
            o_ref[r0:, :] += _dot(a.astype(BF16), vv)
            rt_ref[r0:, :] = run + _lanes(sums[:, 0:1] + l1m[:, 0:1])

        for s in reversed(range(nsub)):
            block(i * nsub + s, s * bk, True)

        def alive(carry):
            u, highest = carry
            return jnp.logical_and(u >= 0, highest > SB_DEAD)

        def step(carry):
            u, _ = carry
            for s in reversed(range(nstep)):
                block(u * nstep + s, 0, False)
            return u - 1, jnp.max(rt_ref[...])

        u_end, _ = lax.while_loop(alive, step, (i * steps_per_tile - 1, jnp.max(rt_ref[...])))
        first_ref[h, i] = u_end + 1

    qb = pl.BlockSpec((bq, HEAD_DIM), lambda h, i: (i, h))
    full = pl.BlockSpec((t, HEAD_DIM), lambda h, i: (0, h))
    return pl.pallas_call(
        body,
        name="sb_fwd",
        grid=(HEADS, t // bq),
        in_specs=[qb, full, full],
        out_specs=[qb, qb, pl.BlockSpec(memory_space=pltpu.SMEM)],
        out_shape=[jax.ShapeDtypeStruct(q.shape, F32), jax.ShapeDtypeStruct(q.shape, F32), jax.ShapeDtypeStruct((HEADS, t // bq), jnp.int32)],
        compiler_params=_params("arbitrary", "arbitrary"),
    )(q, k, v)


def _sb_bwd(q, k, v, rt, first, do):
    t = q.shape[0]
    bq, bk = min(SB_Q, t), min(SB_K, t)
    nsub, rep = bq // bk, bk // LANES
    nstep = min(SB_STEP, nsub)
    steps_per_tile = nsub // nstep
    scale = HEAD_DIM ** -0.5

    def body(first_ref, q_ref, k_ref, v_ref, rt_ref, do_ref, dq_ref, dk_ref, dv_ref, left_ref, pg_ref):
        h = pl.program_id(0)
        i = pl.program_id(1)

        @pl.when(i == 0)
        def _():
            dk_ref[...] = jnp.zeros_like(dk_ref)
            dv_ref[...] = jnp.zeros_like(dv_ref)

        dq_ref[...] = jnp.zeros_like(dq_ref)
        left_ref[...] = jnp.zeros_like(left_ref)
        pg_ref[...] = jnp.zeros_like(pg_ref)
        upto = _tri(bk, lambda r, c: r <= c)

        def block(j, r0, diag):
            st = pl.multiple_of(j * bk, bk)
            kv, vv = k_ref[pl.ds(st, bk), :], v_ref[pl.ds(st, bk), :]
            qv = q_ref[r0:, :]
            dob = do_ref[r0:, :].astype(BF16)
            _, _, lb, l1m = _sb_scores(qv, kv)
            if diag:
                mask = _iota((bq - r0, bk), 1) + j * bk < _iota((bq - r0, bk), 0) + (r0 + i * bq)
                l1m = jnp.where(mask, l1m, 0.0)
            sums = _two_pass(l1m, upto)
            left = left_ref[r0:, :]
            a = jnp.exp(lb + jnp.tile(rt_ref[r0:, :] - left, (1, rep)) - sums)
            if diag:
                a = jnp.where(mask, a, 0.0)
            g = _dot(dob, vv, NT) * a
            dv_ref[pl.ds(st, bk), :] += _dot(a.astype(BF16), dob, TN)
            gsum = _two_pass(g, upto)
            pg = pg_ref[r0:, :]
            dz = g - jnp.exp(lb) * (jnp.tile(pg, (1, rep)) + gsum)
            if diag:
                dz = jnp.where(mask, dz, 0.0)
            dzb = (dz * scale).astype(BF16)
            dk_ref[pl.ds(st, bk), :] += _dot(dzb, qv, TN)
            dq_ref[r0:, :] += _dot(dzb, kv)
            left_ref[r0:, :] = left + _lanes(sums[:, bk - 1:bk])
            pg_ref[r0:, :] = pg + _lanes(gsum[:, bk - 1:bk])

        def step(u, carry):
            for s in range(nstep):
                block(u * nstep + s, 0, False)
            return carry

        lax.fori_loop(first_ref[h, i], i * steps_per_tile, step, 0)
        for s in range(nsub):
            block(i * nsub + s, s * bk, True)

    qb = pl.BlockSpec((bq, HEAD_DIM), lambda h, i: (i, h))
    full = pl.BlockSpec((t, HEAD_DIM), lambda h, i: (0, h))
    return pl.pallas_call(
        body,
        name="sb_bwd",
        grid=(HEADS, t // bq),
        in_specs=[pl.BlockSpec(memory_space=pltpu.SMEM), qb, full, full, qb, qb],
        out_specs=[qb, full, full],
        out_shape=[jax.ShapeDtypeStruct(q.shape, F32)] * 3,
        scratch_shapes=[pltpu.VMEM((bq, LANES), F32), pltpu.VMEM((bq, LANES), F32)],
        compiler_params=_params("arbitrary", "arbitrary"),
    )(first, q, k, v, rt, do)


def _adamw(w, g, m, v, name, tm=256):
    r, c = w.shape
    tm = tm if r % tm == 0 else r

    def body(w_ref, g_ref, m_ref, v_ref, d_ref, nm_ref, nv_ref):
        gv = g_ref[...]
        nm = ADAM_B1 * m_ref[...] + (1.0 - ADAM_B1) * gv
        nv = ADAM_B2 * v_ref[...] + (1.0 - ADAM_B2) * (gv * gv)
        m_hat = nm / (1.0 - ADAM_B1 ** ADAM_STEP)
        v_hat = nv / (1.0 - ADAM_B2 ** ADAM_STEP)
        d_ref[...] = -ADAM_LR * (m_hat / (jnp.sqrt(v_hat) + ADAM_EPS) + ADAM_WD * w_ref[...])
        nm_ref[...] = nm
        nv_ref[...] = nv

    blk = pl.BlockSpec((tm, c), lambda i: (i, 0))
    return pl.pallas_call(
        body,
        name=name,
        grid=(r // tm,),
        in_specs=[blk] * 4,
        out_specs=[blk] * 3,
        out_shape=[jax.ShapeDtypeStruct((r, c), F32)] * 3,
        compiler_params=_params("parallel"),
    )(w, g, m, v)


def _local_step(x, tgt, gains, w_in, small, shards, assemble, early_reduce=None):
    mix_pre, mix_post, mlp_pre, mlp_post, kv_gain = gains
    w_qkvg, w_ba = w_in
    conv_w, a_log, dt_bias, out_gain = small
    t, d = x.shape
    row = lambda a, i=None: a[i:i + 1] if i is not None else a
    al = jnp.zeros((1, LANES), F32).at[:, HEADS:2 * HEADS].set(a_log)
    dtb = jnp.zeros((1, LANES), F32).at[:, HEADS:2 * HEADS].set(dt_bias)
    og = jnp.tile(out_gain, (1, HEADS))
    full = lambda a: (a, a.shape[1], 0)

    (h0,) = _rowwise("norm_in", _fn_norm, [full(x)], [row(mix_pre, 0)], [(d, BF16)])
    qkvg = _matmul(h0, w_qkvg, "nn", F32, "mm_gdn_in", tk=1024)
    ba = _matmul(h0, w_ba, "nn", F32, "mm_gdn_ba", tk=1024)
    (conv, gq, gk, gv), gathered_conv = _conv_fwd(qkvg, conv_w, shards[0])
    beta, gc, gt = _rowwise("gates", _fn_gates, [full(ba)], [al, dtb], [(LANES, F32)] * 3)
    gcr = jnp.swapaxes(gc[:, HEADS:2 * HEADS], 0, 1).reshape(HEADS, t // CHUNK, 1, CHUNK)
    (pw, pu, pqg, pkg, pattn, pgl, ptm), gathered_prep = _gdn_prep(gq, gk, gv, beta, gc, gt, gcr, shards[1])
    w_out, w_kv, w_q, w_o, w_up, w_down = assemble(gathered_conv, gathered_prep)
    o_gdn, states = _gdn_scan(pw, pu, pqg, pkg, pattn, pgl)
    (on,) = _rowwise("out_norm", _fn_outnorm, [full(o_gdn), (qkvg, d, 3)], [og], [(d, BF16)])
    mix0 = _matmul(on, w_out, "nn", F32, "mm_gdn_out", tk=1024)
    x1, h1 = _rowwise("res_a0", _fn_res_norm, [full(x), full(mix0)], [row(mix_post, 0), row(mlp_pre, 0)], [(d, F32), (d, BF16)])
    (a0,) = _matmul(h1, w_up[0], "nn", (BF16,), "mm_up0", tk=1024, epilogue=_relu2_of)
    d0 = _matmul(a0, w_down[0], "nn", F32, "mm_down0")
    x2, hkv, hq = _rowwise("res_b0", _fn_res_norm2, [full(x1), full(d0)], [row(mlp_post, 0), kv_gain, row(mix_pre, 1)], [(d, F32), (d, BF16), (d, BF16)])
    w_k, w_v = w_kv[:, :d], w_kv[:, d:]
    kp = _matmul(hkv, w_k, "nn", BF16, "mm_k", tk=1024)
    vp = _matmul(hkv, w_v, "nn", BF16, "mm_v", tk=1024)
    qp = _matmul(hq, w_q, "nn", BF16, "mm_q", tk=1024)
    o_sb, rt, sb_first = _sb_fwd(qp, kp, vp)
    mix1 = _matmul(o_sb, w_o, "nn", F32, "mm_sb_out", tk=1024)
    x3, h3 = _rowwise("res_a1", _fn_res_norm, [full(x2), full(mix1)], [row(mix_post, 1), row(mlp_pre, 1)], [(d, F32), (d, BF16)])
    (a1,) = _matmul(h3, w_up[1], "nn", (BF16,), "mm_up1", tk=1024, epilogue=_relu2_of)
    d1 = _matmul(a1, w_down[1], "nn", F32, "mm_down1")

    loss, dx3, dd1, g_mlp_post1 = _loss_call(x3, d1, tgt, row(mlp_post, 1))
    (du1,) = _matmul(dd1, w_down[1], "nt", (BF16,), "mm_down1_dx", tk=1024, epilogue=_relu2_cotangent, extras=[a1])
    g_down1 = _matmul(a1, dd1, "tn", F32, "mm_down1_dw")
    dh3 = _matmul(du1, w_up[1], "nt", F32, "mm_up1_dx")
    g_up1 = _matmul(h3, du1, "tn", F32, "mm_up1_dw")
    (dx2, dmix1), (g_mix_post1, g_mlp_pre1) = _rowwise_bwd(
        "res_a1_bwd", _fn_res_norm, [full(x2), full(mix1)], [row(mix_post, 1), row(mlp_pre, 1)], [dx3, dh3], [F32, BF16])
    do_sb = _matmul(dmix1, w_o, "nt", BF16, "mm_sb_out_dx")
    g_o = _matmul(o_sb, dmix1, "tn", F32, "mm_sb_out_dw")
    dqp, dkp, dvp = _sb_bwd(qp, kp, vp, rt, sb_first, do_sb)
    dhq = _matmul(dqp, w_q, "nt", F32, "mm_q_dx")
    g_q = _matmul(hq, dqp, "tn", F32, "mm_q_dw")
    dhkv = _matmul(dvp, w_v, "nt", F32, "mm_v_dx", add=_matmul(dkp, w_k, "nt", F32, "mm_k_dx"))
    g_kv = jnp.concatenate([_matmul(hkv, dkp, "tn", F32, "mm_k_dw"), _matmul(hkv, dvp, "tn", F32, "mm_v_dw")], axis=1)
    (dx1, dd0), (g_mlp_post0, g_kv_gain, g_mix_pre1) = _rowwise_bwd(
        "res_b0_bwd", _fn_res_norm2, [full(x1), full(d0)], [row(mlp_post, 0), kv_gain, row(mix_pre, 1)], [dx2, dhkv, dhq], [F32, BF16])
    (du0,) = _matmul(dd0, w_down[0], "nt", (BF16,), "mm_down0_dx", tk=1024, epilogue=_relu2_cotangent, extras=[a0])
    g_down0 = _matmul(a0, dd0, "tn", F32, "mm_down0_dw")
    dh1 = _matmul(du0, w_up[0], "nt", F32, "mm_up0_dx")
    g_up0 = _matmul(h1, du0, "tn", F32, "mm_up0_dw")
    (dx0, dmix0), (g_mix_post0, g_mlp_pre0) = _rowwise_bwd(
        "res_a0_bwd", _fn_res_norm, [full(x), full(mix0)], [row(mix_post, 0), row(mlp_pre, 0)], [dx1, dh1], [F32, BF16])
    don = _matmul(dmix0, w_out, "nt", F32, "mm_gdn_out_dx")
    g_out = _matmul(on, dmix0, "tn", F32, "mm_gdn_out_dw")
    (do_gdn, dgate), (g_og,) = _rowwise_bwd("out_norm_bwd", _fn_outnorm, [full(o_gdn), (qkvg, d, 3)], [og], [don], [F32, F32])
    dpw, dpu, dpqg, dpkg, dpattn, dpgl = _gdn_scan_bwd(pw, pu, pqg, pkg, pattn, pgl, states, do_gdn)
    partial, partial_bf16 = [], ()
    if early_reduce is not None:
        partial, partial_bf16 = early_reduce(dict(mlp_w_up=(g_up0, g_up1), mlp_w_down=(g_down0, g_down1), gdn_w_out=g_out[None], w_kv=g_kv, sb_w_q=g_q[None], sb_w_o=g_o[None]))
    (dgq, dgk, dgv, dbeta, dgc, dgt, dgcr), from_chips = _gdn_prep_bwd(gq, gk, gv, beta, gc, gt, gcr, ptm, dpw, dpu, dpqg, dpkg, dpattn, dpgl, partial_bf16)
    dgcr_lanes = jnp.pad(jnp.swapaxes(dgcr.reshape(HEADS, t), 0, 1), ((0, 0), (HEADS, LANES - 2 * HEADS)))
    gate_cots = [dbeta, dgc + dgcr_lanes, dgt]
    (dba,), (g_al, g_dtb) = _rowwise_bwd("gates_bwd", _fn_gates, [full(ba)], [al, dtb], gate_cots, [BF16])
    dqkvg, g_conv = _conv_bwd(conv, (dgq, dgk, dgv), dgate, qkvg, conv_w)
    dh0b = _matmul(dba, w_ba, "nt", F32, "mm_gdn_ba_dx", tk=LANES)
    dh0 = _matmul(dqkvg, w_qkvg, "nt", F32, "mm_gdn_in_dx", add=dh0b)
    g_qkvg = _matmul(h0, dqkvg, "tn", F32, "mm_gdn_in_dw")
    g_ba = _matmul(h0, dba, "tn", F32, "mm_gdn_ba_dw")
    (grad_x,), (g_mix_pre0,) = _rowwise_bwd("norm_in_bwd", lambda xx, gg: (_rms(xx, gg), xx), [full(x)], [row(mix_pre, 0)], [dh0, dx0], [F32])

    grads = dict(
        mix_pre_gain=jnp.concatenate([g_mix_pre0, g_mix_pre1], axis=0),
        mix_post_gain=jnp.concatenate([g_mix_post0, g_mix_post1], axis=0),
        mlp_pre_gain=jnp.concatenate([g_mlp_pre0, g_mlp_pre1], axis=0),
        mlp_post_gain=jnp.concatenate([g_mlp_post0, g_mlp_post1], axis=0),
        mlp_w_up=(g_up0, g_up1),
        mlp_w_down=(g_down0, g_down1),
        gdn_w_in=jnp.concatenate([g_qkvg, g_ba[:, :2 * HEADS]], axis=1)[None],
        gdn_conv_w=g_conv[None, :CONV_K],
        gdn_a_log=g_al[:, HEADS:2 * HEADS],
        gdn_dt_bias=g_dtb[:, HEADS:2 * HEADS],
        gdn_out_gain=jnp.sum(g_og.reshape(HEADS, HEAD_DIM), axis=0, keepdims=True),
        gdn_w_out=g_out[None],
        kv_gain=g_kv_gain[0],
        w_kv=g_kv,
        sb_w_q=g_q[None],
        sb_w_o=g_o[None],
    )
    return loss, grad_x, grads, (partial, from_chips)


N_DEV = 8
N_CHIPS = 4
PACK_ROW_TILE = 128

_HBM = pl.BlockSpec(memory_space=pltpu.HBM)


def _place():
    return lax.axis_index("x"), lax.axis_index("y"), lax.axis_index("c")


def _other_chips(x, y):
    return [(1 - x, y), (x, 1 - y), (1 - x, 1 - y)]


def _remote(src, dst, send_sem, recv_sem, to):
    return pltpu.make_async_remote_copy(src_ref=src, dst_ref=dst, send_sem=send_sem, recv_sem=recv_sem, device_id=to, device_id_type=MESH)


def _gather8(v, name):
    rows, cols = v.shape

    def body(v_ref, out_ref, sum_ref, send_sems, recv_sems, local_sem):
        x, y, c = _place()
        me, sibling = (x, y, c), (x, y, 1 - c)
        chips = _other_chips(x, y)

        def blk(px, py, pc):
            return out_ref.at[pl.ds((4 * px + 2 * py + pc) * rows, rows), :]

        def copy(k, block, to, src=None):
            return _remote(blk(*block) if src is None else src, blk(*block), send_sems.at[k], recv_sems.at[k], to)

        mine = pltpu.make_async_copy(v_ref, blk(*me), local_sem)
        mine.start()
        first = [copy(0, me, sibling, src=v_ref)] + [copy(1 + j, me, (*chip, c), src=v_ref) for j, chip in enumerate(chips)]
        for cp in first:
            cp.start()
        passed = [copy(4 + j, (*chip, c), sibling) for j, chip in enumerate(chips)]
        for j, chip in enumerate(chips):
            copy(1 + j, (*chip, c), me).wait_recv()
            passed[j].start()
        copy(0, sibling, me).wait_recv()
        for j, chip in enumerate(chips):
            copy(4 + j, (*chip, 1 - c), me).wait_recv()
        for cp in first + passed:
            cp.wait_send()
        mine.wait()
        acc = out_ref[pl.ds(0, rows), :]
        for dev in range(1, N_DEV):
            acc = acc + out_ref[pl.ds(dev * rows, rows), :]
        sum_ref[...] = acc

    vm = pl.BlockSpec(memory_space=pltpu.VMEM)
    return pl.pallas_call(
        body,
        name=name,
        out_shape=[jax.ShapeDtypeStruct((N_DEV * rows, cols), v.dtype), jax.ShapeDtypeStruct((rows, cols), v.dtype)],
        in_specs=[vm],
        out_specs=[vm, vm],
        scratch_shapes=[pltpu.SemaphoreType.DMA((7,)), pltpu.SemaphoreType.DMA((7,)), pltpu.SemaphoreType.DMA],
    )(v)


def _hbm_call(body, name, arrs, out_shapes, sem_counts):
    n = len(arrs)

    def wrapped(*refs):
        body(refs[:n], refs[n:2 * n], *refs[2 * n:])

    return pl.pallas_call(
        wrapped,
        name=name,
        out_shape=[jax.ShapeDtypeStruct(s, a.dtype) for s, a in zip(out_shapes, arrs)],
        in_specs=[_HBM] * n,
        out_specs=[_HBM] * n,
        scratch_shapes=[pltpu.SemaphoreType.DMA((k,)) for k in sem_counts],
    )(*arrs)


def _gather_sends(w_refs, out_refs, send_sems, recv_sems):
    x, y, c = _place()
    s_me = 2 * x + y
    return [_remote(w.at[c], o.at[s_me, c], send_sems.at[3 * a + j], recv_sems.at[3 * a + j], (px, py, c))
            for a, (w, o) in enumerate(zip(w_refs, out_refs)) for j, (px, py) in enumerate(_other_chips(x, y))]


def _gather_finish(w_refs, out_refs, send_sems, recv_sems, fsend_sems, frecv_sems):
    x, y, c = _place()
    chips = _other_chips(x, y)
    passed = []
    for a, o in enumerate(out_refs):
        for j, (px, py) in enumerate(chips):
            half = o.at[2 * px + py, c]
            _remote(half, half, send_sems.at[3 * a + j], recv_sems.at[3 * a + j], (px, py, c)).wait_recv()
            fwd = _remote(half, half, fsend_sems.at[3 * a + j], frecv_sems.at[3 * a + j], (x, y, 1 - c))
            fwd.start()
            passed.append(fwd)
    for a, o in enumerate(out_refs):
        for j, (px, py) in enumerate(chips):
            half = o.at[2 * px + py, 1 - c]
            _remote(half, half, fsend_sems.at[3 * a + j], frecv_sems.at[3 * a + j], (x, y, 1 - c)).wait_recv()
    for cp in _gather_sends(w_refs, out_refs, send_sems, recv_sems) + passed:
        cp.wait_send()


def _gather_weights(arrs):
    n = len(arrs)

    def body(w_refs, out_refs, send_sems, recv_sems, fsend_sems, frecv_sems):
        for cp in _gather_sends(w_refs, out_refs, send_sems, recv_sems):
            cp.start()
        _gather_finish(w_refs, out_refs, send_sems, recv_sems, fsend_sems, frecv_sems)

    return _hbm_call(body, "gather_weights", arrs, [(N_CHIPS,) + a.shape for a in arrs], [3 * n] * 4)


def _swap_halves(arrs, name):
    n = len(arrs)

    def body(g_refs, a_refs, send_sems, recv_sems):
        x, y, c = _place()
        cps = [_remote(g.at[1 - c], a, send_sems.at[i], recv_sems.at[i], (x, y, 1 - c)) for i, (g, a) in enumerate(zip(g_refs, a_refs))]
        for cp in cps:
            cp.start()
        for cp in cps:
            cp.wait()

    return _hbm_call(body, name, arrs, [a.shape[1:] for a in arrs], [n, n])


def _scatter_copies(p_refs, b_refs, send_sems, recv_sems):
    x, y, c = _place()
    return [_remote(p.at[2 * px + py], b.at[j], send_sems.at[3 * i + j], recv_sems.at[3 * i + j], (px, py, c))
            for i, (p, b) in enumerate(zip(p_refs, b_refs)) for j, (px, py) in enumerate(_other_chips(x, y))]


def _scatter_to_chips(arrs):
    n = len(arrs)

    def body(p_refs, b_refs, send_sems, recv_sems):
        cps = _scatter_copies(p_refs, b_refs, send_sems, recv_sems)
        for cp in cps:
            cp.start()
        for cp in cps:
            cp.wait()

    return _hbm_call(body, "grads_to_chips", arrs, [(3,) + a.shape[1:] for a in arrs], [3 * n, 3 * n])


def _share_halves(arrs):
    n = len(arrs)

    def body(q_refs, out_refs, send_sems, recv_sems):
        x, y, c = _place()
        cps = [_remote(q, o, send_sems.at[i], recv_sems.at[i], (x, y, 1 - c)) for i, (q, o) in enumerate(zip(q_refs, out_refs))]
        for cp in cps:
            cp.start()
        for cp in cps:
            cp.wait()

    return _hbm_call(body, "grads_share", arrs, [a.shape for a in arrs], [n, n])


_GROUPS = (
    (("gdn_w_out", (1, 256, 1024), "rows"), ("mlp_w_up", (2, 1024, 1024), "cols")),
    (("mlp_w_down", (2, 1024, 1024), "rows"), ("sb_w_q", (1, 256, 1024), "rows"), ("sb_w_o", (1, 256, 1024), "rows")),
    (("w_kv", (1024, 512), "cols"),),
    (("gdn_w_in", (1, 1024, 1028), "cols"),),
)
_BEHIND_CONV, _BEHIND_PREP, _FIRST = slice(0, 1), slice(1, 3), slice(3, 4)
_EARLY_GRADS = slice(0, 3)


def _numel(shape):
    n = 1
    for s in shape:
        n *= s
    return n


def _half_rows(shape):
    return _numel(shape[:-1]) // 2


def _pack_shards(shards, dtype):
    return tuple(jnp.concatenate([shards[n].astype(dtype).reshape(2, _half_rows(shape), shape[-1]) for n, shape, _ in grp], axis=1) for grp in _GROUPS)


def _unpack_shards(bufs):
    out = {}
    for grp, buf in zip(_GROUPS, bufs):
        off = 0
        for n, shape, _ in grp:
            out[n] = buf[:, off:off + _half_rows(shape)].reshape(shape)
            off += _half_rows(shape)
    return out


def _join(stacked, how):
    nd = stacked.ndim - 1
    ax = nd - 1 if how == "cols" else nd - 2
    moved = jnp.moveaxis(stacked, 0, ax)
    shape = list(stacked.shape[1:])
    shape[ax] *= N_CHIPS
    return moved.reshape(shape)


def _split(full, shard_shape, how):
    nd = len(shard_shape)
    ax = nd - 1 if how == "cols" else nd - 2
    shape = list(shard_shape)
    shape.insert(ax, N_CHIPS)
    return jnp.moveaxis(full.reshape(shape), ax, 0)


def _unpack_full(gathered, groups):
    out = {}
    for grp, buf in zip(groups, gathered):
        off = 0
        for n, shape, how in grp:
            out[n] = _join(buf[:, :, off:off + _half_rows(shape)].reshape((N_CHIPS,) + shape), how)
            off += _half_rows(shape)
    return out


def _pack_full(full, groups):
    bufs = []
    for grp in groups:
        parts = []
        for n, shape, how in grp:
            if isinstance(full[n], tuple):
                assert len(full[n]) == shape[0] == 2
                parts.append(jnp.stack([_split(layer, shape[1:], how) for layer in full[n]], axis=1))
            else:
                parts.append(_split(full[n], shape, how).reshape(N_CHIPS, 2, _half_rows(shape), shape[-1]))
        buf = jnp.swapaxes(jnp.concatenate(parts, axis=2), 0, 1)
        bufs.append(buf.reshape(2, -1, buf.shape[-1]))
    return tuple(bufs)


_SMALL = (
    ("mix_pre_gain", (2, 1024)),
    ("mix_post_gain", (2, 1024)),
    ("mlp_pre_gain", (2, 1024)),
    ("mlp_post_gain", (2, 1024)),
    ("kv_gain", (1024,)),
    ("gdn_out_gain", (1, 128)),
    ("gdn_a_log", (1, 8)),
    ("gdn_dt_bias", (1, 8)),
    ("gdn_conv_w", (1, 4, 3072)),
    ("loss", ()),
)


def _rows_of(shape):
    return -(-_numel(shape) // LANES)


_SMALL_ROWS = -(-sum(_rows_of(s) for _, s in _SMALL) // 8) * 8


def _pack_small(vals):
    parts = []
    for n, shape in _SMALL:
        flat = vals[n].reshape(-1)
        parts.append(jnp.pad(flat, (0, _rows_of(shape) * LANES - flat.shape[0])))
    flat = jnp.concatenate(parts)
    return jnp.pad(flat, (0, _SMALL_ROWS * LANES - flat.shape[0])).reshape(_SMALL_ROWS, LANES)


def _unpack_small(packed):
    flat = packed.reshape(-1)
    out, off = {}, 0
    for n, shape in _SMALL:
        out[n] = flat[off:off + _numel(shape)].reshape(shape)
        off += _rows_of(shape) * LANES
    return out


_WEIGHTS = ("mix_pre_gain", "mix_post_gain", "mlp_pre_gain", "mlp_post_gain", "mlp_w_up", "mlp_w_down", "gdn_w_in", "gdn_conv_w",
            "gdn_a_log", "gdn_dt_bias", "gdn_out_gain", "gdn_w_out", "kv_gain", "w_kv", "sb_w_q", "sb_w_o")


def _as2d(a):
    return a.reshape(1, -1) if a.ndim <= 1 else a.reshape(-1, a.shape[-1])


def kernel(x, mix_pre_gain, mix_post_gain, mlp_pre_gain, mlp_post_gain, mlp_w_up, mlp_w_down, gdn_w_in, gdn_conv_w, gdn_a_log, gdn_dt_bias, gdn_out_gain, gdn_w_out, kv_gain, w_kv, sb_w_q, sb_w_o, loss_target, m_mix_pre_gain, m_mix_post_gain, m_mlp_pre_gain, m_mlp_post_gain, m_mlp_w_up, m_mlp_w_down, m_gdn_w_in, m_gdn_conv_w, m_gdn_a_log, m_gdn_dt_bias, m_gdn_out_gain, m_gdn_w_out, m_kv_gain, m_w_kv, m_sb_w_q, m_sb_w_o, v_mix_pre_gain, v_mix_post_gain, v_mlp_pre_gain, v_mlp_post_gain, v_mlp_w_up, v_mlp_w_down, v_gdn_w_in, v_gdn_conv_w, v_gdn_a_log, v_gdn_dt_bias, v_gdn_out_gain, v_gdn_w_out, v_kv_gain, v_w_kv, v_sb_w_q, v_sb_w_o):
    w = dict(mix_pre_gain=mix_pre_gain, mix_post_gain=mix_post_gain, mlp_pre_gain=mlp_pre_gain, mlp_post_gain=mlp_post_gain, mlp_w_up=mlp_w_up, mlp_w_down=mlp_w_down, gdn_w_in=gdn_w_in, gdn_conv_w=gdn_conv_w, gdn_a_log=gdn_a_log, gdn_dt_bias=gdn_dt_bias, gdn_out_gain=gdn_out_gain, gdn_w_out=gdn_w_out, kv_gain=kv_gain, w_kv=w_kv, sb_w_q=sb_w_q, sb_w_o=sb_w_o)
    m = dict(mix_pre_gain=m_mix_pre_gain, mix_post_gain=m_mix_post_gain, mlp_pre_gain=m_mlp_pre_gain, mlp_post_gain=m_mlp_post_gain, mlp_w_up=m_mlp_w_up, mlp_w_down=m_mlp_w_down, gdn_w_in=m_gdn_w_in, gdn_conv_w=m_gdn_conv_w, gdn_a_log=m_gdn_a_log, gdn_dt_bias=m_gdn_dt_bias, gdn_out_gain=m_gdn_out_gain, gdn_w_out=m_gdn_w_out, kv_gain=m_kv_gain, w_kv=m_w_kv, sb_w_q=m_sb_w_q, sb_w_o=m_sb_w_o)
    v = dict(mix_pre_gain=v_mix_pre_gain, mix_post_gain=v_mix_post_gain, mlp_pre_gain=v_mlp_pre_gain, mlp_post_gain=v_mlp_post_gain, mlp_w_up=v_mlp_w_up, mlp_w_down=v_mlp_w_down, gdn_w_in=v_gdn_w_in, gdn_conv_w=v_gdn_conv_w, gdn_a_log=v_gdn_a_log, gdn_dt_bias=v_gdn_dt_bias, gdn_out_gain=v_gdn_out_gain, gdn_w_out=v_gdn_w_out, kv_gain=v_kv_gain, w_kv=v_w_kv, sb_w_q=v_sb_w_q, sb_w_o=v_sb_w_o)
    cx, cy, cc = _place()
    chip = 2 * cx + cy
    conv_cols = gdn_conv_w.shape[-1]

    own = _pack_shards(w, BF16)
    with_own = lambda gathered, mine: [lax.dynamic_update_index_in_dim(g, m, chip, 0) for g, m in zip(gathered, mine)]
    w_in = _unpack_full(with_own(_gather_weights(own[_FIRST]), own[_FIRST]), _GROUPS[_FIRST])["gdn_w_in"][0]

    def assemble(gathered_conv, gathered_prep):
        full = {**_unpack_full(with_own(gathered_conv, own[_BEHIND_CONV]), _GROUPS[_BEHIND_CONV]),
                **_unpack_full(with_own(gathered_prep, own[_BEHIND_PREP]), _GROUPS[_BEHIND_PREP])}
        return full["gdn_w_out"][0], full["w_kv"], full["sb_w_q"][0], full["sb_w_o"][0], full["mlp_w_up"], full["mlp_w_down"]

    conv_rows = jnp.pad(gdn_conv_w[0], ((0, 8 - CONV_K), (0, 0))).reshape(-1, LANES)
    conv_all, _ = _gather8(conv_rows, "gather_conv_w")
    conv_all = conv_all.reshape(N_CHIPS, 2, 8, conv_cols)[:, 0, :CONV_K]
    conv_full = jnp.swapaxes(conv_all, 0, 1).reshape(CONV_K, N_CHIPS * conv_cols)

    w_in = (w_in[:, :4 * HEADS * HEAD_DIM], jnp.pad(w_in[:, 4 * HEADS * HEAD_DIM:], ((0, 0), (0, LANES - 2 * HEADS))))
    gains = (mix_pre_gain, mix_post_gain, mlp_pre_gain, mlp_post_gain, kv_gain[None])
    small = (conv_full, gdn_a_log, gdn_dt_bias, gdn_out_gain)
    tile = PACK_ROW_TILE

    def to_chip_partials(grads_full, groups, tag):
        bufs = _pack_full(grads_full, groups)
        p32, p16 = [], []
        for i, (buf, other) in enumerate(zip(bufs, _swap_halves(bufs, f"grads_to_sibling_{tag}"))):
            _, n, cols = buf.shape
            p, pb = _add_rows(f"grads_add_sibling_{tag}{i}", [(buf.reshape(2 * n, cols), cc * (n // tile)), (other, 0)], n, (F32, BF16), tile)
            p32.append(p.reshape(N_CHIPS, -1, cols))
            p16.append(pb.reshape(N_CHIPS, -1, cols))
        return p32, tuple(p16)

    loss_rows, grad_x, g_full, (partial_early, from_chips_early) = _local_step(
        x[0], loss_target[0], gains, w_in, small, (own[_BEHIND_CONV], own[_BEHIND_PREP]), assemble, lambda g: to_chip_partials(g, _GROUPS[_EARLY_GRADS], "early"))

    partial_in, partial_in_bf16 = to_chip_partials(g_full, _GROUPS[_FIRST], "in")
    partial = list(partial_early) + partial_in
    from_chips = list(from_chips_early) + list(_scatter_to_chips(partial_in_bf16))
    reduced = []
    for i, (p, others) in enumerate(zip(partial, from_chips)):
        _, r, cols = p.shape
        terms = [(p.reshape(N_CHIPS * r, cols), chip * (r // tile))] + [(others.reshape(3 * r, cols), j * (r // tile)) for j in range(3)]
        reduced.append(_add_rows(f"grads_add_chips_{i}", terms, r, (F32,), tile)[0])
    g_shard = _unpack_shards([jnp.where(cc == 0, jnp.stack([r, o]), jnp.stack([o, r])) for r, o in zip(reduced, _share_halves(tuple(reduced)))])

    g_small_local = {n: g_full[n] for n, _ in _SMALL if n != "loss"}
    g_small_local["loss"] = loss_rows[0, 0]
    _, small_sum = _gather8(_pack_small(g_small_local), "allreduce_small")
    g_small = _unpack_small(small_sum)
    loss = g_small.pop("loss")
    g_small["gdn_conv_w"] = lax.dynamic_slice_in_dim(g_small["gdn_conv_w"], chip * conv_cols, conv_cols, axis=2)

    grads = {**g_shard, **g_small}
    deltas, new_m, new_v = {}, {}, {}
    for n in _WEIGHTS:
        d2, m2, v2 = _adamw(_as2d(w[n]), _as2d(grads[n]), _as2d(m[n]), _as2d(v[n]), "adamw_" + n)
        deltas[n], new_m[n], new_v[n] = d2.reshape(w[n].shape), m2.reshape(w[n].shape), v2.reshape(w[n].shape)
    return (loss, grad_x[None], *[grads[n].reshape(w[n].shape) for n in _WEIGHTS], *[deltas[n] for n in _WEIGHTS],
            *[new_m[n] for n in _WEIGHTS], *[new_v[n] for n in _WEIGHTS])
```

```python
import functools

import jax
import jax.numpy as jnp
from jax import lax
from jax.experimental import pallas as pl
from jax.experimental.pallas import tpu as pltpu

F32, BF16 = jnp.float32, jnp.bfloat16
HI = lax.Precision.HIGHEST
MESH = pl.DeviceIdType.MESH

EPS = 1e-6
D_MODEL = 1024
HEADS = 8
HEAD_DIM = 128
CHUNK = 64
CHUNK_SHIFT = CHUNK.bit_length() - 1
CONV_K = 4
D_FF = 4096
QKV = 3 * HEADS * HEAD_DIM

ADAM_LR, ADAM_B1, ADAM_B2, ADAM_EPS, ADAM_WD, ADAM_STEP = 0.001, 0.9, 0.999, 1e-08, 0.01, 10

VMEM_LIMIT_BYTES = 48 * 1024 * 1024
LANES = 128

NN = ((1,), (0,))
NT = ((1,), (1,))
TN = ((0,), (0,))


def _dot(a, b, dims=NN, precision=None):
    return lax.dot_general(a, b, (dims, ((), ())), precision=precision, preferred_element_type=F32)


def _params(*sem):
    return pltpu.CompilerParams(dimension_semantics=sem, vmem_limit_bytes=VMEM_LIMIT_BYTES)


def _iota(shape, axis):
    return lax.broadcasted_iota(jnp.int32, shape, axis)


def _matmul(a, b, mode, out_dtype, name, tm=1024, tn=1024, tk=2048, add=None, epilogue=None, extras=()):
    if mode == "nn":
        (m, k), (k2, n) = a.shape, b.shape
    elif mode == "nt":
        (m, k), (n, k2) = a.shape, b.shape
    else:
        (k, m), (k2, n) = a.shape, b.shape
    assert k == k2, (a.shape, b.shape, mode)
    tm, tn, tk = min(tm, m), min(tn, n), min(tk, k)
    assert m % tm == 0 and n % tn == 0 and k % tk == 0, (a.shape, b.shape, mode)
    nk = k // tk
    dims = {"nn": NN, "nt": NT, "tn": TN}[mode]
    tiles = ([add] if add is not None else []) + list(extras)
    out_dtypes = out_dtype if epilogue is not None else (out_dtype,)
    n_in = 2 + len(tiles)

    def finish(acc, extra_refs, o_refs):
        res = (acc,) if epilogue is None else epilogue(acc, *[r[...] for r in extra_refs])
        for o_ref, r in zip(o_refs, res):
            o_ref[...] = r.astype(o_ref.dtype)

    def body(*refs):
        a_ref, b_ref = refs[:2]
        extra_refs = refs[n_in - len(extras):n_in]
        o_refs, acc_ref = refs[n_in:-1], refs[-1]
        prod = _dot(a_ref[...].astype(BF16), b_ref[...].astype(BF16), dims)
        if nk == 1:
            finish(prod + refs[2][...].astype(F32) if add is not None else prod, extra_refs, o_refs)
            return
        kk = pl.program_id(2)

        @pl.when(kk == 0)
        def _():
            acc_ref[...] = refs[2][...].astype(F32) if add is not None else jnp.zeros_like(acc_ref)

        acc_ref[...] += prod

        @pl.when(kk == nk - 1)
        def _():
            finish(acc_ref[...], extra_refs, o_refs)

    a_spec = pl.BlockSpec((tk, tm), lambda i, j, kk: (kk, i)) if mode == "tn" else pl.BlockSpec((tm, tk), lambda i, j, kk: (i, kk))
    b_spec = pl.BlockSpec((tn, tk), lambda i, j, kk: (j, kk)) if mode == "nt" else pl.BlockSpec((tk, tn), lambda i, j, kk: (kk, j))
    o_spec = pl.BlockSpec((tm, tn), lambda i, j, kk: (i, j))
    res = pl.pallas_call(
        body,
        name=name,
        grid=(m // tm, n // tn, nk),
        in_specs=[a_spec, b_spec] + [o_spec] * len(tiles),
        out_specs=[o_spec] * len(out_dtypes),
        out_shape=[jax.ShapeDtypeStruct((m, n), dt) for dt in out_dtypes],
        scratch_shapes=[pltpu.VMEM((tm, tn), F32)],
        compiler_params=_params("parallel", "parallel", "arbitrary"),
    )(a, b, *tiles)
    return res if epilogue is not None else res[0]


def _row_specs(rows, tm):
    return [pl.BlockSpec((tm, w), lambda i, cb=cb: (i, cb)) for _, w, cb in rows]


def _full_spec(p):
    return pl.BlockSpec(p.shape, lambda i: (0,) * p.ndim)


def _rowwise(name, fn, rows, params, outs, tm=256):
    t = rows[0][0].shape[0]
    tm = min(tm, t)
    nr, npar = len(rows), len(params)

    def body(*refs):
        ins = [r[...].astype(F32) for r in refs[:nr]]
        ps = [p[...] for p in refs[nr:nr + npar]]
        res = fn(*ins, *ps)
        for o_ref, r in zip(refs[nr + npar:], res):
            o_ref[...] = r.astype(o_ref.dtype)

    return pl.pallas_call(
        body,
        name=name,
        grid=(t // tm,),
        in_specs=_row_specs(rows, tm) + [_full_spec(p) for p in params],
        out_specs=[pl.BlockSpec((tm, w), lambda i: (i, 0)) for w, _ in outs],
        out_shape=[jax.ShapeDtypeStruct((t, w), dt) for w, dt in outs],
        compiler_params=_params("parallel"),
    )(*[r[0] for r in rows], *params)


def _add_rows(name, terms, n_rows, out_dtypes, tm):
    cols = terms[0][0].shape[1]
    firsts = jnp.stack([jnp.asarray(first, jnp.int32) for _, first in terms])

    def body(firsts_ref, *refs):
        acc = refs[0][...].astype(F32)
        for r in refs[1:len(terms)]:
            acc = acc + r[...].astype(F32)
        for o_ref in refs[len(terms):]:
            o_ref[...] = acc.astype(o_ref.dtype)

    return pl.pallas_call(
        body,
        name=name,
        grid_spec=pltpu.PrefetchScalarGridSpec(
            num_scalar_prefetch=1,
            grid=(n_rows // tm,),
            in_specs=[pl.BlockSpec((tm, cols), lambda i, firsts_ref, k=k: (firsts_ref[k] + i, 0)) for k in range(len(terms))],
            out_specs=[pl.BlockSpec((tm, cols), lambda i, firsts_ref: (i, 0)) for _ in out_dtypes],
        ),
        out_shape=[jax.ShapeDtypeStruct((n_rows, cols), dt) for dt in out_dtypes],
        compiler_params=_params("parallel"),
    )(firsts, *[a for a, _ in terms])


def _rowwise_bwd(name, fn, rows, params, cots, grad_dtypes, tm=256):
    t = rows[0][0].shape[0]
    tm = min(tm, t)
    nr, npar, nc = len(rows), len(params), len(cots)
    want = [j for j, dt in enumerate(grad_dtypes) if dt is not None]
    widths = [rows[j][1] for j in want]
    n_row_outs = len(want)

    def body(*refs):
        i = pl.program_id(0)
        ins = [r[...].astype(F32) for r in refs[:nr]]
        ps = [p[...] for p in refs[nr:nr + npar]]
        cs = tuple(c[...].astype(F32) for c in refs[nr + npar:nr + npar + nc])
        _, vjp = jax.vjp(fn, *ins, *ps)
        gs = vjp(cs)
        outs = refs[nr + npar + nc:]
        for o_ref, j in zip(outs, want):
            o_ref[...] = gs[j].astype(o_ref.dtype)
        pg_refs = outs[n_row_outs:]

        @pl.when(i == 0)
        def _():
            for pg in pg_refs:
                pg[...] = jnp.zeros_like(pg)

        for pg, g in zip(pg_refs, gs[nr:]):
            pg[...] += g

    row_specs = [pl.BlockSpec((tm, w), lambda i: (i, 0)) for w in widths]
    row_shapes = [jax.ShapeDtypeStruct((t, w), grad_dtypes[j]) for j, w in zip(want, widths)]
    res = pl.pallas_call(
        body,
        name=name,
        grid=(t // tm,),
        in_specs=_row_specs(rows, tm) + [_full_spec(p) for p in params] + [pl.BlockSpec((tm, c.shape[1]), lambda i: (i, 0)) for c in cots],
        out_specs=row_specs + [_full_spec(p) for p in params],
        out_shape=row_shapes + [jax.ShapeDtypeStruct(p.shape, F32) for p in params],
        compiler_params=_params("arbitrary"),
    )(*[r[0] for r in rows], *params, *cots)
    return res[:n_row_outs], res[n_row_outs:]


def _rms(x, g):
    return x * lax.rsqrt(jnp.mean(x * x, axis=-1, keepdims=True) + EPS) * g


def _sigmoid(x):
    return 1.0 / (1.0 + jnp.exp(-x))


def _softplus(x):
    return jnp.maximum(x, 0.0) + jnp.log1p(jnp.exp(-jnp.abs(x)))


def _two_pass(x, m):
    hi = x.astype(BF16)
    lo = (x - hi.astype(F32)).astype(BF16)
    return _dot(hi, m) + _dot(lo, m)


def _head_sum_impl(x):
    sums = [jnp.sum(x[:, h * HEAD_DIM:(h + 1) * HEAD_DIM], axis=-1, keepdims=True) for h in range(HEADS)]
    return jnp.concatenate([jnp.broadcast_to(s, (x.shape[0], HEAD_DIM)) for s in sums], axis=1)


@jax.custom_vjp
def _head_sum(x):
    return _head_sum_impl(x)


_head_sum.defvjp(lambda x: (_head_sum_impl(x), None), lambda _, g: (_head_sum_impl(g),))


def _fn_norm(x, g):
    return (_rms(x, g),)


def _fn_gates(ba, al, dt):
    col = _iota((1, LANES), 1)
    g = jnp.where((col >= HEADS) & (col < 2 * HEADS), -jnp.exp(al) * _softplus(ba + dt), 0.0)
    rows = ba.shape[0]
    r, c = _iota((rows, rows), 0), _iota((rows, rows), 1)
    same = (r >> CHUNK_SHIFT) == (c >> CHUNK_SHIFT)
    gc = _dot(jnp.where(same & (r >= c), 1.0, 0.0), g, precision=HI)
    gtot = _dot(jnp.where(same, 1.0, 0.0), g, precision=HI)
    return _sigmoid(ba), gc, gtot


def _fn_post_q(c):
    s = c * _sigmoid(c)
    return (s * lax.rsqrt(_head_sum(s * s) + EPS) * (HEAD_DIM ** -0.5),)


def _fn_post_k(c):
    s = c * _sigmoid(c)
    return (s * lax.rsqrt(_head_sum(s * s) + EPS),)


def _fn_post_v(c):
    return (c * _sigmoid(c),)


def _fn_post(cq, ck, cv):
    return _fn_post_q(cq) + _fn_post_k(ck) + _fn_post_v(cv)


def _fn_outnorm(o, gate, og):
    y = o * lax.rsqrt(_head_sum(o * o) * (1.0 / HEAD_DIM) + EPS) * og
    return (y * (gate * _sigmoid(gate)),)


def _fn_res_norm(x, m, gp, gn):
    x1 = x + _rms(m, gp)
    return x1, _rms(x1, gn)


def _fn_res_norm2(x, m, gp, ga, gb):
    x1 = x + _rms(m, gp)
    return x1, _rms(x1, ga), _rms(x1, gb)


def _relu2_of(u):
    r = jnp.maximum(u, 0.0)
    return (r * r,)


def _relu2_cotangent(da, a):
    return (da * (2.0 * jnp.sqrt(a.astype(F32))),)


def _loss_call(x3, d1, tgt, g, tm=256):
    t, d = x3.shape
    tm = min(tm, t)

    def body(x_ref, d_ref, t_ref, g_ref, loss_ref, dx_ref, dd_ref, dg_ref):
        i = pl.program_id(0)
        y, vjp = jax.vjp(lambda x, dd, gg: x + _rms(dd, gg), x_ref[...], d_ref[...], g_ref[...])
        err = y - t_ref[...]
        lrow = 0.5 * jnp.mean(err * err, axis=-1, keepdims=True)
        dx, dd, dg = vjp(err * (1.0 / d))
        dx_ref[...] = dx
        dd_ref[...] = dd.astype(dd_ref.dtype)

        @pl.when(i == 0)
        def _():
            loss_ref[...] = jnp.zeros_like(loss_ref)
            dg_ref[...] = jnp.zeros_like(dg_ref)

        loss_ref[...] += jnp.broadcast_to(jnp.sum(lrow, axis=0, keepdims=True), loss_ref.shape)
        dg_ref[...] += dg

    row = pl.BlockSpec((tm, d), lambda i: (i, 0))
    return pl.pallas_call(
        body,
        name="loss_head",
        grid=(t // tm,),
        in_specs=[row, row, row, _full_spec(g)],
        out_specs=[pl.BlockSpec((8, LANES), lambda i: (0, 0)), row, row, _full_spec(g)],
        out_shape=[jax.ShapeDtypeStruct((8, LANES), F32), jax.ShapeDtypeStruct((t, d), F32), jax.ShapeDtypeStruct((t, d), BF16), jax.ShapeDtypeStruct(g.shape, F32)],
        compiler_params=_params("arbitrary"),
    )(x3, d1, tgt, g)


HALO = 8


def _conv_fwd(qkvg, conv_w, shards, tm=256):
    t = qkvg.shape[0]
    tm = min(tm, t)
    steps = t // tm
    wide = QKV // 3
    n = len(shards)

    def body(*refs):
        cur_ref, prev_ref, w_ref = refs[:3]
        shard_refs = refs[3:3 + n]
        o_ref, q_ref, k_ref, v_ref = refs[3 + n:7 + n]
        all_refs = refs[7 + n:7 + 2 * n]
        buf, sems = refs[7 + 2 * n], refs[8 + 2 * n:]
        i = pl.program_id(0)

        if n:
            @pl.when(i == 0)
            def _():
                for cp in _gather_sends(shard_refs, all_refs, *sems[:2]):
                    cp.start()

        buf[0:HALO, :] = jnp.where(i > 0, prev_ref[...], 0.0)
        buf[HALO:, :] = cur_ref[...]
        acc = buf[pl.ds(HALO - CONV_K + 1, tm), :] * w_ref[pl.ds(0, 1), :]
        for j in range(1, CONV_K):
            acc = acc + buf[pl.ds(HALO - CONV_K + 1 + j, tm), :] * w_ref[pl.ds(j, 1), :]
        o_ref[...] = acc
        (q_ref[...], k_ref[...], v_ref[...]) = _fn_post(acc[:, 0:wide], acc[:, wide:2 * wide], acc[:, 2 * wide:])

        if n:
            @pl.when(i == steps - 1)
            def _():
                _gather_finish(shard_refs, all_refs, *sems)

    part = pl.BlockSpec((tm, wide), lambda i: (i, 0))
    res = pl.pallas_call(
        body,
        name="conv_fwd",
        grid=(steps,),
        in_specs=[
            pl.BlockSpec((tm, QKV), lambda i: (i, 0)),
            pl.BlockSpec((HALO, QKV), lambda i: (jnp.maximum(i * (tm // HALO) - 1, 0), 0)),
            pl.BlockSpec((CONV_K, QKV), lambda i: (0, 0)),
        ] + [_HBM] * n,
        out_specs=[pl.BlockSpec((tm, QKV), lambda i: (i, 0)), part, part, part] + [_HBM] * n,
        out_shape=[jax.ShapeDtypeStruct((t, QKV), F32)] + [jax.ShapeDtypeStruct((t, wide), F32)] * 3
        + [jax.ShapeDtypeStruct((N_CHIPS,) + s.shape, s.dtype) for s in shards],
        scratch_shapes=[pltpu.VMEM((tm + HALO, QKV), F32)] + [pltpu.SemaphoreType.DMA((3 * n,))] * (4 if n else 0),
        compiler_params=_params("arbitrary"),
    )(qkvg, qkvg, conv_w, *shards)
    return res[:4], res[4:]


def _conv_bwd(conv, dqkv, dgate, qkvg, conv_w, tm=256):
    t = conv.shape[0]
    tm = min(tm, t)
    n = t // tm
    wg = dgate.shape[1]
    wide = QKV // 3

    def conv_cotangent(c_ref, g_refs):
        parts = [c_ref[:, j * wide:(j + 1) * wide] for j in range(3)]
        _, vjp = jax.vjp(_fn_post, *parts)
        return vjp(tuple(g[...] for g in g_refs))

    def body(c_ref, cn_ref, dq_ref, dk_ref, dv_ref, dqn_ref, dkn_ref, dvn_ref, dgate_ref, x_ref, xp_ref, w_ref, dx_ref, dw_ref, bufd, bufx):
        i = pl.program_id(0)
        for j, (cur, nxt) in enumerate(zip(conv_cotangent(c_ref, (dq_ref, dk_ref, dv_ref)), conv_cotangent(cn_ref, (dqn_ref, dkn_ref, dvn_ref)))):
            bufd[0:tm, j * wide:(j + 1) * wide] = cur
            bufd[tm:, j * wide:(j + 1) * wide] = jnp.where(i < n - 1, nxt, 0.0)
        bufx[0:HALO, :] = jnp.where(i > 0, xp_ref[...], 0.0)
        bufx[HALO:, :] = x_ref[...]

        @pl.when(i == 0)
        def _():
            dw_ref[...] = jnp.zeros_like(dw_ref)

        dcv = bufd[0:tm, :]
        acc = bufd[pl.ds(CONV_K - 1, tm), :] * w_ref[pl.ds(0, 1), :]
        for j in range(1, CONV_K):
            acc = acc + bufd[pl.ds(CONV_K - 1 - j, tm), :] * w_ref[pl.ds(j, 1), :]
        dx_ref[:, 0:QKV] = acc.astype(dx_ref.dtype)
        dx_ref[:, QKV:] = dgate_ref[...].astype(dx_ref.dtype)
        for j in range(CONV_K):
            dw_ref[pl.ds(j, 1), :] += jnp.sum(dcv * bufx[pl.ds(HALO - CONV_K + 1 + j, tm), :], axis=0, keepdims=True)

    def cur(width):
        return pl.BlockSpec((tm, width), lambda i: (i, 0))

    def nxt(width):
        return pl.BlockSpec((HALO, width), lambda i: (jnp.minimum((i + 1) * (tm // HALO), t // HALO - 1), 0))

    return pl.pallas_call(
        body,
        name="conv_bwd",
        grid=(n,),
        in_specs=[cur(QKV), nxt(QKV)] + [cur(wide)] * 3 + [nxt(wide)] * 3 + [
            cur(wg),
            cur(QKV),
            pl.BlockSpec((HALO, QKV), lambda i: (jnp.maximum(i * (tm // HALO) - 1, 0), 0)),
            pl.BlockSpec((CONV_K, QKV), lambda i: (0, 0)),
        ],
        out_specs=[pl.BlockSpec((tm, QKV + wg), lambda i: (i, 0)), pl.BlockSpec((HALO, QKV), lambda i: (0, 0))],
        out_shape=[jax.ShapeDtypeStruct((t, QKV + wg), BF16), jax.ShapeDtypeStruct((HALO, QKV), F32)],
        scratch_shapes=[pltpu.VMEM((tm + HALO, QKV), F32), pltpu.VMEM((tm + HALO, QKV), F32)],
        compiler_params=_params("arbitrary"),
    )(conv, conv, *dqkv, *dqkv, dgate, qkvg, qkvg, conv_w)


PREP_CHUNKS = 16
PREP_BWD_CHUNKS = 4
SCAN_CHUNKS = 4


def _hi_lo(x):
    hi = x.astype(BF16)
    return hi, (x - hi.astype(F32)).astype(BF16)


def _mm3(a, b, dims=NN):
    (ah, al), (bh, bl) = _hi_lo(a), _hi_lo(b)
    return _dot(ah, bh, dims) + (_dot(ah, bl, dims) + _dot(al, bh, dims))


def _neumann(lowers):
    c = lowers[0].shape[0]
    eye = jnp.where(_iota((c, c), 0) == _iota((c, c), 1), 1.0, 0.0)
    ps = [-low for low in lowers]
    tmats = [eye + p for p in ps]
    for _ in range(CHUNK_SHIFT - 1):
        ps = [_mm3(p, p) for p in ps]
        tmats = [t + _mm3(t, p) for t, p in zip(tmats, ps)]
    return tuple(tmats)


def _inv_cotangents(tmats, dts):
    half = [_mm3(t, dt, TN) for t, dt in zip(tmats, dts)]
    return tuple(-_mm3(hf, t, NT) for hf, t in zip(half, tmats))


@jax.custom_vjp
def _tri_inv(lowers):
    return _neumann(lowers)


def _tri_inv_fwd(lowers):
    tmats = _neumann(lowers)
    return tmats, tmats


_tri_inv.defvjp(_tri_inv_fwd, lambda tmats, dts: (_inv_cotangents(tmats, dts),))


@jax.custom_vjp
def _tri_inv_known(lowers, tmats):
    return tmats


_tri_inv_known.defvjp(lambda lowers, tmats: (tmats, tmats),
                      lambda tmats, dts: (_inv_cotangents(tmats, dts), tuple(jnp.zeros_like(t) for t in tmats)))


def _prep_chunks(qs, ks, vs, bs, gcs, gts, gcrs, tmats=None):
    c = CHUNK
    r, col = _iota((c, c), 0), _iota((c, c), 1)
    incl, strict = r >= col, r > col
    decays = [jnp.where(incl, jnp.exp(jnp.where(incl, gc - gcr, 0.0)), 0.0) for gc, gcr in zip(gcs, gcrs)]
    kbs = [k * b for k, b in zip(ks, bs)]
    kbfs = [k.astype(BF16) for k in ks]
    lowers = tuple(jnp.where(strict, _dot(kb.astype(BF16), kbf, NT) * decay, 0.0) for kb, kbf, decay in zip(kbs, kbfs, decays))
    tmats = _tri_inv(lowers) if tmats is None else _tri_inv_known(lowers, tuple(tmats))
    outs = []
    for q, k, v, b, gc, gt, kb, kbf, decay, tmat in zip(qs, ks, vs, bs, gcs, gts, kbs, kbfs, decays, tmats):
        tb = tmat.astype(BF16)
        egc = jnp.exp(gc)
        w = _dot(tb, (kb * egc).astype(BF16))
        u = _dot(tb, (v * b).astype(BF16))
        attn = _dot(q.astype(BF16), kbf, NT) * decay
        gl = jnp.broadcast_to(jnp.exp(jnp.mean(gt.reshape(c // 8, 8, 1), axis=0)), (8, HEAD_DIM))
        outs.append((w, u, q * egc, k * jnp.exp(gt - gc), attn, gl))
    return tuple(outs), tmats


def _prep_specs(rows, gch):
    head = pl.BlockSpec((rows, HEAD_DIM), lambda n, h: (n, h))
    gates = pl.BlockSpec((rows, LANES), lambda n, h: (n, 0))
    gcrow = pl.BlockSpec((1, gch, 1, CHUNK), lambda n, h: (h, n, 0, 0))
    square = pl.BlockSpec((1, rows, CHUNK), lambda n, h: (h, n, 0))
    gl = pl.BlockSpec((1, gch * 8, HEAD_DIM), lambda n, h: (h, n, 0))
    return head, gates, gcrow, square, gl


def _pick_lane(ref, sl, lane):
    return jnp.sum(jnp.where(_iota((1, LANES), 1) == lane, ref[sl, :], 0.0), axis=1, keepdims=True)


def _prep_inputs(q_ref, k_ref, v_ref, b_ref, gc_ref, gt_ref, gcr_ref, sls, h):
    return ([q_ref[sl, :] for sl in sls], [k_ref[sl, :] for sl in sls], [v_ref[sl, :] for sl in sls],
            [_pick_lane(b_ref, sl, h) for sl in sls], [_pick_lane(gc_ref, sl, h + HEADS) for sl in sls],
            [_pick_lane(gt_ref, sl, h + HEADS) for sl in sls], [gcr_ref[0, c] for c in range(len(sls))])


def _gdn_prep(q, k, v, beta, gc, gt, gcr, shards=()):
    t = q.shape[0]
    gch = min(PREP_CHUNKS, t // CHUNK)
    rows = gch * CHUNK
    steps = t // rows
    n = len(shards)

    def body(*refs):
        q_ref, k_ref, v_ref, b_ref, gc_ref, gt_ref, gcr_ref = refs[:7]
        shard_refs = refs[7:7 + n]
        w_ref, u_ref, qg_ref, kg_ref, at_ref, gl_ref, tm_ref = refs[7 + n:14 + n]
        all_refs, sems = refs[14 + n:14 + 2 * n], refs[14 + 2 * n:]
        h = pl.program_id(1)

        if n:
            @pl.when(jnp.logical_and(pl.program_id(0) == 0, h == 0))
            def _():
                for cp in _gather_sends(shard_refs, all_refs, *sems[:2]):
                    cp.start()

        sls = [pl.ds(c * CHUNK, CHUNK) for c in range(gch)]
        outs, tmats = _prep_chunks(*_prep_inputs(q_ref, k_ref, v_ref, b_ref, gc_ref, gt_ref, gcr_ref, sls, h))
        for c, (sl, (w, u, qg, kg, attn, gl), tmat) in enumerate(zip(sls, outs, tmats)):
            w_ref[sl, :] = w.astype(BF16)
            u_ref[sl, :] = u
            qg_ref[sl, :] = qg.astype(BF16)
            kg_ref[sl, :] = kg.astype(BF16)
            at_ref[0, sl, :] = attn.astype(BF16)
            gl_ref[0, pl.ds(c * 8, 8), :] = gl
            tm_ref[0, sl, :] = tmat

        if n:
            @pl.when(jnp.logical_and(pl.program_id(0) == steps - 1, h == HEADS - 1))
            def _():
                _gather_finish(shard_refs, all_refs, *sems)

    hb, col, gcrow, square, glb = _prep_specs(rows, gch)
    wide = HEADS * HEAD_DIM
    res = pl.pallas_call(
        body,
        name="gdn_prep",
        grid=(steps, HEADS),
        in_specs=[hb, hb, hb, col, col, col, gcrow] + [_HBM] * n,
        out_specs=[hb, hb, hb, hb, square, glb, square] + [_HBM] * n,
        out_shape=[
            jax.ShapeDtypeStruct((t, wide), BF16),
            jax.ShapeDtypeStruct((t, wide), F32),
            jax.ShapeDtypeStruct((t, wide), BF16),
            jax.ShapeDtypeStruct((t, wide), BF16),
            jax.ShapeDtypeStruct((HEADS, t, CHUNK), BF16),
            jax.ShapeDtypeStruct((HEADS, t // CHUNK * 8, HEAD_DIM), F32),
            jax.ShapeDtypeStruct((HEADS, t, CHUNK), F32),
        ] + [jax.ShapeDtypeStruct((N_CHIPS,) + s.shape, s.dtype) for s in shards],
        scratch_shapes=[pltpu.SemaphoreType.DMA((3 * n,))] * (4 if n else 0),
        compiler_params=_params("arbitrary", "arbitrary") if n else _params("parallel", "parallel"),
    )(q, k, v, beta, gc, gt, gcr, *shards)
    return res[:7], res[7:]


def _gdn_prep_bwd(q, k, v, beta, gc, gt, gcr, tmat, dw, du, dqg, dkg, dattn, dgl, partials=()):
    t = q.shape[0]
    gch = min(PREP_BWD_CHUNKS, t // CHUNK)
    rows = gch * CHUNK
    steps = t // rows
    n_sc = len(partials)

    def body(*refs):
        (q_ref, k_ref, v_ref, b_ref, gc_ref, gt_ref, gcr_ref, tm_ref, dw_ref, du_ref, dqg_ref, dkg_ref, dat_ref, dgl_ref) = refs[:14]
        p_refs = refs[14:14 + n_sc]
        dq_ref, dk_ref, dv_ref, db_ref, dgc_ref, dgt_ref, dgcr_ref = refs[14 + n_sc:21 + n_sc]
        from_refs, sems = refs[21 + n_sc:21 + 2 * n_sc], refs[21 + 2 * n_sc:]
        h = pl.program_id(1)
        lane = _iota((1, LANES), 1)

        if n_sc:
            @pl.when(jnp.logical_and(pl.program_id(0) == 0, h == 0))
            def _():
                for cp in _scatter_copies(p_refs, from_refs, *sems):
                    cp.start()

        @pl.when(h == 0)
        def _():
            db_ref[...] = jnp.zeros_like(db_ref)
            dgc_ref[...] = jnp.zeros_like(dgc_ref)
            dgt_ref[...] = jnp.zeros_like(dgt_ref)

        sls = [pl.ds(c * CHUNK, CHUNK) for c in range(gch)]
        known = [tm_ref[0, sl, :] for sl in sls]
        _, vjp = jax.vjp(lambda *a: _prep_chunks(*a, tmats=known)[0], *_prep_inputs(q_ref, k_ref, v_ref, b_ref, gc_ref, gt_ref, gcr_ref, sls, h))
        cots = tuple((dw_ref[sl, :], du_ref[sl, :], dqg_ref[sl, :], dkg_ref[sl, :], dat_ref[0, sl, :], dgl_ref[0, pl.ds(c * 8, 8), :]) for c, sl in enumerate(sls))
        dqs, dks, dvs, dbs, dgcs, dgts, dgcrs = vjp(cots)
        for c, sl in enumerate(sls):
            dq_ref[sl, :] = dqs[c]
            dk_ref[sl, :] = dks[c]
            dv_ref[sl, :] = dvs[c]
            db_ref[sl, :] += jnp.where(lane == h, dbs[c], 0.0)
            dgc_ref[sl, :] += jnp.where(lane == h + HEADS, dgcs[c], 0.0)
            dgt_ref[sl, :] += jnp.where(lane == h + HEADS, dgts[c], 0.0)
            dgcr_ref[0, c] = dgcrs[c]

        if n_sc:
            @pl.when(jnp.logical_and(pl.program_id(0) == steps - 1, h == HEADS - 1))
            def _():
                for cp in _scatter_copies(p_refs, from_refs, *sems):
                    cp.wait()

    hb, col, gcrow, square, glb = _prep_specs(rows, gch)
    wide = HEADS * HEAD_DIM
    res = pl.pallas_call(
        body,
        name="gdn_prep_bwd",
        grid=(steps, HEADS),
        in_specs=[hb, hb, hb, col, col, col, gcrow, square, hb, hb, hb, hb, square, glb] + [_HBM] * n_sc,
        out_specs=[hb, hb, hb, col, col, col, gcrow] + [_HBM] * n_sc,
        out_shape=[jax.ShapeDtypeStruct((t, wide), F32)] * 3 + [jax.ShapeDtypeStruct((t, LANES), F32)] * 3 + [jax.ShapeDtypeStruct((HEADS, t // CHUNK, 1, CHUNK), F32)]
        + [jax.ShapeDtypeStruct((3,) + p.shape[1:], p.dtype) for p in partials],
        scratch_shapes=[pltpu.SemaphoreType.DMA((3 * n_sc,))] * (2 if n_sc else 0),
        compiler_params=_params("arbitrary", "arbitrary"),
    )(q, k, v, beta, gc, gt, gcr, tmat, dw, du, dqg, dkg, dattn, dgl, *partials)
    return res[:7], res[7:]


def _gdn_scan(w, u, qg, kg, attn, gl):
    t = w.shape[0]
    n = t // CHUNK
    nch = min(SCAN_CHUNKS, n)
    wide = HEADS * HEAD_DIM

    def body(w_ref, u_ref, qg_ref, kg_ref, at_ref, gl_ref, o_ref, st_ref, s_ref):
        @pl.when(pl.program_id(0) == 0)
        def _():
            s_ref[...] = jnp.zeros_like(s_ref)

        heads = range(HEADS)
        cols = [pl.ds(h * HEAD_DIM, HEAD_DIM) for h in heads]
        for c in range(nch):
            rows, gl_rows = pl.ds(c * CHUNK, CHUNK), pl.ds(c * 8, 8)
            ss = [s_ref[h] for h in heads]
            sbs = [s.astype(BF16) for s in ss]
            vbs = [(u_ref[rows, hs] - _dot(w_ref[rows, hs], sb)).astype(BF16) for hs, sb in zip(cols, sbs)]
            outs = [_dot(qg_ref[rows, hs], sb) + _dot(at_ref[h, rows, :], vb) for h, hs, sb, vb in zip(heads, cols, sbs, vbs)]
            new = [s * jnp.tile(gl_ref[h, gl_rows, :], (HEAD_DIM // 8, 1)) + _dot(kg_ref[rows, hs], vb, TN) for h, hs, s, vb in zip(heads, cols, ss, vbs)]
            for h, hs in zip(heads, cols):
                st_ref[c, h] = ss[h]
                o_ref[rows, hs] = outs[h]
                s_ref[h] = new[h]

    row = pl.BlockSpec((nch * CHUNK, wide), lambda i: (i, 0))
    return pl.pallas_call(
        body,
        name="gdn_scan",
        grid=(n // nch,),
        in_specs=[row, row, row, row, pl.BlockSpec((HEADS, nch * CHUNK, CHUNK), lambda i: (0, i, 0)), pl.BlockSpec((HEADS, nch * 8, HEAD_DIM), lambda i: (0, i, 0))],
        out_specs=[row, pl.BlockSpec((nch, HEADS, HEAD_DIM, HEAD_DIM), lambda i: (i, 0, 0, 0))],
        out_shape=[jax.ShapeDtypeStruct((t, wide), F32), jax.ShapeDtypeStruct((n, HEADS, HEAD_DIM, HEAD_DIM), F32)],
        scratch_shapes=[pltpu.VMEM((HEADS, HEAD_DIM, HEAD_DIM), F32)],
        compiler_params=_params("arbitrary"),
    )(w, u, qg, kg, attn, gl)


def _gdn_scan_bwd(w, u, qg, kg, attn, gl, states, do):
    t = w.shape[0]
    n = t // CHUNK
    nch = min(SCAN_CHUNKS, n)
    steps = n // nch
    wide = HEADS * HEAD_DIM

    def body(w_ref, u_ref, qg_ref, kg_ref, at_ref, gl_ref, st_ref, do_ref, dw_ref, du_ref, dqg_ref, dkg_ref, dat_ref, dgl_ref, ds_ref):
        @pl.when(pl.program_id(0) == 0)
        def _():
            ds_ref[...] = jnp.zeros_like(ds_ref)

        heads = range(HEADS)
        cols = [pl.ds(h * HEAD_DIM, HEAD_DIM) for h in heads]
        for c in reversed(range(nch)):
            rows, gl_rows = pl.ds(c * CHUNK, CHUNK), pl.ds(c * 8, 8)
            ss = [st_ref[c, h] for h in heads]
            sbs = [s.astype(BF16) for s in ss]
            dsns = [ds_ref[h] for h in heads]
            dsbs = [d.astype(BF16) for d in dsns]
            dobs = [do_ref[rows, hs].astype(BF16) for hs in cols]
            vbs = [(u_ref[rows, hs] - _dot(w_ref[rows, hs], sb)).astype(BF16) for hs, sb in zip(cols, sbs)]
            dvns = [_dot(at_ref[h, rows, :], dob, TN) + _dot(kg_ref[rows, hs], dsb) for h, hs, dob, dsb in zip(heads, cols, dobs, dsbs)]
            dvbs = [d.astype(BF16) for d in dvns]
            for h, hs in zip(heads, cols):
                dat_ref[h, rows, :] = _dot(dobs[h], vbs[h], NT)
                dqg_ref[rows, hs] = _dot(dobs[h], sbs[h], NT)
                dkg_ref[rows, hs] = _dot(vbs[h], dsbs[h], NT)
                du_ref[rows, hs] = dvns[h]
                dw_ref[rows, hs] = -_dot(dvbs[h], sbs[h], NT)
                dgl_ref[h, gl_rows, :] = jnp.sum((dsns[h] * ss[h]).reshape(HEAD_DIM // 8, 8, HEAD_DIM), axis=0)
            new = [dsn * jnp.tile(gl_ref[h, gl_rows, :], (HEAD_DIM // 8, 1)) + _dot(qg_ref[rows, hs], dob, TN) - _dot(w_ref[rows, hs], dvb, TN)
                   for h, hs, dsn, dob, dvb in zip(heads, cols, dsns, dobs, dvbs)]
            for h in heads:
                ds_ref[h] = new[h]

    row = pl.BlockSpec((nch * CHUNK, wide), lambda i: (steps - 1 - i, 0))
    at = pl.BlockSpec((HEADS, nch * CHUNK, CHUNK), lambda i: (0, steps - 1 - i, 0))
    glb = pl.BlockSpec((HEADS, nch * 8, HEAD_DIM), lambda i: (0, steps - 1 - i, 0))
    return pl.pallas_call(
        body,
        name="gdn_scan_bwd",
        grid=(steps,),
        in_specs=[row, row, row, row, at, glb, pl.BlockSpec((nch, HEADS, HEAD_DIM, HEAD_DIM), lambda i: (steps - 1 - i, 0, 0, 0)), row],
        out_specs=[row, row, row, row, at, glb],
        out_shape=[jax.ShapeDtypeStruct((t, wide), F32)] * 4 + [jax.ShapeDtypeStruct((HEADS, t, CHUNK), F32), jax.ShapeDtypeStruct((HEADS, n * 8, HEAD_DIM), F32)],
        scratch_shapes=[pltpu.VMEM((HEADS, HEAD_DIM, HEAD_DIM), F32)],
        compiler_params=_params("arbitrary"),
    )(w, u, qg, kg, attn, gl, states, do)


SB_Q = 512
SB_K = 256
SB_STEP = 1
SB_DEAD = -105.0


def _sb_scores(q, k):
    z = _dot(q, k, NT) * (HEAD_DIM ** -0.5)
    e = jnp.exp(-jnp.abs(z))
    lb = jnp.minimum(z, 0.0) - jnp.log(1.0 + e)
    return z, e, lb, lb - z


def _tri(n, rel):
    return jnp.where(rel(_iota((n, n), 0), _iota((n, n), 1)), 1.0, 0.0).astype(BF16)


def _lanes(col):
    return jnp.broadcast_to(col, (col.shape[0], LANES))


def _sb_fwd(q, k, v):
    t = q.shape[0]
    bq, bk = min(SB_Q, t), min(SB_K, t)
    nsub, rep = bq // bk, bk // LANES
    nstep = min(SB_STEP, nsub)
    steps_per_tile = nsub // nstep

    def body(q_ref, k_ref, v_ref, o_ref, rt_ref, first_ref):
        h = pl.program_id(0)
        i = pl.program_id(1)
        o_ref[...] = jnp.zeros_like(o_ref)
        rt_ref[...] = jnp.zeros_like(rt_ref)
        after = _tri(bk, lambda r, c: r > c)

        def block(j, r0, diag):
            st = pl.multiple_of(j * bk, bk)
            kv, vv = k_ref[pl.ds(st, bk), :], v_ref[pl.ds(st, bk), :]
            _, _, lb, l1m = _sb_scores(q_ref[r0:, :], kv)
            if diag:
                mask = _iota((bq - r0, bk), 1) + j * bk < _iota((bq - r0, bk), 0) + (r0 + i * bq)
                l1m = jnp.where(mask, l1m, 0.0)
            sums = _two_pass(l1m, after)
            run = rt_ref[r0:, :]
            a = jnp.exp(lb + jnp.tile(run, (1, rep)) + sums)
            if diag:
                a = jnp.where(mask, a, 0.0)
            o_ref[r0:, :] += _dot(a.astype(BF16), vv)
            rt_ref[r0:, :] = run + _lanes(sums[:, 0:1] + l1m[:, 0:1])

        for s in reversed(range(nsub)):
            block(i * nsub + s, s * bk, True)

        def alive(carry):
            u, highest = carry
            return jnp.logical_and(u >= 0, highest > SB_DEAD)

        def step(carry):
            u, _ = carry
            for s in reversed(range(nstep)):
                block(u * nstep + s, 0, False)
            return u - 1, jnp.max(rt_ref[...])

        u_end, _ = lax.while_loop(alive, step, (i * steps_per_tile - 1, jnp.max(rt_ref[...])))
        first_ref[h, i] = u_end + 1

    qb = pl.BlockSpec((bq, HEAD_DIM), lambda h, i: (i, h))
    full = pl.BlockSpec((t, HEAD_DIM), lambda h, i: (0, h))
    return pl.pallas_call(
        body,
        name="sb_fwd",
        grid=(HEADS, t // bq),
        in_specs=[qb, full, full],
        out_specs=[qb, qb, pl.BlockSpec(memory_space=pltpu.SMEM)],
        out_shape=[jax.ShapeDtypeStruct(q.shape, F32), jax.ShapeDtypeStruct(q.shape, F32), jax.ShapeDtypeStruct((HEADS, t // bq), jnp.int32)],
        compiler_params=_params("arbitrary", "arbitrary"),
    )(q, k, v)


def _sb_bwd(q, k, v, rt, first, do):
    t = q.shape[0]
    bq, bk = min(SB_Q, t), min(SB_K, t)
    nsub, rep = bq // bk, bk // LANES
    nstep = min(SB_STEP, nsub)
    steps_per_tile = nsub // nstep
    scale = HEAD_DIM ** -0.5

    def body(first_ref, q_ref, k_ref, v_ref, rt_ref, do_ref, dq_ref, dk_ref, dv_ref, left_ref, pg_ref):
        h = pl.program_id(0)
        i = pl.program_id(1)

        @pl.when(i == 0)
        def _():
            dk_ref[...] = jnp.zeros_like(dk_ref)
            dv_ref[...] = jnp.zeros_like(dv_ref)

        dq_ref[...] = jnp.zeros_like(dq_ref)
        left_ref[...] = jnp.zeros_like(left_ref)
        pg_ref[...] = jnp.zeros_like(pg_ref)
        upto = _tri(bk, lambda r, c: r <= c)

        def block(j, r0, diag):
            st = pl.multiple_of(j * bk, bk)
            kv, vv = k_ref[pl.ds(st, bk), :], v_ref[pl.ds(st, bk), :]
            qv = q_ref[r0:, :]
            dob = do_ref[r0:, :].astype(BF16)
            _, _, lb, l1m = _sb_scores(qv, kv)
            if diag:
                mask = _iota((bq - r0, bk), 1) + j * bk < _iota((bq - r0, bk), 0) + (r0 + i * bq)
                l1m = jnp.where(mask, l1m, 0.0)
            sums = _two_pass(l1m, upto)
            left = left_ref[r0:, :]
            a = jnp.exp(lb + jnp.tile(rt_ref[r0:, :] - left, (1, rep)) - sums)
            if diag:
                a = jnp.where(mask, a, 0.0)
            g = _dot(dob, vv, NT) * a
            dv_ref[pl.ds(st, bk), :] += _dot(a.astype(BF16), dob, TN)
            gsum = _two_pass(g, upto)
            pg = pg_ref[r0:, :]
            dz = g - jnp.exp(lb) * (jnp.tile(pg, (1, rep)) + gsum)
            if diag:
                dz = jnp.where(mask, dz, 0.0)
            dzb = (dz * scale).astype(BF16)
            dk_ref[pl.ds(st, bk), :] += _dot(dzb, qv, TN)
            dq_ref[r0:, :] += _dot(dzb, kv)
            left_ref[r0:, :] = left + _lanes(sums[:, bk - 1:bk])
            pg_ref[r0:, :] = pg + _lanes(gsum[:, bk - 1:bk])

        def step(u, carry):
            for s in range(nstep):
                block(u * nstep + s, 0, False)
            return carry

        lax.fori_loop(first_ref[h, i], i * steps_per_tile, step, 0)
        for s in range(nsub):
            block(i * nsub + s, s * bk, True)

    qb = pl.BlockSpec((bq, HEAD_DIM), lambda h, i: (i, h))
    full = pl.BlockSpec((t, HEAD_DIM), lambda h, i: (0, h))
    return pl.pallas_call(
        body,
        name="sb_bwd",
        grid=(HEADS, t // bq),
        in_specs=[pl.BlockSpec(memory_space=pltpu.SMEM), qb, full, full, qb, qb],
        out_specs=[qb, full, full],
        out_shape=[jax.ShapeDtypeStruct(q.shape, F32)] * 3,
        scratch_shapes=[pltpu.VMEM((bq, LANES), F32), pltpu.VMEM((bq, LANES), F32)],
        compiler_params=_params("arbitrary", "arbitrary"),
    )(first, q, k, v, rt, do)


def _adamw(w, g, m, v, name, tm=256):
    r, c = w.shape
    tm = tm if r % tm == 0 else r

    def body(w_ref, g_ref, m_ref, v_ref, d_ref, nm_ref, nv_ref):
        gv = g_ref[...]
        nm = ADAM_B1 * m_ref[...] + (1.0 - ADAM_B1) * gv
        nv = ADAM_B2 * v_ref[...] + (1.0 - ADAM_B2) * (gv * gv)
        m_hat = nm / (1.0 - ADAM_B1 ** ADAM_STEP)
        v_hat = nv / (1.0 - ADAM_B2 ** ADAM_STEP)
        d_ref[...] = -ADAM_LR * (m_hat / (jnp.sqrt(v_hat) + ADAM_EPS) + ADAM_WD * w_ref[...])
        nm_ref[...] = nm
        nv_ref[...] = nv

    blk = pl.BlockSpec((tm, c), lambda i: (i, 0))
    return pl.pallas_call(
        body,
        name=name,
        grid=(r // tm,),
        in_specs=[blk] * 4,
        out_specs=[blk] * 3,
        out_shape=[jax.ShapeDtypeStruct((r, c), F32)] * 3,
        compiler_params=_params("parallel"),
    )(w, g, m, v)


def _local_step(x, tgt, gains, w_in, small, shards, assemble, early_reduce=None):
    mix_pre, mix_post, mlp_pre, mlp_post, kv_gain = gains
    w_qkvg, w_ba = w_in
    conv_w, a_log, dt_bias, out_gain = small
    t, d = x.shape
    row = lambda a, i=None: a[i:i + 1] if i is not None else a
    al = jnp.zeros((1, LANES), F32).at[:, HEADS:2 * HEADS].set(a_log)
    dtb = jnp.zeros((1, LANES), F32).at[:, HEADS:2 * HEADS].set(dt_bias)
    og = jnp.tile(out_gain, (1, HEADS))
    full = lambda a: (a, a.shape[1], 0)

    (h0,) = _rowwise("norm_in", _fn_norm, [full(x)], [row(mix_pre, 0)], [(d, BF16)])
    qkvg = _matmul(h0, w_qkvg, "nn", F32, "mm_gdn_in", tk=1024)
    ba = _matmul(h0, w_ba, "nn", F32, "mm_gdn_ba", tk=1024)
    (conv, gq, gk, gv), gathered_conv = _conv_fwd(qkvg, conv_w, shards[0])
    beta, gc, gt = _rowwise("gates", _fn_gates, [full(ba)], [al, dtb], [(LANES, F32)] * 3)
    gcr = jnp.swapaxes(gc[:, HEADS:2 * HEADS], 0, 1).reshape(HEADS, t // CHUNK, 1, CHUNK)
    (pw, pu, pqg, pkg, pattn, pgl, ptm), gathered_prep = _gdn_prep(gq, gk, gv, beta, gc, gt, gcr, shards[1])
    w_out, w_kv, w_q, w_o, w_up, w_down = assemble(gathered_conv, gathered_prep)
    transposed = lambda a: jnp.swapaxes(a, -1, -2)
    w_qkvg_t, w_out_t, w_q_t, w_o_t, w_up_t, w_down_t = (transposed(a) for a in (w_qkvg, w_out, w_q, w_o, w_up, w_down))
    o_gdn, states = _gdn_scan(pw, pu, pqg, pkg, pattn, pgl)
    (on,) = _rowwise("out_norm", _fn_outnorm, [full(o_gdn), (qkvg, d, 3)], [og], [(d, BF16)])
    mix0 = _matmul(on, w_out, "nn", F32, "mm_gdn_out", tk=1024)
    x1, h1 = _rowwise("res_a0", _fn_res_norm, [full(x), full(mix0)], [row(mix_post, 0), row(mlp_pre, 0)], [(d, F32), (d, BF16)])
    (a0,) = _matmul(h1, w_up[0], "nn", (BF16,), "mm_up0", tk=1024, epilogue=_relu2_of)
    d0 = _matmul(a0, w_down[0], "nn", F32, "mm_down0")
    x2, hkv, hq = _rowwise("res_b0", _fn_res_norm2, [full(x1), full(d0)], [row(mlp_post, 0), kv_gain, row(mix_pre, 1)], [(d, F32), (d, BF16), (d, BF16)])
    w_k, w_v = w_kv[:, :d], w_kv[:, d:]
    w_k_t, w_v_t = transposed(w_k), transposed(w_v)
    kp = _matmul(hkv, w_k, "nn", BF16, "mm_k", tk=1024)
    vp = _matmul(hkv, w_v, "nn", BF16, "mm_v", tk=1024)
    qp = _matmul(hq, w_q, "nn", BF16, "mm_q", tk=1024)
    o_sb, rt, sb_first = _sb_fwd(qp, kp, vp)
    mix1 = _matmul(o_sb, w_o, "nn", F32, "mm_sb_out", tk=1024)
    x3, h3 = _rowwise("res_a1", _fn_res_norm, [full(x2), full(mix1)], [row(mix_post, 1), row(mlp_pre, 1)], [(d, F32), (d, BF16)])
    (a1,) = _matmul(h3, w_up[1], "nn", (BF16,), "mm_up1", tk=1024, epilogue=_relu2_of)
    d1 = _matmul(a1, w_down[1], "nn", F32, "mm_down1")

    loss, dx3, dd1, g_mlp_post1 = _loss_call(x3, d1, tgt, row(mlp_post, 1))
    (du1,) = _matmul(dd1, w_down_t[1], "nn", (BF16,), "mm_down1_dx", epilogue=_relu2_cotangent, extras=[a1])
    g_down1 = _matmul(a1, dd1, "tn", F32, "mm_down1_dw")
    dh3 = _matmul(du1, w_up_t[1], "nn", F32, "mm_up1_dx")
    g_up1 = _matmul(h3, du1, "tn", F32, "mm_up1_dw")
    (dx2, dmix1), (g_mix_post1, g_mlp_pre1) = _rowwise_bwd(
        "res_a1_bwd", _fn_res_norm, [full(x2), full(mix1)], [row(mix_post, 1), row(mlp_pre, 1)], [dx3, dh3], [F32, BF16])
    do_sb = _matmul(dmix1, w_o_t, "nn", BF16, "mm_sb_out_dx")
    g_o = _matmul(o_sb, dmix1, "tn", F32, "mm_sb_out_dw")
    dqp, dkp, dvp = _sb_bwd(qp, kp, vp, rt, sb_first, do_sb)
    dhq = _matmul(dqp, w_q_t, "nn", F32, "mm_q_dx")
    g_q = _matmul(hq, dqp, "tn", F32, "mm_q_dw")
    dhkv = _matmul(dvp, w_v_t, "nn", F32, "mm_v_dx", add=_matmul(dkp, w_k_t, "nn", F32, "mm_k_dx"))
    g_kv = jnp.concatenate([_matmul(hkv, dkp, "tn", F32, "mm_k_dw"), _matmul(hkv, dvp, "tn", F32, "mm_v_dw")], axis=1)
    (dx1, dd0), (g_mlp_post0, g_kv_gain, g_mix_pre1) = _rowwise_bwd(
        "res_b0_bwd", _fn_res_norm2, [full(x1), full(d0)], [row(mlp_post, 0), kv_gain, row(mix_pre, 1)], [dx2, dhkv, dhq], [F32, BF16])
    (du0,) = _matmul(dd0, w_down_t[0], "nn", (BF16,), "mm_down0_dx", epilogue=_relu2_cotangent, extras=[a0])
    g_down0 = _matmul(a0, dd0, "tn", F32, "mm_down0_dw")
    dh1 = _matmul(du0, w_up_t[0], "nn", F32, "mm_up0_dx")
    g_up0 = _matmul(h1, du0, "tn", F32, "mm_up0_dw")
    (dx0, dmix0), (g_mix_post0, g_mlp_pre0) = _rowwise_bwd(
        "res_a0_bwd", _fn_res_norm, [full(x), full(mix0)], [row(mix_post, 0), row(mlp_pre, 0)], [dx1, dh1], [F32, BF16])
    don = _matmul(dmix0, w_out_t, "nn", F32, "mm_gdn_out_dx")
    g_out = _matmul(on, dmix0, "tn", F32, "mm_gdn_out_dw")
    (do_gdn, dgate), (g_og,) = _rowwise_bwd("out_norm_bwd", _fn_outnorm, [full(o_gdn), (qkvg, d, 3)], [og], [don], [F32, F32])
    dpw, dpu, dpqg, dpkg, dpattn, dpgl = _gdn_scan_bwd(pw, pu, pqg, pkg, pattn, pgl, states, do_gdn)
    partial, partial_bf16 = [], ()
    if early_reduce is not None:
        partial, partial_bf16 = early_reduce(dict(mlp_w_up=(g_up0, g_up1), mlp_w_down=(g_down0, g_down1), gdn_w_out=g_out[None], w_kv=g_kv, sb_w_q=g_q[None], sb_w_o=g_o[None]))
    (dgq, dgk, dgv, dbeta, dgc, dgt, dgcr), from_chips = _gdn_prep_bwd(gq, gk, gv, beta, gc, gt, gcr, ptm, dpw, dpu, dpqg, dpkg, dpattn, dpgl, partial_bf16)
    dgcr_lanes = jnp.pad(jnp.swapaxes(dgcr.reshape(HEADS, t), 0, 1), ((0, 0), (HEADS, LANES - 2 * HEADS)))
    gate_cots = [dbeta, dgc + dgcr_lanes, dgt]
    (dba,), (g_al, g_dtb) = _rowwise_bwd("gates_bwd", _fn_gates, [full(ba)], [al, dtb], gate_cots, [BF16])
    dqkvg, g_conv = _conv_bwd(conv, (dgq, dgk, dgv), dgate, qkvg, conv_w)
    dh0b = _matmul(dba, w_ba, "nt", F32, "mm_gdn_ba_dx", tk=LANES)
    dh0 = _matmul(dqkvg, w_qkvg_t, "nn", F32, "mm_gdn_in_dx", add=dh0b)
    g_qkvg = _matmul(h0, dqkvg, "tn", F32, "mm_gdn_in_dw")
    g_ba = _matmul(h0, dba, "tn", F32, "mm_gdn_ba_dw")
    (grad_x,), (g_mix_pre0,) = _rowwise_bwd("norm_in_bwd", lambda xx, gg: (_rms(xx, gg), xx), [full(x)], [row(mix_pre, 0)], [dh0, dx0], [F32])

    grads = dict(
        mix_pre_gain=jnp.concatenate([g_mix_pre0, g_mix_pre1], axis=0),
        mix_post_gain=jnp.concatenate([g_mix_post0, g_mix_post1], axis=0),
        mlp_pre_gain=jnp.concatenate([g_mlp_pre0, g_mlp_pre1], axis=0),
        mlp_post_gain=jnp.concatenate([g_mlp_post0, g_mlp_post1], axis=0),
        mlp_w_up=(g_up0, g_up1),
        mlp_w_down=(g_down0, g_down1),
        gdn_w_in=jnp.concatenate([g_qkvg, g_ba[:, :2 * HEADS]], axis=1)[None],
        gdn_conv_w=g_conv[None, :CONV_K],
        gdn_a_log=g_al[:, HEADS:2 * HEADS],
        gdn_dt_bias=g_dtb[:, HEADS:2 * HEADS],
        gdn_out_gain=jnp.sum(g_og.reshape(HEADS, HEAD_DIM), axis=0, keepdims=True),
        gdn_w_out=g_out[None],
        kv_gain=g_kv_gain[0],
        w_kv=g_kv,
        sb_w_q=g_q[None],
        sb_w_o=g_o[None],
    )
    return loss, grad_x, grads, (partial, from_chips)


N_DEV = 8
N_CHIPS = 4
PACK_ROW_TILE = 128

_HBM = pl.BlockSpec(memory_space=pltpu.HBM)


def _place():
    return lax.axis_index("x"), lax.axis_index("y"), lax.axis_index("c")


def _other_chips(x, y):
    return [(1 - x, y), (x, 1 - y), (1 - x, 1 - y)]


def _remote(src, dst, send_sem, recv_sem, to):
    return pltpu.make_async_remote_copy(src_ref=src, dst_ref=dst, send_sem=send_sem, recv_sem=recv_sem, device_id=to, device_id_type=MESH)


def _gather8(v, name):
    rows, cols = v.shape

    def body(v_ref, out_ref, sum_ref, send_sems, recv_sems, local_sem):
        x, y, c = _place()
        me, sibling = (x, y, c), (x, y, 1 - c)
        chips = _other_chips(x, y)

        def blk(px, py, pc):
            return out_ref.at[pl.ds((4 * px + 2 * py + pc) * rows, rows), :]

        def copy(k, block, to, src=None):
            return _remote(blk(*block) if src is None else src, blk(*block), send_sems.at[k], recv_sems.at[k], to)

        mine = pltpu.make_async_copy(v_ref, blk(*me), local_sem)
        mine.start()
        first = [copy(0, me, sibling, src=v_ref)] + [copy(1 + j, me, (*chip, c), src=v_ref) for j, chip in enumerate(chips)]
        for cp in first:
            cp.start()
        passed = [copy(4 + j, (*chip, c), sibling) for j, chip in enumerate(chips)]
        for j, chip in enumerate(chips):
            copy(1 + j, (*chip, c), me).wait_recv()
            passed[j].start()
        copy(0, sibling, me).wait_recv()
        for j, chip in enumerate(chips):
            copy(4 + j, (*chip, 1 - c), me).wait_recv()
        for cp in first + passed:
            cp.wait_send()
        mine.wait()
        acc = out_ref[pl.ds(0, rows), :]
        for dev in range(1, N_DEV):
            acc = acc + out_ref[pl.ds(dev * rows, rows), :]
        sum_ref[...] = acc

    vm = pl.BlockSpec(memory_space=pltpu.VMEM)
    return pl.pallas_call(
        body,
        name=name,
        out_shape=[jax.ShapeDtypeStruct((N_DEV * rows, cols), v.dtype), jax.ShapeDtypeStruct((rows, cols), v.dtype)],
        in_specs=[vm],
        out_specs=[vm, vm],
        scratch_shapes=[pltpu.SemaphoreType.DMA((7,)), pltpu.SemaphoreType.DMA((7,)), pltpu.SemaphoreType.DMA],
    )(v)


def _hbm_call(body, name, arrs, out_shapes, sem_counts):
    n = len(arrs)

    def wrapped(*refs):
        body(refs[:n], refs[n:2 * n], *refs[2 * n:])

    return pl.pallas_call(
        wrapped,
        name=name,
        out_shape=[jax.ShapeDtypeStruct(s, a.dtype) for s, a in zip(out_shapes, arrs)],
        in_specs=[_HBM] * n,
        out_specs=[_HBM] * n,
        scratch_shapes=[pltpu.SemaphoreType.DMA((k,)) for k in sem_counts],
    )(*arrs)


def _gather_sends(w_refs, out_refs, send_sems, recv_sems):
    x, y, c = _place()
    s_me = 2 * x + y
    return [_remote(w.at[c], o.at[s_me, c], send_sems.at[3 * a + j], recv_sems.at[3 * a + j], (px, py, c))
            for a, (w, o) in enumerate(zip(w_refs, out_refs)) for j, (px, py) in enumerate(_other_chips(x, y))]


def _gather_finish(w_refs, out_refs, send_sems, recv_sems, fsend_sems, frecv_sems):
    x, y, c = _place()
    chips = _other_chips(x, y)
    passed = []
    for a, o in enumerate(out_refs):
        for j, (px, py) in enumerate(chips):
            half = o.at[2 * px + py, c]
            _remote(half, half, send_sems.at[3 * a + j], recv_sems.at[3 * a + j], (px, py, c)).wait_recv()
            fwd = _remote(half, half, fsend_sems.at[3 * a + j], frecv_sems.at[3 * a + j], (x, y, 1 - c))
            fwd.start()
            passed.append(fwd)
    for a, o in enumerate(out_refs):
        for j, (px, py) in enumerate(chips):
            half = o.at[2 * px + py, 1 - c]
            _remote(half, half, fsend_sems.at[3 * a + j], frecv_sems.at[3 * a + j], (x, y, 1 - c)).wait_recv()
    for cp in _gather_sends(w_refs, out_refs, send_sems, recv_sems) + passed:
        cp.wait_send()


def _gather_weights(arrs):
    n = len(arrs)

    def body(w_refs, out_refs, send_sems, recv_sems, fsend_sems, frecv_sems):
        for cp in _gather_sends(w_refs, out_refs, send_sems, recv_sems):
            cp.start()
        _gather_finish(w_refs, out_refs, send_sems, recv_sems, fsend_sems, frecv_sems)

    return _hbm_call(body, "gather_weights", arrs, [(N_CHIPS,) + a.shape for a in arrs], [3 * n] * 4)


def _swap_halves(arrs, name):
    n = len(arrs)

    def body(g_refs, a_refs, send_sems, recv_sems):
        x, y, c = _place()
        cps = [_remote(g.at[1 - c], a, send_sems.at[i], recv_sems.at[i], (x, y, 1 - c)) for i, (g, a) in enumerate(zip(g_refs, a_refs))]
        for cp in cps:
            cp.start()
        for cp in cps:
            cp.wait()

    return _hbm_call(body, name, arrs, [a.shape[1:] for a in arrs], [n, n])


def _scatter_copies(p_refs, b_refs, send_sems, recv_sems):
    x, y, c = _place()
    return [_remote(p.at[2 * px + py], b.at[j], send_sems.at[3 * i + j], recv_sems.at[3 * i + j], (px, py, c))
            for i, (p, b) in enumerate(zip(p_refs, b_refs)) for j, (px, py) in enumerate(_other_chips(x, y))]


def _scatter_to_chips(arrs):
    n = len(arrs)

    def body(p_refs, b_refs, send_sems, recv_sems):
        cps = _scatter_copies(p_refs, b_refs, send_sems, recv_sems)
        for cp in cps:
            cp.start()
        for cp in cps:
            cp.wait()

    return _hbm_call(body, "grads_to_chips", arrs, [(3,) + a.shape[1:] for a in arrs], [3 * n, 3 * n])


def _share_halves(arrs):
    n = len(arrs)

    def body(q_refs, out_refs, send_sems, recv_sems):
        x, y, c = _place()
        cps = [_remote(q, o, send_sems.at[i], recv_sems.at[i], (x, y, 1 - c)) for i, (q, o) in enumerate(zip(q_refs, out_refs))]
        for cp in cps:
            cp.start()
        for cp in cps:
            cp.wait()

    return _hbm_call(body, "grads_share", arrs, [a.shape for a in arrs], [n, n])


_GROUPS = (
    (("gdn_w_out", (1, 256, 1024), "rows"), ("mlp_w_up", (2, 1024, 1024), "cols")),
    (("mlp_w_down", (2, 1024, 1024), "rows"), ("sb_w_q", (1, 256, 1024), "rows"), ("sb_w_o", (1, 256, 1024), "rows")),
    (("w_kv", (1024, 512), "cols"),),
    (("gdn_w_in", (1, 1024, 1028), "cols"),),
)
_BEHIND_CONV, _BEHIND_PREP, _FIRST = slice(0, 1), slice(1, 3), slice(3, 4)
_EARLY_GRADS = slice(0, 3)


def _numel(shape):
    n = 1
    for s in shape:
        n *= s
    return n


def _half_rows(shape):
    return _numel(shape[:-1]) // 2


def _pack_shards(shards, dtype):
    return tuple(jnp.concatenate([shards[n].astype(dtype).reshape(2, _half_rows(shape), shape[-1]) for n, shape, _ in grp], axis=1) for grp in _GROUPS)


def _unpack_shards(bufs):
    out = {}
    for grp, buf in zip(_GROUPS, bufs):
        off = 0
        for n, shape, _ in grp:
            out[n] = buf[:, off:off + _half_rows(shape)].reshape(shape)
            off += _half_rows(shape)
    return out


def _join(stacked, how):
    nd = stacked.ndim - 1
    ax = nd - 1 if how == "cols" else nd - 2
    moved = jnp.moveaxis(stacked, 0, ax)
    shape = list(stacked.shape[1:])
    shape[ax] *= N_CHIPS
    return moved.reshape(shape)


def _split(full, shard_shape, how):
    nd = len(shard_shape)
    ax = nd - 1 if how == "cols" else nd - 2
    shape = list(shard_shape)
    shape.insert(ax, N_CHIPS)
    return jnp.moveaxis(full.reshape(shape), ax, 0)


def _unpack_full(gathered, groups):
    out = {}
    for grp, buf in zip(groups, gathered):
        off = 0
        for n, shape, how in grp:
            out[n] = _join(buf[:, :, off:off + _half_rows(shape)].reshape((N_CHIPS,) + shape), how)
            off += _half_rows(shape)
    return out


def _pack_full(full, groups):
    bufs = []
    for grp in groups:
        parts = []
        for n, shape, how in grp:
            if isinstance(full[n], tuple):
                assert len(full[n]) == shape[0] == 2
                parts.append(jnp.stack([_split(layer, shape[1:], how) for layer in full[n]], axis=1))
            else:
                parts.append(_split(full[n], shape, how).reshape(N_CHIPS, 2, _half_rows(shape), shape[-1]))
        buf = jnp.swapaxes(jnp.concatenate(parts, axis=2), 0, 1)
        bufs.append(buf.reshape(2, -1, buf.shape[-1]))
    return tuple(bufs)


_SMALL = (
    ("mix_pre_gain", (2, 1024)),
    ("mix_post_gain", (2, 1024)),
    ("mlp_pre_gain", (2, 1024)),
    ("mlp_post_gain", (2, 1024)),
    ("kv_gain", (1024,)),
    ("gdn_out_gain", (1, 128)),
    ("gdn_a_log", (1, 8)),
    ("gdn_dt_bias", (1, 8)),
    ("gdn_conv_w", (1, 4, 3072)),
    ("loss", ()),
)


def _rows_of(shape):
    return -(-_numel(shape) // LANES)


_SMALL_ROWS = -(-sum(_rows_of(s) for _, s in _SMALL) // 8) * 8


def _pack_small(vals):
    parts = []
    for n, shape in _SMALL:
        flat = vals[n].reshape(-1)
        parts.append(jnp.pad(flat, (0, _rows_of(shape) * LANES - flat.shape[0])))
    flat = jnp.concatenate(parts)
    return jnp.pad(flat, (0, _SMALL_ROWS * LANES - flat.shape[0])).reshape(_SMALL_ROWS, LANES)


def _unpack_small(packed):
    flat = packed.reshape(-1)
    out, off = {}, 0
    for n, shape in _SMALL:
        out[n] = flat[off:off + _numel(shape)].reshape(shape)
        off += _rows_of(shape) * LANES
    return out


_WEIGHTS = ("mix_pre_gain", "mix_post_gain", "mlp_pre_gain", "mlp_post_gain", "mlp_w_up", "mlp_w_down", "gdn_w_in", "gdn_conv_w",
            "gdn_a_log", "gdn_dt_bias", "gdn_out_gain", "gdn_w_out", "kv_gain", "w_kv", "sb_w_q", "sb_w_o")


def _as2d(a):
    return a.reshape(1, -1) if a.ndim <= 1 else a.reshape(-1, a.shape[-1])


def kernel(x, mix_pre_gain, mix_post_gain, mlp_pre_gain, mlp_post_gain, mlp_w_up, mlp_w_down, gdn_w_in, gdn_conv_w, gdn_a_log, gdn_dt_bias, gdn_out_gain, gdn_w_out, kv_gain, w_kv, sb_w_q, sb_w_o, loss_target, m_mix_pre_gain, m_mix_post_gain, m_mlp_pre_gain, m_mlp_post_gain, m_mlp_w_up, m_mlp_w_down, m_gdn_w_in, m_gdn_conv_w, m_gdn_a_log, m_gdn_dt_bias, m_gdn_out_gain, m_gdn_w_out, m_kv_gain, m_w_kv, m_sb_w_q, m_sb_w_o, v_mix_pre_gain, v_mix_post_gain, v_mlp_pre_gain, v_mlp_post_gain, v_mlp_w_up, v_mlp_w_down, v_gdn_w_in, v_gdn_conv_w, v_gdn_a_log, v_gdn_dt_bias, v_gdn_out_gain, v_gdn_w_out, v_kv_gain, v_w_kv, v_sb_w_q, v_sb_w_o):
    w = dict(mix_pre_gain=mix_pre_gain, mix_post_gain=mix_post_gain, mlp_pre_gain=mlp_pre_gain, mlp_post_gain=mlp_post_gain, mlp_w_up=mlp_w_up, mlp_w_down=mlp_w_down, gdn_w_in=gdn_w_in, gdn_conv_w=gdn_conv_w, gdn_a_log=gdn_a_log, gdn_dt_bias=gdn_dt_bias, gdn_out_gain=gdn_out_gain, gdn_w_out=gdn_w_out, kv_gain=kv_gain, w_kv=w_kv, sb_w_q=sb_w_q, sb_w_o=sb_w_o)
    m = dict(mix_pre_gain=m_mix_pre_gain, mix_post_gain=m_mix_post_gain, mlp_pre_gain=m_mlp_pre_gain, mlp_post_gain=m_mlp_post_gain, mlp_w_up=m_mlp_w_up, mlp_w_down=m_mlp_w_down, gdn_w_in=m_gdn_w_in, gdn_conv_w=m_gdn_conv_w, gdn_a_log=m_gdn_a_log, gdn_dt_bias=m_gdn_dt_bias, gdn_out_gain=m_gdn_out_gain, gdn_w_out=m_gdn_w_out, kv_gain=m_kv_gain, w_kv=m_w_kv, sb_w_q=m_sb_w_q, sb_w_o=m_sb_w_o)
    v = dict(mix_pre_gain=v_mix_pre_gain, mix_post_gain=v_mix_post_gain, mlp_pre_gain=v_mlp_pre_gain, mlp_post_gain=v_mlp_post_gain, mlp_w_up=v_mlp_w_up, mlp_w_down=v_mlp_w_down, gdn_w_in=v_gdn_w_in, gdn_conv_w=v_gdn_conv_w, gdn_a_log=v_gdn_a_log, gdn_dt_bias=v_gdn_dt_bias, gdn_out_gain=v_gdn_out_gain, gdn_w_out=v_gdn_w_out, kv_gain=v_kv_gain, w_kv=v_w_kv, sb_w_q=v_sb_w_q, sb_w_o=v_sb_w_o)
    cx, cy, cc = _place()
    chip = 2 * cx + cy
    conv_cols = gdn_conv_w.shape[-1]

    own = _pack_shards(w, BF16)
    with_own = lambda gathered, mine: [lax.dynamic_update_index_in_dim(g, m, chip, 0) for g, m in zip(gathered, mine)]
    w_in = _unpack_full(with_own(_gather_weights(own[_FIRST]), own[_FIRST]), _GROUPS[_FIRST])["gdn_w_in"][0]

    def assemble(gathered_conv, gathered_prep):
        full = {**_unpack_full(with_own(gathered_conv, own[_BEHIND_CONV]), _GROUPS[_BEHIND_CONV]),
                **_unpack_full(with_own(gathered_prep, own[_BEHIND_PREP]), _GROUPS[_BEHIND_PREP])}
        return full["gdn_w_out"][0], full["w_kv"], full["sb_w_q"][0], full["sb_w_o"][0], full["mlp_w_up"], full["mlp_w_down"]

    conv_rows = jnp.pad(gdn_conv_w[0], ((0, 8 - CONV_K), (0, 0))).reshape(-1, LANES)
    conv_all, _ = _gather8(conv_rows, "gather_conv_w")
    conv_all = conv_all.reshape(N_CHIPS, 2, 8, conv_cols)[:, 0, :CONV_K]
    conv_full = jnp.swapaxes(conv_all, 0, 1).reshape(CONV_K, N_CHIPS * conv_cols)

    w_in = (w_in[:, :4 * HEADS * HEAD_DIM], jnp.pad(w_in[:, 4 * HEADS * HEAD_DIM:], ((0, 0), (0, LANES - 2 * HEADS))))
    gains = (mix_pre_gain, mix_post_gain, mlp_pre_gain, mlp_post_gain, kv_gain[None])
    small = (conv_full, gdn_a_log, gdn_dt_bias, gdn_out_gain)
    tile = PACK_ROW_TILE

    def to_chip_partials(grads_full, groups, tag):
        bufs = _pack_full(grads_full, groups)
        p32, p16 = [], []
        for i, (buf, other) in enumerate(zip(bufs, _swap_halves(bufs, f"grads_to_sibling_{tag}"))):
            _, n, cols = buf.shape
            p, pb = _add_rows(f"grads_add_sibling_{tag}{i}", [(buf.reshape(2 * n, cols), cc * (n // tile)), (other, 0)], n, (F32, BF16), tile)
            p32.append(p.reshape(N_CHIPS, -1, cols))
            p16.append(pb.reshape(N_CHIPS, -1, cols))
        return p32, tuple(p16)

    loss_rows, grad_x, g_full, (partial_early, from_chips_early) = _local_step(
        x[0], loss_target[0], gains, w_in, small, (own[_BEHIND_CONV], own[_BEHIND_PREP]), assemble, lambda g: to_chip_partials(g, _GROUPS[_EARLY_GRADS], "early"))

    partial_in, partial_in_bf16 = to_chip_partials(g_full, _GROUPS[_FIRST], "in")
    partial = list(partial_early) + partial_in
    from_chips = list(from_chips_early) + list(_scatter_to_chips(partial_in_bf16))
    reduced = []
    for i, (p, others) in enumerate(zip(partial, from_chips)):
        _, r, cols = p.shape
        terms = [(p.reshape(N_CHIPS * r, cols), chip * (r // tile))] + [(others.reshape(3 * r, cols), j * (r // tile)) for j in range(3)]
        reduced.append(_add_rows(f"grads_add_chips_{i}", terms, r, (F32,), tile)[0])
    g_shard = _unpack_shards([jnp.where(cc == 0, jnp.stack([r, o]), jnp.stack([o, r])) for r, o in zip(reduced, _share_halves(tuple(reduced)))])

    g_small_local = {n: g_full[n] for n, _ in _SMALL if n != "loss"}
    g_small_local["loss"] = loss_rows[0, 0]
    _, small_sum = _gather8(_pack_small(g_small_local), "allreduce_small")
    g_small = _unpack_small(small_sum)
    loss = g_small.pop("loss")
    g_small["gdn_conv_w"] = lax.dynamic_slice_in_dim(g_small["gdn_conv_w"], chip * conv_cols, conv_cols, axis=2)

    grads = {**g_shard, **g_small}
    deltas, new_m, new_v = {}, {}, {}
    for n in _WEIGHTS:
        d2, m2, v2 = _adamw(_as2d(w[n]), _as2d(grads[n]), _as2d(m[n]), _as2d(v[n]), "adamw_" + n)
        deltas[n], new_m[n], new_v[n] = d2.reshape(w[n].shape), m2.reshape(w[n].shape), v2.reshape(w[n].shape)
    return (loss, grad_x[None], *[grads[n].reshape(w[n].shape) for n in _WEIGHTS], *[deltas[n] for n in _WEIGHTS],
            *[new_m[n] for n in _WEIGHTS], *[new_v[n] for n in _WEIGHTS])
```

```python
import functools

import jax
import jax.numpy as jnp
from jax import lax
from jax.experimental import pallas as pl
from jax.experimental.pallas import tpu as pltpu

F32, BF16 = jnp.float32, jnp.bfloat16
HI = lax.Precision.HIGHEST
MESH = pl.DeviceIdType.MESH

EPS = 1e-6
D_MODEL = 1024
HEADS = 8
HEAD_DIM = 128
CHUNK = 64
CHUNK_SHIFT = CHUNK.bit_length() - 1
CONV_K = 4
D_FF = 4096
QKV = 3 * HEADS * HEAD_DIM

ADAM_LR, ADAM_B1, ADAM_B2, ADAM_EPS, ADAM_WD, ADAM_STEP = 0.001, 0.9, 0.999, 1e-08, 0.01, 10

VMEM_LIMIT_BYTES = 48 * 1024 * 1024
LANES = 128

NN = ((1,), (0,))
NT = ((1,), (1,))
TN = ((0,), (0,))


def _dot(a, b, dims=NN, precision=None):
    return lax.dot_general(a, b, (dims, ((), ())), precision=precision, preferred_element_type=F32)


def _params(*sem):
    return pltpu.CompilerParams(dimension_semantics=sem, vmem_limit_bytes=VMEM_LIMIT_BYTES)


def _iota(shape, axis):
    return lax.broadcasted_iota(jnp.int32, shape, axis)


def _matmul(a, b, mode, out_dtype, name, tm=1024, tn=1024, tk=2048, add=None, epilogue=None, extras=()):
    if mode == "nn":
        (m, k), (k2, n) = a.shape, b.shape
    elif mode == "nt":
        (m, k), (n, k2) = a.shape, b.shape
    else:
        (k, m), (k2, n) = a.shape, b.shape
    assert k == k2, (a.shape, b.shape, mode)
    tm, tn, tk = min(tm, m), min(tn, n), min(tk, k)
    assert m % tm == 0 and n % tn == 0 and k % tk == 0, (a.shape, b.shape, mode)
    nk = k // tk
    dims = {"nn": NN, "nt": NT, "tn": TN}[mode]
    tiles = ([add] if add is not None else []) + list(extras)
    out_dtypes = out_dtype if epilogue is not None else (out_dtype,)
    n_in = 2 + len(tiles)

    def finish(acc, extra_refs, o_refs):
        res = (acc,) if epilogue is None else epilogue(acc, *[r[...] for r in extra_refs])
        for o_ref, r in zip(o_refs, res):
            o_ref[...] = r.astype(o_ref.dtype)

    def body(*refs):
        a_ref, b_ref = refs[:2]
        extra_refs = refs[n_in - len(extras):n_in]
        o_refs, acc_ref = refs[n_in:-1], refs[-1]
        prod = _dot(a_ref[...].astype(BF16), b_ref[...].astype(BF16), dims)
        if nk == 1:
            finish(prod + refs[2][...].astype(F32) if add is not None else prod, extra_refs, o_refs)
            return
        kk = pl.program_id(2)

        @pl.when(kk == 0)
        def _():
            acc_ref[...] = refs[2][...].astype(F32) if add is not None else jnp.zeros_like(acc_ref)

        acc_ref[...] += prod

        @pl.when(kk == nk - 1)
        def _():
            finish(acc_ref[...], extra_refs, o_refs)

    a_spec = pl.BlockSpec((tk, tm), lambda i, j, kk: (kk, i)) if mode == "tn" else pl.BlockSpec((tm, tk), lambda i, j, kk: (i, kk))
    b_spec = pl.BlockSpec((tn, tk), lambda i, j, kk: (j, kk)) if mode == "nt" else pl.BlockSpec((tk, tn), lambda i, j, kk: (kk, j))
    o_spec = pl.BlockSpec((tm, tn), lambda i, j, kk: (i, j))
    res = pl.pallas_call(
        body,
        name=name,
        grid=(m // tm, n // tn, nk),
        in_specs=[a_spec, b_spec] + [o_spec] * len(tiles),
        out_specs=[o_spec] * len(out_dtypes),
        out_shape=[jax.ShapeDtypeStruct((m, n), dt) for dt in out_dtypes],
        scratch_shapes=[pltpu.VMEM((tm, tn), F32)],
        compiler_params=_params("parallel", "parallel", "arbitrary"),
    )(a, b, *tiles)
    return res if epilogue is not None else res[0]


def _row_specs(rows, tm):
    return [pl.BlockSpec((tm, w), lambda i, cb=cb: (i, cb)) for _, w, cb in rows]


def _full_spec(p):
    return pl.BlockSpec(p.shape, lambda i: (0,) * p.ndim)


def _rowwise(name, fn, rows, params, outs, tm=256, gather=()):
    t = rows[0][0].shape[0]
    tm = min(tm, t)
    steps = t // tm
    nr, npar, nout, ng = len(rows), len(params), len(outs), len(gather)

    def body(*refs):
        ins = [r[...].astype(F32) for r in refs[:nr]]
        ps = [p[...] for p in refs[nr:nr + npar]]
        shard_refs = refs[nr + npar:nr + npar + ng]
        o_refs = refs[nr + npar + ng:nr + npar + ng + nout]
        all_refs, sems = refs[nr + npar + ng + nout:nr + npar + 2 * ng + nout], refs[nr + npar + 2 * ng + nout:]
        if ng:
            @pl.when(pl.program_id(0) == 0)
            def _():
                for cp in _gather_sends(shard_refs, all_refs, *sems[:2]):
                    cp.start()

        res = fn(*ins, *ps)
        for o_ref, r in zip(o_refs, res):
            o_ref[...] = r.astype(o_ref.dtype)

        if ng:
            @pl.when(pl.program_id(0) == steps - 1)
            def _():
                _gather_finish(shard_refs, all_refs, *sems)

    return pl.pallas_call(
        body,
        name=name,
        grid=(steps,),
        in_specs=_row_specs(rows, tm) + [_full_spec(p) for p in params] + [_HBM] * ng,
        out_specs=[pl.BlockSpec((tm, w), lambda i: (i, 0)) for w, _ in outs] + [_HBM] * ng,
        out_shape=[jax.ShapeDtypeStruct((t, w), dt) for w, dt in outs] + [jax.ShapeDtypeStruct((N_CHIPS,) + s.shape, s.dtype) for s in gather],
        scratch_shapes=[pltpu.SemaphoreType.DMA((3 * ng,))] * (4 if ng else 0),
        compiler_params=_params("arbitrary" if ng else "parallel"),
    )(*[r[0] for r in rows], *params, *gather)


def _add_rows(name, terms, n_rows, out_dtypes, tm):
    cols = terms[0][0].shape[1]
    firsts = jnp.stack([jnp.asarray(first, jnp.int32) for _, first in terms])

    def body(firsts_ref, *refs):
        acc = refs[0][...].astype(F32)
        for r in refs[1:len(terms)]:
            acc = acc + r[...].astype(F32)
        for o_ref in refs[len(terms):]:
            o_ref[...] = acc.astype(o_ref.dtype)

    return pl.pallas_call(
        body,
        name=name,
        grid_spec=pltpu.PrefetchScalarGridSpec(
            num_scalar_prefetch=1,
            grid=(n_rows // tm,),
            in_specs=[pl.BlockSpec((tm, cols), lambda i, firsts_ref, k=k: (firsts_ref[k] + i, 0)) for k in range(len(terms))],
            out_specs=[pl.BlockSpec((tm, cols), lambda i, firsts_ref: (i, 0)) for _ in out_dtypes],
        ),
        out_shape=[jax.ShapeDtypeStruct((n_rows, cols), dt) for dt in out_dtypes],
        compiler_params=_params("parallel"),
    )(firsts, *[a for a, _ in terms])


def _rowwise_bwd(name, fn, rows, params, cots, grad_dtypes, tm=256):
    t = rows[0][0].shape[0]
    tm = min(tm, t)
    nr, npar, nc = len(rows), len(params), len(cots)
    want = [j for j, dt in enumerate(grad_dtypes) if dt is not None]
    widths = [rows[j][1] for j in want]
    n_row_outs = len(want)

    def body(*refs):
        i = pl.program_id(0)
        ins = [r[...].astype(F32) for r in refs[:nr]]
        ps = [p[...] for p in refs[nr:nr + npar]]
        cs = tuple(c[...].astype(F32) for c in refs[nr + npar:nr + npar + nc])
        _, vjp = jax.vjp(fn, *ins, *ps)
        gs = vjp(cs)
        outs = refs[nr + npar + nc:]
        for o_ref, j in zip(outs, want):
            o_ref[...] = gs[j].astype(o_ref.dtype)
        pg_refs = outs[n_row_outs:]

        @pl.when(i == 0)
        def _():
            for pg in pg_refs:
                pg[...] = jnp.zeros_like(pg)

        for pg, g in zip(pg_refs, gs[nr:]):
            pg[...] += g

    row_specs = [pl.BlockSpec((tm, w), lambda i: (i, 0)) for w in widths]
    row_shapes = [jax.ShapeDtypeStruct((t, w), grad_dtypes[j]) for j, w in zip(want, widths)]
    res = pl.pallas_call(
        body,
        name=name,
        grid=(t // tm,),
        in_specs=_row_specs(rows, tm) + [_full_spec(p) for p in params] + [pl.BlockSpec((tm, c.shape[1]), lambda i: (i, 0)) for c in cots],
        out_specs=row_specs + [_full_spec(p) for p in params],
        out_shape=row_shapes + [jax.ShapeDtypeStruct(p.shape, F32) for p in params],
        compiler_params=_params("arbitrary"),
    )(*[r[0] for r in rows], *params, *cots)
    return res[:n_row_outs], res[n_row_outs:]


def _rms(x, g):
    return x * lax.rsqrt(jnp.mean(x * x, axis=-1, keepdims=True) + EPS) * g


def _sigmoid(x):
    return 1.0 / (1.0 + jnp.exp(-x))


def _softplus(x):
    return jnp.maximum(x, 0.0) + jnp.log1p(jnp.exp(-jnp.abs(x)))


def _two_pass(x, m):
    hi = x.astype(BF16)
    lo = (x - hi.astype(F32)).astype(BF16)
    return _dot(hi, m) + _dot(lo, m)


def _head_sum_impl(x):
    sums = [jnp.sum(x[:, h * HEAD_DIM:(h + 1) * HEAD_DIM], axis=-1, keepdims=True) for h in range(HEADS)]
    return jnp.concatenate([jnp.broadcast_to(s, (x.shape[0], HEAD_DIM)) for s in sums], axis=1)


@jax.custom_vjp
def _head_sum(x):
    return _head_sum_impl(x)


_head_sum.defvjp(lambda x: (_head_sum_impl(x), None), lambda _, g: (_head_sum_impl(g),))


def _fn_norm(x, g):
    return (_rms(x, g),)


def _fn_gates(ba, al, dt):
    col = _iota((1, LANES), 1)
    g = jnp.where((col >= HEADS) & (col < 2 * HEADS), -jnp.exp(al) * _softplus(ba + dt), 0.0)
    rows = ba.shape[0]
    r, c = _iota((rows, rows), 0), _iota((rows, rows), 1)
    same = (r >> CHUNK_SHIFT) == (c >> CHUNK_SHIFT)
    gc = _dot(jnp.where(same & (r >= c), 1.0, 0.0), g, precision=HI)
    gtot = _dot(jnp.where(same, 1.0, 0.0), g, precision=HI)
    return _sigmoid(ba), gc, gtot


def _fn_post_q(c):
    s = c * _sigmoid(c)
    return (s * lax.rsqrt(_head_sum(s * s) + EPS) * (HEAD_DIM ** -0.5),)


def _fn_post_k(c):
    s = c * _sigmoid(c)
    return (s * lax.rsqrt(_head_sum(s * s) + EPS),)


def _fn_post_v(c):
    return (c * _sigmoid(c),)


def _fn_post(cq, ck, cv):
    return _fn_post_q(cq) + _fn_post_k(ck) + _fn_post_v(cv)


def _fn_outnorm(o, gate, og):
    y = o * lax.rsqrt(_head_sum(o * o) * (1.0 / HEAD_DIM) + EPS) * og
    return (y * (gate * _sigmoid(gate)),)


def _fn_res_norm(x, m, gp, gn):
    x1 = x + _rms(m, gp)
    return x1, _rms(x1, gn)


def _fn_res_norm2(x, m, gp, ga, gb):
    x1 = x + _rms(m, gp)
    return x1, _rms(x1, ga), _rms(x1, gb)


def _relu2_of(u):
    r = jnp.maximum(u, 0.0)
    return (r * r,)


def _relu2_cotangent(da, a):
    return (da * (2.0 * jnp.sqrt(a.astype(F32))),)


def _loss_call(x3, d1, tgt, g, tm=256):
    t, d = x3.shape
    tm = min(tm, t)

    def body(x_ref, d_ref, t_ref, g_ref, loss_ref, dx_ref, dd_ref, dg_ref):
        i = pl.program_id(0)
        y, vjp = jax.vjp(lambda x, dd, gg: x + _rms(dd, gg), x_ref[...], d_ref[...], g_ref[...])
        err = y - t_ref[...]
        lrow = 0.5 * jnp.mean(err * err, axis=-1, keepdims=True)
        dx, dd, dg = vjp(err * (1.0 / d))
        dx_ref[...] = dx
        dd_ref[...] = dd.astype(dd_ref.dtype)

        @pl.when(i == 0)
        def _():
            loss_ref[...] = jnp.zeros_like(loss_ref)
            dg_ref[...] = jnp.zeros_like(dg_ref)

        loss_ref[...] += jnp.broadcast_to(jnp.sum(lrow, axis=0, keepdims=True), loss_ref.shape)
        dg_ref[...] += dg

    row = pl.BlockSpec((tm, d), lambda i: (i, 0))
    return pl.pallas_call(
        body,
        name="loss_head",
        grid=(t // tm,),
        in_specs=[row, row, row, _full_spec(g)],
        out_specs=[pl.BlockSpec((8, LANES), lambda i: (0, 0)), row, row, _full_spec(g)],
        out_shape=[jax.ShapeDtypeStruct((8, LANES), F32), jax.ShapeDtypeStruct((t, d), F32), jax.ShapeDtypeStruct((t, d), BF16), jax.ShapeDtypeStruct(g.shape, F32)],
        compiler_params=_params("arbitrary"),
    )(x3, d1, tgt, g)


HALO = 8


def _conv_fwd(qkvg, conv_w, shards, tm=256):
    t = qkvg.shape[0]
    tm = min(tm, t)
    steps = t // tm
    wide = QKV // 3
    n = len(shards)

    def body(*refs):
        cur_ref, prev_ref, w_ref = refs[:3]
        shard_refs = refs[3:3 + n]
        o_ref, q_ref, k_ref, v_ref = refs[3 + n:7 + n]
        all_refs = refs[7 + n:7 + 2 * n]
        buf, sems = refs[7 + 2 * n], refs[8 + 2 * n:]
        i = pl.program_id(0)

        if n:
            @pl.when(i == 0)
            def _():
                for cp in _gather_sends(shard_refs, all_refs, *sems[:2]):
                    cp.start()

        buf[0:HALO, :] = jnp.where(i > 0, prev_ref[...], 0.0)
        buf[HALO:, :] = cur_ref[...]
        acc = buf[pl.ds(HALO - CONV_K + 1, tm), :] * w_ref[pl.ds(0, 1), :]
        for j in range(1, CONV_K):
            acc = acc + buf[pl.ds(HALO - CONV_K + 1 + j, tm), :] * w_ref[pl.ds(j, 1), :]
        o_ref[...] = acc
        (q_ref[...], k_ref[...], v_ref[...]) = _fn_post(acc[:, 0:wide], acc[:, wide:2 * wide], acc[:, 2 * wide:])

        if n:
            @pl.when(i == steps - 1)
            def _():
                _gather_finish(shard_refs, all_refs, *sems)

    part = pl.BlockSpec((tm, wide), lambda i: (i, 0))
    res = pl.pallas_call(
        body,
        name="conv_fwd",
        grid=(steps,),
        in_specs=[
            pl.BlockSpec((tm, QKV), lambda i: (i, 0)),
            pl.BlockSpec((HALO, QKV), lambda i: (jnp.maximum(i * (tm // HALO) - 1, 0), 0)),
            pl.BlockSpec((CONV_K, QKV), lambda i: (0, 0)),
        ] + [_HBM] * n,
        out_specs=[pl.BlockSpec((tm, QKV), lambda i: (i, 0)), part, part, part] + [_HBM] * n,
        out_shape=[jax.ShapeDtypeStruct((t, QKV), F32)] + [jax.ShapeDtypeStruct((t, wide), F32)] * 3
        + [jax.ShapeDtypeStruct((N_CHIPS,) + s.shape, s.dtype) for s in shards],
        scratch_shapes=[pltpu.VMEM((tm + HALO, QKV), F32)] + [pltpu.SemaphoreType.DMA((3 * n,))] * (4 if n else 0),
        compiler_params=_params("arbitrary"),
    )(qkvg, qkvg, conv_w, *shards)
    return res[:4], res[4:]


def _conv_bwd(conv, dqkv, dgate, qkvg, conv_w, tm=256):
    t = conv.shape[0]
    tm = min(tm, t)
    n = t // tm
    wg = dgate.shape[1]
    wide = QKV // 3

    def conv_cotangent(c_ref, g_refs):
        parts = [c_ref[:, j * wide:(j + 1) * wide] for j in range(3)]
        _, vjp = jax.vjp(_fn_post, *parts)
        return vjp(tuple(g[...] for g in g_refs))

    def body(c_ref, cn_ref, dq_ref, dk_ref, dv_ref, dqn_ref, dkn_ref, dvn_ref, dgate_ref, x_ref, xp_ref, w_ref, dx_ref, dw_ref, bufd, bufx):
        i = pl.program_id(0)
        for j, (cur, nxt) in enumerate(zip(conv_cotangent(c_ref, (dq_ref, dk_ref, dv_ref)), conv_cotangent(cn_ref, (dqn_ref, dkn_ref, dvn_ref)))):
            bufd[0:tm, j * wide:(j + 1) * wide] = cur
            bufd[tm:, j * wide:(j + 1) * wide] = jnp.where(i < n - 1, nxt, 0.0)
        bufx[0:HALO, :] = jnp.where(i > 0, xp_ref[...], 0.0)
        bufx[HALO:, :] = x_ref[...]

        @pl.when(i == 0)
        def _():
            dw_ref[...] = jnp.zeros_like(dw_ref)

        dcv = bufd[0:tm, :]
        acc = bufd[pl.ds(CONV_K - 1, tm), :] * w_ref[pl.ds(0, 1), :]
        for j in range(1, CONV_K):
            acc = acc + bufd[pl.ds(CONV_K - 1 - j, tm), :] * w_ref[pl.ds(j, 1), :]
        dx_ref[:, 0:QKV] = acc.astype(dx_ref.dtype)
        dx_ref[:, QKV:] = dgate_ref[...].astype(dx_ref.dtype)
        for j in range(CONV_K):
            dw_ref[pl.ds(j, 1), :] += jnp.sum(dcv * bufx[pl.ds(HALO - CONV_K + 1 + j, tm), :], axis=0, keepdims=True)

    def cur(width):
        return pl.BlockSpec((tm, width), lambda i: (i, 0))

    def nxt(width):
        return pl.BlockSpec((HALO, width), lambda i: (jnp.minimum((i + 1) * (tm // HALO), t // HALO - 1), 0))

    return pl.pallas_call(
        body,
        name="conv_bwd",
        grid=(n,),
        in_specs=[cur(QKV), nxt(QKV)] + [cur(wide)] * 3 + [nxt(wide)] * 3 + [
            cur(wg),
            cur(QKV),
            pl.BlockSpec((HALO, QKV), lambda i: (jnp.maximum(i * (tm // HALO) - 1, 0), 0)),
            pl.BlockSpec((CONV_K, QKV), lambda i: (0, 0)),
        ],
        out_specs=[pl.BlockSpec((tm, QKV + wg), lambda i: (i, 0)), pl.BlockSpec((HALO, QKV), lambda i: (0, 0))],
        out_shape=[jax.ShapeDtypeStruct((t, QKV + wg), BF16), jax.ShapeDtypeStruct((HALO, QKV), F32)],
        scratch_shapes=[pltpu.VMEM((tm + HALO, QKV), F32), pltpu.VMEM((tm + HALO, QKV), F32)],
        compiler_params=_params("arbitrary"),
    )(conv, conv, *dqkv, *dqkv, dgate, qkvg, qkvg, conv_w)


PREP_CHUNKS = 16
PREP_BWD_CHUNKS = 4
SCAN_CHUNKS = 4


def _hi_lo(x):
    hi = x.astype(BF16)
    return hi, (x - hi.astype(F32)).astype(BF16)


def _mm3(a, b, dims=NN):
    (ah, al), (bh, bl) = _hi_lo(a), _hi_lo(b)
    return _dot(ah, bh, dims) + (_dot(ah, bl, dims) + _dot(al, bh, dims))


def _neumann(lowers):
    c = lowers[0].shape[0]
    eye = jnp.where(_iota((c, c), 0) == _iota((c, c), 1), 1.0, 0.0)
    ps = [-low for low in lowers]
    tmats = [eye + p for p in ps]
    for _ in range(CHUNK_SHIFT - 1):
        ps = [_mm3(p, p) for p in ps]
        tmats = [t + _mm3(t, p) for t, p in zip(tmats, ps)]
    return tuple(tmats)


def _inv_cotangents(tmats, dts):
    half = [_mm3(t, dt, TN) for t, dt in zip(tmats, dts)]
    return tuple(-_mm3(hf, t, NT) for hf, t in zip(half, tmats))


@jax.custom_vjp
def _tri_inv(lowers):
    return _neumann(lowers)


def _tri_inv_fwd(lowers):
    tmats = _neumann(lowers)
    return tmats, tmats


_tri_inv.defvjp(_tri_inv_fwd, lambda tmats, dts: (_inv_cotangents(tmats, dts),))


@jax.custom_vjp
def _tri_inv_known(lowers, tmats):
    return tmats


_tri_inv_known.defvjp(lambda lowers, tmats: (tmats, tmats),
                      lambda tmats, dts: (_inv_cotangents(tmats, dts), tuple(jnp.zeros_like(t) for t in tmats)))


def _prep_chunks(qs, ks, vs, bs, gcs, gts, gcrs, tmats=None):
    c = CHUNK
    r, col = _iota((c, c), 0), _iota((c, c), 1)
    incl, strict = r >= col, r > col
    decays = [jnp.where(incl, jnp.exp(jnp.where(incl, gc - gcr, 0.0)), 0.0) for gc, gcr in zip(gcs, gcrs)]
    kbs = [k * b for k, b in zip(ks, bs)]
    kbfs = [k.astype(BF16) for k in ks]
    lowers = tuple(jnp.where(strict, _dot(kb.astype(BF16), kbf, NT) * decay, 0.0) for kb, kbf, decay in zip(kbs, kbfs, decays))
    tmats = _tri_inv(lowers) if tmats is None else _tri_inv_known(lowers, tuple(tmats))
    outs = []
    for q, k, v, b, gc, gt, kb, kbf, decay, tmat in zip(qs, ks, vs, bs, gcs, gts, kbs, kbfs, decays, tmats):
        tb = tmat.astype(BF16)
        egc = jnp.exp(gc)
        w = _dot(tb, (kb * egc).astype(BF16))
        u = _dot(tb, (v * b).astype(BF16))
        attn = _dot(q.astype(BF16), kbf, NT) * decay
        gl = jnp.broadcast_to(jnp.exp(jnp.mean(gt.reshape(c // 8, 8, 1), axis=0)), (8, HEAD_DIM))
        outs.append((w, u, q * egc, k * jnp.exp(gt - gc), attn, gl))
    return tuple(outs), tmats


def _prep_specs(rows, gch):
    head = pl.BlockSpec((rows, HEAD_DIM), lambda n, h: (n, h))
    gates = pl.BlockSpec((rows, LANES), lambda n, h: (n, 0))
    gcrow = pl.BlockSpec((1, gch, 1, CHUNK), lambda n, h: (h, n, 0, 0))
    square = pl.BlockSpec((1, rows, CHUNK), lambda n, h: (h, n, 0))
    gl = pl.BlockSpec((1, gch * 8, HEAD_DIM), lambda n, h: (h, n, 0))
    return head, gates, gcrow, square, gl


def _pick_lane(ref, sl, lane):
    return jnp.sum(jnp.where(_iota((1, LANES), 1) == lane, ref[sl, :], 0.0), axis=1, keepdims=True)


def _prep_inputs(q_ref, k_ref, v_ref, b_ref, gc_ref, gt_ref, gcr_ref, sls, h):
    return ([q_ref[sl, :] for sl in sls], [k_ref[sl, :] for sl in sls], [v_ref[sl, :] for sl in sls],
            [_pick_lane(b_ref, sl, h) for sl in sls], [_pick_lane(gc_ref, sl, h + HEADS) for sl in sls],
            [_pick_lane(gt_ref, sl, h + HEADS) for sl in sls], [gcr_ref[0, c] for c in range(len(sls))])


def _gdn_prep(q, k, v, beta, gc, gt, gcr, shards=()):
    t = q.shape[0]
    gch = min(PREP_CHUNKS, t // CHUNK)
    rows = gch * CHUNK
    steps = t // rows
    n = len(shards)

    def body(*refs):
        q_ref, k_ref, v_ref, b_ref, gc_ref, gt_ref, gcr_ref = refs[:7]
        shard_refs = refs[7:7 + n]
        w_ref, u_ref, qg_ref, kg_ref, at_ref, gl_ref, tm_ref = refs[7 + n:14 + n]
        all_refs, sems = refs[14 + n:14 + 2 * n], refs[14 + 2 * n:]
        h = pl.program_id(1)

        if n:
            @pl.when(jnp.logical_and(pl.program_id(0) == 0, h == 0))
            def _():
                for cp in _gather_sends(shard_refs, all_refs, *sems[:2]):
                    cp.start()

        sls = [pl.ds(c * CHUNK, CHUNK) for c in range(gch)]
        outs, tmats = _prep_chunks(*_prep_inputs(q_ref, k_ref, v_ref, b_ref, gc_ref, gt_ref, gcr_ref, sls, h))
        for c, (sl, (w, u, qg, kg, attn, gl), tmat) in enumerate(zip(sls, outs, tmats)):
            w_ref[sl, :] = w.astype(BF16)
            u_ref[sl, :] = u
            qg_ref[sl, :] = qg.astype(BF16)
            kg_ref[sl, :] = kg.astype(BF16)
            at_ref[0, sl, :] = attn.astype(BF16)
            gl_ref[0, pl.ds(c * 8, 8), :] = gl
            tm_ref[0, sl, :] = tmat

        if n:
            @pl.when(jnp.logical_and(pl.program_id(0) == steps - 1, h == HEADS - 1))
            def _():
                _gather_finish(shard_refs, all_refs, *sems)

    hb, col, gcrow, square, glb = _prep_specs(rows, gch)
    wide = HEADS * HEAD_DIM
    res = pl.pallas_call(
        body,
        name="gdn_prep",
        grid=(steps, HEADS),
        in_specs=[hb, hb, hb, col, col, col, gcrow] + [_HBM] * n,
        out_specs=[hb, hb, hb, hb, square, glb, square] + [_HBM] * n,
        out_shape=[
            jax.ShapeDtypeStruct((t, wide), BF16),
            jax.ShapeDtypeStruct((t, wide), F32),
            jax.ShapeDtypeStruct((t, wide), BF16),
            jax.ShapeDtypeStruct((t, wide), BF16),
            jax.ShapeDtypeStruct((HEADS, t, CHUNK), BF16),
            jax.ShapeDtypeStruct((HEADS, t // CHUNK * 8, HEAD_DIM), F32),
            jax.ShapeDtypeStruct((HEADS, t, CHUNK), F32),
        ] + [jax.ShapeDtypeStruct((N_CHIPS,) + s.shape, s.dtype) for s in shards],
        scratch_shapes=[pltpu.SemaphoreType.DMA((3 * n,))] * (4 if n else 0),
        compiler_params=_params("arbitrary", "arbitrary") if n else _params("parallel", "parallel"),
    )(q, k, v, beta, gc, gt, gcr, *shards)
    return res[:7], res[7:]


def _gdn_prep_bwd(q, k, v, beta, gc, gt, gcr, tmat, dw, du, dqg, dkg, dattn, dgl, partials=()):
    t = q.shape[0]
    gch = min(PREP_BWD_CHUNKS, t // CHUNK)
    rows = gch * CHUNK
    steps = t // rows
    n_sc = len(partials)

    def body(*refs):
        (q_ref, k_ref, v_ref, b_ref, gc_ref, gt_ref, gcr_ref, tm_ref, dw_ref, du_ref, dqg_ref, dkg_ref, dat_ref, dgl_ref) = refs[:14]
        p_refs = refs[14:14 + n_sc]
        dq_ref, dk_ref, dv_ref, db_ref, dgc_ref, dgt_ref, dgcr_ref = refs[14 + n_sc:21 + n_sc]
        from_refs, sems = refs[21 + n_sc:21 + 2 * n_sc], refs[21 + 2 * n_sc:]
        h = pl.program_id(1)
        lane = _iota((1, LANES), 1)

        if n_sc:
            @pl.when(jnp.logical_and(pl.program_id(0) == 0, h == 0))
            def _():
                for cp in _scatter_copies(p_refs, from_refs, *sems):
                    cp.start()

        @pl.when(h == 0)
        def _():
            db_ref[...] = jnp.zeros_like(db_ref)
            dgc_ref[...] = jnp.zeros_like(dgc_ref)
            dgt_ref[...] = jnp.zeros_like(dgt_ref)

        sls = [pl.ds(c * CHUNK, CHUNK) for c in range(gch)]
        known = [tm_ref[0, sl, :] for sl in sls]
        _, vjp = jax.vjp(lambda *a: _prep_chunks(*a, tmats=known)[0], *_prep_inputs(q_ref, k_ref, v_ref, b_ref, gc_ref, gt_ref, gcr_ref, sls, h))
        cots = tuple((dw_ref[sl, :], du_ref[sl, :], dqg_ref[sl, :], dkg_ref[sl, :], dat_ref[0, sl, :], dgl_ref[0, pl.ds(c * 8, 8), :]) for c, sl in enumerate(sls))
        dqs, dks, dvs, dbs, dgcs, dgts, dgcrs = vjp(cots)
        for c, sl in enumerate(sls):
            dq_ref[sl, :] = dqs[c]
            dk_ref[sl, :] = dks[c]
            dv_ref[sl, :] = dvs[c]
            db_ref[sl, :] += jnp.where(lane == h, dbs[c], 0.0)
            dgc_ref[sl, :] += jnp.where(lane == h + HEADS, dgcs[c], 0.0)
            dgt_ref[sl, :] += jnp.where(lane == h + HEADS, dgts[c], 0.0)
            dgcr_ref[0, c] = dgcrs[c]

        if n_sc:
            @pl.when(jnp.logical_and(pl.program_id(0) == steps - 1, h == HEADS - 1))
            def _():
                for cp in _scatter_copies(p_refs, from_refs, *sems):
                    cp.wait()

    hb, col, gcrow, square, glb = _prep_specs(rows, gch)
    wide = HEADS * HEAD_DIM
    res = pl.pallas_call(
        body,
        name="gdn_prep_bwd",
        grid=(steps, HEADS),
        in_specs=[hb, hb, hb, col, col, col, gcrow, square, hb, hb, hb, hb, square, glb] + [_HBM] * n_sc,
        out_specs=[hb, hb, hb, col, col, col, gcrow] + [_HBM] * n_sc,
        out_shape=[jax.ShapeDtypeStruct((t, wide), F32)] * 3 + [jax.ShapeDtypeStruct((t, LANES), F32)] * 3 + [jax.ShapeDtypeStruct((HEADS, t // CHUNK, 1, CHUNK), F32)]
        + [jax.ShapeDtypeStruct((3,) + p.shape[1:], p.dtype) for p in partials],
        scratch_shapes=[pltpu.SemaphoreType.DMA((3 * n_sc,))] * (2 if n_sc else 0),
        compiler_params=_params("arbitrary", "arbitrary"),
    )(q, k, v, beta, gc, gt, gcr, tmat, dw, du, dqg, dkg, dattn, dgl, *partials)
    return res[:7], res[7:]


def _gdn_scan(w, u, qg, kg, attn, gl):
    t = w.shape[0]
    n = t // CHUNK
    nch = min(SCAN_CHUNKS, n)
    wide = HEADS * HEAD_DIM

    def body(w_ref, u_ref, qg_ref, kg_ref, at_ref, gl_ref, o_ref, st_ref, s_ref):
        @pl.when(pl.program_id(0) == 0)
        def _():
            s_ref[...] = jnp.zeros_like(s_ref)

        heads = range(HEADS)
        cols = [pl.ds(h * HEAD_DIM, HEAD_DIM) for h in heads]
        for c in range(nch):
            rows, gl_rows = pl.ds(c * CHUNK, CHUNK), pl.ds(c * 8, 8)
            ss = [s_ref[h] for h in heads]
            sbs = [s.astype(BF16) for s in ss]
            vbs = [(u_ref[rows, hs] - _dot(w_ref[rows, hs], sb)).astype(BF16) for hs, sb in zip(cols, sbs)]
            outs = [_dot(qg_ref[rows, hs], sb) + _dot(at_ref[h, rows, :], vb) for h, hs, sb, vb in zip(heads, cols, sbs, vbs)]
            new = [s * jnp.tile(gl_ref[h, gl_rows, :], (HEAD_DIM // 8, 1)) + _dot(kg_ref[rows, hs], vb, TN) for h, hs, s, vb in zip(heads, cols, ss, vbs)]
            for h, hs in zip(heads, cols):
                st_ref[c, h] = ss[h]
                o_ref[rows, hs] = outs[h]
                s_ref[h] = new[h]

    row = pl.BlockSpec((nch * CHUNK, wide), lambda i: (i, 0))
    return pl.pallas_call(
        body,
        name="gdn_scan",
        grid=(n // nch,),
        in_specs=[row, row, row, row, pl.BlockSpec((HEADS, nch * CHUNK, CHUNK), lambda i: (0, i, 0)), pl.BlockSpec((HEADS, nch * 8, HEAD_DIM), lambda i: (0, i, 0))],
        out_specs=[row, pl.BlockSpec((nch, HEADS, HEAD_DIM, HEAD_DIM), lambda i: (i, 0, 0, 0))],
        out_shape=[jax.ShapeDtypeStruct((t, wide), F32), jax.ShapeDtypeStruct((n, HEADS, HEAD_DIM, HEAD_DIM), F32)],
        scratch_shapes=[pltpu.VMEM((HEADS, HEAD_DIM, HEAD_DIM), F32)],
        compiler_params=_params("arbitrary"),
    )(w, u, qg, kg, attn, gl)


def _gdn_scan_bwd(w, u, qg, kg, attn, gl, states, do):
    t = w.shape[0]
    n = t // CHUNK
    nch = min(SCAN_CHUNKS, n)
    steps = n // nch
    wide = HEADS * HEAD_DIM

    def body(w_ref, u_ref, qg_ref, kg_ref, at_ref, gl_ref, st_ref, do_ref, dw_ref, du_ref, dqg_ref, dkg_ref, dat_ref, dgl_ref, ds_ref):
        @pl.when(pl.program_id(0) == 0)
        def _():
            ds_ref[...] = jnp.zeros_like(ds_ref)

        heads = range(HEADS)
        cols = [pl.ds(h * HEAD_DIM, HEAD_DIM) for h in heads]
        for c in reversed(range(nch)):
            rows, gl_rows = pl.ds(c * CHUNK, CHUNK), pl.ds(c * 8, 8)
            ss = [st_ref[c, h] for h in heads]
            sbs = [s.astype(BF16) for s in ss]
            dsns = [ds_ref[h] for h in heads]
            dsbs = [d.astype(BF16) for d in dsns]
            dobs = [do_ref[rows, hs].astype(BF16) for hs in cols]
            vbs = [(u_ref[rows, hs] - _dot(w_ref[rows, hs], sb)).astype(BF16) for hs, sb in zip(cols, sbs)]
            dvns = [_dot(at_ref[h, rows, :], dob, TN) + _dot(kg_ref[rows, hs], dsb) for h, hs, dob, dsb in zip(heads, cols, dobs, dsbs)]
            dvbs = [d.astype(BF16) for d in dvns]
            for h, hs in zip(heads, cols):
                dat_ref[h, rows, :] = _dot(dobs[h], vbs[h], NT)
                dqg_ref[rows, hs] = _dot(dobs[h], sbs[h], NT)
                dkg_ref[rows, hs] = _dot(vbs[h], dsbs[h], NT)
                du_ref[rows, hs] = dvns[h]
                dw_ref[rows, hs] = -_dot(dvbs[h], sbs[h], NT)
                dgl_ref[h, gl_rows, :] = jnp.sum((dsns[h] * ss[h]).reshape(HEAD_DIM // 8, 8, HEAD_DIM), axis=0)
            new = [dsn * jnp.tile(gl_ref[h, gl_rows, :], (HEAD_DIM // 8, 1)) + _dot(qg_ref[rows, hs], dob, TN) - _dot(w_ref[rows, hs], dvb, TN)
                   for h, hs, dsn, dob, dvb in zip(heads, cols, dsns, dobs, dvbs)]
            for h in heads:
                ds_ref[h] = new[h]

    row = pl.BlockSpec((nch * CHUNK, wide), lambda i: (steps - 1 - i, 0))
    at = pl.BlockSpec((HEADS, nch * CHUNK, CHUNK), lambda i: (0, steps - 1 - i, 0))
    glb = pl.BlockSpec((HEADS, nch * 8, HEAD_DIM), lambda i: (0, steps - 1 - i, 0))
    return pl.pallas_call(
        body,
        name="gdn_scan_bwd",
        grid=(steps,),
        in_specs=[row, row, row, row, at, glb, pl.BlockSpec((nch, HEADS, HEAD_DIM, HEAD_DIM), lambda i: (steps - 1 - i, 0, 0, 0)), row],
        out_specs=[row, row, row, row, at, glb],
        out_shape=[jax.ShapeDtypeStruct((t, wide), F32)] * 4 + [jax.ShapeDtypeStruct((HEADS, t, CHUNK), F32), jax.ShapeDtypeStruct((HEADS, n * 8, HEAD_DIM), F32)],
        scratch_shapes=[pltpu.VMEM((HEADS, HEAD_DIM, HEAD_DIM), F32)],
        compiler_params=_params("arbitrary"),
    )(w, u, qg, kg, attn, gl, states, do)


SB_Q = 512
SB_K = 256
SB_STEP = 1
SB_DEAD = -105.0


def _sb_scores(q, k):
    z = _dot(q, k, NT) * (HEAD_DIM ** -0.5)
    e = jnp.exp(-jnp.abs(z))
    lb = jnp.minimum(z, 0.0) - jnp.log(1.0 + e)
    return z, e, lb, lb - z


def _tri(n, rel):
    return jnp.where(rel(_iota((n, n), 0), _iota((n, n), 1)), 1.0, 0.0).astype(BF16)


def _lanes(col):
    return jnp.broadcast_to(col, (col.shape[0], LANES))


def _sb_fwd(q, k, v):
    t = q.shape[0]
    bq, bk = min(SB_Q, t), min(SB_K, t)
    nsub, rep = bq // bk, bk // LANES
    nstep = min(SB_STEP, nsub)
    steps_per_tile = nsub // nstep

    def body(q_ref, k_ref, v_ref, o_ref, rt_ref, first_ref):
        h = pl.program_id(0)
        i = pl.program_id(1)
        o_ref[...] = jnp.zeros_like(o_ref)
        rt_ref[...] = jnp.zeros_like(rt_ref)
        after = _tri(bk, lambda r, c: r > c)

        def block(j, r0, diag):
            st = pl.multiple_of(j * bk, bk)
            kv, vv = k_ref[pl.ds(st, bk), :], v_ref[pl.ds(st, bk), :]
            _, _, lb, l1m = _sb_scores(q_ref[r0:, :], kv)
            if diag:
                mask = _iota((bq - r0, bk), 1) + j * bk < _iota((bq - r0, bk), 0) + (r0 + i * bq)
                l1m = jnp.where(mask, l1m, 0.0)
            sums = _two_pass(l1m, after)
            run = rt_ref[r0:, :]
            a = jnp.exp(lb + jnp.tile(run, (1, rep)) + sums)
            if diag:
                a = jnp.where(mask, a, 0.0)
            o_ref[r0:, :] += _dot(a.astype(BF16), vv)
            rt_ref[r0:, :] = run + _lanes(sums[:, 0:1] + l1m[:, 0:1])

        for s in reversed(range(nsub)):
            block(i * nsub + s, s * bk, True)

        def alive(carry):
            u, highest = carry
            return jnp.logical_and(u >= 0, highest > SB_DEAD)

        def step(carry):
            u, _ = carry
            for s in reversed(range(nstep)):
                block(u * nstep + s, 0, False)
            return u - 1, jnp.max(rt_ref[...])

        u_end, _ = lax.while_loop(alive, step, (i * steps_per_tile - 1, jnp.max(rt_ref[...])))
        first_ref[h, i] = u_end + 1

    qb = pl.BlockSpec((bq, HEAD_DIM), lambda h, i: (i, h))
    full = pl.BlockSpec((t, HEAD_DIM), lambda h, i: (0, h))
    return pl.pallas_call(
        body,
        name="sb_fwd",
        grid=(HEADS, t // bq),
        in_specs=[qb, full, full],
        out_specs=[qb, qb, pl.BlockSpec(memory_space=pltpu.SMEM)],
        out_shape=[jax.ShapeDtypeStruct(q.shape, F32), jax.ShapeDtypeStruct(q.shape, F32), jax.ShapeDtypeStruct((HEADS, t // bq), jnp.int32)],
        compiler_params=_params("arbitrary", "arbitrary"),
    )(q, k, v)


def _sb_bwd(q, k, v, rt, first, do):
    t = q.shape[0]
    bq, bk = min(SB_Q, t), min(SB_K, t)
    nsub, rep = bq // bk, bk // LANES
    nstep = min(SB_STEP, nsub)
    steps_per_tile = nsub // nstep
    scale = HEAD_DIM ** -0.5

    def body(first_ref, q_ref, k_ref, v_ref, rt_ref, do_ref, dq_ref, dk_ref, dv_ref, left_ref, pg_ref):
        h = pl.program_id(0)
        i = pl.program_id(1)

        @pl.when(i == 0)
        def _():
            dk_ref[...] = jnp.zeros_like(dk_ref)
            dv_ref[...] = jnp.zeros_like(dv_ref)

        dq_ref[...] = jnp.zeros_like(dq_ref)
        left_ref[...] = jnp.zeros_like(left_ref)
        pg_ref[...] = jnp.zeros_like(pg_ref)
        upto = _tri(bk, lambda r, c: r <= c)

        def block(j, r0, diag):
            st = pl.multiple_of(j * bk, bk)
            kv, vv = k_ref[pl.ds(st, bk), :], v_ref[pl.ds(st, bk), :]
            qv = q_ref[r0:, :]
            dob = do_ref[r0:, :].astype(BF16)
            _, _, lb, l1m = _sb_scores(qv, kv)
            if diag:
                mask = _iota((bq - r0, bk), 1) + j * bk < _iota((bq - r0, bk), 0) + (r0 + i * bq)
                l1m = jnp.where(mask, l1m, 0.0)
            sums = _two_pass(l1m, upto)
            left = left_ref[r0:, :]
            a = jnp.exp(lb + jnp.tile(rt_ref[r0:, :] - left, (1, rep)) - sums)
            if diag:
                a = jnp.where(mask, a, 0.0)
            g = _dot(dob, vv, NT) * a
            dv_ref[pl.ds(st, bk), :] += _dot(a.astype(BF16), dob, TN)
            gsum = _two_pass(g, upto)
            pg = pg_ref[r0:, :]
            dz = g - jnp.exp(lb) * (jnp.tile(pg, (1, rep)) + gsum)
            if diag:
                dz = jnp.where(mask, dz, 0.0)
            dzb = (dz * scale).astype(BF16)
            dk_ref[pl.ds(st, bk), :] += _dot(dzb, qv, TN)
            dq_ref[r0:, :] += _dot(dzb, kv)
            left_ref[r0:, :] = left + _lanes(sums[:, bk - 1:bk])
            pg_ref[r0:, :] = pg + _lanes(gsum[:, bk - 1:bk])

        def step(u, carry):
            for s in range(nstep):
                block(u * nstep + s, 0, False)
            return carry

        lax.fori_loop(first_ref[h, i], i * steps_per_tile, step, 0)
        for s in range(nsub):
            block(i * nsub + s, s * bk, True)

    qb = pl.BlockSpec((bq, HEAD_DIM), lambda h, i: (i, h))
    full = pl.BlockSpec((t, HEAD_DIM), lambda h, i: (0, h))
    return pl.pallas_call(
        body,
        name="sb_bwd",
        grid=(HEADS, t // bq),
        in_specs=[pl.BlockSpec(memory_space=pltpu.SMEM), qb, full, full, qb, qb],
        out_specs=[qb, full, full],
        out_shape=[jax.ShapeDtypeStruct(q.shape, F32)] * 3,
        scratch_shapes=[pltpu.VMEM((bq, LANES), F32), pltpu.VMEM((bq, LANES), F32)],
        compiler_params=_params("arbitrary", "arbitrary"),
    )(first, q, k, v, rt, do)


def _adamw(w, g, m, v, name, tm=256):
    r, c = w.shape
    tm = tm if r % tm == 0 else r

    def body(w_ref, g_ref, m_ref, v_ref, d_ref, nm_ref, nv_ref):
        gv = g_ref[...]
        nm = ADAM_B1 * m_ref[...] + (1.0 - ADAM_B1) * gv
        nv = ADAM_B2 * v_ref[...] + (1.0 - ADAM_B2) * (gv * gv)
        m_hat = nm / (1.0 - ADAM_B1 ** ADAM_STEP)
        v_hat = nv / (1.0 - ADAM_B2 ** ADAM_STEP)
        d_ref[...] = -ADAM_LR * (m_hat / (jnp.sqrt(v_hat) + ADAM_EPS) + ADAM_WD * w_ref[...])
        nm_ref[...] = nm
        nv_ref[...] = nv

    blk = pl.BlockSpec((tm, c), lambda i: (i, 0))
    return pl.pallas_call(
        body,
        name=name,
        grid=(r // tm,),
        in_specs=[blk] * 4,
        out_specs=[blk] * 3,
        out_shape=[jax.ShapeDtypeStruct((r, c), F32)] * 3,
        compiler_params=_params("parallel"),
    )(w, g, m, v)


def _local_step(x, tgt, gains, small, shards, assemble_first, assemble, early_reduce=None):
    mix_pre, mix_post, mlp_pre, mlp_post, kv_gain = gains
    a_log, dt_bias, out_gain = small
    t, d = x.shape
    row = lambda a, i=None: a[i:i + 1] if i is not None else a
    al = jnp.zeros((1, LANES), F32).at[:, HEADS:2 * HEADS].set(a_log)
    dtb = jnp.zeros((1, LANES), F32).at[:, HEADS:2 * HEADS].set(dt_bias)
    og = jnp.tile(out_gain, (1, HEADS))
    full = lambda a: (a, a.shape[1], 0)

    h0, *gathered_first = _rowwise("norm_in", _fn_norm, [full(x)], [row(mix_pre, 0)], [(d, BF16)], gather=shards[0])
    w_qkvg, w_ba, conv_w = assemble_first(gathered_first)
    qkvg = _matmul(h0, w_qkvg, "nn", F32, "mm_gdn_in", tk=1024)
    ba = _matmul(h0, w_ba, "nn", F32, "mm_gdn_ba", tk=1024)
    (conv, gq, gk, gv), gathered_conv = _conv_fwd(qkvg, conv_w, shards[1])
    beta, gc, gt = _rowwise("gates", _fn_gates, [full(ba)], [al, dtb], [(LANES, F32)] * 3)
    gcr = jnp.swapaxes(gc[:, HEADS:2 * HEADS], 0, 1).reshape(HEADS, t // CHUNK, 1, CHUNK)
    (pw, pu, pqg, pkg, pattn, pgl, ptm), gathered_prep = _gdn_prep(gq, gk, gv, beta, gc, gt, gcr, shards[2])
    w_out, w_kv, w_q, w_o, w_up, w_down = assemble(gathered_conv, gathered_prep)
    w_qkvg_t, w_up_t, w_down_t = (jnp.swapaxes(a, -1, -2) for a in (w_qkvg, w_up, w_down))
    o_gdn, states = _gdn_scan(pw, pu, pqg, pkg, pattn, pgl)
    (on,) = _rowwise("out_norm", _fn_outnorm, [full(o_gdn), (qkvg, d, 3)], [og], [(d, BF16)])
    mix0 = _matmul(on, w_out, "nn", F32, "mm_gdn_out", tk=1024)
    x1, h1 = _rowwise("res_a0", _fn_res_norm, [full(x), full(mix0)], [row(mix_post, 0), row(mlp_pre, 0)], [(d, F32), (d, BF16)])
    (a0,) = _matmul(h1, w_up[0], "nn", (BF16,), "mm_up0", tk=1024, epilogue=_relu2_of)
    d0 = _matmul(a0, w_down[0], "nn", F32, "mm_down0")
    x2, hkv, hq = _rowwise("res_b0", _fn_res_norm2, [full(x1), full(d0)], [row(mlp_post, 0), kv_gain, row(mix_pre, 1)], [(d, F32), (d, BF16), (d, BF16)])
    w_k, w_v = w_kv[:, :d], w_kv[:, d:]
    kp = _matmul(hkv, w_k, "nn", BF16, "mm_k", tk=1024)
    vp = _matmul(hkv, w_v, "nn", BF16, "mm_v", tk=1024)
    qp = _matmul(hq, w_q, "nn", BF16, "mm_q", tk=1024)
    o_sb, rt, sb_first = _sb_fwd(qp, kp, vp)
    mix1 = _matmul(o_sb, w_o, "nn", F32, "mm_sb_out", tk=1024)
    x3, h3 = _rowwise("res_a1", _fn_res_norm, [full(x2), full(mix1)], [row(mix_post, 1), row(mlp_pre, 1)], [(d, F32), (d, BF16)])
    (a1,) = _matmul(h3, w_up[1], "nn", (BF16,), "mm_up1", tk=1024, epilogue=_relu2_of)
    d1 = _matmul(a1, w_down[1], "nn", F32, "mm_down1")

    loss, dx3, dd1, g_mlp_post1 = _loss_call(x3, d1, tgt, row(mlp_post, 1))
    (du1,) = _matmul(dd1, w_down_t[1], "nn", (BF16,), "mm_down1_dx", epilogue=_relu2_cotangent, extras=[a1])
    g_down1 = _matmul(a1, dd1, "tn", F32, "mm_down1_dw")
    dh3 = _matmul(du1, w_up_t[1], "nn", F32, "mm_up1_dx")
    g_up1 = _matmul(h3, du1, "tn", F32, "mm_up1_dw")
    (dx2, dmix1), (g_mix_post1, g_mlp_pre1) = _rowwise_bwd(
        "res_a1_bwd", _fn_res_norm, [full(x2), full(mix1)], [row(mix_post, 1), row(mlp_pre, 1)], [dx3, dh3], [F32, BF16])
    do_sb = _matmul(dmix1, w_o, "nt", BF16, "mm_sb_out_dx")
    g_o = _matmul(o_sb, dmix1, "tn", F32, "mm_sb_out_dw")
    dqp, dkp, dvp = _sb_bwd(qp, kp, vp, rt, sb_first, do_sb)
    dhq = _matmul(dqp, w_q, "nt", F32, "mm_q_dx")
    g_q = _matmul(hq, dqp, "tn", F32, "mm_q_dw")
    dhkv = _matmul(dvp, w_v, "nt", F32, "mm_v_dx", add=_matmul(dkp, w_k, "nt", F32, "mm_k_dx"))
    g_kv = jnp.concatenate([_matmul(hkv, dkp, "tn", F32, "mm_k_dw"), _matmul(hkv, dvp, "tn", F32, "mm_v_dw")], axis=1)
    (dx1, dd0), (g_mlp_post0, g_kv_gain, g_mix_pre1) = _rowwise_bwd(
        "res_b0_bwd", _fn_res_norm2, [full(x1), full(d0)], [row(mlp_post, 0), kv_gain, row(mix_pre, 1)], [dx2, dhkv, dhq], [F32, BF16])
    (du0,) = _matmul(dd0, w_down_t[0], "nn", (BF16,), "mm_down0_dx", epilogue=_relu2_cotangent, extras=[a0])
    g_down0 = _matmul(a0, dd0, "tn", F32, "mm_down0_dw")
    dh1 = _matmul(du0, w_up_t[0], "nn", F32, "mm_up0_dx")
    g_up0 = _matmul(h1, du0, "tn", F32, "mm_up0_dw")
    (dx0, dmix0), (g_mix_post0, g_mlp_pre0) = _rowwise_bwd(
        "res_a0_bwd", _fn_res_norm, [full(x), full(mix0)], [row(mix_post, 0), row(mlp_pre, 0)], [dx1, dh1], [F32, BF16])
    don = _matmul(dmix0, w_out, "nt", F32, "mm_gdn_out_dx")
    g_out = _matmul(on, dmix0, "tn", F32, "mm_gdn_out_dw")
    (do_gdn, dgate), (g_og,) = _rowwise_bwd("out_norm_bwd", _fn_outnorm, [full(o_gdn), (qkvg, d, 3)], [og], [don], [F32, F32])
    dpw, dpu, dpqg, dpkg, dpattn, dpgl = _gdn_scan_bwd(pw, pu, pqg, pkg, pattn, pgl, states, do_gdn)
    partial, partial_bf16 = [], ()
    if early_reduce is not None:
        partial, partial_bf16 = early_reduce(dict(mlp_w_up=(g_up0, g_up1), mlp_w_down=(g_down0, g_down1), gdn_w_out=g_out[None], w_kv=g_kv, sb_w_q=g_q[None], sb_w_o=g_o[None]))
    (dgq, dgk, dgv, dbeta, dgc, dgt, dgcr), from_chips = _gdn_prep_bwd(gq, gk, gv, beta, gc, gt, gcr, ptm, dpw, dpu, dpqg, dpkg, dpattn, dpgl, partial_bf16)
    dgcr_lanes = jnp.pad(jnp.swapaxes(dgcr.reshape(HEADS, t), 0, 1), ((0, 0), (HEADS, LANES - 2 * HEADS)))
    gate_cots = [dbeta, dgc + dgcr_lanes, dgt]
    (dba,), (g_al, g_dtb) = _rowwise_bwd("gates_bwd", _fn_gates, [full(ba)], [al, dtb], gate_cots, [BF16])
    dqkvg, g_conv = _conv_bwd(conv, (dgq, dgk, dgv), dgate, qkvg, conv_w)
    dh0b = _matmul(dba, w_ba, "nt", F32, "mm_gdn_ba_dx", tk=LANES)
    dh0 = _matmul(dqkvg, w_qkvg_t, "nn", F32, "mm_gdn_in_dx", add=dh0b)
    g_qkvg = _matmul(h0, dqkvg, "tn", F32, "mm_gdn_in_dw")
    g_ba = _matmul(h0, dba, "tn", F32, "mm_gdn_ba_dw")
    (grad_x,), (g_mix_pre0,) = _rowwise_bwd("norm_in_bwd", lambda xx, gg: (_rms(xx, gg), xx), [full(x)], [row(mix_pre, 0)], [dh0, dx0], [F32])

    grads = dict(
        mix_pre_gain=jnp.concatenate([g_mix_pre0, g_mix_pre1], axis=0),
        mix_post_gain=jnp.concatenate([g_mix_post0, g_mix_post1], axis=0),
        mlp_pre_gain=jnp.concatenate([g_mlp_pre0, g_mlp_pre1], axis=0),
        mlp_post_gain=jnp.concatenate([g_mlp_post0, g_mlp_post1], axis=0),
        mlp_w_up=(g_up0, g_up1),
        mlp_w_down=(g_down0, g_down1),
        gdn_w_in=jnp.concatenate([g_qkvg, g_ba[:, :2 * HEADS]], axis=1)[None],
        gdn_conv_w=g_conv[None, :CONV_K],
        gdn_a_log=g_al[:, HEADS:2 * HEADS],
        gdn_dt_bias=g_dtb[:, HEADS:2 * HEADS],
        gdn_out_gain=jnp.sum(g_og.reshape(HEADS, HEAD_DIM), axis=0, keepdims=True),
        gdn_w_out=g_out[None],
        kv_gain=g_kv_gain[0],
        w_kv=g_kv,
        sb_w_q=g_q[None],
        sb_w_o=g_o[None],
    )
    return loss, grad_x, grads, (partial, from_chips)


N_DEV = 8
N_CHIPS = 4
PACK_ROW_TILE = 128

_HBM = pl.BlockSpec(memory_space=pltpu.HBM)


def _place():
    return lax.axis_index("x"), lax.axis_index("y"), lax.axis_index("c")


def _other_chips(x, y):
    return [(1 - x, y), (x, 1 - y), (1 - x, 1 - y)]


def _remote(src, dst, send_sem, recv_sem, to):
    return pltpu.make_async_remote_copy(src_ref=src, dst_ref=dst, send_sem=send_sem, recv_sem=recv_sem, device_id=to, device_id_type=MESH)


def _gather8(v, name):
    rows, cols = v.shape

    def body(v_ref, out_ref, sum_ref, send_sems, recv_sems, local_sem):
        x, y, c = _place()
        me, sibling = (x, y, c), (x, y, 1 - c)
        chips = _other_chips(x, y)

        def blk(px, py, pc):
            return out_ref.at[pl.ds((4 * px + 2 * py + pc) * rows, rows), :]

        def copy(k, block, to, src=None):
            return _remote(blk(*block) if src is None else src, blk(*block), send_sems.at[k], recv_sems.at[k], to)

        mine = pltpu.make_async_copy(v_ref, blk(*me), local_sem)
        mine.start()
        first = [copy(0, me, sibling, src=v_ref)] + [copy(1 + j, me, (*chip, c), src=v_ref) for j, chip in enumerate(chips)]
        for cp in first:
            cp.start()
        passed = [copy(4 + j, (*chip, c), sibling) for j, chip in enumerate(chips)]
        for j, chip in enumerate(chips):
            copy(1 + j, (*chip, c), me).wait_recv()
            passed[j].start()
        copy(0, sibling, me).wait_recv()
        for j, chip in enumerate(chips):
            copy(4 + j, (*chip, 1 - c), me).wait_recv()
        for cp in first + passed:
            cp.wait_send()
        mine.wait()
        acc = out_ref[pl.ds(0, rows), :]
        for dev in range(1, N_DEV):
            acc = acc + out_ref[pl.ds(dev * rows, rows), :]
        sum_ref[...] = acc

    vm = pl.BlockSpec(memory_space=pltpu.VMEM)
    return pl.pallas_call(
        body,
        name=name,
        out_shape=[jax.ShapeDtypeStruct((N_DEV * rows, cols), v.dtype), jax.ShapeDtypeStruct((rows, cols), v.dtype)],
        in_specs=[vm],
        out_specs=[vm, vm],
        scratch_shapes=[pltpu.SemaphoreType.DMA((7,)), pltpu.SemaphoreType.DMA((7,)), pltpu.SemaphoreType.DMA],
    )(v)


def _hbm_call(body, name, arrs, out_shapes, sem_counts):
    n = len(arrs)

    def wrapped(*refs):
        body(refs[:n], refs[n:2 * n], *refs[2 * n:])

    return pl.pallas_call(
        wrapped,
        name=name,
        out_shape=[jax.ShapeDtypeStruct(s, a.dtype) for s, a in zip(out_shapes, arrs)],
        in_specs=[_HBM] * n,
        out_specs=[_HBM] * n,
        scratch_shapes=[pltpu.SemaphoreType.DMA((k,)) for k in sem_counts],
    )(*arrs)


def _gather_sends(w_refs, out_refs, send_sems, recv_sems):
    x, y, c = _place()
    s_me = 2 * x + y
    return [_remote(w.at[c], o.at[s_me, c], send_sems.at[3 * a + j], recv_sems.at[3 * a + j], (px, py, c))
            for a, (w, o) in enumerate(zip(w_refs, out_refs)) for j, (px, py) in enumerate(_other_chips(x, y))]


def _gather_finish(w_refs, out_refs, send_sems, recv_sems, fsend_sems, frecv_sems):
    x, y, c = _place()
    chips = _other_chips(x, y)
    passed = []
    for a, o in enumerate(out_refs):
        for j, (px, py) in enumerate(chips):
            half = o.at[2 * px + py, c]
            _remote(half, half, send_sems.at[3 * a + j], recv_sems.at[3 * a + j], (px, py, c)).wait_recv()
            fwd = _remote(half, half, fsend_sems.at[3 * a + j], frecv_sems.at[3 * a + j], (x, y, 1 - c))
            fwd.start()
            passed.append(fwd)
    for a, o in enumerate(out_refs):
        for j, (px, py) in enumerate(chips):
            half = o.at[2 * px + py, 1 - c]
            _remote(half, half, fsend_sems.at[3 * a + j], frecv_sems.at[3 * a + j], (x, y, 1 - c)).wait_recv()
    for cp in _gather_sends(w_refs, out_refs, send_sems, recv_sems) + passed:
        cp.wait_send()


def _swap_halves(arrs, name):
    n = len(arrs)

    def body(g_refs, a_refs, send_sems, recv_sems):
        x, y, c = _place()
        cps = [_remote(g.at[1 - c], a, send_sems.at[i], recv_sems.at[i], (x, y, 1 - c)) for i, (g, a) in enumerate(zip(g_refs, a_refs))]
        for cp in cps:
            cp.start()
        for cp in cps:
            cp.wait()

    return _hbm_call(body, name, arrs, [a.shape[1:] for a in arrs], [n, n])


def _scatter_copies(p_refs, b_refs, send_sems, recv_sems):
    x, y, c = _place()
    return [_remote(p.at[2 * px + py], b.at[j], send_sems.at[3 * i + j], recv_sems.at[3 * i + j], (px, py, c))
            for i, (p, b) in enumerate(zip(p_refs, b_refs)) for j, (px, py) in enumerate(_other_chips(x, y))]


def _scatter_to_chips(arrs):
    n = len(arrs)

    def body(p_refs, b_refs, send_sems, recv_sems):
        cps = _scatter_copies(p_refs, b_refs, send_sems, recv_sems)
        for cp in cps:
            cp.start()
        for cp in cps:
            cp.wait()

    return _hbm_call(body, "grads_to_chips", arrs, [(3,) + a.shape[1:] for a in arrs], [3 * n, 3 * n])


def _share_halves(arrs):
    n = len(arrs)

    def body(q_refs, out_refs, send_sems, recv_sems):
        x, y, c = _place()
        cps = [_remote(q, o, send_sems.at[i], recv_sems.at[i], (x, y, 1 - c)) for i, (q, o) in enumerate(zip(q_refs, out_refs))]
        for cp in cps:
            cp.start()
        for cp in cps:
            cp.wait()

    return _hbm_call(body, "grads_share", arrs, [a.shape for a in arrs], [n, n])


_GROUPS = (
    (("gdn_w_out", (1, 256, 1024), "rows"), ("mlp_w_up", (2, 1024, 1024), "cols")),
    (("mlp_w_down", (2, 1024, 1024), "rows"), ("sb_w_q", (1, 256, 1024), "rows"), ("sb_w_o", (1, 256, 1024), "rows")),
    (("w_kv", (1024, 512), "cols"),),
    (("gdn_w_in", (1, 1024, 1028), "cols"),),
)
_BEHIND_CONV, _BEHIND_PREP, _FIRST = slice(0, 1), slice(1, 3), slice(3, 4)
_EARLY_GRADS = slice(0, 3)


def _numel(shape):
    n = 1
    for s in shape:
        n *= s
    return n


def _half_rows(shape):
    return _numel(shape[:-1]) // 2


def _pack_shards(shards, dtype):
    return tuple(jnp.concatenate([shards[n].astype(dtype).reshape(2, _half_rows(shape), shape[-1]) for n, shape, _ in grp], axis=1) for grp in _GROUPS)


def _unpack_shards(bufs):
    out = {}
    for grp, buf in zip(_GROUPS, bufs):
        off = 0
        for n, shape, _ in grp:
            out[n] = buf[:, off:off + _half_rows(shape)].reshape(shape)
            off += _half_rows(shape)
    return out


def _join(stacked, how):
    nd = stacked.ndim - 1
    ax = nd - 1 if how == "cols" else nd - 2
    moved = jnp.moveaxis(stacked, 0, ax)
    shape = list(stacked.shape[1:])
    shape[ax] *= N_CHIPS
    return moved.reshape(shape)


def _split(full, shard_shape, how):
    nd = len(shard_shape)
    ax = nd - 1 if how == "cols" else nd - 2
    shape = list(shard_shape)
    shape.insert(ax, N_CHIPS)
    return jnp.moveaxis(full.reshape(shape), ax, 0)


def _unpack_full(gathered, groups):
    out = {}
    for grp, buf in zip(groups, gathered):
        off = 0
        for n, shape, how in grp:
            out[n] = _join(buf[:, :, off:off + _half_rows(shape)].reshape((N_CHIPS,) + shape), how)
            off += _half_rows(shape)
    return out


def _pack_full(full, groups):
    bufs = []
    for grp in groups:
        parts = []
        for n, shape, how in grp:
            if isinstance(full[n], tuple):
                assert len(full[n]) == shape[0] == 2
                parts.append(jnp.stack([_split(layer, shape[1:], how) for layer in full[n]], axis=1))
            else:
                parts.append(_split(full[n], shape, how).reshape(N_CHIPS, 2, _half_rows(shape), shape[-1]))
        buf = jnp.swapaxes(jnp.concatenate(parts, axis=2), 0, 1)
        bufs.append(buf.reshape(2, -1, buf.shape[-1]))
    return tuple(bufs)


_SMALL = (
    ("mix_pre_gain", (2, 1024)),
    ("mix_post_gain", (2, 1024)),
    ("mlp_pre_gain", (2, 1024)),
    ("mlp_post_gain", (2, 1024)),
    ("kv_gain", (1024,)),
    ("gdn_out_gain", (1, 128)),
    ("gdn_a_log", (1, 8)),
    ("gdn_dt_bias", (1, 8)),
    ("gdn_conv_w", (1, 4, 3072)),
    ("loss", ()),
)


def _rows_of(shape):
    return -(-_numel(shape) // LANES)


_SMALL_ROWS = -(-sum(_rows_of(s) for _, s in _SMALL) // 8) * 8


def _pack_small(vals):
    parts = []
    for n, shape in _SMALL:
        flat = vals[n].reshape(-1)
        parts.append(jnp.pad(flat, (0, _rows_of(shape) * LANES - flat.shape[0])))
    flat = jnp.concatenate(parts)
    return jnp.pad(flat, (0, _SMALL_ROWS * LANES - flat.shape[0])).reshape(_SMALL_ROWS, LANES)


def _unpack_small(packed):
    flat = packed.reshape(-1)
    out, off = {}, 0
    for n, shape in _SMALL:
        out[n] = flat[off:off + _numel(shape)].reshape(shape)
        off += _rows_of(shape) * LANES
    return out


_WEIGHTS = ("mix_pre_gain", "mix_post_gain", "mlp_pre_gain", "mlp_post_gain", "mlp_w_up", "mlp_w_down", "gdn_w_in", "gdn_conv_w",
            "gdn_a_log", "gdn_dt_bias", "gdn_out_gain", "gdn_w_out", "kv_gain", "w_kv", "sb_w_q", "sb_w_o")


def _as2d(a):
    return a.reshape(1, -1) if a.ndim <= 1 else a.reshape(-1, a.shape[-1])


def kernel(x, mix_pre_gain, mix_post_gain, mlp_pre_gain, mlp_post_gain, mlp_w_up, mlp_w_down, gdn_w_in, gdn_conv_w, gdn_a_log, gdn_dt_bias, gdn_out_gain, gdn_w_out, kv_gain, w_kv, sb_w_q, sb_w_o, loss_target, m_mix_pre_gain, m_mix_post_gain, m_mlp_pre_gain, m_mlp_post_gain, m_mlp_w_up, m_mlp_w_down, m_gdn_w_in, m_gdn_conv_w, m_gdn_a_log, m_gdn_dt_bias, m_gdn_out_gain, m_gdn_w_out, m_kv_gain, m_w_kv, m_sb_w_q, m_sb_w_o, v_mix_pre_gain, v_mix_post_gain, v_mlp_pre_gain, v_mlp_post_gain, v_mlp_w_up, v_mlp_w_down, v_gdn_w_in, v_gdn_conv_w, v_gdn_a_log, v_gdn_dt_bias, v_gdn_out_gain, v_gdn_w_out, v_kv_gain, v_w_kv, v_sb_w_q, v_sb_w_o):
    w = dict(mix_pre_gain=mix_pre_gain, mix_post_gain=mix_post_gain, mlp_pre_gain=mlp_pre_gain, mlp_post_gain=mlp_post_gain, mlp_w_up=mlp_w_up, mlp_w_down=mlp_w_down, gdn_w_in=gdn_w_in, gdn_conv_w=gdn_conv_w, gdn_a_log=gdn_a_log, gdn_dt_bias=gdn_dt_bias, gdn_out_gain=gdn_out_gain, gdn_w_out=gdn_w_out, kv_gain=kv_gain, w_kv=w_kv, sb_w_q=sb_w_q, sb_w_o=sb_w_o)
    m = dict(mix_pre_gain=m_mix_pre_gain, mix_post_gain=m_mix_post_gain, mlp_pre_gain=m_mlp_pre_gain, mlp_post_gain=m_mlp_post_gain, mlp_w_up=m_mlp_w_up, mlp_w_down=m_mlp_w_down, gdn_w_in=m_gdn_w_in, gdn_conv_w=m_gdn_conv_w, gdn_a_log=m_gdn_a_log, gdn_dt_bias=m_gdn_dt_bias, gdn_out_gain=m_gdn_out_gain, gdn_w_out=m_gdn_w_out, kv_gain=m_kv_gain, w_kv=m_w_kv, sb_w_q=m_sb_w_q, sb_w_o=m_sb_w_o)
    v = dict(mix_pre_gain=v_mix_pre_gain, mix_post_gain=v_mix_post_gain, mlp_pre_gain=v_mlp_pre_gain, mlp_post_gain=v_mlp_post_gain, mlp_w_up=v_mlp_w_up, mlp_w_down=v_mlp_w_down, gdn_w_in=v_gdn_w_in, gdn_conv_w=v_gdn_conv_w, gdn_a_log=v_gdn_a_log, gdn_dt_bias=v_gdn_dt_bias, gdn_out_gain=v_gdn_out_gain, gdn_w_out=v_gdn_w_out, kv_gain=v_kv_gain, w_kv=v_w_kv, sb_w_q=v_sb_w_q, sb_w_o=v_sb_w_o)
    cx, cy, cc = _place()
    chip = 2 * cx + cy
    conv_cols = gdn_conv_w.shape[-1]

    own = _pack_shards(w, BF16)
    own_taps = jnp.pad(gdn_conv_w[0], ((0, CONV_K), (0, 0))).reshape(2, CONV_K, conv_cols)
    with_own = lambda gathered, mine: [lax.dynamic_update_index_in_dim(g, m, chip, 0) for g, m in zip(gathered, mine)]

    def assemble_first(gathered):
        w_in_all, taps_all = with_own(gathered, (*own[_FIRST], own_taps))
        w_in = _unpack_full([w_in_all], _GROUPS[_FIRST])["gdn_w_in"][0]
        taps = jnp.swapaxes(taps_all[:, 0], 0, 1).reshape(CONV_K, N_CHIPS * conv_cols)
        return w_in[:, :4 * HEADS * HEAD_DIM], jnp.pad(w_in[:, 4 * HEADS * HEAD_DIM:], ((0, 0), (0, LANES - 2 * HEADS))), taps

    def assemble(gathered_conv, gathered_prep):
        full = {**_unpack_full(with_own(gathered_conv, own[_BEHIND_CONV]), _GROUPS[_BEHIND_CONV]),
                **_unpack_full(with_own(gathered_prep, own[_BEHIND_PREP]), _GROUPS[_BEHIND_PREP])}
        return full["gdn_w_out"][0], full["w_kv"], full["sb_w_q"][0], full["sb_w_o"][0], full["mlp_w_up"], full["mlp_w_down"]

    gains = (mix_pre_gain, mix_post_gain, mlp_pre_gain, mlp_post_gain, kv_gain[None])
    small = (gdn_a_log, gdn_dt_bias, gdn_out_gain)
    tile = PACK_ROW_TILE

    def to_chip_partials(grads_full, groups, tag):
        bufs = _pack_full(grads_full, groups)
        p32, p16 = [], []
        for i, (buf, other) in enumerate(zip(bufs, _swap_halves(bufs, f"grads_to_sibling_{tag}"))):
            _, n, cols = buf.shape
            p, pb = _add_rows(f"grads_add_sibling_{tag}{i}", [(buf.reshape(2 * n, cols), cc * (n // tile)), (other, 0)], n, (F32, BF16), tile)
            p32.append(p.reshape(N_CHIPS, -1, cols))
            p16.append(pb.reshape(N_CHIPS, -1, cols))
        return p32, tuple(p16)

    loss_rows, grad_x, g_full, (partial_early, from_chips_early) = _local_step(
        x[0], loss_target[0], gains, small, ((*own[_FIRST], own_taps), own[_BEHIND_CONV], own[_BEHIND_PREP]), assemble_first, assemble,
        lambda g: to_chip_partials(g, _GROUPS[_EARLY_GRADS], "early"))

    partial_in, partial_in_bf16 = to_chip_partials(g_full, _GROUPS[_FIRST], "in")
    partial = list(partial_early) + partial_in
    from_chips = list(from_chips_early) + list(_scatter_to_chips(partial_in_bf16))
    reduced = []
    for i, (p, others) in enumerate(zip(partial, from_chips)):
        _, r, cols = p.shape
        terms = [(p.reshape(N_CHIPS * r, cols), chip * (r // tile))] + [(others.reshape(3 * r, cols), j * (r // tile)) for j in range(3)]
        reduced.append(_add_rows(f"grads_add_chips_{i}", terms, r, (F32,), tile)[0])
    g_shard = _unpack_shards([jnp.where(cc == 0, jnp.stack([r, o]), jnp.stack([o, r])) for r, o in zip(reduced, _share_halves(tuple(reduced)))])

    g_small_local = {n: g_full[n] for n, _ in _SMALL if n != "loss"}
    g_small_local["loss"] = loss_rows[0, 0]
    _, small_sum = _gather8(_pack_small(g_small_local), "allreduce_small")
    g_small = _unpack_small(small_sum)
    loss = g_small.pop("loss")
    g_small["gdn_conv_w"] = lax.dynamic_slice_in_dim(g_small["gdn_conv_w"], chip * conv_cols, conv_cols, axis=2)

    grads = {**g_shard, **g_small}
    deltas, new_m, new_v = {}, {}, {}
    for n in _WEIGHTS:
        d2, m2, v2 = _adamw(_as2d(w[n]), _as2d(grads[n]), _as2d(m[n]), _as2d(v[n]), "adamw_" + n)
        deltas[n], new_m[n], new_v[n] = d2.reshape(w[n].shape), m2.reshape(w[n].shape), v2.reshape(w[n].shape)
    return (loss, grad_x[None], *[grads[n].reshape(w[n].shape) for n in _WEIGHTS], *[deltas[n] for n in _WEIGHTS],
            *[new_m[n] for n in _WEIGHTS], *[new_v[n] for n in _WEIGHTS])
```

```python
import functools

import jax
import jax.numpy as jnp
from jax import lax
from jax.experimental import pallas as pl
from jax.experimental.pallas import tpu as pltpu

F32, BF16 = jnp.float32, jnp.bfloat16
HI = lax.Precision.HIGHEST
MESH = pl.DeviceIdType.MESH

EPS = 1e-6
D_MODEL = 1024
HEADS = 8
HEAD_DIM = 128
CHUNK = 64
CHUNK_SHIFT = CHUNK.bit_length() - 1
CONV_K = 4
D_FF = 4096
QKV = 3 * HEADS * HEAD_DIM

ADAM_LR, ADAM_B1, ADAM_B2, ADAM_EPS, ADAM_WD, ADAM_STEP = 0.001, 0.9, 0.999, 1e-08, 0.01, 10

VMEM_LIMIT_BYTES = 48 * 1024 * 1024
LANES = 128

NN = ((1,), (0,))
NT = ((1,), (1,))
TN = ((0,), (0,))


def _dot(a, b, dims=NN, precision=None):
    return lax.dot_general(a, b, (dims, ((), ())), precision=precision, preferred_element_type=F32)


def _params(*sem):
    return pltpu.CompilerParams(dimension_semantics=sem, vmem_limit_bytes=VMEM_LIMIT_BYTES)


def _iota(shape, axis):
    return lax.broadcasted_iota(jnp.int32, shape, axis)


def _matmul(a, b, mode, out_dtype, name, tm=1024, tn=1024, tk=2048, add=None, epilogue=None, extras=()):
    if mode == "nn":
        (m, k), (k2, n) = a.shape, b.shape
    elif mode == "nt":
        (m, k), (n, k2) = a.shape, b.shape
    else:
        (k, m), (k2, n) = a.shape, b.shape
    assert k == k2, (a.shape, b.shape, mode)
    tm, tn, tk = min(tm, m), min(tn, n), min(tk, k)
    assert m % tm == 0 and n % tn == 0 and k % tk == 0, (a.shape, b.shape, mode)
    nk = k // tk
    dims = {"nn": NN, "nt": NT, "tn": TN}[mode]
    tiles = ([add] if add is not None else []) + list(extras)
    out_dtypes = out_dtype if epilogue is not None else (out_dtype,)
    n_in = 2 + len(tiles)

    def finish(acc, extra_refs, o_refs):
        res = (acc,) if epilogue is None else epilogue(acc, *[r[...] for r in extra_refs])
        for o_ref, r in zip(o_refs, res):
            o_ref[...] = r.astype(o_ref.dtype)

    def body(*refs):
        a_ref, b_ref = refs[:2]
        extra_refs = refs[n_in - len(extras):n_in]
        o_refs, acc_ref = refs[n_in:-1], refs[-1]
        prod = _dot(a_ref[...].astype(BF16), b_ref[...].astype(BF16), dims)
        if nk == 1:
            finish(prod + refs[2][...].astype(F32) if add is not None else prod, extra_refs, o_refs)
            return
        kk = pl.program_id(2)

        @pl.when(kk == 0)
        def _():
            acc_ref[...] = refs[2][...].astype(F32) if add is not None else jnp.zeros_like(acc_ref)

        acc_ref[...] += prod

        @pl.when(kk == nk - 1)
        def _():
            finish(acc_ref[...], extra_refs, o_refs)

    a_spec = pl.BlockSpec((tk, tm), lambda i, j, kk: (kk, i)) if mode == "tn" else pl.BlockSpec((tm, tk), lambda i, j, kk: (i, kk))
    b_spec = pl.BlockSpec((tn, tk), lambda i, j, kk: (j, kk)) if mode == "nt" else pl.BlockSpec((tk, tn), lambda i, j, kk: (kk, j))
    o_spec = pl.BlockSpec((tm, tn), lambda i, j, kk: (i, j))
    res = pl.pallas_call(
        body,
        name=name,
        grid=(m // tm, n // tn, nk),
        in_specs=[a_spec, b_spec] + [o_spec] * len(tiles),
        out_specs=[o_spec] * len(out_dtypes),
        out_shape=[jax.ShapeDtypeStruct((m, n), dt) for dt in out_dtypes],
        scratch_shapes=[pltpu.VMEM((tm, tn), F32)],
        compiler_params=_params("parallel", "parallel", "arbitrary"),
    )(a, b, *tiles)
    return res if epilogue is not None else res[0]


def _row_specs(rows, tm):
    return [pl.BlockSpec((tm, w), lambda i, cb=cb: (i, cb)) for _, w, cb in rows]


def _full_spec(p):
    return pl.BlockSpec(p.shape, lambda i: (0,) * p.ndim)


def _rowwise(name, fn, rows, params, outs, tm=256, gather=()):
    t = rows[0][0].shape[0]
    tm = min(tm, t)
    steps = t // tm
    nr, npar, nout, ng = len(rows), len(params), len(outs), len(gather)

    def body(*refs):
        ins = [r[...].astype(F32) for r in refs[:nr]]
        ps = [p[...] for p in refs[nr:nr + npar]]
        shard_refs = refs[nr + npar:nr + npar + ng]
        o_refs = refs[nr + npar + ng:nr + npar + ng + nout]
        all_refs, sems = refs[nr + npar + ng + nout:nr + npar + 2 * ng + nout], refs[nr + npar + 2 * ng + nout:]
        if ng:
            @pl.when(pl.program_id(0) == 0)
            def _():
                for cp in _gather_sends(shard_refs, all_refs, *sems[:2]):
                    cp.start()

        res = fn(*ins, *ps)
        for o_ref, r in zip(o_refs, res):
            o_ref[...] = r.astype(o_ref.dtype)

        if ng:
            @pl.when(pl.program_id(0) == steps - 1)
            def _():
                _gather_finish(shard_refs, all_refs, *sems)

    return pl.pallas_call(
        body,
        name=name,
        grid=(steps,),
        in_specs=_row_specs(rows, tm) + [_full_spec(p) for p in params] + [_HBM] * ng,
        out_specs=[pl.BlockSpec((tm, w), lambda i: (i, 0)) for w, _ in outs] + [_HBM] * ng,
        out_shape=[jax.ShapeDtypeStruct((t, w), dt) for w, dt in outs] + [jax.ShapeDtypeStruct((N_CHIPS,) + s.shape, s.dtype) for s in gather],
        scratch_shapes=[pltpu.SemaphoreType.DMA((3 * ng,))] * (4 if ng else 0),
        compiler_params=_params("arbitrary" if ng else "parallel"),
    )(*[r[0] for r in rows], *params, *gather)


def _add_rows(name, terms, n_rows, out_dtypes, tm):
    cols = terms[0][0].shape[1]
    firsts = jnp.stack([jnp.asarray(first, jnp.int32) for _, first in terms])

    def body(firsts_ref, *refs):
        acc = refs[0][...].astype(F32)
        for r in refs[1:len(terms)]:
            acc = acc + r[...].astype(F32)
        for o_ref in refs[len(terms):]:
            o_ref[...] = acc.astype(o_ref.dtype)

    return pl.pallas_call(
        body,
        name=name,
        grid_spec=pltpu.PrefetchScalarGridSpec(
            num_scalar_prefetch=1,
            grid=(n_rows // tm,),
            in_specs=[pl.BlockSpec((tm, cols), lambda i, firsts_ref, k=k: (firsts_ref[k] + i, 0)) for k in range(len(terms))],
            out_specs=[pl.BlockSpec((tm, cols), lambda i, firsts_ref: (i, 0)) for _ in out_dtypes],
        ),
        out_shape=[jax.ShapeDtypeStruct((n_rows, cols), dt) for dt in out_dtypes],
        compiler_params=_params("parallel"),
    )(firsts, *[a for a, _ in terms])


def _rowwise_bwd(name, fn, rows, params, cots, grad_dtypes, tm=256, partials=()):
    t = rows[0][0].shape[0]
    tm = min(tm, t)
    steps = t // tm
    nr, npar, nc, nsc = len(rows), len(params), len(cots), len(partials)
    want = [j for j, dt in enumerate(grad_dtypes) if dt is not None]
    widths = [rows[j][1] for j in want]
    n_row_outs = len(want)
    n_in = nr + npar + nc

    def body(*refs):
        i = pl.program_id(0)
        ins = [r[...].astype(F32) for r in refs[:nr]]
        ps = [p[...] for p in refs[nr:nr + npar]]
        cs = tuple(c[...].astype(F32) for c in refs[nr + npar:n_in])
        p_refs = refs[n_in:n_in + nsc]
        outs = refs[n_in + nsc:n_in + nsc + n_row_outs + npar]
        from_refs, sems = refs[n_in + nsc + n_row_outs + npar:n_in + 2 * nsc + n_row_outs + npar], refs[n_in + 2 * nsc + n_row_outs + npar:]
        if nsc:
            @pl.when(i == 0)
            def _():
                for cp in _scatter_copies(p_refs, from_refs, *sems):
                    cp.start()

        _, vjp = jax.vjp(fn, *ins, *ps)
        gs = vjp(cs)
        for o_ref, j in zip(outs, want):
            o_ref[...] = gs[j].astype(o_ref.dtype)
        pg_refs = outs[n_row_outs:]

        @pl.when(i == 0)
        def _():
            for pg in pg_refs:
                pg[...] = jnp.zeros_like(pg)

        for pg, g in zip(pg_refs, gs[nr:]):
            pg[...] += g

        if nsc:
            @pl.when(i == steps - 1)
            def _():
                for cp in _scatter_copies(p_refs, from_refs, *sems):
                    cp.wait()

    row_specs = [pl.BlockSpec((tm, w), lambda i: (i, 0)) for w in widths]
    row_shapes = [jax.ShapeDtypeStruct((t, w), grad_dtypes[j]) for j, w in zip(want, widths)]
    res = pl.pallas_call(
        body,
        name=name,
        grid=(steps,),
        in_specs=_row_specs(rows, tm) + [_full_spec(p) for p in params] + [pl.BlockSpec((tm, c.shape[1]), lambda i: (i, 0)) for c in cots] + [_HBM] * nsc,
        out_specs=row_specs + [_full_spec(p) for p in params] + [_HBM] * nsc,
        out_shape=row_shapes + [jax.ShapeDtypeStruct(p.shape, F32) for p in params] + [jax.ShapeDtypeStruct((3,) + p.shape[1:], p.dtype) for p in partials],
        scratch_shapes=[pltpu.SemaphoreType.DMA((3 * nsc,))] * (2 if nsc else 0),
        compiler_params=_params("arbitrary"),
    )(*[r[0] for r in rows], *params, *cots, *partials)
    return res[:n_row_outs], res[n_row_outs:n_row_outs + npar], res[n_row_outs + npar:]


def _rms(x, g):
    return x * lax.rsqrt(jnp.mean(x * x, axis=-1, keepdims=True) + EPS) * g


def _sigmoid(x):
    return 1.0 / (1.0 + jnp.exp(-x))


def _softplus(x):
    return jnp.maximum(x, 0.0) + jnp.log1p(jnp.exp(-jnp.abs(x)))


def _two_pass(x, m):
    hi = x.astype(BF16)
    lo = (x - hi.astype(F32)).astype(BF16)
    return _dot(hi, m) + _dot(lo, m)


def _head_sum_impl(x):
    sums = [jnp.sum(x[:, h * HEAD_DIM:(h + 1) * HEAD_DIM], axis=-1, keepdims=True) for h in range(HEADS)]
    return jnp.concatenate([jnp.broadcast_to(s, (x.shape[0], HEAD_DIM)) for s in sums], axis=1)


@jax.custom_vjp
def _head_sum(x):
    return _head_sum_impl(x)


_head_sum.defvjp(lambda x: (_head_sum_impl(x), None), lambda _, g: (_head_sum_impl(g),))


def _fn_norm(x, g):
    return (_rms(x, g),)


def _fn_gates(ba, al, dt):
    col = _iota((1, LANES), 1)
    g = jnp.where((col >= HEADS) & (col < 2 * HEADS), -jnp.exp(al) * _softplus(ba + dt), 0.0)
    rows = ba.shape[0]
    r, c = _iota((rows, rows), 0), _iota((rows, rows), 1)
    same = (r >> CHUNK_SHIFT) == (c >> CHUNK_SHIFT)
    gc = _dot(jnp.where(same & (r >= c), 1.0, 0.0), g, precision=HI)
    gtot = _dot(jnp.where(same, 1.0, 0.0), g, precision=HI)
    return _sigmoid(ba), gc, gtot


def _fn_post_q(c):
    s = c * _sigmoid(c)
    return (s * lax.rsqrt(_head_sum(s * s) + EPS) * (HEAD_DIM ** -0.5),)


def _fn_post_k(c):
    s = c * _sigmoid(c)
    return (s * lax.rsqrt(_head_sum(s * s) + EPS),)


def _fn_post_v(c):
    return (c * _sigmoid(c),)


def _fn_post(cq, ck, cv):
    return _fn_post_q(cq) + _fn_post_k(ck) + _fn_post_v(cv)


def _fn_outnorm(o, gate, og):
    y = o * lax.rsqrt(_head_sum(o * o) * (1.0 / HEAD_DIM) + EPS) * og
    return (y * (gate * _sigmoid(gate)),)


def _fn_res_norm(x, m, gp, gn):
    x1 = x + _rms(m, gp)
    return x1, _rms(x1, gn)


def _fn_res_norm2(x, m, gp, ga, gb):
    x1 = x + _rms(m, gp)
    return x1, _rms(x1, ga), _rms(x1, gb)


def _relu2_of(u):
    r = jnp.maximum(u, 0.0)
    return (r * r,)


def _relu2_cotangent(da, a):
    return (da * (2.0 * jnp.sqrt(a.astype(F32))),)


def _loss_call(x3, d1, tgt, g, tm=256):
    t, d = x3.shape
    tm = min(tm, t)

    def body(x_ref, d_ref, t_ref, g_ref, loss_ref, dx_ref, dd_ref, dg_ref):
        i = pl.program_id(0)
        y, vjp = jax.vjp(lambda x, dd, gg: x + _rms(dd, gg), x_ref[...], d_ref[...], g_ref[...])
        err = y - t_ref[...]
        lrow = 0.5 * jnp.mean(err * err, axis=-1, keepdims=True)
        dx, dd, dg = vjp(err * (1.0 / d))
        dx_ref[...] = dx
        dd_ref[...] = dd.astype(dd_ref.dtype)

        @pl.when(i == 0)
        def _():
            loss_ref[...] = jnp.zeros_like(loss_ref)
            dg_ref[...] = jnp.zeros_like(dg_ref)

        loss_ref[...] += jnp.broadcast_to(jnp.sum(lrow, axis=0, keepdims=True), loss_ref.shape)
        dg_ref[...] += dg

    row = pl.BlockSpec((tm, d), lambda i: (i, 0))
    return pl.pallas_call(
        body,
        name="loss_head",
        grid=(t // tm,),
        in_specs=[row, row, row, _full_spec(g)],
        out_specs=[pl.BlockSpec((8, LANES), lambda i: (0, 0)), row, row, _full_spec(g)],
        out_shape=[jax.ShapeDtypeStruct((8, LANES), F32), jax.ShapeDtypeStruct((t, d), F32), jax.ShapeDtypeStruct((t, d), BF16), jax.ShapeDtypeStruct(g.shape, F32)],
        compiler_params=_params("arbitrary"),
    )(x3, d1, tgt, g)


HALO = 8


def _conv_fwd(qkvg, conv_w, shards, tm=256):
    t = qkvg.shape[0]
    tm = min(tm, t)
    steps = t // tm
    wide = QKV // 3
    n = len(shards)

    def body(*refs):
        cur_ref, prev_ref, w_ref = refs[:3]
        shard_refs = refs[3:3 + n]
        o_ref, q_ref, k_ref, v_ref = refs[3 + n:7 + n]
        all_refs = refs[7 + n:7 + 2 * n]
        buf, sems = refs[7 + 2 * n], refs[8 + 2 * n:]
        i = pl.program_id(0)

        if n:
            @pl.when(i == 0)
            def _():
                for cp in _gather_sends(shard_refs, all_refs, *sems[:2]):
                    cp.start()

        buf[0:HALO, :] = jnp.where(i > 0, prev_ref[...], 0.0)
        buf[HALO:, :] = cur_ref[...]
        acc = buf[pl.ds(HALO - CONV_K + 1, tm), :] * w_ref[pl.ds(0, 1), :]
        for j in range(1, CONV_K):
            acc = acc + buf[pl.ds(HALO - CONV_K + 1 + j, tm), :] * w_ref[pl.ds(j, 1), :]
        o_ref[...] = acc
        (q_ref[...], k_ref[...], v_ref[...]) = _fn_post(acc[:, 0:wide], acc[:, wide:2 * wide], acc[:, 2 * wide:])

        if n:
            @pl.when(i == steps - 1)
            def _():
                _gather_finish(shard_refs, all_refs, *sems)

    part = pl.BlockSpec((tm, wide), lambda i: (i, 0))
    res = pl.pallas_call(
        body,
        name="conv_fwd",
        grid=(steps,),
        in_specs=[
            pl.BlockSpec((tm, QKV), lambda i: (i, 0)),
            pl.BlockSpec((HALO, QKV), lambda i: (jnp.maximum(i * (tm // HALO) - 1, 0), 0)),
            pl.BlockSpec((CONV_K, QKV), lambda i: (0, 0)),
        ] + [_HBM] * n,
        out_specs=[pl.BlockSpec((tm, QKV), lambda i: (i, 0)), part, part, part] + [_HBM] * n,
        out_shape=[jax.ShapeDtypeStruct((t, QKV), F32)] + [jax.ShapeDtypeStruct((t, wide), F32)] * 3
        + [jax.ShapeDtypeStruct((N_CHIPS,) + s.shape, s.dtype) for s in shards],
        scratch_shapes=[pltpu.VMEM((tm + HALO, QKV), F32)] + [pltpu.SemaphoreType.DMA((3 * n,))] * (4 if n else 0),
        compiler_params=_params("arbitrary"),
    )(qkvg, qkvg, conv_w, *shards)
    return res[:4], res[4:]


def _conv_bwd(conv, dqkv, dgate, qkvg, conv_w, tm=256):
    t = conv.shape[0]
    tm = min(tm, t)
    n = t // tm
    wg = dgate.shape[1]
    wide = QKV // 3

    def conv_cotangent(c_ref, g_refs):
        parts = [c_ref[:, j * wide:(j + 1) * wide] for j in range(3)]
        _, vjp = jax.vjp(_fn_post, *parts)
        return vjp(tuple(g[...] for g in g_refs))

    def body(c_ref, cn_ref, dq_ref, dk_ref, dv_ref, dqn_ref, dkn_ref, dvn_ref, dgate_ref, x_ref, xp_ref, w_ref, dx_ref, dw_ref, bufd, bufx):
        i = pl.program_id(0)
        for j, (cur, nxt) in enumerate(zip(conv_cotangent(c_ref, (dq_ref, dk_ref, dv_ref)), conv_cotangent(cn_ref, (dqn_ref, dkn_ref, dvn_ref)))):
            bufd[0:tm, j * wide:(j + 1) * wide] = cur
            bufd[tm:, j * wide:(j + 1) * wide] = jnp.where(i < n - 1, nxt, 0.0)
        bufx[0:HALO, :] = jnp.where(i > 0, xp_ref[...], 0.0)
        bufx[HALO:, :] = x_ref[...]

        @pl.when(i == 0)
        def _():
            dw_ref[...] = jnp.zeros_like(dw_ref)

        dcv = bufd[0:tm, :]
        acc = bufd[pl.ds(CONV_K - 1, tm), :] * w_ref[pl.ds(0, 1), :]
        for j in range(1, CONV_K):
            acc = acc + bufd[pl.ds(CONV_K - 1 - j, tm), :] * w_ref[pl.ds(j, 1), :]
        dx_ref[:, 0:QKV] = acc.astype(dx_ref.dtype)
        dx_ref[:, QKV:] = dgate_ref[...].astype(dx_ref.dtype)
        for j in range(CONV_K):
            dw_ref[pl.ds(j, 1), :] += jnp.sum(dcv * bufx[pl.ds(HALO - CONV_K + 1 + j, tm), :], axis=0, keepdims=True)

    def cur(width):
        return pl.BlockSpec((tm, width), lambda i: (i, 0))

    def nxt(width):
        return pl.BlockSpec((HALO, width), lambda i: (jnp.minimum((i + 1) * (tm // HALO), t // HALO - 1), 0))

    return pl.pallas_call(
        body,
        name="conv_bwd",
        grid=(n,),
        in_specs=[cur(QKV), nxt(QKV)] + [cur(wide)] * 3 + [nxt(wide)] * 3 + [
            cur(wg),
            cur(QKV),
            pl.BlockSpec((HALO, QKV), lambda i: (jnp.maximum(i * (tm // HALO) - 1, 0), 0)),
            pl.BlockSpec((CONV_K, QKV), lambda i: (0, 0)),
        ],
        out_specs=[pl.BlockSpec((tm, QKV + wg), lambda i: (i, 0)), pl.BlockSpec((HALO, QKV), lambda i: (0, 0))],
        out_shape=[jax.ShapeDtypeStruct((t, QKV + wg), BF16), jax.ShapeDtypeStruct((HALO, QKV), F32)],
        scratch_shapes=[pltpu.VMEM((tm + HALO, QKV), F32), pltpu.VMEM((tm + HALO, QKV), F32)],
        compiler_params=_params("arbitrary"),
    )(conv, conv, *dqkv, *dqkv, dgate, qkvg, qkvg, conv_w)


PREP_CHUNKS = 16
PREP_BWD_CHUNKS = 4
SCAN_CHUNKS = 4


def _hi_lo(x):
    hi = x.astype(BF16)
    return hi, (x - hi.astype(F32)).astype(BF16)


def _mm3(a, b, dims=NN):
    (ah, al), (bh, bl) = _hi_lo(a), _hi_lo(b)
    return _dot(ah, bh, dims) + (_dot(ah, bl, dims) + _dot(al, bh, dims))


def _neumann(lowers):
    c = lowers[0].shape[0]
    eye = jnp.where(_iota((c, c), 0) == _iota((c, c), 1), 1.0, 0.0)
    ps = [-low for low in lowers]
    tmats = [eye + p for p in ps]
    for _ in range(CHUNK_SHIFT - 1):
        ps = [_mm3(p, p) for p in ps]
        tmats = [t + _mm3(t, p) for t, p in zip(tmats, ps)]
    return tuple(tmats)


def _inv_cotangents(tmats, dts):
    half = [_mm3(t, dt, TN) for t, dt in zip(tmats, dts)]
    return tuple(-_mm3(hf, t, NT) for hf, t in zip(half, tmats))


@jax.custom_vjp
def _tri_inv(lowers):
    return _neumann(lowers)


def _tri_inv_fwd(lowers):
    tmats = _neumann(lowers)
    return tmats, tmats


_tri_inv.defvjp(_tri_inv_fwd, lambda tmats, dts: (_inv_cotangents(tmats, dts),))


@jax.custom_vjp
def _tri_inv_known(lowers, tmats):
    return tmats


_tri_inv_known.defvjp(lambda lowers, tmats: (tmats, tmats),
                      lambda tmats, dts: (_inv_cotangents(tmats, dts), tuple(jnp.zeros_like(t) for t in tmats)))


def _prep_chunks(qs, ks, vs, bs, gcs, gts, gcrs, tmats=None):
    c = CHUNK
    r, col = _iota((c, c), 0), _iota((c, c), 1)
    incl, strict = r >= col, r > col
    decays = [jnp.where(incl, jnp.exp(jnp.where(incl, gc - gcr, 0.0)), 0.0) for gc, gcr in zip(gcs, gcrs)]
    kbs = [k * b for k, b in zip(ks, bs)]
    kbfs = [k.astype(BF16) for k in ks]
    lowers = tuple(jnp.where(strict, _dot(kb.astype(BF16), kbf, NT) * decay, 0.0) for kb, kbf, decay in zip(kbs, kbfs, decays))
    tmats = _tri_inv(lowers) if tmats is None else _tri_inv_known(lowers, tuple(tmats))
    outs = []
    for q, k, v, b, gc, gt, kb, kbf, decay, tmat in zip(qs, ks, vs, bs, gcs, gts, kbs, kbfs, decays, tmats):
        tb = tmat.astype(BF16)
        egc = jnp.exp(gc)
        w = _dot(tb, (kb * egc).astype(BF16))
        u = _dot(tb, (v * b).astype(BF16))
        attn = _dot(q.astype(BF16), kbf, NT) * decay
        gl = jnp.broadcast_to(jnp.exp(jnp.mean(gt.reshape(c // 8, 8, 1), axis=0)), (8, HEAD_DIM))
        outs.append((w, u, q * egc, k * jnp.exp(gt - gc), attn, gl))
    return tuple(outs), tmats


def _prep_specs(rows, gch):
    head = pl.BlockSpec((rows, HEAD_DIM), lambda n, h: (n, h))
    gates = pl.BlockSpec((rows, LANES), lambda n, h: (n, 0))
    gcrow = pl.BlockSpec((1, gch, 1, CHUNK), lambda n, h: (h, n, 0, 0))
    square = pl.BlockSpec((1, rows, CHUNK), lambda n, h: (h, n, 0))
    gl = pl.BlockSpec((1, gch * 8, HEAD_DIM), lambda n, h: (h, n, 0))
    return head, gates, gcrow, square, gl


def _pick_lane(ref, sl, lane):
    return jnp.sum(jnp.where(_iota((1, LANES), 1) == lane, ref[sl, :], 0.0), axis=1, keepdims=True)


def _prep_inputs(q_ref, k_ref, v_ref, b_ref, gc_ref, gt_ref, gcr_ref, sls, h):
    return ([q_ref[sl, :] for sl in sls], [k_ref[sl, :] for sl in sls], [v_ref[sl, :] for sl in sls],
            [_pick_lane(b_ref, sl, h) for sl in sls], [_pick_lane(gc_ref, sl, h + HEADS) for sl in sls],
            [_pick_lane(gt_ref, sl, h + HEADS) for sl in sls], [gcr_ref[0, c] for c in range(len(sls))])


def _gdn_prep(q, k, v, beta, gc, gt, gcr, shards=()):
    t = q.shape[0]
    gch = min(PREP_CHUNKS, t // CHUNK)
    rows = gch * CHUNK
    steps = t // rows
    n = len(shards)

    def body(*refs):
        q_ref, k_ref, v_ref, b_ref, gc_ref, gt_ref, gcr_ref = refs[:7]
        shard_refs = refs[7:7 + n]
        w_ref, u_ref, qg_ref, kg_ref, at_ref, gl_ref, tm_ref = refs[7 + n:14 + n]
        all_refs, sems = refs[14 + n:14 + 2 * n], refs[14 + 2 * n:]
        h = pl.program_id(1)

        if n:
            @pl.when(jnp.logical_and(pl.program_id(0) == 0, h == 0))
            def _():
                for cp in _gather_sends(shard_refs, all_refs, *sems[:2]):
                    cp.start()

        sls = [pl.ds(c * CHUNK, CHUNK) for c in range(gch)]
        outs, tmats = _prep_chunks(*_prep_inputs(q_ref, k_ref, v_ref, b_ref, gc_ref, gt_ref, gcr_ref, sls, h))
        for c, (sl, (w, u, qg, kg, attn, gl), tmat) in enumerate(zip(sls, outs, tmats)):
            w_ref[sl, :] = w.astype(BF16)
            u_ref[sl, :] = u
            qg_ref[sl, :] = qg.astype(BF16)
            kg_ref[sl, :] = kg.astype(BF16)
            at_ref[0, sl, :] = attn.astype(BF16)
            gl_ref[0, pl.ds(c * 8, 8), :] = gl
            tm_ref[0, sl, :] = tmat

        if n:
            @pl.when(jnp.logical_and(pl.program_id(0) == steps - 1, h == HEADS - 1))
            def _():
                _gather_finish(shard_refs, all_refs, *sems)

    hb, col, gcrow, square, glb = _prep_specs(rows, gch)
    wide = HEADS * HEAD_DIM
    res = pl.pallas_call(
        body,
        name="gdn_prep",
        grid=(steps, HEADS),
        in_specs=[hb, hb, hb, col, col, col, gcrow] + [_HBM] * n,
        out_specs=[hb, hb, hb, hb, square, glb, square] + [_HBM] * n,
        out_shape=[
            jax.ShapeDtypeStruct((t, wide), BF16),
            jax.ShapeDtypeStruct((t, wide), F32),
            jax.ShapeDtypeStruct((t, wide), BF16),
            jax.ShapeDtypeStruct((t, wide), BF16),
            jax.ShapeDtypeStruct((HEADS, t, CHUNK), BF16),
            jax.ShapeDtypeStruct((HEADS, t // CHUNK * 8, HEAD_DIM), F32),
            jax.ShapeDtypeStruct((HEADS, t, CHUNK), F32),
        ] + [jax.ShapeDtypeStruct((N_CHIPS,) + s.shape, s.dtype) for s in shards],
        scratch_shapes=[pltpu.SemaphoreType.DMA((3 * n,))] * (4 if n else 0),
        compiler_params=_params("arbitrary", "arbitrary") if n else _params("parallel", "parallel"),
    )(q, k, v, beta, gc, gt, gcr, *shards)
    return res[:7], res[7:]


def _gdn_prep_bwd(q, k, v, beta, gc, gt, gcr, tmat, dw, du, dqg, dkg, dattn, dgl, partials=()):
    t = q.shape[0]
    gch = min(PREP_BWD_CHUNKS, t // CHUNK)
    rows = gch * CHUNK
    steps = t // rows
    n_sc = len(partials)

    def body(*refs):
        (q_ref, k_ref, v_ref, b_ref, gc_ref, gt_ref, gcr_ref, tm_ref, dw_ref, du_ref, dqg_ref, dkg_ref, dat_ref, dgl_ref) = refs[:14]
        p_refs = refs[14:14 + n_sc]
        dq_ref, dk_ref, dv_ref, db_ref, dgc_ref, dgt_ref, dgcr_ref = refs[14 + n_sc:21 + n_sc]
        from_refs, sems = refs[21 + n_sc:21 + 2 * n_sc], refs[21 + 2 * n_sc:]
        h = pl.program_id(1)
        lane = _iota((1, LANES), 1)

        if n_sc:
            @pl.when(jnp.logical_and(pl.program_id(0) == 0, h == 0))
            def _():
                for cp in _scatter_copies(p_refs, from_refs, *sems):
                    cp.start()

        @pl.when(h == 0)
        def _():
            db_ref[...] = jnp.zeros_like(db_ref)
            dgc_ref[...] = jnp.zeros_like(dgc_ref)
            dgt_ref[...] = jnp.zeros_like(dgt_ref)

        sls = [pl.ds(c * CHUNK, CHUNK) for c in range(gch)]
        known = [tm_ref[0, sl, :] for sl in sls]
        _, vjp = jax.vjp(lambda *a: _prep_chunks(*a, tmats=known)[0], *_prep_inputs(q_ref, k_ref, v_ref, b_ref, gc_ref, gt_ref, gcr_ref, sls, h))
        cots = tuple((dw_ref[sl, :], du_ref[sl, :], dqg_ref[sl, :], dkg_ref[sl, :], dat_ref[0, sl, :], dgl_ref[0, pl.ds(c * 8, 8), :]) for c, sl in enumerate(sls))
        dqs, dks, dvs, dbs, dgcs, dgts, dgcrs = vjp(cots)
        for c, sl in enumerate(sls):
            dq_ref[sl, :] = dqs[c]
            dk_ref[sl, :] = dks[c]
            dv_ref[sl, :] = dvs[c]
            db_ref[sl, :] += jnp.where(lane == h, dbs[c], 0.0)
            dgc_ref[sl, :] += jnp.where(lane == h + HEADS, dgcs[c], 0.0)
            dgt_ref[sl, :] += jnp.where(lane == h + HEADS, dgts[c], 0.0)
            dgcr_ref[0, c] = dgcrs[c]

        if n_sc:
            @pl.when(jnp.logical_and(pl.program_id(0) == steps - 1, h == HEADS - 1))
            def _():
                for cp in _scatter_copies(p_refs, from_refs, *sems):
                    cp.wait()

    hb, col, gcrow, square, glb = _prep_specs(rows, gch)
    wide = HEADS * HEAD_DIM
    res = pl.pallas_call(
        body,
        name="gdn_prep_bwd",
        grid=(steps, HEADS),
        in_specs=[hb, hb, hb, col, col, col, gcrow, square, hb, hb, hb, hb, square, glb] + [_HBM] * n_sc,
        out_specs=[hb, hb, hb, col, col, col, gcrow] + [_HBM] * n_sc,
        out_shape=[jax.ShapeDtypeStruct((t, wide), F32)] * 3 + [jax.ShapeDtypeStruct((t, LANES), F32)] * 3 + [jax.ShapeDtypeStruct((HEADS, t // CHUNK, 1, CHUNK), F32)]
        + [jax.ShapeDtypeStruct((3,) + p.shape[1:], p.dtype) for p in partials],
        scratch_shapes=[pltpu.SemaphoreType.DMA((3 * n_sc,))] * (2 if n_sc else 0),
        compiler_params=_params("arbitrary", "arbitrary"),
    )(q, k, v, beta, gc, gt, gcr, tmat, dw, du, dqg, dkg, dattn, dgl, *partials)
    return res[:7], res[7:]


def _gdn_scan(w, u, qg, kg, attn, gl):
    t = w.shape[0]
    n = t // CHUNK
    nch = min(SCAN_CHUNKS, n)
    wide = HEADS * HEAD_DIM

    def body(w_ref, u_ref, qg_ref, kg_ref, at_ref, gl_ref, o_ref, st_ref, s_ref):
        @pl.when(pl.program_id(0) == 0)
        def _():
            s_ref[...] = jnp.zeros_like(s_ref)

        heads = range(HEADS)
        cols = [pl.ds(h * HEAD_DIM, HEAD_DIM) for h in heads]
        for c in range(nch):
            rows, gl_rows = pl.ds(c * CHUNK, CHUNK), pl.ds(c * 8, 8)
            ss = [s_ref[h] for h in heads]
            sbs = [s.astype(BF16) for s in ss]
            vbs = [(u_ref[rows, hs] - _dot(w_ref[rows, hs], sb)).astype(BF16) for hs, sb in zip(cols, sbs)]
            outs = [_dot(qg_ref[rows, hs], sb) + _dot(at_ref[h, rows, :], vb) for h, hs, sb, vb in zip(heads, cols, sbs, vbs)]
            new = [s * jnp.tile(gl_ref[h, gl_rows, :], (HEAD_DIM // 8, 1)) + _dot(kg_ref[rows, hs], vb, TN) for h, hs, s, vb in zip(heads, cols, ss, vbs)]
            for h, hs in zip(heads, cols):
                st_ref[c, h] = ss[h]
                o_ref[rows, hs] = outs[h]
                s_ref[h] = new[h]

    row = pl.BlockSpec((nch * CHUNK, wide), lambda i: (i, 0))
    return pl.pallas_call(
        body,
        name="gdn_scan",
        grid=(n // nch,),
        in_specs=[row, row, row, row, pl.BlockSpec((HEADS, nch * CHUNK, CHUNK), lambda i: (0, i, 0)), pl.BlockSpec((HEADS, nch * 8, HEAD_DIM), lambda i: (0, i, 0))],
        out_specs=[row, pl.BlockSpec((nch, HEADS, HEAD_DIM, HEAD_DIM), lambda i: (i, 0, 0, 0))],
        out_shape=[jax.ShapeDtypeStruct((t, wide), F32), jax.ShapeDtypeStruct((n, HEADS, HEAD_DIM, HEAD_DIM), F32)],
        scratch_shapes=[pltpu.VMEM((HEADS, HEAD_DIM, HEAD_DIM), F32)],
        compiler_params=_params("arbitrary"),
    )(w, u, qg, kg, attn, gl)


def _gdn_scan_bwd(w, u, qg, kg, attn, gl, states, do):
    t = w.shape[0]
    n = t // CHUNK
    nch = min(SCAN_CHUNKS, n)
    steps = n // nch
    wide = HEADS * HEAD_DIM

    def body(w_ref, u_ref, qg_ref, kg_ref, at_ref, gl_ref, st_ref, do_ref, dw_ref, du_ref, dqg_ref, dkg_ref, dat_ref, dgl_ref, ds_ref):
        @pl.when(pl.program_id(0) == 0)
        def _():
            ds_ref[...] = jnp.zeros_like(ds_ref)

        heads = range(HEADS)
        cols = [pl.ds(h * HEAD_DIM, HEAD_DIM) for h in heads]
        for c in reversed(range(nch)):
            rows, gl_rows = pl.ds(c * CHUNK, CHUNK), pl.ds(c * 8, 8)
            ss = [st_ref[c, h] for h in heads]
            sbs = [s.astype(BF16) for s in ss]
            dsns = [ds_ref[h] for h in heads]
            dsbs = [d.astype(BF16) for d in dsns]
            dobs = [do_ref[rows, hs].astype(BF16) for hs in cols]
            vbs = [(u_ref[rows, hs] - _dot(w_ref[rows, hs], sb)).astype(BF16) for hs, sb in zip(cols, sbs)]
            dvns = [_dot(at_ref[h, rows, :], dob, TN) + _dot(kg_ref[rows, hs], dsb) for h, hs, dob, dsb in zip(heads, cols, dobs, dsbs)]
            dvbs = [d.astype(BF16) for d in dvns]
            for h, hs in zip(heads, cols):
                dat_ref[h, rows, :] = _dot(dobs[h], vbs[h], NT)
                dqg_ref[rows, hs] = _dot(dobs[h], sbs[h], NT)
                dkg_ref[rows, hs] = _dot(vbs[h], dsbs[h], NT)
                du_ref[rows, hs] = dvns[h]
                dw_ref[rows, hs] = -_dot(dvbs[h], sbs[h], NT)
                dgl_ref[h, gl_rows, :] = jnp.sum((dsns[h] * ss[h]).reshape(HEAD_DIM // 8, 8, HEAD_DIM), axis=0)
            new = [dsn * jnp.tile(gl_ref[h, gl_rows, :], (HEAD_DIM // 8, 1)) + _dot(qg_ref[rows, hs], dob, TN) - _dot(w_ref[rows, hs], dvb, TN)
                   for h, hs, dsn, dob, dvb in zip(heads, cols, dsns, dobs, dvbs)]
            for h in heads:
                ds_ref[h] = new[h]

    row = pl.BlockSpec((nch * CHUNK, wide), lambda i: (steps - 1 - i, 0))
    at = pl.BlockSpec((HEADS, nch * CHUNK, CHUNK), lambda i: (0, steps - 1 - i, 0))
    glb = pl.BlockSpec((HEADS, nch * 8, HEAD_DIM), lambda i: (0, steps - 1 - i, 0))
    return pl.pallas_call(
        body,
        name="gdn_scan_bwd",
        grid=(steps,),
        in_specs=[row, row, row, row, at, glb, pl.BlockSpec((nch, HEADS, HEAD_DIM, HEAD_DIM), lambda i: (steps - 1 - i, 0, 0, 0)), row],
        out_specs=[row, row, row, row, at, glb],
        out_shape=[jax.ShapeDtypeStruct((t, wide), F32)] * 4 + [jax.ShapeDtypeStruct((HEADS, t, CHUNK), F32), jax.ShapeDtypeStruct((HEADS, n * 8, HEAD_DIM), F32)],
        scratch_shapes=[pltpu.VMEM((HEADS, HEAD_DIM, HEAD_DIM), F32)],
        compiler_params=_params("arbitrary"),
    )(w, u, qg, kg, attn, gl, states, do)


SB_Q = 512
SB_K = 256
SB_STEP = 1
SB_DEAD = -105.0


def _sb_scores(q, k):
    z = _dot(q, k, NT) * (HEAD_DIM ** -0.5)
    e = jnp.exp(-jnp.abs(z))
    lb = jnp.minimum(z, 0.0) - jnp.log(1.0 + e)
    return z, e, lb, lb - z


def _tri(n, rel):
    return jnp.where(rel(_iota((n, n), 0), _iota((n, n), 1)), 1.0, 0.0).astype(BF16)


def _lanes(col):
    return jnp.broadcast_to(col, (col.shape[0], LANES))


def _sb_fwd(q, k, v):
    t = q.shape[0]
    bq, bk = min(SB_Q, t), min(SB_K, t)
    nsub, rep = bq // bk, bk // LANES
    nstep = min(SB_STEP, nsub)
    steps_per_tile = nsub // nstep

    def body(q_ref, k_ref, v_ref, o_ref, rt_ref, first_ref):
        h = pl.program_id(0)
        i = pl.program_id(1)
        o_ref[...] = jnp.zeros_like(o_ref)
        rt_ref[...] = jnp.zeros_like(rt_ref)
        after = _tri(bk, lambda r, c: r > c)

        def block(j, r0, diag):
            st = pl.multiple_of(j * bk, bk)
            kv, vv = k_ref[pl.ds(st, bk), :], v_ref[pl.ds(st, bk), :]
            _, _, lb, l1m = _sb_scores(q_ref[r0:, :], kv)
            if diag:
                mask = _iota((bq - r0, bk), 1) + j * bk < _iota((bq - r0, bk), 0) + (r0 + i * bq)
                l1m = jnp.where(mask, l1m, 0.0)
            sums = _two_pass(l1m, after)
            run = rt_ref[r0:, :]
            a = jnp.exp(lb + jnp.tile(run, (1, rep)) + sums)
            if diag:
                a = jnp.where(mask, a, 0.0)
            o_ref[r0:, :] += _dot(a.astype(BF16), vv)
            rt_ref[r0:, :] = run + _lanes(sums[:, 0:1] + l1m[:, 0:1])

        for s in reversed(range(nsub)):
            block(i * nsub + s, s * bk, True)

        def alive(carry):
            u, highest = carry
            return jnp.logical_and(u >= 0, highest > SB_DEAD)

        def step(carry):
            u, _ = carry
            for s in reversed(range(nstep)):
                block(u * nstep + s, 0, False)
            return u - 1, jnp.max(rt_ref[...])

        u_end, _ = lax.while_loop(alive, step, (i * steps_per_tile - 1, jnp.max(rt_ref[...])))
        first_ref[h, i] = u_end + 1

    qb = pl.BlockSpec((bq, HEAD_DIM), lambda h, i: (i, h))
    full = pl.BlockSpec((t, HEAD_DIM), lambda h, i: (0, h))
    return pl.pallas_call(
        body,
        name="sb_fwd",
        grid=(HEADS, t // bq),
        in_specs=[qb, full, full],
        out_specs=[qb, qb, pl.BlockSpec(memory_space=pltpu.SMEM)],
        out_shape=[jax.ShapeDtypeStruct(q.shape, F32), jax.ShapeDtypeStruct(q.shape, F32), jax.ShapeDtypeStruct((HEADS, t // bq), jnp.int32)],
        compiler_params=_params("arbitrary", "arbitrary"),
    )(q, k, v)


def _sb_bwd(q, k, v, rt, first, do):
    t = q.shape[0]
    bq, bk = min(SB_Q, t), min(SB_K, t)
    nsub, rep = bq // bk, bk // LANES
    nstep = min(SB_STEP, nsub)
    steps_per_tile = nsub // nstep
    scale = HEAD_DIM ** -0.5

    def body(first_ref, q_ref, k_ref, v_ref, rt_ref, do_ref, dq_ref, dk_ref, dv_ref, left_ref, pg_ref):
        h = pl.program_id(0)
        i = pl.program_id(1)

        @pl.when(i == 0)
        def _():
            dk_ref[...] = jnp.zeros_like(dk_ref)
            dv_ref[...] = jnp.zeros_like(dv_ref)

        dq_ref[...] = jnp.zeros_like(dq_ref)
        left_ref[...] = jnp.zeros_like(left_ref)
        pg_ref[...] = jnp.zeros_like(pg_ref)
        upto = _tri(bk, lambda r, c: r <= c)

        def block(j, r0, diag):
            st = pl.multiple_of(j * bk, bk)
            kv, vv = k_ref[pl.ds(st, bk), :], v_ref[pl.ds(st, bk), :]
            qv = q_ref[r0:, :]
            dob = do_ref[r0:, :].astype(BF16)
            _, _, lb, l1m = _sb_scores(qv, kv)
            if diag:
                mask = _iota((bq - r0, bk), 1) + j * bk < _iota((bq - r0, bk), 0) + (r0 + i * bq)
                l1m = jnp.where(mask, l1m, 0.0)
            sums = _two_pass(l1m, upto)
            left = left_ref[r0:, :]
            a = jnp.exp(lb + jnp.tile(rt_ref[r0:, :] - left, (1, rep)) - sums)
            if diag:
                a = jnp.where(mask, a, 0.0)
            g = _dot(dob, vv, NT) * a
            dv_ref[pl.ds(st, bk), :] += _dot(a.astype(BF16), dob, TN)
            gsum = _two_pass(g, upto)
            pg = pg_ref[r0:, :]
            dz = g - jnp.exp(lb) * (jnp.tile(pg, (1, rep)) + gsum)
            if diag:
                dz = jnp.where(mask, dz, 0.0)
            dzb = (dz * scale).astype(BF16)
            dk_ref[pl.ds(st, bk), :] += _dot(dzb, qv, TN)
            dq_ref[r0:, :] += _dot(dzb, kv)
            left_ref[r0:, :] = left + _lanes(sums[:, bk - 1:bk])
            pg_ref[r0:, :] = pg + _lanes(gsum[:, bk - 1:bk])

        def step(u, carry):
            for s in range(nstep):
                block(u * nstep + s, 0, False)
            return carry

        lax.fori_loop(first_ref[h, i], i * steps_per_tile, step, 0)
        for s in range(nsub):
            block(i * nsub + s, s * bk, True)

    qb = pl.BlockSpec((bq, HEAD_DIM), lambda h, i: (i, h))
    full = pl.BlockSpec((t, HEAD_DIM), lambda h, i: (0, h))
    return pl.pallas_call(
        body,
        name="sb_bwd",
        grid=(HEADS, t // bq),
        in_specs=[pl.BlockSpec(memory_space=pltpu.SMEM), qb, full, full, qb, qb],
        out_specs=[qb, full, full],
        out_shape=[jax.ShapeDtypeStruct(q.shape, F32)] * 3,
        scratch_shapes=[pltpu.VMEM((bq, LANES), F32), pltpu.VMEM((bq, LANES), F32)],
        compiler_params=_params("arbitrary", "arbitrary"),
    )(first, q, k, v, rt, do)


def _adamw(w, g, m, v, name, tm=256):
    r, c = w.shape
    tm = tm if r % tm == 0 else r

    def body(w_ref, g_ref, m_ref, v_ref, d_ref, nm_ref, nv_ref):
        gv = g_ref[...]
        nm = ADAM_B1 * m_ref[...] + (1.0 - ADAM_B1) * gv
        nv = ADAM_B2 * v_ref[...] + (1.0 - ADAM_B2) * (gv * gv)
        m_hat = nm / (1.0 - ADAM_B1 ** ADAM_STEP)
        v_hat = nv / (1.0 - ADAM_B2 ** ADAM_STEP)
        d_ref[...] = -ADAM_LR * (m_hat / (jnp.sqrt(v_hat) + ADAM_EPS) + ADAM_WD * w_ref[...])
        nm_ref[...] = nm
        nv_ref[...] = nv

    blk = pl.BlockSpec((tm, c), lambda i: (i, 0))
    return pl.pallas_call(
        body,
        name=name,
        grid=(r // tm,),
        in_specs=[blk] * 4,
        out_specs=[blk] * 3,
        out_shape=[jax.ShapeDtypeStruct((r, c), F32)] * 3,
        compiler_params=_params("parallel"),
    )(w, g, m, v)


def _local_step(x, tgt, gains, small, shards, assemble_first, assemble, early_reduce=None, late_reduce=None):
    mix_pre, mix_post, mlp_pre, mlp_post, kv_gain = gains
    a_log, dt_bias, out_gain = small
    t, d = x.shape
    row = lambda a, i=None: a[i:i + 1] if i is not None else a
    al = jnp.zeros((1, LANES), F32).at[:, HEADS:2 * HEADS].set(a_log)
    dtb = jnp.zeros((1, LANES), F32).at[:, HEADS:2 * HEADS].set(dt_bias)
    og = jnp.tile(out_gain, (1, HEADS))
    full = lambda a: (a, a.shape[1], 0)

    h0, *gathered_first = _rowwise("norm_in", _fn_norm, [full(x)], [row(mix_pre, 0)], [(d, BF16)], gather=shards[0])
    w_qkvg, w_ba, conv_w = assemble_first(gathered_first)
    qkvg = _matmul(h0, w_qkvg, "nn", F32, "mm_gdn_in", tk=1024)
    ba = _matmul(h0, w_ba, "nn", F32, "mm_gdn_ba", tk=1024)
    (conv, gq, gk, gv), gathered_conv = _conv_fwd(qkvg, conv_w, shards[1])
    beta, gc, gt = _rowwise("gates", _fn_gates, [full(ba)], [al, dtb], [(LANES, F32)] * 3)
    gcr = jnp.swapaxes(gc[:, HEADS:2 * HEADS], 0, 1).reshape(HEADS, t // CHUNK, 1, CHUNK)
    (pw, pu, pqg, pkg, pattn, pgl, ptm), gathered_prep = _gdn_prep(gq, gk, gv, beta, gc, gt, gcr, shards[2])
    w_out, w_kv, w_q, w_o, w_up, w_down = assemble(gathered_conv, gathered_prep)
    w_qkvg_t, w_up_t, w_down_t = (jnp.swapaxes(a, -1, -2) for a in (w_qkvg, w_up, w_down))
    o_gdn, states = _gdn_scan(pw, pu, pqg, pkg, pattn, pgl)
    (on,) = _rowwise("out_norm", _fn_outnorm, [full(o_gdn), (qkvg, d, 3)], [og], [(d, BF16)])
    mix0 = _matmul(on, w_out, "nn", F32, "mm_gdn_out", tk=1024)
    x1, h1 = _rowwise("res_a0", _fn_res_norm, [full(x), full(mix0)], [row(mix_post, 0), row(mlp_pre, 0)], [(d, F32), (d, BF16)])
    (a0,) = _matmul(h1, w_up[0], "nn", (BF16,), "mm_up0", tk=1024, epilogue=_relu2_of)
    d0 = _matmul(a0, w_down[0], "nn", F32, "mm_down0")
    x2, hkv, hq = _rowwise("res_b0", _fn_res_norm2, [full(x1), full(d0)], [row(mlp_post, 0), kv_gain, row(mix_pre, 1)], [(d, F32), (d, BF16), (d, BF16)])
    w_k, w_v = w_kv[:, :d], w_kv[:, d:]
    kp = _matmul(hkv, w_k, "nn", BF16, "mm_k", tk=1024)
    vp = _matmul(hkv, w_v, "nn", BF16, "mm_v", tk=1024)
    qp = _matmul(hq, w_q, "nn", BF16, "mm_q", tk=1024)
    o_sb, rt, sb_first = _sb_fwd(qp, kp, vp)
    mix1 = _matmul(o_sb, w_o, "nn", F32, "mm_sb_out", tk=1024)
    x3, h3 = _rowwise("res_a1", _fn_res_norm, [full(x2), full(mix1)], [row(mix_post, 1), row(mlp_pre, 1)], [(d, F32), (d, BF16)])
    (a1,) = _matmul(h3, w_up[1], "nn", (BF16,), "mm_up1", tk=1024, epilogue=_relu2_of)
    d1 = _matmul(a1, w_down[1], "nn", F32, "mm_down1")

    loss, dx3, dd1, g_mlp_post1 = _loss_call(x3, d1, tgt, row(mlp_post, 1))
    (du1,) = _matmul(dd1, w_down_t[1], "nn", (BF16,), "mm_down1_dx", epilogue=_relu2_cotangent, extras=[a1])
    g_down1 = _matmul(a1, dd1, "tn", F32, "mm_down1_dw")
    dh3 = _matmul(du1, w_up_t[1], "nn", F32, "mm_up1_dx")
    g_up1 = _matmul(h3, du1, "tn", F32, "mm_up1_dw")
    (dx2, dmix1), (g_mix_post1, g_mlp_pre1), _ = _rowwise_bwd(
        "res_a1_bwd", _fn_res_norm, [full(x2), full(mix1)], [row(mix_post, 1), row(mlp_pre, 1)], [dx3, dh3], [F32, BF16])
    do_sb = _matmul(dmix1, w_o, "nt", BF16, "mm_sb_out_dx")
    g_o = _matmul(o_sb, dmix1, "tn", F32, "mm_sb_out_dw")
    dqp, dkp, dvp = _sb_bwd(qp, kp, vp, rt, sb_first, do_sb)
    dhq = _matmul(dqp, w_q, "nt", F32, "mm_q_dx")
    g_q = _matmul(hq, dqp, "tn", F32, "mm_q_dw")
    dhkv = _matmul(dvp, w_v, "nt", F32, "mm_v_dx", add=_matmul(dkp, w_k, "nt", F32, "mm_k_dx"))
    g_kv = jnp.concatenate([_matmul(hkv, dkp, "tn", F32, "mm_k_dw"), _matmul(hkv, dvp, "tn", F32, "mm_v_dw")], axis=1)
    (dx1, dd0), (g_mlp_post0, g_kv_gain, g_mix_pre1), _ = _rowwise_bwd(
        "res_b0_bwd", _fn_res_norm2, [full(x1), full(d0)], [row(mlp_post, 0), kv_gain, row(mix_pre, 1)], [dx2, dhkv, dhq], [F32, BF16])
    (du0,) = _matmul(dd0, w_down_t[0], "nn", (BF16,), "mm_down0_dx", epilogue=_relu2_cotangent, extras=[a0])
    g_down0 = _matmul(a0, dd0, "tn", F32, "mm_down0_dw")
    dh1 = _matmul(du0, w_up_t[0], "nn", F32, "mm_up0_dx")
    g_up0 = _matmul(h1, du0, "tn", F32, "mm_up0_dw")
    (dx0, dmix0), (g_mix_post0, g_mlp_pre0), _ = _rowwise_bwd(
        "res_a0_bwd", _fn_res_norm, [full(x), full(mix0)], [row(mix_post, 0), row(mlp_pre, 0)], [dx1, dh1], [F32, BF16])
    don = _matmul(dmix0, w_out, "nt", F32, "mm_gdn_out_dx")
    g_out = _matmul(on, dmix0, "tn", F32, "mm_gdn_out_dw")
    (do_gdn, dgate), (g_og,), _ = _rowwise_bwd("out_norm_bwd", _fn_outnorm, [full(o_gdn), (qkvg, d, 3)], [og], [don], [F32, F32])
    dpw, dpu, dpqg, dpkg, dpattn, dpgl = _gdn_scan_bwd(pw, pu, pqg, pkg, pattn, pgl, states, do_gdn)
    partial, partial_bf16 = [], ()
    if early_reduce is not None:
        partial, partial_bf16 = early_reduce(dict(mlp_w_up=(g_up0, g_up1), mlp_w_down=(g_down0, g_down1), gdn_w_out=g_out[None], w_kv=g_kv, sb_w_q=g_q[None], sb_w_o=g_o[None]))
    (dgq, dgk, dgv, dbeta, dgc, dgt, dgcr), from_chips = _gdn_prep_bwd(gq, gk, gv, beta, gc, gt, gcr, ptm, dpw, dpu, dpqg, dpkg, dpattn, dpgl, partial_bf16)
    dgcr_lanes = jnp.pad(jnp.swapaxes(dgcr.reshape(HEADS, t), 0, 1), ((0, 0), (HEADS, LANES - 2 * HEADS)))
    gate_cots = [dbeta, dgc + dgcr_lanes, dgt]
    (dba,), (g_al, g_dtb), _ = _rowwise_bwd("gates_bwd", _fn_gates, [full(ba)], [al, dtb], gate_cots, [BF16])
    dqkvg, g_conv = _conv_bwd(conv, (dgq, dgk, dgv), dgate, qkvg, conv_w)
    dh0b = _matmul(dba, w_ba, "nt", F32, "mm_gdn_ba_dx", tk=LANES)
    dh0 = _matmul(dqkvg, w_qkvg_t, "nn", F32, "mm_gdn_in_dx", add=dh0b)
    g_qkvg = _matmul(h0, dqkvg, "tn", F32, "mm_gdn_in_dw")
    g_ba = _matmul(h0, dba, "tn", F32, "mm_gdn_ba_dw")
    g_w_in = jnp.concatenate([g_qkvg, g_ba[:, :2 * HEADS]], axis=1)[None]
    partial_late, partial_late_bf16 = late_reduce(dict(gdn_w_in=g_w_in)) if late_reduce is not None else ([], ())
    (grad_x,), (g_mix_pre0,), from_chips_late = _rowwise_bwd(
        "norm_in_bwd", lambda xx, gg: (_rms(xx, gg), xx), [full(x)], [row(mix_pre, 0)], [dh0, dx0], [F32], partials=partial_late_bf16)

    grads = dict(
        mix_pre_gain=jnp.concatenate([g_mix_pre0, g_mix_pre1], axis=0),
        mix_post_gain=jnp.concatenate([g_mix_post0, g_mix_post1], axis=0),
        mlp_pre_gain=jnp.concatenate([g_mlp_pre0, g_mlp_pre1], axis=0),
        mlp_post_gain=jnp.concatenate([g_mlp_post0, g_mlp_post1], axis=0),
        mlp_w_up=(g_up0, g_up1),
        mlp_w_down=(g_down0, g_down1),
        gdn_w_in=g_w_in,
        gdn_conv_w=g_conv[None, :CONV_K],
        gdn_a_log=g_al[:, HEADS:2 * HEADS],
        gdn_dt_bias=g_dtb[:, HEADS:2 * HEADS],
        gdn_out_gain=jnp.sum(g_og.reshape(HEADS, HEAD_DIM), axis=0, keepdims=True),
        gdn_w_out=g_out[None],
        kv_gain=g_kv_gain[0],
        w_kv=g_kv,
        sb_w_q=g_q[None],
        sb_w_o=g_o[None],
    )
    return loss, grad_x, grads, (list(partial) + list(partial_late), list(from_chips) + list(from_chips_late))


N_DEV = 8
N_CHIPS = 4
PACK_ROW_TILE = 128

_HBM = pl.BlockSpec(memory_space=pltpu.HBM)


def _place():
    return lax.axis_index("x"), lax.axis_index("y"), lax.axis_index("c")


def _other_chips(x, y):
    return [(1 - x, y), (x, 1 - y), (1 - x, 1 - y)]


def _remote(src, dst, send_sem, recv_sem, to):
    return pltpu.make_async_remote_copy(src_ref=src, dst_ref=dst, send_sem=send_sem, recv_sem=recv_sem, device_id=to, device_id_type=MESH)


def _gather8(v, name):
    rows, cols = v.shape

    def body(v_ref, out_ref, sum_ref, send_sems, recv_sems, local_sem):
        x, y, c = _place()
        me, sibling = (x, y, c), (x, y, 1 - c)
        chips = _other_chips(x, y)

        def blk(px, py, pc):
            return out_ref.at[pl.ds((4 * px + 2 * py + pc) * rows, rows), :]

        def copy(k, block, to, src=None):
            return _remote(blk(*block) if src is None else src, blk(*block), send_sems.at[k], recv_sems.at[k], to)

        mine = pltpu.make_async_copy(v_ref, blk(*me), local_sem)
        mine.start()
        first = [copy(0, me, sibling, src=v_ref)] + [copy(1 + j, me, (*chip, c), src=v_ref) for j, chip in enumerate(chips)]
        for cp in first:
            cp.start()
        passed = [copy(4 + j, (*chip, c), sibling) for j, chip in enumerate(chips)]
        for j, chip in enumerate(chips):
            copy(1 + j, (*chip, c), me).wait_recv()
            passed[j].start()
        copy(0, sibling, me).wait_recv()
        for j, chip in enumerate(chips):
            copy(4 + j, (*chip, 1 - c), me).wait_recv()
        for cp in first + passed:
            cp.wait_send()
        mine.wait()
        acc = out_ref[pl.ds(0, rows), :]
        for dev in range(1, N_DEV):
            acc = acc + out_ref[pl.ds(dev * rows, rows), :]
        sum_ref[...] = acc

    vm = pl.BlockSpec(memory_space=pltpu.VMEM)
    return pl.pallas_call(
        body,
        name=name,
        out_shape=[jax.ShapeDtypeStruct((N_DEV * rows, cols), v.dtype), jax.ShapeDtypeStruct((rows, cols), v.dtype)],
        in_specs=[vm],
        out_specs=[vm, vm],
        scratch_shapes=[pltpu.SemaphoreType.DMA((7,)), pltpu.SemaphoreType.DMA((7,)), pltpu.SemaphoreType.DMA],
    )(v)


def _hbm_call(body, name, arrs, out_shapes, sem_counts):
    n = len(arrs)

    def wrapped(*refs):
        body(refs[:n], refs[n:2 * n], *refs[2 * n:])

    return pl.pallas_call(
        wrapped,
        name=name,
        out_shape=[jax.ShapeDtypeStruct(s, a.dtype) for s, a in zip(out_shapes, arrs)],
        in_specs=[_HBM] * n,
        out_specs=[_HBM] * n,
        scratch_shapes=[pltpu.SemaphoreType.DMA((k,)) for k in sem_counts],
    )(*arrs)


def _gather_sends(w_refs, out_refs, send_sems, recv_sems):
    x, y, c = _place()
    s_me = 2 * x + y
    return [_remote(w.at[c], o.at[s_me, c], send_sems.at[3 * a + j], recv_sems.at[3 * a + j], (px, py, c))
            for a, (w, o) in enumerate(zip(w_refs, out_refs)) for j, (px, py) in enumerate(_other_chips(x, y))]


def _gather_finish(w_refs, out_refs, send_sems, recv_sems, fsend_sems, frecv_sems):
    x, y, c = _place()
    chips = _other_chips(x, y)
    passed = []
    for a, o in enumerate(out_refs):
        for j, (px, py) in enumerate(chips):
            half = o.at[2 * px + py, c]
            _remote(half, half, send_sems.at[3 * a + j], recv_sems.at[3 * a + j], (px, py, c)).wait_recv()
            fwd = _remote(half, half, fsend_sems.at[3 * a + j], frecv_sems.at[3 * a + j], (x, y, 1 - c))
            fwd.start()
            passed.append(fwd)
    for a, o in enumerate(out_refs):
        for j, (px, py) in enumerate(chips):
            half = o.at[2 * px + py, 1 - c]
            _remote(half, half, fsend_sems.at[3 * a + j], frecv_sems.at[3 * a + j], (x, y, 1 - c)).wait_recv()
    for cp in _gather_sends(w_refs, out_refs, send_sems, recv_sems) + passed:
        cp.wait_send()


def _swap_halves(arrs, name):
    n = len(arrs)

    def body(g_refs, a_refs, send_sems, recv_sems):
        x, y, c = _place()
        cps = [_remote(g.at[1 - c], a, send_sems.at[i], recv_sems.at[i], (x, y, 1 - c)) for i, (g, a) in enumerate(zip(g_refs, a_refs))]
        for cp in cps:
            cp.start()
        for cp in cps:
            cp.wait()

    return _hbm_call(body, name, arrs, [a.shape[1:] for a in arrs], [n, n])


def _scatter_copies(p_refs, b_refs, send_sems, recv_sems):
    x, y, c = _place()
    return [_remote(p.at[2 * px + py], b.at[j], send_sems.at[3 * i + j], recv_sems.at[3 * i + j], (px, py, c))
            for i, (p, b) in enumerate(zip(p_refs, b_refs)) for j, (px, py) in enumerate(_other_chips(x, y))]


def _share_halves(arrs):
    n = len(arrs)

    def body(q_refs, out_refs, send_sems, recv_sems):
        x, y, c = _place()
        cps = [_remote(q, o, send_sems.at[i], recv_sems.at[i], (x, y, 1 - c)) for i, (q, o) in enumerate(zip(q_refs, out_refs))]
        for cp in cps:
            cp.start()
        for cp in cps:
            cp.wait()

    return _hbm_call(body, "grads_share", arrs, [a.shape for a in arrs], [n, n])


_GROUPS = (
    (("gdn_w_out", (1, 256, 1024), "rows"), ("mlp_w_up", (2, 1024, 1024), "cols")),
    (("mlp_w_down", (2, 1024, 1024), "rows"), ("sb_w_q", (1, 256, 1024), "rows"), ("sb_w_o", (1, 256, 1024), "rows")),
    (("w_kv", (1024, 512), "cols"),),
    (("gdn_w_in", (1, 1024, 1028), "cols"),),
)
_BEHIND_CONV, _BEHIND_PREP, _FIRST = slice(0, 1), slice(1, 3), slice(3, 4)
_EARLY_GRADS = slice(0, 3)


def _numel(shape):
    n = 1
    for s in shape:
        n *= s
    return n


def _half_rows(shape):
    return _numel(shape[:-1]) // 2


def _pack_shards(shards, dtype):
    return tuple(jnp.concatenate([shards[n].astype(dtype).reshape(2, _half_rows(shape), shape[-1]) for n, shape, _ in grp], axis=1) for grp in _GROUPS)


def _unpack_shards(bufs):
    out = {}
    for grp, buf in zip(_GROUPS, bufs):
        off = 0
        for n, shape, _ in grp:
            out[n] = buf[:, off:off + _half_rows(shape)].reshape(shape)
            off += _half_rows(shape)
    return out


def _join(stacked, how):
    nd = stacked.ndim - 1
    ax = nd - 1 if how == "cols" else nd - 2
    moved = jnp.moveaxis(stacked, 0, ax)
    shape = list(stacked.shape[1:])
    shape[ax] *= N_CHIPS
    return moved.reshape(shape)


def _split(full, shard_shape, how):
    nd = len(shard_shape)
    ax = nd - 1 if how == "cols" else nd - 2
    shape = list(shard_shape)
    shape.insert(ax, N_CHIPS)
    return jnp.moveaxis(full.reshape(shape), ax, 0)


def _unpack_full(gathered, groups):
    out = {}
    for grp, buf in zip(groups, gathered):
        off = 0
        for n, shape, how in grp:
            out[n] = _join(buf[:, :, off:off + _half_rows(shape)].reshape((N_CHIPS,) + shape), how)
            off += _half_rows(shape)
    return out


def _pack_full(full, groups):
    bufs = []
    for grp in groups:
        parts = []
        for n, shape, how in grp:
            if isinstance(full[n], tuple):
                assert len(full[n]) == shape[0] == 2
                parts.append(jnp.stack([_split(layer, shape[1:], how) for layer in full[n]], axis=1))
            else:
                parts.append(_split(full[n], shape, how).reshape(N_CHIPS, 2, _half_rows(shape), shape[-1]))
        buf = jnp.swapaxes(jnp.concatenate(parts, axis=2), 0, 1)
        bufs.append(buf.reshape(2, -1, buf.shape[-1]))
    return tuple(bufs)


_SMALL = (
    ("mix_pre_gain", (2, 1024)),
    ("mix_post_gain", (2, 1024)),
    ("mlp_pre_gain", (2, 1024)),
    ("mlp_post_gain", (2, 1024)),
    ("kv_gain", (1024,)),
    ("gdn_out_gain", (1, 128)),
    ("gdn_a_log", (1, 8)),
    ("gdn_dt_bias", (1, 8)),
    ("gdn_conv_w", (1, 4, 3072)),
    ("loss", ()),
)


def _rows_of(shape):
    return -(-_numel(shape) // LANES)


_SMALL_ROWS = -(-sum(_rows_of(s) for _, s in _SMALL) // 8) * 8


def _pack_small(vals):
    parts = []
    for n, shape in _SMALL:
        flat = vals[n].reshape(-1)
        parts.append(jnp.pad(flat, (0, _rows_of(shape) * LANES - flat.shape[0])))
    flat = jnp.concatenate(parts)
    return jnp.pad(flat, (0, _SMALL_ROWS * LANES - flat.shape[0])).reshape(_SMALL_ROWS, LANES)


def _unpack_small(packed):
    flat = packed.reshape(-1)
    out, off = {}, 0
    for n, shape in _SMALL:
        out[n] = flat[off:off + _numel(shape)].reshape(shape)
        off += _rows_of(shape) * LANES
    return out


_WEIGHTS = ("mix_pre_gain", "mix_post_gain", "mlp_pre_gain", "mlp_post_gain", "mlp_w_up", "mlp_w_down", "gdn_w_in", "gdn_conv_w",
            "gdn_a_log", "gdn_dt_bias", "gdn_out_gain", "gdn_w_out", "kv_gain", "w_kv", "sb_w_q", "sb_w_o")


def _as2d(a):
    return a.reshape(1, -1) if a.ndim <= 1 else a.reshape(-1, a.shape[-1])


def kernel(x, mix_pre_gain, mix_post_gain, mlp_pre_gain, mlp_post_gain, mlp_w_up, mlp_w_down, gdn_w_in, gdn_conv_w, gdn_a_log, gdn_dt_bias, gdn_out_gain, gdn_w_out, kv_gain, w_kv, sb_w_q, sb_w_o, loss_target, m_mix_pre_gain, m_mix_post_gain, m_mlp_pre_gain, m_mlp_post_gain, m_mlp_w_up, m_mlp_w_down, m_gdn_w_in, m_gdn_conv_w, m_gdn_a_log, m_gdn_dt_bias, m_gdn_out_gain, m_gdn_w_out, m_kv_gain, m_w_kv, m_sb_w_q, m_sb_w_o, v_mix_pre_gain, v_mix_post_gain, v_mlp_pre_gain, v_mlp_post_gain, v_mlp_w_up, v_mlp_w_down, v_gdn_w_in, v_gdn_conv_w, v_gdn_a_log, v_gdn_dt_bias, v_gdn_out_gain, v_gdn_w_out, v_kv_gain, v_w_kv, v_sb_w_q, v_sb_w_o):
    w = dict(mix_pre_gain=mix_pre_gain, mix_post_gain=mix_post_gain, mlp_pre_gain=mlp_pre_gain, mlp_post_gain=mlp_post_gain, mlp_w_up=mlp_w_up, mlp_w_down=mlp_w_down, gdn_w_in=gdn_w_in, gdn_conv_w=gdn_conv_w, gdn_a_log=gdn_a_log, gdn_dt_bias=gdn_dt_bias, gdn_out_gain=gdn_out_gain, gdn_w_out=gdn_w_out, kv_gain=kv_gain, w_kv=w_kv, sb_w_q=sb_w_q, sb_w_o=sb_w_o)
    m = dict(mix_pre_gain=m_mix_pre_gain, mix_post_gain=m_mix_post_gain, mlp_pre_gain=m_mlp_pre_gain, mlp_post_gain=m_mlp_post_gain, mlp_w_up=m_mlp_w_up, mlp_w_down=m_mlp_w_down, gdn_w_in=m_gdn_w_in, gdn_conv_w=m_gdn_conv_w, gdn_a_log=m_gdn_a_log, gdn_dt_bias=m_gdn_dt_bias, gdn_out_gain=m_gdn_out_gain, gdn_w_out=m_gdn_w_out, kv_gain=m_kv_gain, w_kv=m_w_kv, sb_w_q=m_sb_w_q, sb_w_o=m_sb_w_o)
    v = dict(mix_pre_gain=v_mix_pre_gain, mix_post_gain=v_mix_post_gain, mlp_pre_gain=v_mlp_pre_gain, mlp_post_gain=v_mlp_post_gain, mlp_w_up=v_mlp_w_up, mlp_w_down=v_mlp_w_down, gdn_w_in=v_gdn_w_in, gdn_conv_w=v_gdn_conv_w, gdn_a_log=v_gdn_a_log, gdn_dt_bias=v_gdn_dt_bias, gdn_out_gain=v_gdn_out_gain, gdn_w_out=v_gdn_w_out, kv_gain=v_kv_gain, w_kv=v_w_kv, sb_w_q=v_sb_w_q, sb_w_o=v_sb_w_o)
    cx, cy, cc = _place()
    chip = 2 * cx + cy
    conv_cols = gdn_conv_w.shape[-1]

    own = _pack_shards(w, BF16)
    own_taps = jnp.pad(gdn_conv_w[0], ((0, CONV_K), (0, 0))).reshape(2, CONV_K, conv_cols)
    with_own = lambda gathered, mine: [lax.dynamic_update_index_in_dim(g, m, chip, 0) for g, m in zip(gathered, mine)]

    def assemble_first(gathered):
        w_in_all, taps_all = with_own(gathered, (*own[_FIRST], own_taps))
        w_in = _unpack_full([w_in_all], _GROUPS[_FIRST])["gdn_w_in"][0]
        taps = jnp.swapaxes(taps_all[:, 0], 0, 1).reshape(CONV_K, N_CHIPS * conv_cols)
        return w_in[:, :4 * HEADS * HEAD_DIM], jnp.pad(w_in[:, 4 * HEADS * HEAD_DIM:], ((0, 0), (0, LANES - 2 * HEADS))), taps

    def assemble(gathered_conv, gathered_prep):
        full = {**_unpack_full(with_own(gathered_conv, own[_BEHIND_CONV]), _GROUPS[_BEHIND_CONV]),
                **_unpack_full(with_own(gathered_prep, own[_BEHIND_PREP]), _GROUPS[_BEHIND_PREP])}
        return full["gdn_w_out"][0], full["w_kv"], full["sb_w_q"][0], full["sb_w_o"][0], full["mlp_w_up"], full["mlp_w_down"]

    gains = (mix_pre_gain, mix_post_gain, mlp_pre_gain, mlp_post_gain, kv_gain[None])
    small = (gdn_a_log, gdn_dt_bias, gdn_out_gain)
    tile = PACK_ROW_TILE

    def to_chip_partials(grads_full, groups, tag):
        bufs = _pack_full(grads_full, groups)
        p32, p16 = [], []
        for i, (buf, other) in enumerate(zip(bufs, _swap_halves(bufs, f"grads_to_sibling_{tag}"))):
            _, n, cols = buf.shape
            p, pb = _add_rows(f"grads_add_sibling_{tag}{i}", [(buf.reshape(2 * n, cols), cc * (n // tile)), (other, 0)], n, (F32, BF16), tile)
            p32.append(p.reshape(N_CHIPS, -1, cols))
            p16.append(pb.reshape(N_CHIPS, -1, cols))
        return p32, tuple(p16)

    loss_rows, grad_x, g_full, (partial, from_chips) = _local_step(
        x[0], loss_target[0], gains, small, ((*own[_FIRST], own_taps), own[_BEHIND_CONV], own[_BEHIND_PREP]), assemble_first, assemble,
        lambda g: to_chip_partials(g, _GROUPS[_EARLY_GRADS], "early"), lambda g: to_chip_partials(g, _GROUPS[_FIRST], "late"))

    reduced = []
    for i, (p, others) in enumerate(zip(partial, from_chips)):
        _, r, cols = p.shape
        terms = [(p.reshape(N_CHIPS * r, cols), chip * (r // tile))] + [(others.reshape(3 * r, cols), j * (r // tile)) for j in range(3)]
        reduced.append(_add_rows(f"grads_add_chips_{i}", terms, r, (F32,), tile)[0])
    g_shard = _unpack_shards([jnp.where(cc == 0, jnp.stack([r, o]), jnp.stack([o, r])) for r, o in zip(reduced, _share_halves(tuple(reduced)))])

    g_small_local = {n: g_full[n] for n, _ in _SMALL if n != "loss"}
    g_small_local["loss"] = loss_rows[0, 0]
    _, small_sum = _gather8(_pack_small(g_small_local), "allreduce_small")
    g_small = _unpack_small(small_sum)
    loss = g_small.pop("loss")
    g_small["gdn_conv_w"] = lax.dynamic_slice_in_dim(g_small["gdn_conv_w"], chip * conv_cols, conv_cols, axis=2)

    grads = {**g_shard, **g_small}
    deltas, new_m, new_v = {}, {}, {}
    for n in _WEIGHTS:
        d2, m2, v2 = _adamw(_as2d(w[n]), _as2d(grads[n]), _as2d(m[n]), _as2d(v[n]), "adamw_" + n)
        deltas[n], new_m[n], new_v[n] = d2.reshape(w[n].shape), m2.reshape(w[n].shape), v2.reshape(w[n].shape)
    return (loss, grad_x[None], *[grads[n].reshape(w[n].shape) for n in _WEIGHTS], *[deltas[n] for n in _WEIGHTS],
            *[new_m[n] for n in _WEIGHTS], *[new_v[n] for n in _WEIGHTS])
```

```python
import functools

import jax
import jax.numpy as jnp
from jax import lax
from jax.experimental import pallas as pl
from jax.experimental.pallas import tpu as pltpu

F32, BF16 = jnp.float32, jnp.bfloat16
HI = lax.Precision.HIGHEST
MESH = pl.DeviceIdType.MESH

EPS = 1e-6
HEADS = 8
HEAD_DIM = 128
CHUNK = 64
CHUNK_SHIFT = CHUNK.bit_length() - 1
CONV_K = 4
QKV = 3 * HEADS * HEAD_DIM

ADAM_LR, ADAM_B1, ADAM_B2, ADAM_EPS, ADAM_WD, ADAM_STEP = 0.001, 0.9, 0.999, 1e-08, 0.01, 10

VMEM_LIMIT_BYTES = 48 * 1024 * 1024
LANES = 128

NN = ((1,), (0,))
NT = ((1,), (1,))
TN = ((0,), (0,))


def _dot(a, b, dims=NN, precision=None):
    return lax.dot_general(a, b, (dims, ((), ())), precision=precision, preferred_element_type=F32)


def _params(*sem):
    return pltpu.CompilerParams(dimension_semantics=sem, vmem_limit_bytes=VMEM_LIMIT_BYTES)


def _iota(shape, axis):
    return lax.broadcasted_iota(jnp.int32, shape, axis)


def _matmul(a, b, mode, out_dtype, name, tm=1024, tn=1024, tk=2048, add=None, epilogue=None, extras=()):
    if mode == "nn":
        (m, k), (k2, n) = a.shape, b.shape
    elif mode == "nt":
        (m, k), (n, k2) = a.shape, b.shape
    else:
        (k, m), (k2, n) = a.shape, b.shape
    assert k == k2, (a.shape, b.shape, mode)
    tm, tn, tk = min(tm, m), min(tn, n), min(tk, k)
    assert m % tm == 0 and n % tn == 0 and k % tk == 0, (a.shape, b.shape, mode)
    nk = k // tk
    dims = {"nn": NN, "nt": NT, "tn": TN}[mode]
    tiles = ([add] if add is not None else []) + list(extras)
    out_dtypes = out_dtype if epilogue is not None else (out_dtype,)
    n_in = 2 + len(tiles)

    def finish(acc, extra_refs, o_refs):
        res = (acc,) if epilogue is None else epilogue(acc, *[r[...] for r in extra_refs])
        for o_ref, r in zip(o_refs, res):
            o_ref[...] = r.astype(o_ref.dtype)

    def body(*refs):
        a_ref, b_ref = refs[:2]
        extra_refs = refs[n_in - len(extras):n_in]
        o_refs, acc_ref = refs[n_in:-1], refs[-1]
        prod = _dot(a_ref[...].astype(BF16), b_ref[...].astype(BF16), dims)
        if nk == 1:
            finish(prod + refs[2][...].astype(F32) if add is not None else prod, extra_refs, o_refs)
            return
        kk = pl.program_id(2)

        @pl.when(kk == 0)
        def _():
            acc_ref[...] = refs[2][...].astype(F32) if add is not None else jnp.zeros_like(acc_ref)

        acc_ref[...] += prod

        @pl.when(kk == nk - 1)
        def _():
            finish(acc_ref[...], extra_refs, o_refs)

    a_spec = pl.BlockSpec((tk, tm), lambda i, j, kk: (kk, i)) if mode == "tn" else pl.BlockSpec((tm, tk), lambda i, j, kk: (i, kk))
    b_spec = pl.BlockSpec((tn, tk), lambda i, j, kk: (j, kk)) if mode == "nt" else pl.BlockSpec((tk, tn), lambda i, j, kk: (kk, j))
    o_spec = pl.BlockSpec((tm, tn), lambda i, j, kk: (i, j))
    res = pl.pallas_call(
        body,
        name=name,
        grid=(m // tm, n // tn, nk),
        in_specs=[a_spec, b_spec] + [o_spec] * len(tiles),
        out_specs=[o_spec] * len(out_dtypes),
        out_shape=[jax.ShapeDtypeStruct((m, n), dt) for dt in out_dtypes],
        scratch_shapes=[pltpu.VMEM((tm, tn), F32)],
        compiler_params=_params("parallel", "parallel", "arbitrary"),
    )(a, b, *tiles)
    return res if epilogue is not None else res[0]


def _row_specs(rows, tm):
    return [pl.BlockSpec((tm, w), lambda i, cb=cb: (i, cb)) for _, w, cb in rows]


def _full_spec(p):
    return pl.BlockSpec(p.shape, lambda i: (0,) * p.ndim)


def _rowwise(name, fn, rows, params, outs, tm=256, gather=()):
    t = rows[0][0].shape[0]
    tm = min(tm, t)
    steps = t // tm
    nr, npar, nout, ng = len(rows), len(params), len(outs), len(gather)

    def body(*refs):
        ins = [r[...].astype(F32) for r in refs[:nr]]
        ps = [p[...] for p in refs[nr:nr + npar]]
        shard_refs = refs[nr + npar:nr + npar + ng]
        o_refs = refs[nr + npar + ng:nr + npar + ng + nout]
        all_refs, sems = refs[nr + npar + ng + nout:nr + npar + 2 * ng + nout], refs[nr + npar + 2 * ng + nout:]
        if ng:
            @pl.when(pl.program_id(0) == 0)
            def _():
                for cp in _gather_sends(shard_refs, all_refs, *sems[:2]):
                    cp.start()

        res = fn(*ins, *ps)
        for o_ref, r in zip(o_refs, res):
            o_ref[...] = r.astype(o_ref.dtype)

        if ng:
            @pl.when(pl.program_id(0) == steps - 1)
            def _():
                _gather_finish(shard_refs, all_refs, *sems)

    return pl.pallas_call(
        body,
        name=name,
        grid=(steps,),
        in_specs=_row_specs(rows, tm) + [_full_spec(p) for p in params] + [_HBM] * ng,
        out_specs=[pl.BlockSpec((tm, w), lambda i: (i, 0)) for w, _ in outs] + [_HBM] * ng,
        out_shape=[jax.ShapeDtypeStruct((t, w), dt) for w, dt in outs] + [jax.ShapeDtypeStruct((N_CHIPS,) + s.shape, s.dtype) for s in gather],
        scratch_shapes=[pltpu.SemaphoreType.DMA((3 * ng,))] * (4 if ng else 0),
        compiler_params=_params("arbitrary" if ng else "parallel"),
    )(*[r[0] for r in rows], *params, *gather)


def _add_rows(name, terms, n_rows, out_dtypes, tm):
    cols = terms[0][0].shape[1]
    firsts = jnp.stack([jnp.asarray(first, jnp.int32) for _, first in terms])

    def body(firsts_ref, *refs):
        acc = refs[0][...].astype(F32)
        for r in refs[1:len(terms)]:
            acc = acc + r[...].astype(F32)
        for o_ref in refs[len(terms):]:
            o_ref[...] = acc.astype(o_ref.dtype)

    return pl.pallas_call(
        body,
        name=name,
        grid_spec=pltpu.PrefetchScalarGridSpec(
            num_scalar_prefetch=1,
            grid=(n_rows // tm,),
            in_specs=[pl.BlockSpec((tm, cols), lambda i, firsts_ref, k=k: (firsts_ref[k] + i, 0)) for k in range(len(terms))],
            out_specs=[pl.BlockSpec((tm, cols), lambda i, firsts_ref: (i, 0)) for _ in out_dtypes],
        ),
        out_shape=[jax.ShapeDtypeStruct((n_rows, cols), dt) for dt in out_dtypes],
        compiler_params=_params("parallel"),
    )(firsts, *[a for a, _ in terms])


def _rowwise_bwd(name, fn, rows, params, cots, grad_dtypes, tm=256, partials=()):
    t = rows[0][0].shape[0]
    tm = min(tm, t)
    steps = t // tm
    nr, npar, nc, nsc = len(rows), len(params), len(cots), len(partials)
    want = [j for j, dt in enumerate(grad_dtypes) if dt is not None]
    widths = [rows[j][1] for j in want]
    n_row_outs = len(want)
    n_in = nr + npar + nc

    def body(*refs):
        i = pl.program_id(0)
        ins = [r[...].astype(F32) for r in refs[:nr]]
        ps = [p[...] for p in refs[nr:nr + npar]]
        cs = tuple(c[...].astype(F32) for c in refs[nr + npar:n_in])
        p_refs = refs[n_in:n_in + nsc]
        outs = refs[n_in + nsc:n_in + nsc + n_row_outs + npar]
        from_refs, sems = refs[n_in + nsc + n_row_outs + npar:n_in + 2 * nsc + n_row_outs + npar], refs[n_in + 2 * nsc + n_row_outs + npar:]
        if nsc:
            @pl.when(i == 0)
            def _():
                for cp in _scatter_copies(p_refs, from_refs, *sems):
                    cp.start()

        _, vjp = jax.vjp(fn, *ins, *ps)
        gs = vjp(cs)
        for o_ref, j in zip(outs, want):
            o_ref[...] = gs[j].astype(o_ref.dtype)
        pg_refs = outs[n_row_outs:]

        @pl.when(i == 0)
        def _():
            for pg in pg_refs:
                pg[...] = jnp.zeros_like(pg)

        for pg, g in zip(pg_refs, gs[nr:]):
            pg[...] += g

        if nsc:
            @pl.when(i == steps - 1)
            def _():
                for cp in _scatter_copies(p_refs, from_refs, *sems):
                    cp.wait()

    row_specs = [pl.BlockSpec((tm, w), lambda i: (i, 0)) for w in widths]
    row_shapes = [jax.ShapeDtypeStruct((t, w), grad_dtypes[j]) for j, w in zip(want, widths)]
    res = pl.pallas_call(
        body,
        name=name,
        grid=(steps,),
        in_specs=_row_specs(rows, tm) + [_full_spec(p) for p in params] + [pl.BlockSpec((tm, c.shape[1]), lambda i: (i, 0)) for c in cots] + [_HBM] * nsc,
        out_specs=row_specs + [_full_spec(p) for p in params] + [_HBM] * nsc,
        out_shape=row_shapes + [jax.ShapeDtypeStruct(p.shape, F32) for p in params] + [jax.ShapeDtypeStruct((3,) + p.shape[1:], p.dtype) for p in partials],
        scratch_shapes=[pltpu.SemaphoreType.DMA((3 * nsc,))] * (2 if nsc else 0),
        compiler_params=_params("arbitrary"),
    )(*[r[0] for r in rows], *params, *cots, *partials)
    return res[:n_row_outs], res[n_row_outs:n_row_outs + npar], res[n_row_outs + npar:]


def _rms(x, g):
    return x * lax.rsqrt(jnp.mean(x * x, axis=-1, keepdims=True) + EPS) * g


def _sigmoid(x):
    return 1.0 / (1.0 + jnp.exp(-x))


def _softplus(x):
    return jnp.maximum(x, 0.0) + jnp.log1p(jnp.exp(-jnp.abs(x)))


def _two_pass(x, m):
    hi = x.astype(BF16)
    lo = (x - hi.astype(F32)).astype(BF16)
    return _dot(hi, m) + _dot(lo, m)


def _head_sum_impl(x):
    sums = [jnp.sum(x[:, h * HEAD_DIM:(h + 1) * HEAD_DIM], axis=-1, keepdims=True) for h in range(HEADS)]
    return jnp.concatenate([jnp.broadcast_to(s, (x.shape[0], HEAD_DIM)) for s in sums], axis=1)


@jax.custom_vjp
def _head_sum(x):
    return _head_sum_impl(x)


_head_sum.defvjp(lambda x: (_head_sum_impl(x), None), lambda _, g: (_head_sum_impl(g),))


def _fn_norm(x, g):
    return (_rms(x, g),)


def _fn_gates(ba, al, dt):
    col = _iota((1, LANES), 1)
    g = jnp.where((col >= HEADS) & (col < 2 * HEADS), -jnp.exp(al) * _softplus(ba + dt), 0.0)
    rows = ba.shape[0]
    r, c = _iota((rows, rows), 0), _iota((rows, rows), 1)
    same = (r >> CHUNK_SHIFT) == (c >> CHUNK_SHIFT)
    gc = _dot(jnp.where(same & (r >= c), 1.0, 0.0), g, precision=HI)
    gtot = _dot(jnp.where(same, 1.0, 0.0), g, precision=HI)
    return _sigmoid(ba), gc, gtot


def _fn_post_q(c):
    s = c * _sigmoid(c)
    return (s * lax.rsqrt(_head_sum(s * s) + EPS) * (HEAD_DIM ** -0.5),)


def _fn_post_k(c):
    s = c * _sigmoid(c)
    return (s * lax.rsqrt(_head_sum(s * s) + EPS),)


def _fn_post_v(c):
    return (c * _sigmoid(c),)


def _fn_post(cq, ck, cv):
    return _fn_post_q(cq) + _fn_post_k(ck) + _fn_post_v(cv)


def _fn_outnorm(o, gate, og):
    y = o * lax.rsqrt(_head_sum(o * o) * (1.0 / HEAD_DIM) + EPS) * og
    return (y * (gate * _sigmoid(gate)),)


def _fn_res_norm(x, m, gp, gn):
    x1 = x + _rms(m, gp)
    return x1, _rms(x1, gn)


def _fn_res_norm2(x, m, gp, ga, gb):
    x1 = x + _rms(m, gp)
    return x1, _rms(x1, ga), _rms(x1, gb)


def _relu2_of(u):
    r = jnp.maximum(u, 0.0)
    return (r * r,)


def _relu2_cotangent(da, a):
    return (da * (2.0 * jnp.sqrt(a.astype(F32))),)


def _loss_call(x3, d1, tgt, g, tm=256):
    t, d = x3.shape
    tm = min(tm, t)

    def body(x_ref, d_ref, t_ref, g_ref, loss_ref, dx_ref, dd_ref, dg_ref):
        i = pl.program_id(0)
        y, vjp = jax.vjp(lambda x, dd, gg: x + _rms(dd, gg), x_ref[...], d_ref[...], g_ref[...])
        err = y - t_ref[...]
        lrow = 0.5 * jnp.mean(err * err, axis=-1, keepdims=True)
        dx, dd, dg = vjp(err * (1.0 / d))
        dx_ref[...] = dx
        dd_ref[...] = dd.astype(dd_ref.dtype)

        @pl.when(i == 0)
        def _():
            loss_ref[...] = jnp.zeros_like(loss_ref)
            dg_ref[...] = jnp.zeros_like(dg_ref)

        loss_ref[...] += jnp.broadcast_to(jnp.sum(lrow, axis=0, keepdims=True), loss_ref.shape)
        dg_ref[...] += dg

    row = pl.BlockSpec((tm, d), lambda i: (i, 0))
    return pl.pallas_call(
        body,
        name="loss_head",
        grid=(t // tm,),
        in_specs=[row, row, row, _full_spec(g)],
        out_specs=[pl.BlockSpec((8, LANES), lambda i: (0, 0)), row, row, _full_spec(g)],
        out_shape=[jax.ShapeDtypeStruct((8, LANES), F32), jax.ShapeDtypeStruct((t, d), F32), jax.ShapeDtypeStruct((t, d), BF16), jax.ShapeDtypeStruct(g.shape, F32)],
        compiler_params=_params("arbitrary"),
    )(x3, d1, tgt, g)


HALO = 8


def _conv_fwd(qkvg, conv_w, shards, tm=256):
    t = qkvg.shape[0]
    tm = min(tm, t)
    steps = t // tm
    wide = QKV // 3
    n = len(shards)

    def body(*refs):
        cur_ref, prev_ref, w_ref = refs[:3]
        shard_refs = refs[3:3 + n]
        o_ref, q_ref, k_ref, v_ref = refs[3 + n:7 + n]
        all_refs = refs[7 + n:7 + 2 * n]
        buf, sems = refs[7 + 2 * n], refs[8 + 2 * n:]
        i = pl.program_id(0)

        if n:
            @pl.when(i == 0)
            def _():
                for cp in _gather_sends(shard_refs, all_refs, *sems[:2]):
                    cp.start()

        buf[0:HALO, :] = jnp.where(i > 0, prev_ref[...], 0.0)
        buf[HALO:, :] = cur_ref[...]
        acc = buf[pl.ds(HALO - CONV_K + 1, tm), :] * w_ref[pl.ds(0, 1), :]
        for j in range(1, CONV_K):
            acc = acc + buf[pl.ds(HALO - CONV_K + 1 + j, tm), :] * w_ref[pl.ds(j, 1), :]
        o_ref[...] = acc
        (q_ref[...], k_ref[...], v_ref[...]) = _fn_post(acc[:, 0:wide], acc[:, wide:2 * wide], acc[:, 2 * wide:])

        if n:
            @pl.when(i == steps - 1)
            def _():
                _gather_finish(shard_refs, all_refs, *sems)

    part = pl.BlockSpec((tm, wide), lambda i: (i, 0))
    res = pl.pallas_call(
        body,
        name="conv_fwd",
        grid=(steps,),
        in_specs=[
            pl.BlockSpec((tm, QKV), lambda i: (i, 0)),
            pl.BlockSpec((HALO, QKV), lambda i: (jnp.maximum(i * (tm // HALO) - 1, 0), 0)),
            pl.BlockSpec((CONV_K, QKV), lambda i: (0, 0)),
        ] + [_HBM] * n,
        out_specs=[pl.BlockSpec((tm, QKV), lambda i: (i, 0)), part, part, part] + [_HBM] * n,
        out_shape=[jax.ShapeDtypeStruct((t, QKV), F32)] + [jax.ShapeDtypeStruct((t, wide), F32)] * 3
        + [jax.ShapeDtypeStruct((N_CHIPS,) + s.shape, s.dtype) for s in shards],
        scratch_shapes=[pltpu.VMEM((tm + HALO, QKV), F32)] + [pltpu.SemaphoreType.DMA((3 * n,))] * (4 if n else 0),
        compiler_params=_params("arbitrary"),
    )(qkvg, qkvg, conv_w, *shards)
    return res[:4], res[4:]


def _conv_bwd(conv, dqkv, dgate, qkvg, conv_w, tm=256):
    t = conv.shape[0]
    tm = min(tm, t)
    n = t // tm
    wg = dgate.shape[1]
    wide = QKV // 3

    def conv_cotangent(c_ref, g_refs):
        parts = [c_ref[:, j * wide:(j + 1) * wide] for j in range(3)]
        _, vjp = jax.vjp(_fn_post, *parts)
        return vjp(tuple(g[...] for g in g_refs))

    def body(c_ref, cn_ref, dq_ref, dk_ref, dv_ref, dqn_ref, dkn_ref, dvn_ref, dgate_ref, x_ref, xp_ref, w_ref, dx_ref, dw_ref, bufd, bufx):
        i = pl.program_id(0)
        for j, (cur, nxt) in enumerate(zip(conv_cotangent(c_ref, (dq_ref, dk_ref, dv_ref)), conv_cotangent(cn_ref, (dqn_ref, dkn_ref, dvn_ref)))):
            bufd[0:tm, j * wide:(j + 1) * wide] = cur
            bufd[tm:, j * wide:(j + 1) * wide] = jnp.where(i < n - 1, nxt, 0.0)
        bufx[0:HALO, :] = jnp.where(i > 0, xp_ref[...], 0.0)
        bufx[HALO:, :] = x_ref[...]

        @pl.when(i == 0)
        def _():
            dw_ref[...] = jnp.zeros_like(dw_ref)

        dcv = bufd[0:tm, :]
        acc = bufd[pl.ds(CONV_K - 1, tm), :] * w_ref[pl.ds(0, 1), :]
        for j in range(1, CONV_K):
            acc = acc + bufd[pl.ds(CONV_K - 1 - j, tm), :] * w_ref[pl.ds(j, 1), :]
        dx_ref[:, 0:QKV] = acc.astype(dx_ref.dtype)
        dx_ref[:, QKV:] = dgate_ref[...].astype(dx_ref.dtype)
        for j in range(CONV_K):
            dw_ref[pl.ds(j, 1), :] += jnp.sum(dcv * bufx[pl.ds(HALO - CONV_K + 1 + j, tm), :], axis=0, keepdims=True)

    def cur(width):
        return pl.BlockSpec((tm, width), lambda i: (i, 0))

    def nxt(width):
        return pl.BlockSpec((HALO, width), lambda i: (jnp.minimum((i + 1) * (tm // HALO), t // HALO - 1), 0))

    return pl.pallas_call(
        body,
        name="conv_bwd",
        grid=(n,),
        in_specs=[cur(QKV), nxt(QKV)] + [cur(wide)] * 3 + [nxt(wide)] * 3 + [
            cur(wg),
            cur(QKV),
            pl.BlockSpec((HALO, QKV), lambda i: (jnp.maximum(i * (tm // HALO) - 1, 0), 0)),
            pl.BlockSpec((CONV_K, QKV), lambda i: (0, 0)),
        ],
        out_specs=[pl.BlockSpec((tm, QKV + wg), lambda i: (i, 0)), pl.BlockSpec((HALO, QKV), lambda i: (0, 0))],
        out_shape=[jax.ShapeDtypeStruct((t, QKV + wg), BF16), jax.ShapeDtypeStruct((HALO, QKV), F32)],
        scratch_shapes=[pltpu.VMEM((tm + HALO, QKV), F32), pltpu.VMEM((tm + HALO, QKV), F32)],
        compiler_params=_params("arbitrary"),
    )(conv, conv, *dqkv, *dqkv, dgate, qkvg, qkvg, conv_w)


PREP_CHUNKS = 32
PREP_BWD_CHUNKS = 4
SCAN_CHUNKS = 4


def _hi_lo(x):
    hi = x.astype(BF16)
    return hi, (x - hi.astype(F32)).astype(BF16)


def _mm3(a, b, dims=NN):
    (ah, al), (bh, bl) = _hi_lo(a), _hi_lo(b)
    return _dot(ah, bh, dims) + (_dot(ah, bl, dims) + _dot(al, bh, dims))


def _neumann(lowers):
    c = lowers[0].shape[0]
    eye = jnp.where(_iota((c, c), 0) == _iota((c, c), 1), 1.0, 0.0)
    ps = [-low for low in lowers]
    tmats = [eye + p for p in ps]
    for _ in range(CHUNK_SHIFT - 1):
        ps = [_mm3(p, p) for p in ps]
        tmats = [t + _mm3(t, p) for t, p in zip(tmats, ps)]
    return tuple(tmats)


def _inv_cotangents(tmats, dts):
    half = [_mm3(t, dt, TN) for t, dt in zip(tmats, dts)]
    return tuple(-_mm3(hf, t, NT) for hf, t in zip(half, tmats))


@jax.custom_vjp
def _tri_inv(lowers):
    return _neumann(lowers)


def _tri_inv_fwd(lowers):
    tmats = _neumann(lowers)
    return tmats, tmats


_tri_inv.defvjp(_tri_inv_fwd, lambda tmats, dts: (_inv_cotangents(tmats, dts),))


@jax.custom_vjp
def _tri_inv_known(lowers, tmats):
    return tmats


_tri_inv_known.defvjp(lambda lowers, tmats: (tmats, tmats),
                      lambda tmats, dts: (_inv_cotangents(tmats, dts), tuple(jnp.zeros_like(t) for t in tmats)))


def _prep_chunks(qs, ks, vs, bs, gcs, gts, gcrs, tmats=None):
    c = CHUNK
    r, col = _iota((c, c), 0), _iota((c, c), 1)
    incl, strict = r >= col, r > col
    decays = [jnp.where(incl, jnp.exp(jnp.where(incl, gc - gcr, 0.0)), 0.0) for gc, gcr in zip(gcs, gcrs)]
    kbs = [k * b for k, b in zip(ks, bs)]
    kbfs = [k.astype(BF16) for k in ks]
    lowers = tuple(jnp.where(strict, _dot(kb.astype(BF16), kbf, NT) * decay, 0.0) for kb, kbf, decay in zip(kbs, kbfs, decays))
    tmats = _tri_inv(lowers) if tmats is None else _tri_inv_known(lowers, tuple(tmats))
    outs = []
    for q, k, v, b, gc, gt, kb, kbf, decay, tmat in zip(qs, ks, vs, bs, gcs, gts, kbs, kbfs, decays, tmats):
        tb = tmat.astype(BF16)
        egc = jnp.exp(gc)
        w = _dot(tb, (kb * egc).astype(BF16))
        u = _dot(tb, (v * b).astype(BF16))
        attn = _dot(q.astype(BF16), kbf, NT) * decay
        gl = jnp.broadcast_to(jnp.exp(jnp.mean(gt.reshape(c // 8, 8, 1), axis=0)), (8, HEAD_DIM))
        outs.append((w, u, q * egc, k * jnp.exp(gt - gc), attn, gl))
    return tuple(outs), tmats


def _prep_specs(rows, gch):
    head = pl.BlockSpec((rows, HEAD_DIM), lambda n, h: (n, h))
    gates = pl.BlockSpec((rows, LANES), lambda n, h: (n, 0))
    gcrow = pl.BlockSpec((1, gch, 1, CHUNK), lambda n, h: (h, n, 0, 0))
    square = pl.BlockSpec((1, rows, CHUNK), lambda n, h: (h, n, 0))
    gl = pl.BlockSpec((1, gch * 8, HEAD_DIM), lambda n, h: (h, n, 0))
    return head, gates, gcrow, square, gl


def _pick_lane(ref, sl, lane):
    return jnp.sum(jnp.where(_iota((1, LANES), 1) == lane, ref[sl, :], 0.0), axis=1, keepdims=True)


def _prep_inputs(q_ref, k_ref, v_ref, b_ref, gc_ref, gt_ref, gcr_ref, sls, h):
    return ([q_ref[sl, :] for sl in sls], [k_ref[sl, :] for sl in sls], [v_ref[sl, :] for sl in sls],
            [_pick_lane(b_ref, sl, h) for sl in sls], [_pick_lane(gc_ref, sl, h + HEADS) for sl in sls],
            [_pick_lane(gt_ref, sl, h + HEADS) for sl in sls], [gcr_ref[0, c] for c in range(len(sls))])


def _gdn_prep(q, k, v, beta, gc, gt, gcr, shards=()):
    t = q.shape[0]
    gch = min(PREP_CHUNKS, t // CHUNK)
    rows = gch * CHUNK
    steps = t // rows
    n = len(shards)

    def body(*refs):
        q_ref, k_ref, v_ref, b_ref, gc_ref, gt_ref, gcr_ref = refs[:7]
        shard_refs = refs[7:7 + n]
        w_ref, u_ref, qg_ref, kg_ref, at_ref, gl_ref, tm_ref = refs[7 + n:14 + n]
        all_refs, sems = refs[14 + n:14 + 2 * n], refs[14 + 2 * n:]
        h = pl.program_id(1)

        if n:
            @pl.when(jnp.logical_and(pl.program_id(0) == 0, h == 0))
            def _():
                for cp in _gather_sends(shard_refs, all_refs, *sems[:2]):
                    cp.start()

        sls = [pl.ds(c * CHUNK, CHUNK) for c in range(gch)]
        outs, tmats = _prep_chunks(*_prep_inputs(q_ref, k_ref, v_ref, b_ref, gc_ref, gt_ref, gcr_ref, sls, h))
        for c, (sl, (w, u, qg, kg, attn, gl), tmat) in enumerate(zip(sls, outs, tmats)):
            w_ref[sl, :] = w.astype(BF16)
            u_ref[sl, :] = u
            qg_ref[sl, :] = qg.astype(BF16)
            kg_ref[sl, :] = kg.astype(BF16)
            at_ref[0, sl, :] = attn.astype(BF16)
            gl_ref[0, pl.ds(c * 8, 8), :] = gl
            tm_ref[0, sl, :] = tmat

        if n:
            @pl.when(jnp.logical_and(pl.program_id(0) == steps - 1, h == HEADS - 1))
            def _():
                _gather_finish(shard_refs, all_refs, *sems)

    hb, col, gcrow, square, glb = _prep_specs(rows, gch)
    wide = HEADS * HEAD_DIM
    res = pl.pallas_call(
        body,
        name="gdn_prep",
        grid=(steps, HEADS),
        in_specs=[hb, hb, hb, col, col, col, gcrow] + [_HBM] * n,
        out_specs=[hb, hb, hb, hb, square, glb, square] + [_HBM] * n,
        out_shape=[
            jax.ShapeDtypeStruct((t, wide), BF16),
            jax.ShapeDtypeStruct((t, wide), F32),
            jax.ShapeDtypeStruct((t, wide), BF16),
            jax.ShapeDtypeStruct((t, wide), BF16),
            jax.ShapeDtypeStruct((HEADS, t, CHUNK), BF16),
            jax.ShapeDtypeStruct((HEADS, t // CHUNK * 8, HEAD_DIM), F32),
            jax.ShapeDtypeStruct((HEADS, t, CHUNK), F32),
        ] + [jax.ShapeDtypeStruct((N_CHIPS,) + s.shape, s.dtype) for s in shards],
        scratch_shapes=[pltpu.SemaphoreType.DMA((3 * n,))] * (4 if n else 0),
        compiler_params=_params("arbitrary", "arbitrary") if n else _params("parallel", "parallel"),
    )(q, k, v, beta, gc, gt, gcr, *shards)
    return res[:7], res[7:]


def _gdn_prep_bwd(q, k, v, beta, gc, gt, gcr, tmat, dw, du, dqg, dkg, dattn, dgl, partials=()):
    t = q.shape[0]
    gch = min(PREP_BWD_CHUNKS, t // CHUNK)
    rows = gch * CHUNK
    steps = t // rows
    n_sc = len(partials)

    def body(*refs):
        (q_ref, k_ref, v_ref, b_ref, gc_ref, gt_ref, gcr_ref, tm_ref, dw_ref, du_ref, dqg_ref, dkg_ref, dat_ref, dgl_ref) = refs[:14]
        p_refs = refs[14:14 + n_sc]
        dq_ref, dk_ref, dv_ref, db_ref, dgc_ref, dgt_ref, dgcr_ref = refs[14 + n_sc:21 + n_sc]
        from_refs, sems = refs[21 + n_sc:21 + 2 * n_sc], refs[21 + 2 * n_sc:]
        h = pl.program_id(1)
        lane = _iota((1, LANES), 1)

        if n_sc:
            @pl.when(jnp.logical_and(pl.program_id(0) == 0, h == 0))
            def _():
                for cp in _scatter_copies(p_refs, from_refs, *sems):
                    cp.start()

        @pl.when(h == 0)
        def _():
            db_ref[...] = jnp.zeros_like(db_ref)
            dgc_ref[...] = jnp.zeros_like(dgc_ref)
            dgt_ref[...] = jnp.zeros_like(dgt_ref)

        sls = [pl.ds(c * CHUNK, CHUNK) for c in range(gch)]
        known = [tm_ref[0, sl, :] for sl in sls]
        _, vjp = jax.vjp(lambda *a: _prep_chunks(*a, tmats=known)[0], *_prep_inputs(q_ref, k_ref, v_ref, b_ref, gc_ref, gt_ref, gcr_ref, sls, h))
        cots = tuple((dw_ref[sl, :], du_ref[sl, :], dqg_ref[sl, :], dkg_ref[sl, :], dat_ref[0, sl, :], dgl_ref[0, pl.ds(c * 8, 8), :]) for c, sl in enumerate(sls))
        dqs, dks, dvs, dbs, dgcs, dgts, dgcrs = vjp(cots)
        for c, sl in enumerate(sls):
            dq_ref[sl, :] = dqs[c]
            dk_ref[sl, :] = dks[c]
            dv_ref[sl, :] = dvs[c]
            db_ref[sl, :] += jnp.where(lane == h, dbs[c], 0.0)
            dgc_ref[sl, :] += jnp.where(lane == h + HEADS, dgcs[c], 0.0)
            dgt_ref[sl, :] += jnp.where(lane == h + HEADS, dgts[c], 0.0)
            dgcr_ref[0, c] = dgcrs[c]

        if n_sc:
            @pl.when(jnp.logical_and(pl.program_id(0) == steps - 1, h == HEADS - 1))
            def _():
                for cp in _scatter_copies(p_refs, from_refs, *sems):
                    cp.wait()

    hb, col, gcrow, square, glb = _prep_specs(rows, gch)
    wide = HEADS * HEAD_DIM
    res = pl.pallas_call(
        body,
        name="gdn_prep_bwd",
        grid=(steps, HEADS),
        in_specs=[hb, hb, hb, col, col, col, gcrow, square, hb, hb, hb, hb, square, glb] + [_HBM] * n_sc,
        out_specs=[hb, hb, hb, col, col, col, gcrow] + [_HBM] * n_sc,
        out_shape=[jax.ShapeDtypeStruct((t, wide), F32)] * 3 + [jax.ShapeDtypeStruct((t, LANES), F32)] * 3 + [jax.ShapeDtypeStruct((HEADS, t // CHUNK, 1, CHUNK), F32)]
        + [jax.ShapeDtypeStruct((3,) + p.shape[1:], p.dtype) for p in partials],
        scratch_shapes=[pltpu.SemaphoreType.DMA((3 * n_sc,))] * (2 if n_sc else 0),
        compiler_params=_params("arbitrary", "arbitrary"),
    )(q, k, v, beta, gc, gt, gcr, tmat, dw, du, dqg, dkg, dattn, dgl, *partials)
    return res[:7], res[7:]


def _gdn_scan(w, u, qg, kg, attn, gl):
    t = w.shape[0]
    n = t // CHUNK
    nch = min(SCAN_CHUNKS, n)
    wide = HEADS * HEAD_DIM

    def body(w_ref, u_ref, qg_ref, kg_ref, at_ref, gl_ref, o_ref, st_ref, s_ref):
        @pl.when(pl.program_id(0) == 0)
        def _():
            s_ref[...] = jnp.zeros_like(s_ref)

        heads = range(HEADS)
        cols = [pl.ds(h * HEAD_DIM, HEAD_DIM) for h in heads]
        for c in range(nch):
            rows, gl_rows = pl.ds(c * CHUNK, CHUNK), pl.ds(c * 8, 8)
            ss = [s_ref[h] for h in heads]
            sbs = [s.astype(BF16) for s in ss]
            vbs = [(u_ref[rows, hs] - _dot(w_ref[rows, hs], sb)).astype(BF16) for hs, sb in zip(cols, sbs)]
            outs = [_dot(qg_ref[rows, hs], sb) + _dot(at_ref[h, rows, :], vb) for h, hs, sb, vb in zip(heads, cols, sbs, vbs)]
            new = [s * jnp.tile(gl_ref[h, gl_rows, :], (HEAD_DIM // 8, 1)) + _dot(kg_ref[rows, hs], vb, TN) for h, hs, s, vb in zip(heads, cols, ss, vbs)]
            for h, hs in zip(heads, cols):
                st_ref[c, h] = ss[h]
                o_ref[rows, hs] = outs[h]
                s_ref[h] = new[h]

    row = pl.BlockSpec((nch * CHUNK, wide), lambda i: (i, 0))
    return pl.pallas_call(
        body,
        name="gdn_scan",
        grid=(n // nch,),
        in_specs=[row, row, row, row, pl.BlockSpec((HEADS, nch * CHUNK, CHUNK), lambda i: (0, i, 0)), pl.BlockSpec((HEADS, nch * 8, HEAD_DIM), lambda i: (0, i, 0))],
        out_specs=[row, pl.BlockSpec((nch, HEADS, HEAD_DIM, HEAD_DIM), lambda i: (i, 0, 0, 0))],
        out_shape=[jax.ShapeDtypeStruct((t, wide), F32), jax.ShapeDtypeStruct((n, HEADS, HEAD_DIM, HEAD_DIM), F32)],
        scratch_shapes=[pltpu.VMEM((HEADS, HEAD_DIM, HEAD_DIM), F32)],
        compiler_params=_params("arbitrary"),
    )(w, u, qg, kg, attn, gl)


def _gdn_scan_bwd(w, u, qg, kg, attn, gl, states, do):
    t = w.shape[0]
    n = t // CHUNK
    nch = min(SCAN_CHUNKS, n)
    steps = n // nch
    wide = HEADS * HEAD_DIM

    def body(w_ref, u_ref, qg_ref, kg_ref, at_ref, gl_ref, st_ref, do_ref, dw_ref, du_ref, dqg_ref, dkg_ref, dat_ref, dgl_ref, ds_ref):
        @pl.when(pl.program_id(0) == 0)
        def _():
            ds_ref[...] = jnp.zeros_like(ds_ref)

        heads = range(HEADS)
        cols = [pl.ds(h * HEAD_DIM, HEAD_DIM) for h in heads]
        for c in reversed(range(nch)):
            rows, gl_rows = pl.ds(c * CHUNK, CHUNK), pl.ds(c * 8, 8)
            ss = [st_ref[c, h] for h in heads]
            sbs = [s.astype(BF16) for s in ss]
            dsns = [ds_ref[h] for h in heads]
            dsbs = [d.astype(BF16) for d in dsns]
            dobs = [do_ref[rows, hs].astype(BF16) for hs in cols]
            vbs = [(u_ref[rows, hs] - _dot(w_ref[rows, hs], sb)).astype(BF16) for hs, sb in zip(cols, sbs)]
            dvns = [_dot(at_ref[h, rows, :], dob, TN) + _dot(kg_ref[rows, hs], dsb) for h, hs, dob, dsb in zip(heads, cols, dobs, dsbs)]
            dvbs = [d.astype(BF16) for d in dvns]
            for h, hs in zip(heads, cols):
                dat_ref[h, rows, :] = _dot(dobs[h], vbs[h], NT)
                dqg_ref[rows, hs] = _dot(dobs[h], sbs[h], NT)
                dkg_ref[rows, hs] = _dot(vbs[h], dsbs[h], NT)
                du_ref[rows, hs] = dvns[h]
                dw_ref[rows, hs] = -_dot(dvbs[h], sbs[h], NT)
                dgl_ref[h, gl_rows, :] = jnp.sum((dsns[h] * ss[h]).reshape(HEAD_DIM // 8, 8, HEAD_DIM), axis=0)
            new = [dsn * jnp.tile(gl_ref[h, gl_rows, :], (HEAD_DIM // 8, 1)) + _dot(qg_ref[rows, hs], dob, TN) - _dot(w_ref[rows, hs], dvb, TN)
                   for h, hs, dsn, dob, dvb in zip(heads, cols, dsns, dobs, dvbs)]
            for h in heads:
                ds_ref[h] = new[h]

    row = pl.BlockSpec((nch * CHUNK, wide), lambda i: (steps - 1 - i, 0))
    at = pl.BlockSpec((HEADS, nch * CHUNK, CHUNK), lambda i: (0, steps - 1 - i, 0))
    glb = pl.BlockSpec((HEADS, nch * 8, HEAD_DIM), lambda i: (0, steps - 1 - i, 0))
    return pl.pallas_call(
        body,
        name="gdn_scan_bwd",
        grid=(steps,),
        in_specs=[row, row, row, row, at, glb, pl.BlockSpec((nch, HEADS, HEAD_DIM, HEAD_DIM), lambda i: (steps - 1 - i, 0, 0, 0)), row],
        out_specs=[row, row, row, row, at, glb],
        out_shape=[jax.ShapeDtypeStruct((t, wide), F32)] * 4 + [jax.ShapeDtypeStruct((HEADS, t, CHUNK), F32), jax.ShapeDtypeStruct((HEADS, n * 8, HEAD_DIM), F32)],
        scratch_shapes=[pltpu.VMEM((HEADS, HEAD_DIM, HEAD_DIM), F32)],
        compiler_params=_params("arbitrary"),
    )(w, u, qg, kg, attn, gl, states, do)


SB_Q = 512
SB_K = 256
SB_STEP = 1
SB_DEAD = -105.0


def _sb_scores(q, k):
    z = _dot(q, k, NT) * (HEAD_DIM ** -0.5)
    lb = jnp.minimum(z, 0.0) - jnp.log(1.0 + jnp.exp(-jnp.abs(z)))
    return lb, lb - z


def _tri(n, rel):
    return jnp.where(rel(_iota((n, n), 0), _iota((n, n), 1)), 1.0, 0.0).astype(BF16)


def _lanes(col):
    return jnp.broadcast_to(col, (col.shape[0], LANES))


def _sb_fwd(q, k, v):
    t = q.shape[0]
    bq, bk = min(SB_Q, t), min(SB_K, t)
    nsub, rep = bq // bk, bk // LANES
    nstep = min(SB_STEP, nsub)
    steps_per_tile = nsub // nstep

    def body(q_ref, k_ref, v_ref, o_ref, rt_ref, first_ref):
        h = pl.program_id(0)
        i = pl.program_id(1)
        o_ref[...] = jnp.zeros_like(o_ref)
        rt_ref[...] = jnp.zeros_like(rt_ref)
        after = _tri(bk, lambda r, c: r > c)

        def block(j, r0, diag):
            st = pl.multiple_of(j * bk, bk)
            kv, vv = k_ref[pl.ds(st, bk), :], v_ref[pl.ds(st, bk), :]
            lb, l1m = _sb_scores(q_ref[r0:, :], kv)
            if diag:
                mask = _iota((bq - r0, bk), 1) + j * bk < _iota((bq - r0, bk), 0) + (r0 + i * bq)
                l1m = jnp.where(mask, l1m, 0.0)
            sums = _two_pass(l1m, after)
            run = rt_ref[r0:, :]
            a = jnp.exp(lb + jnp.tile(run, (1, rep)) + sums)
            if diag:
                a = jnp.where(mask, a, 0.0)
            o_ref[r0:, :] += _dot(a.astype(BF16), vv)
            rt_ref[r0:, :] = run + _lanes(sums[:, 0:1] + l1m[:, 0:1])

        for s in reversed(range(nsub)):
            block(i * nsub + s, s * bk, True)

        def alive(carry):
            u, highest = carry
            return jnp.logical_and(u >= 0, highest > SB_DEAD)

        def step(carry):
            u, _ = carry
            for s in reversed(range(nstep)):
                block(u * nstep + s, 0, False)
            return u - 1, jnp.max(rt_ref[...])

        u_end, _ = lax.while_loop(alive, step, (i * steps_per_tile - 1, jnp.max(rt_ref[...])))
        first_ref[h, i] = u_end + 1

    qb = pl.BlockSpec((bq, HEAD_DIM), lambda h, i: (i, h))
    full = pl.BlockSpec((t, HEAD_DIM), lambda h, i: (0, h))
    return pl.pallas_call(
        body,
        name="sb_fwd",
        grid=(HEADS, t // bq),
        in_specs=[qb, full, full],
        out_specs=[qb, qb, pl.BlockSpec(memory_space=pltpu.SMEM)],
        out_shape=[jax.ShapeDtypeStruct(q.shape, F32), jax.ShapeDtypeStruct(q.shape, F32), jax.ShapeDtypeStruct((HEADS, t // bq), jnp.int32)],
        compiler_params=_params("arbitrary", "arbitrary"),
    )(q, k, v)


def _sb_bwd(q, k, v, rt, first, do):
    t = q.shape[0]
    bq, bk = min(SB_Q, t), min(SB_K, t)
    nsub, rep = bq // bk, bk // LANES
    nstep = min(SB_STEP, nsub)
    steps_per_tile = nsub // nstep
    scale = HEAD_DIM ** -0.5

    def body(first_ref, q_ref, k_ref, v_ref, rt_ref, do_ref, dq_ref, dk_ref, dv_ref, left_ref, pg_ref):
        h = pl.program_id(0)
        i = pl.program_id(1)

        @pl.when(i == 0)
        def _():
            dk_ref[...] = jnp.zeros_like(dk_ref)
            dv_ref[...] = jnp.zeros_like(dv_ref)

        dq_ref[...] = jnp.zeros_like(dq_ref)
        left_ref[...] = jnp.zeros_like(left_ref)
        pg_ref[...] = jnp.zeros_like(pg_ref)
        upto = _tri(bk, lambda r, c: r <= c)

        def block(j, r0, diag):
            st = pl.multiple_of(j * bk, bk)
            kv, vv = k_ref[pl.ds(st, bk), :], v_ref[pl.ds(st, bk), :]
            qv = q_ref[r0:, :]
            dob = do_ref[r0:, :].astype(BF16)
            lb, l1m = _sb_scores(qv, kv)
            if diag:
                mask = _iota((bq - r0, bk), 1) + j * bk < _iota((bq - r0, bk), 0) + (r0 + i * bq)
                l1m = jnp.where(mask, l1m, 0.0)
            sums = _two_pass(l1m, upto)
            left = left_ref[r0:, :]
            a = jnp.exp(lb + jnp.tile(rt_ref[r0:, :] - left, (1, rep)) - sums)
            if diag:
                a = jnp.where(mask, a, 0.0)
            g = _dot(dob, vv, NT) * a
            dv_ref[pl.ds(st, bk), :] += _dot(a.astype(BF16), dob, TN)
            gsum = _two_pass(g, upto)
            pg = pg_ref[r0:, :]
            dz = g - jnp.exp(lb) * (jnp.tile(pg, (1, rep)) + gsum)
            if diag:
                dz = jnp.where(mask, dz, 0.0)
            dzb = (dz * scale).astype(BF16)
            dk_ref[pl.ds(st, bk), :] += _dot(dzb, qv, TN)
            dq_ref[r0:, :] += _dot(dzb, kv)
            left_ref[r0:, :] = left + _lanes(sums[:, bk - 1:bk])
            pg_ref[r0:, :] = pg + _lanes(gsum[:, bk - 1:bk])

        def step(u, carry):
            for s in range(nstep):
                block(u * nstep + s, 0, False)
            return carry

        lax.fori_loop(first_ref[h, i], i * steps_per_tile, step, 0)
        for s in range(nsub):
            block(i * nsub + s, s * bk, True)

    qb = pl.BlockSpec((bq, HEAD_DIM), lambda h, i: (i, h))
    full = pl.BlockSpec((t, HEAD_DIM), lambda h, i: (0, h))
    return pl.pallas_call(
        body,
        name="sb_bwd",
        grid=(HEADS, t // bq),
        in_specs=[pl.BlockSpec(memory_space=pltpu.SMEM), qb, full, full, qb, qb],
        out_specs=[qb, full, full],
        out_shape=[jax.ShapeDtypeStruct(q.shape, F32)] * 3,
        scratch_shapes=[pltpu.VMEM((bq, LANES), F32), pltpu.VMEM((bq, LANES), F32)],
        compiler_params=_params("arbitrary", "arbitrary"),
    )(first, q, k, v, rt, do)


def _adamw(w, g, m, v, name, tm=256):
    r, c = w.shape
    tm = tm if r % tm == 0 else r

    def body(w_ref, g_ref, m_ref, v_ref, d_ref, nm_ref, nv_ref):
        gv = g_ref[...]
        nm = ADAM_B1 * m_ref[...] + (1.0 - ADAM_B1) * gv
        nv = ADAM_B2 * v_ref[...] + (1.0 - ADAM_B2) * (gv * gv)
        m_hat = nm / (1.0 - ADAM_B1 ** ADAM_STEP)
        v_hat = nv / (1.0 - ADAM_B2 ** ADAM_STEP)
        d_ref[...] = -ADAM_LR * (m_hat / (jnp.sqrt(v_hat) + ADAM_EPS) + ADAM_WD * w_ref[...])
        nm_ref[...] = nm
        nv_ref[...] = nv

    blk = pl.BlockSpec((tm, c), lambda i: (i, 0))
    return pl.pallas_call(
        body,
        name=name,
        grid=(r // tm,),
        in_specs=[blk] * 4,
        out_specs=[blk] * 3,
        out_shape=[jax.ShapeDtypeStruct((r, c), F32)] * 3,
        compiler_params=_params("parallel"),
    )(w, g, m, v)


def _local_step(x, tgt, gains, small, shards, assemble_first, assemble, early_reduce=None, late_reduce=None):
    mix_pre, mix_post, mlp_pre, mlp_post, kv_gain = gains
    a_log, dt_bias, out_gain = small
    t, d = x.shape
    row = lambda a, i=None: a[i:i + 1] if i is not None else a
    al = jnp.zeros((1, LANES), F32).at[:, HEADS:2 * HEADS].set(a_log)
    dtb = jnp.zeros((1, LANES), F32).at[:, HEADS:2 * HEADS].set(dt_bias)
    og = jnp.tile(out_gain, (1, HEADS))
    full = lambda a: (a, a.shape[1], 0)

    h0, *gathered_first = _rowwise("norm_in", _fn_norm, [full(x)], [row(mix_pre, 0)], [(d, BF16)], gather=shards[0])
    w_qkvg, w_ba, conv_w = assemble_first(gathered_first)
    qkvg = _matmul(h0, w_qkvg, "nn", F32, "mm_gdn_in", tk=1024)
    ba = _matmul(h0, w_ba, "nn", F32, "mm_gdn_ba", tk=1024)
    (conv, gq, gk, gv), gathered_conv = _conv_fwd(qkvg, conv_w, shards[1])
    beta, gc, gt = _rowwise("gates", _fn_gates, [full(ba)], [al, dtb], [(LANES, F32)] * 3)
    gcr = jnp.swapaxes(gc[:, HEADS:2 * HEADS], 0, 1).reshape(HEADS, t // CHUNK, 1, CHUNK)
    (pw, pu, pqg, pkg, pattn, pgl, ptm), gathered_prep = _gdn_prep(gq, gk, gv, beta, gc, gt, gcr, shards[2])
    w_out, w_kv, w_q, w_o, w_up, w_down = assemble(gathered_conv, gathered_prep)
    w_qkvg_t, w_up_t, w_down_t = (jnp.swapaxes(a, -1, -2) for a in (w_qkvg, w_up, w_down))
    o_gdn, states = _gdn_scan(pw, pu, pqg, pkg, pattn, pgl)
    (on,) = _rowwise("out_norm", _fn_outnorm, [full(o_gdn), (qkvg, d, 3)], [og], [(d, BF16)])
    mix0 = _matmul(on, w_out, "nn", F32, "mm_gdn_out", tk=1024)
    x1, h1 = _rowwise("res_a0", _fn_res_norm, [full(x), full(mix0)], [row(mix_post, 0), row(mlp_pre, 0)], [(d, F32), (d, BF16)])
    (a0,) = _matmul(h1, w_up[0], "nn", (BF16,), "mm_up0", tk=1024, epilogue=_relu2_of)
    d0 = _matmul(a0, w_down[0], "nn", F32, "mm_down0")
    x2, hkv, hq = _rowwise("res_b0", _fn_res_norm2, [full(x1), full(d0)], [row(mlp_post, 0), kv_gain, row(mix_pre, 1)], [(d, F32), (d, BF16), (d, BF16)])
    w_k, w_v = w_kv[:, :d], w_kv[:, d:]
    kp = _matmul(hkv, w_k, "nn", BF16, "mm_k", tk=1024)
    vp = _matmul(hkv, w_v, "nn", BF16, "mm_v", tk=1024)
    qp = _matmul(hq, w_q, "nn", BF16, "mm_q", tk=1024)
    o_sb, rt, sb_first = _sb_fwd(qp, kp, vp)
    mix1 = _matmul(o_sb, w_o, "nn", F32, "mm_sb_out", tk=1024)
    x3, h3 = _rowwise("res_a1", _fn_res_norm, [full(x2), full(mix1)], [row(mix_post, 1), row(mlp_pre, 1)], [(d, F32), (d, BF16)])
    (a1,) = _matmul(h3, w_up[1], "nn", (BF16,), "mm_up1", tk=1024, epilogue=_relu2_of)
    d1 = _matmul(a1, w_down[1], "nn", F32, "mm_down1")

    loss, dx3, dd1, g_mlp_post1 = _loss_call(x3, d1, tgt, row(mlp_post, 1))
    (du1,) = _matmul(dd1, w_down_t[1], "nn", (BF16,), "mm_down1_dx", epilogue=_relu2_cotangent, extras=[a1])
    g_down1 = _matmul(a1, dd1, "tn", F32, "mm_down1_dw")
    dh3 = _matmul(du1, w_up_t[1], "nn", F32, "mm_up1_dx")
    g_up1 = _matmul(h3, du1, "tn", F32, "mm_up1_dw")
    (dx2, dmix1), (g_mix_post1, g_mlp_pre1), _ = _rowwise_bwd(
        "res_a1_bwd", _fn_res_norm, [full(x2), full(mix1)], [row(mix_post, 1), row(mlp_pre, 1)], [dx3, dh3], [F32, BF16])
    do_sb = _matmul(dmix1, w_o, "nt", BF16, "mm_sb_out_dx")
    g_o = _matmul(o_sb, dmix1, "tn", F32, "mm_sb_out_dw")
    dqp, dkp, dvp = _sb_bwd(qp, kp, vp, rt, sb_first, do_sb)
    dhq = _matmul(dqp, w_q, "nt", F32, "mm_q_dx")
    g_q = _matmul(hq, dqp, "tn", F32, "mm_q_dw")
    dhkv = _matmul(dvp, w_v, "nt", F32, "mm_v_dx", add=_matmul(dkp, w_k, "nt", F32, "mm_k_dx"))
    g_kv = jnp.concatenate([_matmul(hkv, dkp, "tn", F32, "mm_k_dw"), _matmul(hkv, dvp, "tn", F32, "mm_v_dw")], axis=1)
    (dx1, dd0), (g_mlp_post0, g_kv_gain, g_mix_pre1), _ = _rowwise_bwd(
        "res_b0_bwd", _fn_res_norm2, [full(x1), full(d0)], [row(mlp_post, 0), kv_gain, row(mix_pre, 1)], [dx2, dhkv, dhq], [F32, BF16])
    (du0,) = _matmul(dd0, w_down_t[0], "nn", (BF16,), "mm_down0_dx", epilogue=_relu2_cotangent, extras=[a0])
    g_down0 = _matmul(a0, dd0, "tn", F32, "mm_down0_dw")
    dh1 = _matmul(du0, w_up_t[0], "nn", F32, "mm_up0_dx")
    g_up0 = _matmul(h1, du0, "tn", F32, "mm_up0_dw")
    (dx0, dmix0), (g_mix_post0, g_mlp_pre0), _ = _rowwise_bwd(
        "res_a0_bwd", _fn_res_norm, [full(x), full(mix0)], [row(mix_post, 0), row(mlp_pre, 0)], [dx1, dh1], [F32, BF16])
    don = _matmul(dmix0, w_out, "nt", F32, "mm_gdn_out_dx")
    g_out = _matmul(on, dmix0, "tn", F32, "mm_gdn_out_dw")
    (do_gdn, dgate), (g_og,), _ = _rowwise_bwd("out_norm_bwd", _fn_outnorm, [full(o_gdn), (qkvg, d, 3)], [og], [don], [F32, F32])
    dpw, dpu, dpqg, dpkg, dpattn, dpgl = _gdn_scan_bwd(pw, pu, pqg, pkg, pattn, pgl, states, do_gdn)
    partial, partial_bf16 = [], ()
    if early_reduce is not None:
        partial, partial_bf16 = early_reduce(dict(mlp_w_up=(g_up0, g_up1), mlp_w_down=(g_down0, g_down1), gdn_w_out=g_out[None], w_kv=g_kv, sb_w_q=g_q[None], sb_w_o=g_o[None]))
    (dgq, dgk, dgv, dbeta, dgc, dgt, dgcr), from_chips = _gdn_prep_bwd(gq, gk, gv, beta, gc, gt, gcr, ptm, dpw, dpu, dpqg, dpkg, dpattn, dpgl, partial_bf16)
    dgcr_lanes = jnp.pad(jnp.swapaxes(dgcr.reshape(HEADS, t), 0, 1), ((0, 0), (HEADS, LANES - 2 * HEADS)))
    gate_cots = [dbeta, dgc + dgcr_lanes, dgt]
    (dba,), (g_al, g_dtb), _ = _rowwise_bwd("gates_bwd", _fn_gates, [full(ba)], [al, dtb], gate_cots, [BF16])
    dqkvg, g_conv = _conv_bwd(conv, (dgq, dgk, dgv), dgate, qkvg, conv_w)
    dh0b = _matmul(dba, w_ba, "nt", F32, "mm_gdn_ba_dx", tk=LANES)
    dh0 = _matmul(dqkvg, w_qkvg_t, "nn", F32, "mm_gdn_in_dx", add=dh0b)
    g_qkvg = _matmul(h0, dqkvg, "tn", F32, "mm_gdn_in_dw")
    g_ba = _matmul(h0, dba, "tn", F32, "mm_gdn_ba_dw")
    g_w_in = jnp.concatenate([g_qkvg, g_ba[:, :2 * HEADS]], axis=1)[None]
    partial_late, partial_late_bf16 = late_reduce(dict(gdn_w_in=g_w_in)) if late_reduce is not None else ([], ())
    (grad_x,), (g_mix_pre0,), from_chips_late = _rowwise_bwd(
        "norm_in_bwd", lambda xx, gg: (_rms(xx, gg), xx), [full(x)], [row(mix_pre, 0)], [dh0, dx0], [F32], partials=partial_late_bf16)

    grads = dict(
        mix_pre_gain=jnp.concatenate([g_mix_pre0, g_mix_pre1], axis=0),
        mix_post_gain=jnp.concatenate([g_mix_post0, g_mix_post1], axis=0),
        mlp_pre_gain=jnp.concatenate([g_mlp_pre0, g_mlp_pre1], axis=0),
        mlp_post_gain=jnp.concatenate([g_mlp_post0, g_mlp_post1], axis=0),
        mlp_w_up=(g_up0, g_up1),
        mlp_w_down=(g_down0, g_down1),
        gdn_w_in=g_w_in,
        gdn_conv_w=g_conv[None, :CONV_K],
        gdn_a_log=g_al[:, HEADS:2 * HEADS],
        gdn_dt_bias=g_dtb[:, HEADS:2 * HEADS],
        gdn_out_gain=jnp.sum(g_og.reshape(HEADS, HEAD_DIM), axis=0, keepdims=True),
        gdn_w_out=g_out[None],
        kv_gain=g_kv_gain[0],
        w_kv=g_kv,
        sb_w_q=g_q[None],
        sb_w_o=g_o[None],
    )
    return loss, grad_x, grads, (list(partial) + list(partial_late), list(from_chips) + list(from_chips_late))


N_DEV = 8
N_CHIPS = 4
PACK_ROW_TILE = 128

_HBM = pl.BlockSpec(memory_space=pltpu.HBM)


def _place():
    return lax.axis_index("x"), lax.axis_index("y"), lax.axis_index("c")


def _other_chips(x, y):
    return [(1 - x, y), (x, 1 - y), (1 - x, 1 - y)]


def _remote(src, dst, send_sem, recv_sem, to):
    return pltpu.make_async_remote_copy(src_ref=src, dst_ref=dst, send_sem=send_sem, recv_sem=recv_sem, device_id=to, device_id_type=MESH)


def _gather8(v, name):
    rows, cols = v.shape

    def body(v_ref, out_ref, sum_ref, send_sems, recv_sems, local_sem):
        x, y, c = _place()
        me, sibling = (x, y, c), (x, y, 1 - c)
        chips = _other_chips(x, y)

        def blk(px, py, pc):
            return out_ref.at[pl.ds((4 * px + 2 * py + pc) * rows, rows), :]

        def copy(k, block, to, src=None):
            return _remote(blk(*block) if src is None else src, blk(*block), send_sems.at[k], recv_sems.at[k], to)

        mine = pltpu.make_async_copy(v_ref, blk(*me), local_sem)
        mine.start()
        first = [copy(0, me, sibling, src=v_ref)] + [copy(1 + j, me, (*chip, c), src=v_ref) for j, chip in enumerate(chips)]
        for cp in first:
            cp.start()
        passed = [copy(4 + j, (*chip, c), sibling) for j, chip in enumerate(chips)]
        for j, chip in enumerate(chips):
            copy(1 + j, (*chip, c), me).wait_recv()
            passed[j].start()
        copy(0, sibling, me).wait_recv()
        for j, chip in enumerate(chips):
            copy(4 + j, (*chip, 1 - c), me).wait_recv()
        for cp in first + passed:
            cp.wait_send()
        mine.wait()
        acc = out_ref[pl.ds(0, rows), :]
        for dev in range(1, N_DEV):
            acc = acc + out_ref[pl.ds(dev * rows, rows), :]
        sum_ref[...] = acc

    vm = pl.BlockSpec(memory_space=pltpu.VMEM)
    return pl.pallas_call(
        body,
        name=name,
        out_shape=[jax.ShapeDtypeStruct((N_DEV * rows, cols), v.dtype), jax.ShapeDtypeStruct((rows, cols), v.dtype)],
        in_specs=[vm],
        out_specs=[vm, vm],
        scratch_shapes=[pltpu.SemaphoreType.DMA((7,)), pltpu.SemaphoreType.DMA((7,)), pltpu.SemaphoreType.DMA],
    )(v)


def _hbm_call(body, name, arrs, out_shapes, sem_counts):
    n = len(arrs)

    def wrapped(*refs):
        body(refs[:n], refs[n:2 * n], *refs[2 * n:])

    return pl.pallas_call(
        wrapped,
        name=name,
        out_shape=[jax.ShapeDtypeStruct(s, a.dtype) for s, a in zip(out_shapes, arrs)],
        in_specs=[_HBM] * n,
        out_specs=[_HBM] * n,
        scratch_shapes=[pltpu.SemaphoreType.DMA((k,)) for k in sem_counts],
    )(*arrs)


def _gather_sends(w_refs, out_refs, send_sems, recv_sems):
    x, y, c = _place()
    s_me = 2 * x + y
    return [_remote(w.at[c], o.at[s_me, c], send_sems.at[3 * a + j], recv_sems.at[3 * a + j], (px, py, c))
            for a, (w, o) in enumerate(zip(w_refs, out_refs)) for j, (px, py) in enumerate(_other_chips(x, y))]


def _gather_finish(w_refs, out_refs, send_sems, recv_sems, fsend_sems, frecv_sems):
    x, y, c = _place()
    chips = _other_chips(x, y)
    passed = []
    for a, o in enumerate(out_refs):
        for j, (px, py) in enumerate(chips):
            half = o.at[2 * px + py, c]
            _remote(half, half, send_sems.at[3 * a + j], recv_sems.at[3 * a + j], (px, py, c)).wait_recv()
            fwd = _remote(half, half, fsend_sems.at[3 * a + j], frecv_sems.at[3 * a + j], (x, y, 1 - c))
            fwd.start()
            passed.append(fwd)
    for a, o in enumerate(out_refs):
        for j, (px, py) in enumerate(chips):
            half = o.at[2 * px + py, 1 - c]
            _remote(half, half, fsend_sems.at[3 * a + j], frecv_sems.at[3 * a + j], (x, y, 1 - c)).wait_recv()
    for cp in _gather_sends(w_refs, out_refs, send_sems, recv_sems) + passed:
        cp.wait_send()


def _swap_halves(arrs, name):
    n = len(arrs)

    def body(g_refs, a_refs, send_sems, recv_sems):
        x, y, c = _place()
        cps = [_remote(g.at[1 - c], a, send_sems.at[i], recv_sems.at[i], (x, y, 1 - c)) for i, (g, a) in enumerate(zip(g_refs, a_refs))]
        for cp in cps:
            cp.start()
        for cp in cps:
            cp.wait()

    return _hbm_call(body, name, arrs, [a.shape[1:] for a in arrs], [n, n])


def _scatter_copies(p_refs, b_refs, send_sems, recv_sems):
    x, y, c = _place()
    return [_remote(p.at[2 * px + py], b.at[j], send_sems.at[3 * i + j], recv_sems.at[3 * i + j], (px, py, c))
            for i, (p, b) in enumerate(zip(p_refs, b_refs)) for j, (px, py) in enumerate(_other_chips(x, y))]


def _share_halves(arrs):
    n = len(arrs)

    def body(q_refs, out_refs, send_sems, recv_sems):
        x, y, c = _place()
        cps = [_remote(q, o, send_sems.at[i], recv_sems.at[i], (x, y, 1 - c)) for i, (q, o) in enumerate(zip(q_refs, out_refs))]
        for cp in cps:
            cp.start()
        for cp in cps:
            cp.wait()

    return _hbm_call(body, "grads_share", arrs, [a.shape for a in arrs], [n, n])


_GROUPS = (
    (("gdn_w_out", (1, 256, 1024), "rows"), ("mlp_w_up", (2, 1024, 1024), "cols")),
    (("mlp_w_down", (2, 1024, 1024), "rows"), ("sb_w_q", (1, 256, 1024), "rows"), ("sb_w_o", (1, 256, 1024), "rows")),
    (("w_kv", (1024, 512), "cols"),),
    (("gdn_w_in", (1, 1024, 1028), "cols"),),
)
_BEHIND_CONV, _BEHIND_PREP, _FIRST = slice(0, 1), slice(1, 3), slice(3, 4)
_EARLY_GRADS = slice(0, 3)


def _numel(shape):
    n = 1
    for s in shape:
        n *= s
    return n


def _half_rows(shape):
    return _numel(shape[:-1]) // 2


def _pack_shards(shards, dtype):
    return tuple(jnp.concatenate([shards[n].astype(dtype).reshape(2, _half_rows(shape), shape[-1]) for n, shape, _ in grp], axis=1) for grp in _GROUPS)


def _unpack_shards(bufs):
    out = {}
    for grp, buf in zip(_GROUPS, bufs):
        off = 0
        for n, shape, _ in grp:
            out[n] = buf[:, off:off + _half_rows(shape)].reshape(shape)
            off += _half_rows(shape)
    return out


def _join(stacked, how):
    nd = stacked.ndim - 1
    ax = nd - 1 if how == "cols" else nd - 2
    moved = jnp.moveaxis(stacked, 0, ax)
    shape = list(stacked.shape[1:])
    shape[ax] *= N_CHIPS
    return moved.reshape(shape)


def _split(full, shard_shape, how):
    nd = len(shard_shape)
    ax = nd - 1 if how == "cols" else nd - 2
    shape = list(shard_shape)
    shape.insert(ax, N_CHIPS)
    return jnp.moveaxis(full.reshape(shape), ax, 0)


def _unpack_full(gathered, groups):
    out = {}
    for grp, buf in zip(groups, gathered):
        off = 0
        for n, shape, how in grp:
            out[n] = _join(buf[:, :, off:off + _half_rows(shape)].reshape((N_CHIPS,) + shape), how)
            off += _half_rows(shape)
    return out


def _pack_full(full, groups):
    bufs = []
    for grp in groups:
        parts = []
        for n, shape, how in grp:
            if isinstance(full[n], tuple):
                assert len(full[n]) == shape[0] == 2
                parts.append(jnp.stack([_split(layer, shape[1:], how) for layer in full[n]], axis=1))
            else:
                parts.append(_split(full[n], shape, how).reshape(N_CHIPS, 2, _half_rows(shape), shape[-1]))
        buf = jnp.swapaxes(jnp.concatenate(parts, axis=2), 0, 1)
        bufs.append(buf.reshape(2, -1, buf.shape[-1]))
    return tuple(bufs)


_SMALL = (
    ("mix_pre_gain", (2, 1024)),
    ("mix_post_gain", (2, 1024)),
    ("mlp_pre_gain", (2, 1024)),
    ("mlp_post_gain", (2, 1024)),
    ("kv_gain", (1024,)),
    ("gdn_out_gain", (1, 128)),
    ("gdn_a_log", (1, 8)),
    ("gdn_dt_bias", (1, 8)),
    ("gdn_conv_w", (1, 4, 3072)),
    ("loss", ()),
)


def _rows_of(shape):
    return -(-_numel(shape) // LANES)


def _pack_rows(vals, layout):
    parts = []
    for n, shape in layout:
        flat = vals[n].reshape(-1)
        parts.append(jnp.pad(flat, (0, _rows_of(shape) * LANES - flat.shape[0])))
    flat = jnp.concatenate(parts)
    rows = -(-flat.shape[0] // (8 * LANES)) * 8
    return jnp.pad(flat, (0, rows * LANES - flat.shape[0])).reshape(rows, LANES)


def _unpack_rows(packed, layout):
    flat = packed.reshape(-1)
    out, off = {}, 0
    for n, shape in layout:
        out[n] = flat[off:off + _numel(shape)].reshape(shape)
        off += _rows_of(shape) * LANES
    return out


_WEIGHTS = ("mix_pre_gain", "mix_post_gain", "mlp_pre_gain", "mlp_post_gain", "mlp_w_up", "mlp_w_down", "gdn_w_in", "gdn_conv_w",
            "gdn_a_log", "gdn_dt_bias", "gdn_out_gain", "gdn_w_out", "kv_gain", "w_kv", "sb_w_q", "sb_w_o")


def _as2d(a):
    return a.reshape(1, -1) if a.ndim <= 1 else a.reshape(-1, a.shape[-1])


def kernel(x, mix_pre_gain, mix_post_gain, mlp_pre_gain, mlp_post_gain, mlp_w_up, mlp_w_down, gdn_w_in, gdn_conv_w, gdn_a_log, gdn_dt_bias, gdn_out_gain, gdn_w_out, kv_gain, w_kv, sb_w_q, sb_w_o, loss_target, m_mix_pre_gain, m_mix_post_gain, m_mlp_pre_gain, m_mlp_post_gain, m_mlp_w_up, m_mlp_w_down, m_gdn_w_in, m_gdn_conv_w, m_gdn_a_log, m_gdn_dt_bias, m_gdn_out_gain, m_gdn_w_out, m_kv_gain, m_w_kv, m_sb_w_q, m_sb_w_o, v_mix_pre_gain, v_mix_post_gain, v_mlp_pre_gain, v_mlp_post_gain, v_mlp_w_up, v_mlp_w_down, v_gdn_w_in, v_gdn_conv_w, v_gdn_a_log, v_gdn_dt_bias, v_gdn_out_gain, v_gdn_w_out, v_kv_gain, v_w_kv, v_sb_w_q, v_sb_w_o):
    w = dict(mix_pre_gain=mix_pre_gain, mix_post_gain=mix_post_gain, mlp_pre_gain=mlp_pre_gain, mlp_post_gain=mlp_post_gain, mlp_w_up=mlp_w_up, mlp_w_down=mlp_w_down, gdn_w_in=gdn_w_in, gdn_conv_w=gdn_conv_w, gdn_a_log=gdn_a_log, gdn_dt_bias=gdn_dt_bias, gdn_out_gain=gdn_out_gain, gdn_w_out=gdn_w_out, kv_gain=kv_gain, w_kv=w_kv, sb_w_q=sb_w_q, sb_w_o=sb_w_o)
    m = dict(mix_pre_gain=m_mix_pre_gain, mix_post_gain=m_mix_post_gain, mlp_pre_gain=m_mlp_pre_gain, mlp_post_gain=m_mlp_post_gain, mlp_w_up=m_mlp_w_up, mlp_w_down=m_mlp_w_down, gdn_w_in=m_gdn_w_in, gdn_conv_w=m_gdn_conv_w, gdn_a_log=m_gdn_a_log, gdn_dt_bias=m_gdn_dt_bias, gdn_out_gain=m_gdn_out_gain, gdn_w_out=m_gdn_w_out, kv_gain=m_kv_gain, w_kv=m_w_kv, sb_w_q=m_sb_w_q, sb_w_o=m_sb_w_o)
    v = dict(mix_pre_gain=v_mix_pre_gain, mix_post_gain=v_mix_post_gain, mlp_pre_gain=v_mlp_pre_gain, mlp_post_gain=v_mlp_post_gain, mlp_w_up=v_mlp_w_up, mlp_w_down=v_mlp_w_down, gdn_w_in=v_gdn_w_in, gdn_conv_w=v_gdn_conv_w, gdn_a_log=v_gdn_a_log, gdn_dt_bias=v_gdn_dt_bias, gdn_out_gain=v_gdn_out_gain, gdn_w_out=v_gdn_w_out, kv_gain=v_kv_gain, w_kv=v_w_kv, sb_w_q=v_sb_w_q, sb_w_o=v_sb_w_o)
    cx, cy, cc = _place()
    chip = 2 * cx + cy
    conv_cols = gdn_conv_w.shape[-1]

    own = _pack_shards(w, BF16)
    own_taps = jnp.pad(gdn_conv_w[0], ((0, CONV_K), (0, 0))).reshape(2, CONV_K, conv_cols)
    with_own = lambda gathered, mine: [lax.dynamic_update_index_in_dim(g, m, chip, 0) for g, m in zip(gathered, mine)]

    def assemble_first(gathered):
        w_in_all, taps_all = with_own(gathered, (*own[_FIRST], own_taps))
        w_in = _unpack_full([w_in_all], _GROUPS[_FIRST])["gdn_w_in"][0]
        taps = jnp.swapaxes(taps_all[:, 0], 0, 1).reshape(CONV_K, N_CHIPS * conv_cols)
        return w_in[:, :4 * HEADS * HEAD_DIM], jnp.pad(w_in[:, 4 * HEADS * HEAD_DIM:], ((0, 0), (0, LANES - 2 * HEADS))), taps

    def assemble(gathered_conv, gathered_prep):
        full = {**_unpack_full(with_own(gathered_conv, own[_BEHIND_CONV]), _GROUPS[_BEHIND_CONV]),
                **_unpack_full(with_own(gathered_prep, own[_BEHIND_PREP]), _GROUPS[_BEHIND_PREP])}
        return full["gdn_w_out"][0], full["w_kv"], full["sb_w_q"][0], full["sb_w_o"][0], full["mlp_w_up"], full["mlp_w_down"]

    gains = (mix_pre_gain, mix_post_gain, mlp_pre_gain, mlp_post_gain, kv_gain[None])
    small = (gdn_a_log, gdn_dt_bias, gdn_out_gain)
    tile = PACK_ROW_TILE

    def to_chip_partials(grads_full, groups, tag):
        bufs = _pack_full(grads_full, groups)
        p32, p16 = [], []
        for i, (buf, other) in enumerate(zip(bufs, _swap_halves(bufs, f"grads_to_sibling_{tag}"))):
            _, n, cols = buf.shape
            p, pb = _add_rows(f"grads_add_sibling_{tag}{i}", [(buf.reshape(2 * n, cols), cc * (n // tile)), (other, 0)], n, (F32, BF16), tile)
            p32.append(p.reshape(N_CHIPS, -1, cols))
            p16.append(pb.reshape(N_CHIPS, -1, cols))
        return p32, tuple(p16)

    loss_rows, grad_x, g_full, (partial, from_chips) = _local_step(
        x[0], loss_target[0], gains, small, ((*own[_FIRST], own_taps), own[_BEHIND_CONV], own[_BEHIND_PREP]), assemble_first, assemble,
        lambda g: to_chip_partials(g, _GROUPS[_EARLY_GRADS], "early"), lambda g: to_chip_partials(g, _GROUPS[_FIRST], "late"))

    reduced = []
    for i, (p, others) in enumerate(zip(partial, from_chips)):
        _, r, cols = p.shape
        terms = [(p.reshape(N_CHIPS * r, cols), chip * (r // tile))] + [(others.reshape(3 * r, cols), j * (r // tile)) for j in range(3)]
        reduced.append(_add_rows(f"grads_add_chips_{i}", terms, r, (F32,), tile)[0])
    g_shard = _unpack_shards([jnp.where(cc == 0, jnp.stack([r, o]), jnp.stack([o, r])) for r, o in zip(reduced, _share_halves(tuple(reduced)))])

    g_small_local = {n: g_full[n] for n, _ in _SMALL if n != "loss"}
    g_small_local["loss"] = loss_rows[0, 0]
    _, small_sum = _gather8(_pack_rows(g_small_local, _SMALL), "allreduce_small")
    g_small = _unpack_rows(small_sum, _SMALL)
    loss = g_small.pop("loss")
    g_small["gdn_conv_w"] = lax.dynamic_slice_in_dim(g_small["gdn_conv_w"], chip * conv_cols, conv_cols, axis=2)

    grads = {**g_shard, **g_small}
    deltas, new_m, new_v = {}, {}, {}
    for n in g_shard:
        d2, m2, v2 = _adamw(_as2d(w[n]), _as2d(grads[n]), _as2d(m[n]), _as2d(v[n]), "adamw_" + n)
        deltas[n], new_m[n], new_v[n] = d2.reshape(w[n].shape), m2.reshape(w[n].shape), v2.reshape(w[n].shape)
    small_layout = tuple((n, w[n].shape) for n in g_small)
    packed = _adamw(*[_pack_rows(vals, small_layout) for vals in (w, grads, m, v)], "adamw_small")
    for out, rows in zip((deltas, new_m, new_v), packed):
        out.update(_unpack_rows(rows, small_layout))
    return (loss, grad_x[None], *[grads[n].reshape(w[n].shape) for n in _WEIGHTS], *[deltas[n] for n in _WEIGHTS],
            *[new_m[n] for n in _WEIGHTS], *[new_v[n] for n in _WEIGHTS])
```

```python
import functools

import jax
import jax.numpy as jnp
from jax import lax
from jax.experimental import pallas as pl
from jax.experimental.pallas import tpu as pltpu

F32, BF16 = jnp.float32, jnp.bfloat16
HI = lax.Precision.HIGHEST
MESH = pl.DeviceIdType.MESH

EPS = 1e-6
HEADS = 8
HEAD_DIM = 128
CHUNK = 64
CHUNK_SHIFT = CHUNK.bit_length() - 1
CONV_K = 4
QKV = 3 * HEADS * HEAD_DIM

ADAM_LR, ADAM_B1, ADAM_B2, ADAM_EPS, ADAM_WD, ADAM_STEP = 0.001, 0.9, 0.999, 1e-08, 0.01, 10

VMEM_LIMIT_BYTES = 48 * 1024 * 1024
LANES = 128

NN = ((1,), (0,))
NT = ((1,), (1,))
TN = ((0,), (0,))


def _dot(a, b, dims=NN, precision=None):
    return lax.dot_general(a, b, (dims, ((), ())), precision=precision, preferred_element_type=F32)


def _params(*sem):
    return pltpu.CompilerParams(dimension_semantics=sem, vmem_limit_bytes=VMEM_LIMIT_BYTES)


def _iota(shape, axis):
    return lax.broadcasted_iota(jnp.int32, shape, axis)


def _matmul(a, b, mode, out_dtype, name, tm=1024, tn=1024, tk=2048, add=None, epilogue=None, extras=()):
    if mode == "nn":
        (m, k), (k2, n) = a.shape, b.shape
    elif mode == "nt":
        (m, k), (n, k2) = a.shape, b.shape
    else:
        (k, m), (k2, n) = a.shape, b.shape
    assert k == k2, (a.shape, b.shape, mode)
    tm, tn, tk = min(tm, m), min(tn, n), min(tk, k)
    assert m % tm == 0 and n % tn == 0 and k % tk == 0, (a.shape, b.shape, mode)
    nk = k // tk
    dims = {"nn": NN, "nt": NT, "tn": TN}[mode]
    tiles = ([add] if add is not None else []) + list(extras)
    out_dtypes = out_dtype if epilogue is not None else (out_dtype,)
    n_in = 2 + len(tiles)

    def finish(acc, extra_refs, o_refs):
        res = (acc,) if epilogue is None else epilogue(acc, *[r[...] for r in extra_refs])
        for o_ref, r in zip(o_refs, res):
            o_ref[...] = r.astype(o_ref.dtype)

    def body(*refs):
        a_ref, b_ref = refs[:2]
        extra_refs = refs[n_in - len(extras):n_in]
        o_refs, acc_ref = refs[n_in:-1], refs[-1]
        prod = _dot(a_ref[...].astype(BF16), b_ref[...].astype(BF16), dims)
        if nk == 1:
            finish(prod + refs[2][...].astype(F32) if add is not None else prod, extra_refs, o_refs)
            return
        kk = pl.program_id(2)

        @pl.when(kk == 0)
        def _():
            acc_ref[...] = refs[2][...].astype(F32) if add is not None else jnp.zeros_like(acc_ref)

        acc_ref[...] += prod

        @pl.when(kk == nk - 1)
        def _():
            finish(acc_ref[...], extra_refs, o_refs)

    a_spec = pl.BlockSpec((tk, tm), lambda i, j, kk: (kk, i)) if mode == "tn" else pl.BlockSpec((tm, tk), lambda i, j, kk: (i, kk))
    b_spec = pl.BlockSpec((tn, tk), lambda i, j, kk: (j, kk)) if mode == "nt" else pl.BlockSpec((tk, tn), lambda i, j, kk: (kk, j))
    o_spec = pl.BlockSpec((tm, tn), lambda i, j, kk: (i, j))
    res = pl.pallas_call(
        body,
        name=name,
        grid=(m // tm, n // tn, nk),
        in_specs=[a_spec, b_spec] + [o_spec] * len(tiles),
        out_specs=[o_spec] * len(out_dtypes),
        out_shape=[jax.ShapeDtypeStruct((m, n), dt) for dt in out_dtypes],
        scratch_shapes=[pltpu.VMEM((tm, tn), F32)],
        compiler_params=_params("parallel", "parallel", "arbitrary"),
    )(a, b, *tiles)
    return res if epilogue is not None else res[0]


def _row_specs(rows, tm):
    return [pl.BlockSpec((tm, w), lambda i, cb=cb: (i, cb)) for _, w, cb in rows]


def _full_spec(p):
    return pl.BlockSpec(p.shape, lambda i: (0,) * p.ndim)


ROW_TILE = 512


def _rowwise(name, fn, rows, params, outs, tm=ROW_TILE, gather=()):
    t = rows[0][0].shape[0]
    tm = min(tm, t)
    steps = t // tm
    nr, npar, nout, ng = len(rows), len(params), len(outs), len(gather)

    def body(*refs):
        ins = [r[...].astype(F32) for r in refs[:nr]]
        ps = [p[...] for p in refs[nr:nr + npar]]
        shard_refs = refs[nr + npar:nr + npar + ng]
        o_refs = refs[nr + npar + ng:nr + npar + ng + nout]
        all_refs, sems = refs[nr + npar + ng + nout:nr + npar + 2 * ng + nout], refs[nr + npar + 2 * ng + nout:]
        if ng:
            @pl.when(pl.program_id(0) == 0)
            def _():
                for cp in _gather_sends(shard_refs, all_refs, *sems[:2]):
                    cp.start()

        res = fn(*ins, *ps)
        for o_ref, r in zip(o_refs, res):
            o_ref[...] = r.astype(o_ref.dtype)

        if ng:
            @pl.when(pl.program_id(0) == steps - 1)
            def _():
                _gather_finish(shard_refs, all_refs, *sems)

    return pl.pallas_call(
        body,
        name=name,
        grid=(steps,),
        in_specs=_row_specs(rows, tm) + [_full_spec(p) for p in params] + [_HBM] * ng,
        out_specs=[pl.BlockSpec((tm, w), lambda i: (i, 0)) for w, _ in outs] + [_HBM] * ng,
        out_shape=[jax.ShapeDtypeStruct((t, w), dt) for w, dt in outs] + [jax.ShapeDtypeStruct((N_CHIPS,) + s.shape, s.dtype) for s in gather],
        scratch_shapes=[pltpu.SemaphoreType.DMA((3 * ng,))] * (4 if ng else 0),
        compiler_params=_params("arbitrary" if ng else "parallel"),
    )(*[r[0] for r in rows], *params, *gather)


def _add_rows(name, terms, n_rows, out_dtypes, tm):
    cols = terms[0][0].shape[1]
    firsts = jnp.stack([jnp.asarray(first, jnp.int32) for _, first in terms])

    def body(firsts_ref, *refs):
        acc = refs[0][...].astype(F32)
        for r in refs[1:len(terms)]:
            acc = acc + r[...].astype(F32)
        for o_ref in refs[len(terms):]:
            o_ref[...] = acc.astype(o_ref.dtype)

    return pl.pallas_call(
        body,
        name=name,
        grid_spec=pltpu.PrefetchScalarGridSpec(
            num_scalar_prefetch=1,
            grid=(n_rows // tm,),
            in_specs=[pl.BlockSpec((tm, cols), lambda i, firsts_ref, k=k: (firsts_ref[k] + i, 0)) for k in range(len(terms))],
            out_specs=[pl.BlockSpec((tm, cols), lambda i, firsts_ref: (i, 0)) for _ in out_dtypes],
        ),
        out_shape=[jax.ShapeDtypeStruct((n_rows, cols), dt) for dt in out_dtypes],
        compiler_params=_params("parallel"),
    )(firsts, *[a for a, _ in terms])


def _rowwise_bwd(name, fn, rows, params, cots, grad_dtypes, tm=ROW_TILE, partials=()):
    t = rows[0][0].shape[0]
    tm = min(tm, t)
    steps = t // tm
    nr, npar, nc, nsc = len(rows), len(params), len(cots), len(partials)
    want = [j for j, dt in enumerate(grad_dtypes) if dt is not None]
    widths = [rows[j][1] for j in want]
    n_row_outs = len(want)
    n_in = nr + npar + nc

    def body(*refs):
        i = pl.program_id(0)
        ins = [r[...].astype(F32) for r in refs[:nr]]
        ps = [p[...] for p in refs[nr:nr + npar]]
        cs = tuple(c[...].astype(F32) for c in refs[nr + npar:n_in])
        p_refs = refs[n_in:n_in + nsc]
        outs = refs[n_in + nsc:n_in + nsc + n_row_outs + npar]
        from_refs, sems = refs[n_in + nsc + n_row_outs + npar:n_in + 2 * nsc + n_row_outs + npar], refs[n_in + 2 * nsc + n_row_outs + npar:]
        if nsc:
            @pl.when(i == 0)
            def _():
                for cp in _scatter_copies(p_refs, from_refs, *sems):
                    cp.start()

        _, vjp = jax.vjp(fn, *ins, *ps)
        gs = vjp(cs)
        for o_ref, j in zip(outs, want):
            o_ref[...] = gs[j].astype(o_ref.dtype)
        pg_refs = outs[n_row_outs:]

        @pl.when(i == 0)
        def _():
            for pg in pg_refs:
                pg[...] = jnp.zeros_like(pg)

        for pg, g in zip(pg_refs, gs[nr:]):
            pg[...] += g

        if nsc:
            @pl.when(i == steps - 1)
            def _():
                for cp in _scatter_copies(p_refs, from_refs, *sems):
                    cp.wait()

    row_specs = [pl.BlockSpec((tm, w), lambda i: (i, 0)) for w in widths]
    row_shapes = [jax.ShapeDtypeStruct((t, w), grad_dtypes[j]) for j, w in zip(want, widths)]
    res = pl.pallas_call(
        body,
        name=name,
        grid=(steps,),
        in_specs=_row_specs(rows, tm) + [_full_spec(p) for p in params] + [pl.BlockSpec((tm, c.shape[1]), lambda i: (i, 0)) for c in cots] + [_HBM] * nsc,
        out_specs=row_specs + [_full_spec(p) for p in params] + [_HBM] * nsc,
        out_shape=row_shapes + [jax.ShapeDtypeStruct(p.shape, F32) for p in params] + [jax.ShapeDtypeStruct((3,) + p.shape[1:], p.dtype) for p in partials],
        scratch_shapes=[pltpu.SemaphoreType.DMA((3 * nsc,))] * (2 if nsc else 0),
        compiler_params=_params("arbitrary"),
    )(*[r[0] for r in rows], *params, *cots, *partials)
    return res[:n_row_outs], res[n_row_outs:n_row_outs + npar], res[n_row_outs + npar:]


def _rms(x, g):
    return x * lax.rsqrt(jnp.mean(x * x, axis=-1, keepdims=True) + EPS) * g


def _sigmoid(x):
    return 1.0 / (1.0 + jnp.exp(-x))


def _softplus(x):
    return jnp.maximum(x, 0.0) + jnp.log1p(jnp.exp(-jnp.abs(x)))


def _two_pass(x, m):
    hi = x.astype(BF16)
    lo = (x - hi.astype(F32)).astype(BF16)
    return _dot(hi, m) + _dot(lo, m)


def _head_sum_impl(x):
    sums = [jnp.sum(x[:, h * HEAD_DIM:(h + 1) * HEAD_DIM], axis=-1, keepdims=True) for h in range(HEADS)]
    return jnp.concatenate([jnp.broadcast_to(s, (x.shape[0], HEAD_DIM)) for s in sums], axis=1)


@jax.custom_vjp
def _head_sum(x):
    return _head_sum_impl(x)


_head_sum.defvjp(lambda x: (_head_sum_impl(x), None), lambda _, g: (_head_sum_impl(g),))


def _fn_norm(x, g):
    return (_rms(x, g),)


def _fn_gates(ba, al, dt):
    col = _iota((1, LANES), 1)
    g = jnp.where((col >= HEADS) & (col < 2 * HEADS), -jnp.exp(al) * _softplus(ba + dt), 0.0)
    rows = ba.shape[0]
    r, c = _iota((rows, rows), 0), _iota((rows, rows), 1)
    same = (r >> CHUNK_SHIFT) == (c >> CHUNK_SHIFT)
    gc = _dot(jnp.where(same & (r >= c), 1.0, 0.0), g, precision=HI)
    gtot = _dot(jnp.where(same, 1.0, 0.0), g, precision=HI)
    return _sigmoid(ba), gc, gtot


def _fn_post_q(c):
    s = c * _sigmoid(c)
    return (s * lax.rsqrt(_head_sum(s * s) + EPS) * (HEAD_DIM ** -0.5),)


def _fn_post_k(c):
    s = c * _sigmoid(c)
    return (s * lax.rsqrt(_head_sum(s * s) + EPS),)


def _fn_post_v(c):
    return (c * _sigmoid(c),)


def _fn_post(cq, ck, cv):
    return _fn_post_q(cq) + _fn_post_k(ck) + _fn_post_v(cv)


def _fn_outnorm(o, gate, og):
    y = o * lax.rsqrt(_head_sum(o * o) * (1.0 / HEAD_DIM) + EPS) * og
    return (y * (gate * _sigmoid(gate)),)


def _fn_res_norm(x, m, gp, gn):
    x1 = x + _rms(m, gp)
    return x1, _rms(x1, gn)


def _fn_res_norm2(x, m, gp, ga, gb):
    x1 = x + _rms(m, gp)
    return x1, _rms(x1, ga), _rms(x1, gb)


def _relu2_of(u):
    r = jnp.maximum(u, 0.0)
    return (r * r,)


def _relu2_cotangent(da, a):
    return (da * (2.0 * jnp.sqrt(a.astype(F32))),)


def _loss_call(x3, d1, tgt, g, tm=ROW_TILE):
    t, d = x3.shape
    tm = min(tm, t)

    def body(x_ref, d_ref, t_ref, g_ref, loss_ref, dx_ref, dd_ref, dg_ref):
        i = pl.program_id(0)
        y, vjp = jax.vjp(lambda x, dd, gg: x + _rms(dd, gg), x_ref[...], d_ref[...], g_ref[...])
        err = y - t_ref[...]
        lrow = 0.5 * jnp.mean(err * err, axis=-1, keepdims=True)
        dx, dd, dg = vjp(err * (1.0 / d))
        dx_ref[...] = dx
        dd_ref[...] = dd.astype(dd_ref.dtype)

        @pl.when(i == 0)
        def _():
            loss_ref[...] = jnp.zeros_like(loss_ref)
            dg_ref[...] = jnp.zeros_like(dg_ref)

        loss_ref[...] += jnp.broadcast_to(jnp.sum(lrow, axis=0, keepdims=True), loss_ref.shape)
        dg_ref[...] += dg

    row = pl.BlockSpec((tm, d), lambda i: (i, 0))
    return pl.pallas_call(
        body,
        name="loss_head",
        grid=(t // tm,),
        in_specs=[row, row, row, _full_spec(g)],
        out_specs=[pl.BlockSpec((8, LANES), lambda i: (0, 0)), row, row, _full_spec(g)],
        out_shape=[jax.ShapeDtypeStruct((8, LANES), F32), jax.ShapeDtypeStruct((t, d), F32), jax.ShapeDtypeStruct((t, d), BF16), jax.ShapeDtypeStruct(g.shape, F32)],
        compiler_params=_params("arbitrary"),
    )(x3, d1, tgt, g)


HALO = 8


def _conv_fwd(qkvg, conv_w, shards, tm=256):
    t = qkvg.shape[0]
    tm = min(tm, t)
    steps = t // tm
    wide = QKV // 3
    n = len(shards)

    def body(*refs):
        cur_ref, prev_ref, w_ref = refs[:3]
        shard_refs = refs[3:3 + n]
        o_ref, q_ref, k_ref, v_ref = refs[3 + n:7 + n]
        all_refs = refs[7 + n:7 + 2 * n]
        buf, sems = refs[7 + 2 * n], refs[8 + 2 * n:]
        i = pl.program_id(0)

        if n:
            @pl.when(i == 0)
            def _():
                for cp in _gather_sends(shard_refs, all_refs, *sems[:2]):
                    cp.start()

        buf[0:HALO, :] = jnp.where(i > 0, prev_ref[...], 0.0)
        buf[HALO:, :] = cur_ref[...]
        acc = buf[pl.ds(HALO - CONV_K + 1, tm), :] * w_ref[pl.ds(0, 1), :]
        for j in range(1, CONV_K):
            acc = acc + buf[pl.ds(HALO - CONV_K + 1 + j, tm), :] * w_ref[pl.ds(j, 1), :]
        o_ref[...] = acc
        (q_ref[...], k_ref[...], v_ref[...]) = _fn_post(acc[:, 0:wide], acc[:, wide:2 * wide], acc[:, 2 * wide:])

        if n:
            @pl.when(i == steps - 1)
            def _():
                _gather_finish(shard_refs, all_refs, *sems)

    part = pl.BlockSpec((tm, wide), lambda i: (i, 0))
    res = pl.pallas_call(
        body,
        name="conv_fwd",
        grid=(steps,),
        in_specs=[
            pl.BlockSpec((tm, QKV), lambda i: (i, 0)),
            pl.BlockSpec((HALO, QKV), lambda i: (jnp.maximum(i * (tm // HALO) - 1, 0), 0)),
            pl.BlockSpec((CONV_K, QKV), lambda i: (0, 0)),
        ] + [_HBM] * n,
        out_specs=[pl.BlockSpec((tm, QKV), lambda i: (i, 0)), part, part, part] + [_HBM] * n,
        out_shape=[jax.ShapeDtypeStruct((t, QKV), F32)] + [jax.ShapeDtypeStruct((t, wide), F32)] * 3
        + [jax.ShapeDtypeStruct((N_CHIPS,) + s.shape, s.dtype) for s in shards],
        scratch_shapes=[pltpu.VMEM((tm + HALO, QKV), F32)] + [pltpu.SemaphoreType.DMA((3 * n,))] * (4 if n else 0),
        compiler_params=_params("arbitrary"),
    )(qkvg, qkvg, conv_w, *shards)
    return res[:4], res[4:]


def _conv_bwd(conv, dqkv, dgate, qkvg, conv_w, tm=256):
    t = conv.shape[0]
    tm = min(tm, t)
    n = t // tm
    wg = dgate.shape[1]
    wide = QKV // 3

    def conv_cotangent(c_ref, g_refs):
        parts = [c_ref[:, j * wide:(j + 1) * wide] for j in range(3)]
        _, vjp = jax.vjp(_fn_post, *parts)
        return vjp(tuple(g[...] for g in g_refs))

    def body(c_ref, cn_ref, dq_ref, dk_ref, dv_ref, dqn_ref, dkn_ref, dvn_ref, dgate_ref, x_ref, xp_ref, w_ref, dx_ref, dw_ref, bufd, bufx):
        i = pl.program_id(0)
        for j, (cur, nxt) in enumerate(zip(conv_cotangent(c_ref, (dq_ref, dk_ref, dv_ref)), conv_cotangent(cn_ref, (dqn_ref, dkn_ref, dvn_ref)))):
            bufd[0:tm, j * wide:(j + 1) * wide] = cur
            bufd[tm:, j * wide:(j + 1) * wide] = jnp.where(i < n - 1, nxt, 0.0)
        bufx[0:HALO, :] = jnp.where(i > 0, xp_ref[...], 0.0)
        bufx[HALO:, :] = x_ref[...]

        @pl.when(i == 0)
        def _():
            dw_ref[...] = jnp.zeros_like(dw_ref)

        dcv = bufd[0:tm, :]
        acc = bufd[pl.ds(CONV_K - 1, tm), :] * w_ref[pl.ds(0, 1), :]
        for j in range(1, CONV_K):
            acc = acc + bufd[pl.ds(CONV_K - 1 - j, tm), :] * w_ref[pl.ds(j, 1), :]
        dx_ref[:, 0:QKV] = acc.astype(dx_ref.dtype)
        dx_ref[:, QKV:] = dgate_ref[...].astype(dx_ref.dtype)
        for j in range(CONV_K):
            dw_ref[pl.ds(j, 1), :] += jnp.sum(dcv * bufx[pl.ds(HALO - CONV_K + 1 + j, tm), :], axis=0, keepdims=True)

    def cur(width):
        return pl.BlockSpec((tm, width), lambda i: (i, 0))

    def nxt(width):
        return pl.BlockSpec((HALO, width), lambda i: (jnp.minimum((i + 1) * (tm // HALO), t // HALO - 1), 0))

    return pl.pallas_call(
        body,
        name="conv_bwd",
        grid=(n,),
        in_specs=[cur(QKV), nxt(QKV)] + [cur(wide)] * 3 + [nxt(wide)] * 3 + [
            cur(wg),
            cur(QKV),
            pl.BlockSpec((HALO, QKV), lambda i: (jnp.maximum(i * (tm // HALO) - 1, 0), 0)),
            pl.BlockSpec((CONV_K, QKV), lambda i: (0, 0)),
        ],
        out_specs=[pl.BlockSpec((tm, QKV + wg), lambda i: (i, 0)), pl.BlockSpec((HALO, QKV), lambda i: (0, 0))],
        out_shape=[jax.ShapeDtypeStruct((t, QKV + wg), BF16), jax.ShapeDtypeStruct((HALO, QKV), F32)],
        scratch_shapes=[pltpu.VMEM((tm + HALO, QKV), F32), pltpu.VMEM((tm + HALO, QKV), F32)],
        compiler_params=_params("arbitrary"),
    )(conv, conv, *dqkv, *dqkv, dgate, qkvg, qkvg, conv_w)


PREP_CHUNKS = 32
PREP_BWD_CHUNKS = 4
SCAN_CHUNKS = 4


def _hi_lo(x):
    hi = x.astype(BF16)
    return hi, (x - hi.astype(F32)).astype(BF16)


def _mm3(a, b, dims=NN):
    (ah, al), (bh, bl) = _hi_lo(a), _hi_lo(b)
    return _dot(ah, bh, dims) + (_dot(ah, bl, dims) + _dot(al, bh, dims))


def _neumann(lowers):
    c = lowers[0].shape[0]
    eye = jnp.where(_iota((c, c), 0) == _iota((c, c), 1), 1.0, 0.0)
    ps = [-low for low in lowers]
    tmats = [eye + p for p in ps]
    for _ in range(CHUNK_SHIFT - 1):
        ps = [_mm3(p, p) for p in ps]
        tmats = [t + _mm3(t, p) for t, p in zip(tmats, ps)]
    return tuple(tmats)


def _inv_cotangents(tmats, dts):
    half = [_mm3(t, dt, TN) for t, dt in zip(tmats, dts)]
    return tuple(-_mm3(hf, t, NT) for hf, t in zip(half, tmats))


@jax.custom_vjp
def _tri_inv(lowers):
    return _neumann(lowers)


def _tri_inv_fwd(lowers):
    tmats = _neumann(lowers)
    return tmats, tmats


_tri_inv.defvjp(_tri_inv_fwd, lambda tmats, dts: (_inv_cotangents(tmats, dts),))


@jax.custom_vjp
def _tri_inv_known(lowers, tmats):
    return tmats


_tri_inv_known.defvjp(lambda lowers, tmats: (tmats, tmats),
                      lambda tmats, dts: (_inv_cotangents(tmats, dts), tuple(jnp.zeros_like(t) for t in tmats)))


def _prep_chunks(qs, ks, vs, bs, gcs, gts, gcrs, tmats=None):
    c = CHUNK
    r, col = _iota((c, c), 0), _iota((c, c), 1)
    incl, strict = r >= col, r > col
    decays = [jnp.where(incl, jnp.exp(jnp.where(incl, gc - gcr, 0.0)), 0.0) for gc, gcr in zip(gcs, gcrs)]
    kbs = [k * b for k, b in zip(ks, bs)]
    kbfs = [k.astype(BF16) for k in ks]
    lowers = tuple(jnp.where(strict, _dot(kb.astype(BF16), kbf, NT) * decay, 0.0) for kb, kbf, decay in zip(kbs, kbfs, decays))
    tmats = _tri_inv(lowers) if tmats is None else _tri_inv_known(lowers, tuple(tmats))
    outs = []
    for q, k, v, b, gc, gt, kb, kbf, decay, tmat in zip(qs, ks, vs, bs, gcs, gts, kbs, kbfs, decays, tmats):
        tb = tmat.astype(BF16)
        egc = jnp.exp(gc)
        w = _dot(tb, (kb * egc).astype(BF16))
        u = _dot(tb, (v * b).astype(BF16))
        attn = _dot(q.astype(BF16), kbf, NT) * decay
        gl = jnp.broadcast_to(jnp.exp(jnp.mean(gt.reshape(c // 8, 8, 1), axis=0)), (8, HEAD_DIM))
        outs.append((w, u, q * egc, k * jnp.exp(gt - gc), attn, gl))
    return tuple(outs), tmats


def _prep_specs(rows, gch):
    head = pl.BlockSpec((rows, HEAD_DIM), lambda n, h: (n, h))
    gates = pl.BlockSpec((rows, LANES), lambda n, h: (n, 0))
    gcrow = pl.BlockSpec((1, gch, 1, CHUNK), lambda n, h: (h, n, 0, 0))
    square = pl.BlockSpec((1, rows, CHUNK), lambda n, h: (h, n, 0))
    gl = pl.BlockSpec((1, gch * 8, HEAD_DIM), lambda n, h: (h, n, 0))
    return head, gates, gcrow, square, gl


def _pick_lane(ref, sl, lane):
    return jnp.sum(jnp.where(_iota((1, LANES), 1) == lane, ref[sl, :], 0.0), axis=1, keepdims=True)


def _prep_inputs(q_ref, k_ref, v_ref, b_ref, gc_ref, gt_ref, gcr_ref, sls, h):
    return ([q_ref[sl, :] for sl in sls], [k_ref[sl, :] for sl in sls], [v_ref[sl, :] for sl in sls],
            [_pick_lane(b_ref, sl, h) for sl in sls], [_pick_lane(gc_ref, sl, h + HEADS) for sl in sls],
            [_pick_lane(gt_ref, sl, h + HEADS) for sl in sls], [gcr_ref[0, c] for c in range(len(sls))])


def _gdn_prep(q, k, v, beta, gc, gt, gcr, shards=()):
    t = q.shape[0]
    gch = min(PREP_CHUNKS, t // CHUNK)
    rows = gch * CHUNK
    steps = t // rows
    n = len(shards)

    def body(*refs):
        q_ref, k_ref, v_ref, b_ref, gc_ref, gt_ref, gcr_ref = refs[:7]
        shard_refs = refs[7:7 + n]
        w_ref, u_ref, qg_ref, kg_ref, at_ref, gl_ref, tm_ref = refs[7 + n:14 + n]
        all_refs, sems = refs[14 + n:14 + 2 * n], refs[14 + 2 * n:]
        h = pl.program_id(1)

        if n:
            @pl.when(jnp.logical_and(pl.program_id(0) == 0, h == 0))
            def _():
                for cp in _gather_sends(shard_refs, all_refs, *sems[:2]):
                    cp.start()

        sls = [pl.ds(c * CHUNK, CHUNK) for c in range(gch)]
        outs, tmats = _prep_chunks(*_prep_inputs(q_ref, k_ref, v_ref, b_ref, gc_ref, gt_ref, gcr_ref, sls, h))
        for c, (sl, (w, u, qg, kg, attn, gl), tmat) in enumerate(zip(sls, outs, tmats)):
            w_ref[sl, :] = w.astype(BF16)
            u_ref[sl, :] = u
            qg_ref[sl, :] = qg.astype(BF16)
            kg_ref[sl, :] = kg.astype(BF16)
            at_ref[0, sl, :] = attn.astype(BF16)
            gl_ref[0, pl.ds(c * 8, 8), :] = gl
            tm_ref[0, sl, :] = tmat

        if n:
            @pl.when(jnp.logical_and(pl.program_id(0) == steps - 1, h == HEADS - 1))
            def _():
                _gather_finish(shard_refs, all_refs, *sems)

    hb, col, gcrow, square, glb = _prep_specs(rows, gch)
    wide = HEADS * HEAD_DIM
    res = pl.pallas_call(
        body,
        name="gdn_prep",
        grid=(steps, HEADS),
        in_specs=[hb, hb, hb, col, col, col, gcrow] + [_HBM] * n,
        out_specs=[hb, hb, hb, hb, square, glb, square] + [_HBM] * n,
        out_shape=[
            jax.ShapeDtypeStruct((t, wide), BF16),
            jax.ShapeDtypeStruct((t, wide), F32),
            jax.ShapeDtypeStruct((t, wide), BF16),
            jax.ShapeDtypeStruct((t, wide), BF16),
            jax.ShapeDtypeStruct((HEADS, t, CHUNK), BF16),
            jax.ShapeDtypeStruct((HEADS, t // CHUNK * 8, HEAD_DIM), F32),
            jax.ShapeDtypeStruct((HEADS, t, CHUNK), F32),
        ] + [jax.ShapeDtypeStruct((N_CHIPS,) + s.shape, s.dtype) for s in shards],
        scratch_shapes=[pltpu.SemaphoreType.DMA((3 * n,))] * (4 if n else 0),
        compiler_params=_params("arbitrary", "arbitrary") if n else _params("parallel", "parallel"),
    )(q, k, v, beta, gc, gt, gcr, *shards)
    return res[:7], res[7:]


def _gdn_prep_bwd(q, k, v, beta, gc, gt, gcr, tmat, dw, du, dqg, dkg, dattn, dgl, partials=()):
    t = q.shape[0]
    gch = min(PREP_BWD_CHUNKS, t // CHUNK)
    rows = gch * CHUNK
    steps = t // rows
    n_sc = len(partials)

    def body(*refs):
        (q_ref, k_ref, v_ref, b_ref, gc_ref, gt_ref, gcr_ref, tm_ref, dw_ref, du_ref, dqg_ref, dkg_ref, dat_ref, dgl_ref) = refs[:14]
        p_refs = refs[14:14 + n_sc]
        dq_ref, dk_ref, dv_ref, db_ref, dgc_ref, dgt_ref, dgcr_ref = refs[14 + n_sc:21 + n_sc]
        from_refs, sems = refs[21 + n_sc:21 + 2 * n_sc], refs[21 + 2 * n_sc:]
        h = pl.program_id(1)
        lane = _iota((1, LANES), 1)

        if n_sc:
            @pl.when(jnp.logical_and(pl.program_id(0) == 0, h == 0))
            def _():
                for cp in _scatter_copies(p_refs, from_refs, *sems):
                    cp.start()

        @pl.when(h == 0)
        def _():
            db_ref[...] = jnp.zeros_like(db_ref)
            dgc_ref[...] = jnp.zeros_like(dgc_ref)
            dgt_ref[...] = jnp.zeros_like(dgt_ref)

        sls = [pl.ds(c * CHUNK, CHUNK) for c in range(gch)]
        known = [tm_ref[0, sl, :] for sl in sls]
        _, vjp = jax.vjp(lambda *a: _prep_chunks(*a, tmats=known)[0], *_prep_inputs(q_ref, k_ref, v_ref, b_ref, gc_ref, gt_ref, gcr_ref, sls, h))
        cots = tuple((dw_ref[sl, :], du_ref[sl, :], dqg_ref[sl, :], dkg_ref[sl, :], dat_ref[0, sl, :], dgl_ref[0, pl.ds(c * 8, 8), :]) for c, sl in enumerate(sls))
        dqs, dks, dvs, dbs, dgcs, dgts, dgcrs = vjp(cots)
        for c, sl in enumerate(sls):
            dq_ref[sl, :] = dqs[c]
            dk_ref[sl, :] = dks[c]
            dv_ref[sl, :] = dvs[c]
            db_ref[sl, :] += jnp.where(lane == h, dbs[c], 0.0)
            dgc_ref[sl, :] += jnp.where(lane == h + HEADS, dgcs[c], 0.0)
            dgt_ref[sl, :] += jnp.where(lane == h + HEADS, dgts[c], 0.0)
            dgcr_ref[0, c] = dgcrs[c]

        if n_sc:
            @pl.when(jnp.logical_and(pl.program_id(0) == steps - 1, h == HEADS - 1))
            def _():
                for cp in _scatter_copies(p_refs, from_refs, *sems):
                    cp.wait()

    hb, col, gcrow, square, glb = _prep_specs(rows, gch)
    wide = HEADS * HEAD_DIM
    res = pl.pallas_call(
        body,
        name="gdn_prep_bwd",
        grid=(steps, HEADS),
        in_specs=[hb, hb, hb, col, col, col, gcrow, square, hb, hb, hb, hb, square, glb] + [_HBM] * n_sc,
        out_specs=[hb, hb, hb, col, col, col, gcrow] + [_HBM] * n_sc,
        out_shape=[jax.ShapeDtypeStruct((t, wide), F32)] * 3 + [jax.ShapeDtypeStruct((t, LANES), F32)] * 3 + [jax.ShapeDtypeStruct((HEADS, t // CHUNK, 1, CHUNK), F32)]
        + [jax.ShapeDtypeStruct((3,) + p.shape[1:], p.dtype) for p in partials],
        scratch_shapes=[pltpu.SemaphoreType.DMA((3 * n_sc,))] * (2 if n_sc else 0),
        compiler_params=_params("arbitrary", "arbitrary"),
    )(q, k, v, beta, gc, gt, gcr, tmat, dw, du, dqg, dkg, dattn, dgl, *partials)
    return res[:7], res[7:]


def _gdn_scan(w, u, qg, kg, attn, gl):
    t = w.shape[0]
    n = t // CHUNK
    nch = min(SCAN_CHUNKS, n)
    wide = HEADS * HEAD_DIM

    def body(w_ref, u_ref, qg_ref, kg_ref, at_ref, gl_ref, o_ref, st_ref, s_ref):
        @pl.when(pl.program_id(0) == 0)
        def _():
            s_ref[...] = jnp.zeros_like(s_ref)

        heads = range(HEADS)
        cols = [pl.ds(h * HEAD_DIM, HEAD_DIM) for h in heads]
        for c in range(nch):
            rows, gl_rows = pl.ds(c * CHUNK, CHUNK), pl.ds(c * 8, 8)
            ss = [s_ref[h] for h in heads]
            sbs = [s.astype(BF16) for s in ss]
            vbs = [(u_ref[rows, hs] - _dot(w_ref[rows, hs], sb)).astype(BF16) for hs, sb in zip(cols, sbs)]
            outs = [_dot(qg_ref[rows, hs], sb) + _dot(at_ref[h, rows, :], vb) for h, hs, sb, vb in zip(heads, cols, sbs, vbs)]
            new = [s * jnp.tile(gl_ref[h, gl_rows, :], (HEAD_DIM // 8, 1)) + _dot(kg_ref[rows, hs], vb, TN) for h, hs, s, vb in zip(heads, cols, ss, vbs)]
            for h, hs in zip(heads, cols):
                st_ref[c, h] = ss[h]
                o_ref[rows, hs] = outs[h]
                s_ref[h] = new[h]

    row = pl.BlockSpec((nch * CHUNK, wide), lambda i: (i, 0))
    return pl.pallas_call(
        body,
        name="gdn_scan",
        grid=(n // nch,),
        in_specs=[row, row, row, row, pl.BlockSpec((HEADS, nch * CHUNK, CHUNK), lambda i: (0, i, 0)), pl.BlockSpec((HEADS, nch * 8, HEAD_DIM), lambda i: (0, i, 0))],
        out_specs=[row, pl.BlockSpec((nch, HEADS, HEAD_DIM, HEAD_DIM), lambda i: (i, 0, 0, 0))],
        out_shape=[jax.ShapeDtypeStruct((t, wide), F32), jax.ShapeDtypeStruct((n, HEADS, HEAD_DIM, HEAD_DIM), F32)],
        scratch_shapes=[pltpu.VMEM((HEADS, HEAD_DIM, HEAD_DIM), F32)],
        compiler_params=_params("arbitrary"),
    )(w, u, qg, kg, attn, gl)


def _gdn_scan_bwd(w, u, qg, kg, attn, gl, states, do):
    t = w.shape[0]
    n = t // CHUNK
    nch = min(SCAN_CHUNKS, n)
    steps = n // nch
    wide = HEADS * HEAD_DIM

    def body(w_ref, u_ref, qg_ref, kg_ref, at_ref, gl_ref, st_ref, do_ref, dw_ref, du_ref, dqg_ref, dkg_ref, dat_ref, dgl_ref, ds_ref):
        @pl.when(pl.program_id(0) == 0)
        def _():
            ds_ref[...] = jnp.zeros_like(ds_ref)

        heads = range(HEADS)
        cols = [pl.ds(h * HEAD_DIM, HEAD_DIM) for h in heads]
        for c in reversed(range(nch)):
            rows, gl_rows = pl.ds(c * CHUNK, CHUNK), pl.ds(c * 8, 8)
            ss = [st_ref[c, h] for h in heads]
            sbs = [s.astype(BF16) for s in ss]
            dsns = [ds_ref[h] for h in heads]
            dsbs = [d.astype(BF16) for d in dsns]
            dobs = [do_ref[rows, hs].astype(BF16) for hs in cols]
            vbs = [(u_ref[rows, hs] - _dot(w_ref[rows, hs], sb)).astype(BF16) for hs, sb in zip(cols, sbs)]
            dvns = [_dot(at_ref[h, rows, :], dob, TN) + _dot(kg_ref[rows, hs], dsb) for h, hs, dob, dsb in zip(heads, cols, dobs, dsbs)]
            dvbs = [d.astype(BF16) for d in dvns]
            for h, hs in zip(heads, cols):
                dat_ref[h, rows, :] = _dot(dobs[h], vbs[h], NT)
                dqg_ref[rows, hs] = _dot(dobs[h], sbs[h], NT)
                dkg_ref[rows, hs] = _dot(vbs[h], dsbs[h], NT)
                du_ref[rows, hs] = dvns[h]
                dw_ref[rows, hs] = -_dot(dvbs[h], sbs[h], NT)
                dgl_ref[h, gl_rows, :] = jnp.sum((dsns[h] * ss[h]).reshape(HEAD_DIM // 8, 8, HEAD_DIM), axis=0)
            new = [dsn * jnp.tile(gl_ref[h, gl_rows, :], (HEAD_DIM // 8, 1)) + _dot(qg_ref[rows, hs], dob, TN) - _dot(w_ref[rows, hs], dvb, TN)
                   for h, hs, dsn, dob, dvb in zip(heads, cols, dsns, dobs, dvbs)]
            for h in heads:
                ds_ref[h] = new[h]

    row = pl.BlockSpec((nch * CHUNK, wide), lambda i: (steps - 1 - i, 0))
    at = pl.BlockSpec((HEADS, nch * CHUNK, CHUNK), lambda i: (0, steps - 1 - i, 0))
    glb = pl.BlockSpec((HEADS, nch * 8, HEAD_DIM), lambda i: (0, steps - 1 - i, 0))
    return pl.pallas_call(
        body,
        name="gdn_scan_bwd",
        grid=(steps,),
        in_specs=[row, row, row, row, at, glb, pl.BlockSpec((nch, HEADS, HEAD_DIM, HEAD_DIM), lambda i: (steps - 1 - i, 0, 0, 0)), row],
        out_specs=[row, row, row, row, at, glb],
        out_shape=[jax.ShapeDtypeStruct((t, wide), F32)] * 4 + [jax.ShapeDtypeStruct((HEADS, t, CHUNK), F32), jax.ShapeDtypeStruct((HEADS, n * 8, HEAD_DIM), F32)],
        scratch_shapes=[pltpu.VMEM((HEADS, HEAD_DIM, HEAD_DIM), F32)],
        compiler_params=_params("arbitrary"),
    )(w, u, qg, kg, attn, gl, states, do)


SB_Q = 512
SB_K = 256
SB_STEP = 1
SB_DEAD = -105.0


def _sb_scores(q, k):
    z = _dot(q, k, NT) * (HEAD_DIM ** -0.5)
    lb = jnp.minimum(z, 0.0) - jnp.log(1.0 + jnp.exp(-jnp.abs(z)))
    return lb, lb - z


def _tri(n, rel):
    return jnp.where(rel(_iota((n, n), 0), _iota((n, n), 1)), 1.0, 0.0).astype(BF16)


def _lanes(col):
    return jnp.broadcast_to(col, (col.shape[0], LANES))


def _sb_fwd(q, k, v):
    t = q.shape[0]
    bq, bk = min(SB_Q, t), min(SB_K, t)
    nsub, rep = bq // bk, bk // LANES
    nstep = min(SB_STEP, nsub)
    steps_per_tile = nsub // nstep

    def body(q_ref, k_ref, v_ref, o_ref, rt_ref, first_ref):
        h = pl.program_id(0)
        i = pl.program_id(1)
        o_ref[...] = jnp.zeros_like(o_ref)
        rt_ref[...] = jnp.zeros_like(rt_ref)
        after = _tri(bk, lambda r, c: r > c)

        def block(j, r0, diag):
            st = pl.multiple_of(j * bk, bk)
            kv, vv = k_ref[pl.ds(st, bk), :], v_ref[pl.ds(st, bk), :]
            lb, l1m = _sb_scores(q_ref[r0:, :], kv)
            if diag:
                mask = _iota((bq - r0, bk), 1) + j * bk < _iota((bq - r0, bk), 0) + (r0 + i * bq)
                l1m = jnp.where(mask, l1m, 0.0)
            sums = _two_pass(l1m, after)
            run = rt_ref[r0:, :]
            a = jnp.exp(lb + jnp.tile(run, (1, rep)) + sums)
            if diag:
                a = jnp.where(mask, a, 0.0)
            o_ref[r0:, :] += _dot(a.astype(BF16), vv)
            rt_ref[r0:, :] = run + _lanes(sums[:, 0:1] + l1m[:, 0:1])

        for s in reversed(range(nsub)):
            block(i * nsub + s, s * bk, True)

        def alive(carry):
            u, highest = carry
            return jnp.logical_and(u >= 0, highest > SB_DEAD)

        def step(carry):
            u, _ = carry
            for s in reversed(range(nstep)):
                block(u * nstep + s, 0, False)
            return u - 1, jnp.max(rt_ref[...])

        u_end, _ = lax.while_loop(alive, step, (i * steps_per_tile - 1, jnp.max(rt_ref[...])))
        first_ref[h, i] = u_end + 1

    qb = pl.BlockSpec((bq, HEAD_DIM), lambda h, i: (i, h))
    full = pl.BlockSpec((t, HEAD_DIM), lambda h, i: (0, h))
    return pl.pallas_call(
        body,
        name="sb_fwd",
        grid=(HEADS, t // bq),
        in_specs=[qb, full, full],
        out_specs=[qb, qb, pl.BlockSpec(memory_space=pltpu.SMEM)],
        out_shape=[jax.ShapeDtypeStruct(q.shape, F32), jax.ShapeDtypeStruct(q.shape, F32), jax.ShapeDtypeStruct((HEADS, t // bq), jnp.int32)],
        compiler_params=_params("arbitrary", "arbitrary"),
    )(q, k, v)


def _sb_bwd(q, k, v, rt, first, do):
    t = q.shape[0]
    bq, bk = min(SB_Q, t), min(SB_K, t)
    nsub, rep = bq // bk, bk // LANES
    nstep = min(SB_STEP, nsub)
    steps_per_tile = nsub // nstep
    scale = HEAD_DIM ** -0.5

    def body(first_ref, q_ref, k_ref, v_ref, rt_ref, do_ref, dq_ref, dk_ref, dv_ref, left_ref, pg_ref):
        h = pl.program_id(0)
        i = pl.program_id(1)

        @pl.when(i == 0)
        def _():
            dk_ref[...] = jnp.zeros_like(dk_ref)
            dv_ref[...] = jnp.zeros_like(dv_ref)

        dq_ref[...] = jnp.zeros_like(dq_ref)
        left_ref[...] = jnp.zeros_like(left_ref)
        pg_ref[...] = jnp.zeros_like(pg_ref)
        upto = _tri(bk, lambda r, c: r <= c)

        def block(j, r0, diag):
            st = pl.multiple_of(j * bk, bk)
            kv, vv = k_ref[pl.ds(st, bk), :], v_ref[pl.ds(st, bk), :]
            qv = q_ref[r0:, :]
            dob = do_ref[r0:, :].astype(BF16)
            lb, l1m = _sb_scores(qv, kv)
            if diag:
                mask = _iota((bq - r0, bk), 1) + j * bk < _iota((bq - r0, bk), 0) + (r0 + i * bq)
                l1m = jnp.where(mask, l1m, 0.0)
            sums = _two_pass(l1m, upto)
            left = left_ref[r0:, :]
            a = jnp.exp(lb + jnp.tile(rt_ref[r0:, :] - left, (1, rep)) - sums)
            if diag:
                a = jnp.where(mask, a, 0.0)
            g = _dot(dob, vv, NT) * a
            dv_ref[pl.ds(st, bk), :] += _dot(a.astype(BF16), dob, TN)
            gsum = _two_pass(g, upto)
            pg = pg_ref[r0:, :]
            dz = g - jnp.exp(lb) * (jnp.tile(pg, (1, rep)) + gsum)
            if diag:
                dz = jnp.where(mask, dz, 0.0)
            dzb = (dz * scale).astype(BF16)
            dk_ref[pl.ds(st, bk), :] += _dot(dzb, qv, TN)
            dq_ref[r0:, :] += _dot(dzb, kv)
            left_ref[r0:, :] = left + _lanes(sums[:, bk - 1:bk])
            pg_ref[r0:, :] = pg + _lanes(gsum[:, bk - 1:bk])

        def step(u, carry):
            for s in range(nstep):
                block(u * nstep + s, 0, False)
            return carry

        lax.fori_loop(first_ref[h, i], i * steps_per_tile, step, 0)
        for s in range(nsub):
            block(i * nsub + s, s * bk, True)

    qb = pl.BlockSpec((bq, HEAD_DIM), lambda h, i: (i, h))
    full = pl.BlockSpec((t, HEAD_DIM), lambda h, i: (0, h))
    return pl.pallas_call(
        body,
        name="sb_bwd",
        grid=(HEADS, t // bq),
        in_specs=[pl.BlockSpec(memory_space=pltpu.SMEM), qb, full, full, qb, qb],
        out_specs=[qb, full, full],
        out_shape=[jax.ShapeDtypeStruct(q.shape, F32)] * 3,
        scratch_shapes=[pltpu.VMEM((bq, LANES), F32), pltpu.VMEM((bq, LANES), F32)],
        compiler_params=_params("arbitrary", "arbitrary"),
    )(first, q, k, v, rt, do)


def _adamw(w, g, m, v, name, tm=256):
    r, c = w.shape
    tm = tm if r % tm == 0 else r

    def body(w_ref, g_ref, m_ref, v_ref, d_ref, nm_ref, nv_ref):
        gv = g_ref[...]
        nm = ADAM_B1 * m_ref[...] + (1.0 - ADAM_B1) * gv
        nv = ADAM_B2 * v_ref[...] + (1.0 - ADAM_B2) * (gv * gv)
        m_hat = nm / (1.0 - ADAM_B1 ** ADAM_STEP)
        v_hat = nv / (1.0 - ADAM_B2 ** ADAM_STEP)
        d_ref[...] = -ADAM_LR * (m_hat / (jnp.sqrt(v_hat) + ADAM_EPS) + ADAM_WD * w_ref[...])
        nm_ref[...] = nm
        nv_ref[...] = nv

    blk = pl.BlockSpec((tm, c), lambda i: (i, 0))
    return pl.pallas_call(
        body,
        name=name,
        grid=(r // tm,),
        in_specs=[blk] * 4,
        out_specs=[blk] * 3,
        out_shape=[jax.ShapeDtypeStruct((r, c), F32)] * 3,
        compiler_params=_params("parallel"),
    )(w, g, m, v)


def _local_step(x, tgt, gains, small, shards, assemble_first, assemble, early_reduce=None, late_reduce=None):
    mix_pre, mix_post, mlp_pre, mlp_post, kv_gain = gains
    a_log, dt_bias, out_gain = small
    t, d = x.shape
    row = lambda a, i=None: a[i:i + 1] if i is not None else a
    al = jnp.zeros((1, LANES), F32).at[:, HEADS:2 * HEADS].set(a_log)
    dtb = jnp.zeros((1, LANES), F32).at[:, HEADS:2 * HEADS].set(dt_bias)
    og = jnp.tile(out_gain, (1, HEADS))
    full = lambda a: (a, a.shape[1], 0)

    h0, *gathered_first = _rowwise("norm_in", _fn_norm, [full(x)], [row(mix_pre, 0)], [(d, BF16)], gather=shards[0])
    w_qkvg, w_ba, conv_w = assemble_first(gathered_first)
    qkvg = _matmul(h0, w_qkvg, "nn", F32, "mm_gdn_in", tk=1024)
    ba = _matmul(h0, w_ba, "nn", F32, "mm_gdn_ba", tk=1024)
    (conv, gq, gk, gv), gathered_conv = _conv_fwd(qkvg, conv_w, shards[1])
    beta, gc, gt = _rowwise("gates", _fn_gates, [full(ba)], [al, dtb], [(LANES, F32)] * 3)
    gcr = jnp.swapaxes(gc[:, HEADS:2 * HEADS], 0, 1).reshape(HEADS, t // CHUNK, 1, CHUNK)
    (pw, pu, pqg, pkg, pattn, pgl, ptm), gathered_prep = _gdn_prep(gq, gk, gv, beta, gc, gt, gcr, shards[2])
    w_out, w_kv, w_q, w_o, w_up, w_down = assemble(gathered_conv, gathered_prep)
    w_qkvg_t, w_up_t, w_down_t = (jnp.swapaxes(a, -1, -2) for a in (w_qkvg, w_up, w_down))
    o_gdn, states = _gdn_scan(pw, pu, pqg, pkg, pattn, pgl)
    (on,) = _rowwise("out_norm", _fn_outnorm, [full(o_gdn), (qkvg, d, 3)], [og], [(d, BF16)])
    mix0 = _matmul(on, w_out, "nn", F32, "mm_gdn_out", tk=1024)
    x1, h1 = _rowwise("res_a0", _fn_res_norm, [full(x), full(mix0)], [row(mix_post, 0), row(mlp_pre, 0)], [(d, F32), (d, BF16)])
    (a0,) = _matmul(h1, w_up[0], "nn", (BF16,), "mm_up0", tk=1024, epilogue=_relu2_of)
    d0 = _matmul(a0, w_down[0], "nn", F32, "mm_down0")
    x2, hkv, hq = _rowwise("res_b0", _fn_res_norm2, [full(x1), full(d0)], [row(mlp_post, 0), kv_gain, row(mix_pre, 1)], [(d, F32), (d, BF16), (d, BF16)])
    w_k, w_v = w_kv[:, :d], w_kv[:, d:]
    kp = _matmul(hkv, w_k, "nn", BF16, "mm_k", tk=1024)
    vp = _matmul(hkv, w_v, "nn", BF16, "mm_v", tk=1024)
    qp = _matmul(hq, w_q, "nn", BF16, "mm_q", tk=1024)
    o_sb, rt, sb_first = _sb_fwd(qp, kp, vp)
    mix1 = _matmul(o_sb, w_o, "nn", F32, "mm_sb_out", tk=1024)
    x3, h3 = _rowwise("res_a1", _fn_res_norm, [full(x2), full(mix1)], [row(mix_post, 1), row(mlp_pre, 1)], [(d, F32), (d, BF16)])
    (a1,) = _matmul(h3, w_up[1], "nn", (BF16,), "mm_up1", tk=1024, epilogue=_relu2_of)
    d1 = _matmul(a1, w_down[1], "nn", F32, "mm_down1")

    loss, dx3, dd1, g_mlp_post1 = _loss_call(x3, d1, tgt, row(mlp_post, 1))
    (du1,) = _matmul(dd1, w_down_t[1], "nn", (BF16,), "mm_down1_dx", epilogue=_relu2_cotangent, extras=[a1])
    g_down1 = _matmul(a1, dd1, "tn", F32, "mm_down1_dw")
    dh3 = _matmul(du1, w_up_t[1], "nn", F32, "mm_up1_dx")
    g_up1 = _matmul(h3, du1, "tn", F32, "mm_up1_dw")
    (dx2, dmix1), (g_mix_post1, g_mlp_pre1), _ = _rowwise_bwd(
        "res_a1_bwd", _fn_res_norm, [full(x2), full(mix1)], [row(mix_post, 1), row(mlp_pre, 1)], [dx3, dh3], [F32, BF16])
    do_sb = _matmul(dmix1, w_o, "nt", BF16, "mm_sb_out_dx")
    g_o = _matmul(o_sb, dmix1, "tn", F32, "mm_sb_out_dw")
    dqp, dkp, dvp = _sb_bwd(qp, kp, vp, rt, sb_first, do_sb)
    dhq = _matmul(dqp, w_q, "nt", F32, "mm_q_dx")
    g_q = _matmul(hq, dqp, "tn", F32, "mm_q_dw")
    dhkv = _matmul(dvp, w_v, "nt", F32, "mm_v_dx", add=_matmul(dkp, w_k, "nt", F32, "mm_k_dx"))
    g_kv = jnp.concatenate([_matmul(hkv, dkp, "tn", F32, "mm_k_dw"), _matmul(hkv, dvp, "tn", F32, "mm_v_dw")], axis=1)
    (dx1, dd0), (g_mlp_post0, g_kv_gain, g_mix_pre1), _ = _rowwise_bwd(
        "res_b0_bwd", _fn_res_norm2, [full(x1), full(d0)], [row(mlp_post, 0), kv_gain, row(mix_pre, 1)], [dx2, dhkv, dhq], [F32, BF16])
    (du0,) = _matmul(dd0, w_down_t[0], "nn", (BF16,), "mm_down0_dx", epilogue=_relu2_cotangent, extras=[a0])
    g_down0 = _matmul(a0, dd0, "tn", F32, "mm_down0_dw")
    dh1 = _matmul(du0, w_up_t[0], "nn", F32, "mm_up0_dx")
    g_up0 = _matmul(h1, du0, "tn", F32, "mm_up0_dw")
    (dx0, dmix0), (g_mix_post0, g_mlp_pre0), _ = _rowwise_bwd(
        "res_a0_bwd", _fn_res_norm, [full(x), full(mix0)], [row(mix_post, 0), row(mlp_pre, 0)], [dx1, dh1], [F32, BF16])
    don = _matmul(dmix0, w_out, "nt", F32, "mm_gdn_out_dx")
    g_out = _matmul(on, dmix0, "tn", F32, "mm_gdn_out_dw")
    (do_gdn, dgate), (g_og,), _ = _rowwise_bwd("out_norm_bwd", _fn_outnorm, [full(o_gdn), (qkvg, d, 3)], [og], [don], [F32, F32])
    dpw, dpu, dpqg, dpkg, dpattn, dpgl = _gdn_scan_bwd(pw, pu, pqg, pkg, pattn, pgl, states, do_gdn)
    partial, partial_bf16 = [], ()
    if early_reduce is not None:
        partial, partial_bf16 = early_reduce(dict(mlp_w_up=(g_up0, g_up1), mlp_w_down=(g_down0, g_down1), gdn_w_out=g_out[None], w_kv=g_kv, sb_w_q=g_q[None], sb_w_o=g_o[None]))
    (dgq, dgk, dgv, dbeta, dgc, dgt, dgcr), from_chips = _gdn_prep_bwd(gq, gk, gv, beta, gc, gt, gcr, ptm, dpw, dpu, dpqg, dpkg, dpattn, dpgl, partial_bf16)
    dgcr_lanes = jnp.pad(jnp.swapaxes(dgcr.reshape(HEADS, t), 0, 1), ((0, 0), (HEADS, LANES - 2 * HEADS)))
    gate_cots = [dbeta, dgc + dgcr_lanes, dgt]
    (dba,), (g_al, g_dtb), _ = _rowwise_bwd("gates_bwd", _fn_gates, [full(ba)], [al, dtb], gate_cots, [BF16])
    dqkvg, g_conv = _conv_bwd(conv, (dgq, dgk, dgv), dgate, qkvg, conv_w)
    dh0b = _matmul(dba, w_ba, "nt", F32, "mm_gdn_ba_dx", tk=LANES)
    dh0 = _matmul(dqkvg, w_qkvg_t, "nn", F32, "mm_gdn_in_dx", add=dh0b)
    g_qkvg = _matmul(h0, dqkvg, "tn", F32, "mm_gdn_in_dw")
    g_ba = _matmul(h0, dba, "tn", F32, "mm_gdn_ba_dw")
    g_w_in = jnp.concatenate([g_qkvg, g_ba[:, :2 * HEADS]], axis=1)[None]
    partial_late, partial_late_bf16 = late_reduce(dict(gdn_w_in=g_w_in)) if late_reduce is not None else ([], ())
    (grad_x,), (g_mix_pre0,), from_chips_late = _rowwise_bwd(
        "norm_in_bwd", lambda xx, gg: (_rms(xx, gg), xx), [full(x)], [row(mix_pre, 0)], [dh0, dx0], [F32], partials=partial_late_bf16)

    grads = dict(
        mix_pre_gain=jnp.concatenate([g_mix_pre0, g_mix_pre1], axis=0),
        mix_post_gain=jnp.concatenate([g_mix_post0, g_mix_post1], axis=0),
        mlp_pre_gain=jnp.concatenate([g_mlp_pre0, g_mlp_pre1], axis=0),
        mlp_post_gain=jnp.concatenate([g_mlp_post0, g_mlp_post1], axis=0),
        mlp_w_up=(g_up0, g_up1),
        mlp_w_down=(g_down0, g_down1),
        gdn_w_in=g_w_in,
        gdn_conv_w=g_conv[None, :CONV_K],
        gdn_a_log=g_al[:, HEADS:2 * HEADS],
        gdn_dt_bias=g_dtb[:, HEADS:2 * HEADS],
        gdn_out_gain=jnp.sum(g_og.reshape(HEADS, HEAD_DIM), axis=0, keepdims=True),
        gdn_w_out=g_out[None],
        kv_gain=g_kv_gain[0],
        w_kv=g_kv,
        sb_w_q=g_q[None],
        sb_w_o=g_o[None],
    )
    return loss, grad_x, grads, (list(partial) + list(partial_late), list(from_chips) + list(from_chips_late))


N_DEV = 8
N_CHIPS = 4
PACK_ROW_TILE = 128

_HBM = pl.BlockSpec(memory_space=pltpu.HBM)


def _place():
    return lax.axis_index("x"), lax.axis_index("y"), lax.axis_index("c")


def _other_chips(x, y):
    return [(1 - x, y), (x, 1 - y), (1 - x, 1 - y)]


def _remote(src, dst, send_sem, recv_sem, to):
    return pltpu.make_async_remote_copy(src_ref=src, dst_ref=dst, send_sem=send_sem, recv_sem=recv_sem, device_id=to, device_id_type=MESH)


def _gather8(v, name):
    rows, cols = v.shape

    def body(v_ref, out_ref, sum_ref, send_sems, recv_sems, local_sem):
        x, y, c = _place()
        me, sibling = (x, y, c), (x, y, 1 - c)
        chips = _other_chips(x, y)

        def blk(px, py, pc):
            return out_ref.at[pl.ds((4 * px + 2 * py + pc) * rows, rows), :]

        def copy(k, block, to, src=None):
            return _remote(blk(*block) if src is None else src, blk(*block), send_sems.at[k], recv_sems.at[k], to)

        mine = pltpu.make_async_copy(v_ref, blk(*me), local_sem)
        mine.start()
        first = [copy(0, me, sibling, src=v_ref)] + [copy(1 + j, me, (*chip, c), src=v_ref) for j, chip in enumerate(chips)]
        for cp in first:
            cp.start()
        passed = [copy(4 + j, (*chip, c), sibling) for j, chip in enumerate(chips)]
        for j, chip in enumerate(chips):
            copy(1 + j, (*chip, c), me).wait_recv()
            passed[j].start()
        copy(0, sibling, me).wait_recv()
        for j, chip in enumerate(chips):
            copy(4 + j, (*chip, 1 - c), me).wait_recv()
        for cp in first + passed:
            cp.wait_send()
        mine.wait()
        acc = out_ref[pl.ds(0, rows), :]
        for dev in range(1, N_DEV):
            acc = acc + out_ref[pl.ds(dev * rows, rows), :]
        sum_ref[...] = acc

    vm = pl.BlockSpec(memory_space=pltpu.VMEM)
    return pl.pallas_call(
        body,
        name=name,
        out_shape=[jax.ShapeDtypeStruct((N_DEV * rows, cols), v.dtype), jax.ShapeDtypeStruct((rows, cols), v.dtype)],
        in_specs=[vm],
        out_specs=[vm, vm],
        scratch_shapes=[pltpu.SemaphoreType.DMA((7,)), pltpu.SemaphoreType.DMA((7,)), pltpu.SemaphoreType.DMA],
    )(v)


def _hbm_call(body, name, arrs, out_shapes, sem_counts):
    n = len(arrs)

    def wrapped(*refs):
        body(refs[:n], refs[n:2 * n], *refs[2 * n:])

    return pl.pallas_call(
        wrapped,
        name=name,
        out_shape=[jax.ShapeDtypeStruct(s, a.dtype) for s, a in zip(out_shapes, arrs)],
        in_specs=[_HBM] * n,
        out_specs=[_HBM] * n,
        scratch_shapes=[pltpu.SemaphoreType.DMA((k,)) for k in sem_counts],
    )(*arrs)


def _gather_sends(w_refs, out_refs, send_sems, recv_sems):
    x, y, c = _place()
    s_me = 2 * x + y
    return [_remote(w.at[c], o.at[s_me, c], send_sems.at[3 * a + j], recv_sems.at[3 * a + j], (px, py, c))
            for a, (w, o) in enumerate(zip(w_refs, out_refs)) for j, (px, py) in enumerate(_other_chips(x, y))]


def _gather_finish(w_refs, out_refs, send_sems, recv_sems, fsend_sems, frecv_sems):
    x, y, c = _place()
    chips = _other_chips(x, y)
    passed = []
    for a, o in enumerate(out_refs):
        for j, (px, py) in enumerate(chips):
            half = o.at[2 * px + py, c]
            _remote(half, half, send_sems.at[3 * a + j], recv_sems.at[3 * a + j], (px, py, c)).wait_recv()
            fwd = _remote(half, half, fsend_sems.at[3 * a + j], frecv_sems.at[3 * a + j], (x, y, 1 - c))
            fwd.start()
            passed.append(fwd)
    for a, o in enumerate(out_refs):
        for j, (px, py) in enumerate(chips):
            half = o.at[2 * px + py, 1 - c]
            _remote(half, half, fsend_sems.at[3 * a + j], frecv_sems.at[3 * a + j], (x, y, 1 - c)).wait_recv()
    for cp in _gather_sends(w_refs, out_refs, send_sems, recv_sems) + passed:
        cp.wait_send()


def _swap_halves(arrs, name):
    n = len(arrs)

    def body(g_refs, a_refs, send_sems, recv_sems):
        x, y, c = _place()
        cps = [_remote(g.at[1 - c], a, send_sems.at[i], recv_sems.at[i], (x, y, 1 - c)) for i, (g, a) in enumerate(zip(g_refs, a_refs))]
        for cp in cps:
            cp.start()
        for cp in cps:
            cp.wait()

    return _hbm_call(body, name, arrs, [a.shape[1:] for a in arrs], [n, n])


def _scatter_copies(p_refs, b_refs, send_sems, recv_sems):
    x, y, c = _place()
    return [_remote(p.at[2 * px + py], b.at[j], send_sems.at[3 * i + j], recv_sems.at[3 * i + j], (px, py, c))
            for i, (p, b) in enumerate(zip(p_refs, b_refs)) for j, (px, py) in enumerate(_other_chips(x, y))]


def _share_halves(arrs):
    n = len(arrs)

    def body(q_refs, out_refs, send_sems, recv_sems):
        x, y, c = _place()
        cps = [_remote(q, o, send_sems.at[i], recv_sems.at[i], (x, y, 1 - c)) for i, (q, o) in enumerate(zip(q_refs, out_refs))]
        for cp in cps:
            cp.start()
        for cp in cps:
            cp.wait()

    return _hbm_call(body, "grads_share", arrs, [a.shape for a in arrs], [n, n])


_GROUPS = (
    (("gdn_w_out", (1, 256, 1024), "rows"), ("mlp_w_up", (2, 1024, 1024), "cols")),
    (("mlp_w_down", (2, 1024, 1024), "rows"), ("sb_w_q", (1, 256, 1024), "rows"), ("sb_w_o", (1, 256, 1024), "rows")),
    (("w_kv", (1024, 512), "cols"),),
    (("gdn_w_in", (1, 1024, 1028), "cols"),),
)
_BEHIND_CONV, _BEHIND_PREP, _FIRST = slice(0, 1), slice(1, 3), slice(3, 4)
_EARLY_GRADS = slice(0, 3)


def _numel(shape):
    n = 1
    for s in shape:
        n *= s
    return n


def _half_rows(shape):
    return _numel(shape[:-1]) // 2


def _pack_shards(shards, dtype):
    return tuple(jnp.concatenate([shards[n].astype(dtype).reshape(2, _half_rows(shape), shape[-1]) for n, shape, _ in grp], axis=1) for grp in _GROUPS)


def _unpack_shards(bufs):
    out = {}
    for grp, buf in zip(_GROUPS, bufs):
        off = 0
        for n, shape, _ in grp:
            out[n] = buf[:, off:off + _half_rows(shape)].reshape(shape)
            off += _half_rows(shape)
    return out


def _join(stacked, how):
    nd = stacked.ndim - 1
    ax = nd - 1 if how == "cols" else nd - 2
    moved = jnp.moveaxis(stacked, 0, ax)
    shape = list(stacked.shape[1:])
    shape[ax] *= N_CHIPS
    return moved.reshape(shape)


def _split(full, shard_shape, how):
    nd = len(shard_shape)
    ax = nd - 1 if how == "cols" else nd - 2
    shape = list(shard_shape)
    shape.insert(ax, N_CHIPS)
    return jnp.moveaxis(full.reshape(shape), ax, 0)


def _unpack_full(gathered, groups):
    out = {}
    for grp, buf in zip(groups, gathered):
        off = 0
        for n, shape, how in grp:
            out[n] = _join(buf[:, :, off:off + _half_rows(shape)].reshape((N_CHIPS,) + shape), how)
            off += _half_rows(shape)
    return out


def _pack_full(full, groups):
    bufs = []
    for grp in groups:
        parts = []
        for n, shape, how in grp:
            if isinstance(full[n], tuple):
                assert len(full[n]) == shape[0] == 2
                parts.append(jnp.stack([_split(layer, shape[1:], how) for layer in full[n]], axis=1))
            else:
                parts.append(_split(full[n], shape, how).reshape(N_CHIPS, 2, _half_rows(shape), shape[-1]))
        buf = jnp.swapaxes(jnp.concatenate(parts, axis=2), 0, 1)
        bufs.append(buf.reshape(2, -1, buf.shape[-1]))
    return tuple(bufs)


_SMALL = (
    ("mix_pre_gain", (2, 1024)),
    ("mix_post_gain", (2, 1024)),
    ("mlp_pre_gain", (2, 1024)),
    ("mlp_post_gain", (2, 1024)),
    ("kv_gain", (1024,)),
    ("gdn_out_gain", (1, 128)),
    ("gdn_a_log", (1, 8)),
    ("gdn_dt_bias", (1, 8)),
    ("gdn_conv_w", (1, 4, 3072)),
    ("loss", ()),
)


def _rows_of(shape):
    return -(-_numel(shape) // LANES)


def _pack_rows(vals, layout):
    parts = []
    for n, shape in layout:
        flat = vals[n].reshape(-1)
        parts.append(jnp.pad(flat, (0, _rows_of(shape) * LANES - flat.shape[0])))
    flat = jnp.concatenate(parts)
    rows = -(-flat.shape[0] // (8 * LANES)) * 8
    return jnp.pad(flat, (0, rows * LANES - flat.shape[0])).reshape(rows, LANES)


def _unpack_rows(packed, layout):
    flat = packed.reshape(-1)
    out, off = {}, 0
    for n, shape in layout:
        out[n] = flat[off:off + _numel(shape)].reshape(shape)
        off += _rows_of(shape) * LANES
    return out


_WEIGHTS = ("mix_pre_gain", "mix_post_gain", "mlp_pre_gain", "mlp_post_gain", "mlp_w_up", "mlp_w_down", "gdn_w_in", "gdn_conv_w",
            "gdn_a_log", "gdn_dt_bias", "gdn_out_gain", "gdn_w_out", "kv_gain", "w_kv", "sb_w_q", "sb_w_o")


def _as2d(a):
    return a.reshape(1, -1) if a.ndim <= 1 else a.reshape(-1, a.shape[-1])


def kernel(x, mix_pre_gain, mix_post_gain, mlp_pre_gain, mlp_post_gain, mlp_w_up, mlp_w_down, gdn_w_in, gdn_conv_w, gdn_a_log, gdn_dt_bias, gdn_out_gain, gdn_w_out, kv_gain, w_kv, sb_w_q, sb_w_o, loss_target, m_mix_pre_gain, m_mix_post_gain, m_mlp_pre_gain, m_mlp_post_gain, m_mlp_w_up, m_mlp_w_down, m_gdn_w_in, m_gdn_conv_w, m_gdn_a_log, m_gdn_dt_bias, m_gdn_out_gain, m_gdn_w_out, m_kv_gain, m_w_kv, m_sb_w_q, m_sb_w_o, v_mix_pre_gain, v_mix_post_gain, v_mlp_pre_gain, v_mlp_post_gain, v_mlp_w_up, v_mlp_w_down, v_gdn_w_in, v_gdn_conv_w, v_gdn_a_log, v_gdn_dt_bias, v_gdn_out_gain, v_gdn_w_out, v_kv_gain, v_w_kv, v_sb_w_q, v_sb_w_o):
    w = dict(mix_pre_gain=mix_pre_gain, mix_post_gain=mix_post_gain, mlp_pre_gain=mlp_pre_gain, mlp_post_gain=mlp_post_gain, mlp_w_up=mlp_w_up, mlp_w_down=mlp_w_down, gdn_w_in=gdn_w_in, gdn_conv_w=gdn_conv_w, gdn_a_log=gdn_a_log, gdn_dt_bias=gdn_dt_bias, gdn_out_gain=gdn_out_gain, gdn_w_out=gdn_w_out, kv_gain=kv_gain, w_kv=w_kv, sb_w_q=sb_w_q, sb_w_o=sb_w_o)
    m = dict(mix_pre_gain=m_mix_pre_gain, mix_post_gain=m_mix_post_gain, mlp_pre_gain=m_mlp_pre_gain, mlp_post_gain=m_mlp_post_gain, mlp_w_up=m_mlp_w_up, mlp_w_down=m_mlp_w_down, gdn_w_in=m_gdn_w_in, gdn_conv_w=m_gdn_conv_w, gdn_a_log=m_gdn_a_log, gdn_dt_bias=m_gdn_dt_bias, gdn_out_gain=m_gdn_out_gain, gdn_w_out=m_gdn_w_out, kv_gain=m_kv_gain, w_kv=m_w_kv, sb_w_q=m_sb_w_q, sb_w_o=m_sb_w_o)
    v = dict(mix_pre_gain=v_mix_pre_gain, mix_post_gain=v_mix_post_gain, mlp_pre_gain=v_mlp_pre_gain, mlp_post_gain=v_mlp_post_gain, mlp_w_up=v_mlp_w_up, mlp_w_down=v_mlp_w_down, gdn_w_in=v_gdn_w_in, gdn_conv_w=v_gdn_conv_w, gdn_a_log=v_gdn_a_log, gdn_dt_bias=v_gdn_dt_bias, gdn_out_gain=v_gdn_out_gain, gdn_w_out=v_gdn_w_out, kv_gain=v_kv_gain, w_kv=v_w_kv, sb_w_q=v_sb_w_q, sb_w_o=v_sb_w_o)
    cx, cy, cc = _place()
    chip = 2 * cx + cy
    conv_cols = gdn_conv_w.shape[-1]

    own = _pack_shards(w, BF16)
    own_taps = jnp.pad(gdn_conv_w[0], ((0, CONV_K), (0, 0))).reshape(2, CONV_K, conv_cols)
    with_own = lambda gathered, mine: [lax.dynamic_update_index_in_dim(g, m, chip, 0) for g, m in zip(gathered, mine)]

    def assemble_first(gathered):
        w_in_all, taps_all = with_own(gathered, (*own[_FIRST], own_taps))
        w_in = _unpack_full([w_in_all], _GROUPS[_FIRST])["gdn_w_in"][0]
        taps = jnp.swapaxes(taps_all[:, 0], 0, 1).reshape(CONV_K, N_CHIPS * conv_cols)
        return w_in[:, :4 * HEADS * HEAD_DIM], jnp.pad(w_in[:, 4 * HEADS * HEAD_DIM:], ((0, 0), (0, LANES - 2 * HEADS))), taps

    def assemble(gathered_conv, gathered_prep):
        full = {**_unpack_full(with_own(gathered_conv, own[_BEHIND_CONV]), _GROUPS[_BEHIND_CONV]),
                **_unpack_full(with_own(gathered_prep, own[_BEHIND_PREP]), _GROUPS[_BEHIND_PREP])}
        return full["gdn_w_out"][0], full["w_kv"], full["sb_w_q"][0], full["sb_w_o"][0], full["mlp_w_up"], full["mlp_w_down"]

    gains = (mix_pre_gain, mix_post_gain, mlp_pre_gain, mlp_post_gain, kv_gain[None])
    small = (gdn_a_log, gdn_dt_bias, gdn_out_gain)
    tile = PACK_ROW_TILE

    def to_chip_partials(grads_full, groups, tag):
        bufs = _pack_full(grads_full, groups)
        p32, p16 = [], []
        for i, (buf, other) in enumerate(zip(bufs, _swap_halves(bufs, f"grads_to_sibling_{tag}"))):
            _, n, cols = buf.shape
            p, pb = _add_rows(f"grads_add_sibling_{tag}{i}", [(buf.reshape(2 * n, cols), cc * (n // tile)), (other, 0)], n, (F32, BF16), tile)
            p32.append(p.reshape(N_CHIPS, -1, cols))
            p16.append(pb.reshape(N_CHIPS, -1, cols))
        return p32, tuple(p16)

    loss_rows, grad_x, g_full, (partial, from_chips) = _local_step(
        x[0], loss_target[0], gains, small, ((*own[_FIRST], own_taps), own[_BEHIND_CONV], own[_BEHIND_PREP]), assemble_first, assemble,
        lambda g: to_chip_partials(g, _GROUPS[_EARLY_GRADS], "early"), lambda g: to_chip_partials(g, _GROUPS[_FIRST], "late"))

    reduced = []
    for i, (p, others) in enumerate(zip(partial, from_chips)):
        _, r, cols = p.shape
        terms = [(p.reshape(N_CHIPS * r, cols), chip * (r // tile))] + [(others.reshape(3 * r, cols), j * (r // tile)) for j in range(3)]
        reduced.append(_add_rows(f"grads_add_chips_{i}", terms, r, (F32,), tile)[0])
    g_shard = _unpack_shards([jnp.where(cc == 0, jnp.stack([r, o]), jnp.stack([o, r])) for r, o in zip(reduced, _share_halves(tuple(reduced)))])

    g_small_local = {n: g_full[n] for n, _ in _SMALL if n != "loss"}
    g_small_local["loss"] = loss_rows[0, 0]
    _, small_sum = _gather8(_pack_rows(g_small_local, _SMALL), "allreduce_small")
    g_small = _unpack_rows(small_sum, _SMALL)
    loss = g_small.pop("loss")
    g_small["gdn_conv_w"] = lax.dynamic_slice_in_dim(g_small["gdn_conv_w"], chip * conv_cols, conv_cols, axis=2)

    grads = {**g_shard, **g_small}
    deltas, new_m, new_v = {}, {}, {}
    for n in _WEIGHTS:
        d2, m2, v2 = _adamw(_as2d(w[n]), _as2d(grads[n]), _as2d(m[n]), _as2d(v[n]), "adamw_" + n)
        deltas[n], new_m[n], new_v[n] = d2.reshape(w[n].shape), m2.reshape(w[n].shape), v2.reshape(w[n].shape)
    return (loss, grad_x[None], *[grads[n].reshape(w[n].shape) for n in _WEIGHTS], *[deltas[n] for n in _WEIGHTS],
            *[new_m[n] for n in _WEIGHTS], *[new_v[n] for n in _WEIGHTS])
```

```python
import functools

import jax
import jax.numpy as jnp
from jax import lax
from jax.experimental import pallas as pl
from jax.experimental.pallas import tpu as pltpu

F32, BF16 = jnp.float32, jnp.bfloat16
HI = lax.Precision.HIGHEST
MESH = pl.DeviceIdType.MESH

EPS = 1e-6
HEADS = 8
HEAD_DIM = 128
CHUNK = 64
CHUNK_SHIFT = CHUNK.bit_length() - 1
CONV_K = 4
QKV = 3 * HEADS * HEAD_DIM

ADAM_LR, ADAM_B1, ADAM_B2, ADAM_EPS, ADAM_WD, ADAM_STEP = 0.001, 0.9, 0.999, 1e-08, 0.01, 10

VMEM_LIMIT_BYTES = 48 * 1024 * 1024
LANES = 128

NN = ((1,), (0,))
NT = ((1,), (1,))
TN = ((0,), (0,))


def _dot(a, b, dims=NN, precision=None):
    return lax.dot_general(a, b, (dims, ((), ())), precision=precision, preferred_element_type=F32)


def _params(*sem):
    return pltpu.CompilerParams(dimension_semantics=sem, vmem_limit_bytes=VMEM_LIMIT_BYTES)


def _iota(shape, axis):
    return lax.broadcasted_iota(jnp.int32, shape, axis)


def _matmul(a, b, mode, out_dtype, name, tm=1024, tn=1024, tk=2048, add=None, epilogue=None, extras=()):
    if mode == "nn":
        (m, k), (k2, n) = a.shape, b.shape
    elif mode == "nt":
        (m, k), (n, k2) = a.shape, b.shape
    else:
        (k, m), (k2, n) = a.shape, b.shape
    assert k == k2, (a.shape, b.shape, mode)
    tm, tn, tk = min(tm, m), min(tn, n), min(tk, k)
    assert m % tm == 0 and n % tn == 0 and k % tk == 0, (a.shape, b.shape, mode)
    nk = k // tk
    dims = {"nn": NN, "nt": NT, "tn": TN}[mode]
    tiles = ([add] if add is not None else []) + list(extras)
    out_dtypes = out_dtype if epilogue is not None else (out_dtype,)
    n_in = 2 + len(tiles)

    def finish(acc, extra_refs, o_refs):
        res = (acc,) if epilogue is None else epilogue(acc, *[r[...] for r in extra_refs])
        for o_ref, r in zip(o_refs, res):
            o_ref[...] = r.astype(o_ref.dtype)

    def body(*refs):
        a_ref, b_ref = refs[:2]
        extra_refs = refs[n_in - len(extras):n_in]
        o_refs, acc_ref = refs[n_in:-1], refs[-1]
        prod = _dot(a_ref[...].astype(BF16), b_ref[...].astype(BF16), dims)
        if nk == 1:
            finish(prod + refs[2][...].astype(F32) if add is not None else prod, extra_refs, o_refs)
            return
        kk = pl.program_id(2)

        @pl.when(kk == 0)
        def _():
            acc_ref[...] = refs[2][...].astype(F32) if add is not None else jnp.zeros_like(acc_ref)

        acc_ref[...] += prod

        @pl.when(kk == nk - 1)
        def _():
            finish(acc_ref[...], extra_refs, o_refs)

    a_spec = pl.BlockSpec((tk, tm), lambda i, j, kk: (kk, i)) if mode == "tn" else pl.BlockSpec((tm, tk), lambda i, j, kk: (i, kk))
    b_spec = pl.BlockSpec((tn, tk), lambda i, j, kk: (j, kk)) if mode == "nt" else pl.BlockSpec((tk, tn), lambda i, j, kk: (kk, j))
    o_spec = pl.BlockSpec((tm, tn), lambda i, j, kk: (i, j))
    res = pl.pallas_call(
        body,
        name=name,
        grid=(m // tm, n // tn, nk),
        in_specs=[a_spec, b_spec] + [o_spec] * len(tiles),
        out_specs=[o_spec] * len(out_dtypes),
        out_shape=[jax.ShapeDtypeStruct((m, n), dt) for dt in out_dtypes],
        scratch_shapes=[pltpu.VMEM((tm, tn), F32)],
        compiler_params=_params("parallel", "parallel", "arbitrary"),
    )(a, b, *tiles)
    return res if epilogue is not None else res[0]


def _row_specs(rows, tm):
    return [pl.BlockSpec((tm, w), lambda i, cb=cb: (i, cb)) for _, w, cb in rows]


def _full_spec(p):
    return pl.BlockSpec(p.shape, lambda i: (0,) * p.ndim)


ROW_TILE = 512


def _rowwise(name, fn, rows, params, outs, tm=ROW_TILE, gather=()):
    t = rows[0][0].shape[0]
    tm = min(tm, t)
    steps = t // tm
    nr, npar, nout, ng = len(rows), len(params), len(outs), len(gather)

    def body(*refs):
        ins = [r[...].astype(F32) for r in refs[:nr]]
        ps = [p[...] for p in refs[nr:nr + npar]]
        shard_refs = refs[nr + npar:nr + npar + ng]
        o_refs = refs[nr + npar + ng:nr + npar + ng + nout]
        all_refs, sems = refs[nr + npar + ng + nout:nr + npar + 2 * ng + nout], refs[nr + npar + 2 * ng + nout:]
        if ng:
            @pl.when(pl.program_id(0) == 0)
            def _():
                for cp in _gather_sends(shard_refs, all_refs, *sems[:2]):
                    cp.start()

        res = fn(*ins, *ps)
        for o_ref, r in zip(o_refs, res):
            o_ref[...] = r.astype(o_ref.dtype)

        if ng:
            @pl.when(pl.program_id(0) == steps - 1)
            def _():
                _gather_finish(shard_refs, all_refs, *sems)

    return pl.pallas_call(
        body,
        name=name,
        grid=(steps,),
        in_specs=_row_specs(rows, tm) + [_full_spec(p) for p in params] + [_HBM] * ng,
        out_specs=[pl.BlockSpec((tm, w), lambda i: (i, 0)) for w, _ in outs] + [_HBM] * ng,
        out_shape=[jax.ShapeDtypeStruct((t, w), dt) for w, dt in outs] + [jax.ShapeDtypeStruct((N_CHIPS,) + s.shape, s.dtype) for s in gather],
        scratch_shapes=[pltpu.SemaphoreType.DMA((3 * ng,))] * (4 if ng else 0),
        compiler_params=_params("arbitrary" if ng else "parallel"),
    )(*[r[0] for r in rows], *params, *gather)


def _add_rows(name, terms, n_rows, out_dtypes, tm):
    cols = terms[0][0].shape[1]
    firsts = jnp.stack([jnp.asarray(first, jnp.int32) for _, first in terms])

    def body(firsts_ref, *refs):
        acc = refs[0][...].astype(F32)
        for r in refs[1:len(terms)]:
            acc = acc + r[...].astype(F32)
        for o_ref in refs[len(terms):]:
            o_ref[...] = acc.astype(o_ref.dtype)

    return pl.pallas_call(
        body,
        name=name,
        grid_spec=pltpu.PrefetchScalarGridSpec(
            num_scalar_prefetch=1,
            grid=(n_rows // tm,),
            in_specs=[pl.BlockSpec((tm, cols), lambda i, firsts_ref, k=k: (firsts_ref[k] + i, 0)) for k in range(len(terms))],
            out_specs=[pl.BlockSpec((tm, cols), lambda i, firsts_ref: (i, 0)) for _ in out_dtypes],
        ),
        out_shape=[jax.ShapeDtypeStruct((n_rows, cols), dt) for dt in out_dtypes],
        compiler_params=_params("parallel"),
    )(firsts, *[a for a, _ in terms])


def _rowwise_bwd(name, fn, rows, params, cots, grad_dtypes, tm=ROW_TILE, partials=()):
    t = rows[0][0].shape[0]
    tm = min(tm, t)
    steps = t // tm
    nr, npar, nc, nsc = len(rows), len(params), len(cots), len(partials)
    want = [j for j, dt in enumerate(grad_dtypes) if dt is not None]
    widths = [rows[j][1] for j in want]
    n_row_outs = len(want)
    n_in = nr + npar + nc

    def body(*refs):
        i = pl.program_id(0)
        ins = [r[...].astype(F32) for r in refs[:nr]]
        ps = [p[...] for p in refs[nr:nr + npar]]
        cs = tuple(c[...].astype(F32) for c in refs[nr + npar:n_in])
        p_refs = refs[n_in:n_in + nsc]
        outs = refs[n_in + nsc:n_in + nsc + n_row_outs + npar]
        from_refs, sems = refs[n_in + nsc + n_row_outs + npar:n_in + 2 * nsc + n_row_outs + npar], refs[n_in + 2 * nsc + n_row_outs + npar:]
        if nsc:
            @pl.when(i == 0)
            def _():
                for cp in _scatter_copies(p_refs, from_refs, *sems):
                    cp.start()

        _, vjp = jax.vjp(fn, *ins, *ps)
        gs = vjp(cs)
        for o_ref, j in zip(outs, want):
            o_ref[...] = gs[j].astype(o_ref.dtype)
        pg_refs = outs[n_row_outs:]

        @pl.when(i == 0)
        def _():
            for pg in pg_refs:
                pg[...] = jnp.zeros_like(pg)

        for pg, g in zip(pg_refs, gs[nr:]):
            pg[...] += g

        if nsc:
            @pl.when(i == steps - 1)
            def _():
                for cp in _scatter_copies(p_refs, from_refs, *sems):
                    cp.wait()

    row_specs = [pl.BlockSpec((tm, w), lambda i: (i, 0)) for w in widths]
    row_shapes = [jax.ShapeDtypeStruct((t, w), grad_dtypes[j]) for j, w in zip(want, widths)]
    res = pl.pallas_call(
        body,
        name=name,
        grid=(steps,),
        in_specs=_row_specs(rows, tm) + [_full_spec(p) for p in params] + [pl.BlockSpec((tm, c.shape[1]), lambda i: (i, 0)) for c in cots] + [_HBM] * nsc,
        out_specs=row_specs + [_full_spec(p) for p in params] + [_HBM] * nsc,
        out_shape=row_shapes + [jax.ShapeDtypeStruct(p.shape, F32) for p in params] + [jax.ShapeDtypeStruct((3,) + p.shape[1:], p.dtype) for p in partials],
        scratch_shapes=[pltpu.SemaphoreType.DMA((3 * nsc,))] * (2 if nsc else 0),
        compiler_params=_params("arbitrary"),
    )(*[r[0] for r in rows], *params, *cots, *partials)
    return res[:n_row_outs], res[n_row_outs:n_row_outs + npar], res[n_row_outs + npar:]


def _rms(x, g):
    return x * lax.rsqrt(jnp.mean(x * x, axis=-1, keepdims=True) + EPS) * g


def _sigmoid(x):
    return 1.0 / (1.0 + jnp.exp(-x))


def _softplus(x):
    return jnp.maximum(x, 0.0) + jnp.log1p(jnp.exp(-jnp.abs(x)))


def _two_pass(x, m):
    hi = x.astype(BF16)
    lo = (x - hi.astype(F32)).astype(BF16)
    return _dot(hi, m) + _dot(lo, m)


def _head_sum_impl(x):
    sums = [jnp.sum(x[:, h * HEAD_DIM:(h + 1) * HEAD_DIM], axis=-1, keepdims=True) for h in range(HEADS)]
    return jnp.concatenate([jnp.broadcast_to(s, (x.shape[0], HEAD_DIM)) for s in sums], axis=1)


@jax.custom_vjp
def _head_sum(x):
    return _head_sum_impl(x)


_head_sum.defvjp(lambda x: (_head_sum_impl(x), None), lambda _, g: (_head_sum_impl(g),))


def _fn_norm(x, g):
    return (_rms(x, g),)


def _fn_gates(ba, al, dt):
    col = _iota((1, LANES), 1)
    g = jnp.where((col >= HEADS) & (col < 2 * HEADS), -jnp.exp(al) * _softplus(ba + dt), 0.0)
    rows = ba.shape[0]
    r, c = _iota((rows, rows), 0), _iota((rows, rows), 1)
    same = (r >> CHUNK_SHIFT) == (c >> CHUNK_SHIFT)
    gc = _dot(jnp.where(same & (r >= c), 1.0, 0.0), g, precision=HI)
    gtot = _dot(jnp.where(same, 1.0, 0.0), g, precision=HI)
    return _sigmoid(ba), gc, gtot


def _fn_post_q(c):
    s = c * _sigmoid(c)
    return (s * lax.rsqrt(_head_sum(s * s) + EPS) * (HEAD_DIM ** -0.5),)


def _fn_post_k(c):
    s = c * _sigmoid(c)
    return (s * lax.rsqrt(_head_sum(s * s) + EPS),)


def _fn_post_v(c):
    return (c * _sigmoid(c),)


def _fn_post(cq, ck, cv):
    return _fn_post_q(cq) + _fn_post_k(ck) + _fn_post_v(cv)


def _fn_outnorm(o, gate, og):
    y = o * lax.rsqrt(_head_sum(o * o) * (1.0 / HEAD_DIM) + EPS) * og
    return (y * (gate * _sigmoid(gate)),)


def _fn_res_norm(x, m, gp, gn):
    x1 = x + _rms(m, gp)
    return x1, _rms(x1, gn)


def _fn_res_norm2(x, m, gp, ga, gb):
    x1 = x + _rms(m, gp)
    return x1, _rms(x1, ga), _rms(x1, gb)


def _relu2_of(u):
    r = jnp.maximum(u, 0.0)
    return (r * r,)


def _relu2_cotangent(da, a):
    return (da * (2.0 * jnp.sqrt(a.astype(F32))),)


def _loss_call(x3, d1, tgt, g, tm=ROW_TILE):
    t, d = x3.shape
    tm = min(tm, t)

    def body(x_ref, d_ref, t_ref, g_ref, loss_ref, dx_ref, dd_ref, dg_ref):
        i = pl.program_id(0)
        y, vjp = jax.vjp(lambda x, dd, gg: x + _rms(dd, gg), x_ref[...], d_ref[...], g_ref[...])
        err = y - t_ref[...]
        lrow = 0.5 * jnp.mean(err * err, axis=-1, keepdims=True)
        dx, dd, dg = vjp(err * (1.0 / d))
        dx_ref[...] = dx
        dd_ref[...] = dd.astype(dd_ref.dtype)

        @pl.when(i == 0)
        def _():
            loss_ref[...] = jnp.zeros_like(loss_ref)
            dg_ref[...] = jnp.zeros_like(dg_ref)

        loss_ref[...] += jnp.broadcast_to(jnp.sum(lrow, axis=0, keepdims=True), loss_ref.shape)
        dg_ref[...] += dg

    row = pl.BlockSpec((tm, d), lambda i: (i, 0))
    return pl.pallas_call(
        body,
        name="loss_head",
        grid=(t // tm,),
        in_specs=[row, row, row, _full_spec(g)],
        out_specs=[pl.BlockSpec((8, LANES), lambda i: (0, 0)), row, row, _full_spec(g)],
        out_shape=[jax.ShapeDtypeStruct((8, LANES), F32), jax.ShapeDtypeStruct((t, d), F32), jax.ShapeDtypeStruct((t, d), BF16), jax.ShapeDtypeStruct(g.shape, F32)],
        compiler_params=_params("arbitrary"),
    )(x3, d1, tgt, g)


HALO = 8


def _conv_fwd(qkvg, conv_w, shards, tm=256):
    t = qkvg.shape[0]
    tm = min(tm, t)
    steps = t // tm
    wide = QKV // 3
    n = len(shards)

    def body(*refs):
        cur_ref, prev_ref, w_ref = refs[:3]
        shard_refs = refs[3:3 + n]
        o_ref, q_ref, k_ref, v_ref = refs[3 + n:7 + n]
        all_refs = refs[7 + n:7 + 2 * n]
        buf, sems = refs[7 + 2 * n], refs[8 + 2 * n:]
        i = pl.program_id(0)

        if n:
            @pl.when(i == 0)
            def _():
                for cp in _gather_sends(shard_refs, all_refs, *sems[:2]):
                    cp.start()

        buf[0:HALO, :] = jnp.where(i > 0, prev_ref[...], 0.0)
        buf[HALO:, :] = cur_ref[...]
        acc = buf[pl.ds(HALO - CONV_K + 1, tm), :] * w_ref[pl.ds(0, 1), :]
        for j in range(1, CONV_K):
            acc = acc + buf[pl.ds(HALO - CONV_K + 1 + j, tm), :] * w_ref[pl.ds(j, 1), :]
        o_ref[...] = acc
        (q_ref[...], k_ref[...], v_ref[...]) = _fn_post(acc[:, 0:wide], acc[:, wide:2 * wide], acc[:, 2 * wide:])

        if n:
            @pl.when(i == steps - 1)
            def _():
                _gather_finish(shard_refs, all_refs, *sems)

    part = pl.BlockSpec((tm, wide), lambda i: (i, 0))
    res = pl.pallas_call(
        body,
        name="conv_fwd",
        grid=(steps,),
        in_specs=[
            pl.BlockSpec((tm, QKV), lambda i: (i, 0)),
            pl.BlockSpec((HALO, QKV), lambda i: (jnp.maximum(i * (tm // HALO) - 1, 0), 0)),
            pl.BlockSpec((CONV_K, QKV), lambda i: (0, 0)),
        ] + [_HBM] * n,
        out_specs=[pl.BlockSpec((tm, QKV), lambda i: (i, 0)), part, part, part] + [_HBM] * n,
        out_shape=[jax.ShapeDtypeStruct((t, QKV), F32)] + [jax.ShapeDtypeStruct((t, wide), F32)] * 3
        + [jax.ShapeDtypeStruct((N_CHIPS,) + s.shape, s.dtype) for s in shards],
        scratch_shapes=[pltpu.VMEM((tm + HALO, QKV), F32)] + [pltpu.SemaphoreType.DMA((3 * n,))] * (4 if n else 0),
        compiler_params=_params("arbitrary"),
    )(qkvg, qkvg, conv_w, *shards)
    return res[:4], res[4:]


def _conv_bwd(conv, dqkv, dgate, qkvg, conv_w, tm=256):
    t = conv.shape[0]
    tm = min(tm, t)
    n = t // tm
    wg = dgate.shape[1]
    wide = QKV // 3

    def conv_cotangent(c_ref, g_refs):
        parts = [c_ref[:, j * wide:(j + 1) * wide] for j in range(3)]
        _, vjp = jax.vjp(_fn_post, *parts)
        return vjp(tuple(g[...] for g in g_refs))

    def body(c_ref, cn_ref, dq_ref, dk_ref, dv_ref, dqn_ref, dkn_ref, dvn_ref, dgate_ref, x_ref, xp_ref, w_ref, dx_ref, dw_ref, bufd, bufx):
        i = pl.program_id(0)
        for j, (cur, nxt) in enumerate(zip(conv_cotangent(c_ref, (dq_ref, dk_ref, dv_ref)), conv_cotangent(cn_ref, (dqn_ref, dkn_ref, dvn_ref)))):
            bufd[0:tm, j * wide:(j + 1) * wide] = cur
            bufd[tm:, j * wide:(j + 1) * wide] = jnp.where(i < n - 1, nxt, 0.0)
        bufx[0:HALO, :] = jnp.where(i > 0, xp_ref[...], 0.0)
        bufx[HALO:, :] = x_ref[...]

        @pl.when(i == 0)
        def _():
            dw_ref[...] = jnp.zeros_like(dw_ref)

        dcv = bufd[0:tm, :]
        acc = bufd[pl.ds(CONV_K - 1, tm), :] * w_ref[pl.ds(0, 1), :]
        for j in range(1, CONV_K):
            acc = acc + bufd[pl.ds(CONV_K - 1 - j, tm), :] * w_ref[pl.ds(j, 1), :]
        dx_ref[:, 0:QKV] = acc.astype(dx_ref.dtype)
        dx_ref[:, QKV:] = dgate_ref[...].astype(dx_ref.dtype)
        for j in range(CONV_K):
            dw_ref[pl.ds(j, 1), :] += jnp.sum(dcv * bufx[pl.ds(HALO - CONV_K + 1 + j, tm), :], axis=0, keepdims=True)

    def cur(width):
        return pl.BlockSpec((tm, width), lambda i: (i, 0))

    def nxt(width):
        return pl.BlockSpec((HALO, width), lambda i: (jnp.minimum((i + 1) * (tm // HALO), t // HALO - 1), 0))

    return pl.pallas_call(
        body,
        name="conv_bwd",
        grid=(n,),
        in_specs=[cur(QKV), nxt(QKV)] + [cur(wide)] * 3 + [nxt(wide)] * 3 + [
            cur(wg),
            cur(QKV),
            pl.BlockSpec((HALO, QKV), lambda i: (jnp.maximum(i * (tm // HALO) - 1, 0), 0)),
            pl.BlockSpec((CONV_K, QKV), lambda i: (0, 0)),
        ],
        out_specs=[pl.BlockSpec((tm, QKV + wg), lambda i: (i, 0)), pl.BlockSpec((HALO, QKV), lambda i: (0, 0))],
        out_shape=[jax.ShapeDtypeStruct((t, QKV + wg), BF16), jax.ShapeDtypeStruct((HALO, QKV), F32)],
        scratch_shapes=[pltpu.VMEM((tm + HALO, QKV), F32), pltpu.VMEM((tm + HALO, QKV), F32)],
        compiler_params=_params("arbitrary"),
    )(conv, conv, *dqkv, *dqkv, dgate, qkvg, qkvg, conv_w)


PREP_CHUNKS = 32
PREP_BWD_CHUNKS = 4
SCAN_CHUNKS = 8


def _hi_lo(x):
    hi = x.astype(BF16)
    return hi, (x - hi.astype(F32)).astype(BF16)


def _mm3(a, b, dims=NN):
    (ah, al), (bh, bl) = _hi_lo(a), _hi_lo(b)
    return _dot(ah, bh, dims) + (_dot(ah, bl, dims) + _dot(al, bh, dims))


def _neumann(lowers):
    c = lowers[0].shape[0]
    eye = jnp.where(_iota((c, c), 0) == _iota((c, c), 1), 1.0, 0.0)
    ps = [-low for low in lowers]
    tmats = [eye + p for p in ps]
    for _ in range(CHUNK_SHIFT - 1):
        ps = [_mm3(p, p) for p in ps]
        tmats = [t + _mm3(t, p) for t, p in zip(tmats, ps)]
    return tuple(tmats)


def _inv_cotangents(tmats, dts):
    half = [_mm3(t, dt, TN) for t, dt in zip(tmats, dts)]
    return tuple(-_mm3(hf, t, NT) for hf, t in zip(half, tmats))


@jax.custom_vjp
def _tri_inv(lowers):
    return _neumann(lowers)


def _tri_inv_fwd(lowers):
    tmats = _neumann(lowers)
    return tmats, tmats


_tri_inv.defvjp(_tri_inv_fwd, lambda tmats, dts: (_inv_cotangents(tmats, dts),))


@jax.custom_vjp
def _tri_inv_known(lowers, tmats):
    return tmats


_tri_inv_known.defvjp(lambda lowers, tmats: (tmats, tmats),
                      lambda tmats, dts: (_inv_cotangents(tmats, dts), tuple(jnp.zeros_like(t) for t in tmats)))


def _prep_chunks(qs, ks, vs, bs, gcs, gts, gcrs, tmats=None):
    c = CHUNK
    r, col = _iota((c, c), 0), _iota((c, c), 1)
    incl, strict = r >= col, r > col
    decays = [jnp.where(incl, jnp.exp(jnp.where(incl, gc - gcr, 0.0)), 0.0) for gc, gcr in zip(gcs, gcrs)]
    kbs = [k * b for k, b in zip(ks, bs)]
    kbfs = [k.astype(BF16) for k in ks]
    lowers = tuple(jnp.where(strict, _dot(kb.astype(BF16), kbf, NT) * decay, 0.0) for kb, kbf, decay in zip(kbs, kbfs, decays))
    tmats = _tri_inv(lowers) if tmats is None else _tri_inv_known(lowers, tuple(tmats))
    outs = []
    for q, k, v, b, gc, gt, kb, kbf, decay, tmat in zip(qs, ks, vs, bs, gcs, gts, kbs, kbfs, decays, tmats):
        tb = tmat.astype(BF16)
        egc = jnp.exp(gc)
        w = _dot(tb, (kb * egc).astype(BF16))
        u = _dot(tb, (v * b).astype(BF16))
        attn = _dot(q.astype(BF16), kbf, NT) * decay
        gl = jnp.broadcast_to(jnp.exp(jnp.mean(gt.reshape(c // 8, 8, 1), axis=0)), (8, HEAD_DIM))
        outs.append((w, u, q * egc, k * jnp.exp(gt - gc), attn, gl))
    return tuple(outs), tmats


def _prep_specs(rows, gch):
    head = pl.BlockSpec((rows, HEAD_DIM), lambda n, h: (n, h))
    gates = pl.BlockSpec((rows, LANES), lambda n, h: (n, 0))
    gcrow = pl.BlockSpec((1, gch, 1, CHUNK), lambda n, h: (h, n, 0, 0))
    square = pl.BlockSpec((1, rows, CHUNK), lambda n, h: (h, n, 0))
    gl = pl.BlockSpec((1, gch * 8, HEAD_DIM), lambda n, h: (h, n, 0))
    return head, gates, gcrow, square, gl


def _pick_lane(ref, sl, lane):
    return jnp.sum(jnp.where(_iota((1, LANES), 1) == lane, ref[sl, :], 0.0), axis=1, keepdims=True)


def _prep_inputs(q_ref, k_ref, v_ref, b_ref, gc_ref, gt_ref, gcr_ref, sls, h):
    return ([q_ref[sl, :] for sl in sls], [k_ref[sl, :] for sl in sls], [v_ref[sl, :] for sl in sls],
            [_pick_lane(b_ref, sl, h) for sl in sls], [_pick_lane(gc_ref, sl, h + HEADS) for sl in sls],
            [_pick_lane(gt_ref, sl, h + HEADS) for sl in sls], [gcr_ref[0, c] for c in range(len(sls))])


def _gdn_prep(q, k, v, beta, gc, gt, gcr, shards=()):
    t = q.shape[0]
    gch = min(PREP_CHUNKS, t // CHUNK)
    rows = gch * CHUNK
    steps = t // rows
    n = len(shards)

    def body(*refs):
        q_ref, k_ref, v_ref, b_ref, gc_ref, gt_ref, gcr_ref = refs[:7]
        shard_refs = refs[7:7 + n]
        w_ref, u_ref, qg_ref, kg_ref, at_ref, gl_ref, tm_ref = refs[7 + n:14 + n]
        all_refs, sems = refs[14 + n:14 + 2 * n], refs[14 + 2 * n:]
        h = pl.program_id(1)

        if n:
            @pl.when(jnp.logical_and(pl.program_id(0) == 0, h == 0))
            def _():
                for cp in _gather_sends(shard_refs, all_refs, *sems[:2]):
                    cp.start()

        sls = [pl.ds(c * CHUNK, CHUNK) for c in range(gch)]
        outs, tmats = _prep_chunks(*_prep_inputs(q_ref, k_ref, v_ref, b_ref, gc_ref, gt_ref, gcr_ref, sls, h))
        for c, (sl, (w, u, qg, kg, attn, gl), tmat) in enumerate(zip(sls, outs, tmats)):
            w_ref[sl, :] = w.astype(BF16)
            u_ref[sl, :] = u
            qg_ref[sl, :] = qg.astype(BF16)
            kg_ref[sl, :] = kg.astype(BF16)
            at_ref[0, sl, :] = attn.astype(BF16)
            gl_ref[0, pl.ds(c * 8, 8), :] = gl
            tm_ref[0, sl, :] = tmat

        if n:
            @pl.when(jnp.logical_and(pl.program_id(0) == steps - 1, h == HEADS - 1))
            def _():
                _gather_finish(shard_refs, all_refs, *sems)

    hb, col, gcrow, square, glb = _prep_specs(rows, gch)
    wide = HEADS * HEAD_DIM
    res = pl.pallas_call(
        body,
        name="gdn_prep",
        grid=(steps, HEADS),
        in_specs=[hb, hb, hb, col, col, col, gcrow] + [_HBM] * n,
        out_specs=[hb, hb, hb, hb, square, glb, square] + [_HBM] * n,
        out_shape=[
            jax.ShapeDtypeStruct((t, wide), BF16),
            jax.ShapeDtypeStruct((t, wide), F32),
            jax.ShapeDtypeStruct((t, wide), BF16),
            jax.ShapeDtypeStruct((t, wide), BF16),
            jax.ShapeDtypeStruct((HEADS, t, CHUNK), BF16),
            jax.ShapeDtypeStruct((HEADS, t // CHUNK * 8, HEAD_DIM), F32),
            jax.ShapeDtypeStruct((HEADS, t, CHUNK), F32),
        ] + [jax.ShapeDtypeStruct((N_CHIPS,) + s.shape, s.dtype) for s in shards],
        scratch_shapes=[pltpu.SemaphoreType.DMA((3 * n,))] * (4 if n else 0),
        compiler_params=_params("arbitrary", "arbitrary") if n else _params("parallel", "parallel"),
    )(q, k, v, beta, gc, gt, gcr, *shards)
    return res[:7], res[7:]


def _gdn_prep_bwd(q, k, v, beta, gc, gt, gcr, tmat, dw, du, dqg, dkg, dattn, dgl, partials=()):
    t = q.shape[0]
    gch = min(PREP_BWD_CHUNKS, t // CHUNK)
    rows = gch * CHUNK
    steps = t // rows
    n_sc = len(partials)

    def body(*refs):
        (q_ref, k_ref, v_ref, b_ref, gc_ref, gt_ref, gcr_ref, tm_ref, dw_ref, du_ref, dqg_ref, dkg_ref, dat_ref, dgl_ref) = refs[:14]
        p_refs = refs[14:14 + n_sc]
        dq_ref, dk_ref, dv_ref, db_ref, dgc_ref, dgt_ref, dgcr_ref = refs[14 + n_sc:21 + n_sc]
        from_refs, sems = refs[21 + n_sc:21 + 2 * n_sc], refs[21 + 2 * n_sc:]
        h = pl.program_id(1)
        lane = _iota((1, LANES), 1)

        if n_sc:
            @pl.when(jnp.logical_and(pl.program_id(0) == 0, h == 0))
            def _():
                for cp in _scatter_copies(p_refs, from_refs, *sems):
                    cp.start()

        @pl.when(h == 0)
        def _():
            db_ref[...] = jnp.zeros_like(db_ref)
            dgc_ref[...] = jnp.zeros_like(dgc_ref)
            dgt_ref[...] = jnp.zeros_like(dgt_ref)

        sls = [pl.ds(c * CHUNK, CHUNK) for c in range(gch)]
        known = [tm_ref[0, sl, :] for sl in sls]
        _, vjp = jax.vjp(lambda *a: _prep_chunks(*a, tmats=known)[0], *_prep_inputs(q_ref, k_ref, v_ref, b_ref, gc_ref, gt_ref, gcr_ref, sls, h))
        cots = tuple((dw_ref[sl, :], du_ref[sl, :], dqg_ref[sl, :], dkg_ref[sl, :], dat_ref[0, sl, :], dgl_ref[0, pl.ds(c * 8, 8), :]) for c, sl in enumerate(sls))
        dqs, dks, dvs, dbs, dgcs, dgts, dgcrs = vjp(cots)
        for c, sl in enumerate(sls):
            dq_ref[sl, :] = dqs[c]
            dk_ref[sl, :] = dks[c]
            dv_ref[sl, :] = dvs[c]
            db_ref[sl, :] += jnp.where(lane == h, dbs[c], 0.0)
            dgc_ref[sl, :] += jnp.where(lane == h + HEADS, dgcs[c], 0.0)
            dgt_ref[sl, :] += jnp.where(lane == h + HEADS, dgts[c], 0.0)
            dgcr_ref[0, c] = dgcrs[c]

        if n_sc:
            @pl.when(jnp.logical_and(pl.program_id(0) == steps - 1, h == HEADS - 1))
            def _():
                for cp in _scatter_copies(p_refs, from_refs, *sems):
                    cp.wait()

    hb, col, gcrow, square, glb = _prep_specs(rows, gch)
    wide = HEADS * HEAD_DIM
    res = pl.pallas_call(
        body,
        name="gdn_prep_bwd",
        grid=(steps, HEADS),
        in_specs=[hb, hb, hb, col, col, col, gcrow, square, hb, hb, hb, hb, square, glb] + [_HBM] * n_sc,
        out_specs=[hb, hb, hb, col, col, col, gcrow] + [_HBM] * n_sc,
        out_shape=[jax.ShapeDtypeStruct((t, wide), F32)] * 3 + [jax.ShapeDtypeStruct((t, LANES), F32)] * 3 + [jax.ShapeDtypeStruct((HEADS, t // CHUNK, 1, CHUNK), F32)]
        + [jax.ShapeDtypeStruct((3,) + p.shape[1:], p.dtype) for p in partials],
        scratch_shapes=[pltpu.SemaphoreType.DMA((3 * n_sc,))] * (2 if n_sc else 0),
        compiler_params=_params("arbitrary", "arbitrary"),
    )(q, k, v, beta, gc, gt, gcr, tmat, dw, du, dqg, dkg, dattn, dgl, *partials)
    return res[:7], res[7:]


def _gdn_scan(w, u, qg, kg, attn, gl):
    t = w.shape[0]
    n = t // CHUNK
    nch = min(SCAN_CHUNKS, n)
    wide = HEADS * HEAD_DIM

    def body(w_ref, u_ref, qg_ref, kg_ref, at_ref, gl_ref, o_ref, st_ref, s_ref):
        @pl.when(pl.program_id(0) == 0)
        def _():
            s_ref[...] = jnp.zeros_like(s_ref)

        heads = range(HEADS)
        cols = [pl.ds(h * HEAD_DIM, HEAD_DIM) for h in heads]
        for c in range(nch):
            rows, gl_rows = pl.ds(c * CHUNK, CHUNK), pl.ds(c * 8, 8)
            ss = [s_ref[h] for h in heads]
            sbs = [s.astype(BF16) for s in ss]
            vbs = [(u_ref[rows, hs] - _dot(w_ref[rows, hs], sb)).astype(BF16) for hs, sb in zip(cols, sbs)]
            outs = [_dot(qg_ref[rows, hs], sb) + _dot(at_ref[h, rows, :], vb) for h, hs, sb, vb in zip(heads, cols, sbs, vbs)]
            new = [s * jnp.tile(gl_ref[h, gl_rows, :], (HEAD_DIM // 8, 1)) + _dot(kg_ref[rows, hs], vb, TN) for h, hs, s, vb in zip(heads, cols, ss, vbs)]
            for h, hs in zip(heads, cols):
                st_ref[c, h] = ss[h]
                o_ref[rows, hs] = outs[h]
                s_ref[h] = new[h]

    row = pl.BlockSpec((nch * CHUNK, wide), lambda i: (i, 0))
    return pl.pallas_call(
        body,
        name="gdn_scan",
        grid=(n // nch,),
        in_specs=[row, row, row, row, pl.BlockSpec((HEADS, nch * CHUNK, CHUNK), lambda i: (0, i, 0)), pl.BlockSpec((HEADS, nch * 8, HEAD_DIM), lambda i: (0, i, 0))],
        out_specs=[row, pl.BlockSpec((nch, HEADS, HEAD_DIM, HEAD_DIM), lambda i: (i, 0, 0, 0))],
        out_shape=[jax.ShapeDtypeStruct((t, wide), F32), jax.ShapeDtypeStruct((n, HEADS, HEAD_DIM, HEAD_DIM), F32)],
        scratch_shapes=[pltpu.VMEM((HEADS, HEAD_DIM, HEAD_DIM), F32)],
        compiler_params=_params("arbitrary"),
    )(w, u, qg, kg, attn, gl)


def _gdn_scan_bwd(w, u, qg, kg, attn, gl, states, do):
    t = w.shape[0]
    n = t // CHUNK
    nch = min(SCAN_CHUNKS, n)
    steps = n // nch
    wide = HEADS * HEAD_DIM

    def body(w_ref, u_ref, qg_ref, kg_ref, at_ref, gl_ref, st_ref, do_ref, dw_ref, du_ref, dqg_ref, dkg_ref, dat_ref, dgl_ref, ds_ref):
        @pl.when(pl.program_id(0) == 0)
        def _():
            ds_ref[...] = jnp.zeros_like(ds_ref)

        heads = range(HEADS)
        cols = [pl.ds(h * HEAD_DIM, HEAD_DIM) for h in heads]
        for c in reversed(range(nch)):
            rows, gl_rows = pl.ds(c * CHUNK, CHUNK), pl.ds(c * 8, 8)
            ss = [st_ref[c, h] for h in heads]
            sbs = [s.astype(BF16) for s in ss]
            dsns = [ds_ref[h] for h in heads]
            dsbs = [d.astype(BF16) for d in dsns]
            dobs = [do_ref[rows, hs].astype(BF16) for hs in cols]
            vbs = [(u_ref[rows, hs] - _dot(w_ref[rows, hs], sb)).astype(BF16) for hs, sb in zip(cols, sbs)]
            dvns = [_dot(at_ref[h, rows, :], dob, TN) + _dot(kg_ref[rows, hs], dsb) for h, hs, dob, dsb in zip(heads, cols, dobs, dsbs)]
            dvbs = [d.astype(BF16) for d in dvns]
            for h, hs in zip(heads, cols):
                dat_ref[h, rows, :] = _dot(dobs[h], vbs[h], NT)
                dqg_ref[rows, hs] = _dot(dobs[h], sbs[h], NT)
                dkg_ref[rows, hs] = _dot(vbs[h], dsbs[h], NT)
                du_ref[rows, hs] = dvns[h]
                dw_ref[rows, hs] = -_dot(dvbs[h], sbs[h], NT)
                dgl_ref[h, gl_rows, :] = jnp.sum((dsns[h] * ss[h]).reshape(HEAD_DIM // 8, 8, HEAD_DIM), axis=0)
            new = [dsn * jnp.tile(gl_ref[h, gl_rows, :], (HEAD_DIM // 8, 1)) + _dot(qg_ref[rows, hs], dob, TN) - _dot(w_ref[rows, hs], dvb, TN)
                   for h, hs, dsn, dob, dvb in zip(heads, cols, dsns, dobs, dvbs)]
            for h in heads:
                ds_ref[h] = new[h]

    row = pl.BlockSpec((nch * CHUNK, wide), lambda i: (steps - 1 - i, 0))
    at = pl.BlockSpec((HEADS, nch * CHUNK, CHUNK), lambda i: (0, steps - 1 - i, 0))
    glb = pl.BlockSpec((HEADS, nch * 8, HEAD_DIM), lambda i: (0, steps - 1 - i, 0))
    return pl.pallas_call(
        body,
        name="gdn_scan_bwd",
        grid=(steps,),
        in_specs=[row, row, row, row, at, glb, pl.BlockSpec((nch, HEADS, HEAD_DIM, HEAD_DIM), lambda i: (steps - 1 - i, 0, 0, 0)), row],
        out_specs=[row, row, row, row, at, glb],
        out_shape=[jax.ShapeDtypeStruct((t, wide), F32)] * 4 + [jax.ShapeDtypeStruct((HEADS, t, CHUNK), F32), jax.ShapeDtypeStruct((HEADS, n * 8, HEAD_DIM), F32)],
        scratch_shapes=[pltpu.VMEM((HEADS, HEAD_DIM, HEAD_DIM), F32)],
        compiler_params=_params("arbitrary"),
    )(w, u, qg, kg, attn, gl, states, do)


SB_Q = 512
SB_K = 256
SB_STEP = 1
SB_DEAD = -105.0


def _sb_scores(q, k):
    z = _dot(q, k, NT) * (HEAD_DIM ** -0.5)
    lb = jnp.minimum(z, 0.0) - jnp.log(1.0 + jnp.exp(-jnp.abs(z)))
    return lb, lb - z


def _tri(n, rel):
    return jnp.where(rel(_iota((n, n), 0), _iota((n, n), 1)), 1.0, 0.0).astype(BF16)


def _lanes(col):
    return jnp.broadcast_to(col, (col.shape[0], LANES))


def _sb_fwd(q, k, v):
    t = q.shape[0]
    bq, bk = min(SB_Q, t), min(SB_K, t)
    nsub, rep = bq // bk, bk // LANES
    nstep = min(SB_STEP, nsub)
    steps_per_tile = nsub // nstep

    def body(q_ref, k_ref, v_ref, o_ref, rt_ref, first_ref):
        h = pl.program_id(0)
        i = pl.program_id(1)
        o_ref[...] = jnp.zeros_like(o_ref)
        rt_ref[...] = jnp.zeros_like(rt_ref)
        after = _tri(bk, lambda r, c: r > c)

        def block(j, r0, diag):
            st = pl.multiple_of(j * bk, bk)
            kv, vv = k_ref[pl.ds(st, bk), :], v_ref[pl.ds(st, bk), :]
            lb, l1m = _sb_scores(q_ref[r0:, :], kv)
            if diag:
                mask = _iota((bq - r0, bk), 1) + j * bk < _iota((bq - r0, bk), 0) + (r0 + i * bq)
                l1m = jnp.where(mask, l1m, 0.0)
            sums = _two_pass(l1m, after)
            run = rt_ref[r0:, :]
            a = jnp.exp(lb + jnp.tile(run, (1, rep)) + sums)
            if diag:
                a = jnp.where(mask, a, 0.0)
            o_ref[r0:, :] += _dot(a.astype(BF16), vv)
            rt_ref[r0:, :] = run + _lanes(sums[:, 0:1] + l1m[:, 0:1])

        for s in reversed(range(nsub)):
            block(i * nsub + s, s * bk, True)

        def alive(carry):
            u, highest = carry
            return jnp.logical_and(u >= 0, highest > SB_DEAD)

        def step(carry):
            u, _ = carry
            for s in reversed(range(nstep)):
                block(u * nstep + s, 0, False)
            return u - 1, jnp.max(rt_ref[...])

        u_end, _ = lax.while_loop(alive, step, (i * steps_per_tile - 1, jnp.max(rt_ref[...])))
        first_ref[h, i] = u_end + 1

    qb = pl.BlockSpec((bq, HEAD_DIM), lambda h, i: (i, h))
    full = pl.BlockSpec((t, HEAD_DIM), lambda h, i: (0, h))
    return pl.pallas_call(
        body,
        name="sb_fwd",
        grid=(HEADS, t // bq),
        in_specs=[qb, full, full],
        out_specs=[qb, qb, pl.BlockSpec(memory_space=pltpu.SMEM)],
        out_shape=[jax.ShapeDtypeStruct(q.shape, F32), jax.ShapeDtypeStruct(q.shape, F32), jax.ShapeDtypeStruct((HEADS, t // bq), jnp.int32)],
        compiler_params=_params("arbitrary", "arbitrary"),
    )(q, k, v)


def _sb_bwd(q, k, v, rt, first, do):
    t = q.shape[0]
    bq, bk = min(SB_Q, t), min(SB_K, t)
    nsub, rep = bq // bk, bk // LANES
    nstep = min(SB_STEP, nsub)
    steps_per_tile = nsub // nstep
    scale = HEAD_DIM ** -0.5

    def body(first_ref, q_ref, k_ref, v_ref, rt_ref, do_ref, dq_ref, dk_ref, dv_ref, left_ref, pg_ref):
        h = pl.program_id(0)
        i = pl.program_id(1)

        @pl.when(i == 0)
        def _():
            dk_ref[...] = jnp.zeros_like(dk_ref)
            dv_ref[...] = jnp.zeros_like(dv_ref)

        dq_ref[...] = jnp.zeros_like(dq_ref)
        left_ref[...] = jnp.zeros_like(left_ref)
        pg_ref[...] = jnp.zeros_like(pg_ref)
        upto = _tri(bk, lambda r, c: r <= c)

        def block(j, r0, diag):
            st = pl.multiple_of(j * bk, bk)
            kv, vv = k_ref[pl.ds(st, bk), :], v_ref[pl.ds(st, bk), :]
            qv = q_ref[r0:, :]
            dob = do_ref[r0:, :].astype(BF16)
            lb, l1m = _sb_scores(qv, kv)
            if diag:
                mask = _iota((bq - r0, bk), 1) + j * bk < _iota((bq - r0, bk), 0) + (r0 + i * bq)
                l1m = jnp.where(mask, l1m, 0.0)
            sums = _two_pass(l1m, upto)
            left = left_ref[r0:, :]
            a = jnp.exp(lb + jnp.tile(rt_ref[r0:, :] - left, (1, rep)) - sums)
            if diag:
                a = jnp.where(mask, a, 0.0)
            g = _dot(dob, vv, NT) * a
            dv_ref[pl.ds(st, bk), :] += _dot(a.astype(BF16), dob, TN)
            gsum = _two_pass(g, upto)
            pg = pg_ref[r0:, :]
            dz = g - jnp.exp(lb) * (jnp.tile(pg, (1, rep)) + gsum)
            if diag:
                dz = jnp.where(mask, dz, 0.0)
            dzb = (dz * scale).astype(BF16)
            dk_ref[pl.ds(st, bk), :] += _dot(dzb, qv, TN)
            dq_ref[r0:, :] += _dot(dzb, kv)
            left_ref[r0:, :] = left + _lanes(sums[:, bk - 1:bk])
            pg_ref[r0:, :] = pg + _lanes(gsum[:, bk - 1:bk])

        def step(u, carry):
            for s in range(nstep):
                block(u * nstep + s, 0, False)
            return carry

        lax.fori_loop(first_ref[h, i], i * steps_per_tile, step, 0)
        for s in range(nsub):
            block(i * nsub + s, s * bk, True)

    qb = pl.BlockSpec((bq, HEAD_DIM), lambda h, i: (i, h))
    full = pl.BlockSpec((t, HEAD_DIM), lambda h, i: (0, h))
    return pl.pallas_call(
        body,
        name="sb_bwd",
        grid=(HEADS, t // bq),
        in_specs=[pl.BlockSpec(memory_space=pltpu.SMEM), qb, full, full, qb, qb],
        out_specs=[qb, full, full],
        out_shape=[jax.ShapeDtypeStruct(q.shape, F32)] * 3,
        scratch_shapes=[pltpu.VMEM((bq, LANES), F32), pltpu.VMEM((bq, LANES), F32)],
        compiler_params=_params("arbitrary", "arbitrary"),
    )(first, q, k, v, rt, do)


def _adamw(w, g, m, v, name, tm=ROW_TILE):
    r, c = w.shape
    tm = tm if r % tm == 0 else r

    def body(w_ref, g_ref, m_ref, v_ref, d_ref, nm_ref, nv_ref):
        gv = g_ref[...]
        nm = ADAM_B1 * m_ref[...] + (1.0 - ADAM_B1) * gv
        nv = ADAM_B2 * v_ref[...] + (1.0 - ADAM_B2) * (gv * gv)
        m_hat = nm / (1.0 - ADAM_B1 ** ADAM_STEP)
        v_hat = nv / (1.0 - ADAM_B2 ** ADAM_STEP)
        d_ref[...] = -ADAM_LR * (m_hat / (jnp.sqrt(v_hat) + ADAM_EPS) + ADAM_WD * w_ref[...])
        nm_ref[...] = nm
        nv_ref[...] = nv

    blk = pl.BlockSpec((tm, c), lambda i: (i, 0))
    return pl.pallas_call(
        body,
        name=name,
        grid=(r // tm,),
        in_specs=[blk] * 4,
        out_specs=[blk] * 3,
        out_shape=[jax.ShapeDtypeStruct((r, c), F32)] * 3,
        compiler_params=_params("parallel"),
    )(w, g, m, v)


def _local_step(x, tgt, gains, small, shards, assemble_first, assemble, early_reduce=None, late_reduce=None):
    mix_pre, mix_post, mlp_pre, mlp_post, kv_gain = gains
    a_log, dt_bias, out_gain = small
    t, d = x.shape
    row = lambda a, i=None: a[i:i + 1] if i is not None else a
    al = jnp.zeros((1, LANES), F32).at[:, HEADS:2 * HEADS].set(a_log)
    dtb = jnp.zeros((1, LANES), F32).at[:, HEADS:2 * HEADS].set(dt_bias)
    og = jnp.tile(out_gain, (1, HEADS))
    full = lambda a: (a, a.shape[1], 0)

    h0, *gathered_first = _rowwise("norm_in", _fn_norm, [full(x)], [row(mix_pre, 0)], [(d, BF16)], gather=shards[0])
    w_qkvg, w_ba, conv_w = assemble_first(gathered_first)
    qkvg = _matmul(h0, w_qkvg, "nn", F32, "mm_gdn_in", tk=1024)
    ba = _matmul(h0, w_ba, "nn", F32, "mm_gdn_ba", tk=1024)
    (conv, gq, gk, gv), gathered_conv = _conv_fwd(qkvg, conv_w, shards[1])
    beta, gc, gt = _rowwise("gates", _fn_gates, [full(ba)], [al, dtb], [(LANES, F32)] * 3)
    gcr = jnp.swapaxes(gc[:, HEADS:2 * HEADS], 0, 1).reshape(HEADS, t // CHUNK, 1, CHUNK)
    (pw, pu, pqg, pkg, pattn, pgl, ptm), gathered_prep = _gdn_prep(gq, gk, gv, beta, gc, gt, gcr, shards[2])
    w_out, w_kv, w_q, w_o, w_up, w_down = assemble(gathered_conv, gathered_prep)
    w_qkvg_t, w_up_t, w_down_t = (jnp.swapaxes(a, -1, -2) for a in (w_qkvg, w_up, w_down))
    o_gdn, states = _gdn_scan(pw, pu, pqg, pkg, pattn, pgl)
    (on,) = _rowwise("out_norm", _fn_outnorm, [full(o_gdn), (qkvg, d, 3)], [og], [(d, BF16)])
    mix0 = _matmul(on, w_out, "nn", F32, "mm_gdn_out", tk=1024)
    x1, h1 = _rowwise("res_a0", _fn_res_norm, [full(x), full(mix0)], [row(mix_post, 0), row(mlp_pre, 0)], [(d, F32), (d, BF16)])
    (a0,) = _matmul(h1, w_up[0], "nn", (BF16,), "mm_up0", tk=1024, epilogue=_relu2_of)
    d0 = _matmul(a0, w_down[0], "nn", F32, "mm_down0")
    x2, hkv, hq = _rowwise("res_b0", _fn_res_norm2, [full(x1), full(d0)], [row(mlp_post, 0), kv_gain, row(mix_pre, 1)], [(d, F32), (d, BF16), (d, BF16)])
    w_k, w_v = w_kv[:, :d], w_kv[:, d:]
    kp = _matmul(hkv, w_k, "nn", BF16, "mm_k", tk=1024)
    vp = _matmul(hkv, w_v, "nn", BF16, "mm_v", tk=1024)
    qp = _matmul(hq, w_q, "nn", BF16, "mm_q", tk=1024)
    o_sb, rt, sb_first = _sb_fwd(qp, kp, vp)
    mix1 = _matmul(o_sb, w_o, "nn", F32, "mm_sb_out", tk=1024)
    x3, h3 = _rowwise("res_a1", _fn_res_norm, [full(x2), full(mix1)], [row(mix_post, 1), row(mlp_pre, 1)], [(d, F32), (d, BF16)])
    (a1,) = _matmul(h3, w_up[1], "nn", (BF16,), "mm_up1", tk=1024, epilogue=_relu2_of)
    d1 = _matmul(a1, w_down[1], "nn", F32, "mm_down1")

    loss, dx3, dd1, g_mlp_post1 = _loss_call(x3, d1, tgt, row(mlp_post, 1))
    (du1,) = _matmul(dd1, w_down_t[1], "nn", (BF16,), "mm_down1_dx", epilogue=_relu2_cotangent, extras=[a1])
    g_down1 = _matmul(a1, dd1, "tn", F32, "mm_down1_dw")
    dh3 = _matmul(du1, w_up_t[1], "nn", F32, "mm_up1_dx")
    g_up1 = _matmul(h3, du1, "tn", F32, "mm_up1_dw")
    (dx2, dmix1), (g_mix_post1, g_mlp_pre1), _ = _rowwise_bwd(
        "res_a1_bwd", _fn_res_norm, [full(x2), full(mix1)], [row(mix_post, 1), row(mlp_pre, 1)], [dx3, dh3], [F32, BF16])
    do_sb = _matmul(dmix1, w_o, "nt", BF16, "mm_sb_out_dx")
    g_o = _matmul(o_sb, dmix1, "tn", F32, "mm_sb_out_dw")
    dqp, dkp, dvp = _sb_bwd(qp, kp, vp, rt, sb_first, do_sb)
    dhq = _matmul(dqp, w_q, "nt", F32, "mm_q_dx")
    g_q = _matmul(hq, dqp, "tn", F32, "mm_q_dw")
    dhkv = _matmul(dvp, w_v, "nt", F32, "mm_v_dx", add=_matmul(dkp, w_k, "nt", F32, "mm_k_dx"))
    g_kv = jnp.concatenate([_matmul(hkv, dkp, "tn", F32, "mm_k_dw"), _matmul(hkv, dvp, "tn", F32, "mm_v_dw")], axis=1)
    (dx1, dd0), (g_mlp_post0, g_kv_gain, g_mix_pre1), _ = _rowwise_bwd(
        "res_b0_bwd", _fn_res_norm2, [full(x1), full(d0)], [row(mlp_post, 0), kv_gain, row(mix_pre, 1)], [dx2, dhkv, dhq], [F32, BF16])
    (du0,) = _matmul(dd0, w_down_t[0], "nn", (BF16,), "mm_down0_dx", epilogue=_relu2_cotangent, extras=[a0])
    g_down0 = _matmul(a0, dd0, "tn", F32, "mm_down0_dw")
    dh1 = _matmul(du0, w_up_t[0], "nn", F32, "mm_up0_dx")
    g_up0 = _matmul(h1, du0, "tn", F32, "mm_up0_dw")
    (dx0, dmix0), (g_mix_post0, g_mlp_pre0), _ = _rowwise_bwd(
        "res_a0_bwd", _fn_res_norm, [full(x), full(mix0)], [row(mix_post, 0), row(mlp_pre, 0)], [dx1, dh1], [F32, BF16])
    don = _matmul(dmix0, w_out, "nt", F32, "mm_gdn_out_dx")
    g_out = _matmul(on, dmix0, "tn", F32, "mm_gdn_out_dw")
    (do_gdn, dgate), (g_og,), _ = _rowwise_bwd("out_norm_bwd", _fn_outnorm, [full(o_gdn), (qkvg, d, 3)], [og], [don], [F32, F32])
    dpw, dpu, dpqg, dpkg, dpattn, dpgl = _gdn_scan_bwd(pw, pu, pqg, pkg, pattn, pgl, states, do_gdn)
    partial, partial_bf16 = [], ()
    if early_reduce is not None:
        partial, partial_bf16 = early_reduce(dict(mlp_w_up=(g_up0, g_up1), mlp_w_down=(g_down0, g_down1), gdn_w_out=g_out[None], w_kv=g_kv, sb_w_q=g_q[None], sb_w_o=g_o[None]))
    (dgq, dgk, dgv, dbeta, dgc, dgt, dgcr), from_chips = _gdn_prep_bwd(gq, gk, gv, beta, gc, gt, gcr, ptm, dpw, dpu, dpqg, dpkg, dpattn, dpgl, partial_bf16)
    dgcr_lanes = jnp.pad(jnp.swapaxes(dgcr.reshape(HEADS, t), 0, 1), ((0, 0), (HEADS, LANES - 2 * HEADS)))
    gate_cots = [dbeta, dgc + dgcr_lanes, dgt]
    (dba,), (g_al, g_dtb), _ = _rowwise_bwd("gates_bwd", _fn_gates, [full(ba)], [al, dtb], gate_cots, [BF16])
    dqkvg, g_conv = _conv_bwd(conv, (dgq, dgk, dgv), dgate, qkvg, conv_w)
    dh0b = _matmul(dba, w_ba, "nt", F32, "mm_gdn_ba_dx", tk=LANES)
    dh0 = _matmul(dqkvg, w_qkvg_t, "nn", F32, "mm_gdn_in_dx", add=dh0b)
    g_qkvg = _matmul(h0, dqkvg, "tn", F32, "mm_gdn_in_dw")
    g_ba = _matmul(h0, dba, "tn", F32, "mm_gdn_ba_dw")
    g_w_in = jnp.concatenate([g_qkvg, g_ba[:, :2 * HEADS]], axis=1)[None]
    partial_late, partial_late_bf16 = late_reduce(dict(gdn_w_in=g_w_in)) if late_reduce is not None else ([], ())
    (grad_x,), (g_mix_pre0,), from_chips_late = _rowwise_bwd(
        "norm_in_bwd", lambda xx, gg: (_rms(xx, gg), xx), [full(x)], [row(mix_pre, 0)], [dh0, dx0], [F32], partials=partial_late_bf16)

    grads = dict(
        mix_pre_gain=jnp.concatenate([g_mix_pre0, g_mix_pre1], axis=0),
        mix_post_gain=jnp.concatenate([g_mix_post0, g_mix_post1], axis=0),
        mlp_pre_gain=jnp.concatenate([g_mlp_pre0, g_mlp_pre1], axis=0),
        mlp_post_gain=jnp.concatenate([g_mlp_post0, g_mlp_post1], axis=0),
        mlp_w_up=(g_up0, g_up1),
        mlp_w_down=(g_down0, g_down1),
        gdn_w_in=g_w_in,
        gdn_conv_w=g_conv[None, :CONV_K],
        gdn_a_log=g_al[:, HEADS:2 * HEADS],
        gdn_dt_bias=g_dtb[:, HEADS:2 * HEADS],
        gdn_out_gain=jnp.sum(g_og.reshape(HEADS, HEAD_DIM), axis=0, keepdims=True),
        gdn_w_out=g_out[None],
        kv_gain=g_kv_gain[0],
        w_kv=g_kv,
        sb_w_q=g_q[None],
        sb_w_o=g_o[None],
    )
    return loss, grad_x, grads, (list(partial) + list(partial_late), list(from_chips) + list(from_chips_late))


N_DEV = 8
N_CHIPS = 4
PACK_ROW_TILE = 128

_HBM = pl.BlockSpec(memory_space=pltpu.HBM)


def _place():
    return lax.axis_index("x"), lax.axis_index("y"), lax.axis_index("c")


def _other_chips(x, y):
    return [(1 - x, y), (x, 1 - y), (1 - x, 1 - y)]


def _remote(src, dst, send_sem, recv_sem, to):
    return pltpu.make_async_remote_copy(src_ref=src, dst_ref=dst, send_sem=send_sem, recv_sem=recv_sem, device_id=to, device_id_type=MESH)


def _gather8(v, name):
    rows, cols = v.shape

    def body(v_ref, out_ref, sum_ref, send_sems, recv_sems, local_sem):
        x, y, c = _place()
        me, sibling = (x, y, c), (x, y, 1 - c)
        chips = _other_chips(x, y)

        def blk(px, py, pc):
            return out_ref.at[pl.ds((4 * px + 2 * py + pc) * rows, rows), :]

        def copy(k, block, to, src=None):
            return _remote(blk(*block) if src is None else src, blk(*block), send_sems.at[k], recv_sems.at[k], to)

        mine = pltpu.make_async_copy(v_ref, blk(*me), local_sem)
        mine.start()
        first = [copy(0, me, sibling, src=v_ref)] + [copy(1 + j, me, (*chip, c), src=v_ref) for j, chip in enumerate(chips)]
        for cp in first:
            cp.start()
        passed = [copy(4 + j, (*chip, c), sibling) for j, chip in enumerate(chips)]
        for j, chip in enumerate(chips):
            copy(1 + j, (*chip, c), me).wait_recv()
            passed[j].start()
        copy(0, sibling, me).wait_recv()
        for j, chip in enumerate(chips):
            copy(4 + j, (*chip, 1 - c), me).wait_recv()
        for cp in first + passed:
            cp.wait_send()
        mine.wait()
        acc = out_ref[pl.ds(0, rows), :]
        for dev in range(1, N_DEV):
            acc = acc + out_ref[pl.ds(dev * rows, rows), :]
        sum_ref[...] = acc

    vm = pl.BlockSpec(memory_space=pltpu.VMEM)
    return pl.pallas_call(
        body,
        name=name,
        out_shape=[jax.ShapeDtypeStruct((N_DEV * rows, cols), v.dtype), jax.ShapeDtypeStruct((rows, cols), v.dtype)],
        in_specs=[vm],
        out_specs=[vm, vm],
        scratch_shapes=[pltpu.SemaphoreType.DMA((7,)), pltpu.SemaphoreType.DMA((7,)), pltpu.SemaphoreType.DMA],
    )(v)


def _hbm_call(body, name, arrs, out_shapes, sem_counts):
    n = len(arrs)

    def wrapped(*refs):
        body(refs[:n], refs[n:2 * n], *refs[2 * n:])

    return pl.pallas_call(
        wrapped,
        name=name,
        out_shape=[jax.ShapeDtypeStruct(s, a.dtype) for s, a in zip(out_shapes, arrs)],
        in_specs=[_HBM] * n,
        out_specs=[_HBM] * n,
        scratch_shapes=[pltpu.SemaphoreType.DMA((k,)) for k in sem_counts],
    )(*arrs)


def _gather_sends(w_refs, out_refs, send_sems, recv_sems):
    x, y, c = _place()
    s_me = 2 * x + y
    return [_remote(w.at[c], o.at[s_me, c], send_sems.at[3 * a + j], recv_sems.at[3 * a + j], (px, py, c))
            for a, (w, o) in enumerate(zip(w_refs, out_refs)) for j, (px, py) in enumerate(_other_chips(x, y))]


def _gather_finish(w_refs, out_refs, send_sems, recv_sems, fsend_sems, frecv_sems):
    x, y, c = _place()
    chips = _other_chips(x, y)
    passed = []
    for a, o in enumerate(out_refs):
        for j, (px, py) in enumerate(chips):
            half = o.at[2 * px + py, c]
            _remote(half, half, send_sems.at[3 * a + j], recv_sems.at[3 * a + j], (px, py, c)).wait_recv()
            fwd = _remote(half, half, fsend_sems.at[3 * a + j], frecv_sems.at[3 * a + j], (x, y, 1 - c))
            fwd.start()
            passed.append(fwd)
    for a, o in enumerate(out_refs):
        for j, (px, py) in enumerate(chips):
            half = o.at[2 * px + py, 1 - c]
            _remote(half, half, fsend_sems.at[3 * a + j], frecv_sems.at[3 * a + j], (x, y, 1 - c)).wait_recv()
    for cp in _gather_sends(w_refs, out_refs, send_sems, recv_sems) + passed:
        cp.wait_send()


def _swap_halves(arrs, name):
    n = len(arrs)

    def body(g_refs, a_refs, send_sems, recv_sems):
        x, y, c = _place()
        cps = [_remote(g.at[1 - c], a, send_sems.at[i], recv_sems.at[i], (x, y, 1 - c)) for i, (g, a) in enumerate(zip(g_refs, a_refs))]
        for cp in cps:
            cp.start()
        for cp in cps:
            cp.wait()

    return _hbm_call(body, name, arrs, [a.shape[1:] for a in arrs], [n, n])


def _scatter_copies(p_refs, b_refs, send_sems, recv_sems):
    x, y, c = _place()
    return [_remote(p.at[2 * px + py], b.at[j], send_sems.at[3 * i + j], recv_sems.at[3 * i + j], (px, py, c))
            for i, (p, b) in enumerate(zip(p_refs, b_refs)) for j, (px, py) in enumerate(_other_chips(x, y))]


def _share_halves(arrs):
    n = len(arrs)

    def body(q_refs, out_refs, send_sems, recv_sems):
        x, y, c = _place()
        cps = [_remote(q, o, send_sems.at[i], recv_sems.at[i], (x, y, 1 - c)) for i, (q, o) in enumerate(zip(q_refs, out_refs))]
        for cp in cps:
            cp.start()
        for cp in cps:
            cp.wait()

    return _hbm_call(body, "grads_share", arrs, [a.shape for a in arrs], [n, n])


_GROUPS = (
    (("gdn_w_out", (1, 256, 1024), "rows"), ("mlp_w_up", (2, 1024, 1024), "cols")),
    (("mlp_w_down", (2, 1024, 1024), "rows"), ("sb_w_q", (1, 256, 1024), "rows"), ("sb_w_o", (1, 256, 1024), "rows")),
    (("w_kv", (1024, 512), "cols"),),
    (("gdn_w_in", (1, 1024, 1028), "cols"),),
)
_BEHIND_CONV, _BEHIND_PREP, _FIRST = slice(0, 1), slice(1, 3), slice(3, 4)
_EARLY_GRADS = slice(0, 3)


def _numel(shape):
    n = 1
    for s in shape:
        n *= s
    return n


def _half_rows(shape):
    return _numel(shape[:-1]) // 2


def _pack_shards(shards, dtype):
    return tuple(jnp.concatenate([shards[n].astype(dtype).reshape(2, _half_rows(shape), shape[-1]) for n, shape, _ in grp], axis=1) for grp in _GROUPS)


def _unpack_shards(bufs):
    out = {}
    for grp, buf in zip(_GROUPS, bufs):
        off = 0
        for n, shape, _ in grp:
            out[n] = buf[:, off:off + _half_rows(shape)].reshape(shape)
            off += _half_rows(shape)
    return out


def _join(stacked, how):
    nd = stacked.ndim - 1
    ax = nd - 1 if how == "cols" else nd - 2
    moved = jnp.moveaxis(stacked, 0, ax)
    shape = list(stacked.shape[1:])
    shape[ax] *= N_CHIPS
    return moved.reshape(shape)


def _split(full, shard_shape, how):
    nd = len(shard_shape)
    ax = nd - 1 if how == "cols" else nd - 2
    shape = list(shard_shape)
    shape.insert(ax, N_CHIPS)
    return jnp.moveaxis(full.reshape(shape), ax, 0)


def _unpack_full(gathered, groups):
    out = {}
    for grp, buf in zip(groups, gathered):
        off = 0
        for n, shape, how in grp:
            out[n] = _join(buf[:, :, off:off + _half_rows(shape)].reshape((N_CHIPS,) + shape), how)
            off += _half_rows(shape)
    return out


def _pack_full(full, groups):
    bufs = []
    for grp in groups:
        parts = []
        for n, shape, how in grp:
            if isinstance(full[n], tuple):
                assert len(full[n]) == shape[0] == 2
                parts.append(jnp.stack([_split(layer, shape[1:], how) for layer in full[n]], axis=1))
            else:
                parts.append(_split(full[n], shape, how).reshape(N_CHIPS, 2, _half_rows(shape), shape[-1]))
        buf = jnp.swapaxes(jnp.concatenate(parts, axis=2), 0, 1)
        bufs.append(buf.reshape(2, -1, buf.shape[-1]))
    return tuple(bufs)


_SMALL = (
    ("mix_pre_gain", (2, 1024)),
    ("mix_post_gain", (2, 1024)),
    ("mlp_pre_gain", (2, 1024)),
    ("mlp_post_gain", (2, 1024)),
    ("kv_gain", (1024,)),
    ("gdn_out_gain", (1, 128)),
    ("gdn_a_log", (1, 8)),
    ("gdn_dt_bias", (1, 8)),
    ("gdn_conv_w", (1, 4, 3072)),
    ("loss", ()),
)


def _rows_of(shape):
    return -(-_numel(shape) // LANES)


def _pack_rows(vals, layout):
    parts = []
    for n, shape in layout:
        flat = vals[n].reshape(-1)
        parts.append(jnp.pad(flat, (0, _rows_of(shape) * LANES - flat.shape[0])))
    flat = jnp.concatenate(parts)
    rows = -(-flat.shape[0] // (8 * LANES)) * 8
    return jnp.pad(flat, (0, rows * LANES - flat.shape[0])).reshape(rows, LANES)


def _unpack_rows(packed, layout):
    flat = packed.reshape(-1)
    out, off = {}, 0
    for n, shape in layout:
        out[n] = flat[off:off + _numel(shape)].reshape(shape)
        off += _rows_of(shape) * LANES
    return out


_WEIGHTS = ("mix_pre_gain", "mix_post_gain", "mlp_pre_gain", "mlp_post_gain", "mlp_w_up", "mlp_w_down", "gdn_w_in", "gdn_conv_w",
            "gdn_a_log", "gdn_dt_bias", "gdn_out_gain", "gdn_w_out", "kv_gain", "w_kv", "sb_w_q", "sb_w_o")


def _as2d(a):
    return a.reshape(1, -1) if a.ndim <= 1 else a.reshape(-1, a.shape[-1])


def kernel(x, mix_pre_gain, mix_post_gain, mlp_pre_gain, mlp_post_gain, mlp_w_up, mlp_w_down, gdn_w_in, gdn_conv_w, gdn_a_log, gdn_dt_bias, gdn_out_gain, gdn_w_out, kv_gain, w_kv, sb_w_q, sb_w_o, loss_target, m_mix_pre_gain, m_mix_post_gain, m_mlp_pre_gain, m_mlp_post_gain, m_mlp_w_up, m_mlp_w_down, m_gdn_w_in, m_gdn_conv_w, m_gdn_a_log, m_gdn_dt_bias, m_gdn_out_gain, m_gdn_w_out, m_kv_gain, m_w_kv, m_sb_w_q, m_sb_w_o, v_mix_pre_gain, v_mix_post_gain, v_mlp_pre_gain, v_mlp_post_gain, v_mlp_w_up, v_mlp_w_down, v_gdn_w_in, v_gdn_conv_w, v_gdn_a_log, v_gdn_dt_bias, v_gdn_out_gain, v_gdn_w_out, v_kv_gain, v_w_kv, v_sb_w_q, v_sb_w_o):
    w = dict(mix_pre_gain=mix_pre_gain, mix_post_gain=mix_post_gain, mlp_pre_gain=mlp_pre_gain, mlp_post_gain=mlp_post_gain, mlp_w_up=mlp_w_up, mlp_w_down=mlp_w_down, gdn_w_in=gdn_w_in, gdn_conv_w=gdn_conv_w, gdn_a_log=gdn_a_log, gdn_dt_bias=gdn_dt_bias, gdn_out_gain=gdn_out_gain, gdn_w_out=gdn_w_out, kv_gain=kv_gain, w_kv=w_kv, sb_w_q=sb_w_q, sb_w_o=sb_w_o)
    m = dict(mix_pre_gain=m_mix_pre_gain, mix_post_gain=m_mix_post_gain, mlp_pre_gain=m_mlp_pre_gain, mlp_post_gain=m_mlp_post_gain, mlp_w_up=m_mlp_w_up, mlp_w_down=m_mlp_w_down, gdn_w_in=m_gdn_w_in, gdn_conv_w=m_gdn_conv_w, gdn_a_log=m_gdn_a_log, gdn_dt_bias=m_gdn_dt_bias, gdn_out_gain=m_gdn_out_gain, gdn_w_out=m_gdn_w_out, kv_gain=m_kv_gain, w_kv=m_w_kv, sb_w_q=m_sb_w_q, sb_w_o=m_sb_w_o)
    v = dict(mix_pre_gain=v_mix_pre_gain, mix_post_gain=v_mix_post_gain, mlp_pre_gain=v_mlp_pre_gain, mlp_post_gain=v_mlp_post_gain, mlp_w_up=v_mlp_w_up, mlp_w_down=v_mlp_w_down, gdn_w_in=v_gdn_w_in, gdn_conv_w=v_gdn_conv_w, gdn_a_log=v_gdn_a_log, gdn_dt_bias=v_gdn_dt_bias, gdn_out_gain=v_gdn_out_gain, gdn_w_out=v_gdn_w_out, kv_gain=v_kv_gain, w_kv=v_w_kv, sb_w_q=v_sb_w_q, sb_w_o=v_sb_w_o)
    cx, cy, cc = _place()
    chip = 2 * cx + cy
    conv_cols = gdn_conv_w.shape[-1]

    own = _pack_shards(w, BF16)
    own_taps = jnp.pad(gdn_conv_w[0], ((0, CONV_K), (0, 0))).reshape(2, CONV_K, conv_cols)
    with_own = lambda gathered, mine: [lax.dynamic_update_index_in_dim(g, m, chip, 0) for g, m in zip(gathered, mine)]

    def assemble_first(gathered):
        w_in_all, taps_all = with_own(gathered, (*own[_FIRST], own_taps))
        w_in = _unpack_full([w_in_all], _GROUPS[_FIRST])["gdn_w_in"][0]
        taps = jnp.swapaxes(taps_all[:, 0], 0, 1).reshape(CONV_K, N_CHIPS * conv_cols)
        return w_in[:, :4 * HEADS * HEAD_DIM], jnp.pad(w_in[:, 4 * HEADS * HEAD_DIM:], ((0, 0), (0, LANES - 2 * HEADS))), taps

    def assemble(gathered_conv, gathered_prep):
        full = {**_unpack_full(with_own(gathered_conv, own[_BEHIND_CONV]), _GROUPS[_BEHIND_CONV]),
                **_unpack_full(with_own(gathered_prep, own[_BEHIND_PREP]), _GROUPS[_BEHIND_PREP])}
        return full["gdn_w_out"][0], full["w_kv"], full["sb_w_q"][0], full["sb_w_o"][0], full["mlp_w_up"], full["mlp_w_down"]

    gains = (mix_pre_gain, mix_post_gain, mlp_pre_gain, mlp_post_gain, kv_gain[None])
    small = (gdn_a_log, gdn_dt_bias, gdn_out_gain)
    tile = PACK_ROW_TILE

    def to_chip_partials(grads_full, groups, tag):
        bufs = _pack_full(grads_full, groups)
        p32, p16 = [], []
        for i, (buf, other) in enumerate(zip(bufs, _swap_halves(bufs, f"grads_to_sibling_{tag}"))):
            _, n, cols = buf.shape
            p, pb = _add_rows(f"grads_add_sibling_{tag}{i}", [(buf.reshape(2 * n, cols), cc * (n // tile)), (other, 0)], n, (F32, BF16), tile)
            p32.append(p.reshape(N_CHIPS, -1, cols))
            p16.append(pb.reshape(N_CHIPS, -1, cols))
        return p32, tuple(p16)

    loss_rows, grad_x, g_full, (partial, from_chips) = _local_step(
        x[0], loss_target[0], gains, small, ((*own[_FIRST], own_taps), own[_BEHIND_CONV], own[_BEHIND_PREP]), assemble_first, assemble,
        lambda g: to_chip_partials(g, _GROUPS[_EARLY_GRADS], "early"), lambda g: to_chip_partials(g, _GROUPS[_FIRST], "late"))

    reduced = []
    for i, (p, others) in enumerate(zip(partial, from_chips)):
        _, r, cols = p.shape
        terms = [(p.reshape(N_CHIPS * r, cols), chip * (r // tile))] + [(others.reshape(3 * r, cols), j * (r // tile)) for j in range(3)]
        reduced.append(_add_rows(f"grads_add_chips_{i}", terms, r, (F32,), tile)[0])
    g_shard = _unpack_shards([jnp.where(cc == 0, jnp.stack([r, o]), jnp.stack([o, r])) for r, o in zip(reduced, _share_halves(tuple(reduced)))])

    g_small_local = {n: g_full[n] for n, _ in _SMALL if n != "loss"}
    g_small_local["loss"] = loss_rows[0, 0]
    _, small_sum = _gather8(_pack_rows(g_small_local, _SMALL), "allreduce_small")
    g_small = _unpack_rows(small_sum, _SMALL)
    loss = g_small.pop("loss")
    g_small["gdn_conv_w"] = lax.dynamic_slice_in_dim(g_small["gdn_conv_w"], chip * conv_cols, conv_cols, axis=2)

    grads = {**g_shard, **g_small}
    deltas, new_m, new_v = {}, {}, {}
    for n in _WEIGHTS:
        d2, m2, v2 = _adamw(_as2d(w[n]), _as2d(grads[n]), _as2d(m[n]), _as2d(v[n]), "adamw_" + n)
        deltas[n], new_m[n], new_v[n] = d2.reshape(w[n].shape), m2.reshape(w[n].shape), v2.reshape(w[n].shape)
    return (loss, grad_x[None], *[grads[n].reshape(w[n].shape) for n in _WEIGHTS], *[deltas[n] for n in _WEIGHTS],
            *[new_m[n] for n in _WEIGHTS], *[new_v[n] for n in _WEIGHTS])
```

```python
import functools

import jax
import jax.numpy as jnp
from jax import lax
from jax.experimental import pallas as pl
from jax.experimental.pallas import tpu as pltpu

F32, BF16 = jnp.float32, jnp.bfloat16
HI = lax.Precision.HIGHEST
MESH = pl.DeviceIdType.MESH

EPS = 1e-6
HEADS = 8
HEAD_DIM = 128
CHUNK = 64
CHUNK_SHIFT = CHUNK.bit_length() - 1
CONV_K = 4
QKV = 3 * HEADS * HEAD_DIM

ADAM_LR, ADAM_B1, ADAM_B2, ADAM_EPS, ADAM_WD, ADAM_STEP = 0.001, 0.9, 0.999, 1e-08, 0.01, 10

VMEM_LIMIT_BYTES = 48 * 1024 * 1024
LANES = 128

NN = ((1,), (0,))
NT = ((1,), (1,))
TN = ((0,), (0,))


def _dot(a, b, dims=NN, precision=None):
    return lax.dot_general(a, b, (dims, ((), ())), precision=precision, preferred_element_type=F32)


def _params(*sem):
    return pltpu.CompilerParams(dimension_semantics=sem, vmem_limit_bytes=VMEM_LIMIT_BYTES)


def _iota(shape, axis):
    return lax.broadcasted_iota(jnp.int32, shape, axis)


def _matmul(a, b, mode, out_dtype, name, tm=1024, tn=1024, tk=2048, add=None, epilogue=None, extras=(), into=None):
    if mode == "nn":
        (m, k), (k2, n) = a.shape, b.shape
    elif mode == "nt":
        (m, k), (n, k2) = a.shape, b.shape
    else:
        (k, m), (k2, n) = a.shape, b.shape
    assert k == k2, (a.shape, b.shape, mode)
    tm, tn, tk = min(tm, m), min(tn, n), min(tk, k)
    assert m % tm == 0 and n % tn == 0 and k % tk == 0, (a.shape, b.shape, mode)
    nk = k // tk
    dims = {"nn": NN, "nt": NT, "tn": TN}[mode]
    tiles = ([add] if add is not None else []) + list(extras)
    out_dtypes = out_dtype if epilogue is not None else (out_dtype,)
    n_in = 2 + len(tiles)
    carried = [into[1]] if into is not None and into[1] is not None else []

    def finish(acc, extra_refs, o_refs):
        res = (acc,) if epilogue is None else epilogue(acc, *[r[...] for r in extra_refs])
        for o_ref, r in zip(o_refs, res):
            o_ref[...] = r.astype(o_ref.dtype)

    def body(*refs):
        a_ref, b_ref = refs[:2]
        extra_refs = refs[n_in - len(extras):n_in]
        o_refs, acc_ref = refs[n_in + len(carried):-1], refs[-1]
        prod = _dot(a_ref[...].astype(BF16), b_ref[...].astype(BF16), dims)
        if nk == 1:
            finish(prod + refs[2][...].astype(F32) if add is not None else prod, extra_refs, o_refs)
            return
        kk = pl.program_id(2)

        @pl.when(kk == 0)
        def _():
            acc_ref[...] = refs[2][...].astype(F32) if add is not None else jnp.zeros_like(acc_ref)

        acc_ref[...] += prod

        @pl.when(kk == nk - 1)
        def _():
            finish(acc_ref[...], extra_refs, o_refs)

    a_spec = pl.BlockSpec((tk, tm), lambda i, j, kk: (kk, i)) if mode == "tn" else pl.BlockSpec((tm, tk), lambda i, j, kk: (i, kk))
    b_spec = pl.BlockSpec((tn, tk), lambda i, j, kk: (j, kk)) if mode == "nt" else pl.BlockSpec((tk, tn), lambda i, j, kk: (kk, j))
    o_spec = pl.BlockSpec((tm, tn), lambda i, j, kk: (i, j))
    if into is None:
        out_specs, out_shape = [o_spec] * len(out_dtypes), [jax.ShapeDtypeStruct((m, n), dt) for dt in out_dtypes]
    else:
        shape, _, place = into
        out_specs = [pl.BlockSpec((None, None, tm, tn), lambda i, j, kk: (*place(i, j), 0, 0))]
        out_shape = [jax.ShapeDtypeStruct(shape, out_dtype)]
    res = pl.pallas_call(
        body,
        name=name,
        grid=(m // tm, n // tn, nk),
        in_specs=[a_spec, b_spec] + [o_spec] * len(tiles) + [pl.BlockSpec(memory_space=pl.ANY)] * len(carried),
        out_specs=out_specs,
        out_shape=out_shape,
        input_output_aliases={n_in: 0} if carried else {},
        scratch_shapes=[pltpu.VMEM((tm, tn), F32)],
        compiler_params=_params("parallel", "parallel", "arbitrary"),
    )(a, b, *tiles, *carried)
    return res if epilogue is not None else res[0]


def _row_specs(rows, tm):
    return [pl.BlockSpec((tm, w), lambda i, cb=cb: (i, cb)) for _, w, cb in rows]


def _full_spec(p):
    return pl.BlockSpec(p.shape, lambda i: (0,) * p.ndim)


ROW_TILE = 512


def _rowwise(name, fn, rows, params, outs, tm=ROW_TILE, gather=()):
    t = rows[0][0].shape[0]
    tm = min(tm, t)
    steps = t // tm
    nr, npar, nout, ng = len(rows), len(params), len(outs), len(gather)

    def body(*refs):
        ins = [r[...].astype(F32) for r in refs[:nr]]
        ps = [p[...] for p in refs[nr:nr + npar]]
        shard_refs = refs[nr + npar:nr + npar + ng]
        o_refs = refs[nr + npar + ng:nr + npar + ng + nout]
        all_refs, sems = refs[nr + npar + ng + nout:nr + npar + 2 * ng + nout], refs[nr + npar + 2 * ng + nout:]
        if ng:
            @pl.when(pl.program_id(0) == 0)
            def _():
                for cp in _gather_sends(shard_refs, all_refs, *sems[:2]):
                    cp.start()

        res = fn(*ins, *ps)
        for o_ref, r in zip(o_refs, res):
            o_ref[...] = r.astype(o_ref.dtype)

        if ng:
            @pl.when(pl.program_id(0) == steps - 1)
            def _():
                _gather_finish(shard_refs, all_refs, *sems)

    return pl.pallas_call(
        body,
        name=name,
        grid=(steps,),
        in_specs=_row_specs(rows, tm) + [_full_spec(p) for p in params] + [_HBM] * ng,
        out_specs=[pl.BlockSpec((tm, w), lambda i: (i, 0)) for w, _ in outs] + [_HBM] * ng,
        out_shape=[jax.ShapeDtypeStruct((t, w), dt) for w, dt in outs] + [jax.ShapeDtypeStruct((N_CHIPS,) + s.shape, s.dtype) for s in gather],
        scratch_shapes=[pltpu.SemaphoreType.DMA((3 * ng,))] * (4 if ng else 0),
        compiler_params=_params("arbitrary" if ng else "parallel"),
    )(*[r[0] for r in rows], *params, *gather)


def _add_rows(name, terms, n_rows, out_dtypes, tm):
    cols = terms[0][0].shape[1]
    firsts = jnp.stack([jnp.asarray(first, jnp.int32) for _, first in terms])

    def body(firsts_ref, *refs):
        acc = refs[0][...].astype(F32)
        for r in refs[1:len(terms)]:
            acc = acc + r[...].astype(F32)
        for o_ref in refs[len(terms):]:
            o_ref[...] = acc.astype(o_ref.dtype)

    return pl.pallas_call(
        body,
        name=name,
        grid_spec=pltpu.PrefetchScalarGridSpec(
            num_scalar_prefetch=1,
            grid=(n_rows // tm,),
            in_specs=[pl.BlockSpec((tm, cols), lambda i, firsts_ref, k=k: (firsts_ref[k] + i, 0)) for k in range(len(terms))],
            out_specs=[pl.BlockSpec((tm, cols), lambda i, firsts_ref: (i, 0)) for _ in out_dtypes],
        ),
        out_shape=[jax.ShapeDtypeStruct((n_rows, cols), dt) for dt in out_dtypes],
        compiler_params=_params("parallel"),
    )(firsts, *[a for a, _ in terms])


def _rowwise_bwd(name, fn, rows, params, cots, grad_dtypes, tm=ROW_TILE, partials=()):
    t = rows[0][0].shape[0]
    tm = min(tm, t)
    steps = t // tm
    nr, npar, nc, nsc = len(rows), len(params), len(cots), len(partials)
    want = [j for j, dt in enumerate(grad_dtypes) if dt is not None]
    widths = [rows[j][1] for j in want]
    n_row_outs = len(want)
    n_in = nr + npar + nc

    def body(*refs):
        i = pl.program_id(0)
        ins = [r[...].astype(F32) for r in refs[:nr]]
        ps = [p[...] for p in refs[nr:nr + npar]]
        cs = tuple(c[...].astype(F32) for c in refs[nr + npar:n_in])
        p_refs = refs[n_in:n_in + nsc]
        outs = refs[n_in + nsc:n_in + nsc + n_row_outs + npar]
        from_refs, sems = refs[n_in + nsc + n_row_outs + npar:n_in + 2 * nsc + n_row_outs + npar], refs[n_in + 2 * nsc + n_row_outs + npar:]
        if nsc:
            @pl.when(i == 0)
            def _():
                for cp in _scatter_copies(p_refs, from_refs, *sems):
                    cp.start()

        _, vjp = jax.vjp(fn, *ins, *ps)
        gs = vjp(cs)
        for o_ref, j in zip(outs, want):
            o_ref[...] = gs[j].astype(o_ref.dtype)
        pg_refs = outs[n_row_outs:]

        @pl.when(i == 0)
        def _():
            for pg in pg_refs:
                pg[...] = jnp.zeros_like(pg)

        for pg, g in zip(pg_refs, gs[nr:]):
            pg[...] += g

        if nsc:
            @pl.when(i == steps - 1)
            def _():
                for cp in _scatter_copies(p_refs, from_refs, *sems):
                    cp.wait()

    row_specs = [pl.BlockSpec((tm, w), lambda i: (i, 0)) for w in widths]
    row_shapes = [jax.ShapeDtypeStruct((t, w), grad_dtypes[j]) for j, w in zip(want, widths)]
    res = pl.pallas_call(
        body,
        name=name,
        grid=(steps,),
        in_specs=_row_specs(rows, tm) + [_full_spec(p) for p in params] + [pl.BlockSpec((tm, c.shape[1]), lambda i: (i, 0)) for c in cots] + [_HBM] * nsc,
        out_specs=row_specs + [_full_spec(p) for p in params] + [_HBM] * nsc,
        out_shape=row_shapes + [jax.ShapeDtypeStruct(p.shape, F32) for p in params] + [jax.ShapeDtypeStruct((3,) + p.shape[1:], p.dtype) for p in partials],
        scratch_shapes=[pltpu.SemaphoreType.DMA((3 * nsc,))] * (2 if nsc else 0),
        compiler_params=_params("arbitrary"),
    )(*[r[0] for r in rows], *params, *cots, *partials)
    return res[:n_row_outs], res[n_row_outs:n_row_outs + npar], res[n_row_outs + npar:]


def _rms(x, g):
    return x * lax.rsqrt(jnp.mean(x * x, axis=-1, keepdims=True) + EPS) * g


def _sigmoid(x):
    return 1.0 / (1.0 + jnp.exp(-x))


def _softplus(x):
    return jnp.maximum(x, 0.0) + jnp.log1p(jnp.exp(-jnp.abs(x)))


def _two_pass(x, m):
    hi = x.astype(BF16)
    lo = (x - hi.astype(F32)).astype(BF16)
    return _dot(hi, m) + _dot(lo, m)


def _head_sum_impl(x):
    sums = [jnp.sum(x[:, h * HEAD_DIM:(h + 1) * HEAD_DIM], axis=-1, keepdims=True) for h in range(HEADS)]
    return jnp.concatenate([jnp.broadcast_to(s, (x.shape[0], HEAD_DIM)) for s in sums], axis=1)


@jax.custom_vjp
def _head_sum(x):
    return _head_sum_impl(x)


_head_sum.defvjp(lambda x: (_head_sum_impl(x), None), lambda _, g: (_head_sum_impl(g),))


def _fn_norm(x, g):
    return (_rms(x, g),)


def _fn_gates(ba, al, dt):
    col = _iota((1, LANES), 1)
    g = jnp.where((col >= HEADS) & (col < 2 * HEADS), -jnp.exp(al) * _softplus(ba + dt), 0.0)
    rows = ba.shape[0]
    r, c = _iota((rows, rows), 0), _iota((rows, rows), 1)
    same = (r >> CHUNK_SHIFT) == (c >> CHUNK_SHIFT)
    gc = _dot(jnp.where(same & (r >= c), 1.0, 0.0), g, precision=HI)
    gtot = _dot(jnp.where(same, 1.0, 0.0), g, precision=HI)
    return _sigmoid(ba), gc, gtot


def _fn_post_q(c):
    s = c * _sigmoid(c)
    return (s * lax.rsqrt(_head_sum(s * s) + EPS) * (HEAD_DIM ** -0.5),)


def _fn_post_k(c):
    s = c * _sigmoid(c)
    return (s * lax.rsqrt(_head_sum(s * s) + EPS),)


def _fn_post_v(c):
    return (c * _sigmoid(c),)


def _fn_post(cq, ck, cv):
    return _fn_post_q(cq) + _fn_post_k(ck) + _fn_post_v(cv)


def _fn_outnorm(o, gate, og):
    y = o * lax.rsqrt(_head_sum(o * o) * (1.0 / HEAD_DIM) + EPS) * og
    return (y * (gate * _sigmoid(gate)),)


def _fn_res_norm(x, m, gp, gn):
    x1 = x + _rms(m, gp)
    return x1, _rms(x1, gn)


def _fn_res_norm2(x, m, gp, ga, gb):
    x1 = x + _rms(m, gp)
    return x1, _rms(x1, ga), _rms(x1, gb)


def _relu2_of(u):
    r = jnp.maximum(u, 0.0)
    return (r * r,)


def _relu2_cotangent(da, a):
    return (da * (2.0 * jnp.sqrt(a.astype(F32))),)


def _loss_call(x3, d1, tgt, g, tm=ROW_TILE):
    t, d = x3.shape
    tm = min(tm, t)

    def body(x_ref, d_ref, t_ref, g_ref, loss_ref, dx_ref, dd_ref, dg_ref):
        i = pl.program_id(0)
        y, vjp = jax.vjp(lambda x, dd, gg: x + _rms(dd, gg), x_ref[...], d_ref[...], g_ref[...])
        err = y - t_ref[...]
        lrow = 0.5 * jnp.mean(err * err, axis=-1, keepdims=True)
        dx, dd, dg = vjp(err * (1.0 / d))
        dx_ref[...] = dx
        dd_ref[...] = dd.astype(dd_ref.dtype)

        @pl.when(i == 0)
        def _():
            loss_ref[...] = jnp.zeros_like(loss_ref)
            dg_ref[...] = jnp.zeros_like(dg_ref)

        loss_ref[...] += jnp.broadcast_to(jnp.sum(lrow, axis=0, keepdims=True), loss_ref.shape)
        dg_ref[...] += dg

    row = pl.BlockSpec((tm, d), lambda i: (i, 0))
    return pl.pallas_call(
        body,
        name="loss_head",
        grid=(t // tm,),
        in_specs=[row, row, row, _full_spec(g)],
        out_specs=[pl.BlockSpec((8, LANES), lambda i: (0, 0)), row, row, _full_spec(g)],
        out_shape=[jax.ShapeDtypeStruct((8, LANES), F32), jax.ShapeDtypeStruct((t, d), F32), jax.ShapeDtypeStruct((t, d), BF16), jax.ShapeDtypeStruct(g.shape, F32)],
        compiler_params=_params("arbitrary"),
    )(x3, d1, tgt, g)


HALO = 8


def _conv_fwd(qkvg, conv_w, shards, tm=256):
    t = qkvg.shape[0]
    tm = min(tm, t)
    steps = t // tm
    wide = QKV // 3
    n = len(shards)

    def body(*refs):
        cur_ref, prev_ref, w_ref = refs[:3]
        shard_refs = refs[3:3 + n]
        o_ref, q_ref, k_ref, v_ref = refs[3 + n:7 + n]
        all_refs = refs[7 + n:7 + 2 * n]
        buf, sems = refs[7 + 2 * n], refs[8 + 2 * n:]
        i = pl.program_id(0)

        if n:
            @pl.when(i == 0)
            def _():
                for cp in _gather_sends(shard_refs, all_refs, *sems[:2]):
                    cp.start()

        buf[0:HALO, :] = jnp.where(i > 0, prev_ref[...], 0.0)
        buf[HALO:, :] = cur_ref[...]
        acc = buf[pl.ds(HALO - CONV_K + 1, tm), :] * w_ref[pl.ds(0, 1), :]
        for j in range(1, CONV_K):
            acc = acc + buf[pl.ds(HALO - CONV_K + 1 + j, tm), :] * w_ref[pl.ds(j, 1), :]
        o_ref[...] = acc
        (q_ref[...], k_ref[...], v_ref[...]) = _fn_post(acc[:, 0:wide], acc[:, wide:2 * wide], acc[:, 2 * wide:])

        if n:
            @pl.when(i == steps - 1)
            def _():
                _gather_finish(shard_refs, all_refs, *sems)

    part = pl.BlockSpec((tm, wide), lambda i: (i, 0))
    res = pl.pallas_call(
        body,
        name="conv_fwd",
        grid=(steps,),
        in_specs=[
            pl.BlockSpec((tm, QKV), lambda i: (i, 0)),
            pl.BlockSpec((HALO, QKV), lambda i: (jnp.maximum(i * (tm // HALO) - 1, 0), 0)),
            pl.BlockSpec((CONV_K, QKV), lambda i: (0, 0)),
        ] + [_HBM] * n,
        out_specs=[pl.BlockSpec((tm, QKV), lambda i: (i, 0)), part, part, part] + [_HBM] * n,
        out_shape=[jax.ShapeDtypeStruct((t, QKV), F32)] + [jax.ShapeDtypeStruct((t, wide), F32)] * 3
        + [jax.ShapeDtypeStruct((N_CHIPS,) + s.shape, s.dtype) for s in shards],
        scratch_shapes=[pltpu.VMEM((tm + HALO, QKV), F32)] + [pltpu.SemaphoreType.DMA((3 * n,))] * (4 if n else 0),
        compiler_params=_params("arbitrary"),
    )(qkvg, qkvg, conv_w, *shards)
    return res[:4], res[4:]


def _conv_bwd(conv, dqkv, dgate, qkvg, conv_w, tm=256):
    t = conv.shape[0]
    tm = min(tm, t)
    n = t // tm
    wg = dgate.shape[1]
    wide = QKV // 3

    def conv_cotangent(c_ref, g_refs):
        parts = [c_ref[:, j * wide:(j + 1) * wide] for j in range(3)]
        _, vjp = jax.vjp(_fn_post, *parts)
        return vjp(tuple(g[...] for g in g_refs))

    def body(c_ref, cn_ref, dq_ref, dk_ref, dv_ref, dqn_ref, dkn_ref, dvn_ref, dgate_ref, x_ref, xp_ref, w_ref, dx_ref, dw_ref, bufd, bufx):
        i = pl.program_id(0)
        for j, (cur, nxt) in enumerate(zip(conv_cotangent(c_ref, (dq_ref, dk_ref, dv_ref)), conv_cotangent(cn_ref, (dqn_ref, dkn_ref, dvn_ref)))):
            bufd[0:tm, j * wide:(j + 1) * wide] = cur
            bufd[tm:, j * wide:(j + 1) * wide] = jnp.where(i < n - 1, nxt, 0.0)
        bufx[0:HALO, :] = jnp.where(i > 0, xp_ref[...], 0.0)
        bufx[HALO:, :] = x_ref[...]

        @pl.when(i == 0)
        def _():
            dw_ref[...] = jnp.zeros_like(dw_ref)

        dcv = bufd[0:tm, :]
        acc = bufd[pl.ds(CONV_K - 1, tm), :] * w_ref[pl.ds(0, 1), :]
        for j in range(1, CONV_K):
            acc = acc + bufd[pl.ds(CONV_K - 1 - j, tm), :] * w_ref[pl.ds(j, 1), :]
        dx_ref[:, 0:QKV] = acc.astype(dx_ref.dtype)
        dx_ref[:, QKV:] = dgate_ref[...].astype(dx_ref.dtype)
        for j in range(CONV_K):
            dw_ref[pl.ds(j, 1), :] += jnp.sum(dcv * bufx[pl.ds(HALO - CONV_K + 1 + j, tm), :], axis=0, keepdims=True)

    def cur(width):
        return pl.BlockSpec((tm, width), lambda i: (i, 0))

    def nxt(width):
        return pl.BlockSpec((HALO, width), lambda i: (jnp.minimum((i + 1) * (tm // HALO), t // HALO - 1), 0))

    return pl.pallas_call(
        body,
        name="conv_bwd",
        grid=(n,),
        in_specs=[cur(QKV), nxt(QKV)] + [cur(wide)] * 3 + [nxt(wide)] * 3 + [
            cur(wg),
            cur(QKV),
            pl.BlockSpec((HALO, QKV), lambda i: (jnp.maximum(i * (tm // HALO) - 1, 0), 0)),
            pl.BlockSpec((CONV_K, QKV), lambda i: (0, 0)),
        ],
        out_specs=[pl.BlockSpec((tm, QKV + wg), lambda i: (i, 0)), pl.BlockSpec((HALO, QKV), lambda i: (0, 0))],
        out_shape=[jax.ShapeDtypeStruct((t, QKV + wg), BF16), jax.ShapeDtypeStruct((HALO, QKV), F32)],
        scratch_shapes=[pltpu.VMEM((tm + HALO, QKV), F32), pltpu.VMEM((tm + HALO, QKV), F32)],
        compiler_params=_params("arbitrary"),
    )(conv, conv, *dqkv, *dqkv, dgate, qkvg, qkvg, conv_w)


PREP_CHUNKS = 32
PREP_BWD_CHUNKS = 4
SCAN_CHUNKS = 8


def _hi_lo(x):
    hi = x.astype(BF16)
    return hi, (x - hi.astype(F32)).astype(BF16)


def _mm3(a, b, dims=NN):
    (ah, al), (bh, bl) = _hi_lo(a), _hi_lo(b)
    return _dot(ah, bh, dims) + (_dot(ah, bl, dims) + _dot(al, bh, dims))


def _neumann(lowers):
    c = lowers[0].shape[0]
    eye = jnp.where(_iota((c, c), 0) == _iota((c, c), 1), 1.0, 0.0)
    ps = [-low for low in lowers]
    tmats = [eye + p for p in ps]
    for _ in range(CHUNK_SHIFT - 1):
        ps = [_mm3(p, p) for p in ps]
        tmats = [t + _mm3(t, p) for t, p in zip(tmats, ps)]
    return tuple(tmats)


def _inv_cotangents(tmats, dts):
    half = [_mm3(t, dt, TN) for t, dt in zip(tmats, dts)]
    return tuple(-_mm3(hf, t, NT) for hf, t in zip(half, tmats))


@jax.custom_vjp
def _tri_inv(lowers):
    return _neumann(lowers)


def _tri_inv_fwd(lowers):
    tmats = _neumann(lowers)
    return tmats, tmats


_tri_inv.defvjp(_tri_inv_fwd, lambda tmats, dts: (_inv_cotangents(tmats, dts),))


@jax.custom_vjp
def _tri_inv_known(lowers, tmats):
    return tmats


_tri_inv_known.defvjp(lambda lowers, tmats: (tmats, tmats),
                      lambda tmats, dts: (_inv_cotangents(tmats, dts), tuple(jnp.zeros_like(t) for t in tmats)))


def _prep_chunks(qs, ks, vs, bs, gcs, gts, gcrs, tmats=None):
    c = CHUNK
    r, col = _iota((c, c), 0), _iota((c, c), 1)
    incl, strict = r >= col, r > col
    decays = [jnp.where(incl, jnp.exp(jnp.where(incl, gc - gcr, 0.0)), 0.0) for gc, gcr in zip(gcs, gcrs)]
    kbs = [k * b for k, b in zip(ks, bs)]
    kbfs = [k.astype(BF16) for k in ks]
    lowers = tuple(jnp.where(strict, _dot(kb.astype(BF16), kbf, NT) * decay, 0.0) for kb, kbf, decay in zip(kbs, kbfs, decays))
    tmats = _tri_inv(lowers) if tmats is None else _tri_inv_known(lowers, tuple(tmats))
    outs = []
    for q, k, v, b, gc, gt, kb, kbf, decay, tmat in zip(qs, ks, vs, bs, gcs, gts, kbs, kbfs, decays, tmats):
        tb = tmat.astype(BF16)
        egc = jnp.exp(gc)
        w = _dot(tb, (kb * egc).astype(BF16))
        u = _dot(tb, (v * b).astype(BF16))
        attn = _dot(q.astype(BF16), kbf, NT) * decay
        gl = jnp.broadcast_to(jnp.exp(jnp.mean(gt.reshape(c // 8, 8, 1), axis=0)), (8, HEAD_DIM))
        outs.append((w, u, q * egc, k * jnp.exp(gt - gc), attn, gl))
    return tuple(outs), tmats


def _prep_specs(rows, gch):
    head = pl.BlockSpec((rows, HEAD_DIM), lambda n, h: (n, h))
    gates = pl.BlockSpec((rows, LANES), lambda n, h: (n, 0))
    gcrow = pl.BlockSpec((1, gch, 1, CHUNK), lambda n, h: (h, n, 0, 0))
    square = pl.BlockSpec((1, rows, CHUNK), lambda n, h: (h, n, 0))
    gl = pl.BlockSpec((1, gch * 8, HEAD_DIM), lambda n, h: (h, n, 0))
    return head, gates, gcrow, square, gl


def _pick_lane(ref, sl, lane):
    return jnp.sum(jnp.where(_iota((1, LANES), 1) == lane, ref[sl, :], 0.0), axis=1, keepdims=True)


def _prep_inputs(q_ref, k_ref, v_ref, b_ref, gc_ref, gt_ref, gcr_ref, sls, h):
    return ([q_ref[sl, :] for sl in sls], [k_ref[sl, :] for sl in sls], [v_ref[sl, :] for sl in sls],
            [_pick_lane(b_ref, sl, h) for sl in sls], [_pick_lane(gc_ref, sl, h + HEADS) for sl in sls],
            [_pick_lane(gt_ref, sl, h + HEADS) for sl in sls], [gcr_ref[0, c] for c in range(len(sls))])


def _gdn_prep(q, k, v, beta, gc, gt, gcr, shards=()):
    t = q.shape[0]
    gch = min(PREP_CHUNKS, t // CHUNK)
    rows = gch * CHUNK
    steps = t // rows
    n = len(shards)

    def body(*refs):
        q_ref, k_ref, v_ref, b_ref, gc_ref, gt_ref, gcr_ref = refs[:7]
        shard_refs = refs[7:7 + n]
        w_ref, u_ref, qg_ref, kg_ref, at_ref, gl_ref, tm_ref = refs[7 + n:14 + n]
        all_refs, sems = refs[14 + n:14 + 2 * n], refs[14 + 2 * n:]
        h = pl.program_id(1)

        if n:
            @pl.when(jnp.logical_and(pl.program_id(0) == 0, h == 0))
            def _():
                for cp in _gather_sends(shard_refs, all_refs, *sems[:2]):
                    cp.start()

        sls = [pl.ds(c * CHUNK, CHUNK) for c in range(gch)]
        outs, tmats = _prep_chunks(*_prep_inputs(q_ref, k_ref, v_ref, b_ref, gc_ref, gt_ref, gcr_ref, sls, h))
        for c, (sl, (w, u, qg, kg, attn, gl), tmat) in enumerate(zip(sls, outs, tmats)):
            w_ref[sl, :] = w.astype(BF16)
            u_ref[sl, :] = u
            qg_ref[sl, :] = qg.astype(BF16)
            kg_ref[sl, :] = kg.astype(BF16)
            at_ref[0, sl, :] = attn.astype(BF16)
            gl_ref[0, pl.ds(c * 8, 8), :] = gl
            tm_ref[0, sl, :] = tmat

        if n:
            @pl.when(jnp.logical_and(pl.program_id(0) == steps - 1, h == HEADS - 1))
            def _():
                _gather_finish(shard_refs, all_refs, *sems)

    hb, col, gcrow, square, glb = _prep_specs(rows, gch)
    wide = HEADS * HEAD_DIM
    res = pl.pallas_call(
        body,
        name="gdn_prep",
        grid=(steps, HEADS),
        in_specs=[hb, hb, hb, col, col, col, gcrow] + [_HBM] * n,
        out_specs=[hb, hb, hb, hb, square, glb, square] + [_HBM] * n,
        out_shape=[
            jax.ShapeDtypeStruct((t, wide), BF16),
            jax.ShapeDtypeStruct((t, wide), F32),
            jax.ShapeDtypeStruct((t, wide), BF16),
            jax.ShapeDtypeStruct((t, wide), BF16),
            jax.ShapeDtypeStruct((HEADS, t, CHUNK), BF16),
            jax.ShapeDtypeStruct((HEADS, t // CHUNK * 8, HEAD_DIM), F32),
            jax.ShapeDtypeStruct((HEADS, t, CHUNK), F32),
        ] + [jax.ShapeDtypeStruct((N_CHIPS,) + s.shape, s.dtype) for s in shards],
        scratch_shapes=[pltpu.SemaphoreType.DMA((3 * n,))] * (4 if n else 0),
        compiler_params=_params("arbitrary", "arbitrary") if n else _params("parallel", "parallel"),
    )(q, k, v, beta, gc, gt, gcr, *shards)
    return res[:7], res[7:]


def _gdn_prep_bwd(q, k, v, beta, gc, gt, gcr, tmat, dw, du, dqg, dkg, dattn, dgl, partials=()):
    t = q.shape[0]
    gch = min(PREP_BWD_CHUNKS, t // CHUNK)
    rows = gch * CHUNK
    steps = t // rows
    n_sc = len(partials)

    def body(*refs):
        (q_ref, k_ref, v_ref, b_ref, gc_ref, gt_ref, gcr_ref, tm_ref, dw_ref, du_ref, dqg_ref, dkg_ref, dat_ref, dgl_ref) = refs[:14]
        p_refs = refs[14:14 + n_sc]
        dq_ref, dk_ref, dv_ref, db_ref, dgc_ref, dgt_ref, dgcr_ref = refs[14 + n_sc:21 + n_sc]
        from_refs, sems = refs[21 + n_sc:21 + 2 * n_sc], refs[21 + 2 * n_sc:]
        h = pl.program_id(1)
        lane = _iota((1, LANES), 1)

        if n_sc:
            @pl.when(jnp.logical_and(pl.program_id(0) == 0, h == 0))
            def _():
                for cp in _scatter_copies(p_refs, from_refs, *sems):
                    cp.start()

        @pl.when(h == 0)
        def _():
            db_ref[...] = jnp.zeros_like(db_ref)
            dgc_ref[...] = jnp.zeros_like(dgc_ref)
            dgt_ref[...] = jnp.zeros_like(dgt_ref)

        sls = [pl.ds(c * CHUNK, CHUNK) for c in range(gch)]
        known = [tm_ref[0, sl, :] for sl in sls]
        _, vjp = jax.vjp(lambda *a: _prep_chunks(*a, tmats=known)[0], *_prep_inputs(q_ref, k_ref, v_ref, b_ref, gc_ref, gt_ref, gcr_ref, sls, h))
        cots = tuple((dw_ref[sl, :], du_ref[sl, :], dqg_ref[sl, :], dkg_ref[sl, :], dat_ref[0, sl, :], dgl_ref[0, pl.ds(c * 8, 8), :]) for c, sl in enumerate(sls))
        dqs, dks, dvs, dbs, dgcs, dgts, dgcrs = vjp(cots)
        for c, sl in enumerate(sls):
            dq_ref[sl, :] = dqs[c]
            dk_ref[sl, :] = dks[c]
            dv_ref[sl, :] = dvs[c]
            db_ref[sl, :] += jnp.where(lane == h, dbs[c], 0.0)
            dgc_ref[sl, :] += jnp.where(lane == h + HEADS, dgcs[c], 0.0)
            dgt_ref[sl, :] += jnp.where(lane == h + HEADS, dgts[c], 0.0)
            dgcr_ref[0, c] = dgcrs[c]

        if n_sc:
            @pl.when(jnp.logical_and(pl.program_id(0) == steps - 1, h == HEADS - 1))
            def _():
                for cp in _scatter_copies(p_refs, from_refs, *sems):
                    cp.wait()

    hb, col, gcrow, square, glb = _prep_specs(rows, gch)
    wide = HEADS * HEAD_DIM
    res = pl.pallas_call(
        body,
        name="gdn_prep_bwd",
        grid=(steps, HEADS),
        in_specs=[hb, hb, hb, col, col, col, gcrow, square, hb, hb, hb, hb, square, glb] + [_HBM] * n_sc,
        out_specs=[hb, hb, hb, col, col, col, gcrow] + [_HBM] * n_sc,
        out_shape=[jax.ShapeDtypeStruct((t, wide), F32)] * 3 + [jax.ShapeDtypeStruct((t, LANES), F32)] * 3 + [jax.ShapeDtypeStruct((HEADS, t // CHUNK, 1, CHUNK), F32)]
        + [jax.ShapeDtypeStruct((3,) + p.shape[1:], p.dtype) for p in partials],
        scratch_shapes=[pltpu.SemaphoreType.DMA((3 * n_sc,))] * (2 if n_sc else 0),
        compiler_params=_params("arbitrary", "arbitrary"),
    )(q, k, v, beta, gc, gt, gcr, tmat, dw, du, dqg, dkg, dattn, dgl, *partials)
    return res[:7], res[7:]


def _gdn_scan(w, u, qg, kg, attn, gl):
    t = w.shape[0]
    n = t // CHUNK
    nch = min(SCAN_CHUNKS, n)
    wide = HEADS * HEAD_DIM

    def body(w_ref, u_ref, qg_ref, kg_ref, at_ref, gl_ref, o_ref, st_ref, s_ref):
        @pl.when(pl.program_id(0) == 0)
        def _():
            s_ref[...] = jnp.zeros_like(s_ref)

        heads = range(HEADS)
        cols = [pl.ds(h * HEAD_DIM, HEAD_DIM) for h in heads]
        for c in range(nch):
            rows, gl_rows = pl.ds(c * CHUNK, CHUNK), pl.ds(c * 8, 8)
            ss = [s_ref[h] for h in heads]
            sbs = [s.astype(BF16) for s in ss]
            vbs = [(u_ref[rows, hs] - _dot(w_ref[rows, hs], sb)).astype(BF16) for hs, sb in zip(cols, sbs)]
            outs = [_dot(qg_ref[rows, hs], sb) + _dot(at_ref[h, rows, :], vb) for h, hs, sb, vb in zip(heads, cols, sbs, vbs)]
            new = [s * jnp.tile(gl_ref[h, gl_rows, :], (HEAD_DIM // 8, 1)) + _dot(kg_ref[rows, hs], vb, TN) for h, hs, s, vb in zip(heads, cols, ss, vbs)]
            for h, hs in zip(heads, cols):
                st_ref[c, h] = ss[h]
                o_ref[rows, hs] = outs[h]
                s_ref[h] = new[h]

    row = pl.BlockSpec((nch * CHUNK, wide), lambda i: (i, 0))
    return pl.pallas_call(
        body,
        name="gdn_scan",
        grid=(n // nch,),
        in_specs=[row, row, row, row, pl.BlockSpec((HEADS, nch * CHUNK, CHUNK), lambda i: (0, i, 0)), pl.BlockSpec((HEADS, nch * 8, HEAD_DIM), lambda i: (0, i, 0))],
        out_specs=[row, pl.BlockSpec((nch, HEADS, HEAD_DIM, HEAD_DIM), lambda i: (i, 0, 0, 0))],
        out_shape=[jax.ShapeDtypeStruct((t, wide), F32), jax.ShapeDtypeStruct((n, HEADS, HEAD_DIM, HEAD_DIM), F32)],
        scratch_shapes=[pltpu.VMEM((HEADS, HEAD_DIM, HEAD_DIM), F32)],
        compiler_params=_params("arbitrary"),
    )(w, u, qg, kg, attn, gl)


def _gdn_scan_bwd(w, u, qg, kg, attn, gl, states, do):
    t = w.shape[0]
    n = t // CHUNK
    nch = min(SCAN_CHUNKS, n)
    steps = n // nch
    wide = HEADS * HEAD_DIM

    def body(w_ref, u_ref, qg_ref, kg_ref, at_ref, gl_ref, st_ref, do_ref, dw_ref, du_ref, dqg_ref, dkg_ref, dat_ref, dgl_ref, ds_ref):
        @pl.when(pl.program_id(0) == 0)
        def _():
            ds_ref[...] = jnp.zeros_like(ds_ref)

        heads = range(HEADS)
        cols = [pl.ds(h * HEAD_DIM, HEAD_DIM) for h in heads]
        for c in reversed(range(nch)):
            rows, gl_rows = pl.ds(c * CHUNK, CHUNK), pl.ds(c * 8, 8)
            ss = [st_ref[c, h] for h in heads]
            sbs = [s.astype(BF16) for s in ss]
            dsns = [ds_ref[h] for h in heads]
            dsbs = [d.astype(BF16) for d in dsns]
            dobs = [do_ref[rows, hs].astype(BF16) for hs in cols]
            vbs = [(u_ref[rows, hs] - _dot(w_ref[rows, hs], sb)).astype(BF16) for hs, sb in zip(cols, sbs)]
            dvns = [_dot(at_ref[h, rows, :], dob, TN) + _dot(kg_ref[rows, hs], dsb) for h, hs, dob, dsb in zip(heads, cols, dobs, dsbs)]
            dvbs = [d.astype(BF16) for d in dvns]
            for h, hs in zip(heads, cols):
                dat_ref[h, rows, :] = _dot(dobs[h], vbs[h], NT)
                dqg_ref[rows, hs] = _dot(dobs[h], sbs[h], NT)
                dkg_ref[rows, hs] = _dot(vbs[h], dsbs[h], NT)
                du_ref[rows, hs] = dvns[h]
                dw_ref[rows, hs] = -_dot(dvbs[h], sbs[h], NT)
                dgl_ref[h, gl_rows, :] = jnp.sum((dsns[h] * ss[h]).reshape(HEAD_DIM // 8, 8, HEAD_DIM), axis=0)
            new = [dsn * jnp.tile(gl_ref[h, gl_rows, :], (HEAD_DIM // 8, 1)) + _dot(qg_ref[rows, hs], dob, TN) - _dot(w_ref[rows, hs], dvb, TN)
                   for h, hs, dsn, dob, dvb in zip(heads, cols, dsns, dobs, dvbs)]
            for h in heads:
                ds_ref[h] = new[h]

    row = pl.BlockSpec((nch * CHUNK, wide), lambda i: (steps - 1 - i, 0))
    at = pl.BlockSpec((HEADS, nch * CHUNK, CHUNK), lambda i: (0, steps - 1 - i, 0))
    glb = pl.BlockSpec((HEADS, nch * 8, HEAD_DIM), lambda i: (0, steps - 1 - i, 0))
    return pl.pallas_call(
        body,
        name="gdn_scan_bwd",
        grid=(steps,),
        in_specs=[row, row, row, row, at, glb, pl.BlockSpec((nch, HEADS, HEAD_DIM, HEAD_DIM), lambda i: (steps - 1 - i, 0, 0, 0)), row],
        out_specs=[row, row, row, row, at, glb],
        out_shape=[jax.ShapeDtypeStruct((t, wide), F32)] * 4 + [jax.ShapeDtypeStruct((HEADS, t, CHUNK), F32), jax.ShapeDtypeStruct((HEADS, n * 8, HEAD_DIM), F32)],
        scratch_shapes=[pltpu.VMEM((HEADS, HEAD_DIM, HEAD_DIM), F32)],
        compiler_params=_params("arbitrary"),
    )(w, u, qg, kg, attn, gl, states, do)


SB_Q = 512
SB_K = 256
SB_STEP = 1
SB_DEAD = -105.0


def _sb_scores(q, k):
    z = _dot(q, k, NT) * (HEAD_DIM ** -0.5)
    lb = jnp.minimum(z, 0.0) - jnp.log(1.0 + jnp.exp(-jnp.abs(z)))
    return lb, lb - z


def _tri(n, rel):
    return jnp.where(rel(_iota((n, n), 0), _iota((n, n), 1)), 1.0, 0.0).astype(BF16)


def _lanes(col):
    return jnp.broadcast_to(col, (col.shape[0], LANES))


def _sb_fwd(q, k, v):
    t = q.shape[0]
    bq, bk = min(SB_Q, t), min(SB_K, t)
    nsub, rep = bq // bk, bk // LANES
    nstep = min(SB_STEP, nsub)
    steps_per_tile = nsub // nstep

    def body(q_ref, k_ref, v_ref, o_ref, rt_ref, first_ref):
        h = pl.program_id(0)
        i = pl.program_id(1)
        o_ref[...] = jnp.zeros_like(o_ref)
        rt_ref[...] = jnp.zeros_like(rt_ref)
        after = _tri(bk, lambda r, c: r > c)

        def block(j, r0, diag):
            st = pl.multiple_of(j * bk, bk)
            kv, vv = k_ref[pl.ds(st, bk), :], v_ref[pl.ds(st, bk), :]
            lb, l1m = _sb_scores(q_ref[r0:, :], kv)
            if diag:
                mask = _iota((bq - r0, bk), 1) + j * bk < _iota((bq - r0, bk), 0) + (r0 + i * bq)
                l1m = jnp.where(mask, l1m, 0.0)
            sums = _two_pass(l1m, after)
            run = rt_ref[r0:, :]
            a = jnp.exp(lb + jnp.tile(run, (1, rep)) + sums)
            if diag:
                a = jnp.where(mask, a, 0.0)
            o_ref[r0:, :] += _dot(a.astype(BF16), vv)
            rt_ref[r0:, :] = run + _lanes(sums[:, 0:1] + l1m[:, 0:1])

        for s in reversed(range(nsub)):
            block(i * nsub + s, s * bk, True)

        def alive(carry):
            u, highest = carry
            return jnp.logical_and(u >= 0, highest > SB_DEAD)

        def step(carry):
            u, _ = carry
            for s in reversed(range(nstep)):
                block(u * nstep + s, 0, False)
            return u - 1, jnp.max(rt_ref[...])

        u_end, _ = lax.while_loop(alive, step, (i * steps_per_tile - 1, jnp.max(rt_ref[...])))
        first_ref[h, i] = u_end + 1

    qb = pl.BlockSpec((bq, HEAD_DIM), lambda h, i: (i, h))
    full = pl.BlockSpec((t, HEAD_DIM), lambda h, i: (0, h))
    return pl.pallas_call(
        body,
        name="sb_fwd",
        grid=(HEADS, t // bq),
        in_specs=[qb, full, full],
        out_specs=[qb, qb, pl.BlockSpec(memory_space=pltpu.SMEM)],
        out_shape=[jax.ShapeDtypeStruct(q.shape, F32), jax.ShapeDtypeStruct(q.shape, F32), jax.ShapeDtypeStruct((HEADS, t // bq), jnp.int32)],
        compiler_params=_params("arbitrary", "arbitrary"),
    )(q, k, v)


def _sb_bwd(q, k, v, rt, first, do):
    t = q.shape[0]
    bq, bk = min(SB_Q, t), min(SB_K, t)
    nsub, rep = bq // bk, bk // LANES
    nstep = min(SB_STEP, nsub)
    steps_per_tile = nsub // nstep
    scale = HEAD_DIM ** -0.5

    def body(first_ref, q_ref, k_ref, v_ref, rt_ref, do_ref, dq_ref, dk_ref, dv_ref, left_ref, pg_ref):
        h = pl.program_id(0)
        i = pl.program_id(1)

        @pl.when(i == 0)
        def _():
            dk_ref[...] = jnp.zeros_like(dk_ref)
            dv_ref[...] = jnp.zeros_like(dv_ref)

        dq_ref[...] = jnp.zeros_like(dq_ref)
        left_ref[...] = jnp.zeros_like(left_ref)
        pg_ref[...] = jnp.zeros_like(pg_ref)
        upto = _tri(bk, lambda r, c: r <= c)

        def block(j, r0, diag):
            st = pl.multiple_of(j * bk, bk)
            kv, vv = k_ref[pl.ds(st, bk), :], v_ref[pl.ds(st, bk), :]
            qv = q_ref[r0:, :]
            dob = do_ref[r0:, :].astype(BF16)
            lb, l1m = _sb_scores(qv, kv)
            if diag:
                mask = _iota((bq - r0, bk), 1) + j * bk < _iota((bq - r0, bk), 0) + (r0 + i * bq)
                l1m = jnp.where(mask, l1m, 0.0)
            sums = _two_pass(l1m, upto)
            left = left_ref[r0:, :]
            a = jnp.exp(lb + jnp.tile(rt_ref[r0:, :] - left, (1, rep)) - sums)
            if diag:
                a = jnp.where(mask, a, 0.0)
            g = _dot(dob, vv, NT) * a
            dv_ref[pl.ds(st, bk), :] += _dot(a.astype(BF16), dob, TN)
            gsum = _two_pass(g, upto)
            pg = pg_ref[r0:, :]
            dz = g - jnp.exp(lb) * (jnp.tile(pg, (1, rep)) + gsum)
            if diag:
                dz = jnp.where(mask, dz, 0.0)
            dzb = (dz * scale).astype(BF16)
            dk_ref[pl.ds(st, bk), :] += _dot(dzb, qv, TN)
            dq_ref[r0:, :] += _dot(dzb, kv)
            left_ref[r0:, :] = left + _lanes(sums[:, bk - 1:bk])
            pg_ref[r0:, :] = pg + _lanes(gsum[:, bk - 1:bk])

        def step(u, carry):
            for s in range(nstep):
                block(u * nstep + s, 0, False)
            return carry

        lax.fori_loop(first_ref[h, i], i * steps_per_tile, step, 0)
        for s in range(nsub):
            block(i * nsub + s, s * bk, True)

    qb = pl.BlockSpec((bq, HEAD_DIM), lambda h, i: (i, h))
    full = pl.BlockSpec((t, HEAD_DIM), lambda h, i: (0, h))
    return pl.pallas_call(
        body,
        name="sb_bwd",
        grid=(HEADS, t // bq),
        in_specs=[pl.BlockSpec(memory_space=pltpu.SMEM), qb, full, full, qb, qb],
        out_specs=[qb, full, full],
        out_shape=[jax.ShapeDtypeStruct(q.shape, F32)] * 3,
        scratch_shapes=[pltpu.VMEM((bq, LANES), F32), pltpu.VMEM((bq, LANES), F32)],
        compiler_params=_params("arbitrary", "arbitrary"),
    )(first, q, k, v, rt, do)


def _adamw(w, g, m, v, name, tm=ROW_TILE):
    r, c = w.shape
    tm = tm if r % tm == 0 else r

    def body(w_ref, g_ref, m_ref, v_ref, d_ref, nm_ref, nv_ref):
        gv = g_ref[...]
        nm = ADAM_B1 * m_ref[...] + (1.0 - ADAM_B1) * gv
        nv = ADAM_B2 * v_ref[...] + (1.0 - ADAM_B2) * (gv * gv)
        m_hat = nm / (1.0 - ADAM_B1 ** ADAM_STEP)
        v_hat = nv / (1.0 - ADAM_B2 ** ADAM_STEP)
        d_ref[...] = -ADAM_LR * (m_hat / (jnp.sqrt(v_hat) + ADAM_EPS) + ADAM_WD * w_ref[...])
        nm_ref[...] = nm
        nv_ref[...] = nv

    blk = pl.BlockSpec((tm, c), lambda i: (i, 0))
    return pl.pallas_call(
        body,
        name=name,
        grid=(r // tm,),
        in_specs=[blk] * 4,
        out_specs=[blk] * 3,
        out_shape=[jax.ShapeDtypeStruct((r, c), F32)] * 3,
        compiler_params=_params("parallel"),
    )(w, g, m, v)


def _local_step(x, tgt, gains, small, shards, assemble_first, assemble, early_reduce=None, late_reduce=None):
    mix_pre, mix_post, mlp_pre, mlp_post, kv_gain = gains
    a_log, dt_bias, out_gain = small
    t, d = x.shape
    row = lambda a, i=None: a[i:i + 1] if i is not None else a
    al = jnp.zeros((1, LANES), F32).at[:, HEADS:2 * HEADS].set(a_log)
    dtb = jnp.zeros((1, LANES), F32).at[:, HEADS:2 * HEADS].set(dt_bias)
    og = jnp.tile(out_gain, (1, HEADS))
    full = lambda a: (a, a.shape[1], 0)

    h0, *gathered_first = _rowwise("norm_in", _fn_norm, [full(x)], [row(mix_pre, 0)], [(d, BF16)], gather=shards[0])
    w_qkvg, w_ba, conv_w = assemble_first(gathered_first)
    qkvg = _matmul(h0, w_qkvg, "nn", F32, "mm_gdn_in", tk=1024)
    ba = _matmul(h0, w_ba, "nn", F32, "mm_gdn_ba", tk=1024)
    (conv, gq, gk, gv), gathered_conv = _conv_fwd(qkvg, conv_w, shards[1])
    beta, gc, gt = _rowwise("gates", _fn_gates, [full(ba)], [al, dtb], [(LANES, F32)] * 3)
    gcr = jnp.swapaxes(gc[:, HEADS:2 * HEADS], 0, 1).reshape(HEADS, t // CHUNK, 1, CHUNK)
    (pw, pu, pqg, pkg, pattn, pgl, ptm), gathered_prep = _gdn_prep(gq, gk, gv, beta, gc, gt, gcr, shards[2])
    w_out, w_kv, w_q, w_o, w_up, w_down = assemble(gathered_conv, gathered_prep)
    w_qkvg_t, w_up_t, w_down_t = (jnp.swapaxes(a, -1, -2) for a in (w_qkvg, w_up, w_down))
    o_gdn, states = _gdn_scan(pw, pu, pqg, pkg, pattn, pgl)
    (on,) = _rowwise("out_norm", _fn_outnorm, [full(o_gdn), (qkvg, d, 3)], [og], [(d, BF16)])
    mix0 = _matmul(on, w_out, "nn", F32, "mm_gdn_out", tk=1024)
    x1, h1 = _rowwise("res_a0", _fn_res_norm, [full(x), full(mix0)], [row(mix_post, 0), row(mlp_pre, 0)], [(d, F32), (d, BF16)])
    (a0,) = _matmul(h1, w_up[0], "nn", (BF16,), "mm_up0", tk=1024, epilogue=_relu2_of)
    d0 = _matmul(a0, w_down[0], "nn", F32, "mm_down0")
    x2, hkv, hq = _rowwise("res_b0", _fn_res_norm2, [full(x1), full(d0)], [row(mlp_post, 0), kv_gain, row(mix_pre, 1)], [(d, F32), (d, BF16), (d, BF16)])
    w_k, w_v = w_kv[:, :d], w_kv[:, d:]
    kp = _matmul(hkv, w_k, "nn", BF16, "mm_k", tk=1024)
    vp = _matmul(hkv, w_v, "nn", BF16, "mm_v", tk=1024)
    qp = _matmul(hq, w_q, "nn", BF16, "mm_q", tk=1024)
    o_sb, rt, sb_first = _sb_fwd(qp, kp, vp)
    mix1 = _matmul(o_sb, w_o, "nn", F32, "mm_sb_out", tk=1024)
    x3, h3 = _rowwise("res_a1", _fn_res_norm, [full(x2), full(mix1)], [row(mix_post, 1), row(mlp_pre, 1)], [(d, F32), (d, BF16)])
    (a1,) = _matmul(h3, w_up[1], "nn", (BF16,), "mm_up1", tk=1024, epilogue=_relu2_of)
    d1 = _matmul(a1, w_down[1], "nn", F32, "mm_down1")

    loss, dx3, dd1, g_mlp_post1 = _loss_call(x3, d1, tgt, row(mlp_post, 1))
    (du1,) = _matmul(dd1, w_down_t[1], "nn", (BF16,), "mm_down1_dx", epilogue=_relu2_cotangent, extras=[a1])
    up_shape, down_shape = _grad_buffer_shape(_GROUPS[0]), _grad_buffer_shape(_GROUPS[1])
    buf_down = _matmul(a1, dd1, "tn", F32, "mm_down1_dw", into=(down_shape, None, lambda i, j: (1, i)))
    dh3 = _matmul(du1, w_up_t[1], "nn", F32, "mm_up1_dx")
    buf_up = _matmul(h3, du1, "tn", F32, "mm_up1_dw", into=(up_shape, None, lambda i, j: (1, j)))
    (dx2, dmix1), (g_mix_post1, g_mlp_pre1), _ = _rowwise_bwd(
        "res_a1_bwd", _fn_res_norm, [full(x2), full(mix1)], [row(mix_post, 1), row(mlp_pre, 1)], [dx3, dh3], [F32, BF16])
    do_sb = _matmul(dmix1, w_o, "nt", BF16, "mm_sb_out_dx")
    g_o = _matmul(o_sb, dmix1, "tn", F32, "mm_sb_out_dw")
    dqp, dkp, dvp = _sb_bwd(qp, kp, vp, rt, sb_first, do_sb)
    dhq = _matmul(dqp, w_q, "nt", F32, "mm_q_dx")
    g_q = _matmul(hq, dqp, "tn", F32, "mm_q_dw")
    dhkv = _matmul(dvp, w_v, "nt", F32, "mm_v_dx", add=_matmul(dkp, w_k, "nt", F32, "mm_k_dx"))
    g_kv = jnp.concatenate([_matmul(hkv, dkp, "tn", F32, "mm_k_dw"), _matmul(hkv, dvp, "tn", F32, "mm_v_dw")], axis=1)
    (dx1, dd0), (g_mlp_post0, g_kv_gain, g_mix_pre1), _ = _rowwise_bwd(
        "res_b0_bwd", _fn_res_norm2, [full(x1), full(d0)], [row(mlp_post, 0), kv_gain, row(mix_pre, 1)], [dx2, dhkv, dhq], [F32, BF16])
    (du0,) = _matmul(dd0, w_down_t[0], "nn", (BF16,), "mm_down0_dx", epilogue=_relu2_cotangent, extras=[a0])
    buf_down = _matmul(a0, dd0, "tn", F32, "mm_down0_dw", into=(down_shape, buf_down, lambda i, j: (0, i)))
    dh1 = _matmul(du0, w_up_t[0], "nn", F32, "mm_up0_dx")
    buf_up = _matmul(h1, du0, "tn", F32, "mm_up0_dw", into=(up_shape, buf_up, lambda i, j: (0, j)))
    (dx0, dmix0), (g_mix_post0, g_mlp_pre0), _ = _rowwise_bwd(
        "res_a0_bwd", _fn_res_norm, [full(x), full(mix0)], [row(mix_post, 0), row(mlp_pre, 0)], [dx1, dh1], [F32, BF16])
    don = _matmul(dmix0, w_out, "nt", F32, "mm_gdn_out_dx")
    g_out = _matmul(on, dmix0, "tn", F32, "mm_gdn_out_dw")
    (do_gdn, dgate), (g_og,), _ = _rowwise_bwd("out_norm_bwd", _fn_outnorm, [full(o_gdn), (qkvg, d, 3)], [og], [don], [F32, F32])
    dpw, dpu, dpqg, dpkg, dpattn, dpgl = _gdn_scan_bwd(pw, pu, pqg, pkg, pattn, pgl, states, do_gdn)
    partial, partial_bf16 = [], ()
    if early_reduce is not None:
        partial, partial_bf16 = early_reduce(dict(gdn_w_out=g_out[None], w_kv=g_kv, sb_w_q=g_q[None], sb_w_o=g_o[None]), {0: buf_up, 1: buf_down})
    (dgq, dgk, dgv, dbeta, dgc, dgt, dgcr), from_chips = _gdn_prep_bwd(gq, gk, gv, beta, gc, gt, gcr, ptm, dpw, dpu, dpqg, dpkg, dpattn, dpgl, partial_bf16)
    dgcr_lanes = jnp.pad(jnp.swapaxes(dgcr.reshape(HEADS, t), 0, 1), ((0, 0), (HEADS, LANES - 2 * HEADS)))
    gate_cots = [dbeta, dgc + dgcr_lanes, dgt]
    (dba,), (g_al, g_dtb), _ = _rowwise_bwd("gates_bwd", _fn_gates, [full(ba)], [al, dtb], gate_cots, [BF16])
    dqkvg, g_conv = _conv_bwd(conv, (dgq, dgk, dgv), dgate, qkvg, conv_w)
    dh0b = _matmul(dba, w_ba, "nt", F32, "mm_gdn_ba_dx", tk=LANES)
    dh0 = _matmul(dqkvg, w_qkvg_t, "nn", F32, "mm_gdn_in_dx", add=dh0b)
    g_qkvg = _matmul(h0, dqkvg, "tn", F32, "mm_gdn_in_dw")
    g_ba = _matmul(h0, dba, "tn", F32, "mm_gdn_ba_dw")
    g_w_in = jnp.concatenate([g_qkvg, g_ba[:, :2 * HEADS]], axis=1)[None]
    partial_late, partial_late_bf16 = late_reduce(dict(gdn_w_in=g_w_in)) if late_reduce is not None else ([], ())
    (grad_x,), (g_mix_pre0,), from_chips_late = _rowwise_bwd(
        "norm_in_bwd", lambda xx, gg: (_rms(xx, gg), xx), [full(x)], [row(mix_pre, 0)], [dh0, dx0], [F32], partials=partial_late_bf16)

    grads = dict(
        mix_pre_gain=jnp.concatenate([g_mix_pre0, g_mix_pre1], axis=0),
        mix_post_gain=jnp.concatenate([g_mix_post0, g_mix_post1], axis=0),
        mlp_pre_gain=jnp.concatenate([g_mlp_pre0, g_mlp_pre1], axis=0),
        mlp_post_gain=jnp.concatenate([g_mlp_post0, g_mlp_post1], axis=0),
        mlp_w_up=jnp.stack([_join(buf_up[layer, :, :d], "cols") for layer in range(2)]),
        mlp_w_down=jnp.stack([_join(buf_down[layer, :, :d], "rows") for layer in range(2)]),
        gdn_w_in=g_w_in,
        gdn_conv_w=g_conv[None, :CONV_K],
        gdn_a_log=g_al[:, HEADS:2 * HEADS],
        gdn_dt_bias=g_dtb[:, HEADS:2 * HEADS],
        gdn_out_gain=jnp.sum(g_og.reshape(HEADS, HEAD_DIM), axis=0, keepdims=True),
        gdn_w_out=g_out[None],
        kv_gain=g_kv_gain[0],
        w_kv=g_kv,
        sb_w_q=g_q[None],
        sb_w_o=g_o[None],
    )
    return loss, grad_x, grads, (list(partial) + list(partial_late), list(from_chips) + list(from_chips_late))


N_DEV = 8
N_CHIPS = 4
PACK_ROW_TILE = 128

_HBM = pl.BlockSpec(memory_space=pltpu.HBM)


def _place():
    return lax.axis_index("x"), lax.axis_index("y"), lax.axis_index("c")


def _other_chips(x, y):
    return [(1 - x, y), (x, 1 - y), (1 - x, 1 - y)]


def _remote(src, dst, send_sem, recv_sem, to):
    return pltpu.make_async_remote_copy(src_ref=src, dst_ref=dst, send_sem=send_sem, recv_sem=recv_sem, device_id=to, device_id_type=MESH)


def _gather8(v, name):
    rows, cols = v.shape

    def body(v_ref, out_ref, sum_ref, send_sems, recv_sems, local_sem):
        x, y, c = _place()
        me, sibling = (x, y, c), (x, y, 1 - c)
        chips = _other_chips(x, y)

        def blk(px, py, pc):
            return out_ref.at[pl.ds((4 * px + 2 * py + pc) * rows, rows), :]

        def copy(k, block, to, src=None):
            return _remote(blk(*block) if src is None else src, blk(*block), send_sems.at[k], recv_sems.at[k], to)

        mine = pltpu.make_async_copy(v_ref, blk(*me), local_sem)
        mine.start()
        first = [copy(0, me, sibling, src=v_ref)] + [copy(1 + j, me, (*chip, c), src=v_ref) for j, chip in enumerate(chips)]
        for cp in first:
            cp.start()
        passed = [copy(4 + j, (*chip, c), sibling) for j, chip in enumerate(chips)]
        for j, chip in enumerate(chips):
            copy(1 + j, (*chip, c), me).wait_recv()
            passed[j].start()
        copy(0, sibling, me).wait_recv()
        for j, chip in enumerate(chips):
            copy(4 + j, (*chip, 1 - c), me).wait_recv()
        for cp in first + passed:
            cp.wait_send()
        mine.wait()
        acc = out_ref[pl.ds(0, rows), :]
        for dev in range(1, N_DEV):
            acc = acc + out_ref[pl.ds(dev * rows, rows), :]
        sum_ref[...] = acc

    vm = pl.BlockSpec(memory_space=pltpu.VMEM)
    return pl.pallas_call(
        body,
        name=name,
        out_shape=[jax.ShapeDtypeStruct((N_DEV * rows, cols), v.dtype), jax.ShapeDtypeStruct((rows, cols), v.dtype)],
        in_specs=[vm],
        out_specs=[vm, vm],
        scratch_shapes=[pltpu.SemaphoreType.DMA((7,)), pltpu.SemaphoreType.DMA((7,)), pltpu.SemaphoreType.DMA],
    )(v)


def _hbm_call(body, name, arrs, out_shapes, sem_counts):
    n = len(arrs)

    def wrapped(*refs):
        body(refs[:n], refs[n:2 * n], *refs[2 * n:])

    return pl.pallas_call(
        wrapped,
        name=name,
        out_shape=[jax.ShapeDtypeStruct(s, a.dtype) for s, a in zip(out_shapes, arrs)],
        in_specs=[_HBM] * n,
        out_specs=[_HBM] * n,
        scratch_shapes=[pltpu.SemaphoreType.DMA((k,)) for k in sem_counts],
    )(*arrs)


def _gather_sends(w_refs, out_refs, send_sems, recv_sems):
    x, y, c = _place()
    s_me = 2 * x + y
    return [_remote(w.at[c], o.at[s_me, c], send_sems.at[3 * a + j], recv_sems.at[3 * a + j], (px, py, c))
            for a, (w, o) in enumerate(zip(w_refs, out_refs)) for j, (px, py) in enumerate(_other_chips(x, y))]


def _gather_finish(w_refs, out_refs, send_sems, recv_sems, fsend_sems, frecv_sems):
    x, y, c = _place()
    chips = _other_chips(x, y)
    passed = []
    for a, o in enumerate(out_refs):
        for j, (px, py) in enumerate(chips):
            half = o.at[2 * px + py, c]
            _remote(half, half, send_sems.at[3 * a + j], recv_sems.at[3 * a + j], (px, py, c)).wait_recv()
            fwd = _remote(half, half, fsend_sems.at[3 * a + j], frecv_sems.at[3 * a + j], (x, y, 1 - c))
            fwd.start()
            passed.append(fwd)
    for a, o in enumerate(out_refs):
        for j, (px, py) in enumerate(chips):
            half = o.at[2 * px + py, 1 - c]
            _remote(half, half, fsend_sems.at[3 * a + j], frecv_sems.at[3 * a + j], (x, y, 1 - c)).wait_recv()
    for cp in _gather_sends(w_refs, out_refs, send_sems, recv_sems) + passed:
        cp.wait_send()


def _swap_halves(arrs, name):
    n = len(arrs)

    def body(g_refs, a_refs, send_sems, recv_sems):
        x, y, c = _place()
        cps = [_remote(g.at[1 - c], a, send_sems.at[i], recv_sems.at[i], (x, y, 1 - c)) for i, (g, a) in enumerate(zip(g_refs, a_refs))]
        for cp in cps:
            cp.start()
        for cp in cps:
            cp.wait()

    return _hbm_call(body, name, arrs, [a.shape[1:] for a in arrs], [n, n])


def _scatter_copies(p_refs, b_refs, send_sems, recv_sems):
    x, y, c = _place()
    return [_remote(p.at[2 * px + py], b.at[j], send_sems.at[3 * i + j], recv_sems.at[3 * i + j], (px, py, c))
            for i, (p, b) in enumerate(zip(p_refs, b_refs)) for j, (px, py) in enumerate(_other_chips(x, y))]


def _share_halves(arrs):
    n = len(arrs)

    def body(q_refs, out_refs, send_sems, recv_sems):
        x, y, c = _place()
        cps = [_remote(q, o, send_sems.at[i], recv_sems.at[i], (x, y, 1 - c)) for i, (q, o) in enumerate(zip(q_refs, out_refs))]
        for cp in cps:
            cp.start()
        for cp in cps:
            cp.wait()

    return _hbm_call(body, "grads_share", arrs, [a.shape for a in arrs], [n, n])


_GROUPS = (
    (("mlp_w_up", (2, 1024, 1024), "cols"), ("gdn_w_out", (1, 256, 1024), "rows")),
    (("mlp_w_down", (2, 1024, 1024), "rows"), ("sb_w_q", (1, 256, 1024), "rows"), ("sb_w_o", (1, 256, 1024), "rows")),
    (("w_kv", (1024, 512), "cols"),),
    (("gdn_w_in", (1, 1024, 1028), "cols"),),
)
_BEHIND_CONV, _BEHIND_PREP, _FIRST = slice(0, 1), slice(1, 3), slice(3, 4)
_EARLY_GRADS = slice(0, 3)


def _numel(shape):
    n = 1
    for s in shape:
        n *= s
    return n


def _half_rows(shape):
    return _numel(shape[:-1]) // 2


def _pack_shards(shards, dtype):
    return tuple(jnp.concatenate([shards[n].astype(dtype).reshape(2, _half_rows(shape), shape[-1]) for n, shape, _ in grp], axis=1) for grp in _GROUPS)


def _unpack_shards(bufs):
    out = {}
    for grp, buf in zip(_GROUPS, bufs):
        off = 0
        for n, shape, _ in grp:
            out[n] = buf[:, off:off + _half_rows(shape)].reshape(shape)
            off += _half_rows(shape)
    return out


def _join(stacked, how):
    nd = stacked.ndim - 1
    ax = nd - 1 if how == "cols" else nd - 2
    moved = jnp.moveaxis(stacked, 0, ax)
    shape = list(stacked.shape[1:])
    shape[ax] *= N_CHIPS
    return moved.reshape(shape)


def _split(full, shard_shape, how):
    nd = len(shard_shape)
    ax = nd - 1 if how == "cols" else nd - 2
    shape = list(shard_shape)
    shape.insert(ax, N_CHIPS)
    return jnp.moveaxis(full.reshape(shape), ax, 0)


def _unpack_full(gathered, groups):
    out = {}
    for grp, buf in zip(groups, gathered):
        off = 0
        for n, shape, how in grp:
            out[n] = _join(buf[:, :, off:off + _half_rows(shape)].reshape((N_CHIPS,) + shape), how)
            off += _half_rows(shape)
    return out


def _grad_buffer_shape(grp):
    return (2, N_CHIPS, sum(_half_rows(shape) for _, shape, _ in grp), grp[0][1][-1])


def _pack_full(full, groups, started=None):
    bufs = []
    for gi, grp in enumerate(groups):
        def halves(n, shape, how):
            return jnp.swapaxes(_split(full[n], shape, how).reshape(N_CHIPS, 2, _half_rows(shape), shape[-1]), 0, 1)

        if started is not None and gi in started:
            buf, off = started[gi], _half_rows(grp[0][1])
            for n, shape, how in grp[1:]:
                buf = buf.at[:, :, off:off + _half_rows(shape), :].set(halves(n, shape, how))
                off += _half_rows(shape)
        else:
            buf = jnp.concatenate([halves(n, shape, how) for n, shape, how in grp], axis=2)
        bufs.append(buf.reshape(2, -1, buf.shape[-1]))
    return tuple(bufs)


_SMALL = (
    ("mix_pre_gain", (2, 1024)),
    ("mix_post_gain", (2, 1024)),
    ("mlp_pre_gain", (2, 1024)),
    ("mlp_post_gain", (2, 1024)),
    ("kv_gain", (1024,)),
    ("gdn_out_gain", (1, 128)),
    ("gdn_a_log", (1, 8)),
    ("gdn_dt_bias", (1, 8)),
    ("gdn_conv_w", (1, 4, 3072)),
    ("loss", ()),
)


def _rows_of(shape):
    return -(-_numel(shape) // LANES)


def _pack_rows(vals, layout):
    parts = []
    for n, shape in layout:
        flat = vals[n].reshape(-1)
        parts.append(jnp.pad(flat, (0, _rows_of(shape) * LANES - flat.shape[0])))
    flat = jnp.concatenate(parts)
    rows = -(-flat.shape[0] // (8 * LANES)) * 8
    return jnp.pad(flat, (0, rows * LANES - flat.shape[0])).reshape(rows, LANES)


def _unpack_rows(packed, layout):
    flat = packed.reshape(-1)
    out, off = {}, 0
    for n, shape in layout:
        out[n] = flat[off:off + _numel(shape)].reshape(shape)
        off += _rows_of(shape) * LANES
    return out


_WEIGHTS = ("mix_pre_gain", "mix_post_gain", "mlp_pre_gain", "mlp_post_gain", "mlp_w_up", "mlp_w_down", "gdn_w_in", "gdn_conv_w",
            "gdn_a_log", "gdn_dt_bias", "gdn_out_gain", "gdn_w_out", "kv_gain", "w_kv", "sb_w_q", "sb_w_o")


def _as2d(a):
    return a.reshape(1, -1) if a.ndim <= 1 else a.reshape(-1, a.shape[-1])


def kernel(x, mix_pre_gain, mix_post_gain, mlp_pre_gain, mlp_post_gain, mlp_w_up, mlp_w_down, gdn_w_in, gdn_conv_w, gdn_a_log, gdn_dt_bias, gdn_out_gain, gdn_w_out, kv_gain, w_kv, sb_w_q, sb_w_o, loss_target, m_mix_pre_gain, m_mix_post_gain, m_mlp_pre_gain, m_mlp_post_gain, m_mlp_w_up, m_mlp_w_down, m_gdn_w_in, m_gdn_conv_w, m_gdn_a_log, m_gdn_dt_bias, m_gdn_out_gain, m_gdn_w_out, m_kv_gain, m_w_kv, m_sb_w_q, m_sb_w_o, v_mix_pre_gain, v_mix_post_gain, v_mlp_pre_gain, v_mlp_post_gain, v_mlp_w_up, v_mlp_w_down, v_gdn_w_in, v_gdn_conv_w, v_gdn_a_log, v_gdn_dt_bias, v_gdn_out_gain, v_gdn_w_out, v_kv_gain, v_w_kv, v_sb_w_q, v_sb_w_o):
    w = dict(mix_pre_gain=mix_pre_gain, mix_post_gain=mix_post_gain, mlp_pre_gain=mlp_pre_gain, mlp_post_gain=mlp_post_gain, mlp_w_up=mlp_w_up, mlp_w_down=mlp_w_down, gdn_w_in=gdn_w_in, gdn_conv_w=gdn_conv_w, gdn_a_log=gdn_a_log, gdn_dt_bias=gdn_dt_bias, gdn_out_gain=gdn_out_gain, gdn_w_out=gdn_w_out, kv_gain=kv_gain, w_kv=w_kv, sb_w_q=sb_w_q, sb_w_o=sb_w_o)
    m = dict(mix_pre_gain=m_mix_pre_gain, mix_post_gain=m_mix_post_gain, mlp_pre_gain=m_mlp_pre_gain, mlp_post_gain=m_mlp_post_gain, mlp_w_up=m_mlp_w_up, mlp_w_down=m_mlp_w_down, gdn_w_in=m_gdn_w_in, gdn_conv_w=m_gdn_conv_w, gdn_a_log=m_gdn_a_log, gdn_dt_bias=m_gdn_dt_bias, gdn_out_gain=m_gdn_out_gain, gdn_w_out=m_gdn_w_out, kv_gain=m_kv_gain, w_kv=m_w_kv, sb_w_q=m_sb_w_q, sb_w_o=m_sb_w_o)
    v = dict(mix_pre_gain=v_mix_pre_gain, mix_post_gain=v_mix_post_gain, mlp_pre_gain=v_mlp_pre_gain, mlp_post_gain=v_mlp_post_gain, mlp_w_up=v_mlp_w_up, mlp_w_down=v_mlp_w_down, gdn_w_in=v_gdn_w_in, gdn_conv_w=v_gdn_conv_w, gdn_a_log=v_gdn_a_log, gdn_dt_bias=v_gdn_dt_bias, gdn_out_gain=v_gdn_out_gain, gdn_w_out=v_gdn_w_out, kv_gain=v_kv_gain, w_kv=v_w_kv, sb_w_q=v_sb_w_q, sb_w_o=v_sb_w_o)
    cx, cy, cc = _place()
    chip = 2 * cx + cy
    conv_cols = gdn_conv_w.shape[-1]

    own = _pack_shards(w, BF16)
    own_taps = jnp.pad(gdn_conv_w[0], ((0, CONV_K), (0, 0))).reshape(2, CONV_K, conv_cols)
    with_own = lambda gathered, mine: [lax.dynamic_update_index_in_dim(g, m, chip, 0) for g, m in zip(gathered, mine)]

    def assemble_first(gathered):
        w_in_all, taps_all = with_own(gathered, (*own[_FIRST], own_taps))
        w_in = _unpack_full([w_in_all], _GROUPS[_FIRST])["gdn_w_in"][0]
        taps = jnp.swapaxes(taps_all[:, 0], 0, 1).reshape(CONV_K, N_CHIPS * conv_cols)
        return w_in[:, :4 * HEADS * HEAD_DIM], jnp.pad(w_in[:, 4 * HEADS * HEAD_DIM:], ((0, 0), (0, LANES - 2 * HEADS))), taps

    def assemble(gathered_conv, gathered_prep):
        full = {**_unpack_full(with_own(gathered_conv, own[_BEHIND_CONV]), _GROUPS[_BEHIND_CONV]),
                **_unpack_full(with_own(gathered_prep, own[_BEHIND_PREP]), _GROUPS[_BEHIND_PREP])}
        return full["gdn_w_out"][0], full["w_kv"], full["sb_w_q"][0], full["sb_w_o"][0], full["mlp_w_up"], full["mlp_w_down"]

    gains = (mix_pre_gain, mix_post_gain, mlp_pre_gain, mlp_post_gain, kv_gain[None])
    small = (gdn_a_log, gdn_dt_bias, gdn_out_gain)
    tile = PACK_ROW_TILE

    def to_chip_partials(grads_full, groups, tag, started=None):
        bufs = _pack_full(grads_full, groups, started)
        p32, p16 = [], []
        for i, (buf, other) in enumerate(zip(bufs, _swap_halves(bufs, f"grads_to_sibling_{tag}"))):
            _, n, cols = buf.shape
            p, pb = _add_rows(f"grads_add_sibling_{tag}{i}", [(buf.reshape(2 * n, cols), cc * (n // tile)), (other, 0)], n, (F32, BF16), tile)
            p32.append(p.reshape(N_CHIPS, -1, cols))
            p16.append(pb.reshape(N_CHIPS, -1, cols))
        return p32, tuple(p16)

    loss_rows, grad_x, g_full, (partial, from_chips) = _local_step(
        x[0], loss_target[0], gains, small, ((*own[_FIRST], own_taps), own[_BEHIND_CONV], own[_BEHIND_PREP]), assemble_first, assemble,
        lambda g, started: to_chip_partials(g, _GROUPS[_EARLY_GRADS], "early", started), lambda g: to_chip_partials(g, _GROUPS[_FIRST], "late"))

    reduced = []
    for i, (p, others) in enumerate(zip(partial, from_chips)):
        _, r, cols = p.shape
        terms = [(p.reshape(N_CHIPS * r, cols), chip * (r // tile))] + [(others.reshape(3 * r, cols), j * (r // tile)) for j in range(3)]
        reduced.append(_add_rows(f"grads_add_chips_{i}", terms, r, (F32,), tile)[0])
    g_shard = _unpack_shards([jnp.where(cc == 0, jnp.stack([r, o]), jnp.stack([o, r])) for r, o in zip(reduced, _share_halves(tuple(reduced)))])

    g_small_local = {n: g_full[n] for n, _ in _SMALL if n != "loss"}
    g_small_local["loss"] = loss_rows[0, 0]
    _, small_sum = _gather8(_pack_rows(g_small_local, _SMALL), "allreduce_small")
    g_small = _unpack_rows(small_sum, _SMALL)
    loss = g_small.pop("loss")
    g_small["gdn_conv_w"] = lax.dynamic_slice_in_dim(g_small["gdn_conv_w"], chip * conv_cols, conv_cols, axis=2)

    grads = {**g_shard, **g_small}
    deltas, new_m, new_v = {}, {}, {}
    for n in _WEIGHTS:
        d2, m2, v2 = _adamw(_as2d(w[n]), _as2d(grads[n]), _as2d(m[n]), _as2d(v[n]), "adamw_" + n)
        deltas[n], new_m[n], new_v[n] = d2.reshape(w[n].shape), m2.reshape(w[n].shape), v2.reshape(w[n].shape)
    return (loss, grad_x[None], *[grads[n].reshape(w[n].shape) for n in _WEIGHTS], *[deltas[n] for n in _WEIGHTS],
            *[new_m[n] for n in _WEIGHTS], *[new_v[n] for n in _WEIGHTS])
```

```python
import functools

import jax
import jax.numpy as jnp
from jax import lax
from jax.experimental import pallas as pl
from jax.experimental.pallas import tpu as pltpu

F32, BF16 = jnp.float32, jnp.bfloat16
HI = lax.Precision.HIGHEST
MESH = pl.DeviceIdType.MESH

EPS = 1e-6
HEADS = 8
HEAD_DIM = 128
CHUNK = 64
CHUNK_SHIFT = CHUNK.bit_length() - 1
CONV_K = 4
QKV = 3 * HEADS * HEAD_DIM

ADAM_LR, ADAM_B1, ADAM_B2, ADAM_EPS, ADAM_WD, ADAM_STEP = 0.001, 0.9, 0.999, 1e-08, 0.01, 10

VMEM_LIMIT_BYTES = 48 * 1024 * 1024
LANES = 128

NN = ((1,), (0,))
NT = ((1,), (1,))
TN = ((0,), (0,))


def _dot(a, b, dims=NN, precision=None):
    return lax.dot_general(a, b, (dims, ((), ())), precision=precision, preferred_element_type=F32)


def _params(*sem):
    return pltpu.CompilerParams(dimension_semantics=sem, vmem_limit_bytes=VMEM_LIMIT_BYTES)


def _iota(shape, axis):
    return lax.broadcasted_iota(jnp.int32, shape, axis)


def _matmul(a, b, mode, out_dtype, name, tm=1024, tn=1024, tk=2048, add=None, epilogue=None, extras=(), into=None):
    if mode == "nn":
        (m, k), (k2, n) = a.shape, b.shape
    elif mode == "nt":
        (m, k), (n, k2) = a.shape, b.shape
    else:
        (k, m), (k2, n) = a.shape, b.shape
    assert k == k2, (a.shape, b.shape, mode)
    tm, tn, tk = min(tm, m), min(tn, n), min(tk, k)
    assert m % tm == 0 and n % tn == 0 and k % tk == 0, (a.shape, b.shape, mode)
    nk = k // tk
    dims = {"nn": NN, "nt": NT, "tn": TN}[mode]
    tiles = ([add] if add is not None else []) + list(extras)
    out_dtypes = out_dtype if epilogue is not None else (out_dtype,)
    n_in = 2 + len(tiles)
    carried = [into[1]] if into is not None and into[1] is not None else []

    def finish(acc, extra_refs, o_refs):
        res = (acc,) if epilogue is None else epilogue(acc, *[r[...] for r in extra_refs])
        for o_ref, r in zip(o_refs, res):
            o_ref[...] = r.astype(o_ref.dtype)

    def body(*refs):
        a_ref, b_ref = refs[:2]
        extra_refs = refs[n_in - len(extras):n_in]
        o_refs, acc_ref = refs[n_in + len(carried):-1], refs[-1]
        prod = _dot(a_ref[...].astype(BF16), b_ref[...].astype(BF16), dims)
        if nk == 1:
            finish(prod + refs[2][...].astype(F32) if add is not None else prod, extra_refs, o_refs)
            return
        kk = pl.program_id(2)

        @pl.when(kk == 0)
        def _():
            acc_ref[...] = refs[2][...].astype(F32) if add is not None else jnp.zeros_like(acc_ref)

        acc_ref[...] += prod

        @pl.when(kk == nk - 1)
        def _():
            finish(acc_ref[...], extra_refs, o_refs)

    a_spec = pl.BlockSpec((tk, tm), lambda i, j, kk: (kk, i)) if mode == "tn" else pl.BlockSpec((tm, tk), lambda i, j, kk: (i, kk))
    b_spec = pl.BlockSpec((tn, tk), lambda i, j, kk: (j, kk)) if mode == "nt" else pl.BlockSpec((tk, tn), lambda i, j, kk: (kk, j))
    o_spec = pl.BlockSpec((tm, tn), lambda i, j, kk: (i, j))
    if into is None:
        out_specs, out_shape = [o_spec] * len(out_dtypes), [jax.ShapeDtypeStruct((m, n), dt) for dt in out_dtypes]
    else:
        shape, _, place = into
        out_specs = [pl.BlockSpec((None, None, tm, tn), lambda i, j, kk: (*place(i, j), 0, 0))]
        out_shape = [jax.ShapeDtypeStruct(shape, out_dtype)]
    res = pl.pallas_call(
        body,
        name=name,
        grid=(m // tm, n // tn, nk),
        in_specs=[a_spec, b_spec] + [o_spec] * len(tiles) + [pl.BlockSpec(memory_space=pl.ANY)] * len(carried),
        out_specs=out_specs,
        out_shape=out_shape,
        input_output_aliases={n_in: 0} if carried else {},
        scratch_shapes=[pltpu.VMEM((tm, tn), F32)],
        compiler_params=_params("parallel", "parallel", "arbitrary"),
    )(a, b, *tiles, *carried)
    return res if epilogue is not None else res[0]


def _row_specs(rows, tm):
    return [pl.BlockSpec((tm, w), lambda i, cb=cb: (i, cb)) for _, w, cb in rows]


def _full_spec(p):
    return pl.BlockSpec(p.shape, lambda i: (0,) * p.ndim)


ROW_TILE = 512


def _rowwise(name, fn, rows, params, outs, tm=ROW_TILE, gather=()):
    t = rows[0][0].shape[0]
    tm = min(tm, t)
    steps = t // tm
    nr, npar, nout, ng = len(rows), len(params), len(outs), len(gather)

    def body(*refs):
        ins = [r[...].astype(F32) for r in refs[:nr]]
        ps = [p[...] for p in refs[nr:nr + npar]]
        shard_refs = refs[nr + npar:nr + npar + ng]
        o_refs = refs[nr + npar + ng:nr + npar + ng + nout]
        all_refs, sems = refs[nr + npar + ng + nout:nr + npar + 2 * ng + nout], refs[nr + npar + 2 * ng + nout:]
        if ng:
            @pl.when(pl.program_id(0) == 0)
            def _():
                for cp in _gather_sends(shard_refs, all_refs, *sems[:2]):
                    cp.start()

        res = fn(*ins, *ps)
        for o_ref, r in zip(o_refs, res):
            o_ref[...] = r.astype(o_ref.dtype)

        if ng:
            @pl.when(pl.program_id(0) == steps - 1)
            def _():
                _gather_finish(shard_refs, all_refs, *sems)

    return pl.pallas_call(
        body,
        name=name,
        grid=(steps,),
        in_specs=_row_specs(rows, tm) + [_full_spec(p) for p in params] + [_HBM] * ng,
        out_specs=[pl.BlockSpec((tm, w), lambda i: (i, 0)) for w, _ in outs] + [_HBM] * ng,
        out_shape=[jax.ShapeDtypeStruct((t, w), dt) for w, dt in outs] + [jax.ShapeDtypeStruct((N_CHIPS,) + s.shape, s.dtype) for s in gather],
        scratch_shapes=[pltpu.SemaphoreType.DMA((3 * ng,))] * (4 if ng else 0),
        compiler_params=_params("arbitrary" if ng else "parallel"),
    )(*[r[0] for r in rows], *params, *gather)


def _add_rows(name, terms, n_rows, out_dtypes, tm):
    cols = terms[0][0].shape[1]
    firsts = jnp.stack([jnp.asarray(first, jnp.int32) for _, first in terms])

    def body(firsts_ref, *refs):
        acc = refs[0][...].astype(F32)
        for r in refs[1:len(terms)]:
            acc = acc + r[...].astype(F32)
        for o_ref in refs[len(terms):]:
            o_ref[...] = acc.astype(o_ref.dtype)

    return pl.pallas_call(
        body,
        name=name,
        grid_spec=pltpu.PrefetchScalarGridSpec(
            num_scalar_prefetch=1,
            grid=(n_rows // tm,),
            in_specs=[pl.BlockSpec((tm, cols), lambda i, firsts_ref, k=k: (firsts_ref[k] + i, 0)) for k in range(len(terms))],
            out_specs=[pl.BlockSpec((tm, cols), lambda i, firsts_ref: (i, 0)) for _ in out_dtypes],
        ),
        out_shape=[jax.ShapeDtypeStruct((n_rows, cols), dt) for dt in out_dtypes],
        compiler_params=_params("parallel"),
    )(firsts, *[a for a, _ in terms])


def _rowwise_bwd(name, fn, rows, params, cots, grad_dtypes, tm=ROW_TILE, partials=()):
    t = rows[0][0].shape[0]
    tm = min(tm, t)
    steps = t // tm
    nr, npar, nc, nsc = len(rows), len(params), len(cots), len(partials)
    want = [j for j, dt in enumerate(grad_dtypes) if dt is not None]
    widths = [rows[j][1] for j in want]
    n_row_outs = len(want)
    n_in = nr + npar + nc

    def body(*refs):
        i = pl.program_id(0)
        ins = [r[...].astype(F32) for r in refs[:nr]]
        ps = [p[...] for p in refs[nr:nr + npar]]
        cs = tuple(c[...].astype(F32) for c in refs[nr + npar:n_in])
        p_refs = refs[n_in:n_in + nsc]
        outs = refs[n_in + nsc:n_in + nsc + n_row_outs + npar]
        from_refs, sems = refs[n_in + nsc + n_row_outs + npar:n_in + 2 * nsc + n_row_outs + npar], refs[n_in + 2 * nsc + n_row_outs + npar:]
        if nsc:
            @pl.when(i == 0)
            def _():
                for cp in _scatter_copies(p_refs, from_refs, *sems):
                    cp.start()

        _, vjp = jax.vjp(fn, *ins, *ps)
        gs = vjp(cs)
        for o_ref, j in zip(outs, want):
            o_ref[...] = gs[j].astype(o_ref.dtype)
        pg_refs = outs[n_row_outs:]

        @pl.when(i == 0)
        def _():
            for pg in pg_refs:
                pg[...] = jnp.zeros_like(pg)

        for pg, g in zip(pg_refs, gs[nr:]):
            pg[...] += g

        if nsc:
            @pl.when(i == steps - 1)
            def _():
                for cp in _scatter_copies(p_refs, from_refs, *sems):
                    cp.wait()

    row_specs = [pl.BlockSpec((tm, w), lambda i: (i, 0)) for w in widths]
    row_shapes = [jax.ShapeDtypeStruct((t, w), grad_dtypes[j]) for j, w in zip(want, widths)]
    res = pl.pallas_call(
        body,
        name=name,
        grid=(steps,),
        in_specs=_row_specs(rows, tm) + [_full_spec(p) for p in params] + [pl.BlockSpec((tm, c.shape[1]), lambda i: (i, 0)) for c in cots] + [_HBM] * nsc,
        out_specs=row_specs + [_full_spec(p) for p in params] + [_HBM] * nsc,
        out_shape=row_shapes + [jax.ShapeDtypeStruct(p.shape, F32) for p in params] + [jax.ShapeDtypeStruct((3,) + p.shape[1:], p.dtype) for p in partials],
        scratch_shapes=[pltpu.SemaphoreType.DMA((3 * nsc,))] * (2 if nsc else 0),
        compiler_params=_params("arbitrary"),
    )(*[r[0] for r in rows], *params, *cots, *partials)
    return res[:n_row_outs], res[n_row_outs:n_row_outs + npar], res[n_row_outs + npar:]


def _rms(x, g):
    return x * lax.rsqrt(jnp.mean(x * x, axis=-1, keepdims=True) + EPS) * g


def _sigmoid(x):
    return 1.0 / (1.0 + jnp.exp(-x))


def _softplus(x):
    return jnp.maximum(x, 0.0) + jnp.log1p(jnp.exp(-jnp.abs(x)))


def _two_pass(x, m):
    hi = x.astype(BF16)
    lo = (x - hi.astype(F32)).astype(BF16)
    return _dot(hi, m) + _dot(lo, m)


def _head_sum_impl(x):
    sums = [jnp.sum(x[:, h * HEAD_DIM:(h + 1) * HEAD_DIM], axis=-1, keepdims=True) for h in range(HEADS)]
    return jnp.concatenate([jnp.broadcast_to(s, (x.shape[0], HEAD_DIM)) for s in sums], axis=1)


@jax.custom_vjp
def _head_sum(x):
    return _head_sum_impl(x)


_head_sum.defvjp(lambda x: (_head_sum_impl(x), None), lambda _, g: (_head_sum_impl(g),))


def _fn_norm(x, g):
    return (_rms(x, g),)


def _fn_gates(ba, al, dt):
    col = _iota((1, LANES), 1)
    g = jnp.where((col >= HEADS) & (col < 2 * HEADS), -jnp.exp(al) * _softplus(ba + dt), 0.0)
    rows = ba.shape[0]
    r, c = _iota((rows, rows), 0), _iota((rows, rows), 1)
    same = (r >> CHUNK_SHIFT) == (c >> CHUNK_SHIFT)
    gc = _dot(jnp.where(same & (r >= c), 1.0, 0.0), g, precision=HI)
    gtot = _dot(jnp.where(same, 1.0, 0.0), g, precision=HI)
    return _sigmoid(ba), gc, gtot


def _fn_post_q(c):
    s = c * _sigmoid(c)
    return (s * lax.rsqrt(_head_sum(s * s) + EPS) * (HEAD_DIM ** -0.5),)


def _fn_post_k(c):
    s = c * _sigmoid(c)
    return (s * lax.rsqrt(_head_sum(s * s) + EPS),)


def _fn_post_v(c):
    return (c * _sigmoid(c),)


def _fn_post(cq, ck, cv):
    return _fn_post_q(cq) + _fn_post_k(ck) + _fn_post_v(cv)


def _fn_outnorm(o, gate, og):
    y = o * lax.rsqrt(_head_sum(o * o) * (1.0 / HEAD_DIM) + EPS) * og
    return (y * (gate * _sigmoid(gate)),)


def _fn_res_norm(x, m, gp, gn):
    x1 = x + _rms(m, gp)
    return x1, _rms(x1, gn)


def _fn_res_norm2(x, m, gp, ga, gb):
    x1 = x + _rms(m, gp)
    return x1, _rms(x1, ga), _rms(x1, gb)


def _relu2_of(u):
    r = jnp.maximum(u, 0.0)
    return (r * r,)


def _relu2_cotangent(da, a):
    return (da * (2.0 * jnp.sqrt(a.astype(F32))),)


def _loss_call(x3, d1, tgt, g, tm=ROW_TILE):
    t, d = x3.shape
    tm = min(tm, t)

    def body(x_ref, d_ref, t_ref, g_ref, loss_ref, dx_ref, dd_ref, dg_ref):
        i = pl.program_id(0)
        y, vjp = jax.vjp(lambda x, dd, gg: x + _rms(dd, gg), x_ref[...], d_ref[...], g_ref[...])
        err = y - t_ref[...]
        lrow = 0.5 * jnp.mean(err * err, axis=-1, keepdims=True)
        dx, dd, dg = vjp(err * (1.0 / d))
        dx_ref[...] = dx
        dd_ref[...] = dd.astype(dd_ref.dtype)

        @pl.when(i == 0)
        def _():
            loss_ref[...] = jnp.zeros_like(loss_ref)
            dg_ref[...] = jnp.zeros_like(dg_ref)

        loss_ref[...] += jnp.broadcast_to(jnp.sum(lrow, axis=0, keepdims=True), loss_ref.shape)
        dg_ref[...] += dg

    row = pl.BlockSpec((tm, d), lambda i: (i, 0))
    return pl.pallas_call(
        body,
        name="loss_head",
        grid=(t // tm,),
        in_specs=[row, row, row, _full_spec(g)],
        out_specs=[pl.BlockSpec((8, LANES), lambda i: (0, 0)), row, row, _full_spec(g)],
        out_shape=[jax.ShapeDtypeStruct((8, LANES), F32), jax.ShapeDtypeStruct((t, d), F32), jax.ShapeDtypeStruct((t, d), BF16), jax.ShapeDtypeStruct(g.shape, F32)],
        compiler_params=_params("arbitrary"),
    )(x3, d1, tgt, g)


HALO = 8


def _conv_fwd(qkvg, conv_w, shards, tm=256):
    t = qkvg.shape[0]
    tm = min(tm, t)
    steps = t // tm
    wide = QKV // 3
    n = len(shards)

    def body(*refs):
        cur_ref, prev_ref, w_ref = refs[:3]
        shard_refs = refs[3:3 + n]
        o_ref, q_ref, k_ref, v_ref = refs[3 + n:7 + n]
        all_refs = refs[7 + n:7 + 2 * n]
        buf, sems = refs[7 + 2 * n], refs[8 + 2 * n:]
        i = pl.program_id(0)

        if n:
            @pl.when(i == 0)
            def _():
                for cp in _gather_sends(shard_refs, all_refs, *sems[:2]):
                    cp.start()

        buf[0:HALO, :] = jnp.where(i > 0, prev_ref[...], 0.0)
        buf[HALO:, :] = cur_ref[...]
        acc = buf[pl.ds(HALO - CONV_K + 1, tm), :] * w_ref[pl.ds(0, 1), :]
        for j in range(1, CONV_K):
            acc = acc + buf[pl.ds(HALO - CONV_K + 1 + j, tm), :] * w_ref[pl.ds(j, 1), :]
        o_ref[...] = acc
        (q_ref[...], k_ref[...], v_ref[...]) = _fn_post(acc[:, 0:wide], acc[:, wide:2 * wide], acc[:, 2 * wide:])

        if n:
            @pl.when(i == steps - 1)
            def _():
                _gather_finish(shard_refs, all_refs, *sems)

    part = pl.BlockSpec((tm, wide), lambda i: (i, 0))
    res = pl.pallas_call(
        body,
        name="conv_fwd",
        grid=(steps,),
        in_specs=[
            pl.BlockSpec((tm, QKV), lambda i: (i, 0)),
            pl.BlockSpec((HALO, QKV), lambda i: (jnp.maximum(i * (tm // HALO) - 1, 0), 0)),
            pl.BlockSpec((CONV_K, QKV), lambda i: (0, 0)),
        ] + [_HBM] * n,
        out_specs=[pl.BlockSpec((tm, QKV), lambda i: (i, 0)), part, part, part] + [_HBM] * n,
        out_shape=[jax.ShapeDtypeStruct((t, QKV), F32)] + [jax.ShapeDtypeStruct((t, wide), F32)] * 3
        + [jax.ShapeDtypeStruct((N_CHIPS,) + s.shape, s.dtype) for s in shards],
        scratch_shapes=[pltpu.VMEM((tm + HALO, QKV), F32)] + [pltpu.SemaphoreType.DMA((3 * n,))] * (4 if n else 0),
        compiler_params=_params("arbitrary"),
    )(qkvg, qkvg, conv_w, *shards)
    return res[:4], res[4:]


def _conv_bwd(conv, dqkv, dgate, qkvg, conv_w, tm=256):
    t = conv.shape[0]
    tm = min(tm, t)
    n = t // tm
    wg = dgate.shape[1]
    wide = QKV // 3

    def conv_cotangent(c_ref, g_refs):
        parts = [c_ref[:, j * wide:(j + 1) * wide] for j in range(3)]
        _, vjp = jax.vjp(_fn_post, *parts)
        return vjp(tuple(g[...] for g in g_refs))

    def body(c_ref, cn_ref, dq_ref, dk_ref, dv_ref, dqn_ref, dkn_ref, dvn_ref, dgate_ref, x_ref, xp_ref, w_ref, dx_ref, dw_ref, bufd, bufx):
        i = pl.program_id(0)
        for j, (cur, nxt) in enumerate(zip(conv_cotangent(c_ref, (dq_ref, dk_ref, dv_ref)), conv_cotangent(cn_ref, (dqn_ref, dkn_ref, dvn_ref)))):
            bufd[0:tm, j * wide:(j + 1) * wide] = cur
            bufd[tm:, j * wide:(j + 1) * wide] = jnp.where(i < n - 1, nxt, 0.0)
        bufx[0:HALO, :] = jnp.where(i > 0, xp_ref[...], 0.0)
        bufx[HALO:, :] = x_ref[...]

        @pl.when(i == 0)
        def _():
            dw_ref[...] = jnp.zeros_like(dw_ref)

        dcv = bufd[0:tm, :]
        acc = bufd[pl.ds(CONV_K - 1, tm), :] * w_ref[pl.ds(0, 1), :]
        for j in range(1, CONV_K):
            acc = acc + bufd[pl.ds(CONV_K - 1 - j, tm), :] * w_ref[pl.ds(j, 1), :]
        dx_ref[:, 0:QKV] = acc.astype(dx_ref.dtype)
        dx_ref[:, QKV:] = dgate_ref[...].astype(dx_ref.dtype)
        for j in range(CONV_K):
            dw_ref[pl.ds(j, 1), :] += jnp.sum(dcv * bufx[pl.ds(HALO - CONV_K + 1 + j, tm), :], axis=0, keepdims=True)

    def cur(width):
        return pl.BlockSpec((tm, width), lambda i: (i, 0))

    def nxt(width):
        return pl.BlockSpec((HALO, width), lambda i: (jnp.minimum((i + 1) * (tm // HALO), t // HALO - 1), 0))

    return pl.pallas_call(
        body,
        name="conv_bwd",
        grid=(n,),
        in_specs=[cur(QKV), nxt(QKV)] + [cur(wide)] * 3 + [nxt(wide)] * 3 + [
            cur(wg),
            cur(QKV),
            pl.BlockSpec((HALO, QKV), lambda i: (jnp.maximum(i * (tm // HALO) - 1, 0), 0)),
            pl.BlockSpec((CONV_K, QKV), lambda i: (0, 0)),
        ],
        out_specs=[pl.BlockSpec((tm, QKV + wg), lambda i: (i, 0)), pl.BlockSpec((HALO, QKV), lambda i: (0, 0))],
        out_shape=[jax.ShapeDtypeStruct((t, QKV + wg), BF16), jax.ShapeDtypeStruct((HALO, QKV), F32)],
        scratch_shapes=[pltpu.VMEM((tm + HALO, QKV), F32), pltpu.VMEM((tm + HALO, QKV), F32)],
        compiler_params=_params("arbitrary"),
    )(conv, conv, *dqkv, *dqkv, dgate, qkvg, qkvg, conv_w)


PREP_CHUNKS = 32
PREP_BWD_CHUNKS = 4
SCAN_CHUNKS = 8


def _hi_lo(x):
    hi = x.astype(BF16)
    return hi, (x - hi.astype(F32)).astype(BF16)


def _mm3(a, b, dims=NN):
    (ah, al), (bh, bl) = _hi_lo(a), _hi_lo(b)
    return _dot(ah, bh, dims) + (_dot(ah, bl, dims) + _dot(al, bh, dims))


def _neumann(lowers):
    c = lowers[0].shape[0]
    eye = jnp.where(_iota((c, c), 0) == _iota((c, c), 1), 1.0, 0.0)
    ps = [-low for low in lowers]
    tmats = [eye + p for p in ps]
    for _ in range(CHUNK_SHIFT - 1):
        ps = [_mm3(p, p) for p in ps]
        tmats = [t + _mm3(t, p) for t, p in zip(tmats, ps)]
    return tuple(tmats)


def _inv_cotangents(tmats, dts):
    half = [_mm3(t, dt, TN) for t, dt in zip(tmats, dts)]
    return tuple(-_mm3(hf, t, NT) for hf, t in zip(half, tmats))


@jax.custom_vjp
def _tri_inv(lowers):
    return _neumann(lowers)


def _tri_inv_fwd(lowers):
    tmats = _neumann(lowers)
    return tmats, tmats


_tri_inv.defvjp(_tri_inv_fwd, lambda tmats, dts: (_inv_cotangents(tmats, dts),))


@jax.custom_vjp
def _tri_inv_known(lowers, tmats):
    return tmats


_tri_inv_known.defvjp(lambda lowers, tmats: (tmats, tmats),
                      lambda tmats, dts: (_inv_cotangents(tmats, dts), tuple(jnp.zeros_like(t) for t in tmats)))


def _prep_chunks(qs, ks, vs, bs, gcs, gts, gcrs, tmats=None):
    c = CHUNK
    r, col = _iota((c, c), 0), _iota((c, c), 1)
    incl, strict = r >= col, r > col
    decays = [jnp.where(incl, jnp.exp(jnp.where(incl, gc - gcr, 0.0)), 0.0) for gc, gcr in zip(gcs, gcrs)]
    kbs = [k * b for k, b in zip(ks, bs)]
    kbfs = [k.astype(BF16) for k in ks]
    lowers = tuple(jnp.where(strict, _dot(kb.astype(BF16), kbf, NT) * decay, 0.0) for kb, kbf, decay in zip(kbs, kbfs, decays))
    tmats = _tri_inv(lowers) if tmats is None else _tri_inv_known(lowers, tuple(tmats))
    outs = []
    for q, k, v, b, gc, gt, kb, kbf, decay, tmat in zip(qs, ks, vs, bs, gcs, gts, kbs, kbfs, decays, tmats):
        tb = tmat.astype(BF16)
        egc = jnp.exp(gc)
        w = _dot(tb, (kb * egc).astype(BF16))
        u = _dot(tb, (v * b).astype(BF16))
        attn = _dot(q.astype(BF16), kbf, NT) * decay
        gl = jnp.broadcast_to(jnp.exp(jnp.mean(gt.reshape(c // 8, 8, 1), axis=0)), (8, HEAD_DIM))
        outs.append((w, u, q * egc, k * jnp.exp(gt - gc), attn, gl))
    return tuple(outs), tmats


def _prep_specs(rows, gch):
    head = pl.BlockSpec((rows, HEAD_DIM), lambda n, h: (n, h))
    gates = pl.BlockSpec((rows, LANES), lambda n, h: (n, 0))
    gcrow = pl.BlockSpec((1, gch, 1, CHUNK), lambda n, h: (h, n, 0, 0))
    square = pl.BlockSpec((1, rows, CHUNK), lambda n, h: (h, n, 0))
    gl = pl.BlockSpec((1, gch * 8, HEAD_DIM), lambda n, h: (h, n, 0))
    return head, gates, gcrow, square, gl


def _pick_lane(ref, sl, lane):
    return jnp.sum(jnp.where(_iota((1, LANES), 1) == lane, ref[sl, :], 0.0), axis=1, keepdims=True)


def _prep_inputs(q_ref, k_ref, v_ref, b_ref, gc_ref, gt_ref, gcr_ref, sls, h):
    return ([q_ref[sl, :] for sl in sls], [k_ref[sl, :] for sl in sls], [v_ref[sl, :] for sl in sls],
            [_pick_lane(b_ref, sl, h) for sl in sls], [_pick_lane(gc_ref, sl, h + HEADS) for sl in sls],
            [_pick_lane(gt_ref, sl, h + HEADS) for sl in sls], [gcr_ref[0, c] for c in range(len(sls))])


def _gdn_prep(q, k, v, beta, gc, gt, gcr, shards=()):
    t = q.shape[0]
    gch = min(PREP_CHUNKS, t // CHUNK)
    rows = gch * CHUNK
    steps = t // rows
    n = len(shards)

    def body(*refs):
        q_ref, k_ref, v_ref, b_ref, gc_ref, gt_ref, gcr_ref = refs[:7]
        shard_refs = refs[7:7 + n]
        w_ref, u_ref, qg_ref, kg_ref, at_ref, gl_ref, tm_ref = refs[7 + n:14 + n]
        all_refs, sems = refs[14 + n:14 + 2 * n], refs[14 + 2 * n:]
        h = pl.program_id(1)

        if n:
            @pl.when(jnp.logical_and(pl.program_id(0) == 0, h == 0))
            def _():
                for cp in _gather_sends(shard_refs, all_refs, *sems[:2]):
                    cp.start()

        sls = [pl.ds(c * CHUNK, CHUNK) for c in range(gch)]
        outs, tmats = _prep_chunks(*_prep_inputs(q_ref, k_ref, v_ref, b_ref, gc_ref, gt_ref, gcr_ref, sls, h))
        for c, (sl, (w, u, qg, kg, attn, gl), tmat) in enumerate(zip(sls, outs, tmats)):
            w_ref[sl, :] = w.astype(BF16)
            u_ref[sl, :] = u
            qg_ref[sl, :] = qg.astype(BF16)
            kg_ref[sl, :] = kg.astype(BF16)
            at_ref[0, sl, :] = attn.astype(BF16)
            gl_ref[0, pl.ds(c * 8, 8), :] = gl
            tm_ref[0, sl, :] = tmat

        if n:
            @pl.when(jnp.logical_and(pl.program_id(0) == steps - 1, h == HEADS - 1))
            def _():
                _gather_finish(shard_refs, all_refs, *sems)

    hb, col, gcrow, square, glb = _prep_specs(rows, gch)
    wide = HEADS * HEAD_DIM
    res = pl.pallas_call(
        body,
        name="gdn_prep",
        grid=(steps, HEADS),
        in_specs=[hb, hb, hb, col, col, col, gcrow] + [_HBM] * n,
        out_specs=[hb, hb, hb, hb, square, glb, square] + [_HBM] * n,
        out_shape=[
            jax.ShapeDtypeStruct((t, wide), BF16),
            jax.ShapeDtypeStruct((t, wide), F32),
            jax.ShapeDtypeStruct((t, wide), BF16),
            jax.ShapeDtypeStruct((t, wide), BF16),
            jax.ShapeDtypeStruct((HEADS, t, CHUNK), BF16),
            jax.ShapeDtypeStruct((HEADS, t // CHUNK * 8, HEAD_DIM), F32),
            jax.ShapeDtypeStruct((HEADS, t, CHUNK), F32),
        ] + [jax.ShapeDtypeStruct((N_CHIPS,) + s.shape, s.dtype) for s in shards],
        scratch_shapes=[pltpu.SemaphoreType.DMA((3 * n,))] * (4 if n else 0),
        compiler_params=_params("arbitrary", "arbitrary") if n else _params("parallel", "parallel"),
    )(q, k, v, beta, gc, gt, gcr, *shards)
    return res[:7], res[7:]


def _gdn_prep_bwd(q, k, v, beta, gc, gt, gcr, tmat, dw, du, dqg, dkg, dattn, dgl, partials=()):
    t = q.shape[0]
    gch = min(PREP_BWD_CHUNKS, t // CHUNK)
    rows = gch * CHUNK
    steps = t // rows
    n_sc = len(partials)

    def body(*refs):
        (q_ref, k_ref, v_ref, b_ref, gc_ref, gt_ref, gcr_ref, tm_ref, dw_ref, du_ref, dqg_ref, dkg_ref, dat_ref, dgl_ref) = refs[:14]
        p_refs = refs[14:14 + n_sc]
        dq_ref, dk_ref, dv_ref, db_ref, dgc_ref, dgt_ref, dgcr_ref = refs[14 + n_sc:21 + n_sc]
        from_refs, sems = refs[21 + n_sc:21 + 2 * n_sc], refs[21 + 2 * n_sc:]
        h = pl.program_id(1)
        lane = _iota((1, LANES), 1)

        if n_sc:
            @pl.when(jnp.logical_and(pl.program_id(0) == 0, h == 0))
            def _():
                for cp in _scatter_copies(p_refs, from_refs, *sems):
                    cp.start()

        @pl.when(h == 0)
        def _():
            db_ref[...] = jnp.zeros_like(db_ref)
            dgc_ref[...] = jnp.zeros_like(dgc_ref)
            dgt_ref[...] = jnp.zeros_like(dgt_ref)

        sls = [pl.ds(c * CHUNK, CHUNK) for c in range(gch)]
        known = [tm_ref[0, sl, :] for sl in sls]
        _, vjp = jax.vjp(lambda *a: _prep_chunks(*a, tmats=known)[0], *_prep_inputs(q_ref, k_ref, v_ref, b_ref, gc_ref, gt_ref, gcr_ref, sls, h))
        cots = tuple((dw_ref[sl, :], du_ref[sl, :], dqg_ref[sl, :], dkg_ref[sl, :], dat_ref[0, sl, :], dgl_ref[0, pl.ds(c * 8, 8), :]) for c, sl in enumerate(sls))
        dqs, dks, dvs, dbs, dgcs, dgts, dgcrs = vjp(cots)
        for c, sl in enumerate(sls):
            dq_ref[sl, :] = dqs[c]
            dk_ref[sl, :] = dks[c]
            dv_ref[sl, :] = dvs[c]
            db_ref[sl, :] += jnp.where(lane == h, dbs[c], 0.0)
            dgc_ref[sl, :] += jnp.where(lane == h + HEADS, dgcs[c], 0.0)
            dgt_ref[sl, :] += jnp.where(lane == h + HEADS, dgts[c], 0.0)
            dgcr_ref[0, c] = dgcrs[c]

        if n_sc:
            @pl.when(jnp.logical_and(pl.program_id(0) == steps - 1, h == HEADS - 1))
            def _():
                for cp in _scatter_copies(p_refs, from_refs, *sems):
                    cp.wait()

    hb, col, gcrow, square, glb = _prep_specs(rows, gch)
    wide = HEADS * HEAD_DIM
    res = pl.pallas_call(
        body,
        name="gdn_prep_bwd",
        grid=(steps, HEADS),
        in_specs=[hb, hb, hb, col, col, col, gcrow, square, hb, hb, hb, hb, square, glb] + [_HBM] * n_sc,
        out_specs=[hb, hb, hb, col, col, col, gcrow] + [_HBM] * n_sc,
        out_shape=[jax.ShapeDtypeStruct((t, wide), F32)] * 3 + [jax.ShapeDtypeStruct((t, LANES), F32)] * 3 + [jax.ShapeDtypeStruct((HEADS, t // CHUNK, 1, CHUNK), F32)]
        + [jax.ShapeDtypeStruct((3,) + p.shape[1:], p.dtype) for p in partials],
        scratch_shapes=[pltpu.SemaphoreType.DMA((3 * n_sc,))] * (2 if n_sc else 0),
        compiler_params=_params("arbitrary", "arbitrary"),
    )(q, k, v, beta, gc, gt, gcr, tmat, dw, du, dqg, dkg, dattn, dgl, *partials)
    return res[:7], res[7:]


def _gdn_scan(w, u, qg, kg, attn, gl):
    t = w.shape[0]
    n = t // CHUNK
    nch = min(SCAN_CHUNKS, n)
    wide = HEADS * HEAD_DIM

    def body(w_ref, u_ref, qg_ref, kg_ref, at_ref, gl_ref, o_ref, st_ref, s_ref):
        @pl.when(pl.program_id(0) == 0)
        def _():
            s_ref[...] = jnp.zeros_like(s_ref)

        heads = range(HEADS)
        cols = [pl.ds(h * HEAD_DIM, HEAD_DIM) for h in heads]
        for c in range(nch):
            rows, gl_rows = pl.ds(c * CHUNK, CHUNK), pl.ds(c * 8, 8)
            ss = [s_ref[h] for h in heads]
            sbs = [s.astype(BF16) for s in ss]
            vbs = [(u_ref[rows, hs] - _dot(w_ref[rows, hs], sb)).astype(BF16) for hs, sb in zip(cols, sbs)]
            outs = [_dot(qg_ref[rows, hs], sb) + _dot(at_ref[h, rows, :], vb) for h, hs, sb, vb in zip(heads, cols, sbs, vbs)]
            new = [s * jnp.tile(gl_ref[h, gl_rows, :], (HEAD_DIM // 8, 1)) + _dot(kg_ref[rows, hs], vb, TN) for h, hs, s, vb in zip(heads, cols, ss, vbs)]
            for h, hs in zip(heads, cols):
                st_ref[c, h] = ss[h]
                o_ref[rows, hs] = outs[h]
                s_ref[h] = new[h]

    row = pl.BlockSpec((nch * CHUNK, wide), lambda i: (i, 0))
    return pl.pallas_call(
        body,
        name="gdn_scan",
        grid=(n // nch,),
        in_specs=[row, row, row, row, pl.BlockSpec((HEADS, nch * CHUNK, CHUNK), lambda i: (0, i, 0)), pl.BlockSpec((HEADS, nch * 8, HEAD_DIM), lambda i: (0, i, 0))],
        out_specs=[row, pl.BlockSpec((nch, HEADS, HEAD_DIM, HEAD_DIM), lambda i: (i, 0, 0, 0))],
        out_shape=[jax.ShapeDtypeStruct((t, wide), F32), jax.ShapeDtypeStruct((n, HEADS, HEAD_DIM, HEAD_DIM), F32)],
        scratch_shapes=[pltpu.VMEM((HEADS, HEAD_DIM, HEAD_DIM), F32)],
        compiler_params=_params("arbitrary"),
    )(w, u, qg, kg, attn, gl)


def _gdn_scan_bwd(w, u, qg, kg, attn, gl, states, do):
    t = w.shape[0]
    n = t // CHUNK
    nch = min(SCAN_CHUNKS, n)
    steps = n // nch
    wide = HEADS * HEAD_DIM

    def body(w_ref, u_ref, qg_ref, kg_ref, at_ref, gl_ref, st_ref, do_ref, dw_ref, du_ref, dqg_ref, dkg_ref, dat_ref, dgl_ref, ds_ref):
        @pl.when(pl.program_id(0) == 0)
        def _():
            ds_ref[...] = jnp.zeros_like(ds_ref)

        heads = range(HEADS)
        cols = [pl.ds(h * HEAD_DIM, HEAD_DIM) for h in heads]
        for c in reversed(range(nch)):
            rows, gl_rows = pl.ds(c * CHUNK, CHUNK), pl.ds(c * 8, 8)
            ss = [st_ref[c, h] for h in heads]
            sbs = [s.astype(BF16) for s in ss]
            dsns = [ds_ref[h] for h in heads]
            dsbs = [d.astype(BF16) for d in dsns]
            dobs = [do_ref[rows, hs].astype(BF16) for hs in cols]
            vbs = [(u_ref[rows, hs] - _dot(w_ref[rows, hs], sb)).astype(BF16) for hs, sb in zip(cols, sbs)]
            dvns = [_dot(at_ref[h, rows, :], dob, TN) + _dot(kg_ref[rows, hs], dsb) for h, hs, dob, dsb in zip(heads, cols, dobs, dsbs)]
            dvbs = [d.astype(BF16) for d in dvns]
            for h, hs in zip(heads, cols):
                dat_ref[h, rows, :] = _dot(dobs[h], vbs[h], NT)
                dqg_ref[rows, hs] = _dot(dobs[h], sbs[h], NT)
                dkg_ref[rows, hs] = _dot(vbs[h], dsbs[h], NT)
                du_ref[rows, hs] = dvns[h]
                dw_ref[rows, hs] = -_dot(dvbs[h], sbs[h], NT)
                dgl_ref[h, gl_rows, :] = jnp.sum((dsns[h] * ss[h]).reshape(HEAD_DIM // 8, 8, HEAD_DIM), axis=0)
            new = [dsn * jnp.tile(gl_ref[h, gl_rows, :], (HEAD_DIM // 8, 1)) + _dot(qg_ref[rows, hs], dob, TN) - _dot(w_ref[rows, hs], dvb, TN)
                   for h, hs, dsn, dob, dvb in zip(heads, cols, dsns, dobs, dvbs)]
            for h in heads:
                ds_ref[h] = new[h]

    row = pl.BlockSpec((nch * CHUNK, wide), lambda i: (steps - 1 - i, 0))
    at = pl.BlockSpec((HEADS, nch * CHUNK, CHUNK), lambda i: (0, steps - 1 - i, 0))
    glb = pl.BlockSpec((HEADS, nch * 8, HEAD_DIM), lambda i: (0, steps - 1 - i, 0))
    return pl.pallas_call(
        body,
        name="gdn_scan_bwd",
        grid=(steps,),
        in_specs=[row, row, row, row, at, glb, pl.BlockSpec((nch, HEADS, HEAD_DIM, HEAD_DIM), lambda i: (steps - 1 - i, 0, 0, 0)), row],
        out_specs=[row, row, row, row, at, glb],
        out_shape=[jax.ShapeDtypeStruct((t, wide), F32)] * 4 + [jax.ShapeDtypeStruct((HEADS, t, CHUNK), F32), jax.ShapeDtypeStruct((HEADS, n * 8, HEAD_DIM), F32)],
        scratch_shapes=[pltpu.VMEM((HEADS, HEAD_DIM, HEAD_DIM), F32)],
        compiler_params=_params("arbitrary"),
    )(w, u, qg, kg, attn, gl, states, do)


SB_Q = 512
SB_K = 256
SB_STEP = 1
SB_DEAD = -105.0


def _sb_scores(q, k):
    z = _dot(q, k, NT) * (HEAD_DIM ** -0.5)
    lb = jnp.minimum(z, 0.0) - jnp.log(1.0 + jnp.exp(-jnp.abs(z)))
    return lb, lb - z


def _tri(n, rel):
    return jnp.where(rel(_iota((n, n), 0), _iota((n, n), 1)), 1.0, 0.0).astype(BF16)


def _lanes(col):
    return jnp.broadcast_to(col, (col.shape[0], LANES))


def _sb_fwd(q, k, v):
    t = q.shape[0]
    bq, bk = min(SB_Q, t), min(SB_K, t)
    nsub, rep = bq // bk, bk // LANES
    nstep = min(SB_STEP, nsub)
    steps_per_tile = nsub // nstep

    def body(q_ref, k_ref, v_ref, o_ref, rt_ref, first_ref):
        h = pl.program_id(0)
        i = pl.program_id(1)
        o_ref[...] = jnp.zeros_like(o_ref)
        rt_ref[...] = jnp.zeros_like(rt_ref)
        after = _tri(bk, lambda r, c: r > c)

        def block(j, r0, diag):
            st = pl.multiple_of(j * bk, bk)
            kv, vv = k_ref[pl.ds(st, bk), :], v_ref[pl.ds(st, bk), :]
            lb, l1m = _sb_scores(q_ref[r0:, :], kv)
            if diag:
                mask = _iota((bq - r0, bk), 1) + j * bk < _iota((bq - r0, bk), 0) + (r0 + i * bq)
                l1m = jnp.where(mask, l1m, 0.0)
            sums = _two_pass(l1m, after)
            run = rt_ref[r0:, :]
            a = jnp.exp(lb + jnp.tile(run, (1, rep)) + sums)
            if diag:
                a = jnp.where(mask, a, 0.0)
            o_ref[r0:, :] += _dot(a.astype(BF16), vv)
            rt_ref[r0:, :] = run + _lanes(sums[:, 0:1] + l1m[:, 0:1])

        for s in reversed(range(nsub)):
            block(i * nsub + s, s * bk, True)

        def alive(carry):
            u, highest = carry
            return jnp.logical_and(u >= 0, highest > SB_DEAD)

        def step(carry):
            u, _ = carry
            for s in reversed(range(nstep)):
                block(u * nstep + s, 0, False)
            return u - 1, jnp.max(rt_ref[...])

        u_end, _ = lax.while_loop(alive, step, (i * steps_per_tile - 1, jnp.max(rt_ref[...])))
        first_ref[h, i] = u_end + 1

    qb = pl.BlockSpec((bq, HEAD_DIM), lambda h, i: (i, h))
    full = pl.BlockSpec((t, HEAD_DIM), lambda h, i: (0, h))
    return pl.pallas_call(
        body,
        name="sb_fwd",
        grid=(HEADS, t // bq),
        in_specs=[qb, full, full],
        out_specs=[qb, qb, pl.BlockSpec(memory_space=pltpu.SMEM)],
        out_shape=[jax.ShapeDtypeStruct(q.shape, F32), jax.ShapeDtypeStruct(q.shape, F32), jax.ShapeDtypeStruct((HEADS, t // bq), jnp.int32)],
        compiler_params=_params("arbitrary", "arbitrary"),
    )(q, k, v)


def _sb_bwd(q, k, v, rt, first, do):
    t = q.shape[0]
    bq, bk = min(SB_Q, t), min(SB_K, t)
    nsub, rep = bq // bk, bk // LANES
    nstep = min(SB_STEP, nsub)
    steps_per_tile = nsub // nstep
    scale = HEAD_DIM ** -0.5

    def body(first_ref, q_ref, k_ref, v_ref, rt_ref, do_ref, dq_ref, dk_ref, dv_ref, left_ref, pg_ref):
        h = pl.program_id(0)
        i = pl.program_id(1)

        @pl.when(i == 0)
        def _():
            dk_ref[...] = jnp.zeros_like(dk_ref)
            dv_ref[...] = jnp.zeros_like(dv_ref)

        dq_ref[...] = jnp.zeros_like(dq_ref)
        left_ref[...] = jnp.zeros_like(left_ref)
        pg_ref[...] = jnp.zeros_like(pg_ref)
        upto = _tri(bk, lambda r, c: r <= c)

        def block(j, r0, diag):
            st = pl.multiple_of(j * bk, bk)
            kv, vv = k_ref[pl.ds(st, bk), :], v_ref[pl.ds(st, bk), :]
            qv = q_ref[r0:, :]
            dob = do_ref[r0:, :].astype(BF16)
            lb, l1m = _sb_scores(qv, kv)
            if diag:
                mask = _iota((bq - r0, bk), 1) + j * bk < _iota((bq - r0, bk), 0) + (r0 + i * bq)
                l1m = jnp.where(mask, l1m, 0.0)
            sums = _two_pass(l1m, upto)
            left = left_ref[r0:, :]
            a = jnp.exp(lb + jnp.tile(rt_ref[r0:, :] - left, (1, rep)) - sums)
            if diag:
                a = jnp.where(mask, a, 0.0)
            g = _dot(dob, vv, NT) * a
            dv_ref[pl.ds(st, bk), :] += _dot(a.astype(BF16), dob, TN)
            gsum = _two_pass(g, upto)
            pg = pg_ref[r0:, :]
            dz = g - jnp.exp(lb) * (jnp.tile(pg, (1, rep)) + gsum)
            if diag:
                dz = jnp.where(mask, dz, 0.0)
            dzb = (dz * scale).astype(BF16)
            dk_ref[pl.ds(st, bk), :] += _dot(dzb, qv, TN)
            dq_ref[r0:, :] += _dot(dzb, kv)
            left_ref[r0:, :] = left + _lanes(sums[:, bk - 1:bk])
            pg_ref[r0:, :] = pg + _lanes(gsum[:, bk - 1:bk])

        def step(u, carry):
            for s in range(nstep):
                block(u * nstep + s, 0, False)
            return carry

        lax.fori_loop(first_ref[h, i], i * steps_per_tile, step, 0)
        for s in range(nsub):
            block(i * nsub + s, s * bk, True)

    qb = pl.BlockSpec((bq, HEAD_DIM), lambda h, i: (i, h))
    full = pl.BlockSpec((t, HEAD_DIM), lambda h, i: (0, h))
    return pl.pallas_call(
        body,
        name="sb_bwd",
        grid=(HEADS, t // bq),
        in_specs=[pl.BlockSpec(memory_space=pltpu.SMEM), qb, full, full, qb, qb],
        out_specs=[qb, full, full],
        out_shape=[jax.ShapeDtypeStruct(q.shape, F32)] * 3,
        scratch_shapes=[pltpu.VMEM((bq, LANES), F32), pltpu.VMEM((bq, LANES), F32)],
        compiler_params=_params("arbitrary", "arbitrary"),
    )(first, q, k, v, rt, do)


def _adamw(w, g, m, v, name, tm=ROW_TILE):
    r, c = w.shape
    tm = tm if r % tm == 0 else r

    def body(w_ref, g_ref, m_ref, v_ref, d_ref, nm_ref, nv_ref):
        gv = g_ref[...]
        nm = ADAM_B1 * m_ref[...] + (1.0 - ADAM_B1) * gv
        nv = ADAM_B2 * v_ref[...] + (1.0 - ADAM_B2) * (gv * gv)
        m_hat = nm / (1.0 - ADAM_B1 ** ADAM_STEP)
        v_hat = nv / (1.0 - ADAM_B2 ** ADAM_STEP)
        d_ref[...] = -ADAM_LR * (m_hat / (jnp.sqrt(v_hat) + ADAM_EPS) + ADAM_WD * w_ref[...])
        nm_ref[...] = nm
        nv_ref[...] = nv

    blk = pl.BlockSpec((tm, c), lambda i: (i, 0))
    return pl.pallas_call(
        body,
        name=name,
        grid=(r // tm,),
        in_specs=[blk] * 4,
        out_specs=[blk] * 3,
        out_shape=[jax.ShapeDtypeStruct((r, c), F32)] * 3,
        compiler_params=_params("parallel"),
    )(w, g, m, v)


def _local_step(x, tgt, gains, small, shards, assemble_first, assemble, early_reduce=None, late_reduce=None):
    mix_pre, mix_post, mlp_pre, mlp_post, kv_gain = gains
    a_log, dt_bias, out_gain = small
    t, d = x.shape
    row = lambda a, i=None: a[i:i + 1] if i is not None else a
    al = jnp.zeros((1, LANES), F32).at[:, HEADS:2 * HEADS].set(a_log)
    dtb = jnp.zeros((1, LANES), F32).at[:, HEADS:2 * HEADS].set(dt_bias)
    og = jnp.tile(out_gain, (1, HEADS))
    full = lambda a: (a, a.shape[1], 0)

    h0, *gathered_first = _rowwise("norm_in", _fn_norm, [full(x)], [row(mix_pre, 0)], [(d, BF16)], gather=shards[0])
    w_qkvg, w_ba, conv_w = assemble_first(gathered_first)
    qkvg = _matmul(h0, w_qkvg, "nn", F32, "mm_gdn_in", tk=1024)
    ba = _matmul(h0, w_ba, "nn", F32, "mm_gdn_ba", tk=1024)
    (conv, gq, gk, gv), gathered_conv = _conv_fwd(qkvg, conv_w, shards[1])
    beta, gc, gt = _rowwise("gates", _fn_gates, [full(ba)], [al, dtb], [(LANES, F32)] * 3)
    gcr = jnp.swapaxes(gc[:, HEADS:2 * HEADS], 0, 1).reshape(HEADS, t // CHUNK, 1, CHUNK)
    (pw, pu, pqg, pkg, pattn, pgl, ptm), gathered_prep = _gdn_prep(gq, gk, gv, beta, gc, gt, gcr, shards[2])
    w_out, w_kv, w_q, w_o, w_up, w_down = assemble(gathered_conv, gathered_prep)
    w_qkvg_t, w_up_t = jnp.swapaxes(w_qkvg, -1, -2), jnp.swapaxes(w_up, -1, -2)
    o_gdn, states = _gdn_scan(pw, pu, pqg, pkg, pattn, pgl)
    (on,) = _rowwise("out_norm", _fn_outnorm, [full(o_gdn), (qkvg, d, 3)], [og], [(d, BF16)])
    mix0 = _matmul(on, w_out, "nn", F32, "mm_gdn_out", tk=1024)
    x1, h1 = _rowwise("res_a0", _fn_res_norm, [full(x), full(mix0)], [row(mix_post, 0), row(mlp_pre, 0)], [(d, F32), (d, BF16)])
    (a0,) = _matmul(h1, w_up[0], "nn", (BF16,), "mm_up0", tk=1024, epilogue=_relu2_of)
    d0 = _matmul(a0, w_down[0], "nn", F32, "mm_down0")
    x2, hkv, hq = _rowwise("res_b0", _fn_res_norm2, [full(x1), full(d0)], [row(mlp_post, 0), kv_gain, row(mix_pre, 1)], [(d, F32), (d, BF16), (d, BF16)])
    w_k, w_v = w_kv[:, :d], w_kv[:, d:]
    kp = _matmul(hkv, w_k, "nn", BF16, "mm_k", tk=1024)
    vp = _matmul(hkv, w_v, "nn", BF16, "mm_v", tk=1024)
    qp = _matmul(hq, w_q, "nn", BF16, "mm_q", tk=1024)
    o_sb, rt, sb_first = _sb_fwd(qp, kp, vp)
    mix1 = _matmul(o_sb, w_o, "nn", F32, "mm_sb_out", tk=1024)
    x3, h3 = _rowwise("res_a1", _fn_res_norm, [full(x2), full(mix1)], [row(mix_post, 1), row(mlp_pre, 1)], [(d, F32), (d, BF16)])
    (a1,) = _matmul(h3, w_up[1], "nn", (BF16,), "mm_up1", tk=1024, epilogue=_relu2_of)
    d1 = _matmul(a1, w_down[1], "nn", F32, "mm_down1")

    loss, dx3, dd1, g_mlp_post1 = _loss_call(x3, d1, tgt, row(mlp_post, 1))
    (du1,) = _matmul(dd1, w_down[1], "nt", (BF16,), "mm_down1_dx", epilogue=_relu2_cotangent, extras=[a1])
    up_shape, down_shape = _grad_buffer_shape(_GROUPS[0]), _grad_buffer_shape(_GROUPS[1])
    buf_down = _matmul(a1, dd1, "tn", F32, "mm_down1_dw", into=(down_shape, None, lambda i, j: (1, i)))
    dh3 = _matmul(du1, w_up_t[1], "nn", F32, "mm_up1_dx")
    buf_up = _matmul(h3, du1, "tn", F32, "mm_up1_dw", into=(up_shape, None, lambda i, j: (1, j)))
    (dx2, dmix1), (g_mix_post1, g_mlp_pre1), _ = _rowwise_bwd(
        "res_a1_bwd", _fn_res_norm, [full(x2), full(mix1)], [row(mix_post, 1), row(mlp_pre, 1)], [dx3, dh3], [F32, BF16])
    do_sb = _matmul(dmix1, w_o, "nt", BF16, "mm_sb_out_dx")
    g_o = _matmul(o_sb, dmix1, "tn", F32, "mm_sb_out_dw")
    dqp, dkp, dvp = _sb_bwd(qp, kp, vp, rt, sb_first, do_sb)
    dhq = _matmul(dqp, w_q, "nt", F32, "mm_q_dx")
    g_q = _matmul(hq, dqp, "tn", F32, "mm_q_dw")
    dhkv = _matmul(dvp, w_v, "nt", F32, "mm_v_dx", add=_matmul(dkp, w_k, "nt", F32, "mm_k_dx"))
    g_kv = jnp.concatenate([_matmul(hkv, dkp, "tn", F32, "mm_k_dw"), _matmul(hkv, dvp, "tn", F32, "mm_v_dw")], axis=1)
    (dx1, dd0), (g_mlp_post0, g_kv_gain, g_mix_pre1), _ = _rowwise_bwd(
        "res_b0_bwd", _fn_res_norm2, [full(x1), full(d0)], [row(mlp_post, 0), kv_gain, row(mix_pre, 1)], [dx2, dhkv, dhq], [F32, BF16])
    (du0,) = _matmul(dd0, w_down[0], "nt", (BF16,), "mm_down0_dx", epilogue=_relu2_cotangent, extras=[a0])
    buf_down = _matmul(a0, dd0, "tn", F32, "mm_down0_dw", into=(down_shape, buf_down, lambda i, j: (0, i)))
    dh1 = _matmul(du0, w_up_t[0], "nn", F32, "mm_up0_dx")
    buf_up = _matmul(h1, du0, "tn", F32, "mm_up0_dw", into=(up_shape, buf_up, lambda i, j: (0, j)))
    (dx0, dmix0), (g_mix_post0, g_mlp_pre0), _ = _rowwise_bwd(
        "res_a0_bwd", _fn_res_norm, [full(x), full(mix0)], [row(mix_post, 0), row(mlp_pre, 0)], [dx1, dh1], [F32, BF16])
    don = _matmul(dmix0, w_out, "nt", F32, "mm_gdn_out_dx")
    g_out = _matmul(on, dmix0, "tn", F32, "mm_gdn_out_dw")
    (do_gdn, dgate), (g_og,), _ = _rowwise_bwd("out_norm_bwd", _fn_outnorm, [full(o_gdn), (qkvg, d, 3)], [og], [don], [F32, F32])
    dpw, dpu, dpqg, dpkg, dpattn, dpgl = _gdn_scan_bwd(pw, pu, pqg, pkg, pattn, pgl, states, do_gdn)
    partial, partial_bf16 = [], ()
    if early_reduce is not None:
        partial, partial_bf16 = early_reduce(dict(gdn_w_out=g_out[None], w_kv=g_kv, sb_w_q=g_q[None], sb_w_o=g_o[None]), {0: buf_up, 1: buf_down})
    (dgq, dgk, dgv, dbeta, dgc, dgt, dgcr), from_chips = _gdn_prep_bwd(gq, gk, gv, beta, gc, gt, gcr, ptm, dpw, dpu, dpqg, dpkg, dpattn, dpgl, partial_bf16)
    dgcr_lanes = jnp.pad(jnp.swapaxes(dgcr.reshape(HEADS, t), 0, 1), ((0, 0), (HEADS, LANES - 2 * HEADS)))
    gate_cots = [dbeta, dgc + dgcr_lanes, dgt]
    (dba,), (g_al, g_dtb), _ = _rowwise_bwd("gates_bwd", _fn_gates, [full(ba)], [al, dtb], gate_cots, [BF16])
    dqkvg, g_conv = _conv_bwd(conv, (dgq, dgk, dgv), dgate, qkvg, conv_w)
    dh0b = _matmul(dba, w_ba, "nt", F32, "mm_gdn_ba_dx", tk=LANES)
    dh0 = _matmul(dqkvg, w_qkvg_t, "nn", F32, "mm_gdn_in_dx", add=dh0b)
    g_qkvg = _matmul(h0, dqkvg, "tn", F32, "mm_gdn_in_dw")
    g_ba = _matmul(h0, dba, "tn", F32, "mm_gdn_ba_dw")
    g_w_in = jnp.concatenate([g_qkvg, g_ba[:, :2 * HEADS]], axis=1)[None]
    partial_late, partial_late_bf16 = late_reduce(dict(gdn_w_in=g_w_in)) if late_reduce is not None else ([], ())
    (grad_x,), (g_mix_pre0,), from_chips_late = _rowwise_bwd(
        "norm_in_bwd", lambda xx, gg: (_rms(xx, gg), xx), [full(x)], [row(mix_pre, 0)], [dh0, dx0], [F32], partials=partial_late_bf16)

    grads = dict(
        mix_pre_gain=jnp.concatenate([g_mix_pre0, g_mix_pre1], axis=0),
        mix_post_gain=jnp.concatenate([g_mix_post0, g_mix_post1], axis=0),
        mlp_pre_gain=jnp.concatenate([g_mlp_pre0, g_mlp_pre1], axis=0),
        mlp_post_gain=jnp.concatenate([g_mlp_post0, g_mlp_post1], axis=0),
        mlp_w_up=jnp.stack([_join(buf_up[layer, :, :d], "cols") for layer in range(2)]),
        mlp_w_down=jnp.stack([_join(buf_down[layer, :, :d], "rows") for layer in range(2)]),
        gdn_w_in=g_w_in,
        gdn_conv_w=g_conv[None, :CONV_K],
        gdn_a_log=g_al[:, HEADS:2 * HEADS],
        gdn_dt_bias=g_dtb[:, HEADS:2 * HEADS],
        gdn_out_gain=jnp.sum(g_og.reshape(HEADS, HEAD_DIM), axis=0, keepdims=True),
        gdn_w_out=g_out[None],
        kv_gain=g_kv_gain[0],
        w_kv=g_kv,
        sb_w_q=g_q[None],
        sb_w_o=g_o[None],
    )
    return loss, grad_x, grads, (list(partial) + list(partial_late), list(from_chips) + list(from_chips_late))


N_DEV = 8
N_CHIPS = 4
PACK_ROW_TILE = 128

_HBM = pl.BlockSpec(memory_space=pltpu.HBM)


def _place():
    return lax.axis_index("x"), lax.axis_index("y"), lax.axis_index("c")


def _other_chips(x, y):
    return [(1 - x, y), (x, 1 - y), (1 - x, 1 - y)]


def _remote(src, dst, send_sem, recv_sem, to):
    return pltpu.make_async_remote_copy(src_ref=src, dst_ref=dst, send_sem=send_sem, recv_sem=recv_sem, device_id=to, device_id_type=MESH)


def _gather8(v, name):
    rows, cols = v.shape

    def body(v_ref, out_ref, sum_ref, send_sems, recv_sems, local_sem):
        x, y, c = _place()
        me, sibling = (x, y, c), (x, y, 1 - c)
        chips = _other_chips(x, y)

        def blk(px, py, pc):
            return out_ref.at[pl.ds((4 * px + 2 * py + pc) * rows, rows), :]

        def copy(k, block, to, src=None):
            return _remote(blk(*block) if src is None else src, blk(*block), send_sems.at[k], recv_sems.at[k], to)

        mine = pltpu.make_async_copy(v_ref, blk(*me), local_sem)
        mine.start()
        first = [copy(0, me, sibling, src=v_ref)] + [copy(1 + j, me, (*chip, c), src=v_ref) for j, chip in enumerate(chips)]
        for cp in first:
            cp.start()
        passed = [copy(4 + j, (*chip, c), sibling) for j, chip in enumerate(chips)]
        for j, chip in enumerate(chips):
            copy(1 + j, (*chip, c), me).wait_recv()
            passed[j].start()
        copy(0, sibling, me).wait_recv()
        for j, chip in enumerate(chips):
            copy(4 + j, (*chip, 1 - c), me).wait_recv()
        for cp in first + passed:
            cp.wait_send()
        mine.wait()
        acc = out_ref[pl.ds(0, rows), :]
        for dev in range(1, N_DEV):
            acc = acc + out_ref[pl.ds(dev * rows, rows), :]
        sum_ref[...] = acc

    vm = pl.BlockSpec(memory_space=pltpu.VMEM)
    return pl.pallas_call(
        body,
        name=name,
        out_shape=[jax.ShapeDtypeStruct((N_DEV * rows, cols), v.dtype), jax.ShapeDtypeStruct((rows, cols), v.dtype)],
        in_specs=[vm],
        out_specs=[vm, vm],
        scratch_shapes=[pltpu.SemaphoreType.DMA((7,)), pltpu.SemaphoreType.DMA((7,)), pltpu.SemaphoreType.DMA],
    )(v)


def _hbm_call(body, name, arrs, out_shapes, sem_counts):
    n = len(arrs)

    def wrapped(*refs):
        body(refs[:n], refs[n:2 * n], *refs[2 * n:])

    return pl.pallas_call(
        wrapped,
        name=name,
        out_shape=[jax.ShapeDtypeStruct(s, a.dtype) for s, a in zip(out_shapes, arrs)],
        in_specs=[_HBM] * n,
        out_specs=[_HBM] * n,
        scratch_shapes=[pltpu.SemaphoreType.DMA((k,)) for k in sem_counts],
    )(*arrs)


def _gather_sends(w_refs, out_refs, send_sems, recv_sems):
    x, y, c = _place()
    s_me = 2 * x + y
    return [_remote(w.at[c], o.at[s_me, c], send_sems.at[3 * a + j], recv_sems.at[3 * a + j], (px, py, c))
            for a, (w, o) in enumerate(zip(w_refs, out_refs)) for j, (px, py) in enumerate(_other_chips(x, y))]


def _gather_finish(w_refs, out_refs, send_sems, recv_sems, fsend_sems, frecv_sems):
    x, y, c = _place()
    chips = _other_chips(x, y)
    passed = []
    for a, o in enumerate(out_refs):
        for j, (px, py) in enumerate(chips):
            half = o.at[2 * px + py, c]
            _remote(half, half, send_sems.at[3 * a + j], recv_sems.at[3 * a + j], (px, py, c)).wait_recv()
            fwd = _remote(half, half, fsend_sems.at[3 * a + j], frecv_sems.at[3 * a + j], (x, y, 1 - c))
            fwd.start()
            passed.append(fwd)
    for a, o in enumerate(out_refs):
        for j, (px, py) in enumerate(chips):
            half = o.at[2 * px + py, 1 - c]
            _remote(half, half, fsend_sems.at[3 * a + j], frecv_sems.at[3 * a + j], (x, y, 1 - c)).wait_recv()
    for cp in _gather_sends(w_refs, out_refs, send_sems, recv_sems) + passed:
        cp.wait_send()


def _swap_halves(arrs, name):
    n = len(arrs)

    def body(g_refs, a_refs, send_sems, recv_sems):
        x, y, c = _place()
        cps = [_remote(g.at[1 - c], a, send_sems.at[i], recv_sems.at[i], (x, y, 1 - c)) for i, (g, a) in enumerate(zip(g_refs, a_refs))]
        for cp in cps:
            cp.start()
        for cp in cps:
            cp.wait()

    return _hbm_call(body, name, arrs, [a.shape[1:] for a in arrs], [n, n])


def _scatter_copies(p_refs, b_refs, send_sems, recv_sems):
    x, y, c = _place()
    return [_remote(p.at[2 * px + py], b.at[j], send_sems.at[3 * i + j], recv_sems.at[3 * i + j], (px, py, c))
            for i, (p, b) in enumerate(zip(p_refs, b_refs)) for j, (px, py) in enumerate(_other_chips(x, y))]


def _share_halves(arrs):
    n = len(arrs)

    def body(q_refs, out_refs, send_sems, recv_sems):
        x, y, c = _place()
        cps = [_remote(q, o, send_sems.at[i], recv_sems.at[i], (x, y, 1 - c)) for i, (q, o) in enumerate(zip(q_refs, out_refs))]
        for cp in cps:
            cp.start()
        for cp in cps:
            cp.wait()

    return _hbm_call(body, "grads_share", arrs, [a.shape for a in arrs], [n, n])


_GROUPS = (
    (("mlp_w_up", (2, 1024, 1024), "cols"), ("gdn_w_out", (1, 256, 1024), "rows")),
    (("mlp_w_down", (2, 1024, 1024), "rows"), ("sb_w_q", (1, 256, 1024), "rows"), ("sb_w_o", (1, 256, 1024), "rows")),
    (("w_kv", (1024, 512), "cols"),),
    (("gdn_w_in", (1, 1024, 1028), "cols"),),
)
_BEHIND_CONV, _BEHIND_PREP, _FIRST = slice(0, 1), slice(1, 3), slice(3, 4)
_EARLY_GRADS = slice(0, 3)


def _numel(shape):
    n = 1
    for s in shape:
        n *= s
    return n


def _half_rows(shape):
    return _numel(shape[:-1]) // 2


def _pack_shards(shards, dtype):
    return tuple(jnp.concatenate([shards[n].astype(dtype).reshape(2, _half_rows(shape), shape[-1]) for n, shape, _ in grp], axis=1) for grp in _GROUPS)


def _unpack_shards(bufs):
    out = {}
    for grp, buf in zip(_GROUPS, bufs):
        off = 0
        for n, shape, _ in grp:
            out[n] = buf[:, off:off + _half_rows(shape)].reshape(shape)
            off += _half_rows(shape)
    return out


def _join(stacked, how):
    nd = stacked.ndim - 1
    ax = nd - 1 if how == "cols" else nd - 2
    moved = jnp.moveaxis(stacked, 0, ax)
    shape = list(stacked.shape[1:])
    shape[ax] *= N_CHIPS
    return moved.reshape(shape)


def _split(full, shard_shape, how):
    nd = len(shard_shape)
    ax = nd - 1 if how == "cols" else nd - 2
    shape = list(shard_shape)
    shape.insert(ax, N_CHIPS)
    return jnp.moveaxis(full.reshape(shape), ax, 0)


def _unpack_full(gathered, groups):
    out = {}
    for grp, buf in zip(groups, gathered):
        off = 0
        for n, shape, how in grp:
            out[n] = _join(buf[:, :, off:off + _half_rows(shape)].reshape((N_CHIPS,) + shape), how)
            off += _half_rows(shape)
    return out


def _grad_buffer_shape(grp):
    return (2, N_CHIPS, sum(_half_rows(shape) for _, shape, _ in grp), grp[0][1][-1])


def _pack_full(full, groups, started=None):
    bufs = []
    for gi, grp in enumerate(groups):
        def halves(n, shape, how):
            return jnp.swapaxes(_split(full[n], shape, how).reshape(N_CHIPS, 2, _half_rows(shape), shape[-1]), 0, 1)

        if started is not None and gi in started:
            buf, off = started[gi], _half_rows(grp[0][1])
            for n, shape, how in grp[1:]:
                buf = buf.at[:, :, off:off + _half_rows(shape), :].set(halves(n, shape, how))
                off += _half_rows(shape)
        else:
            buf = jnp.concatenate([halves(n, shape, how) for n, shape, how in grp], axis=2)
        bufs.append(buf.reshape(2, -1, buf.shape[-1]))
    return tuple(bufs)


_SMALL = (
    ("mix_pre_gain", (2, 1024)),
    ("mix_post_gain", (2, 1024)),
    ("mlp_pre_gain", (2, 1024)),
    ("mlp_post_gain", (2, 1024)),
    ("kv_gain", (1024,)),
    ("gdn_out_gain", (1, 128)),
    ("gdn_a_log", (1, 8)),
    ("gdn_dt_bias", (1, 8)),
    ("gdn_conv_w", (1, 4, 3072)),
    ("loss", ()),
)


def _rows_of(shape):
    return -(-_numel(shape) // LANES)


def _pack_rows(vals, layout):
    parts = []
    for n, shape in layout:
        flat = vals[n].reshape(-1)
        parts.append(jnp.pad(flat, (0, _rows_of(shape) * LANES - flat.shape[0])))
    flat = jnp.concatenate(parts)
    rows = -(-flat.shape[0] // (8 * LANES)) * 8
    return jnp.pad(flat, (0, rows * LANES - flat.shape[0])).reshape(rows, LANES)


def _unpack_rows(packed, layout):
    flat = packed.reshape(-1)
    out, off = {}, 0
    for n, shape in layout:
        out[n] = flat[off:off + _numel(shape)].reshape(shape)
        off += _rows_of(shape) * LANES
    return out


_WEIGHTS = ("mix_pre_gain", "mix_post_gain", "mlp_pre_gain", "mlp_post_gain", "mlp_w_up", "mlp_w_down", "gdn_w_in", "gdn_conv_w",
            "gdn_a_log", "gdn_dt_bias", "gdn_out_gain", "gdn_w_out", "kv_gain", "w_kv", "sb_w_q", "sb_w_o")


def _as2d(a):
    return a.reshape(1, -1) if a.ndim <= 1 else a.reshape(-1, a.shape[-1])


def kernel(x, mix_pre_gain, mix_post_gain, mlp_pre_gain, mlp_post_gain, mlp_w_up, mlp_w_down, gdn_w_in, gdn_conv_w, gdn_a_log, gdn_dt_bias, gdn_out_gain, gdn_w_out, kv_gain, w_kv, sb_w_q, sb_w_o, loss_target, m_mix_pre_gain, m_mix_post_gain, m_mlp_pre_gain, m_mlp_post_gain, m_mlp_w_up, m_mlp_w_down, m_gdn_w_in, m_gdn_conv_w, m_gdn_a_log, m_gdn_dt_bias, m_gdn_out_gain, m_gdn_w_out, m_kv_gain, m_w_kv, m_sb_w_q, m_sb_w_o, v_mix_pre_gain, v_mix_post_gain, v_mlp_pre_gain, v_mlp_post_gain, v_mlp_w_up, v_mlp_w_down, v_gdn_w_in, v_gdn_conv_w, v_gdn_a_log, v_gdn_dt_bias, v_gdn_out_gain, v_gdn_w_out, v_kv_gain, v_w_kv, v_sb_w_q, v_sb_w_o):
    w = dict(mix_pre_gain=mix_pre_gain, mix_post_gain=mix_post_gain, mlp_pre_gain=mlp_pre_gain, mlp_post_gain=mlp_post_gain, mlp_w_up=mlp_w_up, mlp_w_down=mlp_w_down, gdn_w_in=gdn_w_in, gdn_conv_w=gdn_conv_w, gdn_a_log=gdn_a_log, gdn_dt_bias=gdn_dt_bias, gdn_out_gain=gdn_out_gain, gdn_w_out=gdn_w_out, kv_gain=kv_gain, w_kv=w_kv, sb_w_q=sb_w_q, sb_w_o=sb_w_o)
    m = dict(mix_pre_gain=m_mix_pre_gain, mix_post_gain=m_mix_post_gain, mlp_pre_gain=m_mlp_pre_gain, mlp_post_gain=m_mlp_post_gain, mlp_w_up=m_mlp_w_up, mlp_w_down=m_mlp_w_down, gdn_w_in=m_gdn_w_in, gdn_conv_w=m_gdn_conv_w, gdn_a_log=m_gdn_a_log, gdn_dt_bias=m_gdn_dt_bias, gdn_out_gain=m_gdn_out_gain, gdn_w_out=m_gdn_w_out, kv_gain=m_kv_gain, w_kv=m_w_kv, sb_w_q=m_sb_w_q, sb_w_o=m_sb_w_o)
    v = dict(mix_pre_gain=v_mix_pre_gain, mix_post_gain=v_mix_post_gain, mlp_pre_gain=v_mlp_pre_gain, mlp_post_gain=v_mlp_post_gain, mlp_w_up=v_mlp_w_up, mlp_w_down=v_mlp_w_down, gdn_w_in=v_gdn_w_in, gdn_conv_w=v_gdn_conv_w, gdn_a_log=v_gdn_a_log, gdn_dt_bias=v_gdn_dt_bias, gdn_out_gain=v_gdn_out_gain, gdn_w_out=v_gdn_w_out, kv_gain=v_kv_gain, w_kv=v_w_kv, sb_w_q=v_sb_w_q, sb_w_o=v_sb_w_o)
    cx, cy, cc = _place()
    chip = 2 * cx + cy
    conv_cols = gdn_conv_w.shape[-1]

    own = _pack_shards(w, BF16)
    own_taps = jnp.pad(gdn_conv_w[0], ((0, CONV_K), (0, 0))).reshape(2, CONV_K, conv_cols)
    with_own = lambda gathered, mine: [lax.dynamic_update_index_in_dim(g, m, chip, 0) for g, m in zip(gathered, mine)]

    def assemble_first(gathered):
        w_in_all, taps_all = with_own(gathered, (*own[_FIRST], own_taps))
        w_in = _unpack_full([w_in_all], _GROUPS[_FIRST])["gdn_w_in"][0]
        taps = jnp.swapaxes(taps_all[:, 0], 0, 1).reshape(CONV_K, N_CHIPS * conv_cols)
        return w_in[:, :4 * HEADS * HEAD_DIM], jnp.pad(w_in[:, 4 * HEADS * HEAD_DIM:], ((0, 0), (0, LANES - 2 * HEADS))), taps

    def assemble(gathered_conv, gathered_prep):
        full = {**_unpack_full(with_own(gathered_conv, own[_BEHIND_CONV]), _GROUPS[_BEHIND_CONV]),
                **_unpack_full(with_own(gathered_prep, own[_BEHIND_PREP]), _GROUPS[_BEHIND_PREP])}
        return full["gdn_w_out"][0], full["w_kv"], full["sb_w_q"][0], full["sb_w_o"][0], full["mlp_w_up"], full["mlp_w_down"]

    gains = (mix_pre_gain, mix_post_gain, mlp_pre_gain, mlp_post_gain, kv_gain[None])
    small = (gdn_a_log, gdn_dt_bias, gdn_out_gain)
    tile = PACK_ROW_TILE

    def to_chip_partials(grads_full, groups, tag, started=None):
        bufs = _pack_full(grads_full, groups, started)
        p32, p16 = [], []
        for i, (buf, other) in enumerate(zip(bufs, _swap_halves(bufs, f"grads_to_sibling_{tag}"))):
            _, n, cols = buf.shape
            p, pb = _add_rows(f"grads_add_sibling_{tag}{i}", [(buf.reshape(2 * n, cols), cc * (n // tile)), (other, 0)], n, (F32, BF16), tile)
            p32.append(p.reshape(N_CHIPS, -1, cols))
            p16.append(pb.reshape(N_CHIPS, -1, cols))
        return p32, tuple(p16)

    loss_rows, grad_x, g_full, (partial, from_chips) = _local_step(
        x[0], loss_target[0], gains, small, ((*own[_FIRST], own_taps), own[_BEHIND_CONV], own[_BEHIND_PREP]), assemble_first, assemble,
        lambda g, started: to_chip_partials(g, _GROUPS[_EARLY_GRADS], "early", started), lambda g: to_chip_partials(g, _GROUPS[_FIRST], "late"))

    reduced = []
    for i, (p, others) in enumerate(zip(partial, from_chips)):
        _, r, cols = p.shape
        terms = [(p.reshape(N_CHIPS * r, cols), chip * (r // tile))] + [(others.reshape(3 * r, cols), j * (r // tile)) for j in range(3)]
        reduced.append(_add_rows(f"grads_add_chips_{i}", terms, r, (F32,), tile)[0])
    g_shard = _unpack_shards([jnp.where(cc == 0, jnp.stack([r, o]), jnp.stack([o, r])) for r, o in zip(reduced, _share_halves(tuple(reduced)))])

    g_small_local = {n: g_full[n] for n, _ in _SMALL if n != "loss"}
    g_small_local["loss"] = loss_rows[0, 0]
    _, small_sum = _gather8(_pack_rows(g_small_local, _SMALL), "allreduce_small")
    g_small = _unpack_rows(small_sum, _SMALL)
    loss = g_small.pop("loss")
    g_small["gdn_conv_w"] = lax.dynamic_slice_in_dim(g_small["gdn_conv_w"], chip * conv_cols, conv_cols, axis=2)

    grads = {**g_shard, **g_small}
    deltas, new_m, new_v = {}, {}, {}
    for n in _WEIGHTS:
        d2, m2, v2 = _adamw(_as2d(w[n]), _as2d(grads[n]), _as2d(m[n]), _as2d(v[n]), "adamw_" + n)
        deltas[n], new_m[n], new_v[n] = d2.reshape(w[n].shape), m2.reshape(w[n].shape), v2.reshape(w[n].shape)
    return (loss, grad_x[None], *[grads[n].reshape(w[n].shape) for n in _WEIGHTS], *[deltas[n] for n in _WEIGHTS],
            *[new_m[n] for n in _WEIGHTS], *[new_v[n] for n in _WEIGHTS])
```

```python
import functools

import jax
import jax.numpy as jnp
from jax import lax
from jax.experimental import pallas as pl
from jax.experimental.pallas import tpu as pltpu

F32, BF16 = jnp.float32, jnp.bfloat16
HI = lax.Precision.HIGHEST
MESH = pl.DeviceIdType.MESH

EPS = 1e-6
HEADS = 8
HEAD_DIM = 128
CHUNK = 64
CHUNK_SHIFT = CHUNK.bit_length() - 1
CONV_K = 4
QKV = 3 * HEADS * HEAD_DIM

ADAM_LR, ADAM_B1, ADAM_B2, ADAM_EPS, ADAM_WD, ADAM_STEP = 0.001, 0.9, 0.999, 1e-08, 0.01, 10

VMEM_LIMIT_BYTES = 48 * 1024 * 1024
LANES = 128

NN = ((1,), (0,))
NT = ((1,), (1,))
TN = ((0,), (0,))


def _dot(a, b, dims=NN, precision=None):
    return lax.dot_general(a, b, (dims, ((), ())), precision=precision, preferred_element_type=F32)


def _params(*sem):
    return pltpu.CompilerParams(dimension_semantics=sem, vmem_limit_bytes=VMEM_LIMIT_BYTES)


def _iota(shape, axis):
    return lax.broadcasted_iota(jnp.int32, shape, axis)


def _matmul(a, b, mode, out_dtype, name, tm=1024, tn=1024, tk=2048, add=None, epilogue=None, extras=(), into=None):
    if mode == "nn":
        (m, k), (k2, n) = a.shape, b.shape
    elif mode == "nt":
        (m, k), (n, k2) = a.shape, b.shape
    else:
        (k, m), (k2, n) = a.shape, b.shape
    assert k == k2, (a.shape, b.shape, mode)
    tm, tn, tk = min(tm, m), min(tn, n), min(tk, k)
    assert m % tm == 0 and n % tn == 0 and k % tk == 0, (a.shape, b.shape, mode)
    nk = k // tk
    dims = {"nn": NN, "nt": NT, "tn": TN}[mode]
    tiles = ([add] if add is not None else []) + list(extras)
    out_dtypes = out_dtype if isinstance(out_dtype, tuple) else (out_dtype,)
    n_in = 2 + len(tiles)
    carried = list(into[1]) if into is not None and into[1] is not None else []

    def finish(acc, extra_refs, o_refs):
        res = (acc,) * len(o_refs) if epilogue is None else epilogue(acc, *[r[...] for r in extra_refs])
        for o_ref, r in zip(o_refs, res):
            o_ref[...] = r.astype(o_ref.dtype)

    def body(*refs):
        a_ref, b_ref = refs[:2]
        extra_refs = refs[n_in - len(extras):n_in]
        o_refs, acc_ref = refs[n_in + len(carried):-1], refs[-1]
        prod = _dot(a_ref[...].astype(BF16), b_ref[...].astype(BF16), dims)
        if nk == 1:
            finish(prod + refs[2][...].astype(F32) if add is not None else prod, extra_refs, o_refs)
            return
        kk = pl.program_id(2)

        @pl.when(kk == 0)
        def _():
            acc_ref[...] = refs[2][...].astype(F32) if add is not None else jnp.zeros_like(acc_ref)

        acc_ref[...] += prod

        @pl.when(kk == nk - 1)
        def _():
            finish(acc_ref[...], extra_refs, o_refs)

    a_spec = pl.BlockSpec((tk, tm), lambda i, j, kk: (kk, i)) if mode == "tn" else pl.BlockSpec((tm, tk), lambda i, j, kk: (i, kk))
    b_spec = pl.BlockSpec((tn, tk), lambda i, j, kk: (j, kk)) if mode == "nt" else pl.BlockSpec((tk, tn), lambda i, j, kk: (kk, j))
    o_spec = pl.BlockSpec((tm, tn), lambda i, j, kk: (i, j))
    if into is None:
        out_specs, out_shape = [o_spec] * len(out_dtypes), [jax.ShapeDtypeStruct((m, n), dt) for dt in out_dtypes]
    else:
        shape, _, place = into
        out_specs = [pl.BlockSpec((None, None, tm, tn), lambda i, j, kk: (*place(i, j), 0, 0))] * len(out_dtypes)
        out_shape = [jax.ShapeDtypeStruct(shape, dt) for dt in out_dtypes]
    res = pl.pallas_call(
        body,
        name=name,
        grid=(m // tm, n // tn, nk),
        in_specs=[a_spec, b_spec] + [o_spec] * len(tiles) + [pl.BlockSpec(memory_space=pl.ANY)] * len(carried),
        out_specs=out_specs,
        out_shape=out_shape,
        input_output_aliases={n_in + t: t for t in range(len(carried))},
        scratch_shapes=[pltpu.VMEM((tm, tn), F32)],
        compiler_params=_params("parallel", "parallel", "arbitrary"),
    )(a, b, *tiles, *carried)
    return res if isinstance(out_dtype, tuple) else res[0]


def _row_specs(rows, tm):
    return [pl.BlockSpec((tm, w), lambda i, cb=cb: (i, cb)) for _, w, cb in rows]


def _full_spec(p):
    return pl.BlockSpec(p.shape, lambda i: (0,) * p.ndim)


ROW_TILE = 512


def _rowwise(name, fn, rows, params, outs, tm=ROW_TILE, gather=()):
    t = rows[0][0].shape[0]
    tm = min(tm, t)
    steps = t // tm
    nr, npar, nout, ng = len(rows), len(params), len(outs), len(gather)

    def body(*refs):
        ins = [r[...].astype(F32) for r in refs[:nr]]
        ps = [p[...] for p in refs[nr:nr + npar]]
        shard_refs = refs[nr + npar:nr + npar + ng]
        o_refs = refs[nr + npar + ng:nr + npar + ng + nout]
        all_refs, sems = refs[nr + npar + ng + nout:nr + npar + 2 * ng + nout], refs[nr + npar + 2 * ng + nout:]
        if ng:
            @pl.when(pl.program_id(0) == 0)
            def _():
                for cp in _gather_sends(shard_refs, all_refs, *sems[:2]):
                    cp.start()

        res = fn(*ins, *ps)
        for o_ref, r in zip(o_refs, res):
            o_ref[...] = r.astype(o_ref.dtype)

        if ng:
            @pl.when(pl.program_id(0) == steps - 1)
            def _():
                _gather_finish(shard_refs, all_refs, *sems)

    return pl.pallas_call(
        body,
        name=name,
        grid=(steps,),
        in_specs=_row_specs(rows, tm) + [_full_spec(p) for p in params] + [_HBM] * ng,
        out_specs=[pl.BlockSpec((tm, w), lambda i: (i, 0)) for w, _ in outs] + [_HBM] * ng,
        out_shape=[jax.ShapeDtypeStruct((t, w), dt) for w, dt in outs] + [jax.ShapeDtypeStruct((N_CHIPS,) + s.shape, s.dtype) for s in gather],
        scratch_shapes=[pltpu.SemaphoreType.DMA((3 * ng,))] * (4 if ng else 0),
        compiler_params=_params("arbitrary" if ng else "parallel"),
    )(*[r[0] for r in rows], *params, *gather)


def _add_rows(name, terms, n_rows, out_dtypes, tm):
    cols = terms[0][0].shape[1]
    firsts = jnp.stack([jnp.asarray(first, jnp.int32) for _, first in terms])

    def body(firsts_ref, *refs):
        acc = refs[0][...].astype(F32)
        for r in refs[1:len(terms)]:
            acc = acc + r[...].astype(F32)
        for o_ref in refs[len(terms):]:
            o_ref[...] = acc.astype(o_ref.dtype)

    return pl.pallas_call(
        body,
        name=name,
        grid_spec=pltpu.PrefetchScalarGridSpec(
            num_scalar_prefetch=1,
            grid=(n_rows // tm,),
            in_specs=[pl.BlockSpec((tm, cols), lambda i, firsts_ref, k=k: (firsts_ref[k] + i, 0)) for k in range(len(terms))],
            out_specs=[pl.BlockSpec((tm, cols), lambda i, firsts_ref: (i, 0)) for _ in out_dtypes],
        ),
        out_shape=[jax.ShapeDtypeStruct((n_rows, cols), dt) for dt in out_dtypes],
        compiler_params=_params("parallel"),
    )(firsts, *[a for a, _ in terms])


def _rowwise_bwd(name, fn, rows, params, cots, grad_dtypes, tm=ROW_TILE, partials=()):
    t = rows[0][0].shape[0]
    tm = min(tm, t)
    steps = t // tm
    nr, npar, nc, nsc = len(rows), len(params), len(cots), len(partials)
    want = [j for j, dt in enumerate(grad_dtypes) if dt is not None]
    widths = [rows[j][1] for j in want]
    n_row_outs = len(want)
    n_in = nr + npar + nc

    def body(*refs):
        i = pl.program_id(0)
        ins = [r[...].astype(F32) for r in refs[:nr]]
        ps = [p[...] for p in refs[nr:nr + npar]]
        cs = tuple(c[...].astype(F32) for c in refs[nr + npar:n_in])
        p_refs = refs[n_in:n_in + nsc]
        outs = refs[n_in + nsc:n_in + nsc + n_row_outs + npar]
        from_refs, sems = refs[n_in + nsc + n_row_outs + npar:n_in + 2 * nsc + n_row_outs + npar], refs[n_in + 2 * nsc + n_row_outs + npar:]
        if nsc:
            @pl.when(i == 0)
            def _():
                for cp in _scatter_copies(p_refs, from_refs, *sems):
                    cp.start()

        _, vjp = jax.vjp(fn, *ins, *ps)
        gs = vjp(cs)
        for o_ref, j in zip(outs, want):
            o_ref[...] = gs[j].astype(o_ref.dtype)
        pg_refs = outs[n_row_outs:]

        @pl.when(i == 0)
        def _():
            for pg in pg_refs:
                pg[...] = jnp.zeros_like(pg)

        for pg, g in zip(pg_refs, gs[nr:]):
            pg[...] += g

        if nsc:
            @pl.when(i == steps - 1)
            def _():
                for cp in _scatter_copies(p_refs, from_refs, *sems):
                    cp.wait()

    row_specs = [pl.BlockSpec((tm, w), lambda i: (i, 0)) for w in widths]
    row_shapes = [jax.ShapeDtypeStruct((t, w), grad_dtypes[j]) for j, w in zip(want, widths)]
    res = pl.pallas_call(
        body,
        name=name,
        grid=(steps,),
        in_specs=_row_specs(rows, tm) + [_full_spec(p) for p in params] + [pl.BlockSpec((tm, c.shape[1]), lambda i: (i, 0)) for c in cots] + [_HBM] * nsc,
        out_specs=row_specs + [_full_spec(p) for p in params] + [_HBM] * nsc,
        out_shape=row_shapes + [jax.ShapeDtypeStruct(p.shape, F32) for p in params] + [jax.ShapeDtypeStruct((3,) + p.shape[1:], p.dtype) for p in partials],
        scratch_shapes=[pltpu.SemaphoreType.DMA((3 * nsc,))] * (2 if nsc else 0),
        compiler_params=_params("arbitrary"),
    )(*[r[0] for r in rows], *params, *cots, *partials)
    return res[:n_row_outs], res[n_row_outs:n_row_outs + npar], res[n_row_outs + npar:]


def _rms(x, g):
    return x * lax.rsqrt(jnp.mean(x * x, axis=-1, keepdims=True) + EPS) * g


def _sigmoid(x):
    return 1.0 / (1.0 + jnp.exp(-x))


def _softplus(x):
    return jnp.maximum(x, 0.0) + jnp.log1p(jnp.exp(-jnp.abs(x)))


def _two_pass(x, m):
    hi = x.astype(BF16)
    lo = (x - hi.astype(F32)).astype(BF16)
    return _dot(hi, m) + _dot(lo, m)


def _head_sum_impl(x):
    sums = [jnp.sum(x[:, h * HEAD_DIM:(h + 1) * HEAD_DIM], axis=-1, keepdims=True) for h in range(HEADS)]
    return jnp.concatenate([jnp.broadcast_to(s, (x.shape[0], HEAD_DIM)) for s in sums], axis=1)


@jax.custom_vjp
def _head_sum(x):
    return _head_sum_impl(x)


_head_sum.defvjp(lambda x: (_head_sum_impl(x), None), lambda _, g: (_head_sum_impl(g),))


def _fn_norm(x, g):
    return (_rms(x, g),)


def _fn_gates(ba, al, dt):
    col = _iota((1, LANES), 1)
    g = jnp.where((col >= HEADS) & (col < 2 * HEADS), -jnp.exp(al) * _softplus(ba + dt), 0.0)
    rows = ba.shape[0]
    r, c = _iota((rows, rows), 0), _iota((rows, rows), 1)
    same = (r >> CHUNK_SHIFT) == (c >> CHUNK_SHIFT)
    gc = _dot(jnp.where(same & (r >= c), 1.0, 0.0), g, precision=HI)
    gtot = _dot(jnp.where(same, 1.0, 0.0), g, precision=HI)
    return _sigmoid(ba), gc, gtot


def _fn_post_q(c):
    s = c * _sigmoid(c)
    return (s * lax.rsqrt(_head_sum(s * s) + EPS) * (HEAD_DIM ** -0.5),)


def _fn_post_k(c):
    s = c * _sigmoid(c)
    return (s * lax.rsqrt(_head_sum(s * s) + EPS),)


def _fn_post_v(c):
    return (c * _sigmoid(c),)


def _fn_post(cq, ck, cv):
    return _fn_post_q(cq) + _fn_post_k(ck) + _fn_post_v(cv)


def _fn_outnorm(o, gate, og):
    y = o * lax.rsqrt(_head_sum(o * o) * (1.0 / HEAD_DIM) + EPS) * og
    return (y * (gate * _sigmoid(gate)),)


def _fn_res_norm(x, m, gp, gn):
    x1 = x + _rms(m, gp)
    return x1, _rms(x1, gn)


def _fn_res_norm2(x, m, gp, ga, gb):
    x1 = x + _rms(m, gp)
    return x1, _rms(x1, ga), _rms(x1, gb)


def _relu2_of(u):
    r = jnp.maximum(u, 0.0)
    return (r * r,)


def _relu2_cotangent(da, a):
    return (da * (2.0 * jnp.sqrt(a.astype(F32))),)


def _loss_call(x3, d1, tgt, g, tm=ROW_TILE):
    t, d = x3.shape
    tm = min(tm, t)

    def body(x_ref, d_ref, t_ref, g_ref, loss_ref, dx_ref, dd_ref, dg_ref):
        i = pl.program_id(0)
        y, vjp = jax.vjp(lambda x, dd, gg: x + _rms(dd, gg), x_ref[...], d_ref[...], g_ref[...])
        err = y - t_ref[...]
        lrow = 0.5 * jnp.mean(err * err, axis=-1, keepdims=True)
        dx, dd, dg = vjp(err * (1.0 / d))
        dx_ref[...] = dx
        dd_ref[...] = dd.astype(dd_ref.dtype)

        @pl.when(i == 0)
        def _():
            loss_ref[...] = jnp.zeros_like(loss_ref)
            dg_ref[...] = jnp.zeros_like(dg_ref)

        loss_ref[...] += jnp.broadcast_to(jnp.sum(lrow, axis=0, keepdims=True), loss_ref.shape)
        dg_ref[...] += dg

    row = pl.BlockSpec((tm, d), lambda i: (i, 0))
    return pl.pallas_call(
        body,
        name="loss_head",
        grid=(t // tm,),
        in_specs=[row, row, row, _full_spec(g)],
        out_specs=[pl.BlockSpec((8, LANES), lambda i: (0, 0)), row, row, _full_spec(g)],
        out_shape=[jax.ShapeDtypeStruct((8, LANES), F32), jax.ShapeDtypeStruct((t, d), F32), jax.ShapeDtypeStruct((t, d), BF16), jax.ShapeDtypeStruct(g.shape, F32)],
        compiler_params=_params("arbitrary"),
    )(x3, d1, tgt, g)


HALO = 8


def _conv_fwd(qkvg, conv_w, shards, tm=256):
    t = qkvg.shape[0]
    tm = min(tm, t)
    steps = t // tm
    wide = QKV // 3
    n = len(shards)

    def body(*refs):
        cur_ref, prev_ref, w_ref = refs[:3]
        shard_refs = refs[3:3 + n]
        o_ref, q_ref, k_ref, v_ref = refs[3 + n:7 + n]
        all_refs = refs[7 + n:7 + 2 * n]
        buf, sems = refs[7 + 2 * n], refs[8 + 2 * n:]
        i = pl.program_id(0)

        if n:
            @pl.when(i == 0)
            def _():
                for cp in _gather_sends(shard_refs, all_refs, *sems[:2]):
                    cp.start()

        buf[0:HALO, :] = jnp.where(i > 0, prev_ref[...], 0.0)
        buf[HALO:, :] = cur_ref[...]
        acc = buf[pl.ds(HALO - CONV_K + 1, tm), :] * w_ref[pl.ds(0, 1), :]
        for j in range(1, CONV_K):
            acc = acc + buf[pl.ds(HALO - CONV_K + 1 + j, tm), :] * w_ref[pl.ds(j, 1), :]
        o_ref[...] = acc
        (q_ref[...], k_ref[...], v_ref[...]) = _fn_post(acc[:, 0:wide], acc[:, wide:2 * wide], acc[:, 2 * wide:])

        if n:
            @pl.when(i == steps - 1)
            def _():
                _gather_finish(shard_refs, all_refs, *sems)

    part = pl.BlockSpec((tm, wide), lambda i: (i, 0))
    res = pl.pallas_call(
        body,
        name="conv_fwd",
        grid=(steps,),
        in_specs=[
            pl.BlockSpec((tm, QKV), lambda i: (i, 0)),
            pl.BlockSpec((HALO, QKV), lambda i: (jnp.maximum(i * (tm // HALO) - 1, 0), 0)),
            pl.BlockSpec((CONV_K, QKV), lambda i: (0, 0)),
        ] + [_HBM] * n,
        out_specs=[pl.BlockSpec((tm, QKV), lambda i: (i, 0)), part, part, part] + [_HBM] * n,
        out_shape=[jax.ShapeDtypeStruct((t, QKV), F32)] + [jax.ShapeDtypeStruct((t, wide), F32)] * 3
        + [jax.ShapeDtypeStruct((N_CHIPS,) + s.shape, s.dtype) for s in shards],
        scratch_shapes=[pltpu.VMEM((tm + HALO, QKV), F32)] + [pltpu.SemaphoreType.DMA((3 * n,))] * (4 if n else 0),
        compiler_params=_params("arbitrary"),
    )(qkvg, qkvg, conv_w, *shards)
    return res[:4], res[4:]


def _conv_bwd(conv, dqkv, dgate, qkvg, conv_w, tm=256):
    t = conv.shape[0]
    tm = min(tm, t)
    n = t // tm
    wg = dgate.shape[1]
    wide = QKV // 3

    def conv_cotangent(c_ref, g_refs):
        parts = [c_ref[:, j * wide:(j + 1) * wide] for j in range(3)]
        _, vjp = jax.vjp(_fn_post, *parts)
        return vjp(tuple(g[...] for g in g_refs))

    def body(c_ref, cn_ref, dq_ref, dk_ref, dv_ref, dqn_ref, dkn_ref, dvn_ref, dgate_ref, x_ref, xp_ref, w_ref, dx_ref, dw_ref, bufd, bufx):
        i = pl.program_id(0)
        for j, (cur, nxt) in enumerate(zip(conv_cotangent(c_ref, (dq_ref, dk_ref, dv_ref)), conv_cotangent(cn_ref, (dqn_ref, dkn_ref, dvn_ref)))):
            bufd[0:tm, j * wide:(j + 1) * wide] = cur
            bufd[tm:, j * wide:(j + 1) * wide] = jnp.where(i < n - 1, nxt, 0.0)
        bufx[0:HALO, :] = jnp.where(i > 0, xp_ref[...], 0.0)
        bufx[HALO:, :] = x_ref[...]

        @pl.when(i == 0)
        def _():
            dw_ref[...] = jnp.zeros_like(dw_ref)

        dcv = bufd[0:tm, :]
        acc = bufd[pl.ds(CONV_K - 1, tm), :] * w_ref[pl.ds(0, 1), :]
        for j in range(1, CONV_K):
            acc = acc + bufd[pl.ds(CONV_K - 1 - j, tm), :] * w_ref[pl.ds(j, 1), :]
        dx_ref[:, 0:QKV] = acc.astype(dx_ref.dtype)
        dx_ref[:, QKV:] = dgate_ref[...].astype(dx_ref.dtype)
        for j in range(CONV_K):
            dw_ref[pl.ds(j, 1), :] += jnp.sum(dcv * bufx[pl.ds(HALO - CONV_K + 1 + j, tm), :], axis=0, keepdims=True)

    def cur(width):
        return pl.BlockSpec((tm, width), lambda i: (i, 0))

    def nxt(width):
        return pl.BlockSpec((HALO, width), lambda i: (jnp.minimum((i + 1) * (tm // HALO), t // HALO - 1), 0))

    return pl.pallas_call(
        body,
        name="conv_bwd",
        grid=(n,),
        in_specs=[cur(QKV), nxt(QKV)] + [cur(wide)] * 3 + [nxt(wide)] * 3 + [
            cur(wg),
            cur(QKV),
            pl.BlockSpec((HALO, QKV), lambda i: (jnp.maximum(i * (tm // HALO) - 1, 0), 0)),
            pl.BlockSpec((CONV_K, QKV), lambda i: (0, 0)),
        ],
        out_specs=[pl.BlockSpec((tm, QKV + wg), lambda i: (i, 0)), pl.BlockSpec((HALO, QKV), lambda i: (0, 0))],
        out_shape=[jax.ShapeDtypeStruct((t, QKV + wg), BF16), jax.ShapeDtypeStruct((HALO, QKV), F32)],
        scratch_shapes=[pltpu.VMEM((tm + HALO, QKV), F32), pltpu.VMEM((tm + HALO, QKV), F32)],
        compiler_params=_params("arbitrary"),
    )(conv, conv, *dqkv, *dqkv, dgate, qkvg, qkvg, conv_w)


PREP_CHUNKS = 32
PREP_BWD_CHUNKS = 4
SCAN_CHUNKS = 8


def _hi_lo(x):
    hi = x.astype(BF16)
    return hi, (x - hi.astype(F32)).astype(BF16)


def _mm3(a, b, dims=NN):
    (ah, al), (bh, bl) = _hi_lo(a), _hi_lo(b)
    return _dot(ah, bh, dims) + (_dot(ah, bl, dims) + _dot(al, bh, dims))


def _neumann(lowers):
    c = lowers[0].shape[0]
    eye = jnp.where(_iota((c, c), 0) == _iota((c, c), 1), 1.0, 0.0)
    ps = [-low for low in lowers]
    tmats = [eye + p for p in ps]
    for _ in range(CHUNK_SHIFT - 1):
        ps = [_mm3(p, p) for p in ps]
        tmats = [t + _mm3(t, p) for t, p in zip(tmats, ps)]
    return tuple(tmats)


def _inv_cotangents(tmats, dts):
    half = [_mm3(t, dt, TN) for t, dt in zip(tmats, dts)]
    return tuple(-_mm3(hf, t, NT) for hf, t in zip(half, tmats))


@jax.custom_vjp
def _tri_inv(lowers):
    return _neumann(lowers)


def _tri_inv_fwd(lowers):
    tmats = _neumann(lowers)
    return tmats, tmats


_tri_inv.defvjp(_tri_inv_fwd, lambda tmats, dts: (_inv_cotangents(tmats, dts),))


@jax.custom_vjp
def _tri_inv_known(lowers, tmats):
    return tmats


_tri_inv_known.defvjp(lambda lowers, tmats: (tmats, tmats),
                      lambda tmats, dts: (_inv_cotangents(tmats, dts), tuple(jnp.zeros_like(t) for t in tmats)))


def _prep_chunks(qs, ks, vs, bs, gcs, gts, gcrs, tmats=None):
    c = CHUNK
    r, col = _iota((c, c), 0), _iota((c, c), 1)
    incl, strict = r >= col, r > col
    decays = [jnp.where(incl, jnp.exp(jnp.where(incl, gc - gcr, 0.0)), 0.0) for gc, gcr in zip(gcs, gcrs)]
    kbs = [k * b for k, b in zip(ks, bs)]
    kbfs = [k.astype(BF16) for k in ks]
    lowers = tuple(jnp.where(strict, _dot(kb.astype(BF16), kbf, NT) * decay, 0.0) for kb, kbf, decay in zip(kbs, kbfs, decays))
    tmats = _tri_inv(lowers) if tmats is None else _tri_inv_known(lowers, tuple(tmats))
    outs = []
    for q, k, v, b, gc, gt, kb, kbf, decay, tmat in zip(qs, ks, vs, bs, gcs, gts, kbs, kbfs, decays, tmats):
        tb = tmat.astype(BF16)
        egc = jnp.exp(gc)
        w = _dot(tb, (kb * egc).astype(BF16))
        u = _dot(tb, (v * b).astype(BF16))
        attn = _dot(q.astype(BF16), kbf, NT) * decay
        gl = jnp.broadcast_to(jnp.exp(jnp.mean(gt.reshape(c // 8, 8, 1), axis=0)), (8, HEAD_DIM))
        outs.append((w, u, q * egc, k * jnp.exp(gt - gc), attn, gl))
    return tuple(outs), tmats


def _prep_specs(rows, gch):
    head = pl.BlockSpec((rows, HEAD_DIM), lambda n, h: (n, h))
    gates = pl.BlockSpec((rows, LANES), lambda n, h: (n, 0))
    gcrow = pl.BlockSpec((1, gch, 1, CHUNK), lambda n, h: (h, n, 0, 0))
    square = pl.BlockSpec((1, rows, CHUNK), lambda n, h: (h, n, 0))
    gl = pl.BlockSpec((1, gch * 8, HEAD_DIM), lambda n, h: (h, n, 0))
    return head, gates, gcrow, square, gl


def _pick_lane(ref, sl, lane):
    return jnp.sum(jnp.where(_iota((1, LANES), 1) == lane, ref[sl, :], 0.0), axis=1, keepdims=True)


def _prep_inputs(q_ref, k_ref, v_ref, b_ref, gc_ref, gt_ref, gcr_ref, sls, h):
    return ([q_ref[sl, :] for sl in sls], [k_ref[sl, :] for sl in sls], [v_ref[sl, :] for sl in sls],
            [_pick_lane(b_ref, sl, h) for sl in sls], [_pick_lane(gc_ref, sl, h + HEADS) for sl in sls],
            [_pick_lane(gt_ref, sl, h + HEADS) for sl in sls], [gcr_ref[0, c] for c in range(len(sls))])


def _gdn_prep(q, k, v, beta, gc, gt, gcr, shards=()):
    t = q.shape[0]
    gch = min(PREP_CHUNKS, t // CHUNK)
    rows = gch * CHUNK
    steps = t // rows
    n = len(shards)

    def body(*refs):
        q_ref, k_ref, v_ref, b_ref, gc_ref, gt_ref, gcr_ref = refs[:7]
        shard_refs = refs[7:7 + n]
        w_ref, u_ref, qg_ref, kg_ref, at_ref, gl_ref, tm_ref = refs[7 + n:14 + n]
        all_refs, sems = refs[14 + n:14 + 2 * n], refs[14 + 2 * n:]
        h = pl.program_id(1)

        if n:
            @pl.when(jnp.logical_and(pl.program_id(0) == 0, h == 0))
            def _():
                for cp in _gather_sends(shard_refs, all_refs, *sems[:2]):
                    cp.start()

        sls = [pl.ds(c * CHUNK, CHUNK) for c in range(gch)]
        outs, tmats = _prep_chunks(*_prep_inputs(q_ref, k_ref, v_ref, b_ref, gc_ref, gt_ref, gcr_ref, sls, h))
        for c, (sl, (w, u, qg, kg, attn, gl), tmat) in enumerate(zip(sls, outs, tmats)):
            w_ref[sl, :] = w.astype(BF16)
            u_ref[sl, :] = u
            qg_ref[sl, :] = qg.astype(BF16)
            kg_ref[sl, :] = kg.astype(BF16)
            at_ref[0, sl, :] = attn.astype(BF16)
            gl_ref[0, pl.ds(c * 8, 8), :] = gl
            tm_ref[0, sl, :] = tmat

        if n:
            @pl.when(jnp.logical_and(pl.program_id(0) == steps - 1, h == HEADS - 1))
            def _():
                _gather_finish(shard_refs, all_refs, *sems)

    hb, col, gcrow, square, glb = _prep_specs(rows, gch)
    wide = HEADS * HEAD_DIM
    res = pl.pallas_call(
        body,
        name="gdn_prep",
        grid=(steps, HEADS),
        in_specs=[hb, hb, hb, col, col, col, gcrow] + [_HBM] * n,
        out_specs=[hb, hb, hb, hb, square, glb, square] + [_HBM] * n,
        out_shape=[
            jax.ShapeDtypeStruct((t, wide), BF16),
            jax.ShapeDtypeStruct((t, wide), F32),
            jax.ShapeDtypeStruct((t, wide), BF16),
            jax.ShapeDtypeStruct((t, wide), BF16),
            jax.ShapeDtypeStruct((HEADS, t, CHUNK), BF16),
            jax.ShapeDtypeStruct((HEADS, t // CHUNK * 8, HEAD_DIM), F32),
            jax.ShapeDtypeStruct((HEADS, t, CHUNK), F32),
        ] + [jax.ShapeDtypeStruct((N_CHIPS,) + s.shape, s.dtype) for s in shards],
        scratch_shapes=[pltpu.SemaphoreType.DMA((3 * n,))] * (4 if n else 0),
        compiler_params=_params("arbitrary", "arbitrary") if n else _params("parallel", "parallel"),
    )(q, k, v, beta, gc, gt, gcr, *shards)
    return res[:7], res[7:]


def _gdn_prep_bwd(q, k, v, beta, gc, gt, gcr, tmat, dw, du, dqg, dkg, dattn, dgl, partials=()):
    t = q.shape[0]
    gch = min(PREP_BWD_CHUNKS, t // CHUNK)
    rows = gch * CHUNK
    steps = t // rows
    n_sc = len(partials)

    def body(*refs):
        (q_ref, k_ref, v_ref, b_ref, gc_ref, gt_ref, gcr_ref, tm_ref, dw_ref, du_ref, dqg_ref, dkg_ref, dat_ref, dgl_ref) = refs[:14]
        p_refs = refs[14:14 + n_sc]
        dq_ref, dk_ref, dv_ref, db_ref, dgc_ref, dgt_ref, dgcr_ref = refs[14 + n_sc:21 + n_sc]
        from_refs, sems = refs[21 + n_sc:21 + 2 * n_sc], refs[21 + 2 * n_sc:]
        h = pl.program_id(1)
        lane = _iota((1, LANES), 1)

        if n_sc:
            @pl.when(jnp.logical_and(pl.program_id(0) == 0, h == 0))
            def _():
                for cp in _scatter_copies(p_refs, from_refs, *sems):
                    cp.start()

        @pl.when(h == 0)
        def _():
            db_ref[...] = jnp.zeros_like(db_ref)
            dgc_ref[...] = jnp.zeros_like(dgc_ref)
            dgt_ref[...] = jnp.zeros_like(dgt_ref)

        sls = [pl.ds(c * CHUNK, CHUNK) for c in range(gch)]
        known = [tm_ref[0, sl, :] for sl in sls]
        _, vjp = jax.vjp(lambda *a: _prep_chunks(*a, tmats=known)[0], *_prep_inputs(q_ref, k_ref, v_ref, b_ref, gc_ref, gt_ref, gcr_ref, sls, h))
        cots = tuple((dw_ref[sl, :], du_ref[sl, :], dqg_ref[sl, :], dkg_ref[sl, :], dat_ref[0, sl, :], dgl_ref[0, pl.ds(c * 8, 8), :]) for c, sl in enumerate(sls))
        dqs, dks, dvs, dbs, dgcs, dgts, dgcrs = vjp(cots)
        for c, sl in enumerate(sls):
            dq_ref[sl, :] = dqs[c]
            dk_ref[sl, :] = dks[c]
            dv_ref[sl, :] = dvs[c]
            db_ref[sl, :] += jnp.where(lane == h, dbs[c], 0.0)
            dgc_ref[sl, :] += jnp.where(lane == h + HEADS, dgcs[c], 0.0)
            dgt_ref[sl, :] += jnp.where(lane == h + HEADS, dgts[c], 0.0)
            dgcr_ref[0, c] = dgcrs[c]

        if n_sc:
            @pl.when(jnp.logical_and(pl.program_id(0) == steps - 1, h == HEADS - 1))
            def _():
                for cp in _scatter_copies(p_refs, from_refs, *sems):
                    cp.wait()

    hb, col, gcrow, square, glb = _prep_specs(rows, gch)
    wide = HEADS * HEAD_DIM
    res = pl.pallas_call(
        body,
        name="gdn_prep_bwd",
        grid=(steps, HEADS),
        in_specs=[hb, hb, hb, col, col, col, gcrow, square, hb, hb, hb, hb, square, glb] + [_HBM] * n_sc,
        out_specs=[hb, hb, hb, col, col, col, gcrow] + [_HBM] * n_sc,
        out_shape=[jax.ShapeDtypeStruct((t, wide), F32)] * 3 + [jax.ShapeDtypeStruct((t, LANES), F32)] * 3 + [jax.ShapeDtypeStruct((HEADS, t // CHUNK, 1, CHUNK), F32)]
        + [jax.ShapeDtypeStruct((3,) + p.shape[1:], p.dtype) for p in partials],
        scratch_shapes=[pltpu.SemaphoreType.DMA((3 * n_sc,))] * (2 if n_sc else 0),
        compiler_params=_params("arbitrary", "arbitrary"),
    )(q, k, v, beta, gc, gt, gcr, tmat, dw, du, dqg, dkg, dattn, dgl, *partials)
    return res[:7], res[7:]


def _gdn_scan(w, u, qg, kg, attn, gl):
    t = w.shape[0]
    n = t // CHUNK
    nch = min(SCAN_CHUNKS, n)
    wide = HEADS * HEAD_DIM

    def body(w_ref, u_ref, qg_ref, kg_ref, at_ref, gl_ref, o_ref, st_ref, s_ref):
        @pl.when(pl.program_id(0) == 0)
        def _():
            s_ref[...] = jnp.zeros_like(s_ref)

        heads = range(HEADS)
        cols = [pl.ds(h * HEAD_DIM, HEAD_DIM) for h in heads]
        for c in range(nch):
            rows, gl_rows = pl.ds(c * CHUNK, CHUNK), pl.ds(c * 8, 8)
            ss = [s_ref[h] for h in heads]
            sbs = [s.astype(BF16) for s in ss]
            vbs = [(u_ref[rows, hs] - _dot(w_ref[rows, hs], sb)).astype(BF16) for hs, sb in zip(cols, sbs)]
            outs = [_dot(qg_ref[rows, hs], sb) + _dot(at_ref[h, rows, :], vb) for h, hs, sb, vb in zip(heads, cols, sbs, vbs)]
            new = [s * jnp.tile(gl_ref[h, gl_rows, :], (HEAD_DIM // 8, 1)) + _dot(kg_ref[rows, hs], vb, TN) for h, hs, s, vb in zip(heads, cols, ss, vbs)]
            for h, hs in zip(heads, cols):
                st_ref[c, h] = ss[h]
                o_ref[rows, hs] = outs[h]
                s_ref[h] = new[h]

    row = pl.BlockSpec((nch * CHUNK, wide), lambda i: (i, 0))
    return pl.pallas_call(
        body,
        name="gdn_scan",
        grid=(n // nch,),
        in_specs=[row, row, row, row, pl.BlockSpec((HEADS, nch * CHUNK, CHUNK), lambda i: (0, i, 0)), pl.BlockSpec((HEADS, nch * 8, HEAD_DIM), lambda i: (0, i, 0))],
        out_specs=[row, pl.BlockSpec((nch, HEADS, HEAD_DIM, HEAD_DIM), lambda i: (i, 0, 0, 0))],
        out_shape=[jax.ShapeDtypeStruct((t, wide), F32), jax.ShapeDtypeStruct((n, HEADS, HEAD_DIM, HEAD_DIM), F32)],
        scratch_shapes=[pltpu.VMEM((HEADS, HEAD_DIM, HEAD_DIM), F32)],
        compiler_params=_params("arbitrary"),
    )(w, u, qg, kg, attn, gl)


def _gdn_scan_bwd(w, u, qg, kg, attn, gl, states, do):
    t = w.shape[0]
    n = t // CHUNK
    nch = min(SCAN_CHUNKS, n)
    steps = n // nch
    wide = HEADS * HEAD_DIM

    def body(w_ref, u_ref, qg_ref, kg_ref, at_ref, gl_ref, st_ref, do_ref, dw_ref, du_ref, dqg_ref, dkg_ref, dat_ref, dgl_ref, ds_ref):
        @pl.when(pl.program_id(0) == 0)
        def _():
            ds_ref[...] = jnp.zeros_like(ds_ref)

        heads = range(HEADS)
        cols = [pl.ds(h * HEAD_DIM, HEAD_DIM) for h in heads]
        for c in reversed(range(nch)):
            rows, gl_rows = pl.ds(c * CHUNK, CHUNK), pl.ds(c * 8, 8)
            ss = [st_ref[c, h] for h in heads]
            sbs = [s.astype(BF16) for s in ss]
            dsns = [ds_ref[h] for h in heads]
            dsbs = [d.astype(BF16) for d in dsns]
            dobs = [do_ref[rows, hs].astype(BF16) for hs in cols]
            vbs = [(u_ref[rows, hs] - _dot(w_ref[rows, hs], sb)).astype(BF16) for hs, sb in zip(cols, sbs)]
            dvns = [_dot(at_ref[h, rows, :], dob, TN) + _dot(kg_ref[rows, hs], dsb) for h, hs, dob, dsb in zip(heads, cols, dobs, dsbs)]
            dvbs = [d.astype(BF16) for d in dvns]
            for h, hs in zip(heads, cols):
                dat_ref[h, rows, :] = _dot(dobs[h], vbs[h], NT)
                dqg_ref[rows, hs] = _dot(dobs[h], sbs[h], NT)
                dkg_ref[rows, hs] = _dot(vbs[h], dsbs[h], NT)
                du_ref[rows, hs] = dvns[h]
                dw_ref[rows, hs] = -_dot(dvbs[h], sbs[h], NT)
                dgl_ref[h, gl_rows, :] = jnp.sum((dsns[h] * ss[h]).reshape(HEAD_DIM // 8, 8, HEAD_DIM), axis=0)
            new = [dsn * jnp.tile(gl_ref[h, gl_rows, :], (HEAD_DIM // 8, 1)) + _dot(qg_ref[rows, hs], dob, TN) - _dot(w_ref[rows, hs], dvb, TN)
                   for h, hs, dsn, dob, dvb in zip(heads, cols, dsns, dobs, dvbs)]
            for h in heads:
                ds_ref[h] = new[h]

    row = pl.BlockSpec((nch * CHUNK, wide), lambda i: (steps - 1 - i, 0))
    at = pl.BlockSpec((HEADS, nch * CHUNK, CHUNK), lambda i: (0, steps - 1 - i, 0))
    glb = pl.BlockSpec((HEADS, nch * 8, HEAD_DIM), lambda i: (0, steps - 1 - i, 0))
    return pl.pallas_call(
        body,
        name="gdn_scan_bwd",
        grid=(steps,),
        in_specs=[row, row, row, row, at, glb, pl.BlockSpec((nch, HEADS, HEAD_DIM, HEAD_DIM), lambda i: (steps - 1 - i, 0, 0, 0)), row],
        out_specs=[row, row, row, row, at, glb],
        out_shape=[jax.ShapeDtypeStruct((t, wide), F32)] * 4 + [jax.ShapeDtypeStruct((HEADS, t, CHUNK), F32), jax.ShapeDtypeStruct((HEADS, n * 8, HEAD_DIM), F32)],
        scratch_shapes=[pltpu.VMEM((HEADS, HEAD_DIM, HEAD_DIM), F32)],
        compiler_params=_params("arbitrary"),
    )(w, u, qg, kg, attn, gl, states, do)


SB_Q = 512
SB_K = 256
SB_STEP = 1
SB_DEAD = -105.0


def _sb_scores(q, k):
    z = _dot(q, k, NT) * (HEAD_DIM ** -0.5)
    lb = jnp.minimum(z, 0.0) - jnp.log(1.0 + jnp.exp(-jnp.abs(z)))
    return lb, lb - z


def _tri(n, rel):
    return jnp.where(rel(_iota((n, n), 0), _iota((n, n), 1)), 1.0, 0.0).astype(BF16)


def _lanes(col):
    return jnp.broadcast_to(col, (col.shape[0], LANES))


def _sb_fwd(q, k, v):
    t = q.shape[0]
    bq, bk = min(SB_Q, t), min(SB_K, t)
    nsub, rep = bq // bk, bk // LANES
    nstep = min(SB_STEP, nsub)
    steps_per_tile = nsub // nstep

    def body(q_ref, k_ref, v_ref, o_ref, rt_ref, first_ref):
        h = pl.program_id(0)
        i = pl.program_id(1)
        o_ref[...] = jnp.zeros_like(o_ref)
        rt_ref[...] = jnp.zeros_like(rt_ref)
        after = _tri(bk, lambda r, c: r > c)

        def block(j, r0, diag):
            st = pl.multiple_of(j * bk, bk)
            kv, vv = k_ref[pl.ds(st, bk), :], v_ref[pl.ds(st, bk), :]
            lb, l1m = _sb_scores(q_ref[r0:, :], kv)
            if diag:
                mask = _iota((bq - r0, bk), 1) + j * bk < _iota((bq - r0, bk), 0) + (r0 + i * bq)
                l1m = jnp.where(mask, l1m, 0.0)
            sums = _two_pass(l1m, after)
            run = rt_ref[r0:, :]
            a = jnp.exp(lb + jnp.tile(run, (1, rep)) + sums)
            if diag:
                a = jnp.where(mask, a, 0.0)
            o_ref[r0:, :] += _dot(a.astype(BF16), vv)
            rt_ref[r0:, :] = run + _lanes(sums[:, 0:1] + l1m[:, 0:1])

        for s in reversed(range(nsub)):
            block(i * nsub + s, s * bk, True)

        def alive(carry):
            u, highest = carry
            return jnp.logical_and(u >= 0, highest > SB_DEAD)

        def step(carry):
            u, _ = carry
            for s in reversed(range(nstep)):
                block(u * nstep + s, 0, False)
            return u - 1, jnp.max(rt_ref[...])

        u_end, _ = lax.while_loop(alive, step, (i * steps_per_tile - 1, jnp.max(rt_ref[...])))
        first_ref[h, i] = u_end + 1

    qb = pl.BlockSpec((bq, HEAD_DIM), lambda h, i: (i, h))
    full = pl.BlockSpec((t, HEAD_DIM), lambda h, i: (0, h))
    return pl.pallas_call(
        body,
        name="sb_fwd",
        grid=(HEADS, t // bq),
        in_specs=[qb, full, full],
        out_specs=[qb, qb, pl.BlockSpec(memory_space=pltpu.SMEM)],
        out_shape=[jax.ShapeDtypeStruct(q.shape, F32), jax.ShapeDtypeStruct(q.shape, F32), jax.ShapeDtypeStruct((HEADS, t // bq), jnp.int32)],
        compiler_params=_params("arbitrary", "arbitrary"),
    )(q, k, v)


def _sb_bwd(q, k, v, rt, first, do):
    t = q.shape[0]
    bq, bk = min(SB_Q, t), min(SB_K, t)
    nsub, rep = bq // bk, bk // LANES
    nstep = min(SB_STEP, nsub)
    steps_per_tile = nsub // nstep
    scale = HEAD_DIM ** -0.5

    def body(first_ref, q_ref, k_ref, v_ref, rt_ref, do_ref, dq_ref, dk_ref, dv_ref, left_ref, pg_ref):
        h = pl.program_id(0)
        i = pl.program_id(1)

        @pl.when(i == 0)
        def _():
            dk_ref[...] = jnp.zeros_like(dk_ref)
            dv_ref[...] = jnp.zeros_like(dv_ref)

        dq_ref[...] = jnp.zeros_like(dq_ref)
        left_ref[...] = jnp.zeros_like(left_ref)
        pg_ref[...] = jnp.zeros_like(pg_ref)
        upto = _tri(bk, lambda r, c: r <= c)

        def block(j, r0, diag):
            st = pl.multiple_of(j * bk, bk)
            kv, vv = k_ref[pl.ds(st, bk), :], v_ref[pl.ds(st, bk), :]
            qv = q_ref[r0:, :]
            dob = do_ref[r0:, :].astype(BF16)
            lb, l1m = _sb_scores(qv, kv)
            if diag:
                mask = _iota((bq - r0, bk), 1) + j * bk < _iota((bq - r0, bk), 0) + (r0 + i * bq)
                l1m = jnp.where(mask, l1m, 0.0)
            sums = _two_pass(l1m, upto)
            left = left_ref[r0:, :]
            a = jnp.exp(lb + jnp.tile(rt_ref[r0:, :] - left, (1, rep)) - sums)
            if diag:
                a = jnp.where(mask, a, 0.0)
            g = _dot(dob, vv, NT) * a
            dv_ref[pl.ds(st, bk), :] += _dot(a.astype(BF16), dob, TN)
            gsum = _two_pass(g, upto)
            pg = pg_ref[r0:, :]
            dz = g - jnp.exp(lb) * (jnp.tile(pg, (1, rep)) + gsum)
            if diag:
                dz = jnp.where(mask, dz, 0.0)
            dzb = (dz * scale).astype(BF16)
            dk_ref[pl.ds(st, bk), :] += _dot(dzb, qv, TN)
            dq_ref[r0:, :] += _dot(dzb, kv)
            left_ref[r0:, :] = left + _lanes(sums[:, bk - 1:bk])
            pg_ref[r0:, :] = pg + _lanes(gsum[:, bk - 1:bk])

        def step(u, carry):
            for s in range(nstep):
                block(u * nstep + s, 0, False)
            return carry

        lax.fori_loop(first_ref[h, i], i * steps_per_tile, step, 0)
        for s in range(nsub):
            block(i * nsub + s, s * bk, True)

    qb = pl.BlockSpec((bq, HEAD_DIM), lambda h, i: (i, h))
    full = pl.BlockSpec((t, HEAD_DIM), lambda h, i: (0, h))
    return pl.pallas_call(
        body,
        name="sb_bwd",
        grid=(HEADS, t // bq),
        in_specs=[pl.BlockSpec(memory_space=pltpu.SMEM), qb, full, full, qb, qb],
        out_specs=[qb, full, full],
        out_shape=[jax.ShapeDtypeStruct(q.shape, F32)] * 3,
        scratch_shapes=[pltpu.VMEM((bq, LANES), F32), pltpu.VMEM((bq, LANES), F32)],
        compiler_params=_params("arbitrary", "arbitrary"),
    )(first, q, k, v, rt, do)


def _adamw(w, g, m, v, name, tm=ROW_TILE):
    r, c = w.shape
    tm = tm if r % tm == 0 else r

    def body(w_ref, g_ref, m_ref, v_ref, d_ref, nm_ref, nv_ref):
        gv = g_ref[...]
        nm = ADAM_B1 * m_ref[...] + (1.0 - ADAM_B1) * gv
        nv = ADAM_B2 * v_ref[...] + (1.0 - ADAM_B2) * (gv * gv)
        m_hat = nm / (1.0 - ADAM_B1 ** ADAM_STEP)
        v_hat = nv / (1.0 - ADAM_B2 ** ADAM_STEP)
        d_ref[...] = -ADAM_LR * (m_hat / (jnp.sqrt(v_hat) + ADAM_EPS) + ADAM_WD * w_ref[...])
        nm_ref[...] = nm
        nv_ref[...] = nv

    blk = pl.BlockSpec((tm, c), lambda i: (i, 0))
    return pl.pallas_call(
        body,
        name=name,
        grid=(r // tm,),
        in_specs=[blk] * 4,
        out_specs=[blk] * 3,
        out_shape=[jax.ShapeDtypeStruct((r, c), F32)] * 3,
        compiler_params=_params("parallel"),
    )(w, g, m, v)


def _local_step(x, tgt, gains, small, shards, assemble_first, assemble, early_reduce=None, late_reduce=None):
    mix_pre, mix_post, mlp_pre, mlp_post, kv_gain = gains
    a_log, dt_bias, out_gain = small
    t, d = x.shape
    row = lambda a, i=None: a[i:i + 1] if i is not None else a
    al = jnp.zeros((1, LANES), F32).at[:, HEADS:2 * HEADS].set(a_log)
    dtb = jnp.zeros((1, LANES), F32).at[:, HEADS:2 * HEADS].set(dt_bias)
    og = jnp.tile(out_gain, (1, HEADS))
    full = lambda a: (a, a.shape[1], 0)

    h0, *gathered_first = _rowwise("norm_in", _fn_norm, [full(x)], [row(mix_pre, 0)], [(d, BF16)], gather=shards[0])
    w_qkvg, w_ba, conv_w = assemble_first(gathered_first)
    qkvg = _matmul(h0, w_qkvg, "nn", F32, "mm_gdn_in", tk=1024)
    ba = _matmul(h0, w_ba, "nn", F32, "mm_gdn_ba", tk=1024)
    (conv, gq, gk, gv), gathered_conv = _conv_fwd(qkvg, conv_w, shards[1])
    beta, gc, gt = _rowwise("gates", _fn_gates, [full(ba)], [al, dtb], [(LANES, F32)] * 3)
    gcr = jnp.swapaxes(gc[:, HEADS:2 * HEADS], 0, 1).reshape(HEADS, t // CHUNK, 1, CHUNK)
    (pw, pu, pqg, pkg, pattn, pgl, ptm), gathered_prep = _gdn_prep(gq, gk, gv, beta, gc, gt, gcr, shards[2])
    w_out, w_kv, w_q, w_o, w_up, w_down = assemble(gathered_conv, gathered_prep)
    w_qkvg_t, w_up_t = jnp.swapaxes(w_qkvg, -1, -2), jnp.swapaxes(w_up, -1, -2)
    o_gdn, states = _gdn_scan(pw, pu, pqg, pkg, pattn, pgl)
    (on,) = _rowwise("out_norm", _fn_outnorm, [full(o_gdn), (qkvg, d, 3)], [og], [(d, BF16)])
    mix0 = _matmul(on, w_out, "nn", F32, "mm_gdn_out", tk=1024)
    x1, h1 = _rowwise("res_a0", _fn_res_norm, [full(x), full(mix0)], [row(mix_post, 0), row(mlp_pre, 0)], [(d, F32), (d, BF16)])
    (a0,) = _matmul(h1, w_up[0], "nn", (BF16,), "mm_up0", tk=1024, epilogue=_relu2_of)
    d0 = _matmul(a0, w_down[0], "nn", F32, "mm_down0")
    x2, hkv, hq = _rowwise("res_b0", _fn_res_norm2, [full(x1), full(d0)], [row(mlp_post, 0), kv_gain, row(mix_pre, 1)], [(d, F32), (d, BF16), (d, BF16)])
    w_k, w_v = w_kv[:, :d], w_kv[:, d:]
    kp = _matmul(hkv, w_k, "nn", BF16, "mm_k", tk=1024)
    vp = _matmul(hkv, w_v, "nn", BF16, "mm_v", tk=1024)
    qp = _matmul(hq, w_q, "nn", BF16, "mm_q", tk=1024)
    o_sb, rt, sb_first = _sb_fwd(qp, kp, vp)
    mix1 = _matmul(o_sb, w_o, "nn", F32, "mm_sb_out", tk=1024)
    x3, h3 = _rowwise("res_a1", _fn_res_norm, [full(x2), full(mix1)], [row(mix_post, 1), row(mlp_pre, 1)], [(d, F32), (d, BF16)])
    (a1,) = _matmul(h3, w_up[1], "nn", (BF16,), "mm_up1", tk=1024, epilogue=_relu2_of)
    d1 = _matmul(a1, w_down[1], "nn", F32, "mm_down1")

    loss, dx3, dd1, g_mlp_post1 = _loss_call(x3, d1, tgt, row(mlp_post, 1))
    (du1,) = _matmul(dd1, w_down[1], "nt", (BF16,), "mm_down1_dx", epilogue=_relu2_cotangent, extras=[a1])
    up_shape, down_shape = _grad_buffer_shape(_GROUPS[0]), _grad_buffer_shape(_GROUPS[1])
    buf_down = _matmul(a1, dd1, "tn", (F32, BF16), "mm_down1_dw", into=(down_shape, None, lambda i, j: (1, i)))
    dh3 = _matmul(du1, w_up_t[1], "nn", F32, "mm_up1_dx")
    buf_up = _matmul(h3, du1, "tn", (F32, BF16), "mm_up1_dw", into=(up_shape, None, lambda i, j: (1, j)))
    (dx2, dmix1), (g_mix_post1, g_mlp_pre1), _ = _rowwise_bwd(
        "res_a1_bwd", _fn_res_norm, [full(x2), full(mix1)], [row(mix_post, 1), row(mlp_pre, 1)], [dx3, dh3], [F32, BF16])
    do_sb = _matmul(dmix1, w_o, "nt", BF16, "mm_sb_out_dx")
    g_o = _matmul(o_sb, dmix1, "tn", F32, "mm_sb_out_dw")
    dqp, dkp, dvp = _sb_bwd(qp, kp, vp, rt, sb_first, do_sb)
    dhq = _matmul(dqp, w_q, "nt", F32, "mm_q_dx")
    g_q = _matmul(hq, dqp, "tn", F32, "mm_q_dw")
    dhkv = _matmul(dvp, w_v, "nt", F32, "mm_v_dx", add=_matmul(dkp, w_k, "nt", F32, "mm_k_dx"))
    g_kv = jnp.concatenate([_matmul(hkv, dkp, "tn", F32, "mm_k_dw"), _matmul(hkv, dvp, "tn", F32, "mm_v_dw")], axis=1)
    (dx1, dd0), (g_mlp_post0, g_kv_gain, g_mix_pre1), _ = _rowwise_bwd(
        "res_b0_bwd", _fn_res_norm2, [full(x1), full(d0)], [row(mlp_post, 0), kv_gain, row(mix_pre, 1)], [dx2, dhkv, dhq], [F32, BF16])
    (du0,) = _matmul(dd0, w_down[0], "nt", (BF16,), "mm_down0_dx", epilogue=_relu2_cotangent, extras=[a0])
    buf_down = _matmul(a0, dd0, "tn", (F32, BF16), "mm_down0_dw", into=(down_shape, buf_down, lambda i, j: (0, i)))
    dh1 = _matmul(du0, w_up_t[0], "nn", F32, "mm_up0_dx")
    buf_up = _matmul(h1, du0, "tn", (F32, BF16), "mm_up0_dw", into=(up_shape, buf_up, lambda i, j: (0, j)))
    (dx0, dmix0), (g_mix_post0, g_mlp_pre0), _ = _rowwise_bwd(
        "res_a0_bwd", _fn_res_norm, [full(x), full(mix0)], [row(mix_post, 0), row(mlp_pre, 0)], [dx1, dh1], [F32, BF16])
    don = _matmul(dmix0, w_out, "nt", F32, "mm_gdn_out_dx")
    g_out = _matmul(on, dmix0, "tn", F32, "mm_gdn_out_dw")
    (do_gdn, dgate), (g_og,), _ = _rowwise_bwd("out_norm_bwd", _fn_outnorm, [full(o_gdn), (qkvg, d, 3)], [og], [don], [F32, F32])
    dpw, dpu, dpqg, dpkg, dpattn, dpgl = _gdn_scan_bwd(pw, pu, pqg, pkg, pattn, pgl, states, do_gdn)
    partial, partial_bf16 = [], ()
    if early_reduce is not None:
        partial, partial_bf16 = early_reduce(dict(gdn_w_out=g_out[None], w_kv=g_kv, sb_w_q=g_q[None], sb_w_o=g_o[None]), {0: buf_up, 1: buf_down})
    (dgq, dgk, dgv, dbeta, dgc, dgt, dgcr), from_chips = _gdn_prep_bwd(gq, gk, gv, beta, gc, gt, gcr, ptm, dpw, dpu, dpqg, dpkg, dpattn, dpgl, partial_bf16)
    dgcr_lanes = jnp.pad(jnp.swapaxes(dgcr.reshape(HEADS, t), 0, 1), ((0, 0), (HEADS, LANES - 2 * HEADS)))
    gate_cots = [dbeta, dgc + dgcr_lanes, dgt]
    (dba,), (g_al, g_dtb), _ = _rowwise_bwd("gates_bwd", _fn_gates, [full(ba)], [al, dtb], gate_cots, [BF16])
    dqkvg, g_conv = _conv_bwd(conv, (dgq, dgk, dgv), dgate, qkvg, conv_w)
    dh0b = _matmul(dba, w_ba, "nt", F32, "mm_gdn_ba_dx", tk=LANES)
    dh0 = _matmul(dqkvg, w_qkvg_t, "nn", F32, "mm_gdn_in_dx", add=dh0b)
    g_qkvg = _matmul(h0, dqkvg, "tn", F32, "mm_gdn_in_dw")
    g_ba = _matmul(h0, dba, "tn", F32, "mm_gdn_ba_dw")
    g_w_in = jnp.concatenate([g_qkvg, g_ba[:, :2 * HEADS]], axis=1)[None]
    partial_late, partial_late_bf16 = late_reduce(dict(gdn_w_in=g_w_in)) if late_reduce is not None else ([], ())
    (grad_x,), (g_mix_pre0,), from_chips_late = _rowwise_bwd(
        "norm_in_bwd", lambda xx, gg: (_rms(xx, gg), xx), [full(x)], [row(mix_pre, 0)], [dh0, dx0], [F32], partials=partial_late_bf16)

    grads = dict(
        mix_pre_gain=jnp.concatenate([g_mix_pre0, g_mix_pre1], axis=0),
        mix_post_gain=jnp.concatenate([g_mix_post0, g_mix_post1], axis=0),
        mlp_pre_gain=jnp.concatenate([g_mlp_pre0, g_mlp_pre1], axis=0),
        mlp_post_gain=jnp.concatenate([g_mlp_post0, g_mlp_post1], axis=0),
        mlp_w_up=jnp.stack([_join(buf_up[0][layer, :, :d], "cols") for layer in range(2)]),
        mlp_w_down=jnp.stack([_join(buf_down[0][layer, :, :d], "rows") for layer in range(2)]),
        gdn_w_in=g_w_in,
        gdn_conv_w=g_conv[None, :CONV_K],
        gdn_a_log=g_al[:, HEADS:2 * HEADS],
        gdn_dt_bias=g_dtb[:, HEADS:2 * HEADS],
        gdn_out_gain=jnp.sum(g_og.reshape(HEADS, HEAD_DIM), axis=0, keepdims=True),
        gdn_w_out=g_out[None],
        kv_gain=g_kv_gain[0],
        w_kv=g_kv,
        sb_w_q=g_q[None],
        sb_w_o=g_o[None],
    )
    return loss, grad_x, grads, (list(partial) + list(partial_late), list(from_chips) + list(from_chips_late))


N_DEV = 8
N_CHIPS = 4
PACK_ROW_TILE = 128

_HBM = pl.BlockSpec(memory_space=pltpu.HBM)


def _place():
    return lax.axis_index("x"), lax.axis_index("y"), lax.axis_index("c")


def _other_chips(x, y):
    return [(1 - x, y), (x, 1 - y), (1 - x, 1 - y)]


def _remote(src, dst, send_sem, recv_sem, to):
    return pltpu.make_async_remote_copy(src_ref=src, dst_ref=dst, send_sem=send_sem, recv_sem=recv_sem, device_id=to, device_id_type=MESH)


def _gather8(v, name):
    rows, cols = v.shape

    def body(v_ref, out_ref, sum_ref, send_sems, recv_sems, local_sem):
        x, y, c = _place()
        me, sibling = (x, y, c), (x, y, 1 - c)
        chips = _other_chips(x, y)

        def blk(px, py, pc):
            return out_ref.at[pl.ds((4 * px + 2 * py + pc) * rows, rows), :]

        def copy(k, block, to, src=None):
            return _remote(blk(*block) if src is None else src, blk(*block), send_sems.at[k], recv_sems.at[k], to)

        mine = pltpu.make_async_copy(v_ref, blk(*me), local_sem)
        mine.start()
        first = [copy(0, me, sibling, src=v_ref)] + [copy(1 + j, me, (*chip, c), src=v_ref) for j, chip in enumerate(chips)]
        for cp in first:
            cp.start()
        passed = [copy(4 + j, (*chip, c), sibling) for j, chip in enumerate(chips)]
        for j, chip in enumerate(chips):
            copy(1 + j, (*chip, c), me).wait_recv()
            passed[j].start()
        copy(0, sibling, me).wait_recv()
        for j, chip in enumerate(chips):
            copy(4 + j, (*chip, 1 - c), me).wait_recv()
        for cp in first + passed:
            cp.wait_send()
        mine.wait()
        acc = out_ref[pl.ds(0, rows), :]
        for dev in range(1, N_DEV):
            acc = acc + out_ref[pl.ds(dev * rows, rows), :]
        sum_ref[...] = acc

    vm = pl.BlockSpec(memory_space=pltpu.VMEM)
    return pl.pallas_call(
        body,
        name=name,
        out_shape=[jax.ShapeDtypeStruct((N_DEV * rows, cols), v.dtype), jax.ShapeDtypeStruct((rows, cols), v.dtype)],
        in_specs=[vm],
        out_specs=[vm, vm],
        scratch_shapes=[pltpu.SemaphoreType.DMA((7,)), pltpu.SemaphoreType.DMA((7,)), pltpu.SemaphoreType.DMA],
    )(v)


def _hbm_call(body, name, arrs, out_shapes, sem_counts):
    n = len(arrs)

    def wrapped(*refs):
        body(refs[:n], refs[n:2 * n], *refs[2 * n:])

    return pl.pallas_call(
        wrapped,
        name=name,
        out_shape=[jax.ShapeDtypeStruct(s, a.dtype) for s, a in zip(out_shapes, arrs)],
        in_specs=[_HBM] * n,
        out_specs=[_HBM] * n,
        scratch_shapes=[pltpu.SemaphoreType.DMA((k,)) for k in sem_counts],
    )(*arrs)


def _gather_sends(w_refs, out_refs, send_sems, recv_sems):
    x, y, c = _place()
    s_me = 2 * x + y
    return [_remote(w.at[c], o.at[s_me, c], send_sems.at[3 * a + j], recv_sems.at[3 * a + j], (px, py, c))
            for a, (w, o) in enumerate(zip(w_refs, out_refs)) for j, (px, py) in enumerate(_other_chips(x, y))]


def _gather_finish(w_refs, out_refs, send_sems, recv_sems, fsend_sems, frecv_sems):
    x, y, c = _place()
    chips = _other_chips(x, y)
    passed = []
    for a, o in enumerate(out_refs):
        for j, (px, py) in enumerate(chips):
            half = o.at[2 * px + py, c]
            _remote(half, half, send_sems.at[3 * a + j], recv_sems.at[3 * a + j], (px, py, c)).wait_recv()
            fwd = _remote(half, half, fsend_sems.at[3 * a + j], frecv_sems.at[3 * a + j], (x, y, 1 - c))
            fwd.start()
            passed.append(fwd)
    for a, o in enumerate(out_refs):
        for j, (px, py) in enumerate(chips):
            half = o.at[2 * px + py, 1 - c]
            _remote(half, half, fsend_sems.at[3 * a + j], frecv_sems.at[3 * a + j], (x, y, 1 - c)).wait_recv()
    for cp in _gather_sends(w_refs, out_refs, send_sems, recv_sems) + passed:
        cp.wait_send()


def _swap_halves(arrs, name):
    n = len(arrs)

    def body(g_refs, a_refs, send_sems, recv_sems):
        x, y, c = _place()
        cps = [_remote(g.at[1 - c], a, send_sems.at[i], recv_sems.at[i], (x, y, 1 - c)) for i, (g, a) in enumerate(zip(g_refs, a_refs))]
        for cp in cps:
            cp.start()
        for cp in cps:
            cp.wait()

    return _hbm_call(body, name, arrs, [a.shape[1:] for a in arrs], [n, n])


def _scatter_copies(p_refs, b_refs, send_sems, recv_sems):
    x, y, c = _place()
    return [_remote(p.at[2 * px + py], b.at[j], send_sems.at[3 * i + j], recv_sems.at[3 * i + j], (px, py, c))
            for i, (p, b) in enumerate(zip(p_refs, b_refs)) for j, (px, py) in enumerate(_other_chips(x, y))]


def _share_halves(arrs):
    n = len(arrs)

    def body(q_refs, out_refs, send_sems, recv_sems):
        x, y, c = _place()
        cps = [_remote(q, o, send_sems.at[i], recv_sems.at[i], (x, y, 1 - c)) for i, (q, o) in enumerate(zip(q_refs, out_refs))]
        for cp in cps:
            cp.start()
        for cp in cps:
            cp.wait()

    return _hbm_call(body, "grads_share", arrs, [a.shape for a in arrs], [n, n])


_GROUPS = (
    (("mlp_w_up", (2, 1024, 1024), "cols"), ("gdn_w_out", (1, 256, 1024), "rows")),
    (("mlp_w_down", (2, 1024, 1024), "rows"), ("sb_w_q", (1, 256, 1024), "rows"), ("sb_w_o", (1, 256, 1024), "rows")),
    (("w_kv", (1024, 512), "cols"),),
    (("gdn_w_in", (1, 1024, 1028), "cols"),),
)
_BEHIND_CONV, _BEHIND_PREP, _FIRST = slice(0, 1), slice(1, 3), slice(3, 4)
_EARLY_GRADS = slice(0, 3)


def _numel(shape):
    n = 1
    for s in shape:
        n *= s
    return n


def _half_rows(shape):
    return _numel(shape[:-1]) // 2


def _pack_shards(shards, dtype):
    return tuple(jnp.concatenate([shards[n].astype(dtype).reshape(2, _half_rows(shape), shape[-1]) for n, shape, _ in grp], axis=1) for grp in _GROUPS)


def _unpack_shards(bufs):
    out = {}
    for grp, buf in zip(_GROUPS, bufs):
        off = 0
        for n, shape, _ in grp:
            out[n] = buf[:, off:off + _half_rows(shape)].reshape(shape)
            off += _half_rows(shape)
    return out


def _join(stacked, how):
    nd = stacked.ndim - 1
    ax = nd - 1 if how == "cols" else nd - 2
    moved = jnp.moveaxis(stacked, 0, ax)
    shape = list(stacked.shape[1:])
    shape[ax] *= N_CHIPS
    return moved.reshape(shape)


def _split(full, shard_shape, how):
    nd = len(shard_shape)
    ax = nd - 1 if how == "cols" else nd - 2
    shape = list(shard_shape)
    shape.insert(ax, N_CHIPS)
    return jnp.moveaxis(full.reshape(shape), ax, 0)


def _unpack_full(gathered, groups):
    out = {}
    for grp, buf in zip(groups, gathered):
        off = 0
        for n, shape, how in grp:
            out[n] = _join(buf[:, :, off:off + _half_rows(shape)].reshape((N_CHIPS,) + shape), how)
            off += _half_rows(shape)
    return out


def _grad_buffer_shape(grp):
    return (2, N_CHIPS, sum(_half_rows(shape) for _, shape, _ in grp), grp[0][1][-1])


def _pack_full(full, groups, started=None):
    bufs = []
    for gi, grp in enumerate(groups):
        def halves(n, shape, how):
            return jnp.swapaxes(_split(full[n], shape, how).reshape(N_CHIPS, 2, _half_rows(shape), shape[-1]), 0, 1)

        if started is not None and gi in started:
            buf, off = started[gi], _half_rows(grp[0][1])
            for n, shape, how in grp[1:]:
                buf = buf.at[:, :, off:off + _half_rows(shape), :].set(halves(n, shape, how))
                off += _half_rows(shape)
        else:
            buf = jnp.concatenate([halves(n, shape, how) for n, shape, how in grp], axis=2)
        bufs.append(buf.reshape(2, -1, buf.shape[-1]))
    return tuple(bufs)


_SMALL = (
    ("mix_pre_gain", (2, 1024)),
    ("mix_post_gain", (2, 1024)),
    ("mlp_pre_gain", (2, 1024)),
    ("mlp_post_gain", (2, 1024)),
    ("kv_gain", (1024,)),
    ("gdn_out_gain", (1, 128)),
    ("gdn_a_log", (1, 8)),
    ("gdn_dt_bias", (1, 8)),
    ("gdn_conv_w", (1, 4, 3072)),
    ("loss", ()),
)


def _rows_of(shape):
    return -(-_numel(shape) // LANES)


def _pack_rows(vals, layout):
    parts = []
    for n, shape in layout:
        flat = vals[n].reshape(-1)
        parts.append(jnp.pad(flat, (0, _rows_of(shape) * LANES - flat.shape[0])))
    flat = jnp.concatenate(parts)
    rows = -(-flat.shape[0] // (8 * LANES)) * 8
    return jnp.pad(flat, (0, rows * LANES - flat.shape[0])).reshape(rows, LANES)


def _unpack_rows(packed, layout):
    flat = packed.reshape(-1)
    out, off = {}, 0
    for n, shape in layout:
        out[n] = flat[off:off + _numel(shape)].reshape(shape)
        off += _rows_of(shape) * LANES
    return out


_WEIGHTS = ("mix_pre_gain", "mix_post_gain", "mlp_pre_gain", "mlp_post_gain", "mlp_w_up", "mlp_w_down", "gdn_w_in", "gdn_conv_w",
            "gdn_a_log", "gdn_dt_bias", "gdn_out_gain", "gdn_w_out", "kv_gain", "w_kv", "sb_w_q", "sb_w_o")


def _as2d(a):
    return a.reshape(1, -1) if a.ndim <= 1 else a.reshape(-1, a.shape[-1])


def kernel(x, mix_pre_gain, mix_post_gain, mlp_pre_gain, mlp_post_gain, mlp_w_up, mlp_w_down, gdn_w_in, gdn_conv_w, gdn_a_log, gdn_dt_bias, gdn_out_gain, gdn_w_out, kv_gain, w_kv, sb_w_q, sb_w_o, loss_target, m_mix_pre_gain, m_mix_post_gain, m_mlp_pre_gain, m_mlp_post_gain, m_mlp_w_up, m_mlp_w_down, m_gdn_w_in, m_gdn_conv_w, m_gdn_a_log, m_gdn_dt_bias, m_gdn_out_gain, m_gdn_w_out, m_kv_gain, m_w_kv, m_sb_w_q, m_sb_w_o, v_mix_pre_gain, v_mix_post_gain, v_mlp_pre_gain, v_mlp_post_gain, v_mlp_w_up, v_mlp_w_down, v_gdn_w_in, v_gdn_conv_w, v_gdn_a_log, v_gdn_dt_bias, v_gdn_out_gain, v_gdn_w_out, v_kv_gain, v_w_kv, v_sb_w_q, v_sb_w_o):
    w = dict(mix_pre_gain=mix_pre_gain, mix_post_gain=mix_post_gain, mlp_pre_gain=mlp_pre_gain, mlp_post_gain=mlp_post_gain, mlp_w_up=mlp_w_up, mlp_w_down=mlp_w_down, gdn_w_in=gdn_w_in, gdn_conv_w=gdn_conv_w, gdn_a_log=gdn_a_log, gdn_dt_bias=gdn_dt_bias, gdn_out_gain=gdn_out_gain, gdn_w_out=gdn_w_out, kv_gain=kv_gain, w_kv=w_kv, sb_w_q=sb_w_q, sb_w_o=sb_w_o)
    m = dict(mix_pre_gain=m_mix_pre_gain, mix_post_gain=m_mix_post_gain, mlp_pre_gain=m_mlp_pre_gain, mlp_post_gain=m_mlp_post_gain, mlp_w_up=m_mlp_w_up, mlp_w_down=m_mlp_w_down, gdn_w_in=m_gdn_w_in, gdn_conv_w=m_gdn_conv_w, gdn_a_log=m_gdn_a_log, gdn_dt_bias=m_gdn_dt_bias, gdn_out_gain=m_gdn_out_gain, gdn_w_out=m_gdn_w_out, kv_gain=m_kv_gain, w_kv=m_w_kv, sb_w_q=m_sb_w_q, sb_w_o=m_sb_w_o)
    v = dict(mix_pre_gain=v_mix_pre_gain, mix_post_gain=v_mix_post_gain, mlp_pre_gain=v_mlp_pre_gain, mlp_post_gain=v_mlp_post_gain, mlp_w_up=v_mlp_w_up, mlp_w_down=v_mlp_w_down, gdn_w_in=v_gdn_w_in, gdn_conv_w=v_gdn_conv_w, gdn_a_log=v_gdn_a_log, gdn_dt_bias=v_gdn_dt_bias, gdn_out_gain=v_gdn_out_gain, gdn_w_out=v_gdn_w_out, kv_gain=v_kv_gain, w_kv=v_w_kv, sb_w_q=v_sb_w_q, sb_w_o=v_sb_w_o)
    cx, cy, cc = _place()
    chip = 2 * cx + cy
    conv_cols = gdn_conv_w.shape[-1]

    own = _pack_shards(w, BF16)
    own_taps = jnp.pad(gdn_conv_w[0], ((0, CONV_K), (0, 0))).reshape(2, CONV_K, conv_cols)
    with_own = lambda gathered, mine: [lax.dynamic_update_index_in_dim(g, m, chip, 0) for g, m in zip(gathered, mine)]

    def assemble_first(gathered):
        w_in_all, taps_all = with_own(gathered, (*own[_FIRST], own_taps))
        w_in = _unpack_full([w_in_all], _GROUPS[_FIRST])["gdn_w_in"][0]
        taps = jnp.swapaxes(taps_all[:, 0], 0, 1).reshape(CONV_K, N_CHIPS * conv_cols)
        return w_in[:, :4 * HEADS * HEAD_DIM], jnp.pad(w_in[:, 4 * HEADS * HEAD_DIM:], ((0, 0), (0, LANES - 2 * HEADS))), taps

    def assemble(gathered_conv, gathered_prep):
        full = {**_unpack_full(with_own(gathered_conv, own[_BEHIND_CONV]), _GROUPS[_BEHIND_CONV]),
                **_unpack_full(with_own(gathered_prep, own[_BEHIND_PREP]), _GROUPS[_BEHIND_PREP])}
        return full["gdn_w_out"][0], full["w_kv"], full["sb_w_q"][0], full["sb_w_o"][0], full["mlp_w_up"], full["mlp_w_down"]

    gains = (mix_pre_gain, mix_post_gain, mlp_pre_gain, mlp_post_gain, kv_gain[None])
    small = (gdn_a_log, gdn_dt_bias, gdn_out_gain)
    tile = PACK_ROW_TILE

    def to_chip_partials(grads_full, groups, tag, started=None):
        bufs = _pack_full(grads_full, groups, None if started is None else {gi: s[0] for gi, s in started.items()})
        send = list(bufs)
        if started is not None:
            low = _pack_full(grads_full, groups, {gi: s[1] for gi, s in started.items()})
            for gi in started:
                send[gi] = low[gi]
        p32, p16 = [], []
        for i, (buf, other) in enumerate(zip(bufs, _swap_halves(tuple(send), f"grads_to_sibling_{tag}"))):
            _, n, cols = buf.shape
            p, pb = _add_rows(f"grads_add_sibling_{tag}{i}", [(buf.reshape(2 * n, cols), cc * (n // tile)), (other, 0)], n, (F32, BF16), tile)
            p32.append(p.reshape(N_CHIPS, -1, cols))
            p16.append(pb.reshape(N_CHIPS, -1, cols))
        return p32, tuple(p16)

    loss_rows, grad_x, g_full, (partial, from_chips) = _local_step(
        x[0], loss_target[0], gains, small, ((*own[_FIRST], own_taps), own[_BEHIND_CONV], own[_BEHIND_PREP]), assemble_first, assemble,
        lambda g, started: to_chip_partials(g, _GROUPS[_EARLY_GRADS], "early", started), lambda g: to_chip_partials(g, _GROUPS[_FIRST], "late"))

    reduced = []
    for i, (p, others) in enumerate(zip(partial, from_chips)):
        _, r, cols = p.shape
        terms = [(p.reshape(N_CHIPS * r, cols), chip * (r // tile))] + [(others.reshape(3 * r, cols), j * (r // tile)) for j in range(3)]
        reduced.append(_add_rows(f"grads_add_chips_{i}", terms, r, (F32,), tile)[0])
    g_shard = _unpack_shards([jnp.where(cc == 0, jnp.stack([r, o]), jnp.stack([o, r])) for r, o in zip(reduced, _share_halves(tuple(reduced)))])

    g_small_local = {n: g_full[n] for n, _ in _SMALL if n != "loss"}
    g_small_local["loss"] = loss_rows[0, 0]
    _, small_sum = _gather8(_pack_rows(g_small_local, _SMALL), "allreduce_small")
    g_small = _unpack_rows(small_sum, _SMALL)
    loss = g_small.pop("loss")
    g_small["gdn_conv_w"] = lax.dynamic_slice_in_dim(g_small["gdn_conv_w"], chip * conv_cols, conv_cols, axis=2)

    grads = {**g_shard, **g_small}
    deltas, new_m, new_v = {}, {}, {}
    for n in _WEIGHTS:
        d2, m2, v2 = _adamw(_as2d(w[n]), _as2d(grads[n]), _as2d(m[n]), _as2d(v[n]), "adamw_" + n)
        deltas[n], new_m[n], new_v[n] = d2.reshape(w[n].shape), m2.reshape(w[n].shape), v2.reshape(w[n].shape)
    return (loss, grad_x[None], *[grads[n].reshape(w[n].shape) for n in _WEIGHTS], *[deltas[n] for n in _WEIGHTS],
            *[new_m[n] for n in _WEIGHTS], *[new_v[n] for n in _WEIGHTS])
```

```python
import functools

import jax
import jax.numpy as jnp
from jax import lax
from jax.experimental import pallas as pl
from jax.experimental.pallas import tpu as pltpu

F32, BF16 = jnp.float32, jnp.bfloat16
HI = lax.Precision.HIGHEST
MESH = pl.DeviceIdType.MESH

EPS = 1e-6
HEADS = 8
HEAD_DIM = 128
CHUNK = 64
CHUNK_SHIFT = CHUNK.bit_length() - 1
CONV_K = 4
QKV = 3 * HEADS * HEAD_DIM

ADAM_LR, ADAM_B1, ADAM_B2, ADAM_EPS, ADAM_WD, ADAM_STEP = 0.001, 0.9, 0.999, 1e-08, 0.01, 10

VMEM_LIMIT_BYTES = 48 * 1024 * 1024
LANES = 128

NN = ((1,), (0,))
NT = ((1,), (1,))
TN = ((0,), (0,))


def _dot(a, b, dims=NN, precision=None):
    return lax.dot_general(a, b, (dims, ((), ())), precision=precision, preferred_element_type=F32)


def _params(*sem):
    return pltpu.CompilerParams(dimension_semantics=sem, vmem_limit_bytes=VMEM_LIMIT_BYTES)


def _iota(shape, axis):
    return lax.broadcasted_iota(jnp.int32, shape, axis)


def _matmul(a, b, mode, out_dtype, name, tm=1024, tn=1024, tk=2048, add=None, epilogue=None, extras=(), into=None):
    if mode == "nn":
        (m, k), (k2, n) = a.shape, b.shape
    elif mode == "nt":
        (m, k), (n, k2) = a.shape, b.shape
    else:
        (k, m), (k2, n) = a.shape, b.shape
    assert k == k2, (a.shape, b.shape, mode)
    tm, tn, tk = min(tm, m), min(tn, n), min(tk, k)
    assert m % tm == 0 and n % tn == 0 and k % tk == 0, (a.shape, b.shape, mode)
    nk = k // tk
    dims = {"nn": NN, "nt": NT, "tn": TN}[mode]
    tiles = ([add] if add is not None else []) + list(extras)
    out_dtypes = out_dtype if epilogue is not None else (out_dtype,)
    n_in = 2 + len(tiles)
    carried = [into[1]] if into is not None and into[1] is not None else []

    def finish(acc, extra_refs, o_refs):
        res = (acc,) if epilogue is None else epilogue(acc, *[r[...] for r in extra_refs])
        for o_ref, r in zip(o_refs, res):
            o_ref[...] = r.astype(o_ref.dtype)

    def body(*refs):
        a_ref, b_ref = refs[:2]
        extra_refs = refs[n_in - len(extras):n_in]
        o_refs, acc_ref = refs[n_in + len(carried):-1], refs[-1]
        prod = _dot(a_ref[...].astype(BF16), b_ref[...].astype(BF16), dims)
        if nk == 1:
            finish(prod + refs[2][...].astype(F32) if add is not None else prod, extra_refs, o_refs)
            return
        kk = pl.program_id(2)

        @pl.when(kk == 0)
        def _():
            acc_ref[...] = refs[2][...].astype(F32) if add is not None else jnp.zeros_like(acc_ref)

        acc_ref[...] += prod

        @pl.when(kk == nk - 1)
        def _():
            finish(acc_ref[...], extra_refs, o_refs)

    a_spec = pl.BlockSpec((tk, tm), lambda i, j, kk: (kk, i)) if mode == "tn" else pl.BlockSpec((tm, tk), lambda i, j, kk: (i, kk))
    b_spec = pl.BlockSpec((tn, tk), lambda i, j, kk: (j, kk)) if mode == "nt" else pl.BlockSpec((tk, tn), lambda i, j, kk: (kk, j))
    o_spec = pl.BlockSpec((tm, tn), lambda i, j, kk: (i, j))
    if into is None:
        out_specs, out_shape = [o_spec] * len(out_dtypes), [jax.ShapeDtypeStruct((m, n), dt) for dt in out_dtypes]
    else:
        shape, _, place = into
        out_specs = [pl.BlockSpec((None, None, tm, tn), lambda i, j, kk: (*place(i, j), 0, 0))]
        out_shape = [jax.ShapeDtypeStruct(shape, out_dtype)]
    res = pl.pallas_call(
        body,
        name=name,
        grid=(m // tm, n // tn, nk),
        in_specs=[a_spec, b_spec] + [o_spec] * len(tiles) + [pl.BlockSpec(memory_space=pl.ANY)] * len(carried),
        out_specs=out_specs,
        out_shape=out_shape,
        input_output_aliases={n_in: 0} if carried else {},
        scratch_shapes=[pltpu.VMEM((tm, tn), F32)],
        compiler_params=_params("parallel", "parallel", "arbitrary"),
    )(a, b, *tiles, *carried)
    return res if epilogue is not None else res[0]


def _row_specs(rows, tm):
    return [pl.BlockSpec((tm, w), lambda i, cb=cb: (i, cb)) for _, w, cb in rows]


def _full_spec(p):
    return pl.BlockSpec(p.shape, lambda i: (0,) * p.ndim)


ROW_TILE = 512


def _rowwise(name, fn, rows, params, outs, tm=ROW_TILE, gather=()):
    t = rows[0][0].shape[0]
    tm = min(tm, t)
    steps = t // tm
    nr, npar, nout, ng = len(rows), len(params), len(outs), len(gather)

    def body(*refs):
        ins = [r[...].astype(F32) for r in refs[:nr]]
        ps = [p[...] for p in refs[nr:nr + npar]]
        shard_refs = refs[nr + npar:nr + npar + ng]
        o_refs = refs[nr + npar + ng:nr + npar + ng + nout]
        all_refs, sems = refs[nr + npar + ng + nout:nr + npar + 2 * ng + nout], refs[nr + npar + 2 * ng + nout:]
        if ng:
            @pl.when(pl.program_id(0) == 0)
            def _():
                for cp in _gather_sends(shard_refs, all_refs, *sems[:2]):
                    cp.start()

        res = fn(*ins, *ps)
        for o_ref, r in zip(o_refs, res):
            o_ref[...] = r.astype(o_ref.dtype)

        if ng:
            @pl.when(pl.program_id(0) == steps - 1)
            def _():
                _gather_finish(shard_refs, all_refs, *sems)

    return pl.pallas_call(
        body,
        name=name,
        grid=(steps,),
        in_specs=_row_specs(rows, tm) + [_full_spec(p) for p in params] + [_HBM] * ng,
        out_specs=[pl.BlockSpec((tm, w), lambda i: (i, 0)) for w, _ in outs] + [_HBM] * ng,
        out_shape=[jax.ShapeDtypeStruct((t, w), dt) for w, dt in outs] + [jax.ShapeDtypeStruct((N_CHIPS,) + s.shape, s.dtype) for s in gather],
        scratch_shapes=[pltpu.SemaphoreType.DMA((3 * ng,))] * (4 if ng else 0),
        compiler_params=_params("arbitrary" if ng else "parallel"),
    )(*[r[0] for r in rows], *params, *gather)


def _add_rows(name, terms, n_rows, out_dtypes, tm):
    cols = terms[0][0].shape[1]
    firsts = jnp.stack([jnp.asarray(first, jnp.int32) for _, first in terms])

    def body(firsts_ref, *refs):
        acc = refs[0][...].astype(F32)
        for r in refs[1:len(terms)]:
            acc = acc + r[...].astype(F32)
        for o_ref in refs[len(terms):]:
            o_ref[...] = acc.astype(o_ref.dtype)

    return pl.pallas_call(
        body,
        name=name,
        grid_spec=pltpu.PrefetchScalarGridSpec(
            num_scalar_prefetch=1,
            grid=(n_rows // tm,),
            in_specs=[pl.BlockSpec((tm, cols), lambda i, firsts_ref, k=k: (firsts_ref[k] + i, 0)) for k in range(len(terms))],
            out_specs=[pl.BlockSpec((tm, cols), lambda i, firsts_ref: (i, 0)) for _ in out_dtypes],
        ),
        out_shape=[jax.ShapeDtypeStruct((n_rows, cols), dt) for dt in out_dtypes],
        compiler_params=_params("parallel"),
    )(firsts, *[a for a, _ in terms])


def _rowwise_bwd(name, fn, rows, params, cots, grad_dtypes, tm=ROW_TILE, partials=()):
    t = rows[0][0].shape[0]
    tm = min(tm, t)
    steps = t // tm
    nr, npar, nc, nsc = len(rows), len(params), len(cots), len(partials)
    want = [j for j, dt in enumerate(grad_dtypes) if dt is not None]
    widths = [rows[j][1] for j in want]
    n_row_outs = len(want)
    n_in = nr + npar + nc

    def body(*refs):
        i = pl.program_id(0)
        ins = [r[...].astype(F32) for r in refs[:nr]]
        ps = [p[...] for p in refs[nr:nr + npar]]
        cs = tuple(c[...].astype(F32) for c in refs[nr + npar:n_in])
        p_refs = refs[n_in:n_in + nsc]
        outs = refs[n_in + nsc:n_in + nsc + n_row_outs + npar]
        from_refs, sems = refs[n_in + nsc + n_row_outs + npar:n_in + 2 * nsc + n_row_outs + npar], refs[n_in + 2 * nsc + n_row_outs + npar:]
        if nsc:
            @pl.when(i == 0)
            def _():
                for cp in _scatter_copies(p_refs, from_refs, *sems):
                    cp.start()

        _, vjp = jax.vjp(fn, *ins, *ps)
        gs = vjp(cs)
        for o_ref, j in zip(outs, want):
            o_ref[...] = gs[j].astype(o_ref.dtype)
        pg_refs = outs[n_row_outs:]

        @pl.when(i == 0)
        def _():
            for pg in pg_refs:
                pg[...] = jnp.zeros_like(pg)

        for pg, g in zip(pg_refs, gs[nr:]):
            pg[...] += g

        if nsc:
            @pl.when(i == steps - 1)
            def _():
                for cp in _scatter_copies(p_refs, from_refs, *sems):
                    cp.wait()

    row_specs = [pl.BlockSpec((tm, w), lambda i: (i, 0)) for w in widths]
    row_shapes = [jax.ShapeDtypeStruct((t, w), grad_dtypes[j]) for j, w in zip(want, widths)]
    res = pl.pallas_call(
        body,
        name=name,
        grid=(steps,),
        in_specs=_row_specs(rows, tm) + [_full_spec(p) for p in params] + [pl.BlockSpec((tm, c.shape[1]), lambda i: (i, 0)) for c in cots] + [_HBM] * nsc,
        out_specs=row_specs + [_full_spec(p) for p in params] + [_HBM] * nsc,
        out_shape=row_shapes + [jax.ShapeDtypeStruct(p.shape, F32) for p in params] + [jax.ShapeDtypeStruct((3,) + p.shape[1:], p.dtype) for p in partials],
        scratch_shapes=[pltpu.SemaphoreType.DMA((3 * nsc,))] * (2 if nsc else 0),
        compiler_params=_params("arbitrary"),
    )(*[r[0] for r in rows], *params, *cots, *partials)
    return res[:n_row_outs], res[n_row_outs:n_row_outs + npar], res[n_row_outs + npar:]


def _rms(x, g):
    return x * lax.rsqrt(jnp.mean(x * x, axis=-1, keepdims=True) + EPS) * g


def _sigmoid(x):
    return 1.0 / (1.0 + jnp.exp(-x))


def _softplus(x):
    return jnp.maximum(x, 0.0) + jnp.log1p(jnp.exp(-jnp.abs(x)))


def _two_pass(x, m):
    hi = x.astype(BF16)
    lo = (x - hi.astype(F32)).astype(BF16)
    return _dot(hi, m) + _dot(lo, m)


def _head_sum_impl(x):
    sums = [jnp.sum(x[:, h * HEAD_DIM:(h + 1) * HEAD_DIM], axis=-1, keepdims=True) for h in range(HEADS)]
    return jnp.concatenate([jnp.broadcast_to(s, (x.shape[0], HEAD_DIM)) for s in sums], axis=1)


@jax.custom_vjp
def _head_sum(x):
    return _head_sum_impl(x)


_head_sum.defvjp(lambda x: (_head_sum_impl(x), None), lambda _, g: (_head_sum_impl(g),))


def _fn_norm(x, g):
    return (_rms(x, g),)


def _fn_gates(ba, al, dt):
    col = _iota((1, LANES), 1)
    g = jnp.where((col >= HEADS) & (col < 2 * HEADS), -jnp.exp(al) * _softplus(ba + dt), 0.0)
    rows = ba.shape[0]
    r, c = _iota((rows, rows), 0), _iota((rows, rows), 1)
    same = (r >> CHUNK_SHIFT) == (c >> CHUNK_SHIFT)
    gc = _dot(jnp.where(same & (r >= c), 1.0, 0.0), g, precision=HI)
    gtot = _dot(jnp.where(same, 1.0, 0.0), g, precision=HI)
    return _sigmoid(ba), gc, gtot


def _fn_post_q(c):
    s = c * _sigmoid(c)
    return (s * lax.rsqrt(_head_sum(s * s) + EPS) * (HEAD_DIM ** -0.5),)


def _fn_post_k(c):
    s = c * _sigmoid(c)
    return (s * lax.rsqrt(_head_sum(s * s) + EPS),)


def _fn_post_v(c):
    return (c * _sigmoid(c),)


def _fn_post(cq, ck, cv):
    return _fn_post_q(cq) + _fn_post_k(ck) + _fn_post_v(cv)


def _fn_outnorm(o, gate, og):
    y = o * lax.rsqrt(_head_sum(o * o) * (1.0 / HEAD_DIM) + EPS) * og
    return (y * (gate * _sigmoid(gate)),)


def _fn_res_norm(x, m, gp, gn):
    x1 = x + _rms(m, gp)
    return x1, _rms(x1, gn)


def _fn_res_norm2(x, m, gp, ga, gb):
    x1 = x + _rms(m, gp)
    return x1, _rms(x1, ga), _rms(x1, gb)


def _relu2_of(u):
    r = jnp.maximum(u, 0.0)
    return (r * r,)


def _relu2_cotangent(da, a):
    return (da * (2.0 * jnp.sqrt(a.astype(F32))),)


def _loss_call(x3, d1, tgt, g, tm=ROW_TILE):
    t, d = x3.shape
    tm = min(tm, t)

    def body(x_ref, d_ref, t_ref, g_ref, loss_ref, dx_ref, dd_ref, dg_ref):
        i = pl.program_id(0)
        y, vjp = jax.vjp(lambda x, dd, gg: x + _rms(dd, gg), x_ref[...], d_ref[...], g_ref[...])
        err = y - t_ref[...]
        lrow = 0.5 * jnp.mean(err * err, axis=-1, keepdims=True)
        dx, dd, dg = vjp(err * (1.0 / d))
        dx_ref[...] = dx
        dd_ref[...] = dd.astype(dd_ref.dtype)

        @pl.when(i == 0)
        def _():
            loss_ref[...] = jnp.zeros_like(loss_ref)
            dg_ref[...] = jnp.zeros_like(dg_ref)

        loss_ref[...] += jnp.broadcast_to(jnp.sum(lrow, axis=0, keepdims=True), loss_ref.shape)
        dg_ref[...] += dg

    row = pl.BlockSpec((tm, d), lambda i: (i, 0))
    return pl.pallas_call(
        body,
        name="loss_head",
        grid=(t // tm,),
        in_specs=[row, row, row, _full_spec(g)],
        out_specs=[pl.BlockSpec((8, LANES), lambda i: (0, 0)), row, row, _full_spec(g)],
        out_shape=[jax.ShapeDtypeStruct((8, LANES), F32), jax.ShapeDtypeStruct((t, d), F32), jax.ShapeDtypeStruct((t, d), BF16), jax.ShapeDtypeStruct(g.shape, F32)],
        compiler_params=_params("arbitrary"),
    )(x3, d1, tgt, g)


HALO = 8


def _conv_fwd(qkvg, conv_w, shards, tm=256):
    t = qkvg.shape[0]
    tm = min(tm, t)
    steps = t // tm
    wide = QKV // 3
    n = len(shards)

    def body(*refs):
        cur_ref, prev_ref, w_ref = refs[:3]
        shard_refs = refs[3:3 + n]
        o_ref, q_ref, k_ref, v_ref = refs[3 + n:7 + n]
        all_refs = refs[7 + n:7 + 2 * n]
        buf, sems = refs[7 + 2 * n], refs[8 + 2 * n:]
        i = pl.program_id(0)

        if n:
            @pl.when(i == 0)
            def _():
                for cp in _gather_sends(shard_refs, all_refs, *sems[:2]):
                    cp.start()

        buf[0:HALO, :] = jnp.where(i > 0, prev_ref[...], 0.0)
        buf[HALO:, :] = cur_ref[...]
        acc = buf[pl.ds(HALO - CONV_K + 1, tm), :] * w_ref[pl.ds(0, 1), :]
        for j in range(1, CONV_K):
            acc = acc + buf[pl.ds(HALO - CONV_K + 1 + j, tm), :] * w_ref[pl.ds(j, 1), :]
        o_ref[...] = acc
        (q_ref[...], k_ref[...], v_ref[...]) = _fn_post(acc[:, 0:wide], acc[:, wide:2 * wide], acc[:, 2 * wide:])

        if n:
            @pl.when(i == steps - 1)
            def _():
                _gather_finish(shard_refs, all_refs, *sems)

    part = pl.BlockSpec((tm, wide), lambda i: (i, 0))
    res = pl.pallas_call(
        body,
        name="conv_fwd",
        grid=(steps,),
        in_specs=[
            pl.BlockSpec((tm, QKV), lambda i: (i, 0)),
            pl.BlockSpec((HALO, QKV), lambda i: (jnp.maximum(i * (tm // HALO) - 1, 0), 0)),
            pl.BlockSpec((CONV_K, QKV), lambda i: (0, 0)),
        ] + [_HBM] * n,
        out_specs=[pl.BlockSpec((tm, QKV), lambda i: (i, 0)), part, part, part] + [_HBM] * n,
        out_shape=[jax.ShapeDtypeStruct((t, QKV), F32)] + [jax.ShapeDtypeStruct((t, wide), F32)] * 3
        + [jax.ShapeDtypeStruct((N_CHIPS,) + s.shape, s.dtype) for s in shards],
        scratch_shapes=[pltpu.VMEM((tm + HALO, QKV), F32)] + [pltpu.SemaphoreType.DMA((3 * n,))] * (4 if n else 0),
        compiler_params=_params("arbitrary"),
    )(qkvg, qkvg, conv_w, *shards)
    return res[:4], res[4:]


def _conv_bwd(conv, dqkv, dgate, qkvg, conv_w, tm=256):
    t = conv.shape[0]
    tm = min(tm, t)
    n = t // tm
    wg = dgate.shape[1]
    wide = QKV // 3

    def conv_cotangent(c_ref, g_refs):
        parts = [c_ref[:, j * wide:(j + 1) * wide] for j in range(3)]
        _, vjp = jax.vjp(_fn_post, *parts)
        return vjp(tuple(g[...] for g in g_refs))

    def body(c_ref, cn_ref, dq_ref, dk_ref, dv_ref, dqn_ref, dkn_ref, dvn_ref, dgate_ref, x_ref, xp_ref, w_ref, dx_ref, dw_ref, bufd, bufx):
        i = pl.program_id(0)
        for j, (cur, nxt) in enumerate(zip(conv_cotangent(c_ref, (dq_ref, dk_ref, dv_ref)), conv_cotangent(cn_ref, (dqn_ref, dkn_ref, dvn_ref)))):
            bufd[0:tm, j * wide:(j + 1) * wide] = cur
            bufd[tm:, j * wide:(j + 1) * wide] = jnp.where(i < n - 1, nxt, 0.0)
        bufx[0:HALO, :] = jnp.where(i > 0, xp_ref[...], 0.0)
        bufx[HALO:, :] = x_ref[...]

        @pl.when(i == 0)
        def _():
            dw_ref[...] = jnp.zeros_like(dw_ref)

        dcv = bufd[0:tm, :]
        acc = bufd[pl.ds(CONV_K - 1, tm), :] * w_ref[pl.ds(0, 1), :]
        for j in range(1, CONV_K):
            acc = acc + bufd[pl.ds(CONV_K - 1 - j, tm), :] * w_ref[pl.ds(j, 1), :]
        dx_ref[:, 0:QKV] = acc.astype(dx_ref.dtype)
        dx_ref[:, QKV:] = dgate_ref[...].astype(dx_ref.dtype)
        for j in range(CONV_K):
            dw_ref[pl.ds(j, 1), :] += jnp.sum(dcv * bufx[pl.ds(HALO - CONV_K + 1 + j, tm), :], axis=0, keepdims=True)

    def cur(width):
        return pl.BlockSpec((tm, width), lambda i: (i, 0))

    def nxt(width):
        return pl.BlockSpec((HALO, width), lambda i: (jnp.minimum((i + 1) * (tm // HALO), t // HALO - 1), 0))

    return pl.pallas_call(
        body,
        name="conv_bwd",
        grid=(n,),
        in_specs=[cur(QKV), nxt(QKV)] + [cur(wide)] * 3 + [nxt(wide)] * 3 + [
            cur(wg),
            cur(QKV),
            pl.BlockSpec((HALO, QKV), lambda i: (jnp.maximum(i * (tm // HALO) - 1, 0), 0)),
            pl.BlockSpec((CONV_K, QKV), lambda i: (0, 0)),
        ],
        out_specs=[pl.BlockSpec((tm, QKV + wg), lambda i: (i, 0)), pl.BlockSpec((HALO, QKV), lambda i: (0, 0))],
        out_shape=[jax.ShapeDtypeStruct((t, QKV + wg), BF16), jax.ShapeDtypeStruct((HALO, QKV), F32)],
        scratch_shapes=[pltpu.VMEM((tm + HALO, QKV), F32), pltpu.VMEM((tm + HALO, QKV), F32)],
        compiler_params=_params("arbitrary"),
    )(conv, conv, *dqkv, *dqkv, dgate, qkvg, qkvg, conv_w)


PREP_CHUNKS = 32
PREP_BWD_CHUNKS = 4
SCAN_CHUNKS = 8


def _hi_lo(x):
    hi = x.astype(BF16)
    return hi, (x - hi.astype(F32)).astype(BF16)


def _mm3(a, b, dims=NN):
    (ah, al), (bh, bl) = _hi_lo(a), _hi_lo(b)
    return _dot(ah, bh, dims) + (_dot(ah, bl, dims) + _dot(al, bh, dims))


def _neumann(lowers):
    c = lowers[0].shape[0]
    eye = jnp.where(_iota((c, c), 0) == _iota((c, c), 1), 1.0, 0.0)
    ps = [-low for low in lowers]
    tmats = [eye + p for p in ps]
    for _ in range(CHUNK_SHIFT - 1):
        ps = [_mm3(p, p) for p in ps]
        tmats = [t + _mm3(t, p) for t, p in zip(tmats, ps)]
    return tuple(tmats)


def _inv_cotangents(tmats, dts):
    half = [_mm3(t, dt, TN) for t, dt in zip(tmats, dts)]
    return tuple(-_mm3(hf, t, NT) for hf, t in zip(half, tmats))


@jax.custom_vjp
def _tri_inv(lowers):
    return _neumann(lowers)


def _tri_inv_fwd(lowers):
    tmats = _neumann(lowers)
    return tmats, tmats


_tri_inv.defvjp(_tri_inv_fwd, lambda tmats, dts: (_inv_cotangents(tmats, dts),))


@jax.custom_vjp
def _tri_inv_known(lowers, tmats):
    return tmats


_tri_inv_known.defvjp(lambda lowers, tmats: (tmats, tmats),
                      lambda tmats, dts: (_inv_cotangents(tmats, dts), tuple(jnp.zeros_like(t) for t in tmats)))


def _prep_chunks(qs, ks, vs, bs, gcs, gts, gcrs, tmats=None):
    c = CHUNK
    r, col = _iota((c, c), 0), _iota((c, c), 1)
    incl, strict = r >= col, r > col
    decays = [jnp.where(incl, jnp.exp(jnp.where(incl, gc - gcr, 0.0)), 0.0) for gc, gcr in zip(gcs, gcrs)]
    kbs = [k * b for k, b in zip(ks, bs)]
    kbfs = [k.astype(BF16) for k in ks]
    lowers = tuple(jnp.where(strict, _dot(kb.astype(BF16), kbf, NT) * decay, 0.0) for kb, kbf, decay in zip(kbs, kbfs, decays))
    tmats = _tri_inv(lowers) if tmats is None else _tri_inv_known(lowers, tuple(tmats))
    outs = []
    for q, k, v, b, gc, gt, kb, kbf, decay, tmat in zip(qs, ks, vs, bs, gcs, gts, kbs, kbfs, decays, tmats):
        tb = tmat.astype(BF16)
        egc = jnp.exp(gc)
        w = _dot(tb, (kb * egc).astype(BF16))
        u = _dot(tb, (v * b).astype(BF16))
        attn = _dot(q.astype(BF16), kbf, NT) * decay
        gl = jnp.broadcast_to(jnp.exp(jnp.mean(gt.reshape(c // 8, 8, 1), axis=0)), (8, HEAD_DIM))
        outs.append((w, u, q * egc, k * jnp.exp(gt - gc), attn, gl))
    return tuple(outs), tmats


def _prep_specs(rows, gch):
    head = pl.BlockSpec((rows, HEAD_DIM), lambda n, h: (n, h))
    gates = pl.BlockSpec((rows, LANES), lambda n, h: (n, 0))
    gcrow = pl.BlockSpec((1, gch, 1, CHUNK), lambda n, h: (h, n, 0, 0))
    square = pl.BlockSpec((1, rows, CHUNK), lambda n, h: (h, n, 0))
    gl = pl.BlockSpec((1, gch * 8, HEAD_DIM), lambda n, h: (h, n, 0))
    return head, gates, gcrow, square, gl


def _pick_lane(ref, sl, lane):
    return jnp.sum(jnp.where(_iota((1, LANES), 1) == lane, ref[sl, :], 0.0), axis=1, keepdims=True)


def _prep_inputs(q_ref, k_ref, v_ref, b_ref, gc_ref, gt_ref, gcr_ref, sls, h):
    return ([q_ref[sl, :] for sl in sls], [k_ref[sl, :] for sl in sls], [v_ref[sl, :] for sl in sls],
            [_pick_lane(b_ref, sl, h) for sl in sls], [_pick_lane(gc_ref, sl, h + HEADS) for sl in sls],
            [_pick_lane(gt_ref, sl, h + HEADS) for sl in sls], [gcr_ref[0, c] for c in range(len(sls))])


def _gdn_prep(q, k, v, beta, gc, gt, gcr, shards=()):
    t = q.shape[0]
    gch = min(PREP_CHUNKS, t // CHUNK)
    rows = gch * CHUNK
    steps = t // rows
    n = len(shards)

    def body(*refs):
        q_ref, k_ref, v_ref, b_ref, gc_ref, gt_ref, gcr_ref = refs[:7]
        shard_refs = refs[7:7 + n]
        w_ref, u_ref, qg_ref, kg_ref, at_ref, gl_ref, tm_ref = refs[7 + n:14 + n]
        all_refs, sems = refs[14 + n:14 + 2 * n], refs[14 + 2 * n:]
        h = pl.program_id(1)

        if n:
            @pl.when(jnp.logical_and(pl.program_id(0) == 0, h == 0))
            def _():
                for cp in _gather_sends(shard_refs, all_refs, *sems[:2]):
                    cp.start()

        sls = [pl.ds(c * CHUNK, CHUNK) for c in range(gch)]
        outs, tmats = _prep_chunks(*_prep_inputs(q_ref, k_ref, v_ref, b_ref, gc_ref, gt_ref, gcr_ref, sls, h))
        for c, (sl, (w, u, qg, kg, attn, gl), tmat) in enumerate(zip(sls, outs, tmats)):
            w_ref[sl, :] = w.astype(BF16)
            u_ref[sl, :] = u
            qg_ref[sl, :] = qg.astype(BF16)
            kg_ref[sl, :] = kg.astype(BF16)
            at_ref[0, sl, :] = attn.astype(BF16)
            gl_ref[0, pl.ds(c * 8, 8), :] = gl
            tm_ref[0, sl, :] = tmat

        if n:
            @pl.when(jnp.logical_and(pl.program_id(0) == steps - 1, h == HEADS - 1))
            def _():
                _gather_finish(shard_refs, all_refs, *sems)

    hb, col, gcrow, square, glb = _prep_specs(rows, gch)
    wide = HEADS * HEAD_DIM
    res = pl.pallas_call(
        body,
        name="gdn_prep",
        grid=(steps, HEADS),
        in_specs=[hb, hb, hb, col, col, col, gcrow] + [_HBM] * n,
        out_specs=[hb, hb, hb, hb, square, glb, square] + [_HBM] * n,
        out_shape=[
            jax.ShapeDtypeStruct((t, wide), BF16),
            jax.ShapeDtypeStruct((t, wide), F32),
            jax.ShapeDtypeStruct((t, wide), BF16),
            jax.ShapeDtypeStruct((t, wide), BF16),
            jax.ShapeDtypeStruct((HEADS, t, CHUNK), BF16),
            jax.ShapeDtypeStruct((HEADS, t // CHUNK * 8, HEAD_DIM), F32),
            jax.ShapeDtypeStruct((HEADS, t, CHUNK), F32),
        ] + [jax.ShapeDtypeStruct((N_CHIPS,) + s.shape, s.dtype) for s in shards],
        scratch_shapes=[pltpu.SemaphoreType.DMA((3 * n,))] * (4 if n else 0),
        compiler_params=_params("arbitrary", "arbitrary") if n else _params("parallel", "parallel"),
    )(q, k, v, beta, gc, gt, gcr, *shards)
    return res[:7], res[7:]


def _gdn_prep_bwd(q, k, v, beta, gc, gt, gcr, tmat, dw, du, dqg, dkg, dattn, dgl, partials=()):
    t = q.shape[0]
    gch = min(PREP_BWD_CHUNKS, t // CHUNK)
    rows = gch * CHUNK
    steps = t // rows
    n_sc = len(partials)

    def body(*refs):
        (q_ref, k_ref, v_ref, b_ref, gc_ref, gt_ref, gcr_ref, tm_ref, dw_ref, du_ref, dqg_ref, dkg_ref, dat_ref, dgl_ref) = refs[:14]
        p_refs = refs[14:14 + n_sc]
        dq_ref, dk_ref, dv_ref, db_ref, dgc_ref, dgt_ref, dgcr_ref = refs[14 + n_sc:21 + n_sc]
        from_refs, sems = refs[21 + n_sc:21 + 2 * n_sc], refs[21 + 2 * n_sc:]
        h = pl.program_id(1)
        lane = _iota((1, LANES), 1)

        if n_sc:
            @pl.when(jnp.logical_and(pl.program_id(0) == 0, h == 0))
            def _():
                for cp in _scatter_copies(p_refs, from_refs, *sems):
                    cp.start()

        @pl.when(h == 0)
        def _():
            db_ref[...] = jnp.zeros_like(db_ref)
            dgc_ref[...] = jnp.zeros_like(dgc_ref)
            dgt_ref[...] = jnp.zeros_like(dgt_ref)

        sls = [pl.ds(c * CHUNK, CHUNK) for c in range(gch)]
        known = [tm_ref[0, sl, :] for sl in sls]
        _, vjp = jax.vjp(lambda *a: _prep_chunks(*a, tmats=known)[0], *_prep_inputs(q_ref, k_ref, v_ref, b_ref, gc_ref, gt_ref, gcr_ref, sls, h))
        cots = tuple((dw_ref[sl, :], du_ref[sl, :], dqg_ref[sl, :], dkg_ref[sl, :], dat_ref[0, sl, :], dgl_ref[0, pl.ds(c * 8, 8), :]) for c, sl in enumerate(sls))
        dqs, dks, dvs, dbs, dgcs, dgts, dgcrs = vjp(cots)
        for c, sl in enumerate(sls):
            dq_ref[sl, :] = dqs[c]
            dk_ref[sl, :] = dks[c]
            dv_ref[sl, :] = dvs[c]
            db_ref[sl, :] += jnp.where(lane == h, dbs[c], 0.0)
            dgc_ref[sl, :] += jnp.where(lane == h + HEADS, dgcs[c], 0.0)
            dgt_ref[sl, :] += jnp.where(lane == h + HEADS, dgts[c], 0.0)
            dgcr_ref[0, c] = dgcrs[c]

        if n_sc:
            @pl.when(jnp.logical_and(pl.program_id(0) == steps - 1, h == HEADS - 1))
            def _():
                for cp in _scatter_copies(p_refs, from_refs, *sems):
                    cp.wait()

    hb, col, gcrow, square, glb = _prep_specs(rows, gch)
    wide = HEADS * HEAD_DIM
    res = pl.pallas_call(
        body,
        name="gdn_prep_bwd",
        grid=(steps, HEADS),
        in_specs=[hb, hb, hb, col, col, col, gcrow, square, hb, hb, hb, hb, square, glb] + [_HBM] * n_sc,
        out_specs=[hb, hb, hb, col, col, col, gcrow] + [_HBM] * n_sc,
        out_shape=[jax.ShapeDtypeStruct((t, wide), F32)] * 3 + [jax.ShapeDtypeStruct((t, LANES), F32)] * 3 + [jax.ShapeDtypeStruct((HEADS, t // CHUNK, 1, CHUNK), F32)]
        + [jax.ShapeDtypeStruct((3,) + p.shape[1:], p.dtype) for p in partials],
        scratch_shapes=[pltpu.SemaphoreType.DMA((3 * n_sc,))] * (2 if n_sc else 0),
        compiler_params=_params("arbitrary", "arbitrary"),
    )(q, k, v, beta, gc, gt, gcr, tmat, dw, du, dqg, dkg, dattn, dgl, *partials)
    return res[:7], res[7:]


def _gdn_scan(w, u, qg, kg, attn, gl):
    t = w.shape[0]
    n = t // CHUNK
    nch = min(SCAN_CHUNKS, n)
    wide = HEADS * HEAD_DIM

    def body(w_ref, u_ref, qg_ref, kg_ref, at_ref, gl_ref, o_ref, st_ref, s_ref):
        @pl.when(pl.program_id(0) == 0)
        def _():
            s_ref[...] = jnp.zeros_like(s_ref)

        heads = range(HEADS)
        cols = [pl.ds(h * HEAD_DIM, HEAD_DIM) for h in heads]
        for c in range(nch):
            rows, gl_rows = pl.ds(c * CHUNK, CHUNK), pl.ds(c * 8, 8)
            ss = [s_ref[h] for h in heads]
            sbs = [s.astype(BF16) for s in ss]
            vbs = [(u_ref[rows, hs] - _dot(w_ref[rows, hs], sb)).astype(BF16) for hs, sb in zip(cols, sbs)]
            outs = [_dot(qg_ref[rows, hs], sb) + _dot(at_ref[h, rows, :], vb) for h, hs, sb, vb in zip(heads, cols, sbs, vbs)]
            new = [s * jnp.tile(gl_ref[h, gl_rows, :], (HEAD_DIM // 8, 1)) + _dot(kg_ref[rows, hs], vb, TN) for h, hs, s, vb in zip(heads, cols, ss, vbs)]
            for h, hs in zip(heads, cols):
                st_ref[c, h] = ss[h]
                o_ref[rows, hs] = outs[h]
                s_ref[h] = new[h]

    row = pl.BlockSpec((nch * CHUNK, wide), lambda i: (i, 0))
    return pl.pallas_call(
        body,
        name="gdn_scan",
        grid=(n // nch,),
        in_specs=[row, row, row, row, pl.BlockSpec((HEADS, nch * CHUNK, CHUNK), lambda i: (0, i, 0)), pl.BlockSpec((HEADS, nch * 8, HEAD_DIM), lambda i: (0, i, 0))],
        out_specs=[row, pl.BlockSpec((nch, HEADS, HEAD_DIM, HEAD_DIM), lambda i: (i, 0, 0, 0))],
        out_shape=[jax.ShapeDtypeStruct((t, wide), F32), jax.ShapeDtypeStruct((n, HEADS, HEAD_DIM, HEAD_DIM), F32)],
        scratch_shapes=[pltpu.VMEM((HEADS, HEAD_DIM, HEAD_DIM), F32)],
        compiler_params=_params("arbitrary"),
    )(w, u, qg, kg, attn, gl)


def _gdn_scan_bwd(w, u, qg, kg, attn, gl, states, do, swaps=()):
    t = w.shape[0]
    n = t // CHUNK
    nch = min(SCAN_CHUNKS, n)
    steps = n // nch
    wide = HEADS * HEAD_DIM
    n_sw = len(swaps)

    def body(*refs):
        w_ref, u_ref, qg_ref, kg_ref, at_ref, gl_ref, st_ref, do_ref = refs[:8]
        g_refs = refs[8:8 + n_sw]
        dw_ref, du_ref, dqg_ref, dkg_ref, dat_ref, dgl_ref = refs[8 + n_sw:14 + n_sw]
        a_refs, ds_ref, sems = refs[14 + n_sw:14 + 2 * n_sw], refs[14 + 2 * n_sw], refs[15 + 2 * n_sw:]

        @pl.when(pl.program_id(0) == 0)
        def _():
            ds_ref[...] = jnp.zeros_like(ds_ref)
            for cp in _swap_copies(g_refs, a_refs, *sems) if n_sw else []:
                cp.start()

        heads = range(HEADS)
        cols = [pl.ds(h * HEAD_DIM, HEAD_DIM) for h in heads]
        for c in reversed(range(nch)):
            rows, gl_rows = pl.ds(c * CHUNK, CHUNK), pl.ds(c * 8, 8)
            ss = [st_ref[c, h] for h in heads]
            sbs = [s.astype(BF16) for s in ss]
            dsns = [ds_ref[h] for h in heads]
            dsbs = [d.astype(BF16) for d in dsns]
            dobs = [do_ref[rows, hs].astype(BF16) for hs in cols]
            vbs = [(u_ref[rows, hs] - _dot(w_ref[rows, hs], sb)).astype(BF16) for hs, sb in zip(cols, sbs)]
            dvns = [_dot(at_ref[h, rows, :], dob, TN) + _dot(kg_ref[rows, hs], dsb) for h, hs, dob, dsb in zip(heads, cols, dobs, dsbs)]
            dvbs = [d.astype(BF16) for d in dvns]
            for h, hs in zip(heads, cols):
                dat_ref[h, rows, :] = _dot(dobs[h], vbs[h], NT)
                dqg_ref[rows, hs] = _dot(dobs[h], sbs[h], NT)
                dkg_ref[rows, hs] = _dot(vbs[h], dsbs[h], NT)
                du_ref[rows, hs] = dvns[h]
                dw_ref[rows, hs] = -_dot(dvbs[h], sbs[h], NT)
                dgl_ref[h, gl_rows, :] = jnp.sum((dsns[h] * ss[h]).reshape(HEAD_DIM // 8, 8, HEAD_DIM), axis=0)
            new = [dsn * jnp.tile(gl_ref[h, gl_rows, :], (HEAD_DIM // 8, 1)) + _dot(qg_ref[rows, hs], dob, TN) - _dot(w_ref[rows, hs], dvb, TN)
                   for h, hs, dsn, dob, dvb in zip(heads, cols, dsns, dobs, dvbs)]
            for h in heads:
                ds_ref[h] = new[h]

        if n_sw:
            @pl.when(pl.program_id(0) == steps - 1)
            def _():
                for cp in _swap_copies(g_refs, a_refs, *sems):
                    cp.wait()

    row = pl.BlockSpec((nch * CHUNK, wide), lambda i: (steps - 1 - i, 0))
    at = pl.BlockSpec((HEADS, nch * CHUNK, CHUNK), lambda i: (0, steps - 1 - i, 0))
    glb = pl.BlockSpec((HEADS, nch * 8, HEAD_DIM), lambda i: (0, steps - 1 - i, 0))
    res = pl.pallas_call(
        body,
        name="gdn_scan_bwd",
        grid=(steps,),
        in_specs=[row, row, row, row, at, glb, pl.BlockSpec((nch, HEADS, HEAD_DIM, HEAD_DIM), lambda i: (steps - 1 - i, 0, 0, 0)), row] + [_HBM] * n_sw,
        out_specs=[row, row, row, row, at, glb] + [_HBM] * n_sw,
        out_shape=[jax.ShapeDtypeStruct((t, wide), F32)] * 4 + [jax.ShapeDtypeStruct((HEADS, t, CHUNK), F32), jax.ShapeDtypeStruct((HEADS, n * 8, HEAD_DIM), F32)]
        + [jax.ShapeDtypeStruct(g.shape[1:], g.dtype) for g in swaps],
        scratch_shapes=[pltpu.VMEM((HEADS, HEAD_DIM, HEAD_DIM), F32)] + [pltpu.SemaphoreType.DMA((n_sw,))] * (2 if n_sw else 0),
        compiler_params=_params("arbitrary"),
    )(w, u, qg, kg, attn, gl, states, do, *swaps)
    return res[:6], res[6:]


SB_Q = 512
SB_K = 256
SB_STEP = 1
SB_DEAD = -105.0


def _sb_scores(q, k):
    z = _dot(q, k, NT) * (HEAD_DIM ** -0.5)
    lb = jnp.minimum(z, 0.0) - jnp.log(1.0 + jnp.exp(-jnp.abs(z)))
    return lb, lb - z


def _tri(n, rel):
    return jnp.where(rel(_iota((n, n), 0), _iota((n, n), 1)), 1.0, 0.0).astype(BF16)


def _lanes(col):
    return jnp.broadcast_to(col, (col.shape[0], LANES))


def _sb_fwd(q, k, v):
    t = q.shape[0]
    bq, bk = min(SB_Q, t), min(SB_K, t)
    nsub, rep = bq // bk, bk // LANES
    nstep = min(SB_STEP, nsub)
    steps_per_tile = nsub // nstep

    def body(q_ref, k_ref, v_ref, o_ref, rt_ref, first_ref):
        h = pl.program_id(0)
        i = pl.program_id(1)
        o_ref[...] = jnp.zeros_like(o_ref)
        rt_ref[...] = jnp.zeros_like(rt_ref)
        after = _tri(bk, lambda r, c: r > c)

        def block(j, r0, diag):
            st = pl.multiple_of(j * bk, bk)
            kv, vv = k_ref[pl.ds(st, bk), :], v_ref[pl.ds(st, bk), :]
            lb, l1m = _sb_scores(q_ref[r0:, :], kv)
            if diag:
                mask = _iota((bq - r0, bk), 1) + j * bk < _iota((bq - r0, bk), 0) + (r0 + i * bq)
                l1m = jnp.where(mask, l1m, 0.0)
            sums = _two_pass(l1m, after)
            run = rt_ref[r0:, :]
            a = jnp.exp(lb + jnp.tile(run, (1, rep)) + sums)
            if diag:
                a = jnp.where(mask, a, 0.0)
            o_ref[r0:, :] += _dot(a.astype(BF16), vv)
            rt_ref[r0:, :] = run + _lanes(sums[:, 0:1] + l1m[:, 0:1])

        for s in reversed(range(nsub)):
            block(i * nsub + s, s * bk, True)

        def alive(carry):
            u, highest = carry
            return jnp.logical_and(u >= 0, highest > SB_DEAD)

        def step(carry):
            u, _ = carry
            for s in reversed(range(nstep)):
                block(u * nstep + s, 0, False)
            return u - 1, jnp.max(rt_ref[...])

        u_end, _ = lax.while_loop(alive, step, (i * steps_per_tile - 1, jnp.max(rt_ref[...])))
        first_ref[h, i] = u_end + 1

    qb = pl.BlockSpec((bq, HEAD_DIM), lambda h, i: (i, h))
    full = pl.BlockSpec((t, HEAD_DIM), lambda h, i: (0, h))
    return pl.pallas_call(
        body,
        name="sb_fwd",
        grid=(HEADS, t // bq),
        in_specs=[qb, full, full],
        out_specs=[qb, qb, pl.BlockSpec(memory_space=pltpu.SMEM)],
        out_shape=[jax.ShapeDtypeStruct(q.shape, F32), jax.ShapeDtypeStruct(q.shape, F32), jax.ShapeDtypeStruct((HEADS, t // bq), jnp.int32)],
        compiler_params=_params("arbitrary", "arbitrary"),
    )(q, k, v)


def _sb_bwd(q, k, v, rt, first, do):
    t = q.shape[0]
    bq, bk = min(SB_Q, t), min(SB_K, t)
    nsub, rep = bq // bk, bk // LANES
    nstep = min(SB_STEP, nsub)
    steps_per_tile = nsub // nstep
    scale = HEAD_DIM ** -0.5

    def body(first_ref, q_ref, k_ref, v_ref, rt_ref, do_ref, dq_ref, dk_ref, dv_ref, left_ref, pg_ref):
        h = pl.program_id(0)
        i = pl.program_id(1)

        @pl.when(i == 0)
        def _():
            dk_ref[...] = jnp.zeros_like(dk_ref)
            dv_ref[...] = jnp.zeros_like(dv_ref)

        dq_ref[...] = jnp.zeros_like(dq_ref)
        left_ref[...] = jnp.zeros_like(left_ref)
        pg_ref[...] = jnp.zeros_like(pg_ref)
        upto = _tri(bk, lambda r, c: r <= c)

        def block(j, r0, diag):
            st = pl.multiple_of(j * bk, bk)
            kv, vv = k_ref[pl.ds(st, bk), :], v_ref[pl.ds(st, bk), :]
            qv = q_ref[r0:, :]
            dob = do_ref[r0:, :].astype(BF16)
            lb, l1m = _sb_scores(qv, kv)
            if diag:
                mask = _iota((bq - r0, bk), 1) + j * bk < _iota((bq - r0, bk), 0) + (r0 + i * bq)
                l1m = jnp.where(mask, l1m, 0.0)
            sums = _two_pass(l1m, upto)
            left = left_ref[r0:, :]
            a = jnp.exp(lb + jnp.tile(rt_ref[r0:, :] - left, (1, rep)) - sums)
            if diag:
                a = jnp.where(mask, a, 0.0)
            g = _dot(dob, vv, NT) * a
            dv_ref[pl.ds(st, bk), :] += _dot(a.astype(BF16), dob, TN)
            gsum = _two_pass(g, upto)
            pg = pg_ref[r0:, :]
            dz = g - jnp.exp(lb) * (jnp.tile(pg, (1, rep)) + gsum)
            if diag:
                dz = jnp.where(mask, dz, 0.0)
            dzb = (dz * scale).astype(BF16)
            dk_ref[pl.ds(st, bk), :] += _dot(dzb, qv, TN)
            dq_ref[r0:, :] += _dot(dzb, kv)
            left_ref[r0:, :] = left + _lanes(sums[:, bk - 1:bk])
            pg_ref[r0:, :] = pg + _lanes(gsum[:, bk - 1:bk])

        def step(u, carry):
            for s in range(nstep):
                block(u * nstep + s, 0, False)
            return carry

        lax.fori_loop(first_ref[h, i], i * steps_per_tile, step, 0)
        for s in range(nsub):
            block(i * nsub + s, s * bk, True)

    qb = pl.BlockSpec((bq, HEAD_DIM), lambda h, i: (i, h))
    full = pl.BlockSpec((t, HEAD_DIM), lambda h, i: (0, h))
    return pl.pallas_call(
        body,
        name="sb_bwd",
        grid=(HEADS, t // bq),
        in_specs=[pl.BlockSpec(memory_space=pltpu.SMEM), qb, full, full, qb, qb],
        out_specs=[qb, full, full],
        out_shape=[jax.ShapeDtypeStruct(q.shape, F32)] * 3,
        scratch_shapes=[pltpu.VMEM((bq, LANES), F32), pltpu.VMEM((bq, LANES), F32)],
        compiler_params=_params("arbitrary", "arbitrary"),
    )(first, q, k, v, rt, do)


def _adamw(w, g, m, v, name, tm=ROW_TILE):
    r, c = w.shape
    tm = tm if r % tm == 0 else r

    def body(w_ref, g_ref, m_ref, v_ref, d_ref, nm_ref, nv_ref):
        gv = g_ref[...]
        nm = ADAM_B1 * m_ref[...] + (1.0 - ADAM_B1) * gv
        nv = ADAM_B2 * v_ref[...] + (1.0 - ADAM_B2) * (gv * gv)
        m_hat = nm / (1.0 - ADAM_B1 ** ADAM_STEP)
        v_hat = nv / (1.0 - ADAM_B2 ** ADAM_STEP)
        d_ref[...] = -ADAM_LR * (m_hat / (jnp.sqrt(v_hat) + ADAM_EPS) + ADAM_WD * w_ref[...])
        nm_ref[...] = nm
        nv_ref[...] = nv

    blk = pl.BlockSpec((tm, c), lambda i: (i, 0))
    return pl.pallas_call(
        body,
        name=name,
        grid=(r // tm,),
        in_specs=[blk] * 4,
        out_specs=[blk] * 3,
        out_shape=[jax.ShapeDtypeStruct((r, c), F32)] * 3,
        compiler_params=_params("parallel"),
    )(w, g, m, v)


def _local_step(x, tgt, gains, small, shards, assemble_first, assemble, early_reduce=None, late_reduce=None):
    mix_pre, mix_post, mlp_pre, mlp_post, kv_gain = gains
    a_log, dt_bias, out_gain = small
    t, d = x.shape
    row = lambda a, i=None: a[i:i + 1] if i is not None else a
    al = jnp.zeros((1, LANES), F32).at[:, HEADS:2 * HEADS].set(a_log)
    dtb = jnp.zeros((1, LANES), F32).at[:, HEADS:2 * HEADS].set(dt_bias)
    og = jnp.tile(out_gain, (1, HEADS))
    full = lambda a: (a, a.shape[1], 0)

    h0, *gathered_first = _rowwise("norm_in", _fn_norm, [full(x)], [row(mix_pre, 0)], [(d, BF16)], gather=shards[0])
    w_qkvg, w_ba, conv_w = assemble_first(gathered_first)
    qkvg = _matmul(h0, w_qkvg, "nn", F32, "mm_gdn_in", tk=1024)
    ba = _matmul(h0, w_ba, "nn", F32, "mm_gdn_ba", tk=1024)
    (conv, gq, gk, gv), gathered_conv = _conv_fwd(qkvg, conv_w, shards[1])
    beta, gc, gt = _rowwise("gates", _fn_gates, [full(ba)], [al, dtb], [(LANES, F32)] * 3)
    gcr = jnp.swapaxes(gc[:, HEADS:2 * HEADS], 0, 1).reshape(HEADS, t // CHUNK, 1, CHUNK)
    (pw, pu, pqg, pkg, pattn, pgl, ptm), gathered_prep = _gdn_prep(gq, gk, gv, beta, gc, gt, gcr, shards[2])
    w_out, w_kv, w_q, w_o, w_up, w_down = assemble(gathered_conv, gathered_prep)
    w_qkvg_t, w_up_t = jnp.swapaxes(w_qkvg, -1, -2), jnp.swapaxes(w_up, -1, -2)
    o_gdn, states = _gdn_scan(pw, pu, pqg, pkg, pattn, pgl)
    (on,) = _rowwise("out_norm", _fn_outnorm, [full(o_gdn), (qkvg, d, 3)], [og], [(d, BF16)])
    mix0 = _matmul(on, w_out, "nn", F32, "mm_gdn_out", tk=1024)
    x1, h1 = _rowwise("res_a0", _fn_res_norm, [full(x), full(mix0)], [row(mix_post, 0), row(mlp_pre, 0)], [(d, F32), (d, BF16)])
    (a0,) = _matmul(h1, w_up[0], "nn", (BF16,), "mm_up0", tk=1024, epilogue=_relu2_of)
    d0 = _matmul(a0, w_down[0], "nn", F32, "mm_down0")
    x2, hkv, hq = _rowwise("res_b0", _fn_res_norm2, [full(x1), full(d0)], [row(mlp_post, 0), kv_gain, row(mix_pre, 1)], [(d, F32), (d, BF16), (d, BF16)])
    w_k, w_v = w_kv[:, :d], w_kv[:, d:]
    kp = _matmul(hkv, w_k, "nn", BF16, "mm_k", tk=1024)
    vp = _matmul(hkv, w_v, "nn", BF16, "mm_v", tk=1024)
    qp = _matmul(hq, w_q, "nn", BF16, "mm_q", tk=1024)
    o_sb, rt, sb_first = _sb_fwd(qp, kp, vp)
    mix1 = _matmul(o_sb, w_o, "nn", F32, "mm_sb_out", tk=1024)
    x3, h3 = _rowwise("res_a1", _fn_res_norm, [full(x2), full(mix1)], [row(mix_post, 1), row(mlp_pre, 1)], [(d, F32), (d, BF16)])
    (a1,) = _matmul(h3, w_up[1], "nn", (BF16,), "mm_up1", tk=1024, epilogue=_relu2_of)
    d1 = _matmul(a1, w_down[1], "nn", F32, "mm_down1")

    loss, dx3, dd1, g_mlp_post1 = _loss_call(x3, d1, tgt, row(mlp_post, 1))
    (du1,) = _matmul(dd1, w_down[1], "nt", (BF16,), "mm_down1_dx", epilogue=_relu2_cotangent, extras=[a1])
    up_shape, down_shape = _grad_buffer_shape(_GROUPS[0]), _grad_buffer_shape(_GROUPS[1])
    buf_down = _matmul(a1, dd1, "tn", F32, "mm_down1_dw", into=(down_shape, None, lambda i, j: (1, i)))
    dh3 = _matmul(du1, w_up_t[1], "nn", F32, "mm_up1_dx")
    buf_up = _matmul(h3, du1, "tn", F32, "mm_up1_dw", into=(up_shape, None, lambda i, j: (1, j)))
    (dx2, dmix1), (g_mix_post1, g_mlp_pre1), _ = _rowwise_bwd(
        "res_a1_bwd", _fn_res_norm, [full(x2), full(mix1)], [row(mix_post, 1), row(mlp_pre, 1)], [dx3, dh3], [F32, BF16])
    do_sb = _matmul(dmix1, w_o, "nt", BF16, "mm_sb_out_dx")
    g_o = _matmul(o_sb, dmix1, "tn", F32, "mm_sb_out_dw")
    dqp, dkp, dvp = _sb_bwd(qp, kp, vp, rt, sb_first, do_sb)
    dhq = _matmul(dqp, w_q, "nt", F32, "mm_q_dx")
    g_q = _matmul(hq, dqp, "tn", F32, "mm_q_dw")
    dhkv = _matmul(dvp, w_v, "nt", F32, "mm_v_dx", add=_matmul(dkp, w_k, "nt", F32, "mm_k_dx"))
    g_kv = jnp.concatenate([_matmul(hkv, dkp, "tn", F32, "mm_k_dw"), _matmul(hkv, dvp, "tn", F32, "mm_v_dw")], axis=1)
    (dx1, dd0), (g_mlp_post0, g_kv_gain, g_mix_pre1), _ = _rowwise_bwd(
        "res_b0_bwd", _fn_res_norm2, [full(x1), full(d0)], [row(mlp_post, 0), kv_gain, row(mix_pre, 1)], [dx2, dhkv, dhq], [F32, BF16])
    (du0,) = _matmul(dd0, w_down[0], "nt", (BF16,), "mm_down0_dx", epilogue=_relu2_cotangent, extras=[a0])
    buf_down = _matmul(a0, dd0, "tn", F32, "mm_down0_dw", into=(down_shape, buf_down, lambda i, j: (0, i)))
    dh1 = _matmul(du0, w_up_t[0], "nn", F32, "mm_up0_dx")
    buf_up = _matmul(h1, du0, "tn", F32, "mm_up0_dw", into=(up_shape, buf_up, lambda i, j: (0, j)))
    (dx0, dmix0), (g_mix_post0, g_mlp_pre0), _ = _rowwise_bwd(
        "res_a0_bwd", _fn_res_norm, [full(x), full(mix0)], [row(mix_post, 0), row(mlp_pre, 0)], [dx1, dh1], [F32, BF16])
    don = _matmul(dmix0, w_out, "nt", F32, "mm_gdn_out_dx")
    g_out = _matmul(on, dmix0, "tn", F32, "mm_gdn_out_dw")
    (do_gdn, dgate), (g_og,), _ = _rowwise_bwd("out_norm_bwd", _fn_outnorm, [full(o_gdn), (qkvg, d, 3)], [og], [don], [F32, F32])
    partial, partial_bf16, packed = [], (), ()
    if early_reduce is not None:
        pack, add_sibling = early_reduce
        packed = pack(dict(gdn_w_out=g_out[None], w_kv=g_kv, sb_w_q=g_q[None], sb_w_o=g_o[None]), {0: buf_up, 1: buf_down})
    (dpw, dpu, dpqg, dpkg, dpattn, dpgl), from_sibling = _gdn_scan_bwd(pw, pu, pqg, pkg, pattn, pgl, states, do_gdn, packed)
    if early_reduce is not None:
        partial, partial_bf16 = add_sibling(packed, from_sibling)
    (dgq, dgk, dgv, dbeta, dgc, dgt, dgcr), from_chips = _gdn_prep_bwd(gq, gk, gv, beta, gc, gt, gcr, ptm, dpw, dpu, dpqg, dpkg, dpattn, dpgl, partial_bf16)
    dgcr_lanes = jnp.pad(jnp.swapaxes(dgcr.reshape(HEADS, t), 0, 1), ((0, 0), (HEADS, LANES - 2 * HEADS)))
    gate_cots = [dbeta, dgc + dgcr_lanes, dgt]
    (dba,), (g_al, g_dtb), _ = _rowwise_bwd("gates_bwd", _fn_gates, [full(ba)], [al, dtb], gate_cots, [BF16])
    dqkvg, g_conv = _conv_bwd(conv, (dgq, dgk, dgv), dgate, qkvg, conv_w)
    dh0b = _matmul(dba, w_ba, "nt", F32, "mm_gdn_ba_dx", tk=LANES)
    dh0 = _matmul(dqkvg, w_qkvg_t, "nn", F32, "mm_gdn_in_dx", add=dh0b)
    g_qkvg = _matmul(h0, dqkvg, "tn", F32, "mm_gdn_in_dw")
    g_ba = _matmul(h0, dba, "tn", F32, "mm_gdn_ba_dw")
    g_w_in = jnp.concatenate([g_qkvg, g_ba[:, :2 * HEADS]], axis=1)[None]
    partial_late, partial_late_bf16 = late_reduce(dict(gdn_w_in=g_w_in)) if late_reduce is not None else ([], ())
    (grad_x,), (g_mix_pre0,), from_chips_late = _rowwise_bwd(
        "norm_in_bwd", lambda xx, gg: (_rms(xx, gg), xx), [full(x)], [row(mix_pre, 0)], [dh0, dx0], [F32], partials=partial_late_bf16)

    grads = dict(
        mix_pre_gain=jnp.concatenate([g_mix_pre0, g_mix_pre1], axis=0),
        mix_post_gain=jnp.concatenate([g_mix_post0, g_mix_post1], axis=0),
        mlp_pre_gain=jnp.concatenate([g_mlp_pre0, g_mlp_pre1], axis=0),
        mlp_post_gain=jnp.concatenate([g_mlp_post0, g_mlp_post1], axis=0),
        mlp_w_up=jnp.stack([_join(buf_up[layer, :, :d], "cols") for layer in range(2)]),
        mlp_w_down=jnp.stack([_join(buf_down[layer, :, :d], "rows") for layer in range(2)]),
        gdn_w_in=g_w_in,
        gdn_conv_w=g_conv[None, :CONV_K],
        gdn_a_log=g_al[:, HEADS:2 * HEADS],
        gdn_dt_bias=g_dtb[:, HEADS:2 * HEADS],
        gdn_out_gain=jnp.sum(g_og.reshape(HEADS, HEAD_DIM), axis=0, keepdims=True),
        gdn_w_out=g_out[None],
        kv_gain=g_kv_gain[0],
        w_kv=g_kv,
        sb_w_q=g_q[None],
        sb_w_o=g_o[None],
    )
    return loss, grad_x, grads, (list(partial) + list(partial_late), list(from_chips) + list(from_chips_late))


N_DEV = 8
N_CHIPS = 4
PACK_ROW_TILE = 128

_HBM = pl.BlockSpec(memory_space=pltpu.HBM)


def _place():
    return lax.axis_index("x"), lax.axis_index("y"), lax.axis_index("c")


def _other_chips(x, y):
    return [(1 - x, y), (x, 1 - y), (1 - x, 1 - y)]


def _remote(src, dst, send_sem, recv_sem, to):
    return pltpu.make_async_remote_copy(src_ref=src, dst_ref=dst, send_sem=send_sem, recv_sem=recv_sem, device_id=to, device_id_type=MESH)


def _gather8(v, name):
    rows, cols = v.shape

    def body(v_ref, out_ref, sum_ref, send_sems, recv_sems, local_sem):
        x, y, c = _place()
        me, sibling = (x, y, c), (x, y, 1 - c)
        chips = _other_chips(x, y)

        def blk(px, py, pc):
            return out_ref.at[pl.ds((4 * px + 2 * py + pc) * rows, rows), :]

        def copy(k, block, to, src=None):
            return _remote(blk(*block) if src is None else src, blk(*block), send_sems.at[k], recv_sems.at[k], to)

        mine = pltpu.make_async_copy(v_ref, blk(*me), local_sem)
        mine.start()
        first = [copy(0, me, sibling, src=v_ref)] + [copy(1 + j, me, (*chip, c), src=v_ref) for j, chip in enumerate(chips)]
        for cp in first:
            cp.start()
        passed = [copy(4 + j, (*chip, c), sibling) for j, chip in enumerate(chips)]
        for j, chip in enumerate(chips):
            copy(1 + j, (*chip, c), me).wait_recv()
            passed[j].start()
        copy(0, sibling, me).wait_recv()
        for j, chip in enumerate(chips):
            copy(4 + j, (*chip, 1 - c), me).wait_recv()
        for cp in first + passed:
            cp.wait_send()
        mine.wait()
        acc = out_ref[pl.ds(0, rows), :]
        for dev in range(1, N_DEV):
            acc = acc + out_ref[pl.ds(dev * rows, rows), :]
        sum_ref[...] = acc

    vm = pl.BlockSpec(memory_space=pltpu.VMEM)
    return pl.pallas_call(
        body,
        name=name,
        out_shape=[jax.ShapeDtypeStruct((N_DEV * rows, cols), v.dtype), jax.ShapeDtypeStruct((rows, cols), v.dtype)],
        in_specs=[vm],
        out_specs=[vm, vm],
        scratch_shapes=[pltpu.SemaphoreType.DMA((7,)), pltpu.SemaphoreType.DMA((7,)), pltpu.SemaphoreType.DMA],
    )(v)


def _hbm_call(body, name, arrs, out_shapes, sem_counts):
    n = len(arrs)

    def wrapped(*refs):
        body(refs[:n], refs[n:2 * n], *refs[2 * n:])

    return pl.pallas_call(
        wrapped,
        name=name,
        out_shape=[jax.ShapeDtypeStruct(s, a.dtype) for s, a in zip(out_shapes, arrs)],
        in_specs=[_HBM] * n,
        out_specs=[_HBM] * n,
        scratch_shapes=[pltpu.SemaphoreType.DMA((k,)) for k in sem_counts],
    )(*arrs)


def _gather_sends(w_refs, out_refs, send_sems, recv_sems):
    x, y, c = _place()
    s_me = 2 * x + y
    return [_remote(w.at[c], o.at[s_me, c], send_sems.at[3 * a + j], recv_sems.at[3 * a + j], (px, py, c))
            for a, (w, o) in enumerate(zip(w_refs, out_refs)) for j, (px, py) in enumerate(_other_chips(x, y))]


def _gather_finish(w_refs, out_refs, send_sems, recv_sems, fsend_sems, frecv_sems):
    x, y, c = _place()
    chips = _other_chips(x, y)
    passed = []
    for a, o in enumerate(out_refs):
        for j, (px, py) in enumerate(chips):
            half = o.at[2 * px + py, c]
            _remote(half, half, send_sems.at[3 * a + j], recv_sems.at[3 * a + j], (px, py, c)).wait_recv()
            fwd = _remote(half, half, fsend_sems.at[3 * a + j], frecv_sems.at[3 * a + j], (x, y, 1 - c))
            fwd.start()
            passed.append(fwd)
    for a, o in enumerate(out_refs):
        for j, (px, py) in enumerate(chips):
            half = o.at[2 * px + py, 1 - c]
            _remote(half, half, fsend_sems.at[3 * a + j], frecv_sems.at[3 * a + j], (x, y, 1 - c)).wait_recv()
    for cp in _gather_sends(w_refs, out_refs, send_sems, recv_sems) + passed:
        cp.wait_send()


def _swap_copies(g_refs, a_refs, send_sems, recv_sems):
    x, y, c = _place()
    return [_remote(g.at[1 - c], a, send_sems.at[i], recv_sems.at[i], (x, y, 1 - c)) for i, (g, a) in enumerate(zip(g_refs, a_refs))]


def _swap_halves(arrs, name):
    n = len(arrs)

    def body(g_refs, a_refs, send_sems, recv_sems):
        cps = _swap_copies(g_refs, a_refs, send_sems, recv_sems)
        for cp in cps:
            cp.start()
        for cp in cps:
            cp.wait()

    return _hbm_call(body, name, arrs, [a.shape[1:] for a in arrs], [n, n])


def _scatter_copies(p_refs, b_refs, send_sems, recv_sems):
    x, y, c = _place()
    return [_remote(p.at[2 * px + py], b.at[j], send_sems.at[3 * i + j], recv_sems.at[3 * i + j], (px, py, c))
            for i, (p, b) in enumerate(zip(p_refs, b_refs)) for j, (px, py) in enumerate(_other_chips(x, y))]


def _share_halves(arrs):
    n = len(arrs)

    def body(q_refs, out_refs, send_sems, recv_sems):
        x, y, c = _place()
        cps = [_remote(q, o, send_sems.at[i], recv_sems.at[i], (x, y, 1 - c)) for i, (q, o) in enumerate(zip(q_refs, out_refs))]
        for cp in cps:
            cp.start()
        for cp in cps:
            cp.wait()

    return _hbm_call(body, "grads_share", arrs, [a.shape for a in arrs], [n, n])


_GROUPS = (
    (("mlp_w_up", (2, 1024, 1024), "cols"), ("gdn_w_out", (1, 256, 1024), "rows")),
    (("mlp_w_down", (2, 1024, 1024), "rows"), ("sb_w_q", (1, 256, 1024), "rows"), ("sb_w_o", (1, 256, 1024), "rows")),
    (("w_kv", (1024, 512), "cols"),),
    (("gdn_w_in", (1, 1024, 1028), "cols"),),
)
_BEHIND_CONV, _BEHIND_PREP, _FIRST = slice(0, 1), slice(1, 3), slice(3, 4)
_EARLY_GRADS = slice(0, 3)


def _numel(shape):
    n = 1
    for s in shape:
        n *= s
    return n


def _half_rows(shape):
    return _numel(shape[:-1]) // 2


def _pack_shards(shards, dtype):
    return tuple(jnp.concatenate([shards[n].astype(dtype).reshape(2, _half_rows(shape), shape[-1]) for n, shape, _ in grp], axis=1) for grp in _GROUPS)


def _unpack_shards(bufs):
    out = {}
    for grp, buf in zip(_GROUPS, bufs):
        off = 0
        for n, shape, _ in grp:
            out[n] = buf[:, off:off + _half_rows(shape)].reshape(shape)
            off += _half_rows(shape)
    return out


def _join(stacked, how):
    nd = stacked.ndim - 1
    ax = nd - 1 if how == "cols" else nd - 2
    moved = jnp.moveaxis(stacked, 0, ax)
    shape = list(stacked.shape[1:])
    shape[ax] *= N_CHIPS
    return moved.reshape(shape)


def _split(full, shard_shape, how):
    nd = len(shard_shape)
    ax = nd - 1 if how == "cols" else nd - 2
    shape = list(shard_shape)
    shape.insert(ax, N_CHIPS)
    return jnp.moveaxis(full.reshape(shape), ax, 0)


def _unpack_full(gathered, groups):
    out = {}
    for grp, buf in zip(groups, gathered):
        off = 0
        for n, shape, how in grp:
            out[n] = _join(buf[:, :, off:off + _half_rows(shape)].reshape((N_CHIPS,) + shape), how)
            off += _half_rows(shape)
    return out


def _grad_buffer_shape(grp):
    return (2, N_CHIPS, sum(_half_rows(shape) for _, shape, _ in grp), grp[0][1][-1])


def _pack_full(full, groups, started=None):
    bufs = []
    for gi, grp in enumerate(groups):
        def halves(n, shape, how):
            return jnp.swapaxes(_split(full[n], shape, how).reshape(N_CHIPS, 2, _half_rows(shape), shape[-1]), 0, 1)

        if started is not None and gi in started:
            buf, off = started[gi], _half_rows(grp[0][1])
            for n, shape, how in grp[1:]:
                buf = buf.at[:, :, off:off + _half_rows(shape), :].set(halves(n, shape, how))
                off += _half_rows(shape)
        else:
            buf = jnp.concatenate([halves(n, shape, how) for n, shape, how in grp], axis=2)
        bufs.append(buf.reshape(2, -1, buf.shape[-1]))
    return tuple(bufs)


_SMALL = (
    ("mix_pre_gain", (2, 1024)),
    ("mix_post_gain", (2, 1024)),
    ("mlp_pre_gain", (2, 1024)),
    ("mlp_post_gain", (2, 1024)),
    ("kv_gain", (1024,)),
    ("gdn_out_gain", (1, 128)),
    ("gdn_a_log", (1, 8)),
    ("gdn_dt_bias", (1, 8)),
    ("gdn_conv_w", (1, 4, 3072)),
    ("loss", ()),
)


def _rows_of(shape):
    return -(-_numel(shape) // LANES)


def _pack_rows(vals, layout):
    parts = []
    for n, shape in layout:
        flat = vals[n].reshape(-1)
        parts.append(jnp.pad(flat, (0, _rows_of(shape) * LANES - flat.shape[0])))
    flat = jnp.concatenate(parts)
    rows = -(-flat.shape[0] // (8 * LANES)) * 8
    return jnp.pad(flat, (0, rows * LANES - flat.shape[0])).reshape(rows, LANES)


def _unpack_rows(packed, layout):
    flat = packed.reshape(-1)
    out, off = {}, 0
    for n, shape in layout:
        out[n] = flat[off:off + _numel(shape)].reshape(shape)
        off += _rows_of(shape) * LANES
    return out


_WEIGHTS = ("mix_pre_gain", "mix_post_gain", "mlp_pre_gain", "mlp_post_gain", "mlp_w_up", "mlp_w_down", "gdn_w_in", "gdn_conv_w",
            "gdn_a_log", "gdn_dt_bias", "gdn_out_gain", "gdn_w_out", "kv_gain", "w_kv", "sb_w_q", "sb_w_o")


def _as2d(a):
    return a.reshape(1, -1) if a.ndim <= 1 else a.reshape(-1, a.shape[-1])


def kernel(x, mix_pre_gain, mix_post_gain, mlp_pre_gain, mlp_post_gain, mlp_w_up, mlp_w_down, gdn_w_in, gdn_conv_w, gdn_a_log, gdn_dt_bias, gdn_out_gain, gdn_w_out, kv_gain, w_kv, sb_w_q, sb_w_o, loss_target, m_mix_pre_gain, m_mix_post_gain, m_mlp_pre_gain, m_mlp_post_gain, m_mlp_w_up, m_mlp_w_down, m_gdn_w_in, m_gdn_conv_w, m_gdn_a_log, m_gdn_dt_bias, m_gdn_out_gain, m_gdn_w_out, m_kv_gain, m_w_kv, m_sb_w_q, m_sb_w_o, v_mix_pre_gain, v_mix_post_gain, v_mlp_pre_gain, v_mlp_post_gain, v_mlp_w_up, v_mlp_w_down, v_gdn_w_in, v_gdn_conv_w, v_gdn_a_log, v_gdn_dt_bias, v_gdn_out_gain, v_gdn_w_out, v_kv_gain, v_w_kv, v_sb_w_q, v_sb_w_o):
    w = dict(mix_pre_gain=mix_pre_gain, mix_post_gain=mix_post_gain, mlp_pre_gain=mlp_pre_gain, mlp_post_gain=mlp_post_gain, mlp_w_up=mlp_w_up, mlp_w_down=mlp_w_down, gdn_w_in=gdn_w_in, gdn_conv_w=gdn_conv_w, gdn_a_log=gdn_a_log, gdn_dt_bias=gdn_dt_bias, gdn_out_gain=gdn_out_gain, gdn_w_out=gdn_w_out, kv_gain=kv_gain, w_kv=w_kv, sb_w_q=sb_w_q, sb_w_o=sb_w_o)
    m = dict(mix_pre_gain=m_mix_pre_gain, mix_post_gain=m_mix_post_gain, mlp_pre_gain=m_mlp_pre_gain, mlp_post_gain=m_mlp_post_gain, mlp_w_up=m_mlp_w_up, mlp_w_down=m_mlp_w_down, gdn_w_in=m_gdn_w_in, gdn_conv_w=m_gdn_conv_w, gdn_a_log=m_gdn_a_log, gdn_dt_bias=m_gdn_dt_bias, gdn_out_gain=m_gdn_out_gain, gdn_w_out=m_gdn_w_out, kv_gain=m_kv_gain, w_kv=m_w_kv, sb_w_q=m_sb_w_q, sb_w_o=m_sb_w_o)
    v = dict(mix_pre_gain=v_mix_pre_gain, mix_post_gain=v_mix_post_gain, mlp_pre_gain=v_mlp_pre_gain, mlp_post_gain=v_mlp_post_gain, mlp_w_up=v_mlp_w_up, mlp_w_down=v_mlp_w_down, gdn_w_in=v_gdn_w_in, gdn_conv_w=v_gdn_conv_w, gdn_a_log=v_gdn_a_log, gdn_dt_bias=v_gdn_dt_bias, gdn_out_gain=v_gdn_out_gain, gdn_w_out=v_gdn_w_out, kv_gain=v_kv_gain, w_kv=v_w_kv, sb_w_q=v_sb_w_q, sb_w_o=v_sb_w_o)
    cx, cy, cc = _place()
    chip = 2 * cx + cy
    conv_cols = gdn_conv_w.shape[-1]

    own = _pack_shards(w, BF16)
    own_taps = jnp.pad(gdn_conv_w[0], ((0, CONV_K), (0, 0))).reshape(2, CONV_K, conv_cols)
    with_own = lambda gathered, mine: [lax.dynamic_update_index_in_dim(g, m, chip, 0) for g, m in zip(gathered, mine)]

    def assemble_first(gathered):
        w_in_all, taps_all = with_own(gathered, (*own[_FIRST], own_taps))
        w_in = _unpack_full([w_in_all], _GROUPS[_FIRST])["gdn_w_in"][0]
        taps = jnp.swapaxes(taps_all[:, 0], 0, 1).reshape(CONV_K, N_CHIPS * conv_cols)
        return w_in[:, :4 * HEADS * HEAD_DIM], jnp.pad(w_in[:, 4 * HEADS * HEAD_DIM:], ((0, 0), (0, LANES - 2 * HEADS))), taps

    def assemble(gathered_conv, gathered_prep):
        full = {**_unpack_full(with_own(gathered_conv, own[_BEHIND_CONV]), _GROUPS[_BEHIND_CONV]),
                **_unpack_full(with_own(gathered_prep, own[_BEHIND_PREP]), _GROUPS[_BEHIND_PREP])}
        return full["gdn_w_out"][0], full["w_kv"], full["sb_w_q"][0], full["sb_w_o"][0], full["mlp_w_up"], full["mlp_w_down"]

    gains = (mix_pre_gain, mix_post_gain, mlp_pre_gain, mlp_post_gain, kv_gain[None])
    small = (gdn_a_log, gdn_dt_bias, gdn_out_gain)
    tile = PACK_ROW_TILE

    def add_sibling(bufs, others, tag):
        p32, p16 = [], []
        for i, (buf, other) in enumerate(zip(bufs, others)):
            _, n, cols = buf.shape
            p, pb = _add_rows(f"grads_add_sibling_{tag}{i}", [(buf.reshape(2 * n, cols), cc * (n // tile)), (other, 0)], n, (F32, BF16), tile)
            p32.append(p.reshape(N_CHIPS, -1, cols))
            p16.append(pb.reshape(N_CHIPS, -1, cols))
        return p32, tuple(p16)

    def to_chip_partials(grads_full, groups, tag):
        bufs = _pack_full(grads_full, groups)
        return add_sibling(bufs, _swap_halves(bufs, f"grads_to_sibling_{tag}"), tag)

    loss_rows, grad_x, g_full, (partial, from_chips) = _local_step(
        x[0], loss_target[0], gains, small, ((*own[_FIRST], own_taps), own[_BEHIND_CONV], own[_BEHIND_PREP]), assemble_first, assemble,
        (lambda g, started: _pack_full(g, _GROUPS[_EARLY_GRADS], started), lambda bufs, others: add_sibling(bufs, others, "early")),
        lambda g: to_chip_partials(g, _GROUPS[_FIRST], "late"))

    reduced = []
    for i, (p, others) in enumerate(zip(partial, from_chips)):
        _, r, cols = p.shape
        terms = [(p.reshape(N_CHIPS * r, cols), chip * (r // tile))] + [(others.reshape(3 * r, cols), j * (r // tile)) for j in range(3)]
        reduced.append(_add_rows(f"grads_add_chips_{i}", terms, r, (F32,), tile)[0])
    g_shard = _unpack_shards([jnp.where(cc == 0, jnp.stack([r, o]), jnp.stack([o, r])) for r, o in zip(reduced, _share_halves(tuple(reduced)))])

    g_small_local = {n: g_full[n] for n, _ in _SMALL if n != "loss"}
    g_small_local["loss"] = loss_rows[0, 0]
    _, small_sum = _gather8(_pack_rows(g_small_local, _SMALL), "allreduce_small")
    g_small = _unpack_rows(small_sum, _SMALL)
    loss = g_small.pop("loss")
    g_small["gdn_conv_w"] = lax.dynamic_slice_in_dim(g_small["gdn_conv_w"], chip * conv_cols, conv_cols, axis=2)

    grads = {**g_shard, **g_small}
    deltas, new_m, new_v = {}, {}, {}
    for n in _WEIGHTS:
        d2, m2, v2 = _adamw(_as2d(w[n]), _as2d(grads[n]), _as2d(m[n]), _as2d(v[n]), "adamw_" + n)
        deltas[n], new_m[n], new_v[n] = d2.reshape(w[n].shape), m2.reshape(w[n].shape), v2.reshape(w[n].shape)
    return (loss, grad_x[None], *[grads[n].reshape(w[n].shape) for n in _WEIGHTS], *[deltas[n] for n in _WEIGHTS],
            *[new_m[n] for n in _WEIGHTS], *[new_v[n] for n in _WEIGHTS])
```

```python
import functools

import jax
import jax.numpy as jnp
from jax import lax
from jax.experimental import pallas as pl
from jax.experimental.pallas import tpu as pltpu

F32, BF16 = jnp.float32, jnp.bfloat16
HI = lax.Precision.HIGHEST
MESH = pl.DeviceIdType.MESH

EPS = 1e-6
HEADS = 8
HEAD_DIM = 128
CHUNK = 64
CHUNK_SHIFT = CHUNK.bit_length() - 1
CONV_K = 4
QKV = 3 * HEADS * HEAD_DIM

ADAM_LR, ADAM_B1, ADAM_B2, ADAM_EPS, ADAM_WD, ADAM_STEP = 0.001, 0.9, 0.999, 1e-08, 0.01, 10

VMEM_LIMIT_BYTES = 48 * 1024 * 1024
LANES = 128

NN = ((1,), (0,))
NT = ((1,), (1,))
TN = ((0,), (0,))


def _dot(a, b, dims=NN, precision=None):
    return lax.dot_general(a, b, (dims, ((), ())), precision=precision, preferred_element_type=F32)


def _params(*sem):
    return pltpu.CompilerParams(dimension_semantics=sem, vmem_limit_bytes=VMEM_LIMIT_BYTES)


def _iota(shape, axis):
    return lax.broadcasted_iota(jnp.int32, shape, axis)


def _matmul(a, b, mode, out_dtype, name, tm=1024, tn=1024, tk=2048, add=None, epilogue=None, extras=(), into=None):
    if mode == "nn":
        (m, k), (k2, n) = a.shape, b.shape
    elif mode == "nt":
        (m, k), (n, k2) = a.shape, b.shape
    else:
        (k, m), (k2, n) = a.shape, b.shape
    assert k == k2, (a.shape, b.shape, mode)
    tm, tn, tk = min(tm, m), min(tn, n), min(tk, k)
    assert m % tm == 0 and n % tn == 0 and k % tk == 0, (a.shape, b.shape, mode)
    nk = k // tk
    dims = {"nn": NN, "nt": NT, "tn": TN}[mode]
    tiles = ([add] if add is not None else []) + list(extras)
    out_dtypes = out_dtype if epilogue is not None else (out_dtype,)
    n_in = 2 + len(tiles)
    carried = [into[1]] if into is not None and into[1] is not None else []

    def finish(acc, extra_refs, o_refs):
        res = (acc,) if epilogue is None else epilogue(acc, *[r[...] for r in extra_refs])
        for o_ref, r in zip(o_refs, res):
            o_ref[...] = r.astype(o_ref.dtype)

    def body(*refs):
        a_ref, b_ref = refs[:2]
        extra_refs = refs[n_in - len(extras):n_in]
        o_refs, acc_ref = refs[n_in + len(carried):-1], refs[-1]
        prod = _dot(a_ref[...].astype(BF16), b_ref[...].astype(BF16), dims)
        if nk == 1:
            finish(prod + refs[2][...].astype(F32) if add is not None else prod, extra_refs, o_refs)
            return
        kk = pl.program_id(2)

        @pl.when(kk == 0)
        def _():
            acc_ref[...] = refs[2][...].astype(F32) if add is not None else jnp.zeros_like(acc_ref)

        acc_ref[...] += prod

        @pl.when(kk == nk - 1)
        def _():
            finish(acc_ref[...], extra_refs, o_refs)

    a_spec = pl.BlockSpec((tk, tm), lambda i, j, kk: (kk, i)) if mode == "tn" else pl.BlockSpec((tm, tk), lambda i, j, kk: (i, kk))
    b_spec = pl.BlockSpec((tn, tk), lambda i, j, kk: (j, kk)) if mode == "nt" else pl.BlockSpec((tk, tn), lambda i, j, kk: (kk, j))
    o_spec = pl.BlockSpec((tm, tn), lambda i, j, kk: (i, j))
    if into is None:
        out_specs, out_shape = [o_spec] * len(out_dtypes), [jax.ShapeDtypeStruct((m, n), dt) for dt in out_dtypes]
    else:
        shape, _, place = into
        out_specs = [pl.BlockSpec((None, None, tm, tn), lambda i, j, kk: (*place(i, j), 0, 0))]
        out_shape = [jax.ShapeDtypeStruct(shape, out_dtype)]
    res = pl.pallas_call(
        body,
        name=name,
        grid=(m // tm, n // tn, nk),
        in_specs=[a_spec, b_spec] + [o_spec] * len(tiles) + [pl.BlockSpec(memory_space=pl.ANY)] * len(carried),
        out_specs=out_specs,
        out_shape=out_shape,
        input_output_aliases={n_in: 0} if carried else {},
        scratch_shapes=[pltpu.VMEM((tm, tn), F32)],
        compiler_params=_params("parallel", "parallel", "arbitrary"),
    )(a, b, *tiles, *carried)
    return res if epilogue is not None else res[0]


def _row_specs(rows, tm):
    return [pl.BlockSpec((tm, w), lambda i, cb=cb: (i, cb)) for _, w, cb in rows]


def _full_spec(p):
    return pl.BlockSpec(p.shape, lambda i: (0,) * p.ndim)


ROW_TILE = 512


def _rowwise(name, fn, rows, params, outs, tm=ROW_TILE, gather=()):
    t = rows[0][0].shape[0]
    tm = min(tm, t)
    steps = t // tm
    nr, npar, nout, ng = len(rows), len(params), len(outs), len(gather)

    def body(*refs):
        ins = [r[...].astype(F32) for r in refs[:nr]]
        ps = [p[...] for p in refs[nr:nr + npar]]
        shard_refs = refs[nr + npar:nr + npar + ng]
        o_refs = refs[nr + npar + ng:nr + npar + ng + nout]
        all_refs, sems = refs[nr + npar + ng + nout:nr + npar + 2 * ng + nout], refs[nr + npar + 2 * ng + nout:]
        if ng:
            @pl.when(pl.program_id(0) == 0)
            def _():
                for cp in _gather_sends(shard_refs, all_refs, *sems[:2]):
                    cp.start()

        res = fn(*ins, *ps)
        for o_ref, r in zip(o_refs, res):
            o_ref[...] = r.astype(o_ref.dtype)

        if ng:
            @pl.when(pl.program_id(0) == steps - 1)
            def _():
                _gather_finish(shard_refs, all_refs, *sems)

    return pl.pallas_call(
        body,
        name=name,
        grid=(steps,),
        in_specs=_row_specs(rows, tm) + [_full_spec(p) for p in params] + [_HBM] * ng,
        out_specs=[pl.BlockSpec((tm, w), lambda i: (i, 0)) for w, _ in outs] + [_HBM] * ng,
        out_shape=[jax.ShapeDtypeStruct((t, w), dt) for w, dt in outs] + [jax.ShapeDtypeStruct((N_CHIPS,) + s.shape, s.dtype) for s in gather],
        scratch_shapes=[pltpu.SemaphoreType.DMA((3 * ng,))] * (4 if ng else 0),
        compiler_params=_params("arbitrary" if ng else "parallel"),
    )(*[r[0] for r in rows], *params, *gather)


def _add_rows(name, terms, n_rows, out_dtypes, tm):
    cols = terms[0][0].shape[1]
    firsts = jnp.stack([jnp.asarray(first, jnp.int32) for _, first in terms])

    def body(firsts_ref, *refs):
        acc = refs[0][...].astype(F32)
        for r in refs[1:len(terms)]:
            acc = acc + r[...].astype(F32)
        for o_ref in refs[len(terms):]:
            o_ref[...] = acc.astype(o_ref.dtype)

    return pl.pallas_call(
        body,
        name=name,
        grid_spec=pltpu.PrefetchScalarGridSpec(
            num_scalar_prefetch=1,
            grid=(n_rows // tm,),
            in_specs=[pl.BlockSpec((tm, cols), lambda i, firsts_ref, k=k: (firsts_ref[k] + i, 0)) for k in range(len(terms))],
            out_specs=[pl.BlockSpec((tm, cols), lambda i, firsts_ref: (i, 0)) for _ in out_dtypes],
        ),
        out_shape=[jax.ShapeDtypeStruct((n_rows, cols), dt) for dt in out_dtypes],
        compiler_params=_params("parallel"),
    )(firsts, *[a for a, _ in terms])


def _rowwise_bwd(name, fn, rows, params, cots, grad_dtypes, tm=ROW_TILE, partials=(), shares=()):
    t = rows[0][0].shape[0]
    tm = min(tm, t)
    steps = t // tm
    nr, npar, nc, nsc, nsh = len(rows), len(params), len(cots), len(partials), len(shares)
    want = [j for j, dt in enumerate(grad_dtypes) if dt is not None]
    widths = [rows[j][1] for j in want]
    n_row_outs = len(want)
    n_in = nr + npar + nc

    def body(*refs):
        i = pl.program_id(0)
        ins = [r[...].astype(F32) for r in refs[:nr]]
        ps = [p[...] for p in refs[nr:nr + npar]]
        cs = tuple(c[...].astype(F32) for c in refs[nr + npar:n_in])
        p_refs, q_refs = refs[n_in:n_in + nsc], refs[n_in + nsc:n_in + nsc + nsh]
        first_out = n_in + nsc + nsh
        outs = refs[first_out:first_out + n_row_outs + npar]
        first_from = first_out + n_row_outs + npar
        from_refs, shared_refs, sems = refs[first_from:first_from + nsc], refs[first_from + nsc:first_from + nsc + nsh], refs[first_from + nsc + nsh:]

        def hosted():
            scatter = _scatter_copies(p_refs, from_refs, *sems[:2]) if nsc else []
            return scatter + (_share_copies(q_refs, shared_refs, *sems[-2:]) if nsh else [])

        if nsc or nsh:
            @pl.when(i == 0)
            def _():
                for cp in hosted():
                    cp.start()

        _, vjp = jax.vjp(fn, *ins, *ps)
        gs = vjp(cs)
        for o_ref, j in zip(outs, want):
            o_ref[...] = gs[j].astype(o_ref.dtype)
        pg_refs = outs[n_row_outs:]

        @pl.when(i == 0)
        def _():
            for pg in pg_refs:
                pg[...] = jnp.zeros_like(pg)

        for pg, g in zip(pg_refs, gs[nr:]):
            pg[...] += g

        if nsc or nsh:
            @pl.when(i == steps - 1)
            def _():
                for cp in hosted():
                    cp.wait()

    row_specs = [pl.BlockSpec((tm, w), lambda i: (i, 0)) for w in widths]
    row_shapes = [jax.ShapeDtypeStruct((t, w), grad_dtypes[j]) for j, w in zip(want, widths)]
    res = pl.pallas_call(
        body,
        name=name,
        grid=(steps,),
        in_specs=_row_specs(rows, tm) + [_full_spec(p) for p in params] + [pl.BlockSpec((tm, c.shape[1]), lambda i: (i, 0)) for c in cots] + [_HBM] * (nsc + nsh),
        out_specs=row_specs + [_full_spec(p) for p in params] + [_HBM] * (nsc + nsh),
        out_shape=row_shapes + [jax.ShapeDtypeStruct(p.shape, F32) for p in params] + [jax.ShapeDtypeStruct((3,) + p.shape[1:], p.dtype) for p in partials]
        + [jax.ShapeDtypeStruct(q.shape, q.dtype) for q in shares],
        scratch_shapes=[pltpu.SemaphoreType.DMA((3 * nsc,))] * (2 if nsc else 0) + [pltpu.SemaphoreType.DMA((nsh,))] * (2 if nsh else 0),
        compiler_params=_params("arbitrary"),
    )(*[r[0] for r in rows], *params, *cots, *partials, *shares)
    return res[:n_row_outs], res[n_row_outs:n_row_outs + npar], res[n_row_outs + npar:]


def _rms(x, g):
    return x * lax.rsqrt(jnp.mean(x * x, axis=-1, keepdims=True) + EPS) * g


def _sigmoid(x):
    return 1.0 / (1.0 + jnp.exp(-x))


def _softplus(x):
    return jnp.maximum(x, 0.0) + jnp.log1p(jnp.exp(-jnp.abs(x)))


def _two_pass(x, m):
    hi = x.astype(BF16)
    lo = (x - hi.astype(F32)).astype(BF16)
    return _dot(hi, m) + _dot(lo, m)


def _head_sum_impl(x):
    sums = [jnp.sum(x[:, h * HEAD_DIM:(h + 1) * HEAD_DIM], axis=-1, keepdims=True) for h in range(HEADS)]
    return jnp.concatenate([jnp.broadcast_to(s, (x.shape[0], HEAD_DIM)) for s in sums], axis=1)


@jax.custom_vjp
def _head_sum(x):
    return _head_sum_impl(x)


_head_sum.defvjp(lambda x: (_head_sum_impl(x), None), lambda _, g: (_head_sum_impl(g),))


def _fn_norm(x, g):
    return (_rms(x, g),)


def _fn_gates(ba, al, dt):
    col = _iota((1, LANES), 1)
    g = jnp.where((col >= HEADS) & (col < 2 * HEADS), -jnp.exp(al) * _softplus(ba + dt), 0.0)
    rows = ba.shape[0]
    r, c = _iota((rows, rows), 0), _iota((rows, rows), 1)
    same = (r >> CHUNK_SHIFT) == (c >> CHUNK_SHIFT)
    gc = _dot(jnp.where(same & (r >= c), 1.0, 0.0), g, precision=HI)
    gtot = _dot(jnp.where(same, 1.0, 0.0), g, precision=HI)
    return _sigmoid(ba), gc, gtot


def _fn_post_q(c):
    s = c * _sigmoid(c)
    return (s * lax.rsqrt(_head_sum(s * s) + EPS) * (HEAD_DIM ** -0.5),)


def _fn_post_k(c):
    s = c * _sigmoid(c)
    return (s * lax.rsqrt(_head_sum(s * s) + EPS),)


def _fn_post_v(c):
    return (c * _sigmoid(c),)


def _fn_post(cq, ck, cv):
    return _fn_post_q(cq) + _fn_post_k(ck) + _fn_post_v(cv)


def _fn_outnorm(o, gate, og):
    y = o * lax.rsqrt(_head_sum(o * o) * (1.0 / HEAD_DIM) + EPS) * og
    return (y * (gate * _sigmoid(gate)),)


def _fn_res_norm(x, m, gp, gn):
    x1 = x + _rms(m, gp)
    return x1, _rms(x1, gn)


def _fn_res_norm2(x, m, gp, ga, gb):
    x1 = x + _rms(m, gp)
    return x1, _rms(x1, ga), _rms(x1, gb)


def _relu2_of(u):
    r = jnp.maximum(u, 0.0)
    return (r * r,)


def _relu2_cotangent(da, a):
    return (da * (2.0 * jnp.sqrt(a.astype(F32))),)


def _loss_call(x3, d1, tgt, g, tm=ROW_TILE):
    t, d = x3.shape
    tm = min(tm, t)

    def body(x_ref, d_ref, t_ref, g_ref, loss_ref, dx_ref, dd_ref, dg_ref):
        i = pl.program_id(0)
        y, vjp = jax.vjp(lambda x, dd, gg: x + _rms(dd, gg), x_ref[...], d_ref[...], g_ref[...])
        err = y - t_ref[...]
        lrow = 0.5 * jnp.mean(err * err, axis=-1, keepdims=True)
        dx, dd, dg = vjp(err * (1.0 / d))
        dx_ref[...] = dx
        dd_ref[...] = dd.astype(dd_ref.dtype)

        @pl.when(i == 0)
        def _():
            loss_ref[...] = jnp.zeros_like(loss_ref)
            dg_ref[...] = jnp.zeros_like(dg_ref)

        loss_ref[...] += jnp.broadcast_to(jnp.sum(lrow, axis=0, keepdims=True), loss_ref.shape)
        dg_ref[...] += dg

    row = pl.BlockSpec((tm, d), lambda i: (i, 0))
    return pl.pallas_call(
        body,
        name="loss_head",
        grid=(t // tm,),
        in_specs=[row, row, row, _full_spec(g)],
        out_specs=[pl.BlockSpec((8, LANES), lambda i: (0, 0)), row, row, _full_spec(g)],
        out_shape=[jax.ShapeDtypeStruct((8, LANES), F32), jax.ShapeDtypeStruct((t, d), F32), jax.ShapeDtypeStruct((t, d), BF16), jax.ShapeDtypeStruct(g.shape, F32)],
        compiler_params=_params("arbitrary"),
    )(x3, d1, tgt, g)


HALO = 8


def _conv_fwd(qkvg, conv_w, shards, tm=256):
    t = qkvg.shape[0]
    tm = min(tm, t)
    steps = t // tm
    wide = QKV // 3
    n = len(shards)

    def body(*refs):
        cur_ref, prev_ref, w_ref = refs[:3]
        shard_refs = refs[3:3 + n]
        o_ref, q_ref, k_ref, v_ref = refs[3 + n:7 + n]
        all_refs = refs[7 + n:7 + 2 * n]
        buf, sems = refs[7 + 2 * n], refs[8 + 2 * n:]
        i = pl.program_id(0)

        if n:
            @pl.when(i == 0)
            def _():
                for cp in _gather_sends(shard_refs, all_refs, *sems[:2]):
                    cp.start()

        buf[0:HALO, :] = jnp.where(i > 0, prev_ref[...], 0.0)
        buf[HALO:, :] = cur_ref[...]
        acc = buf[pl.ds(HALO - CONV_K + 1, tm), :] * w_ref[pl.ds(0, 1), :]
        for j in range(1, CONV_K):
            acc = acc + buf[pl.ds(HALO - CONV_K + 1 + j, tm), :] * w_ref[pl.ds(j, 1), :]
        o_ref[...] = acc
        (q_ref[...], k_ref[...], v_ref[...]) = _fn_post(acc[:, 0:wide], acc[:, wide:2 * wide], acc[:, 2 * wide:])

        if n:
            @pl.when(i == steps - 1)
            def _():
                _gather_finish(shard_refs, all_refs, *sems)

    part = pl.BlockSpec((tm, wide), lambda i: (i, 0))
    res = pl.pallas_call(
        body,
        name="conv_fwd",
        grid=(steps,),
        in_specs=[
            pl.BlockSpec((tm, QKV), lambda i: (i, 0)),
            pl.BlockSpec((HALO, QKV), lambda i: (jnp.maximum(i * (tm // HALO) - 1, 0), 0)),
            pl.BlockSpec((CONV_K, QKV), lambda i: (0, 0)),
        ] + [_HBM] * n,
        out_specs=[pl.BlockSpec((tm, QKV), lambda i: (i, 0)), part, part, part] + [_HBM] * n,
        out_shape=[jax.ShapeDtypeStruct((t, QKV), F32)] + [jax.ShapeDtypeStruct((t, wide), F32)] * 3
        + [jax.ShapeDtypeStruct((N_CHIPS,) + s.shape, s.dtype) for s in shards],
        scratch_shapes=[pltpu.VMEM((tm + HALO, QKV), F32)] + [pltpu.SemaphoreType.DMA((3 * n,))] * (4 if n else 0),
        compiler_params=_params("arbitrary"),
    )(qkvg, qkvg, conv_w, *shards)
    return res[:4], res[4:]


def _conv_bwd(conv, dqkv, dgate, qkvg, conv_w, tm=256):
    t = conv.shape[0]
    tm = min(tm, t)
    n = t // tm
    wg = dgate.shape[1]
    wide = QKV // 3

    def conv_cotangent(c_ref, g_refs):
        parts = [c_ref[:, j * wide:(j + 1) * wide] for j in range(3)]
        _, vjp = jax.vjp(_fn_post, *parts)
        return vjp(tuple(g[...] for g in g_refs))

    def body(c_ref, cn_ref, dq_ref, dk_ref, dv_ref, dqn_ref, dkn_ref, dvn_ref, dgate_ref, x_ref, xp_ref, w_ref, dx_ref, dw_ref, bufd, bufx):
        i = pl.program_id(0)
        for j, (cur, nxt) in enumerate(zip(conv_cotangent(c_ref, (dq_ref, dk_ref, dv_ref)), conv_cotangent(cn_ref, (dqn_ref, dkn_ref, dvn_ref)))):
            bufd[0:tm, j * wide:(j + 1) * wide] = cur
            bufd[tm:, j * wide:(j + 1) * wide] = jnp.where(i < n - 1, nxt, 0.0)
        bufx[0:HALO, :] = jnp.where(i > 0, xp_ref[...], 0.0)
        bufx[HALO:, :] = x_ref[...]

        @pl.when(i == 0)
        def _():
            dw_ref[...] = jnp.zeros_like(dw_ref)

        dcv = bufd[0:tm, :]
        acc = bufd[pl.ds(CONV_K - 1, tm), :] * w_ref[pl.ds(0, 1), :]
        for j in range(1, CONV_K):
            acc = acc + bufd[pl.ds(CONV_K - 1 - j, tm), :] * w_ref[pl.ds(j, 1), :]
        dx_ref[:, 0:QKV] = acc.astype(dx_ref.dtype)
        dx_ref[:, QKV:] = dgate_ref[...].astype(dx_ref.dtype)
        for j in range(CONV_K):
            dw_ref[pl.ds(j, 1), :] += jnp.sum(dcv * bufx[pl.ds(HALO - CONV_K + 1 + j, tm), :], axis=0, keepdims=True)

    def cur(width):
        return pl.BlockSpec((tm, width), lambda i: (i, 0))

    def nxt(width):
        return pl.BlockSpec((HALO, width), lambda i: (jnp.minimum((i + 1) * (tm // HALO), t // HALO - 1), 0))

    return pl.pallas_call(
        body,
        name="conv_bwd",
        grid=(n,),
        in_specs=[cur(QKV), nxt(QKV)] + [cur(wide)] * 3 + [nxt(wide)] * 3 + [
            cur(wg),
            cur(QKV),
            pl.BlockSpec((HALO, QKV), lambda i: (jnp.maximum(i * (tm // HALO) - 1, 0), 0)),
            pl.BlockSpec((CONV_K, QKV), lambda i: (0, 0)),
        ],
        out_specs=[pl.BlockSpec((tm, QKV + wg), lambda i: (i, 0)), pl.BlockSpec((HALO, QKV), lambda i: (0, 0))],
        out_shape=[jax.ShapeDtypeStruct((t, QKV + wg), BF16), jax.ShapeDtypeStruct((HALO, QKV), F32)],
        scratch_shapes=[pltpu.VMEM((tm + HALO, QKV), F32), pltpu.VMEM((tm + HALO, QKV), F32)],
        compiler_params=_params("arbitrary"),
    )(conv, conv, *dqkv, *dqkv, dgate, qkvg, qkvg, conv_w)


PREP_CHUNKS = 32
PREP_BWD_CHUNKS = 4
SCAN_CHUNKS = 8


def _hi_lo(x):
    hi = x.astype(BF16)
    return hi, (x - hi.astype(F32)).astype(BF16)


def _mm3(a, b, dims=NN):
    (ah, al), (bh, bl) = _hi_lo(a), _hi_lo(b)
    return _dot(ah, bh, dims) + (_dot(ah, bl, dims) + _dot(al, bh, dims))


def _neumann(lowers):
    c = lowers[0].shape[0]
    eye = jnp.where(_iota((c, c), 0) == _iota((c, c), 1), 1.0, 0.0)
    ps = [-low for low in lowers]
    tmats = [eye + p for p in ps]
    for _ in range(CHUNK_SHIFT - 1):
        ps = [_mm3(p, p) for p in ps]
        tmats = [t + _mm3(t, p) for t, p in zip(tmats, ps)]
    return tuple(tmats)


def _inv_cotangents(tmats, dts):
    half = [_mm3(t, dt, TN) for t, dt in zip(tmats, dts)]
    return tuple(-_mm3(hf, t, NT) for hf, t in zip(half, tmats))


@jax.custom_vjp
def _tri_inv(lowers):
    return _neumann(lowers)


def _tri_inv_fwd(lowers):
    tmats = _neumann(lowers)
    return tmats, tmats


_tri_inv.defvjp(_tri_inv_fwd, lambda tmats, dts: (_inv_cotangents(tmats, dts),))


@jax.custom_vjp
def _tri_inv_known(lowers, tmats):
    return tmats


_tri_inv_known.defvjp(lambda lowers, tmats: (tmats, tmats),
                      lambda tmats, dts: (_inv_cotangents(tmats, dts), tuple(jnp.zeros_like(t) for t in tmats)))


def _prep_chunks(qs, ks, vs, bs, gcs, gts, gcrs, tmats=None):
    c = CHUNK
    r, col = _iota((c, c), 0), _iota((c, c), 1)
    incl, strict = r >= col, r > col
    decays = [jnp.where(incl, jnp.exp(jnp.where(incl, gc - gcr, 0.0)), 0.0) for gc, gcr in zip(gcs, gcrs)]
    kbs = [k * b for k, b in zip(ks, bs)]
    kbfs = [k.astype(BF16) for k in ks]
    lowers = tuple(jnp.where(strict, _dot(kb.astype(BF16), kbf, NT) * decay, 0.0) for kb, kbf, decay in zip(kbs, kbfs, decays))
    tmats = _tri_inv(lowers) if tmats is None else _tri_inv_known(lowers, tuple(tmats))
    outs = []
    for q, k, v, b, gc, gt, kb, kbf, decay, tmat in zip(qs, ks, vs, bs, gcs, gts, kbs, kbfs, decays, tmats):
        tb = tmat.astype(BF16)
        egc = jnp.exp(gc)
        w = _dot(tb, (kb * egc).astype(BF16))
        u = _dot(tb, (v * b).astype(BF16))
        attn = _dot(q.astype(BF16), kbf, NT) * decay
        gl = jnp.broadcast_to(jnp.exp(jnp.mean(gt.reshape(c // 8, 8, 1), axis=0)), (8, HEAD_DIM))
        outs.append((w, u, q * egc, k * jnp.exp(gt - gc), attn, gl))
    return tuple(outs), tmats


def _prep_specs(rows, gch):
    head = pl.BlockSpec((rows, HEAD_DIM), lambda n, h: (n, h))
    gates = pl.BlockSpec((rows, LANES), lambda n, h: (n, 0))
    gcrow = pl.BlockSpec((1, gch, 1, CHUNK), lambda n, h: (h, n, 0, 0))
    square = pl.BlockSpec((1, rows, CHUNK), lambda n, h: (h, n, 0))
    gl = pl.BlockSpec((1, gch * 8, HEAD_DIM), lambda n, h: (h, n, 0))
    return head, gates, gcrow, square, gl


def _pick_lane(ref, sl, lane):
    return jnp.sum(jnp.where(_iota((1, LANES), 1) == lane, ref[sl, :], 0.0), axis=1, keepdims=True)


def _prep_inputs(q_ref, k_ref, v_ref, b_ref, gc_ref, gt_ref, gcr_ref, sls, h):
    return ([q_ref[sl, :] for sl in sls], [k_ref[sl, :] for sl in sls], [v_ref[sl, :] for sl in sls],
            [_pick_lane(b_ref, sl, h) for sl in sls], [_pick_lane(gc_ref, sl, h + HEADS) for sl in sls],
            [_pick_lane(gt_ref, sl, h + HEADS) for sl in sls], [gcr_ref[0, c] for c in range(len(sls))])


def _gdn_prep(q, k, v, beta, gc, gt, gcr, shards=()):
    t = q.shape[0]
    gch = min(PREP_CHUNKS, t // CHUNK)
    rows = gch * CHUNK
    steps = t // rows
    n = len(shards)

    def body(*refs):
        q_ref, k_ref, v_ref, b_ref, gc_ref, gt_ref, gcr_ref = refs[:7]
        shard_refs = refs[7:7 + n]
        w_ref, u_ref, qg_ref, kg_ref, at_ref, gl_ref, tm_ref = refs[7 + n:14 + n]
        all_refs, sems = refs[14 + n:14 + 2 * n], refs[14 + 2 * n:]
        h = pl.program_id(1)

        if n:
            @pl.when(jnp.logical_and(pl.program_id(0) == 0, h == 0))
            def _():
                for cp in _gather_sends(shard_refs, all_refs, *sems[:2]):
                    cp.start()

        sls = [pl.ds(c * CHUNK, CHUNK) for c in range(gch)]
        outs, tmats = _prep_chunks(*_prep_inputs(q_ref, k_ref, v_ref, b_ref, gc_ref, gt_ref, gcr_ref, sls, h))
        for c, (sl, (w, u, qg, kg, attn, gl), tmat) in enumerate(zip(sls, outs, tmats)):
            w_ref[sl, :] = w.astype(BF16)
            u_ref[sl, :] = u
            qg_ref[sl, :] = qg.astype(BF16)
            kg_ref[sl, :] = kg.astype(BF16)
            at_ref[0, sl, :] = attn.astype(BF16)
            gl_ref[0, pl.ds(c * 8, 8), :] = gl
            tm_ref[0, sl, :] = tmat

        if n:
            @pl.when(jnp.logical_and(pl.program_id(0) == steps - 1, h == HEADS - 1))
            def _():
                _gather_finish(shard_refs, all_refs, *sems)

    hb, col, gcrow, square, glb = _prep_specs(rows, gch)
    wide = HEADS * HEAD_DIM
    res = pl.pallas_call(
        body,
        name="gdn_prep",
        grid=(steps, HEADS),
        in_specs=[hb, hb, hb, col, col, col, gcrow] + [_HBM] * n,
        out_specs=[hb, hb, hb, hb, square, glb, square] + [_HBM] * n,
        out_shape=[
            jax.ShapeDtypeStruct((t, wide), BF16),
            jax.ShapeDtypeStruct((t, wide), F32),
            jax.ShapeDtypeStruct((t, wide), BF16),
            jax.ShapeDtypeStruct((t, wide), BF16),
            jax.ShapeDtypeStruct((HEADS, t, CHUNK), BF16),
            jax.ShapeDtypeStruct((HEADS, t // CHUNK * 8, HEAD_DIM), F32),
            jax.ShapeDtypeStruct((HEADS, t, CHUNK), F32),
        ] + [jax.ShapeDtypeStruct((N_CHIPS,) + s.shape, s.dtype) for s in shards],
        scratch_shapes=[pltpu.SemaphoreType.DMA((3 * n,))] * (4 if n else 0),
        compiler_params=_params("arbitrary", "arbitrary") if n else _params("parallel", "parallel"),
    )(q, k, v, beta, gc, gt, gcr, *shards)
    return res[:7], res[7:]


def _gdn_prep_bwd(q, k, v, beta, gc, gt, gcr, tmat, dw, du, dqg, dkg, dattn, dgl, partials=()):
    t = q.shape[0]
    gch = min(PREP_BWD_CHUNKS, t // CHUNK)
    rows = gch * CHUNK
    steps = t // rows
    n_sc = len(partials)

    def body(*refs):
        (q_ref, k_ref, v_ref, b_ref, gc_ref, gt_ref, gcr_ref, tm_ref, dw_ref, du_ref, dqg_ref, dkg_ref, dat_ref, dgl_ref) = refs[:14]
        p_refs = refs[14:14 + n_sc]
        dq_ref, dk_ref, dv_ref, db_ref, dgc_ref, dgt_ref, dgcr_ref = refs[14 + n_sc:21 + n_sc]
        from_refs, sems = refs[21 + n_sc:21 + 2 * n_sc], refs[21 + 2 * n_sc:]
        h = pl.program_id(1)
        lane = _iota((1, LANES), 1)

        if n_sc:
            @pl.when(jnp.logical_and(pl.program_id(0) == 0, h == 0))
            def _():
                for cp in _scatter_copies(p_refs, from_refs, *sems):
                    cp.start()

        @pl.when(h == 0)
        def _():
            db_ref[...] = jnp.zeros_like(db_ref)
            dgc_ref[...] = jnp.zeros_like(dgc_ref)
            dgt_ref[...] = jnp.zeros_like(dgt_ref)

        sls = [pl.ds(c * CHUNK, CHUNK) for c in range(gch)]
        known = [tm_ref[0, sl, :] for sl in sls]
        _, vjp = jax.vjp(lambda *a: _prep_chunks(*a, tmats=known)[0], *_prep_inputs(q_ref, k_ref, v_ref, b_ref, gc_ref, gt_ref, gcr_ref, sls, h))
        cots = tuple((dw_ref[sl, :], du_ref[sl, :], dqg_ref[sl, :], dkg_ref[sl, :], dat_ref[0, sl, :], dgl_ref[0, pl.ds(c * 8, 8), :]) for c, sl in enumerate(sls))
        dqs, dks, dvs, dbs, dgcs, dgts, dgcrs = vjp(cots)
        for c, sl in enumerate(sls):
            dq_ref[sl, :] = dqs[c]
            dk_ref[sl, :] = dks[c]
            dv_ref[sl, :] = dvs[c]
            db_ref[sl, :] += jnp.where(lane == h, dbs[c], 0.0)
            dgc_ref[sl, :] += jnp.where(lane == h + HEADS, dgcs[c], 0.0)
            dgt_ref[sl, :] += jnp.where(lane == h + HEADS, dgts[c], 0.0)
            dgcr_ref[0, c] = dgcrs[c]

        if n_sc:
            @pl.when(jnp.logical_and(pl.program_id(0) == steps - 1, h == HEADS - 1))
            def _():
                for cp in _scatter_copies(p_refs, from_refs, *sems):
                    cp.wait()

    hb, col, gcrow, square, glb = _prep_specs(rows, gch)
    wide = HEADS * HEAD_DIM
    res = pl.pallas_call(
        body,
        name="gdn_prep_bwd",
        grid=(steps, HEADS),
        in_specs=[hb, hb, hb, col, col, col, gcrow, square, hb, hb, hb, hb, square, glb] + [_HBM] * n_sc,
        out_specs=[hb, hb, hb, col, col, col, gcrow] + [_HBM] * n_sc,
        out_shape=[jax.ShapeDtypeStruct((t, wide), F32)] * 3 + [jax.ShapeDtypeStruct((t, LANES), F32)] * 3 + [jax.ShapeDtypeStruct((HEADS, t // CHUNK, 1, CHUNK), F32)]
        + [jax.ShapeDtypeStruct((3,) + p.shape[1:], p.dtype) for p in partials],
        scratch_shapes=[pltpu.SemaphoreType.DMA((3 * n_sc,))] * (2 if n_sc else 0),
        compiler_params=_params("arbitrary", "arbitrary"),
    )(q, k, v, beta, gc, gt, gcr, tmat, dw, du, dqg, dkg, dattn, dgl, *partials)
    return res[:7], res[7:]


def _gdn_scan(w, u, qg, kg, attn, gl):
    t = w.shape[0]
    n = t // CHUNK
    nch = min(SCAN_CHUNKS, n)
    wide = HEADS * HEAD_DIM

    def body(w_ref, u_ref, qg_ref, kg_ref, at_ref, gl_ref, o_ref, st_ref, s_ref):
        @pl.when(pl.program_id(0) == 0)
        def _():
            s_ref[...] = jnp.zeros_like(s_ref)

        heads = range(HEADS)
        cols = [pl.ds(h * HEAD_DIM, HEAD_DIM) for h in heads]
        for c in range(nch):
            rows, gl_rows = pl.ds(c * CHUNK, CHUNK), pl.ds(c * 8, 8)
            ss = [s_ref[h] for h in heads]
            sbs = [s.astype(BF16) for s in ss]
            vbs = [(u_ref[rows, hs] - _dot(w_ref[rows, hs], sb)).astype(BF16) for hs, sb in zip(cols, sbs)]
            outs = [_dot(qg_ref[rows, hs], sb) + _dot(at_ref[h, rows, :], vb) for h, hs, sb, vb in zip(heads, cols, sbs, vbs)]
            new = [s * jnp.tile(gl_ref[h, gl_rows, :], (HEAD_DIM // 8, 1)) + _dot(kg_ref[rows, hs], vb, TN) for h, hs, s, vb in zip(heads, cols, ss, vbs)]
            for h, hs in zip(heads, cols):
                st_ref[c, h] = ss[h]
                o_ref[rows, hs] = outs[h]
                s_ref[h] = new[h]

    row = pl.BlockSpec((nch * CHUNK, wide), lambda i: (i, 0))
    return pl.pallas_call(
        body,
        name="gdn_scan",
        grid=(n // nch,),
        in_specs=[row, row, row, row, pl.BlockSpec((HEADS, nch * CHUNK, CHUNK), lambda i: (0, i, 0)), pl.BlockSpec((HEADS, nch * 8, HEAD_DIM), lambda i: (0, i, 0))],
        out_specs=[row, pl.BlockSpec((nch, HEADS, HEAD_DIM, HEAD_DIM), lambda i: (i, 0, 0, 0))],
        out_shape=[jax.ShapeDtypeStruct((t, wide), F32), jax.ShapeDtypeStruct((n, HEADS, HEAD_DIM, HEAD_DIM), F32)],
        scratch_shapes=[pltpu.VMEM((HEADS, HEAD_DIM, HEAD_DIM), F32)],
        compiler_params=_params("arbitrary"),
    )(w, u, qg, kg, attn, gl)


def _gdn_scan_bwd(w, u, qg, kg, attn, gl, states, do, swaps=()):
    t = w.shape[0]
    n = t // CHUNK
    nch = min(SCAN_CHUNKS, n)
    steps = n // nch
    wide = HEADS * HEAD_DIM
    n_sw = len(swaps)

    def body(*refs):
        w_ref, u_ref, qg_ref, kg_ref, at_ref, gl_ref, st_ref, do_ref = refs[:8]
        g_refs = refs[8:8 + n_sw]
        dw_ref, du_ref, dqg_ref, dkg_ref, dat_ref, dgl_ref = refs[8 + n_sw:14 + n_sw]
        a_refs, ds_ref, sems = refs[14 + n_sw:14 + 2 * n_sw], refs[14 + 2 * n_sw], refs[15 + 2 * n_sw:]

        @pl.when(pl.program_id(0) == 0)
        def _():
            ds_ref[...] = jnp.zeros_like(ds_ref)
            for cp in _swap_copies(g_refs, a_refs, *sems) if n_sw else []:
                cp.start()

        heads = range(HEADS)
        cols = [pl.ds(h * HEAD_DIM, HEAD_DIM) for h in heads]
        for c in reversed(range(nch)):
            rows, gl_rows = pl.ds(c * CHUNK, CHUNK), pl.ds(c * 8, 8)
            ss = [st_ref[c, h] for h in heads]
            sbs = [s.astype(BF16) for s in ss]
            dsns = [ds_ref[h] for h in heads]
            dsbs = [d.astype(BF16) for d in dsns]
            dobs = [do_ref[rows, hs].astype(BF16) for hs in cols]
            vbs = [(u_ref[rows, hs] - _dot(w_ref[rows, hs], sb)).astype(BF16) for hs, sb in zip(cols, sbs)]
            dvns = [_dot(at_ref[h, rows, :], dob, TN) + _dot(kg_ref[rows, hs], dsb) for h, hs, dob, dsb in zip(heads, cols, dobs, dsbs)]
            dvbs = [d.astype(BF16) for d in dvns]
            for h, hs in zip(heads, cols):
                dat_ref[h, rows, :] = _dot(dobs[h], vbs[h], NT)
                dqg_ref[rows, hs] = _dot(dobs[h], sbs[h], NT)
                dkg_ref[rows, hs] = _dot(vbs[h], dsbs[h], NT)
                du_ref[rows, hs] = dvns[h]
                dw_ref[rows, hs] = -_dot(dvbs[h], sbs[h], NT)
                dgl_ref[h, gl_rows, :] = jnp.sum((dsns[h] * ss[h]).reshape(HEAD_DIM // 8, 8, HEAD_DIM), axis=0)
            new = [dsn * jnp.tile(gl_ref[h, gl_rows, :], (HEAD_DIM // 8, 1)) + _dot(qg_ref[rows, hs], dob, TN) - _dot(w_ref[rows, hs], dvb, TN)
                   for h, hs, dsn, dob, dvb in zip(heads, cols, dsns, dobs, dvbs)]
            for h in heads:
                ds_ref[h] = new[h]

        if n_sw:
            @pl.when(pl.program_id(0) == steps - 1)
            def _():
                for cp in _swap_copies(g_refs, a_refs, *sems):
                    cp.wait()

    row = pl.BlockSpec((nch * CHUNK, wide), lambda i: (steps - 1 - i, 0))
    at = pl.BlockSpec((HEADS, nch * CHUNK, CHUNK), lambda i: (0, steps - 1 - i, 0))
    glb = pl.BlockSpec((HEADS, nch * 8, HEAD_DIM), lambda i: (0, steps - 1 - i, 0))
    res = pl.pallas_call(
        body,
        name="gdn_scan_bwd",
        grid=(steps,),
        in_specs=[row, row, row, row, at, glb, pl.BlockSpec((nch, HEADS, HEAD_DIM, HEAD_DIM), lambda i: (steps - 1 - i, 0, 0, 0)), row] + [_HBM] * n_sw,
        out_specs=[row, row, row, row, at, glb] + [_HBM] * n_sw,
        out_shape=[jax.ShapeDtypeStruct((t, wide), F32)] * 4 + [jax.ShapeDtypeStruct((HEADS, t, CHUNK), F32), jax.ShapeDtypeStruct((HEADS, n * 8, HEAD_DIM), F32)]
        + [jax.ShapeDtypeStruct(g.shape[1:], g.dtype) for g in swaps],
        scratch_shapes=[pltpu.VMEM((HEADS, HEAD_DIM, HEAD_DIM), F32)] + [pltpu.SemaphoreType.DMA((n_sw,))] * (2 if n_sw else 0),
        compiler_params=_params("arbitrary"),
    )(w, u, qg, kg, attn, gl, states, do, *swaps)
    return res[:6], res[6:]


SB_Q = 512
SB_K = 256
SB_STEP = 1
SB_DEAD = -105.0


def _sb_scores(q, k):
    z = _dot(q, k, NT) * (HEAD_DIM ** -0.5)
    lb = jnp.minimum(z, 0.0) - jnp.log(1.0 + jnp.exp(-jnp.abs(z)))
    return lb, lb - z


def _tri(n, rel):
    return jnp.where(rel(_iota((n, n), 0), _iota((n, n), 1)), 1.0, 0.0).astype(BF16)


def _lanes(col):
    return jnp.broadcast_to(col, (col.shape[0], LANES))


def _sb_fwd(q, k, v):
    t = q.shape[0]
    bq, bk = min(SB_Q, t), min(SB_K, t)
    nsub, rep = bq // bk, bk // LANES
    nstep = min(SB_STEP, nsub)
    steps_per_tile = nsub // nstep

    def body(q_ref, k_ref, v_ref, o_ref, rt_ref, first_ref):
        h = pl.program_id(0)
        i = pl.program_id(1)
        o_ref[...] = jnp.zeros_like(o_ref)
        rt_ref[...] = jnp.zeros_like(rt_ref)
        after = _tri(bk, lambda r, c: r > c)

        def block(j, r0, diag):
            st = pl.multiple_of(j * bk, bk)
            kv, vv = k_ref[pl.ds(st, bk), :], v_ref[pl.ds(st, bk), :]
            lb, l1m = _sb_scores(q_ref[r0:, :], kv)
            if diag:
                mask = _iota((bq - r0, bk), 1) + j * bk < _iota((bq - r0, bk), 0) + (r0 + i * bq)
                l1m = jnp.where(mask, l1m, 0.0)
            sums = _two_pass(l1m, after)
            run = rt_ref[r0:, :]
            a = jnp.exp(lb + jnp.tile(run, (1, rep)) + sums)
            if diag:
                a = jnp.where(mask, a, 0.0)
            o_ref[r0:, :] += _dot(a.astype(BF16), vv)
            rt_ref[r0:, :] = run + _lanes(sums[:, 0:1] + l1m[:, 0:1])

        for s in reversed(range(nsub)):
            block(i * nsub + s, s * bk, True)

        def alive(carry):
            u, highest = carry
            return jnp.logical_and(u >= 0, highest > SB_DEAD)

        def step(carry):
            u, _ = carry
            for s in reversed(range(nstep)):
                block(u * nstep + s, 0, False)
            return u - 1, jnp.max(rt_ref[...])

        u_end, _ = lax.while_loop(alive, step, (i * steps_per_tile - 1, jnp.max(rt_ref[...])))
        first_ref[h, i] = u_end + 1

    qb = pl.BlockSpec((bq, HEAD_DIM), lambda h, i: (i, h))
    full = pl.BlockSpec((t, HEAD_DIM), lambda h, i: (0, h))
    return pl.pallas_call(
        body,
        name="sb_fwd",
        grid=(HEADS, t // bq),
        in_specs=[qb, full, full],
        out_specs=[qb, qb, pl.BlockSpec(memory_space=pltpu.SMEM)],
        out_shape=[jax.ShapeDtypeStruct(q.shape, F32), jax.ShapeDtypeStruct(q.shape, F32), jax.ShapeDtypeStruct((HEADS, t // bq), jnp.int32)],
        compiler_params=_params("arbitrary", "arbitrary"),
    )(q, k, v)


def _sb_bwd(q, k, v, rt, first, do):
    t = q.shape[0]
    bq, bk = min(SB_Q, t), min(SB_K, t)
    nsub, rep = bq // bk, bk // LANES
    nstep = min(SB_STEP, nsub)
    steps_per_tile = nsub // nstep
    scale = HEAD_DIM ** -0.5

    def body(first_ref, q_ref, k_ref, v_ref, rt_ref, do_ref, dq_ref, dk_ref, dv_ref, left_ref, pg_ref):
        h = pl.program_id(0)
        i = pl.program_id(1)

        @pl.when(i == 0)
        def _():
            dk_ref[...] = jnp.zeros_like(dk_ref)
            dv_ref[...] = jnp.zeros_like(dv_ref)

        dq_ref[...] = jnp.zeros_like(dq_ref)
        left_ref[...] = jnp.zeros_like(left_ref)
        pg_ref[...] = jnp.zeros_like(pg_ref)
        upto = _tri(bk, lambda r, c: r <= c)

        def block(j, r0, diag):
            st = pl.multiple_of(j * bk, bk)
            kv, vv = k_ref[pl.ds(st, bk), :], v_ref[pl.ds(st, bk), :]
            qv = q_ref[r0:, :]
            dob = do_ref[r0:, :].astype(BF16)
            lb, l1m = _sb_scores(qv, kv)
            if diag:
                mask = _iota((bq - r0, bk), 1) + j * bk < _iota((bq - r0, bk), 0) + (r0 + i * bq)
                l1m = jnp.where(mask, l1m, 0.0)
            sums = _two_pass(l1m, upto)
            left = left_ref[r0:, :]
            a = jnp.exp(lb + jnp.tile(rt_ref[r0:, :] - left, (1, rep)) - sums)
            if diag:
                a = jnp.where(mask, a, 0.0)
            g = _dot(dob, vv, NT) * a
            dv_ref[pl.ds(st, bk), :] += _dot(a.astype(BF16), dob, TN)
            gsum = _two_pass(g, upto)
            pg = pg_ref[r0:, :]
            dz = g - jnp.exp(lb) * (jnp.tile(pg, (1, rep)) + gsum)
            if diag:
                dz = jnp.where(mask, dz, 0.0)
            dzb = (dz * scale).astype(BF16)
            dk_ref[pl.ds(st, bk), :] += _dot(dzb, qv, TN)
            dq_ref[r0:, :] += _dot(dzb, kv)
            left_ref[r0:, :] = left + _lanes(sums[:, bk - 1:bk])
            pg_ref[r0:, :] = pg + _lanes(gsum[:, bk - 1:bk])

        def step(u, carry):
            for s in range(nstep):
                block(u * nstep + s, 0, False)
            return carry

        lax.fori_loop(first_ref[h, i], i * steps_per_tile, step, 0)
        for s in range(nsub):
            block(i * nsub + s, s * bk, True)

    qb = pl.BlockSpec((bq, HEAD_DIM), lambda h, i: (i, h))
    full = pl.BlockSpec((t, HEAD_DIM), lambda h, i: (0, h))
    return pl.pallas_call(
        body,
        name="sb_bwd",
        grid=(HEADS, t // bq),
        in_specs=[pl.BlockSpec(memory_space=pltpu.SMEM), qb, full, full, qb, qb],
        out_specs=[qb, full, full],
        out_shape=[jax.ShapeDtypeStruct(q.shape, F32)] * 3,
        scratch_shapes=[pltpu.VMEM((bq, LANES), F32), pltpu.VMEM((bq, LANES), F32)],
        compiler_params=_params("arbitrary", "arbitrary"),
    )(first, q, k, v, rt, do)


def _adamw(w, g, m, v, name, tm=ROW_TILE):
    r, c = w.shape
    tm = tm if r % tm == 0 else r

    def body(w_ref, g_ref, m_ref, v_ref, d_ref, nm_ref, nv_ref):
        gv = g_ref[...]
        nm = ADAM_B1 * m_ref[...] + (1.0 - ADAM_B1) * gv
        nv = ADAM_B2 * v_ref[...] + (1.0 - ADAM_B2) * (gv * gv)
        m_hat = nm / (1.0 - ADAM_B1 ** ADAM_STEP)
        v_hat = nv / (1.0 - ADAM_B2 ** ADAM_STEP)
        d_ref[...] = -ADAM_LR * (m_hat / (jnp.sqrt(v_hat) + ADAM_EPS) + ADAM_WD * w_ref[...])
        nm_ref[...] = nm
        nv_ref[...] = nv

    blk = pl.BlockSpec((tm, c), lambda i: (i, 0))
    return pl.pallas_call(
        body,
        name=name,
        grid=(r // tm,),
        in_specs=[blk] * 4,
        out_specs=[blk] * 3,
        out_shape=[jax.ShapeDtypeStruct((r, c), F32)] * 3,
        compiler_params=_params("parallel"),
    )(w, g, m, v)


def _local_step(x, tgt, gains, small, shards, assemble_first, assemble, early_reduce=None, late_reduce=None, early_finish=None):
    mix_pre, mix_post, mlp_pre, mlp_post, kv_gain = gains
    a_log, dt_bias, out_gain = small
    t, d = x.shape
    row = lambda a, i=None: a[i:i + 1] if i is not None else a
    al = jnp.zeros((1, LANES), F32).at[:, HEADS:2 * HEADS].set(a_log)
    dtb = jnp.zeros((1, LANES), F32).at[:, HEADS:2 * HEADS].set(dt_bias)
    og = jnp.tile(out_gain, (1, HEADS))
    full = lambda a: (a, a.shape[1], 0)

    h0, *gathered_first = _rowwise("norm_in", _fn_norm, [full(x)], [row(mix_pre, 0)], [(d, BF16)], gather=shards[0])
    w_qkvg, w_ba, conv_w = assemble_first(gathered_first)
    qkvg = _matmul(h0, w_qkvg, "nn", F32, "mm_gdn_in", tk=1024)
    ba = _matmul(h0, w_ba, "nn", F32, "mm_gdn_ba", tk=1024)
    (conv, gq, gk, gv), gathered_conv = _conv_fwd(qkvg, conv_w, shards[1])
    beta, gc, gt = _rowwise("gates", _fn_gates, [full(ba)], [al, dtb], [(LANES, F32)] * 3)
    gcr = jnp.swapaxes(gc[:, HEADS:2 * HEADS], 0, 1).reshape(HEADS, t // CHUNK, 1, CHUNK)
    (pw, pu, pqg, pkg, pattn, pgl, ptm), gathered_prep = _gdn_prep(gq, gk, gv, beta, gc, gt, gcr, shards[2])
    w_out, w_kv, w_q, w_o, w_up, w_down = assemble(gathered_conv, gathered_prep)
    w_qkvg_t, w_up_t = jnp.swapaxes(w_qkvg, -1, -2), jnp.swapaxes(w_up, -1, -2)
    o_gdn, states = _gdn_scan(pw, pu, pqg, pkg, pattn, pgl)
    (on,) = _rowwise("out_norm", _fn_outnorm, [full(o_gdn), (qkvg, d, 3)], [og], [(d, BF16)])
    mix0 = _matmul(on, w_out, "nn", F32, "mm_gdn_out", tk=1024)
    x1, h1 = _rowwise("res_a0", _fn_res_norm, [full(x), full(mix0)], [row(mix_post, 0), row(mlp_pre, 0)], [(d, F32), (d, BF16)])
    (a0,) = _matmul(h1, w_up[0], "nn", (BF16,), "mm_up0", tk=1024, epilogue=_relu2_of)
    d0 = _matmul(a0, w_down[0], "nn", F32, "mm_down0")
    x2, hkv, hq = _rowwise("res_b0", _fn_res_norm2, [full(x1), full(d0)], [row(mlp_post, 0), kv_gain, row(mix_pre, 1)], [(d, F32), (d, BF16), (d, BF16)])
    w_k, w_v = w_kv[:, :d], w_kv[:, d:]
    kp = _matmul(hkv, w_k, "nn", BF16, "mm_k", tk=1024)
    vp = _matmul(hkv, w_v, "nn", BF16, "mm_v", tk=1024)
    qp = _matmul(hq, w_q, "nn", BF16, "mm_q", tk=1024)
    o_sb, rt, sb_first = _sb_fwd(qp, kp, vp)
    mix1 = _matmul(o_sb, w_o, "nn", F32, "mm_sb_out", tk=1024)
    x3, h3 = _rowwise("res_a1", _fn_res_norm, [full(x2), full(mix1)], [row(mix_post, 1), row(mlp_pre, 1)], [(d, F32), (d, BF16)])
    (a1,) = _matmul(h3, w_up[1], "nn", (BF16,), "mm_up1", tk=1024, epilogue=_relu2_of)
    d1 = _matmul(a1, w_down[1], "nn", F32, "mm_down1")

    loss, dx3, dd1, g_mlp_post1 = _loss_call(x3, d1, tgt, row(mlp_post, 1))
    (du1,) = _matmul(dd1, w_down[1], "nt", (BF16,), "mm_down1_dx", epilogue=_relu2_cotangent, extras=[a1])
    up_shape, down_shape = _grad_buffer_shape(_GROUPS[0]), _grad_buffer_shape(_GROUPS[1])
    buf_down = _matmul(a1, dd1, "tn", F32, "mm_down1_dw", into=(down_shape, None, lambda i, j: (1, i)))
    dh3 = _matmul(du1, w_up_t[1], "nn", F32, "mm_up1_dx")
    buf_up = _matmul(h3, du1, "tn", F32, "mm_up1_dw", into=(up_shape, None, lambda i, j: (1, j)))
    (dx2, dmix1), (g_mix_post1, g_mlp_pre1), _ = _rowwise_bwd(
        "res_a1_bwd", _fn_res_norm, [full(x2), full(mix1)], [row(mix_post, 1), row(mlp_pre, 1)], [dx3, dh3], [F32, BF16])
    do_sb = _matmul(dmix1, w_o, "nt", BF16, "mm_sb_out_dx")
    g_o = _matmul(o_sb, dmix1, "tn", F32, "mm_sb_out_dw")
    dqp, dkp, dvp = _sb_bwd(qp, kp, vp, rt, sb_first, do_sb)
    dhq = _matmul(dqp, w_q, "nt", F32, "mm_q_dx")
    g_q = _matmul(hq, dqp, "tn", F32, "mm_q_dw")
    dhkv = _matmul(dvp, w_v, "nt", F32, "mm_v_dx", add=_matmul(dkp, w_k, "nt", F32, "mm_k_dx"))
    g_kv = jnp.concatenate([_matmul(hkv, dkp, "tn", F32, "mm_k_dw"), _matmul(hkv, dvp, "tn", F32, "mm_v_dw")], axis=1)
    (dx1, dd0), (g_mlp_post0, g_kv_gain, g_mix_pre1), _ = _rowwise_bwd(
        "res_b0_bwd", _fn_res_norm2, [full(x1), full(d0)], [row(mlp_post, 0), kv_gain, row(mix_pre, 1)], [dx2, dhkv, dhq], [F32, BF16])
    (du0,) = _matmul(dd0, w_down[0], "nt", (BF16,), "mm_down0_dx", epilogue=_relu2_cotangent, extras=[a0])
    buf_down = _matmul(a0, dd0, "tn", F32, "mm_down0_dw", into=(down_shape, buf_down, lambda i, j: (0, i)))
    dh1 = _matmul(du0, w_up_t[0], "nn", F32, "mm_up0_dx")
    buf_up = _matmul(h1, du0, "tn", F32, "mm_up0_dw", into=(up_shape, buf_up, lambda i, j: (0, j)))
    (dx0, dmix0), (g_mix_post0, g_mlp_pre0), _ = _rowwise_bwd(
        "res_a0_bwd", _fn_res_norm, [full(x), full(mix0)], [row(mix_post, 0), row(mlp_pre, 0)], [dx1, dh1], [F32, BF16])
    don = _matmul(dmix0, w_out, "nt", F32, "mm_gdn_out_dx")
    g_out = _matmul(on, dmix0, "tn", F32, "mm_gdn_out_dw")
    (do_gdn, dgate), (g_og,), _ = _rowwise_bwd("out_norm_bwd", _fn_outnorm, [full(o_gdn), (qkvg, d, 3)], [og], [don], [F32, F32])
    partial, partial_bf16, packed = [], (), ()
    if early_reduce is not None:
        pack, add_sibling = early_reduce
        packed = pack(dict(gdn_w_out=g_out[None], w_kv=g_kv, sb_w_q=g_q[None], sb_w_o=g_o[None]), {0: buf_up, 1: buf_down})
    (dpw, dpu, dpqg, dpkg, dpattn, dpgl), from_sibling = _gdn_scan_bwd(pw, pu, pqg, pkg, pattn, pgl, states, do_gdn, packed)
    if early_reduce is not None:
        partial, partial_bf16 = add_sibling(packed, from_sibling)
    (dgq, dgk, dgv, dbeta, dgc, dgt, dgcr), from_chips = _gdn_prep_bwd(gq, gk, gv, beta, gc, gt, gcr, ptm, dpw, dpu, dpqg, dpkg, dpattn, dpgl, partial_bf16)
    dgcr_lanes = jnp.pad(jnp.swapaxes(dgcr.reshape(HEADS, t), 0, 1), ((0, 0), (HEADS, LANES - 2 * HEADS)))
    gate_cots = [dbeta, dgc + dgcr_lanes, dgt]
    (dba,), (g_al, g_dtb), _ = _rowwise_bwd("gates_bwd", _fn_gates, [full(ba)], [al, dtb], gate_cots, [BF16])
    dqkvg, g_conv = _conv_bwd(conv, (dgq, dgk, dgv), dgate, qkvg, conv_w)
    dh0b = _matmul(dba, w_ba, "nt", F32, "mm_gdn_ba_dx", tk=LANES)
    dh0 = _matmul(dqkvg, w_qkvg_t, "nn", F32, "mm_gdn_in_dx", add=dh0b)
    g_qkvg = _matmul(h0, dqkvg, "tn", F32, "mm_gdn_in_dw")
    g_ba = _matmul(h0, dba, "tn", F32, "mm_gdn_ba_dw")
    g_w_in = jnp.concatenate([g_qkvg, g_ba[:, :2 * HEADS]], axis=1)[None]
    partial_late, partial_late_bf16 = late_reduce(dict(gdn_w_in=g_w_in)) if late_reduce is not None else ([], ())
    reduced_early = tuple(early_finish(partial, from_chips)) if early_finish is not None else ()
    (grad_x,), (g_mix_pre0,), arrived = _rowwise_bwd(
        "norm_in_bwd", lambda xx, gg: (_rms(xx, gg), xx), [full(x)], [row(mix_pre, 0)], [dh0, dx0], [F32], partials=partial_late_bf16, shares=reduced_early)
    from_chips_late, shared_early = arrived[:len(partial_late_bf16)], arrived[len(partial_late_bf16):]

    grads = dict(
        mix_pre_gain=jnp.concatenate([g_mix_pre0, g_mix_pre1], axis=0),
        mix_post_gain=jnp.concatenate([g_mix_post0, g_mix_post1], axis=0),
        mlp_pre_gain=jnp.concatenate([g_mlp_pre0, g_mlp_pre1], axis=0),
        mlp_post_gain=jnp.concatenate([g_mlp_post0, g_mlp_post1], axis=0),
        mlp_w_up=jnp.stack([_join(buf_up[layer, :, :d], "cols") for layer in range(2)]),
        mlp_w_down=jnp.stack([_join(buf_down[layer, :, :d], "rows") for layer in range(2)]),
        gdn_w_in=g_w_in,
        gdn_conv_w=g_conv[None, :CONV_K],
        gdn_a_log=g_al[:, HEADS:2 * HEADS],
        gdn_dt_bias=g_dtb[:, HEADS:2 * HEADS],
        gdn_out_gain=jnp.sum(g_og.reshape(HEADS, HEAD_DIM), axis=0, keepdims=True),
        gdn_w_out=g_out[None],
        kv_gain=g_kv_gain[0],
        w_kv=g_kv,
        sb_w_q=g_q[None],
        sb_w_o=g_o[None],
    )
    if early_finish is not None:
        return loss, grad_x, grads, (list(partial_late), list(from_chips_late), reduced_early, tuple(shared_early))
    return loss, grad_x, grads, (list(partial) + list(partial_late), list(from_chips) + list(from_chips_late))


N_DEV = 8
N_CHIPS = 4
PACK_ROW_TILE = 128

_HBM = pl.BlockSpec(memory_space=pltpu.HBM)


def _place():
    return lax.axis_index("x"), lax.axis_index("y"), lax.axis_index("c")


def _other_chips(x, y):
    return [(1 - x, y), (x, 1 - y), (1 - x, 1 - y)]


def _remote(src, dst, send_sem, recv_sem, to):
    return pltpu.make_async_remote_copy(src_ref=src, dst_ref=dst, send_sem=send_sem, recv_sem=recv_sem, device_id=to, device_id_type=MESH)


def _gather8(v, name):
    rows, cols = v.shape

    def body(v_ref, out_ref, sum_ref, send_sems, recv_sems, local_sem):
        x, y, c = _place()
        me, sibling = (x, y, c), (x, y, 1 - c)
        chips = _other_chips(x, y)

        def blk(px, py, pc):
            return out_ref.at[pl.ds((4 * px + 2 * py + pc) * rows, rows), :]

        def copy(k, block, to, src=None):
            return _remote(blk(*block) if src is None else src, blk(*block), send_sems.at[k], recv_sems.at[k], to)

        mine = pltpu.make_async_copy(v_ref, blk(*me), local_sem)
        mine.start()
        first = [copy(0, me, sibling, src=v_ref)] + [copy(1 + j, me, (*chip, c), src=v_ref) for j, chip in enumerate(chips)]
        for cp in first:
            cp.start()
        passed = [copy(4 + j, (*chip, c), sibling) for j, chip in enumerate(chips)]
        for j, chip in enumerate(chips):
            copy(1 + j, (*chip, c), me).wait_recv()
            passed[j].start()
        copy(0, sibling, me).wait_recv()
        for j, chip in enumerate(chips):
            copy(4 + j, (*chip, 1 - c), me).wait_recv()
        for cp in first + passed:
            cp.wait_send()
        mine.wait()
        acc = out_ref[pl.ds(0, rows), :]
        for dev in range(1, N_DEV):
            acc = acc + out_ref[pl.ds(dev * rows, rows), :]
        sum_ref[...] = acc

    vm = pl.BlockSpec(memory_space=pltpu.VMEM)
    return pl.pallas_call(
        body,
        name=name,
        out_shape=[jax.ShapeDtypeStruct((N_DEV * rows, cols), v.dtype), jax.ShapeDtypeStruct((rows, cols), v.dtype)],
        in_specs=[vm],
        out_specs=[vm, vm],
        scratch_shapes=[pltpu.SemaphoreType.DMA((7,)), pltpu.SemaphoreType.DMA((7,)), pltpu.SemaphoreType.DMA],
    )(v)


def _hbm_call(body, name, arrs, out_shapes, sem_counts):
    n = len(arrs)

    def wrapped(*refs):
        body(refs[:n], refs[n:2 * n], *refs[2 * n:])

    return pl.pallas_call(
        wrapped,
        name=name,
        out_shape=[jax.ShapeDtypeStruct(s, a.dtype) for s, a in zip(out_shapes, arrs)],
        in_specs=[_HBM] * n,
        out_specs=[_HBM] * n,
        scratch_shapes=[pltpu.SemaphoreType.DMA((k,)) for k in sem_counts],
    )(*arrs)


def _gather_sends(w_refs, out_refs, send_sems, recv_sems):
    x, y, c = _place()
    s_me = 2 * x + y
    return [_remote(w.at[c], o.at[s_me, c], send_sems.at[3 * a + j], recv_sems.at[3 * a + j], (px, py, c))
            for a, (w, o) in enumerate(zip(w_refs, out_refs)) for j, (px, py) in enumerate(_other_chips(x, y))]


def _gather_finish(w_refs, out_refs, send_sems, recv_sems, fsend_sems, frecv_sems):
    x, y, c = _place()
    chips = _other_chips(x, y)
    passed = []
    for a, o in enumerate(out_refs):
        for j, (px, py) in enumerate(chips):
            half = o.at[2 * px + py, c]
            _remote(half, half, send_sems.at[3 * a + j], recv_sems.at[3 * a + j], (px, py, c)).wait_recv()
            fwd = _remote(half, half, fsend_sems.at[3 * a + j], frecv_sems.at[3 * a + j], (x, y, 1 - c))
            fwd.start()
            passed.append(fwd)
    for a, o in enumerate(out_refs):
        for j, (px, py) in enumerate(chips):
            half = o.at[2 * px + py, 1 - c]
            _remote(half, half, fsend_sems.at[3 * a + j], frecv_sems.at[3 * a + j], (x, y, 1 - c)).wait_recv()
    for cp in _gather_sends(w_refs, out_refs, send_sems, recv_sems) + passed:
        cp.wait_send()


def _swap_copies(g_refs, a_refs, send_sems, recv_sems):
    x, y, c = _place()
    return [_remote(g.at[1 - c], a, send_sems.at[i], recv_sems.at[i], (x, y, 1 - c)) for i, (g, a) in enumerate(zip(g_refs, a_refs))]


def _swap_halves(arrs, name):
    n = len(arrs)

    def body(g_refs, a_refs, send_sems, recv_sems):
        cps = _swap_copies(g_refs, a_refs, send_sems, recv_sems)
        for cp in cps:
            cp.start()
        for cp in cps:
            cp.wait()

    return _hbm_call(body, name, arrs, [a.shape[1:] for a in arrs], [n, n])


def _scatter_copies(p_refs, b_refs, send_sems, recv_sems):
    x, y, c = _place()
    return [_remote(p.at[2 * px + py], b.at[j], send_sems.at[3 * i + j], recv_sems.at[3 * i + j], (px, py, c))
            for i, (p, b) in enumerate(zip(p_refs, b_refs)) for j, (px, py) in enumerate(_other_chips(x, y))]


def _share_copies(q_refs, out_refs, send_sems, recv_sems):
    x, y, c = _place()
    return [_remote(q, o, send_sems.at[i], recv_sems.at[i], (x, y, 1 - c)) for i, (q, o) in enumerate(zip(q_refs, out_refs))]


def _share_halves(arrs):
    n = len(arrs)

    def body(q_refs, out_refs, send_sems, recv_sems):
        cps = _share_copies(q_refs, out_refs, send_sems, recv_sems)
        for cp in cps:
            cp.start()
        for cp in cps:
            cp.wait()

    return _hbm_call(body, "grads_share", arrs, [a.shape for a in arrs], [n, n])


_GROUPS = (
    (("mlp_w_up", (2, 1024, 1024), "cols"), ("gdn_w_out", (1, 256, 1024), "rows")),
    (("mlp_w_down", (2, 1024, 1024), "rows"), ("sb_w_q", (1, 256, 1024), "rows"), ("sb_w_o", (1, 256, 1024), "rows")),
    (("w_kv", (1024, 512), "cols"),),
    (("gdn_w_in", (1, 1024, 1028), "cols"),),
)
_BEHIND_CONV, _BEHIND_PREP, _FIRST = slice(0, 1), slice(1, 3), slice(3, 4)
_EARLY_GRADS = slice(0, 3)


def _numel(shape):
    n = 1
    for s in shape:
        n *= s
    return n


def _half_rows(shape):
    return _numel(shape[:-1]) // 2


def _pack_shards(shards, dtype):
    return tuple(jnp.concatenate([shards[n].astype(dtype).reshape(2, _half_rows(shape), shape[-1]) for n, shape, _ in grp], axis=1) for grp in _GROUPS)


def _unpack_shards(bufs):
    out = {}
    for grp, buf in zip(_GROUPS, bufs):
        off = 0
        for n, shape, _ in grp:
            out[n] = buf[:, off:off + _half_rows(shape)].reshape(shape)
            off += _half_rows(shape)
    return out


def _join(stacked, how):
    nd = stacked.ndim - 1
    ax = nd - 1 if how == "cols" else nd - 2
    moved = jnp.moveaxis(stacked, 0, ax)
    shape = list(stacked.shape[1:])
    shape[ax] *= N_CHIPS
    return moved.reshape(shape)


def _split(full, shard_shape, how):
    nd = len(shard_shape)
    ax = nd - 1 if how == "cols" else nd - 2
    shape = list(shard_shape)
    shape.insert(ax, N_CHIPS)
    return jnp.moveaxis(full.reshape(shape), ax, 0)


def _unpack_full(gathered, groups):
    out = {}
    for grp, buf in zip(groups, gathered):
        off = 0
        for n, shape, how in grp:
            out[n] = _join(buf[:, :, off:off + _half_rows(shape)].reshape((N_CHIPS,) + shape), how)
            off += _half_rows(shape)
    return out


def _grad_buffer_shape(grp):
    return (2, N_CHIPS, sum(_half_rows(shape) for _, shape, _ in grp), grp[0][1][-1])


def _pack_full(full, groups, started=None):
    bufs = []
    for gi, grp in enumerate(groups):
        def halves(n, shape, how):
            return jnp.swapaxes(_split(full[n], shape, how).reshape(N_CHIPS, 2, _half_rows(shape), shape[-1]), 0, 1)

        if started is not None and gi in started:
            buf, off = started[gi], _half_rows(grp[0][1])
            for n, shape, how in grp[1:]:
                buf = buf.at[:, :, off:off + _half_rows(shape), :].set(halves(n, shape, how))
                off += _half_rows(shape)
        else:
            buf = jnp.concatenate([halves(n, shape, how) for n, shape, how in grp], axis=2)
        bufs.append(buf.reshape(2, -1, buf.shape[-1]))
    return tuple(bufs)


_SMALL = (
    ("mix_pre_gain", (2, 1024)),
    ("mix_post_gain", (2, 1024)),
    ("mlp_pre_gain", (2, 1024)),
    ("mlp_post_gain", (2, 1024)),
    ("kv_gain", (1024,)),
    ("gdn_out_gain", (1, 128)),
    ("gdn_a_log", (1, 8)),
    ("gdn_dt_bias", (1, 8)),
    ("gdn_conv_w", (1, 4, 3072)),
    ("loss", ()),
)


def _rows_of(shape):
    return -(-_numel(shape) // LANES)


def _pack_rows(vals, layout):
    parts = []
    for n, shape in layout:
        flat = vals[n].reshape(-1)
        parts.append(jnp.pad(flat, (0, _rows_of(shape) * LANES - flat.shape[0])))
    flat = jnp.concatenate(parts)
    rows = -(-flat.shape[0] // (8 * LANES)) * 8
    return jnp.pad(flat, (0, rows * LANES - flat.shape[0])).reshape(rows, LANES)


def _unpack_rows(packed, layout):
    flat = packed.reshape(-1)
    out, off = {}, 0
    for n, shape in layout:
        out[n] = flat[off:off + _numel(shape)].reshape(shape)
        off += _rows_of(shape) * LANES
    return out


_WEIGHTS = ("mix_pre_gain", "mix_post_gain", "mlp_pre_gain", "mlp_post_gain", "mlp_w_up", "mlp_w_down", "gdn_w_in", "gdn_conv_w",
            "gdn_a_log", "gdn_dt_bias", "gdn_out_gain", "gdn_w_out", "kv_gain", "w_kv", "sb_w_q", "sb_w_o")


def _as2d(a):
    return a.reshape(1, -1) if a.ndim <= 1 else a.reshape(-1, a.shape[-1])


def kernel(x, mix_pre_gain, mix_post_gain, mlp_pre_gain, mlp_post_gain, mlp_w_up, mlp_w_down, gdn_w_in, gdn_conv_w, gdn_a_log, gdn_dt_bias, gdn_out_gain, gdn_w_out, kv_gain, w_kv, sb_w_q, sb_w_o, loss_target, m_mix_pre_gain, m_mix_post_gain, m_mlp_pre_gain, m_mlp_post_gain, m_mlp_w_up, m_mlp_w_down, m_gdn_w_in, m_gdn_conv_w, m_gdn_a_log, m_gdn_dt_bias, m_gdn_out_gain, m_gdn_w_out, m_kv_gain, m_w_kv, m_sb_w_q, m_sb_w_o, v_mix_pre_gain, v_mix_post_gain, v_mlp_pre_gain, v_mlp_post_gain, v_mlp_w_up, v_mlp_w_down, v_gdn_w_in, v_gdn_conv_w, v_gdn_a_log, v_gdn_dt_bias, v_gdn_out_gain, v_gdn_w_out, v_kv_gain, v_w_kv, v_sb_w_q, v_sb_w_o):
    w = dict(mix_pre_gain=mix_pre_gain, mix_post_gain=mix_post_gain, mlp_pre_gain=mlp_pre_gain, mlp_post_gain=mlp_post_gain, mlp_w_up=mlp_w_up, mlp_w_down=mlp_w_down, gdn_w_in=gdn_w_in, gdn_conv_w=gdn_conv_w, gdn_a_log=gdn_a_log, gdn_dt_bias=gdn_dt_bias, gdn_out_gain=gdn_out_gain, gdn_w_out=gdn_w_out, kv_gain=kv_gain, w_kv=w_kv, sb_w_q=sb_w_q, sb_w_o=sb_w_o)
    m = dict(mix_pre_gain=m_mix_pre_gain, mix_post_gain=m_mix_post_gain, mlp_pre_gain=m_mlp_pre_gain, mlp_post_gain=m_mlp_post_gain, mlp_w_up=m_mlp_w_up, mlp_w_down=m_mlp_w_down, gdn_w_in=m_gdn_w_in, gdn_conv_w=m_gdn_conv_w, gdn_a_log=m_gdn_a_log, gdn_dt_bias=m_gdn_dt_bias, gdn_out_gain=m_gdn_out_gain, gdn_w_out=m_gdn_w_out, kv_gain=m_kv_gain, w_kv=m_w_kv, sb_w_q=m_sb_w_q, sb_w_o=m_sb_w_o)
    v = dict(mix_pre_gain=v_mix_pre_gain, mix_post_gain=v_mix_post_gain, mlp_pre_gain=v_mlp_pre_gain, mlp_post_gain=v_mlp_post_gain, mlp_w_up=v_mlp_w_up, mlp_w_down=v_mlp_w_down, gdn_w_in=v_gdn_w_in, gdn_conv_w=v_gdn_conv_w, gdn_a_log=v_gdn_a_log, gdn_dt_bias=v_gdn_dt_bias, gdn_out_gain=v_gdn_out_gain, gdn_w_out=v_gdn_w_out, kv_gain=v_kv_gain, w_kv=v_w_kv, sb_w_q=v_sb_w_q, sb_w_o=v_sb_w_o)
    cx, cy, cc = _place()
    chip = 2 * cx + cy
    conv_cols = gdn_conv_w.shape[-1]

    own = _pack_shards(w, BF16)
    own_taps = jnp.pad(gdn_conv_w[0], ((0, CONV_K), (0, 0))).reshape(2, CONV_K, conv_cols)
    with_own = lambda gathered, mine: [lax.dynamic_update_index_in_dim(g, m, chip, 0) for g, m in zip(gathered, mine)]

    def assemble_first(gathered):
        w_in_all, taps_all = with_own(gathered, (*own[_FIRST], own_taps))
        w_in = _unpack_full([w_in_all], _GROUPS[_FIRST])["gdn_w_in"][0]
        taps = jnp.swapaxes(taps_all[:, 0], 0, 1).reshape(CONV_K, N_CHIPS * conv_cols)
        return w_in[:, :4 * HEADS * HEAD_DIM], jnp.pad(w_in[:, 4 * HEADS * HEAD_DIM:], ((0, 0), (0, LANES - 2 * HEADS))), taps

    def assemble(gathered_conv, gathered_prep):
        full = {**_unpack_full(with_own(gathered_conv, own[_BEHIND_CONV]), _GROUPS[_BEHIND_CONV]),
                **_unpack_full(with_own(gathered_prep, own[_BEHIND_PREP]), _GROUPS[_BEHIND_PREP])}
        return full["gdn_w_out"][0], full["w_kv"], full["sb_w_q"][0], full["sb_w_o"][0], full["mlp_w_up"], full["mlp_w_down"]

    gains = (mix_pre_gain, mix_post_gain, mlp_pre_gain, mlp_post_gain, kv_gain[None])
    small = (gdn_a_log, gdn_dt_bias, gdn_out_gain)
    tile = PACK_ROW_TILE

    def add_sibling(bufs, others, tag):
        p32, p16 = [], []
        for i, (buf, other) in enumerate(zip(bufs, others)):
            _, n, cols = buf.shape
            p, pb = _add_rows(f"grads_add_sibling_{tag}{i}", [(buf.reshape(2 * n, cols), cc * (n // tile)), (other, 0)], n, (F32, BF16), tile)
            p32.append(p.reshape(N_CHIPS, -1, cols))
            p16.append(pb.reshape(N_CHIPS, -1, cols))
        return p32, tuple(p16)

    def to_chip_partials(grads_full, groups, tag):
        bufs = _pack_full(grads_full, groups)
        return add_sibling(bufs, _swap_halves(bufs, f"grads_to_sibling_{tag}"), tag)

    def add_chips(partials, from_chips, first):
        reduced = []
        for i, (p, others) in enumerate(zip(partials, from_chips), first):
            _, r, cols = p.shape
            terms = [(p.reshape(N_CHIPS * r, cols), chip * (r // tile))] + [(others.reshape(3 * r, cols), j * (r // tile)) for j in range(3)]
            reduced.append(_add_rows(f"grads_add_chips_{i}", terms, r, (F32,), tile)[0])
        return tuple(reduced)

    loss_rows, grad_x, g_full, (partial, from_chips, reduced_early, shared_early) = _local_step(
        x[0], loss_target[0], gains, small, ((*own[_FIRST], own_taps), own[_BEHIND_CONV], own[_BEHIND_PREP]), assemble_first, assemble,
        (lambda g, started: _pack_full(g, _GROUPS[_EARLY_GRADS], started), lambda bufs, others: add_sibling(bufs, others, "early")),
        lambda g: to_chip_partials(g, _GROUPS[_FIRST], "late"), lambda p, f: add_chips(p, f, 0))

    reduced_late = add_chips(partial, from_chips, len(reduced_early))
    reduced, shared = reduced_early + reduced_late, shared_early + tuple(_share_halves(reduced_late))
    g_shard = _unpack_shards([jnp.where(cc == 0, jnp.stack([r, o]), jnp.stack([o, r])) for r, o in zip(reduced, shared)])

    g_small_local = {n: g_full[n] for n, _ in _SMALL if n != "loss"}
    g_small_local["loss"] = loss_rows[0, 0]
    _, small_sum = _gather8(_pack_rows(g_small_local, _SMALL), "allreduce_small")
    g_small = _unpack_rows(small_sum, _SMALL)
    loss = g_small.pop("loss")
    g_small["gdn_conv_w"] = lax.dynamic_slice_in_dim(g_small["gdn_conv_w"], chip * conv_cols, conv_cols, axis=2)

    grads = {**g_shard, **g_small}
    deltas, new_m, new_v = {}, {}, {}
    for n in _WEIGHTS:
        d2, m2, v2 = _adamw(_as2d(w[n]), _as2d(grads[n]), _as2d(m[n]), _as2d(v[n]), "adamw_" + n)
        deltas[n], new_m[n], new_v[n] = d2.reshape(w[n].shape), m2.reshape(w[n].shape), v2.reshape(w[n].shape)
    return (loss, grad_x[None], *[grads[n].reshape(w[n].shape) for n in _WEIGHTS], *[deltas[n] for n in _WEIGHTS],
            *[new_m[n] for n in _WEIGHTS], *[new_v[n] for n in _WEIGHTS])
```
